```python
import jax, jax.numpy as jnp
from jax import lax
import numpy as np

D_MODEL = 2048
BATCH = 8
SEQ = 4096
DEPTH = 1

CHUNK = 64
MIX_WIDTH = D_MODEL
HG_HEADS = 8
HG_HEAD_DIM = 128
HG_WIDTH = HG_HEADS * HG_HEAD_DIM
ATT_HEADS = 8
ATT_HEAD_DIM = 128
ATT_WIDTH = ATT_HEADS * ATT_HEAD_DIM
LEFT_CHUNKS = 8
BAND = (LEFT_CHUNKS + 1) * CHUNK
REL_CLIP = 128
D_FF = 5632
CONV_WIDTH = 3
PLE_DIM = 256
EPS = 1e-6
IN_SPLITS = (HG_WIDTH, HG_WIDTH, HG_WIDTH, HG_WIDTH, ATT_WIDTH, ATT_WIDTH, ATT_WIDTH)
IN_COLS = sum(IN_SPLITS)

kernel_name = "hybrid_hgrn2_chunkattn_convffn_ple"


def rmsnorm(x, g):
    xf = x.astype(jnp.float32)
    y = xf * lax.rsqrt(jnp.mean(xf * xf, axis=-1, keepdims=True) + EPS)
    return (y * g.astype(jnp.float32)).astype(x.dtype)


def hgrn2_mixer(q, f_pre, i_in, g, lb, norm_g):
    B, S, _ = q.shape
    nc = S // CHUNK
    lb = lb.astype(jnp.float32)
    sig = jax.nn.sigmoid(f_pre.astype(jnp.float32))
    f = lb + (1.0 - lb) * sig
    log_f = jnp.log(f)
    k = (1.0 - lb) * jax.nn.sigmoid(-f_pre.astype(jnp.float32))
    qf = jax.nn.silu(q.astype(jnp.float32))
    vf = i_in.astype(jnp.float32)

    def heads(t):
        return t.reshape(B, nc, CHUNK, HG_HEADS, HG_HEAD_DIM).transpose(1, 0, 3, 2, 4)

    causal = jnp.tril(jnp.ones((CHUNK, CHUNK), dtype=bool))

    def step(state, inp):
        qc, kc, vc, lfc = inp
        b = jnp.cumsum(lfc, axis=2)
        diff = b[:, :, :, None, :] - b[:, :, None, :, :]
        decay = jnp.exp(jnp.where(causal[None, None, :, :, None], diff, -jnp.inf))
        scores = jnp.einsum('bhtd,bhsd,bhtsd->bhts', qc, kc, decay)
        o_intra = jnp.einsum('bhts,bhsv->bhtv', scores, vc)
        o_inter = jnp.einsum('bhtd,bhdv->bhtv', qc * jnp.exp(b), state)
        b_last = b[:, :, -1:, :]
        new_state = (jnp.exp(b_last[:, :, 0, :])[..., None] * state
                     + jnp.einsum('bhsd,bhsv->bhdv', kc * jnp.exp(b_last - b), vc))
        return new_state, o_intra + o_inter

    s0 = jnp.zeros((B, HG_HEADS, HG_HEAD_DIM, HG_HEAD_DIM), jnp.float32)
    _, o = lax.scan(step, s0, (heads(qf), heads(k), heads(vf), heads(log_f)))
    o = o.transpose(1, 0, 3, 2, 4).reshape(B, S, HG_HEADS, HG_HEAD_DIM)
    o = rmsnorm(o, norm_g)
    gate = jax.nn.silu(g.astype(jnp.float32)).reshape(B, S, HG_HEADS, HG_HEAD_DIM)
    return (o * gate).reshape(B, S, HG_WIDTH).astype(q.dtype)


def chunk_attention(q, k, v, rel_bias):
    B, S, _ = q.shape
    nc = S // CHUNK
    pad = LEFT_CHUNKS * CHUNK

    def heads(t):
        return t.reshape(B, S, ATT_HEADS, ATT_HEAD_DIM).transpose(0, 2, 1, 3)

    qh = heads(q) * (ATT_HEAD_DIM ** -0.5)
    kp = jnp.pad(heads(k), ((0, 0), (0, 0), (pad, 0), (0, 0)))
    vp = jnp.pad(heads(v), ((0, 0), (0, 0), (pad, 0), (0, 0)))
    t_off = jnp.arange(CHUNK)[:, None]
    j_off = jnp.arange(BAND)[None, :]
    rel = t_off + pad - j_off
    bias = rel_bias.astype(jnp.float32)[:, jnp.clip(rel, -REL_CLIP, REL_CLIP) + REL_CLIP]
    band_idx = jnp.arange(BAND)

    def one_chunk(c):
        start = c * CHUNK
        qc = lax.dynamic_slice_in_dim(qh, start, CHUNK, axis=2)
        kc = lax.dynamic_slice_in_dim(kp, start, BAND, axis=2)
        vc = lax.dynamic_slice_in_dim(vp, start, BAND, axis=2)
        s = jnp.einsum('bhtd,bhsd->bhts', qc, kc).astype(jnp.float32) + bias[None]
        valid = (start - pad + band_idx) >= 0
        s = jnp.where(valid[None, None, None, :], s, jnp.finfo(jnp.float32).min)
        pr = jax.nn.softmax(s, axis=-1)
        return jnp.einsum('bhts,bhsd->bhtd', pr.astype(vc.dtype), vc)

    out = lax.map(one_chunk, jnp.arange(nc))
    return out.transpose(1, 0, 3, 2, 4).reshape(B, S, ATT_WIDTH)


def conv_ffn(h, w_up, conv_w, conv_b, w_down):
    S = h.shape[1]
    u = h @ w_up
    up = jnp.pad(u, ((0, 0), (CONV_WIDTH - 1, 0), (0, 0)))
    uc = conv_b + sum(conv_w[j] * up[:, j:j + S] for j in range(CONV_WIDTH))
    gate, val = uc[..., :D_FF], uc[..., D_FF:]
    return (jax.nn.silu(gate) * val) @ w_down


def _fwd_setup_inputs(seed: int = 0) -> dict:
    key = jax.random.key(seed)
    ks = jax.random.split(key, 20)
    f32 = jnp.float32
    nrm = lambda k, shape, s: jax.random.normal(k, shape, f32) * s
    return {
        "x": nrm(ks[0], (BATCH, SEQ, D_MODEL), 1.0),
        "p": nrm(ks[1], (DEPTH, BATCH, SEQ, PLE_DIM), 1.0),
        "norm_mix": 1.0 + nrm(ks[2], (DEPTH, D_MODEL), 0.02),
        "w_in": nrm(ks[3], (DEPTH, D_MODEL, IN_COLS), D_MODEL ** -0.5),
        "lb_logits": nrm(ks[4], (DEPTH + 1, HG_WIDTH), 0.5),
        "hg_norm": 1.0 + nrm(ks[5], (DEPTH, HG_HEAD_DIM), 0.02),
        "rel_bias": nrm(ks[6], (DEPTH, ATT_HEADS, 2 * REL_CLIP + 1), 0.5),
        "w_out": nrm(ks[7], (DEPTH, MIX_WIDTH, D_MODEL), MIX_WIDTH ** -0.5),
        "norm_ffn": 1.0 + nrm(ks[8], (DEPTH, D_MODEL), 0.02),
        "w_up": nrm(ks[9], (DEPTH, D_MODEL, 2 * D_FF), D_MODEL ** -0.5),
        "conv_w": nrm(ks[10], (DEPTH, CONV_WIDTH, 2 * D_FF), CONV_WIDTH ** -0.5),
        "conv_b": nrm(ks[11], (DEPTH, 2 * D_FF), 0.02),
        "w_down": nrm(ks[12], (DEPTH, D_FF, D_MODEL), D_FF ** -0.5),
        "norm_ple": 1.0 + nrm(ks[13], (DEPTH, D_MODEL), 0.02),
        "w_ple_gate": nrm(ks[14], (DEPTH, D_MODEL, D_MODEL), D_MODEL ** -0.5),
        "w_ple_proj": nrm(ks[15], (DEPTH, PLE_DIM, D_MODEL), PLE_DIM ** -0.5),
        "final_norm": 1.0 + nrm(ks[16], (D_MODEL,), 0.02),
    }


def _fwd_reference(x, p, norm_mix, w_in, lb_logits, hg_norm, rel_bias, w_out, norm_ffn,
              w_up, conv_w, conv_b, w_down, norm_ple, w_ple_gate, w_ple_proj, final_norm):
    lb_all = jnp.cumsum(jax.nn.softmax(lb_logits.astype(jnp.float32), axis=0), axis=0)[:DEPTH]
    split_idx = [int(v) for v in np.cumsum(IN_SPLITS)[:-1]]
    h = x
    for i in range(DEPTH):
        a = rmsnorm(h, norm_mix[i])
        proj = a @ w_in[i]
        hq, hf, hi, hg, aq, ak, av = jnp.split(proj, split_idx, axis=-1)
        y_hg = hgrn2_mixer(hq, hf, hi, hg, lb_all[i], hg_norm[i])
        y_att = chunk_attention(aq, ak, av, rel_bias[i])
        h = h + jnp.concatenate([y_hg, y_att], axis=-1) @ w_out[i]
        h = h + conv_ffn(rmsnorm(h, norm_ffn[i]), w_up[i], conv_w[i], conv_b[i], w_down[i])
        gate = jax.nn.sigmoid(rmsnorm(h, norm_ple[i]) @ w_ple_gate[i])
        h = h + gate * (p[i] @ w_ple_proj[i])
    return rmsnorm(h, final_norm)


import jax as _jax
import jax.numpy as _jnp

TWIN_FORMAT = 'train_step'
FWD_PARAMS = ['x', 'p', 'norm_mix', 'w_in', 'lb_logits', 'hg_norm', 'rel_bias', 'w_out', 'norm_ffn', 'w_up', 'conv_w', 'conv_b', 'w_down', 'norm_ple', 'w_ple_gate', 'w_ple_proj', 'final_norm']
TWIN_WEIGHTS = ['norm_mix', 'w_in', 'lb_logits', 'hg_norm', 'rel_bias', 'w_out', 'norm_ffn', 'w_up', 'conv_w', 'conv_b', 'w_down', 'norm_ple', 'w_ple_gate', 'w_ple_proj', 'final_norm']
TWIN_DIFF_INPUT = 'x'
TWIN_INPUTS = ['x', 'p', 'norm_mix', 'w_in', 'lb_logits', 'hg_norm', 'rel_bias', 'w_out', 'norm_ffn', 'w_up', 'conv_w', 'conv_b', 'w_down', 'norm_ple', 'w_ple_gate', 'w_ple_proj', 'final_norm', 'loss_target', 'm_norm_mix', 'm_w_in', 'm_lb_logits', 'm_hg_norm', 'm_rel_bias', 'm_w_out', 'm_norm_ffn', 'm_w_up', 'm_conv_w', 'm_conv_b', 'm_w_down', 'm_norm_ple', 'm_w_ple_gate', 'm_w_ple_proj', 'm_final_norm', 'v_norm_mix', 'v_w_in', 'v_lb_logits', 'v_hg_norm', 'v_rel_bias', 'v_w_out', 'v_norm_ffn', 'v_w_up', 'v_conv_w', 'v_conv_b', 'v_w_down', 'v_norm_ple', 'v_w_ple_gate', 'v_w_ple_proj', 'v_final_norm']
TWIN_OUTPUTS = ['loss', 'grad_x', 'grad_norm_mix', 'grad_w_in', 'grad_lb_logits', 'grad_hg_norm', 'grad_rel_bias', 'grad_w_out', 'grad_norm_ffn', 'grad_w_up', 'grad_conv_w', 'grad_conv_b', 'grad_w_down', 'grad_norm_ple', 'grad_w_ple_gate', 'grad_w_ple_proj', 'grad_final_norm', 'delta_norm_mix', 'delta_w_in', 'delta_lb_logits', 'delta_hg_norm', 'delta_rel_bias', 'delta_w_out', 'delta_norm_ffn', 'delta_w_up', 'delta_conv_w', 'delta_conv_b', 'delta_w_down', 'delta_norm_ple', 'delta_w_ple_gate', 'delta_w_ple_proj', 'delta_final_norm', 'new_m_norm_mix', 'new_m_w_in', 'new_m_lb_logits', 'new_m_hg_norm', 'new_m_rel_bias', 'new_m_w_out', 'new_m_norm_ffn', 'new_m_w_up', 'new_m_conv_w', 'new_m_conv_b', 'new_m_w_down', 'new_m_norm_ple', 'new_m_w_ple_gate', 'new_m_w_ple_proj', 'new_m_final_norm', 'new_v_norm_mix', 'new_v_w_in', 'new_v_lb_logits', 'new_v_hg_norm', 'new_v_rel_bias', 'new_v_w_out', 'new_v_norm_ffn', 'new_v_w_up', 'new_v_conv_w', 'new_v_conv_b', 'new_v_w_down', 'new_v_norm_ple', 'new_v_w_ple_gate', 'new_v_w_ple_proj', 'new_v_final_norm']
TWIN_LEAF_KINDS = {'loss': 'loss', 'grad_x': 'grad_x', 'grad_norm_mix': 'grad_w', 'grad_w_in': 'grad_w', 'grad_lb_logits': 'grad_w', 'grad_hg_norm': 'grad_w', 'grad_rel_bias': 'grad_w', 'grad_w_out': 'grad_w', 'grad_norm_ffn': 'grad_w', 'grad_w_up': 'grad_w', 'grad_conv_w': 'grad_w', 'grad_conv_b': 'grad_w', 'grad_w_down': 'grad_w', 'grad_norm_ple': 'grad_w', 'grad_w_ple_gate': 'grad_w', 'grad_w_ple_proj': 'grad_w', 'grad_final_norm': 'grad_w', 'delta_norm_mix': 'delta_w', 'delta_w_in': 'delta_w', 'delta_lb_logits': 'delta_w', 'delta_hg_norm': 'delta_w', 'delta_rel_bias': 'delta_w', 'delta_w_out': 'delta_w', 'delta_norm_ffn': 'delta_w', 'delta_w_up': 'delta_w', 'delta_conv_w': 'delta_w', 'delta_conv_b': 'delta_w', 'delta_w_down': 'delta_w', 'delta_norm_ple': 'delta_w', 'delta_w_ple_gate': 'delta_w', 'delta_w_ple_proj': 'delta_w', 'delta_final_norm': 'delta_w', 'new_m_norm_mix': 'new_m', 'new_m_w_in': 'new_m', 'new_m_lb_logits': 'new_m', 'new_m_hg_norm': 'new_m', 'new_m_rel_bias': 'new_m', 'new_m_w_out': 'new_m', 'new_m_norm_ffn': 'new_m', 'new_m_w_up': 'new_m', 'new_m_conv_w': 'new_m', 'new_m_conv_b': 'new_m', 'new_m_w_down': 'new_m', 'new_m_norm_ple': 'new_m', 'new_m_w_ple_gate': 'new_m', 'new_m_w_ple_proj': 'new_m', 'new_m_final_norm': 'new_m', 'new_v_norm_mix': 'new_v', 'new_v_w_in': 'new_v', 'new_v_lb_logits': 'new_v', 'new_v_hg_norm': 'new_v', 'new_v_rel_bias': 'new_v', 'new_v_w_out': 'new_v', 'new_v_norm_ffn': 'new_v', 'new_v_w_up': 'new_v', 'new_v_conv_w': 'new_v', 'new_v_conv_b': 'new_v', 'new_v_w_down': 'new_v', 'new_v_norm_ple': 'new_v', 'new_v_w_ple_gate': 'new_v', 'new_v_w_ple_proj': 'new_v', 'new_v_final_norm': 'new_v'}


def _forward(args):
    return _fwd_reference(*[args[k] for k in FWD_PARAMS])


def _output_shape():
    def fwd():
        inp = _fwd_setup_inputs(0)
        return _fwd_reference(*[inp[k] for k in FWD_PARAMS])
    out = _jax.eval_shape(fwd)
    return out.shape, out.dtype

N_MICROBATCH = 1
ADAM_LR = 0.001
ADAM_B1 = 0.9
ADAM_B2 = 0.999
ADAM_EPS = 1e-08
ADAM_WD = 0.01
ADAM_STEP = 10
PER_EXAMPLE_BATCH_AXIS = {'x': 0, 'p': 1, 'loss_target': 0}
SHARED_INPUTS = []
_WEIGHT_DTYPES = {'norm_mix': _jnp.float32, 'w_in': _jnp.float32, 'lb_logits': _jnp.float32, 'hg_norm': _jnp.float32, 'rel_bias': _jnp.float32, 'w_out': _jnp.float32, 'norm_ffn': _jnp.float32, 'w_up': _jnp.float32, 'conv_w': _jnp.float32, 'conv_b': _jnp.float32, 'w_down': _jnp.float32, 'norm_ple': _jnp.float32, 'w_ple_gate': _jnp.float32, 'w_ple_proj': _jnp.float32, 'final_norm': _jnp.float32}
MOMENT_SCALE = {'norm_mix': 5.435011e-02, 'w_in': 2.901903e-02, 'lb_logits': 4.408048e-03, 'hg_norm': 1.350336e-01, 'rel_bias': 9.055290e-03, 'w_out': 3.707365e-02, 'norm_ffn': 5.850263e-02, 'w_up': 2.499834e-02, 'conv_w': 2.506116e-02, 'conv_b': 2.470774e-02, 'w_down': 4.086885e-02, 'norm_ple': 1.356454e-02, 'w_ple_gate': 1.380965e-02, 'w_ple_proj': 3.550989e-02, 'final_norm': 1.600417e+01}


def _to_microbatches(a, axis):
    t = _jnp.moveaxis(a, axis, 0)
    t = t.reshape((N_MICROBATCH, t.shape[0] // N_MICROBATCH) + t.shape[1:])
    return _jnp.moveaxis(t, 1, axis + 1)


def setup_inputs(seed: int = 0) -> dict:
    inp = _fwd_setup_inputs(seed)
    key = _jax.random.fold_in(_jax.random.key(seed), 7919)
    shape, _ = _output_shape()
    out = dict(inp)
    out["loss_target"] = _jax.random.normal(_jax.random.fold_in(key, 0), shape, _jnp.float32)
    for i, name in enumerate(TWIN_WEIGHTS):
        w = inp[name].astype(_jnp.float32)
        if MOMENT_SCALE is None:
            s = _jnp.sqrt(_jnp.mean(_jnp.square(w)) + 1e-30)
        else:
            s = MOMENT_SCALE[name]
        km, kv = _jax.random.split(_jax.random.fold_in(key, i + 1))
        out[name] = w
        out["m_" + name] = s * _jax.random.normal(km, w.shape, _jnp.float32)
        out["v_" + name] = (s * s) * _jax.random.uniform(kv, w.shape, _jnp.float32, 0.5, 1.5)
    if N_MICROBATCH > 1:
        for name, axis in PER_EXAMPLE_BATCH_AXIS.items():
            out[name] = _to_microbatches(out[name], axis)
    return {'x': out['x'], 'p': out['p'], 'norm_mix': out['norm_mix'], 'w_in': out['w_in'], 'lb_logits': out['lb_logits'], 'hg_norm': out['hg_norm'], 'rel_bias': out['rel_bias'], 'w_out': out['w_out'], 'norm_ffn': out['norm_ffn'], 'w_up': out['w_up'], 'conv_w': out['conv_w'], 'conv_b': out['conv_b'], 'w_down': out['w_down'], 'norm_ple': out['norm_ple'], 'w_ple_gate': out['w_ple_gate'], 'w_ple_proj': out['w_ple_proj'], 'final_norm': out['final_norm'], 'loss_target': out['loss_target'], 'm_norm_mix': out['m_norm_mix'], 'm_w_in': out['m_w_in'], 'm_lb_logits': out['m_lb_logits'], 'm_hg_norm': out['m_hg_norm'], 'm_rel_bias': out['m_rel_bias'], 'm_w_out': out['m_w_out'], 'm_norm_ffn': out['m_norm_ffn'], 'm_w_up': out['m_w_up'], 'm_conv_w': out['m_conv_w'], 'm_conv_b': out['m_conv_b'], 'm_w_down': out['m_w_down'], 'm_norm_ple': out['m_norm_ple'], 'm_w_ple_gate': out['m_w_ple_gate'], 'm_w_ple_proj': out['m_w_ple_proj'], 'm_final_norm': out['m_final_norm'], 'v_norm_mix': out['v_norm_mix'], 'v_w_in': out['v_w_in'], 'v_lb_logits': out['v_lb_logits'], 'v_hg_norm': out['v_hg_norm'], 'v_rel_bias': out['v_rel_bias'], 'v_w_out': out['v_w_out'], 'v_norm_ffn': out['v_norm_ffn'], 'v_w_up': out['v_w_up'], 'v_conv_w': out['v_conv_w'], 'v_conv_b': out['v_conv_b'], 'v_w_down': out['v_w_down'], 'v_norm_ple': out['v_norm_ple'], 'v_w_ple_gate': out['v_w_ple_gate'], 'v_w_ple_proj': out['v_w_ple_proj'], 'v_final_norm': out['v_final_norm']}


def _loss(weights, diff, rest, loss_target):
    with _jax.named_scope("forward"):
        args = {**rest, TWIN_DIFF_INPUT: diff, **{k: w.astype(_WEIGHT_DTYPES[k]) for k, w in weights.items()}}
        y = _forward(args)
    with _jax.named_scope("loss_head"):
        err = _jnp.square(y.astype(_jnp.float32) - loss_target)
        return 0.5 * _jnp.sum(_jnp.mean(err, axis=-1)) if err.ndim else 0.5 * err


def _adamw(w, g, m, v):
    m = ADAM_B1 * m + (1.0 - ADAM_B1) * g
    v = ADAM_B2 * v + (1.0 - ADAM_B2) * _jnp.square(g)
    m_hat = m / (1.0 - ADAM_B1 ** ADAM_STEP)
    v_hat = v / (1.0 - ADAM_B2 ** ADAM_STEP)
    delta = -ADAM_LR * (m_hat / (_jnp.sqrt(v_hat) + ADAM_EPS) + ADAM_WD * w)
    return delta, m, v


def reference(x, p, norm_mix, w_in, lb_logits, hg_norm, rel_bias, w_out, norm_ffn, w_up, conv_w, conv_b, w_down, norm_ple, w_ple_gate, w_ple_proj, final_norm, loss_target, m_norm_mix, m_w_in, m_lb_logits, m_hg_norm, m_rel_bias, m_w_out, m_norm_ffn, m_w_up, m_conv_w, m_conv_b, m_w_down, m_norm_ple, m_w_ple_gate, m_w_ple_proj, m_final_norm, v_norm_mix, v_w_in, v_lb_logits, v_hg_norm, v_rel_bias, v_w_out, v_norm_ffn, v_w_up, v_conv_w, v_conv_b, v_w_down, v_norm_ple, v_w_ple_gate, v_w_ple_proj, v_final_norm):
    given = dict(x=x, p=p, norm_mix=norm_mix, w_in=w_in, lb_logits=lb_logits, hg_norm=hg_norm, rel_bias=rel_bias, w_out=w_out, norm_ffn=norm_ffn, w_up=w_up, conv_w=conv_w, conv_b=conv_b, w_down=w_down, norm_ple=norm_ple, w_ple_gate=w_ple_gate, w_ple_proj=w_ple_proj, final_norm=final_norm, loss_target=loss_target, m_norm_mix=m_norm_mix, m_w_in=m_w_in, m_lb_logits=m_lb_logits, m_hg_norm=m_hg_norm, m_rel_bias=m_rel_bias, m_w_out=m_w_out, m_norm_ffn=m_norm_ffn, m_w_up=m_w_up, m_conv_w=m_conv_w, m_conv_b=m_conv_b, m_w_down=m_w_down, m_norm_ple=m_norm_ple, m_w_ple_gate=m_w_ple_gate, m_w_ple_proj=m_w_ple_proj, m_final_norm=m_final_norm, v_norm_mix=v_norm_mix, v_w_in=v_w_in, v_lb_logits=v_lb_logits, v_hg_norm=v_hg_norm, v_rel_bias=v_rel_bias, v_w_out=v_w_out, v_norm_ffn=v_norm_ffn, v_w_up=v_w_up, v_conv_w=v_conv_w, v_conv_b=v_conv_b, v_w_down=v_w_down, v_norm_ple=v_norm_ple, v_w_ple_gate=v_w_ple_gate, v_w_ple_proj=v_w_ple_proj, v_final_norm=v_final_norm)
    weights = {n: given[n] for n in TWIN_WEIGHTS}
    shared = {n: given[n] for n in SHARED_INPUTS}
    per_example = {n: given[n] for n in ['x', 'p']}
    grad_fn = _jax.value_and_grad(_loss, argnums=(0, 1))

    def one_microbatch(ex, loss_target):
        ex = dict(ex)
        diff = ex.pop(TWIN_DIFF_INPUT)
        return grad_fn(weights, diff, {**shared, **ex}, loss_target)

    if N_MICROBATCH == 1:
        loss, (grad_w, grad_x) = one_microbatch(per_example, given["loss_target"])
    else:
        def body(carry, xs):
            loss_sum, grad_sum = carry
            l_k, (gw_k, gx_k) = one_microbatch(xs[0], xs[1])
            with _jax.named_scope("update"):
                return (loss_sum + l_k, _jax.tree.map(_jnp.add, grad_sum, gw_k)), gx_k

        init = (_jnp.zeros((), _jnp.float32), _jax.tree.map(_jnp.zeros_like, weights))
        (loss, grad_w), grad_x = _jax.lax.scan(body, init, (per_example, given["loss_target"]))
    with _jax.named_scope("update"):
        delta_w, new_m, new_v = {}, {}, {}
        for n in TWIN_WEIGHTS:
            delta_w[n], new_m[n], new_v[n] = _adamw(weights[n], grad_w[n], given["m_" + n], given["v_" + n])
    return (loss, grad_x, *[grad_w[n] for n in TWIN_WEIGHTS], *[delta_w[n] for n in TWIN_WEIGHTS],
            *[new_m[n] for n in TWIN_WEIGHTS], *[new_v[n] for n in TWIN_WEIGHTS])
```

```python
import functools

import jax
import jax.numpy as jnp
from jax import lax
from jax.experimental import pallas as pl
from jax.experimental.pallas import tpu as pltpu

F32 = jnp.float32
BF16 = jnp.bfloat16

D_MODEL = 2048
CHUNK = 64
HG_HEADS = 8
HEAD_DIM = 128
HG_WIDTH = HG_HEADS * HEAD_DIM
ATT_HEADS = 8
ATT_WIDTH = ATT_HEADS * HEAD_DIM
LEFT_CHUNKS = 8
PAD = LEFT_CHUNKS * CHUNK
BAND = PAD + CHUNK
REL_CLIP = 128
N_REL = 2 * REL_CLIP + 1
N_REL_PAD = 384
D_FF = 5632
EPS = 1e-6
ATT_SCALE = HEAD_DIM ** -0.5
SUB = 16

ADAM_LR = 0.001
ADAM_B1 = 0.9
ADAM_B2 = 0.999
ADAM_EPS = 1e-08
ADAM_WD = 0.01
ADAM_STEP = 10

N_DEV = 8
VMEM_LIMIT = 48 * 1024 * 1024
MESH = pl.DeviceIdType.MESH
ANY = pl.BlockSpec(memory_space=pl.ANY)
HIGHEST = lax.Precision.HIGHEST

NN = (((1,), (0,)), ((), ()))
NT = (((1,), (1,)), ((), ()))
TN = (((0,), (0,)), ((), ()))


def _params(*sem):
    return pltpu.CompilerParams(dimension_semantics=sem if sem else None, vmem_limit_bytes=VMEM_LIMIT)


def _dot(a, b, dims=NN):
    return lax.dot_general(a, b, dims, preferred_element_type=F32)


def _dot3(a, b, dims=NN):
    a_hi, b_hi = a.astype(BF16), b.astype(BF16)
    a_lo, b_lo = (a - a_hi.astype(F32)).astype(BF16), (b - b_hi.astype(F32)).astype(BF16)
    return _dot(a_hi, b_hi, dims) + (_dot(a_hi, b_lo, dims) + _dot(a_lo, b_hi, dims))


def _sigmoid(x):
    return 1.0 / (1.0 + jnp.exp(-x))


def _tile(n, prefs):
    for t in prefs:
        if n % t == 0:
            return t
    return n


def _matmul(a, b, mode, out_dtype, name, tm=512, tn=1024, tk=None, resid=None):
    if mode == "nn":
        (m, k), n = a.shape, b.shape[1]
    elif mode == "nt":
        (m, k), n = a.shape, b.shape[0]
    else:
        (k, m), n = a.shape, b.shape[1]
    tm = _tile(m, (tm, 256, 128))
    tn = _tile(n, (tn, 512, 256, 128))
    tk = k if tk is None else _tile(k, (tk,))
    nk = k // tk
    dims = {"nn": NN, "nt": NT, "tn": TN}[mode]
    a_spec = pl.BlockSpec((tk, tm), lambda i, j, s: (s, i)) if mode == "tn" else pl.BlockSpec((tm, tk), lambda i, j, s: (i, s))
    b_spec = pl.BlockSpec((tn, tk), lambda i, j, s: (j, s)) if mode == "nt" else pl.BlockSpec((tk, tn), lambda i, j, s: (s, j))
    o_spec = pl.BlockSpec((tm, tn), lambda i, j, s: (i, j))
    has_res = resid is not None

    def body(*refs):
        a_ref, b_ref = refs[0], refs[1]
        o_ref = refs[2 + has_res]
        part = _dot(a_ref[...].astype(BF16), b_ref[...].astype(BF16), dims)

        def finish(acc):
            if has_res:
                acc = acc + refs[2][...]
            o_ref[...] = acc.astype(out_dtype)

        if nk == 1:
            finish(part)
        else:
            acc_ref = refs[-1]
            s = pl.program_id(2)

            @pl.when(s == 0)
            def _():
                acc_ref[...] = part

            @pl.when(s > 0)
            def _():
                acc_ref[...] += part

            @pl.when(s == nk - 1)
            def _():
                finish(acc_ref[...])

    return pl.pallas_call(
        body,
        name=name,
        grid=(m // tm, n // tn, nk),
        in_specs=[a_spec, b_spec] + ([o_spec] if has_res else []),
        out_specs=o_spec,
        out_shape=jax.ShapeDtypeStruct((m, n), out_dtype),
        scratch_shapes=[pltpu.VMEM((tm, tn), F32)] if nk > 1 else [],
        compiler_params=_params("parallel", "parallel", "arbitrary"),
    )(*([a, b] + ([resid] if has_res else [])))


def _rms_fwd(x, g, name):
    t, d = x.shape
    tm = _tile(t, (256,))

    def body(x_ref, g_ref, a_ref, r_ref):
        xv = x_ref[...]
        r = lax.rsqrt(jnp.mean(xv * xv, axis=-1, keepdims=True) + EPS)
        a_ref[...] = (xv * r * g_ref[...]).astype(BF16)
        r_ref[...] = r

    row = pl.BlockSpec((tm, d), lambda i: (i, 0))
    return pl.pallas_call(
        body,
        name=name,
        grid=(t // tm,),
        in_specs=[row, pl.BlockSpec((1, d), lambda i: (0, 0))],
        out_specs=[row, pl.BlockSpec((tm, 1), lambda i: (i, 0))],
        out_shape=[jax.ShapeDtypeStruct((t, d), BF16), jax.ShapeDtypeStruct((t, 1), F32)],
        compiler_params=_params("parallel"),
    )(x, g)


def _rms_bwd(da, x, r, g, resid, name):
    t, d = x.shape
    tm = _tile(t, (256,))

    def body(da_ref, x_ref, r_ref, g_ref, res_ref, dx_ref, dg_ref):
        i = pl.program_id(0)
        rv = r_ref[...]
        n = x_ref[...] * rv
        dav = da_ref[...]
        dn = dav * g_ref[...]
        dx_ref[...] = rv * (dn - n * jnp.mean(dn * n, axis=-1, keepdims=True)) + res_ref[...]
        part = jnp.sum(dav * n, axis=0, keepdims=True)

        @pl.when(i == 0)
        def _():
            dg_ref[...] = part

        @pl.when(i > 0)
        def _():
            dg_ref[...] += part

    row = pl.BlockSpec((tm, d), lambda i: (i, 0))
    vec = pl.BlockSpec((1, d), lambda i: (0, 0))
    return pl.pallas_call(
        body,
        name=name,
        grid=(t // tm,),
        in_specs=[row, row, pl.BlockSpec((tm, 1), lambda i: (i, 0)), vec, row],
        out_specs=[row, vec],
        out_shape=[jax.ShapeDtypeStruct((t, d), F32), jax.ShapeDtypeStruct((1, d), F32)],
        compiler_params=_params("arbitrary"),
    )(da, x, r, g, resid)


def _tri(n, upper):
    r = lax.broadcasted_iota(jnp.int32, (n, n), 0)
    c = lax.broadcasted_iota(jnp.int32, (n, n), 1)
    return jnp.where((c >= r) if upper else (c <= r), 1.0, 0.0).astype(F32)


def _hgrn_gates(q, fp, lbl):
    l0, l1 = lbl[0:1, :], lbl[1:2, :]
    mx = jnp.maximum(l0, l1)
    e0, e1 = jnp.exp(l0 - mx), jnp.exp(l1 - mx)
    lb = e0 / (e0 + e1)
    sig = _sigmoid(fp)
    f = lb + (1.0 - lb) * sig
    kk = (1.0 - lb) * _sigmoid(-fp)
    sq = _sigmoid(q)
    b = jnp.dot(_tri(CHUNK, False), jnp.log(f), precision=HIGHEST, preferred_element_type=F32)
    return lb, sig, f, kk, sq, q * sq, b


def _intra_blocks(b):
    out = []
    for lo in range(0, CHUNK, SUB):
        hi = lo + SUB
        br = b[lo + SUB // 2 : lo + SUB // 2 + 1, :]
        row = lax.broadcasted_iota(jnp.int32, (SUB, hi), 0) + lo
        col = lax.broadcasted_iota(jnp.int32, (SUB, hi), 1)
        out.append((lo, hi, jnp.exp(b[lo:hi] - br), jnp.exp(br - b[:hi]), col <= row))
    return out


def _hgrn_fwd(proj, lb_logits, hg_norm):
    t = proj.shape[0]
    nc = t // CHUNK

    def body(q_ref, f_ref, i_ref, g_ref, lbl_ref, hgn_ref, y_ref, o_ref, st_ref, s_scr):
        c = pl.program_id(1)

        @pl.when(c == 0)
        def _():
            s_scr[...] = jnp.zeros_like(s_scr)

        st = s_scr[...]
        st_ref[...] = st
        _, _, _, kk, _, qf, b = _hgrn_gates(q_ref[...], f_ref[...], lbl_ref[...])
        vb = i_ref[...].astype(BF16)
        bl = b[CHUNK - 1 : CHUNK, :]
        o = _dot((qf * jnp.exp(b)).astype(BF16), st.astype(BF16), NT)
        parts = []
        for lo, hi, ea, eb, mask in _intra_blocks(b):
            p = jnp.where(mask, _dot((qf[lo:hi] * ea).astype(BF16), (kk[:hi] * eb).astype(BF16), NT), 0.0)
            parts.append(_dot(p.astype(BF16), vb[:hi]))
        o = o + jnp.concatenate(parts, axis=0)
        s_scr[...] = st * jnp.exp(bl) + _dot(vb, (kk * jnp.exp(bl - b)).astype(BF16), TN)
        o_ref[...] = o
        r = lax.rsqrt(jnp.mean(o * o, axis=-1, keepdims=True) + EPS)
        gg = g_ref[...]
        y_ref[...] = ((o * r * hgn_ref[...]) * (gg * _sigmoid(gg))).astype(BF16)

    def col(k):
        return pl.BlockSpec((CHUNK, HEAD_DIM), lambda h, c: (c, k * HG_HEADS + h))

    out = pl.BlockSpec((CHUNK, HEAD_DIM), lambda h, c: (c, h))
    return pl.pallas_call(
        body,
        name="hgrn_fwd",
        grid=(HG_HEADS, nc),
        in_specs=[col(0), col(1), col(2), col(3), pl.BlockSpec((2, HEAD_DIM), lambda h, c: (0, h)), pl.BlockSpec((1, HEAD_DIM), lambda h, c: (0, 0))],
        out_specs=[out, out, pl.BlockSpec((None, None, HEAD_DIM, HEAD_DIM), lambda h, c: (h, c, 0, 0))],
        out_shape=[
            jax.ShapeDtypeStruct((t, HG_WIDTH), BF16),
            jax.ShapeDtypeStruct((t, HG_WIDTH), F32),
            jax.ShapeDtypeStruct((HG_HEADS, nc, HEAD_DIM, HEAD_DIM), F32),
        ],
        scratch_shapes=[pltpu.VMEM((HEAD_DIM, HEAD_DIM), F32)],
        compiler_params=_params("arbitrary", "arbitrary"),
    )(proj, proj, proj, proj, lb_logits, hg_norm)


def _hgrn_bwd(proj, lb_logits, hg_norm, o_hg, dycat, states):
    t = proj.shape[0]
    nc = t // CHUNK

    def body(q_ref, f_ref, i_ref, g_ref, lbl_ref, hgn_ref, o_ref, dy_ref, st_ref, dp_ref, dlbl_ref, dhgn_ref, dst_scr, dlb_scr):
        h = pl.program_id(0)
        c = pl.program_id(1)

        @pl.when(c == 0)
        def _():
            dst_scr[...] = jnp.zeros_like(dst_scr)
            dlb_scr[...] = jnp.zeros_like(dlb_scr)

        @pl.when((c == 0) & (h == 0))
        def _():
            dhgn_ref[...] = jnp.zeros_like(dhgn_ref)

        q, fp, gg = q_ref[...], f_ref[...], g_ref[...]
        lb, sig, f, kk, sq, qf, b = _hgrn_gates(q, fp, lbl_ref[...])
        hgn = hgn_ref[...]
        o, dy = o_ref[...], dy_ref[...]
        sg = _sigmoid(gg)
        r = lax.rsqrt(jnp.mean(o * o, axis=-1, keepdims=True) + EPS)
        n = o * r
        don = dy * (gg * sg)
        dgg = dy * (n * hgn) * (sg * (1.0 + gg * (1.0 - sg)))
        dhgn_ref[...] += jnp.sum(don * n, axis=0, keepdims=True)
        dn = don * hgn
        do = r * (dn - n * jnp.mean(dn * n, axis=-1, keepdims=True))
        dob = do.astype(BF16)
        vi = i_ref[...]
        vb = vi.astype(BF16)
        st, dstn = st_ref[...], dst_scr[...]
        bl = b[CHUNK - 1 : CHUNK, :]
        e_b, e_bl, e_l = jnp.exp(b), jnp.exp(bl - b), jnp.exp(bl)
        dq_acc = _dot3(do, st) * e_b
        dk_inter = _dot3(vi, dstn) * e_bl
        dk_acc = dk_inter
        dv_acc = _dot((kk * e_bl).astype(BF16), dstn.astype(BF16), NT)
        dst_scr[...] = dstn * e_l + _dot(dob, (qf * e_b).astype(BF16), TN)
        db_last = e_l * jnp.sum(st * dstn, axis=0, keepdims=True) + jnp.sum(kk * dk_inter, axis=0, keepdims=True)
        dq_parts = []
        for lo, hi, ea, eb, mask in _intra_blocks(b):
            a, bk = qf[lo:hi] * ea, kk[:hi] * eb
            p = jnp.where(mask, _dot(a.astype(BF16), bk.astype(BF16), NT), 0.0)
            dp = jnp.where(mask, _dot3(do[lo:hi], vi[:hi], NT), 0.0)
            dq_parts.append(_dot3(dp, bk) * ea)
            dki = _dot3(dp, a, TN) * eb
            dvi = _dot(p.astype(BF16), dob[lo:hi], TN)
            if hi < CHUNK:
                zeros = jnp.zeros((CHUNK - hi, HEAD_DIM), F32)
                dki = jnp.concatenate([dki, zeros], axis=0)
                dvi = jnp.concatenate([dvi, zeros], axis=0)
            dk_acc = dk_acc + dki
            dv_acc = dv_acc + dvi
        dq_acc = dq_acc + jnp.concatenate(dq_parts, axis=0)
        rows = lax.broadcasted_iota(jnp.int32, (CHUNK, HEAD_DIM), 0)
        db = qf * dq_acc - kk * dk_acc + jnp.where(rows == CHUNK - 1, db_last, 0.0)
        dlf = jnp.dot(_tri(CHUNK, True), db, precision=HIGHEST, preferred_element_type=F32)
        dfk = dlf / f - dk_acc
        dp_ref[0] = (dq_acc * (sq * (1.0 + q * (1.0 - sq)))).astype(BF16)
        dp_ref[1] = ((1.0 - lb) * dfk * sig * (1.0 - sig)).astype(BF16)
        dp_ref[2] = dv_acc.astype(BF16)
        dp_ref[3] = dgg.astype(BF16)
        dlb_scr[...] += jnp.sum(dfk * (1.0 - sig), axis=0, keepdims=True)

        @pl.when(c == nc - 1)
        def _():
            dl0 = dlb_scr[...] * lb * (1.0 - lb)
            dlbl_ref[0:1, :] = dl0
            dlbl_ref[1:2, :] = -dl0

    def col(k):
        return pl.BlockSpec((CHUNK, HEAD_DIM), lambda h, c: (nc - 1 - c, k * HG_HEADS + h))

    blk = pl.BlockSpec((CHUNK, HEAD_DIM), lambda h, c: (nc - 1 - c, h))
    return pl.pallas_call(
        body,
        name="hgrn_bwd",
        grid=(HG_HEADS, nc),
        in_specs=[
            col(0), col(1), col(2), col(3),
            pl.BlockSpec((2, HEAD_DIM), lambda h, c: (0, h)),
            pl.BlockSpec((1, HEAD_DIM), lambda h, c: (0, 0)),
            blk, blk,
            pl.BlockSpec((None, None, HEAD_DIM, HEAD_DIM), lambda h, c: (h, nc - 1 - c, 0, 0)),
        ],
        out_specs=[
            pl.BlockSpec((4, CHUNK, HEAD_DIM), lambda h, c: (0, nc - 1 - c, h)),
            pl.BlockSpec((2, HEAD_DIM), lambda h, c: (0, h)),
            pl.BlockSpec((1, HEAD_DIM), lambda h, c: (0, 0)),
        ],
        out_shape=[
            jax.ShapeDtypeStruct((4, t, HG_WIDTH), BF16),
            jax.ShapeDtypeStruct((2, HG_WIDTH), F32),
            jax.ShapeDtypeStruct((1, HEAD_DIM), F32),
        ],
        scratch_shapes=[pltpu.VMEM((HEAD_DIM, HEAD_DIM), F32), pltpu.VMEM((1, HEAD_DIM), F32)],
        compiler_params=_params("arbitrary", "arbitrary"),
    )(proj, proj, proj, proj, lb_logits, hg_norm, o_hg, dycat, states)


def _rel_index(rows, cols, row0):
    r = lax.broadcasted_iota(jnp.int32, (rows, cols), 0) + row0
    c = lax.broadcasted_iota(jnp.int32, (rows, cols), 1)
    return jnp.clip(r + PAD - c, -REL_CLIP, REL_CLIP) + REL_CLIP


def _bias_table(rel_bias):
    def body(rb_ref, o_ref):
        h = pl.program_id(0)
        idx = _rel_index(CHUNK, BAND, 0)

        def step(i, acc):
            return jnp.where(idx == i, rb_ref[h, i], acc)

        o_ref[...] = lax.fori_loop(0, N_REL, step, jnp.zeros((CHUNK, BAND), F32))

    return pl.pallas_call(
        body,
        name="bias_table",
        grid=(ATT_HEADS,),
        in_specs=[pl.BlockSpec(memory_space=pltpu.SMEM)],
        out_specs=pl.BlockSpec((None, CHUNK, BAND), lambda h: (h, 0, 0)),
        out_shape=jax.ShapeDtypeStruct((ATT_HEADS, CHUNK, BAND), F32),
        compiler_params=_params("parallel"),
    )(rel_bias)


def _att_probs(q_ref, kpad, bias_ref, c):
    qs = (q_ref[...] * ATT_SCALE).astype(BF16)
    start = pl.multiple_of(c * CHUNK, CHUNK)
    kb = kpad[pl.ds(start, BAND), :]
    s = _dot(qs, kb, NT) + bias_ref[...]
    key = lax.broadcasted_iota(jnp.int32, (CHUNK, BAND), 1) + (c * CHUNK - PAD)
    s = jnp.where(key >= 0, s, jnp.finfo(F32).min)
    e = jnp.exp(s - jnp.max(s, axis=-1, keepdims=True))
    return qs, kb, start, e / jnp.sum(e, axis=-1, keepdims=True)


def _fill_padded(dst, src):
    dst[0:PAD, :] = jnp.zeros((PAD, HEAD_DIM), BF16)
    dst[PAD:, :] = src[...].astype(BF16)


def _att_fwd(proj, bias):
    t = proj.shape[0]
    nc = t // CHUNK

    def body(q_ref, k_ref, v_ref, bias_ref, y_ref, kpad, vpad):
        c = pl.program_id(1)

        @pl.when(c == 0)
        def _():
            _fill_padded(kpad, k_ref)
            _fill_padded(vpad, v_ref)

        _, _, start, p = _att_probs(q_ref, kpad, bias_ref, c)
        y_ref[...] = _dot(p.astype(BF16), vpad[pl.ds(start, BAND), :]).astype(BF16)

    base = 4 * HG_HEADS
    return pl.pallas_call(
        body,
        name="att_fwd",
        grid=(ATT_HEADS, nc),
        in_specs=[
            pl.BlockSpec((CHUNK, HEAD_DIM), lambda h, c: (c, base + h)),
            pl.BlockSpec((t, HEAD_DIM), lambda h, c: (0, base + ATT_HEADS + h)),
            pl.BlockSpec((t, HEAD_DIM), lambda h, c: (0, base + 2 * ATT_HEADS + h)),
            pl.BlockSpec((None, CHUNK, BAND), lambda h, c: (h, 0, 0)),
        ],
        out_specs=pl.BlockSpec((CHUNK, HEAD_DIM), lambda h, c: (c, h)),
        out_shape=jax.ShapeDtypeStruct((t, ATT_WIDTH), BF16),
        scratch_shapes=[pltpu.VMEM((t + PAD, HEAD_DIM), BF16), pltpu.VMEM((t + PAD, HEAD_DIM), BF16)],
        compiler_params=_params("arbitrary", "arbitrary"),
    )(proj, proj, proj, bias)


def _att_bwd(proj, bias, dycat):
    t = proj.shape[0]
    nc = t // CHUNK

    def body(q_ref, k_ref, v_ref, bias_ref, dy_ref, dq_ref, dk_ref, dv_ref, g_ref, kpad, vpad, dkacc, dvacc):
        c = pl.program_id(1)

        @pl.when(c == 0)
        def _():
            _fill_padded(kpad, k_ref)
            _fill_padded(vpad, v_ref)
            dkacc[...] = jnp.zeros_like(dkacc)
            dvacc[...] = jnp.zeros_like(dvacc)
            g_ref[...] = jnp.zeros_like(g_ref)

        qs, kb, start, p = _att_probs(q_ref, kpad, bias_ref, c)
        band = pl.ds(start, BAND)
        dyb = dy_ref[...].astype(BF16)
        dvacc[band, :] += _dot(p.astype(BF16), dyb, TN)
        dp = _dot(dyb, vpad[band, :], NT)
        ds = p * (dp - jnp.sum(dp * p, axis=-1, keepdims=True))
        g_ref[...] += ds
        dsb = ds.astype(BF16)
        dq_ref[...] = (_dot(dsb, kb) * ATT_SCALE).astype(BF16)
        dkacc[band, :] += _dot(dsb, qs, TN)

        @pl.when(c == nc - 1)
        def _():
            dk_ref[...] = dkacc[PAD:, :].astype(BF16)
            dv_ref[...] = dvacc[PAD:, :].astype(BF16)

    base = 4 * HG_HEADS
    whole = pl.BlockSpec((t, HEAD_DIM), lambda h, c: (0, h))
    return pl.pallas_call(
        body,
        name="att_bwd",
        grid=(ATT_HEADS, nc),
        in_specs=[
            pl.BlockSpec((CHUNK, HEAD_DIM), lambda h, c: (c, base + h)),
            pl.BlockSpec((t, HEAD_DIM), lambda h, c: (0, base + ATT_HEADS + h)),
            pl.BlockSpec((t, HEAD_DIM), lambda h, c: (0, base + 2 * ATT_HEADS + h)),
            pl.BlockSpec((None, CHUNK, BAND), lambda h, c: (h, 0, 0)),
            pl.BlockSpec((CHUNK, HEAD_DIM), lambda h, c: (c, HG_HEADS + h)),
        ],
        out_specs=[pl.BlockSpec((CHUNK, HEAD_DIM), lambda h, c: (c, h)), whole, whole, pl.BlockSpec((None, CHUNK, BAND), lambda h, c: (h, 0, 0))],
        out_shape=[
            jax.ShapeDtypeStruct((t, ATT_WIDTH), BF16),
            jax.ShapeDtypeStruct((t, ATT_WIDTH), BF16),
            jax.ShapeDtypeStruct((t, ATT_WIDTH), BF16),
            jax.ShapeDtypeStruct((ATT_HEADS, CHUNK, BAND), F32),
        ],
        scratch_shapes=[
            pltpu.VMEM((t + PAD, HEAD_DIM), BF16),
            pltpu.VMEM((t + PAD, HEAD_DIM), BF16),
            pltpu.VMEM((t + PAD, HEAD_DIM), F32),
            pltpu.VMEM((t + PAD, HEAD_DIM), F32),
        ],
        compiler_params=_params("arbitrary", "arbitrary"),
    )(proj, proj, proj, bias, dycat)


def _rel_bias_grad(gsum):
    def body(g_ref, o_ref):
        def step(t, acc):
            rows = jnp.concatenate([g_ref[h, pl.ds(t, 1), :] for h in range(ATT_HEADS)], axis=0)
            hi = rows.astype(BF16)
            lo = (rows - hi.astype(F32)).astype(BF16)
            key = lax.broadcasted_iota(jnp.int32, (BAND, N_REL_PAD), 0)
            slot = lax.broadcasted_iota(jnp.int32, (BAND, N_REL_PAD), 1)
            onehot = jnp.where(jnp.clip(t + PAD - key, -REL_CLIP, REL_CLIP) + REL_CLIP == slot, 1.0, 0.0).astype(BF16)
            return acc + _dot(hi, onehot) + _dot(lo, onehot)

        o_ref[...] = lax.fori_loop(0, CHUNK, step, jnp.zeros((ATT_HEADS, N_REL_PAD), F32))

    return pl.pallas_call(
        body,
        name="rel_bias_grad",
        out_shape=jax.ShapeDtypeStruct((ATT_HEADS, N_REL_PAD), F32),
        compiler_params=_params(),
    )(gsum)


HALO = 16


def _ffn_tiles(t):
    tm = _tile(t, (512,))
    tc = 256
    return tm, tc, D_FF // tc


def _shift_down(x, halo, k, tm):
    rows = lax.broadcasted_iota(jnp.int32, x.shape, 0)
    out = jnp.where(rows >= k, pltpu.roll(x, k, 0), halo[HALO - 1 : HALO, :])
    if k == 2:
        out = jnp.where(rows == 0, halo[HALO - 2 : HALO - 1, :], out)
    return out


def _shift_up(x, halo, k, tm):
    rows = lax.broadcasted_iota(jnp.int32, x.shape, 0)
    out = jnp.where(rows < tm - k, pltpu.roll(x, tm - k, 0), halo[0:1, :])
    if k == 2:
        out = jnp.where(rows == tm - 1, halo[1:2, :], out)
    return out


def _conv_taps(u_ref, halo_ref, i, tm):
    u = u_ref[...].astype(F32)
    halo = jnp.where(i > 0, halo_ref[...].astype(F32), 0.0)
    return _shift_down(u, halo, 2, tm), _shift_down(u, halo, 1, tm), u


def _ffn_in_specs(tm, tc, nj):
    before = lambda off: pl.BlockSpec((HALO, tc), lambda j, i: (jnp.maximum(i * (tm // HALO) - 1, 0), off + j))
    tile = lambda off: pl.BlockSpec((tm, tc), lambda j, i: (i, off + j))
    vec = lambda rows, off: pl.BlockSpec((rows, tc), lambda j, i: (0, off + j))
    return tile, before, vec


def _ffn_act_fwd(u, conv_w, conv_b):
    t = u.shape[0]
    tm, tc, nj = _ffn_tiles(t)
    tile, before, vec = _ffn_in_specs(tm, tc, nj)

    def body(ug_ref, hg_ref, uv_ref, hv_ref, wg_ref, wv_ref, bg_ref, bv_ref, z_ref):
        i = pl.program_id(1)
        g2, g1, g0 = _conv_taps(ug_ref, hg_ref, i, tm)
        v2, v1, v0 = _conv_taps(uv_ref, hv_ref, i, tm)
        wg, wv = wg_ref[...], wv_ref[...]
        gate = bg_ref[...] + wg[0:1] * g2 + wg[1:2] * g1 + wg[2:3] * g0
        val = bv_ref[...] + wv[0:1] * v2 + wv[1:2] * v1 + wv[2:3] * v0
        z_ref[...] = (gate * _sigmoid(gate) * val).astype(BF16)

    return pl.pallas_call(
        body,
        name="ffn_act_fwd",
        grid=(nj, t // tm),
        in_specs=[tile(0), before(0), tile(nj), before(nj), vec(3, 0), vec(3, nj), vec(1, 0), vec(1, nj)],
        out_specs=pl.BlockSpec((tm, tc), lambda j, i: (i, j)),
        out_shape=jax.ShapeDtypeStruct((t, D_FF), BF16),
        compiler_params=_params("parallel", "parallel"),
    )(u, u, u, u, conv_w, conv_w, conv_b, conv_b)


def _ffn_act_bwd(u, dz, conv_w, conv_b):
    t = u.shape[0]
    tm, tc, nj = _ffn_tiles(t)
    tile, before, vec = _ffn_in_specs(tm, tc, nj)

    def body(ug_ref, hg_ref, uv_ref, hv_ref, wg_ref, wv_ref, bg_ref, bv_ref, dz_ref, dc_ref, dw_ref, db_ref):
        i = pl.program_id(1)
        gt = _conv_taps(ug_ref, hg_ref, i, tm)
        vt = _conv_taps(uv_ref, hv_ref, i, tm)
        wg, wv = wg_ref[...], wv_ref[...]
        gate = bg_ref[...] + wg[0:1] * gt[0] + wg[1:2] * gt[1] + wg[2:3] * gt[2]
        val = bv_ref[...] + wv[0:1] * vt[0] + wv[1:2] * vt[1] + wv[2:3] * vt[2]
        dz = dz_ref[...].astype(F32)
        sg = _sigmoid(gate)
        dgate = dz * val * (sg * (1.0 + gate * (1.0 - sg)))
        dval = dz * (gate * sg)
        dc_ref[0] = dgate
        dc_ref[1] = dval

        @pl.when(i == 0)
        def _():
            dw_ref[...] = jnp.zeros_like(dw_ref)
            db_ref[...] = jnp.zeros_like(db_ref)

        for half, (d, taps) in enumerate(((dgate, gt), (dval, vt))):
            for k, tap in enumerate(taps):
                dw_ref[half, k : k + 1, :] += jnp.sum(d * tap, axis=0, keepdims=True)
            db_ref[half] += jnp.sum(d, axis=0, keepdims=True)

    return pl.pallas_call(
        body,
        name="ffn_act_bwd",
        grid=(nj, t // tm),
        in_specs=[tile(0), before(0), tile(nj), before(nj), vec(3, 0), vec(3, nj), vec(1, 0), vec(1, nj), tile(0)],
        out_specs=[
            pl.BlockSpec((2, tm, tc), lambda j, i: (0, i, j)),
            pl.BlockSpec((2, 3, tc), lambda j, i: (0, 0, j)),
            pl.BlockSpec((2, 1, tc), lambda j, i: (0, 0, j)),
        ],
        out_shape=[
            jax.ShapeDtypeStruct((2, t, D_FF), F32),
            jax.ShapeDtypeStruct((2, 3, D_FF), F32),
            jax.ShapeDtypeStruct((2, 1, D_FF), F32),
        ],
        compiler_params=_params("parallel", "arbitrary"),
    )(u, u, u, u, conv_w, conv_w, conv_b, conv_b, dz)


def _conv_transpose(dc, conv_w):
    t = dc.shape[1]
    tm, tc, nj = _ffn_tiles(t)
    nt = t // tm
    rows = 8

    def body(d_ref, h_ref, w_ref, o_ref):
        i = pl.program_id(2)
        d = d_ref[...]
        halo = jnp.where(i < nt - 1, h_ref[...], 0.0)
        w = w_ref[...]
        o_ref[...] = (w[2:3] * d + w[1:2] * _shift_up(d, halo, 1, tm) + w[0:1] * _shift_up(d, halo, 2, tm)).astype(BF16)

    return pl.pallas_call(
        body,
        name="conv_transpose",
        grid=(2, nj, nt),
        in_specs=[
            pl.BlockSpec((None, tm, tc), lambda s, j, i: (s, i, j)),
            pl.BlockSpec((None, rows, tc), lambda s, j, i: (s, jnp.minimum((i + 1) * (tm // rows), t // rows - 1), j)),
            pl.BlockSpec((3, tc), lambda s, j, i: (0, s * nj + j)),
        ],
        out_specs=pl.BlockSpec((tm, tc), lambda s, j, i: (i, s * nj + j)),
        out_shape=jax.ShapeDtypeStruct((t, 2 * D_FF), BF16),
        compiler_params=_params("parallel", "parallel", "parallel"),
    )(dc, dc, conv_w)


def _ple_loss(gpre, pp, h2, final_norm, target):
    t, d = h2.shape
    tm = _tile(t, (256,))

    def body(gp_ref, pp_ref, h_ref, g_ref, tg_ref, dh_ref, dgp_ref, dpp_ref, dg_ref, loss_ref):
        i = pl.program_id(0)
        gate = _sigmoid(gp_ref[...])
        ppv = pp_ref[...]
        h3 = h_ref[...] + gate * ppv
        r = lax.rsqrt(jnp.mean(h3 * h3, axis=-1, keepdims=True) + EPS)
        n = h3 * r
        g = g_ref[...]
        err = n * g - tg_ref[...]
        loss = 0.5 * jnp.sum(jnp.mean(err * err, axis=-1, keepdims=True))
        dy = err * (1.0 / d)
        dn = dy * g
        dh = r * (dn - n * jnp.mean(dn * n, axis=-1, keepdims=True))
        dh_ref[...] = dh
        dgp_ref[...] = (dh * ppv * gate * (1.0 - gate)).astype(BF16)
        dpp_ref[...] = (dh * gate).astype(BF16)
        dg = jnp.sum(dy * n, axis=0, keepdims=True)

        @pl.when(i == 0)
        def _():
            dg_ref[...] = dg
            loss_ref[...] = jnp.full(loss_ref.shape, loss, F32)

        @pl.when(i > 0)
        def _():
            dg_ref[...] += dg
            loss_ref[...] += loss

    row = pl.BlockSpec((tm, d), lambda i: (i, 0))
    vec = pl.BlockSpec((1, d), lambda i: (0, 0))
    return pl.pallas_call(
        body,
        name="ple_loss",
        grid=(t // tm,),
        in_specs=[row, row, row, vec, row],
        out_specs=[row, row, row, vec, pl.BlockSpec((8, 128), lambda i: (0, 0))],
        out_shape=[
            jax.ShapeDtypeStruct((t, d), F32),
            jax.ShapeDtypeStruct((t, d), BF16),
            jax.ShapeDtypeStruct((t, d), BF16),
            jax.ShapeDtypeStruct((1, d), F32),
            jax.ShapeDtypeStruct((8, 128), F32),
        ],
        compiler_params=_params("arbitrary"),
    )(gpre, pp, h2, final_norm, target)


def _adamw(w, g, m, v):
    m = ADAM_B1 * m + (1.0 - ADAM_B1) * g
    v = ADAM_B2 * v + (1.0 - ADAM_B2) * (g * g)
    m_hat = m / (1.0 - ADAM_B1 ** ADAM_STEP)
    v_hat = v / (1.0 - ADAM_B2 ** ADAM_STEP)
    return -ADAM_LR * (m_hat / (jnp.sqrt(v_hat) + ADAM_EPS) + ADAM_WD * w), m, v


def _adam_big(w, m, v, own, recv, name):
    r, c = w.shape
    tr = _tile(r, (256, 176))

    def body(w_ref, m_ref, v_ref, own_ref, recv_ref, g_ref, d_ref, nm_ref, nv_ref):
        g = own_ref[...]
        for k in range(3):
            g = g + recv_ref[k].astype(F32)
        g_ref[...] = g
        d_ref[...], nm_ref[...], nv_ref[...] = _adamw(w_ref[...], g, m_ref[...], v_ref[...])

    blk = pl.BlockSpec((tr, c), lambda i: (i, 0))
    return pl.pallas_call(
        body,
        name=name,
        grid=(r // tr,),
        in_specs=[blk, blk, blk, blk, pl.BlockSpec((3, tr, c), lambda i: (0, i, 0))],
        out_specs=[blk] * 4,
        out_shape=[jax.ShapeDtypeStruct((r, c), F32)] * 4,
        compiler_params=_params("parallel"),
    )(w, m, v, own, recv)


def _adam_small(w, g, m, v):
    def body(w_ref, g_ref, m_ref, v_ref, d_ref, nm_ref, nv_ref):
        d_ref[...], nm_ref[...], nv_ref[...] = _adamw(w_ref[...], g_ref[...], m_ref[...], v_ref[...])

    return pl.pallas_call(body, name="adam_small", out_shape=[jax.ShapeDtypeStruct(w.shape, F32)] * 3, compiler_params=_params())(w, g, m, v)


def _cast_bf16(w, name):
    r, c = w.shape
    tr = _tile(r, (256, 176))

    def body(w_ref, o_ref):
        o_ref[...] = w_ref[...].astype(BF16)

    blk = pl.BlockSpec((tr, c), lambda i: (i, 0))
    return pl.pallas_call(
        body, name=name, grid=(r // tr,), in_specs=[blk], out_specs=blk, out_shape=jax.ShapeDtypeStruct((r, c), BF16), compiler_params=_params("parallel")
    )(w)


def _position():
    return lax.axis_index("x"), lax.axis_index("y"), lax.axis_index("c")


def _other_chips(x, y):
    return [(1 - x, y), (x, 1 - y), (1 - x, 1 - y)]


def _shard_of(ref, axis, size, dev):
    start = pl.multiple_of((4 * dev[0] + 2 * dev[1] + dev[2]) * size, 128 if axis == 1 else 16)
    return ref.at[:, pl.ds(start, size)] if axis == 1 else ref.at[pl.ds(start, size), :]


def _all_gather(shards, axes):
    n = len(shards)

    def body(*refs):
        ins, outs = refs[:n], refs[n : 2 * n]
        send_sems, recv_sems, local_sems = refs[2 * n :]
        x, y, c = _position()
        me, sibling = (x, y, c), (x, y, 1 - c)
        chips = _other_chips(x, y)
        firsts, passed, locals_ = [], [], []
        for w in range(n):
            size = shards[w].shape[axes[w]]
            slot = functools.partial(_shard_of, outs[w], axes[w], size)

            def copy(k, block, to, src=None, w=w, slot=slot):
                return pltpu.make_async_remote_copy(
                    src_ref=slot(block) if src is None else src,
                    dst_ref=slot(block),
                    send_sem=send_sems.at[7 * w + k],
                    recv_sem=recv_sems.at[7 * w + k],
                    device_id=to,
                    device_id_type=MESH,
                )

            mine = pltpu.make_async_copy(ins[w], slot(me), local_sems.at[w])
            mine.start()
            locals_.append(mine)
            first = [copy(0, me, sibling, src=ins[w])] + [copy(1 + j, me, (*chip, c), src=ins[w]) for j, chip in enumerate(chips)]
            for cp in first:
                cp.start()
            firsts.append((first, copy))
        for w in range(n):
            first, copy = firsts[w]
            fwd = [copy(4 + j, (*chip, c), sibling) for j, chip in enumerate(chips)]
            for j, chip in enumerate(chips):
                copy(1 + j, (*chip, c), me).wait_recv()
                fwd[j].start()
            passed.append(fwd)
        for w in range(n):
            first, copy = firsts[w]
            copy(0, sibling, me).wait_recv()
            for j, chip in enumerate(chips):
                copy(4 + j, (*chip, 1 - c), me).wait_recv()
            for cp in first + passed[w]:
                cp.wait_send()
            locals_[w].wait()

    def full(s, ax):
        shape = list(s.shape)
        shape[ax] *= N_DEV
        return jax.ShapeDtypeStruct(tuple(shape), s.dtype)

    return pl.pallas_call(
        body,
        name="all_gather_weights",
        in_specs=[ANY] * n,
        out_specs=[ANY] * n,
        out_shape=[full(s, ax) for s, ax in zip(shards, axes)],
        scratch_shapes=[pltpu.SemaphoreType.DMA((7 * n,)), pltpu.SemaphoreType.DMA((7 * n,)), pltpu.SemaphoreType.DMA((n,))],
    )(*shards)


def _exchange_sibling(grads, axes, sizes):
    n = len(grads)

    def body(*refs):
        ins, outs = refs[:n], refs[n : 2 * n]
        send_sems, recv_sems = refs[2 * n :]
        x, y, c = _position()
        copies = []
        for w in range(n):
            for p in range(4):
                copies.append(
                    pltpu.make_async_remote_copy(
                        src_ref=_shard_of(ins[w], axes[w], sizes[w], (p // 2, p % 2, 1 - c)),
                        dst_ref=outs[w].at[p],
                        send_sem=send_sems.at[4 * w + p],
                        recv_sem=recv_sems.at[4 * w + p],
                        device_id=(x, y, 1 - c),
                        device_id_type=MESH,
                    )
                )
        for cp in copies:
            cp.start()
        for cp in copies:
            cp.wait()

    def landing(g, ax, size):
        shape = list(g.shape)
        shape[ax] = size
        return jax.ShapeDtypeStruct((4, *shape), g.dtype)

    return pl.pallas_call(
        body,
        name="exchange_sibling",
        in_specs=[ANY] * n,
        out_specs=[ANY] * n,
        out_shape=[landing(g, ax, s) for g, ax, s in zip(grads, axes, sizes)],
        scratch_shapes=[pltpu.SemaphoreType.DMA((4 * n,)), pltpu.SemaphoreType.DMA((4 * n,))],
    )(*grads)


def _exchange_chips(parts):
    n = len(parts)

    def body(*refs):
        ins, outs = refs[:n], refs[n : 2 * n]
        send_sems, recv_sems = refs[2 * n :]
        x, y, c = _position()
        copies = []
        for w in range(n):
            for j, chip in enumerate(_other_chips(x, y)):
                copies.append(
                    pltpu.make_async_remote_copy(
                        src_ref=ins[w].at[j],
                        dst_ref=outs[w].at[j],
                        send_sem=send_sems.at[3 * w + j],
                        recv_sem=recv_sems.at[3 * w + j],
                        device_id=(*chip, c),
                        device_id_type=MESH,
                    )
                )
        for cp in copies:
            cp.start()
        for cp in copies:
            cp.wait()

    return pl.pallas_call(
        body,
        name="exchange_chips",
        in_specs=[ANY] * n,
        out_specs=[ANY] * n,
        out_shape=[jax.ShapeDtypeStruct(p.shape, p.dtype) for p in parts],
        scratch_shapes=[pltpu.SemaphoreType.DMA((3 * n,)), pltpu.SemaphoreType.DMA((3 * n,))],
    )(*parts)


def _add_blocks(ids, grad, landed, axis, size, targets, out_dtype, name):
    rows = size if axis == 0 else grad.shape[0]
    cols = size if axis == 1 else grad.shape[1]
    tr = _tile(rows, (256, 176))
    nr = rows // tr
    nt = len(targets)

    def body(ids_ref, g_ref, l_ref, o_ref):
        o_ref[...] = (g_ref[...] + l_ref[...]).astype(out_dtype)

    if axis == 1:
        g_spec = pl.BlockSpec((tr, cols), lambda k, i, ids: (i, ids[targets[0] + k]))
    else:
        g_spec = pl.BlockSpec((tr, cols), lambda k, i, ids: (ids[targets[0] + k] * nr + i, 0))
    return pl.pallas_call(
        body,
        name=name,
        grid_spec=pltpu.PrefetchScalarGridSpec(
            num_scalar_prefetch=1,
            grid=(nt, nr),
            in_specs=[g_spec, pl.BlockSpec((None, tr, cols), lambda k, i, ids: (ids[4 + targets[0] + k], i, 0))],
            out_specs=pl.BlockSpec((None, tr, cols), lambda k, i, ids: (k, i, 0)),
        ),
        out_shape=jax.ShapeDtypeStruct((nt, rows, cols), out_dtype),
        compiler_params=_params("parallel", "parallel"),
    )(ids, grad, landed)


def _all_reduce_small(vec):
    rows = vec.shape[0]

    def body(v_ref, o_ref, land, send_sems, recv_sems):
        x, y, c = _position()
        mine = 4 * x + 2 * y + c
        copies = []
        for mask in range(1, N_DEV):
            peer = (1 - x if mask & 4 else x, 1 - y if mask & 2 else y, 1 - c if mask & 1 else c)
            copies.append(
                pltpu.make_async_remote_copy(
                    src_ref=v_ref, dst_ref=land.at[mine], send_sem=send_sems.at[mask - 1], recv_sem=recv_sems.at[mask - 1], device_id=peer, device_id_type=MESH
                )
            )
        for cp in copies:
            cp.start()
        land[mine] = v_ref[...]
        for cp in copies:
            cp.wait()
        acc = land[0]
        for k in range(1, N_DEV):
            acc = acc + land[k]
        o_ref[...] = acc

    return pl.pallas_call(
        body,
        name="all_reduce_small",
        out_shape=jax.ShapeDtypeStruct(vec.shape, F32),
        in_specs=[pl.BlockSpec(memory_space=pltpu.VMEM)],
        out_specs=pl.BlockSpec(memory_space=pltpu.VMEM),
        scratch_shapes=[pltpu.VMEM((N_DEV, rows, 128), F32), pltpu.SemaphoreType.DMA((N_DEV - 1,)), pltpu.SemaphoreType.DMA((N_DEV - 1,))],
    )(vec)


def _rows128(a, rows):
    flat = a.reshape(-1)
    return jnp.pad(flat, (0, rows * 128 - flat.shape[0])).reshape(rows, 128)


def _pad_rel(a):
    return jnp.pad(a.reshape(ATT_HEADS, -1)[:, :N_REL], ((0, 0), (0, N_REL_PAD - N_REL)))


SMALL = [("norm_mix", 16), ("lb_logits", 16), ("hg_norm", 8), ("rel_bias", 24), ("norm_ffn", 16), ("conv_b", 88), ("norm_ple", 16), ("final_norm", 16)]
CONV_W_FULL_ROWS = 3 * 2 * D_FF // 128
CONV_W_SHARD_ROWS = 40


def _pack_small(parts):
    return jnp.concatenate([_rows128(_pad_rel(parts[k]) if k == "rel_bias" else parts[k], rows) for k, rows in SMALL], axis=0)


def _unpack_small(packed, shapes):
    out, at = {}, 0
    for k, rows in SMALL:
        blk = packed[at : at + rows]
        at += rows
        if k == "rel_bias":
            out[k] = blk.reshape(ATT_HEADS, N_REL_PAD)[:, :N_REL].reshape(shapes[k])
        else:
            n = 1
            for s in shapes[k]:
                n *= s
            out[k] = blk.reshape(-1)[:n].reshape(shapes[k])
    return out, at


BIG = [("w_in", 1), ("w_out", 0), ("w_up", 1), ("w_down", 0), ("w_ple_gate", 0), ("w_ple_proj", 1)]


def _local_step(x, p, target, small, conv_w, wfull):
    w_in, w_out, w_up, w_down, w_pg, w_pp = (wfull[k] for k, _ in BIG)
    a1, r1 = _rms_fwd(x, small["norm_mix"], "norm_mix_fwd")
    proj = _matmul(a1, w_in, "nn", F32, "in_proj")
    bias = _bias_table(small["rel_bias"])
    y_hg, o_hg, states = _hgrn_fwd(proj, small["lb_logits"], small["hg_norm"])
    y_att = _att_fwd(proj, bias)
    ycat = jnp.concatenate([y_hg, y_att], axis=1)
    h1 = _matmul(ycat, w_out, "nn", F32, "out_proj", resid=x)
    a2, r2 = _rms_fwd(h1, small["norm_ffn"], "norm_ffn_fwd")
    u = _matmul(a2, w_up, "nn", BF16, "up_proj")
    z = _ffn_act_fwd(u, conv_w, small["conv_b"])
    h2 = _matmul(z, w_down, "nn", F32, "down_proj", tk=2816, resid=h1)
    a3, r3 = _rms_fwd(h2, small["norm_ple"], "norm_ple_fwd")
    gpre = _matmul(a3, w_pg, "nn", F32, "ple_gate")
    pp = _matmul(p, w_pp, "nn", F32, "ple_proj")
    dh3, dgpre, dpp, d_final, loss = _ple_loss(gpre, pp, h2, small["final_norm"], target)

    grads = {}
    grads["w_ple_proj"] = _matmul(p, dpp, "tn", F32, "d_w_ple_proj", tk=2048)
    grads["w_ple_gate"] = _matmul(a3, dgpre, "tn", F32, "d_w_ple_gate", tk=2048)
    da3 = _matmul(dgpre, w_pg, "nt", F32, "d_norm_ple_out")
    dh2, d_ple = _rms_bwd(da3, h2, r3, small["norm_ple"], dh3, "norm_ple_bwd")
    dz = _matmul(dh2, w_down, "nt", BF16, "d_ffn_act")
    grads["w_down"] = _matmul(z, dh2, "tn", F32, "d_w_down", tk=2048)
    dc, dcw, dcb = _ffn_act_bwd(u, dz, conv_w, small["conv_b"])
    du = _conv_transpose(dc, conv_w)
    d_conv_w = jnp.concatenate([dcw[0], dcw[1]], axis=1)
    d_conv_b = jnp.concatenate([dcb[0], dcb[1]], axis=1)
    grads["w_up"] = _matmul(a2, du, "tn", F32, "d_w_up", tk=2048)
    da2 = _matmul(du, w_up, "nt", F32, "d_norm_ffn_out", tk=2816)
    dh1, d_ffn = _rms_bwd(da2, h1, r2, small["norm_ffn"], dh2, "norm_ffn_bwd")
    dycat = _matmul(dh1, w_out, "nt", F32, "d_mix_out")
    grads["w_out"] = _matmul(ycat, dh1, "tn", F32, "d_w_out", tk=2048)
    dp_hg, d_lb, d_hgn = _hgrn_bwd(proj, small["lb_logits"], small["hg_norm"], o_hg, dycat, states)
    dq_att, dk_att, dv_att, gsum = _att_bwd(proj, bias, dycat)
    d_rel = _rel_bias_grad(gsum)
    dproj = jnp.concatenate([dp_hg[0], dp_hg[1], dp_hg[2], dp_hg[3], dq_att, dk_att, dv_att], axis=1)
    grads["w_in"] = _matmul(a1, dproj, "tn", F32, "d_w_in", tk=2048)
    da1 = _matmul(dproj, w_in, "nt", F32, "d_norm_mix_out", tk=1792)
    dx, d_mix = _rms_bwd(da1, x, r1, small["norm_mix"], dh1, "norm_mix_bwd")
    d_small = {
        "norm_mix": d_mix, "lb_logits": d_lb, "hg_norm": d_hgn, "rel_bias": d_rel, "norm_ffn": d_ffn,
        "conv_b": d_conv_b, "norm_ple": d_ple, "final_norm": d_final,
    }
    return loss, dx, grads, d_small, d_conv_w


def kernel(x, p, norm_mix, w_in, lb_logits, hg_norm, rel_bias, w_out, norm_ffn, w_up, conv_w, conv_b, w_down, norm_ple, w_ple_gate, w_ple_proj, final_norm, loss_target, m_norm_mix, m_w_in, m_lb_logits, m_hg_norm, m_rel_bias, m_w_out, m_norm_ffn, m_w_up, m_conv_w, m_conv_b, m_w_down, m_norm_ple, m_w_ple_gate, m_w_ple_proj, m_final_norm, v_norm_mix, v_w_in, v_lb_logits, v_hg_norm, v_rel_bias, v_w_out, v_norm_ffn, v_w_up, v_conv_w, v_conv_b, v_w_down, v_norm_ple, v_w_ple_gate, v_w_ple_proj, v_final_norm):
    given = dict(locals())
    mx, my, mc = _position()
    me = 4 * mx + 2 * my + mc
    big = {k: given[k][0] for k, _ in BIG}
    axes = [ax for _, ax in BIG]
    sizes = [big[k].shape[ax] for k, ax in BIG]

    shards = [_cast_bf16(big[k], "cast_" + k) for k, _ in BIG] + [conv_w[0]]
    gathered = _all_gather(shards, axes + [1])
    wfull = {k: g for (k, _), g in zip(BIG, gathered)}
    conv_w_full = gathered[-1]

    small = {
        "norm_mix": norm_mix, "lb_logits": lb_logits, "hg_norm": hg_norm, "rel_bias": rel_bias[0], "norm_ffn": norm_ffn,
        "conv_b": conv_b, "norm_ple": norm_ple, "final_norm": final_norm.reshape(1, -1),
    }
    loss, dx, grads, d_small, d_conv_w = _local_step(x[0], p[0, 0], loss_target[0], small, conv_w_full, wfull)

    glist = [grads[k] for k, _ in BIG]
    landed = _exchange_sibling(glist, axes, sizes)
    chips = [(mx, my), (1 - mx, my), (mx, 1 - my), (1 - mx, 1 - my)]
    ids = jnp.stack([4 * cx + 2 * cy + mc for cx, cy in chips] + [2 * cx + cy for cx, cy in chips]).astype(jnp.int32)
    own = [_add_blocks(ids, g, l, ax, s, [0], F32, "add_own_" + k)[0] for (k, ax), g, l, s in zip(BIG, glist, landed, sizes)]
    send = [_add_blocks(ids, g, l, ax, s, [1, 2, 3], BF16, "add_send_" + k) for (k, ax), g, l, s in zip(BIG, glist, landed, sizes)]
    recv = _exchange_chips(send)
    out = {}
    for (k, _), o, r in zip(BIG, own, recv):
        g, d, nm, nv = _adam_big(big[k], given["m_" + k][0], given["v_" + k][0], o, r, "adam_" + k)
        out[k] = tuple(a[None] for a in (g, d, nm, nv))

    packed = jnp.concatenate([_pack_small(d_small), _rows128(d_conv_w, CONV_W_FULL_ROWS), _rows128(loss[0:1, 0:1], 8)], axis=0)
    reduced = _all_reduce_small(packed)
    shapes = {k: given[k].shape for k, _ in SMALL}
    g_small, at = _unpack_small(reduced, shapes)
    g_conv_full = reduced[at : at + CONV_W_FULL_ROWS].reshape(3, 2 * D_FF)
    total_loss = reduced[at + CONV_W_FULL_ROWS, 0]
    cw = conv_w.shape[2]
    g_conv = lax.dynamic_slice_in_dim(g_conv_full, me * cw, cw, axis=1)

    def pack_with_conv(parts, conv_part):
        return jnp.concatenate([_pack_small(parts), _rows128(conv_part, CONV_W_SHARD_ROWS)], axis=0)

    d_pk, m_pk, v_pk = _adam_small(
        pack_with_conv({k: given[k] for k, _ in SMALL}, conv_w),
        pack_with_conv(g_small, g_conv),
        pack_with_conv({k: given["m_" + k] for k, _ in SMALL}, m_conv_w),
        pack_with_conv({k: given["v_" + k] for k, _ in SMALL}, v_conv_w),
    )
    for name, pk in (("d", d_pk), ("m", m_pk), ("v", v_pk)):
        parts, at = _unpack_small(pk, shapes)
        parts["conv_w"] = pk[at : at + CONV_W_SHARD_ROWS].reshape(-1)[: 3 * cw].reshape(conv_w.shape)
        for k, a in parts.items():
            out.setdefault(k, {})
            out[k][name] = a
    for k, _ in SMALL:
        out[k]["g"] = g_small[k]
    out["conv_w"]["g"] = g_conv.reshape(conv_w.shape)

    order = ["norm_mix", "w_in", "lb_logits", "hg_norm", "rel_bias", "w_out", "norm_ffn", "w_up", "conv_w", "conv_b", "w_down", "norm_ple", "w_ple_gate", "w_ple_proj", "final_norm"]

    def pick(k, what):
        return out[k][what] if isinstance(out[k], dict) else out[k][{"g": 0, "d": 1, "m": 2, "v": 3}[what]]

    return (total_loss, dx[None], *[pick(k, "g") for k in order], *[pick(k, "d") for k in order], *[pick(k, "m") for k in order], *[pick(k, "v") for k in order])
```

```python
import functools

import jax
import jax.numpy as jnp
from jax import lax
from jax.experimental import pallas as pl
from jax.experimental.pallas import tpu as pltpu

F32 = jnp.float32
BF16 = jnp.bfloat16

D_MODEL = 2048
CHUNK = 64
HG_HEADS = 8
HEAD_DIM = 128
HG_WIDTH = HG_HEADS * HEAD_DIM
ATT_HEADS = 8
ATT_WIDTH = ATT_HEADS * HEAD_DIM
LEFT_CHUNKS = 8
PAD = LEFT_CHUNKS * CHUNK
BAND = PAD + CHUNK
REL_CLIP = 128
N_REL = 2 * REL_CLIP + 1
N_REL_PAD = 384
D_FF = 5632
EPS = 1e-6
ATT_SCALE = HEAD_DIM ** -0.5
SUB = 16

ADAM_LR = 0.001
ADAM_B1 = 0.9
ADAM_B2 = 0.999
ADAM_EPS = 1e-08
ADAM_WD = 0.01
ADAM_STEP = 10

N_DEV = 8
VMEM_LIMIT = 48 * 1024 * 1024
MESH = pl.DeviceIdType.MESH
ANY = pl.BlockSpec(memory_space=pl.ANY)
HIGHEST = lax.Precision.HIGHEST

NN = (((1,), (0,)), ((), ()))
NT = (((1,), (1,)), ((), ()))
TN = (((0,), (0,)), ((), ()))


def _params(*sem):
    return pltpu.CompilerParams(dimension_semantics=sem if sem else None, vmem_limit_bytes=VMEM_LIMIT)


def _pallas(body, n_in, dep, **kw):
    if dep is None:
        return pl.pallas_call(body, **kw)

    def body_after(*refs):
        body(*refs[:n_in], *refs[n_in + 1 :])

    call = pl.pallas_call(body_after, **dict(kw, in_specs=list(kw["in_specs"]) + [ANY]))
    return lambda *ops: call(*ops, dep)


def _dot(a, b, dims=NN):
    return lax.dot_general(a, b, dims, preferred_element_type=F32)


def _dot3(a, b, dims=NN):
    a_hi, b_hi = a.astype(BF16), b.astype(BF16)
    a_lo, b_lo = (a - a_hi.astype(F32)).astype(BF16), (b - b_hi.astype(F32)).astype(BF16)
    return _dot(a_hi, b_hi, dims) + (_dot(a_hi, b_lo, dims) + _dot(a_lo, b_hi, dims))


def _sigmoid(x):
    return 1.0 / (1.0 + jnp.exp(-x))


def _tile(n, prefs):
    for t in prefs:
        if n % t == 0:
            return t
    return n


def _matmul(a, b, mode, out_dtype, name, tm=512, tn=1024, tk=None, resid=None, dep=None):
    if mode == "nn":
        (m, k), n = a.shape, b.shape[1]
    elif mode == "nt":
        (m, k), n = a.shape, b.shape[0]
    else:
        (k, m), n = a.shape, b.shape[1]
    tm = _tile(m, (tm, 256, 128))
    tn = _tile(n, (tn, 512, 256, 128))
    tk = k if tk is None else _tile(k, (tk,))
    nk = k // tk
    dims = {"nn": NN, "nt": NT, "tn": TN}[mode]
    a_spec = pl.BlockSpec((tk, tm), lambda i, j, s: (s, i)) if mode == "tn" else pl.BlockSpec((tm, tk), lambda i, j, s: (i, s))
    b_spec = pl.BlockSpec((tn, tk), lambda i, j, s: (j, s)) if mode == "nt" else pl.BlockSpec((tk, tn), lambda i, j, s: (s, j))
    o_spec = pl.BlockSpec((tm, tn), lambda i, j, s: (i, j))
    has_res = resid is not None

    def body(*refs):
        a_ref, b_ref = refs[0], refs[1]
        o_ref = refs[2 + has_res]
        part = _dot(a_ref[...].astype(BF16), b_ref[...].astype(BF16), dims)

        def finish(acc):
            if has_res:
                acc = acc + refs[2][...]
            o_ref[...] = acc.astype(out_dtype)

        if nk == 1:
            finish(part)
        else:
            acc_ref = refs[-1]
            s = pl.program_id(2)

            @pl.when(s == 0)
            def _():
                acc_ref[...] = part

            @pl.when(s > 0)
            def _():
                acc_ref[...] += part

            @pl.when(s == nk - 1)
            def _():
                finish(acc_ref[...])

    return _pallas(
        body,
        2 + has_res,
        dep,
        name=name,
        grid=(m // tm, n // tn, nk),
        in_specs=[a_spec, b_spec] + ([o_spec] if has_res else []),
        out_specs=o_spec,
        out_shape=jax.ShapeDtypeStruct((m, n), out_dtype),
        scratch_shapes=[pltpu.VMEM((tm, tn), F32)] if nk > 1 else [],
        compiler_params=_params("parallel", "parallel", "arbitrary"),
    )(*([a, b] + ([resid] if has_res else [])))


def _rms_fwd(x, g, name, dep=None):
    t, d = x.shape
    tm = _tile(t, (256,))

    def body(x_ref, g_ref, a_ref, r_ref):
        xv = x_ref[...]
        r = lax.rsqrt(jnp.mean(xv * xv, axis=-1, keepdims=True) + EPS)
        a_ref[...] = (xv * r * g_ref[...]).astype(BF16)
        r_ref[...] = r

    row = pl.BlockSpec((tm, d), lambda i: (i, 0))
    return _pallas(
        body,
        2,
        dep,
        name=name,
        grid=(t // tm,),
        in_specs=[row, pl.BlockSpec((1, d), lambda i: (0, 0))],
        out_specs=[row, pl.BlockSpec((tm, 1), lambda i: (i, 0))],
        out_shape=[jax.ShapeDtypeStruct((t, d), BF16), jax.ShapeDtypeStruct((t, 1), F32)],
        compiler_params=_params("parallel"),
    )(x, g)


def _rms_bwd(da, x, r, g, resid, name, dep=None):
    t, d = x.shape
    tm = _tile(t, (256,))

    def body(da_ref, x_ref, r_ref, g_ref, res_ref, dx_ref, dg_ref):
        i = pl.program_id(0)
        rv = r_ref[...]
        n = x_ref[...] * rv
        dav = da_ref[...]
        dn = dav * g_ref[...]
        dx_ref[...] = rv * (dn - n * jnp.mean(dn * n, axis=-1, keepdims=True)) + res_ref[...]
        part = jnp.sum(dav * n, axis=0, keepdims=True)

        @pl.when(i == 0)
        def _():
            dg_ref[...] = part

        @pl.when(i > 0)
        def _():
            dg_ref[...] += part

    row = pl.BlockSpec((tm, d), lambda i: (i, 0))
    vec = pl.BlockSpec((1, d), lambda i: (0, 0))
    return _pallas(
        body,
        5,
        dep,
        name=name,
        grid=(t // tm,),
        in_specs=[row, row, pl.BlockSpec((tm, 1), lambda i: (i, 0)), vec, row],
        out_specs=[row, vec],
        out_shape=[jax.ShapeDtypeStruct((t, d), F32), jax.ShapeDtypeStruct((1, d), F32)],
        compiler_params=_params("arbitrary"),
    )(da, x, r, g, resid)


def _tri(n, upper):
    r = lax.broadcasted_iota(jnp.int32, (n, n), 0)
    c = lax.broadcasted_iota(jnp.int32, (n, n), 1)
    return jnp.where((c >= r) if upper else (c <= r), 1.0, 0.0).astype(F32)


def _hgrn_gates(q, fp, lbl):
    l0, l1 = lbl[0:1, :], lbl[1:2, :]
    mx = jnp.maximum(l0, l1)
    e0, e1 = jnp.exp(l0 - mx), jnp.exp(l1 - mx)
    lb = e0 / (e0 + e1)
    sig = _sigmoid(fp)
    f = lb + (1.0 - lb) * sig
    kk = (1.0 - lb) * _sigmoid(-fp)
    sq = _sigmoid(q)
    b = jnp.dot(_tri(CHUNK, False), jnp.log(f), precision=HIGHEST, preferred_element_type=F32)
    return lb, sig, f, kk, sq, q * sq, b


def _intra_blocks(b):
    out = []
    for lo in range(0, CHUNK, SUB):
        hi = lo + SUB
        br = b[lo + SUB // 2 : lo + SUB // 2 + 1, :]
        row = lax.broadcasted_iota(jnp.int32, (SUB, hi), 0) + lo
        col = lax.broadcasted_iota(jnp.int32, (SUB, hi), 1)
        out.append((lo, hi, jnp.exp(b[lo:hi] - br), jnp.exp(br - b[:hi]), col <= row))
    return out


def _hgrn_fwd(proj, lb_logits, hg_norm):
    t = proj.shape[0]
    nc = t // CHUNK

    def body(q_ref, f_ref, i_ref, g_ref, lbl_ref, hgn_ref, y_ref, o_ref, st_ref, s_scr):
        c = pl.program_id(1)

        @pl.when(c == 0)
        def _():
            s_scr[...] = jnp.zeros_like(s_scr)

        st = s_scr[...]
        st_ref[...] = st
        _, _, _, kk, _, qf, b = _hgrn_gates(q_ref[...], f_ref[...], lbl_ref[...])
        vb = i_ref[...].astype(BF16)
        bl = b[CHUNK - 1 : CHUNK, :]
        o = _dot((qf * jnp.exp(b)).astype(BF16), st.astype(BF16), NT)
        parts = []
        for lo, hi, ea, eb, mask in _intra_blocks(b):
            p = jnp.where(mask, _dot((qf[lo:hi] * ea).astype(BF16), (kk[:hi] * eb).astype(BF16), NT), 0.0)
            parts.append(_dot(p.astype(BF16), vb[:hi]))
        o = o + jnp.concatenate(parts, axis=0)
        s_scr[...] = st * jnp.exp(bl) + _dot(vb, (kk * jnp.exp(bl - b)).astype(BF16), TN)
        o_ref[...] = o
        r = lax.rsqrt(jnp.mean(o * o, axis=-1, keepdims=True) + EPS)
        gg = g_ref[...]
        y_ref[...] = ((o * r * hgn_ref[...]) * (gg * _sigmoid(gg))).astype(BF16)

    def col(k):
        return pl.BlockSpec((CHUNK, HEAD_DIM), lambda h, c: (c, k * HG_HEADS + h))

    out = pl.BlockSpec((CHUNK, HEAD_DIM), lambda h, c: (c, h))
    return pl.pallas_call(
        body,
        name="hgrn_fwd",
        grid=(HG_HEADS, nc),
        in_specs=[col(0), col(1), col(2), col(3), pl.BlockSpec((2, HEAD_DIM), lambda h, c: (0, h)), pl.BlockSpec((1, HEAD_DIM), lambda h, c: (0, 0))],
        out_specs=[out, out, pl.BlockSpec((None, None, HEAD_DIM, HEAD_DIM), lambda h, c: (h, c, 0, 0))],
        out_shape=[
            jax.ShapeDtypeStruct((t, HG_WIDTH), BF16),
            jax.ShapeDtypeStruct((t, HG_WIDTH), F32),
            jax.ShapeDtypeStruct((HG_HEADS, nc, HEAD_DIM, HEAD_DIM), F32),
        ],
        scratch_shapes=[pltpu.VMEM((HEAD_DIM, HEAD_DIM), F32)],
        compiler_params=_params("arbitrary", "arbitrary"),
    )(proj, proj, proj, proj, lb_logits, hg_norm)


def _hgrn_bwd(proj, lb_logits, hg_norm, o_hg, dycat, states, dep=None):
    t = proj.shape[0]
    nc = t // CHUNK

    def body(q_ref, f_ref, i_ref, g_ref, lbl_ref, hgn_ref, o_ref, dy_ref, st_ref, dp_ref, dlbl_ref, dhgn_ref, dst_scr, dlb_scr):
        h = pl.program_id(0)
        c = pl.program_id(1)

        @pl.when(c == 0)
        def _():
            dst_scr[...] = jnp.zeros_like(dst_scr)
            dlb_scr[...] = jnp.zeros_like(dlb_scr)

        @pl.when((c == 0) & (h == 0))
        def _():
            dhgn_ref[...] = jnp.zeros_like(dhgn_ref)

        q, fp, gg = q_ref[...], f_ref[...], g_ref[...]
        lb, sig, f, kk, sq, qf, b = _hgrn_gates(q, fp, lbl_ref[...])
        hgn = hgn_ref[...]
        o, dy = o_ref[...], dy_ref[...]
        sg = _sigmoid(gg)
        r = lax.rsqrt(jnp.mean(o * o, axis=-1, keepdims=True) + EPS)
        n = o * r
        don = dy * (gg * sg)
        dgg = dy * (n * hgn) * (sg * (1.0 + gg * (1.0 - sg)))
        dhgn_ref[...] += jnp.sum(don * n, axis=0, keepdims=True)
        dn = don * hgn
        do = r * (dn - n * jnp.mean(dn * n, axis=-1, keepdims=True))
        dob = do.astype(BF16)
        vi = i_ref[...]
        vb = vi.astype(BF16)
        st, dstn = st_ref[...], dst_scr[...]
        bl = b[CHUNK - 1 : CHUNK, :]
        e_b, e_bl, e_l = jnp.exp(b), jnp.exp(bl - b), jnp.exp(bl)
        dq_acc = _dot3(do, st) * e_b
        dk_inter = _dot3(vi, dstn) * e_bl
        dk_acc = dk_inter
        dv_acc = _dot((kk * e_bl).astype(BF16), dstn.astype(BF16), NT)
        dst_scr[...] = dstn * e_l + _dot(dob, (qf * e_b).astype(BF16), TN)
        db_last = e_l * jnp.sum(st * dstn, axis=0, keepdims=True) + jnp.sum(kk * dk_inter, axis=0, keepdims=True)
        dq_parts = []
        for lo, hi, ea, eb, mask in _intra_blocks(b):
            a, bk = qf[lo:hi] * ea, kk[:hi] * eb
            p = jnp.where(mask, _dot(a.astype(BF16), bk.astype(BF16), NT), 0.0)
            dp = jnp.where(mask, _dot3(do[lo:hi], vi[:hi], NT), 0.0)
            dq_parts.append(_dot3(dp, bk) * ea)
            dki = _dot3(dp, a, TN) * eb
            dvi = _dot(p.astype(BF16), dob[lo:hi], TN)
            if hi < CHUNK:
                zeros = jnp.zeros((CHUNK - hi, HEAD_DIM), F32)
                dki = jnp.concatenate([dki, zeros], axis=0)
                dvi = jnp.concatenate([dvi, zeros], axis=0)
            dk_acc = dk_acc + dki
            dv_acc = dv_acc + dvi
        dq_acc = dq_acc + jnp.concatenate(dq_parts, axis=0)
        rows = lax.broadcasted_iota(jnp.int32, (CHUNK, HEAD_DIM), 0)
        db = qf * dq_acc - kk * dk_acc + jnp.where(rows == CHUNK - 1, db_last, 0.0)
        dlf = jnp.dot(_tri(CHUNK, True), db, precision=HIGHEST, preferred_element_type=F32)
        dfk = dlf / f - dk_acc
        dp_ref[0] = (dq_acc * (sq * (1.0 + q * (1.0 - sq)))).astype(BF16)
        dp_ref[1] = ((1.0 - lb) * dfk * sig * (1.0 - sig)).astype(BF16)
        dp_ref[2] = dv_acc.astype(BF16)
        dp_ref[3] = dgg.astype(BF16)
        dlb_scr[...] += jnp.sum(dfk * (1.0 - sig), axis=0, keepdims=True)

        @pl.when(c == nc - 1)
        def _():
            dl0 = dlb_scr[...] * lb * (1.0 - lb)
            dlbl_ref[0:1, :] = dl0
            dlbl_ref[1:2, :] = -dl0

    def col(k):
        return pl.BlockSpec((CHUNK, HEAD_DIM), lambda h, c: (nc - 1 - c, k * HG_HEADS + h))

    blk = pl.BlockSpec((CHUNK, HEAD_DIM), lambda h, c: (nc - 1 - c, h))
    return _pallas(
        body,
        9,
        dep,
        name="hgrn_bwd",
        grid=(HG_HEADS, nc),
        in_specs=[
            col(0), col(1), col(2), col(3),
            pl.BlockSpec((2, HEAD_DIM), lambda h, c: (0, h)),
            pl.BlockSpec((1, HEAD_DIM), lambda h, c: (0, 0)),
            blk, blk,
            pl.BlockSpec((None, None, HEAD_DIM, HEAD_DIM), lambda h, c: (h, nc - 1 - c, 0, 0)),
        ],
        out_specs=[
            pl.BlockSpec((4, CHUNK, HEAD_DIM), lambda h, c: (0, nc - 1 - c, h)),
            pl.BlockSpec((2, HEAD_DIM), lambda h, c: (0, h)),
            pl.BlockSpec((1, HEAD_DIM), lambda h, c: (0, 0)),
        ],
        out_shape=[
            jax.ShapeDtypeStruct((4, t, HG_WIDTH), BF16),
            jax.ShapeDtypeStruct((2, HG_WIDTH), F32),
            jax.ShapeDtypeStruct((1, HEAD_DIM), F32),
        ],
        scratch_shapes=[pltpu.VMEM((HEAD_DIM, HEAD_DIM), F32), pltpu.VMEM((1, HEAD_DIM), F32)],
        compiler_params=_params("arbitrary", "arbitrary"),
    )(proj, proj, proj, proj, lb_logits, hg_norm, o_hg, dycat, states)


def _rel_index(rows, cols, row0):
    r = lax.broadcasted_iota(jnp.int32, (rows, cols), 0) + row0
    c = lax.broadcasted_iota(jnp.int32, (rows, cols), 1)
    return jnp.clip(r + PAD - c, -REL_CLIP, REL_CLIP) + REL_CLIP


def _bias_table(rel_bias):
    def body(rb_ref, o_ref):
        h = pl.program_id(0)
        idx = _rel_index(CHUNK, BAND, 0)

        def step(i, acc):
            return jnp.where(idx == i, rb_ref[h, i], acc)

        o_ref[...] = lax.fori_loop(0, N_REL, step, jnp.zeros((CHUNK, BAND), F32))

    return pl.pallas_call(
        body,
        name="bias_table",
        grid=(ATT_HEADS,),
        in_specs=[pl.BlockSpec(memory_space=pltpu.SMEM)],
        out_specs=pl.BlockSpec((None, CHUNK, BAND), lambda h: (h, 0, 0)),
        out_shape=jax.ShapeDtypeStruct((ATT_HEADS, CHUNK, BAND), F32),
        compiler_params=_params("parallel"),
    )(rel_bias)


def _att_probs(q_ref, kpad, bias_ref, c):
    qs = (q_ref[...] * ATT_SCALE).astype(BF16)
    start = pl.multiple_of(c * CHUNK, CHUNK)
    kb = kpad[pl.ds(start, BAND), :]
    s = _dot(qs, kb, NT) + bias_ref[...]
    key = lax.broadcasted_iota(jnp.int32, (CHUNK, BAND), 1) + (c * CHUNK - PAD)
    s = jnp.where(key >= 0, s, jnp.finfo(F32).min)
    e = jnp.exp(s - jnp.max(s, axis=-1, keepdims=True))
    return qs, kb, start, e / jnp.sum(e, axis=-1, keepdims=True)


def _fill_padded(dst, src):
    dst[0:PAD, :] = jnp.zeros((PAD, HEAD_DIM), BF16)
    dst[PAD:, :] = src[...].astype(BF16)


def _att_fwd(proj, bias, dep=None):
    t = proj.shape[0]
    nc = t // CHUNK

    def body(q_ref, k_ref, v_ref, bias_ref, y_ref, kpad, vpad):
        c = pl.program_id(1)

        @pl.when(c == 0)
        def _():
            _fill_padded(kpad, k_ref)
            _fill_padded(vpad, v_ref)

        _, _, start, p = _att_probs(q_ref, kpad, bias_ref, c)
        y_ref[...] = _dot(p.astype(BF16), vpad[pl.ds(start, BAND), :]).astype(BF16)

    base = 4 * HG_HEADS
    return _pallas(
        body,
        4,
        dep,
        name="att_fwd",
        grid=(ATT_HEADS, nc),
        in_specs=[
            pl.BlockSpec((CHUNK, HEAD_DIM), lambda h, c: (c, base + h)),
            pl.BlockSpec((t, HEAD_DIM), lambda h, c: (0, base + ATT_HEADS + h)),
            pl.BlockSpec((t, HEAD_DIM), lambda h, c: (0, base + 2 * ATT_HEADS + h)),
            pl.BlockSpec((None, CHUNK, BAND), lambda h, c: (h, 0, 0)),
        ],
        out_specs=pl.BlockSpec((CHUNK, HEAD_DIM), lambda h, c: (c, h)),
        out_shape=jax.ShapeDtypeStruct((t, ATT_WIDTH), BF16),
        scratch_shapes=[pltpu.VMEM((t + PAD, HEAD_DIM), BF16), pltpu.VMEM((t + PAD, HEAD_DIM), BF16)],
        compiler_params=_params("arbitrary", "arbitrary"),
    )(proj, proj, proj, bias)


def _att_bwd(proj, bias, dycat, dep=None):
    t = proj.shape[0]
    nc = t // CHUNK

    def body(q_ref, k_ref, v_ref, bias_ref, dy_ref, dq_ref, dk_ref, dv_ref, g_ref, kpad, vpad, dkacc, dvacc):
        c = pl.program_id(1)

        @pl.when(c == 0)
        def _():
            _fill_padded(kpad, k_ref)
            _fill_padded(vpad, v_ref)
            dkacc[...] = jnp.zeros_like(dkacc)
            dvacc[...] = jnp.zeros_like(dvacc)
            g_ref[...] = jnp.zeros_like(g_ref)

        qs, kb, start, p = _att_probs(q_ref, kpad, bias_ref, c)
        band = pl.ds(start, BAND)
        dyb = dy_ref[...].astype(BF16)
        dvacc[band, :] += _dot(p.astype(BF16), dyb, TN)
        dp = _dot(dyb, vpad[band, :], NT)
        ds = p * (dp - jnp.sum(dp * p, axis=-1, keepdims=True))
        g_ref[...] += ds
        dsb = ds.astype(BF16)
        dq_ref[...] = (_dot(dsb, kb) * ATT_SCALE).astype(BF16)
        dkacc[band, :] += _dot(dsb, qs, TN)

        @pl.when(c == nc - 1)
        def _():
            dk_ref[...] = dkacc[PAD:, :].astype(BF16)
            dv_ref[...] = dvacc[PAD:, :].astype(BF16)

    base = 4 * HG_HEADS
    whole = pl.BlockSpec((t, HEAD_DIM), lambda h, c: (0, h))
    return _pallas(
        body,
        5,
        dep,
        name="att_bwd",
        grid=(ATT_HEADS, nc),
        in_specs=[
            pl.BlockSpec((CHUNK, HEAD_DIM), lambda h, c: (c, base + h)),
            pl.BlockSpec((t, HEAD_DIM), lambda h, c: (0, base + ATT_HEADS + h)),
            pl.BlockSpec((t, HEAD_DIM), lambda h, c: (0, base + 2 * ATT_HEADS + h)),
            pl.BlockSpec((None, CHUNK, BAND), lambda h, c: (h, 0, 0)),
            pl.BlockSpec((CHUNK, HEAD_DIM), lambda h, c: (c, HG_HEADS + h)),
        ],
        out_specs=[pl.BlockSpec((CHUNK, HEAD_DIM), lambda h, c: (c, h)), whole, whole, pl.BlockSpec((None, CHUNK, BAND), lambda h, c: (h, 0, 0))],
        out_shape=[
            jax.ShapeDtypeStruct((t, ATT_WIDTH), BF16),
            jax.ShapeDtypeStruct((t, ATT_WIDTH), BF16),
            jax.ShapeDtypeStruct((t, ATT_WIDTH), BF16),
            jax.ShapeDtypeStruct((ATT_HEADS, CHUNK, BAND), F32),
        ],
        scratch_shapes=[
            pltpu.VMEM((t + PAD, HEAD_DIM), BF16),
            pltpu.VMEM((t + PAD, HEAD_DIM), BF16),
            pltpu.VMEM((t + PAD, HEAD_DIM), F32),
            pltpu.VMEM((t + PAD, HEAD_DIM), F32),
        ],
        compiler_params=_params("arbitrary", "arbitrary"),
    )(proj, proj, proj, bias, dycat)


def _rel_bias_grad(gsum):
    def body(g_ref, o_ref):
        def step(t, acc):
            rows = jnp.concatenate([g_ref[h, pl.ds(t, 1), :] for h in range(ATT_HEADS)], axis=0)
            hi = rows.astype(BF16)
            lo = (rows - hi.astype(F32)).astype(BF16)
            key = lax.broadcasted_iota(jnp.int32, (BAND, N_REL_PAD), 0)
            slot = lax.broadcasted_iota(jnp.int32, (BAND, N_REL_PAD), 1)
            onehot = jnp.where(jnp.clip(t + PAD - key, -REL_CLIP, REL_CLIP) + REL_CLIP == slot, 1.0, 0.0).astype(BF16)
            return acc + _dot(hi, onehot) + _dot(lo, onehot)

        o_ref[...] = lax.fori_loop(0, CHUNK, step, jnp.zeros((ATT_HEADS, N_REL_PAD), F32))

    return pl.pallas_call(
        body,
        name="rel_bias_grad",
        out_shape=jax.ShapeDtypeStruct((ATT_HEADS, N_REL_PAD), F32),
        compiler_params=_params(),
    )(gsum)


HALO = 16


def _ffn_tiles(t):
    tm = _tile(t, (512,))
    tc = 256
    return tm, tc, D_FF // tc


def _shift_down(x, halo, k, tm):
    rows = lax.broadcasted_iota(jnp.int32, x.shape, 0)
    out = jnp.where(rows >= k, pltpu.roll(x, k, 0), halo[HALO - 1 : HALO, :])
    if k == 2:
        out = jnp.where(rows == 0, halo[HALO - 2 : HALO - 1, :], out)
    return out


def _shift_up(x, halo, k, tm):
    rows = lax.broadcasted_iota(jnp.int32, x.shape, 0)
    out = jnp.where(rows < tm - k, pltpu.roll(x, tm - k, 0), halo[0:1, :])
    if k == 2:
        out = jnp.where(rows == tm - 1, halo[1:2, :], out)
    return out


def _conv_taps(u_ref, halo_ref, i, tm):
    u = u_ref[...].astype(F32)
    halo = jnp.where(i > 0, halo_ref[...].astype(F32), 0.0)
    return _shift_down(u, halo, 2, tm), _shift_down(u, halo, 1, tm), u


def _ffn_in_specs(tm, tc, nj):
    before = lambda off: pl.BlockSpec((HALO, tc), lambda j, i: (jnp.maximum(i * (tm // HALO) - 1, 0), off + j))
    tile = lambda off: pl.BlockSpec((tm, tc), lambda j, i: (i, off + j))
    vec = lambda rows, off: pl.BlockSpec((rows, tc), lambda j, i: (0, off + j))
    return tile, before, vec


def _ffn_act_fwd(u, conv_w, conv_b):
    t = u.shape[0]
    tm, tc, nj = _ffn_tiles(t)
    tile, before, vec = _ffn_in_specs(tm, tc, nj)

    def body(ug_ref, hg_ref, uv_ref, hv_ref, wg_ref, wv_ref, bg_ref, bv_ref, z_ref):
        i = pl.program_id(1)
        g2, g1, g0 = _conv_taps(ug_ref, hg_ref, i, tm)
        v2, v1, v0 = _conv_taps(uv_ref, hv_ref, i, tm)
        wg, wv = wg_ref[...], wv_ref[...]
        gate = bg_ref[...] + wg[0:1] * g2 + wg[1:2] * g1 + wg[2:3] * g0
        val = bv_ref[...] + wv[0:1] * v2 + wv[1:2] * v1 + wv[2:3] * v0
        z_ref[...] = (gate * _sigmoid(gate) * val).astype(BF16)

    return pl.pallas_call(
        body,
        name="ffn_act_fwd",
        grid=(nj, t // tm),
        in_specs=[tile(0), before(0), tile(nj), before(nj), vec(3, 0), vec(3, nj), vec(1, 0), vec(1, nj)],
        out_specs=pl.BlockSpec((tm, tc), lambda j, i: (i, j)),
        out_shape=jax.ShapeDtypeStruct((t, D_FF), BF16),
        compiler_params=_params("parallel", "parallel"),
    )(u, u, u, u, conv_w, conv_w, conv_b, conv_b)


def _ffn_act_bwd(u, dz, conv_w, conv_b, dep=None):
    t = u.shape[0]
    tm, tc, nj = _ffn_tiles(t)
    tile, before, vec = _ffn_in_specs(tm, tc, nj)

    def body(ug_ref, hg_ref, uv_ref, hv_ref, wg_ref, wv_ref, bg_ref, bv_ref, dz_ref, dc_ref, dw_ref, db_ref):
        i = pl.program_id(1)
        gt = _conv_taps(ug_ref, hg_ref, i, tm)
        vt = _conv_taps(uv_ref, hv_ref, i, tm)
        wg, wv = wg_ref[...], wv_ref[...]
        gate = bg_ref[...] + wg[0:1] * gt[0] + wg[1:2] * gt[1] + wg[2:3] * gt[2]
        val = bv_ref[...] + wv[0:1] * vt[0] + wv[1:2] * vt[1] + wv[2:3] * vt[2]
        dz = dz_ref[...].astype(F32)
        sg = _sigmoid(gate)
        dgate = dz * val * (sg * (1.0 + gate * (1.0 - sg)))
        dval = dz * (gate * sg)
        dc_ref[0] = dgate
        dc_ref[1] = dval

        @pl.when(i == 0)
        def _():
            dw_ref[...] = jnp.zeros_like(dw_ref)
            db_ref[...] = jnp.zeros_like(db_ref)

        for half, (d, taps) in enumerate(((dgate, gt), (dval, vt))):
            for k, tap in enumerate(taps):
                dw_ref[half, k : k + 1, :] += jnp.sum(d * tap, axis=0, keepdims=True)
            db_ref[half] += jnp.sum(d, axis=0, keepdims=True)

    return _pallas(
        body,
        9,
        dep,
        name="ffn_act_bwd",
        grid=(nj, t // tm),
        in_specs=[tile(0), before(0), tile(nj), before(nj), vec(3, 0), vec(3, nj), vec(1, 0), vec(1, nj), tile(0)],
        out_specs=[
            pl.BlockSpec((2, tm, tc), lambda j, i: (0, i, j)),
            pl.BlockSpec((2, 3, tc), lambda j, i: (0, 0, j)),
            pl.BlockSpec((2, 1, tc), lambda j, i: (0, 0, j)),
        ],
        out_shape=[
            jax.ShapeDtypeStruct((2, t, D_FF), F32),
            jax.ShapeDtypeStruct((2, 3, D_FF), F32),
            jax.ShapeDtypeStruct((2, 1, D_FF), F32),
        ],
        compiler_params=_params("parallel", "arbitrary"),
    )(u, u, u, u, conv_w, conv_w, conv_b, conv_b, dz)


def _conv_transpose(dc, conv_w):
    t = dc.shape[1]
    tm, tc, nj = _ffn_tiles(t)
    nt = t // tm
    rows = 8

    def body(d_ref, h_ref, w_ref, o_ref):
        i = pl.program_id(2)
        d = d_ref[...]
        halo = jnp.where(i < nt - 1, h_ref[...], 0.0)
        w = w_ref[...]
        o_ref[...] = (w[2:3] * d + w[1:2] * _shift_up(d, halo, 1, tm) + w[0:1] * _shift_up(d, halo, 2, tm)).astype(BF16)

    return pl.pallas_call(
        body,
        name="conv_transpose",
        grid=(2, nj, nt),
        in_specs=[
            pl.BlockSpec((None, tm, tc), lambda s, j, i: (s, i, j)),
            pl.BlockSpec((None, rows, tc), lambda s, j, i: (s, jnp.minimum((i + 1) * (tm // rows), t // rows - 1), j)),
            pl.BlockSpec((3, tc), lambda s, j, i: (0, s * nj + j)),
        ],
        out_specs=pl.BlockSpec((tm, tc), lambda s, j, i: (i, s * nj + j)),
        out_shape=jax.ShapeDtypeStruct((t, 2 * D_FF), BF16),
        compiler_params=_params("parallel", "parallel", "parallel"),
    )(dc, dc, conv_w)


def _ple_loss(gpre, pp, h2, final_norm, target):
    t, d = h2.shape
    tm = _tile(t, (256,))

    def body(gp_ref, pp_ref, h_ref, g_ref, tg_ref, dh_ref, dgp_ref, dpp_ref, dg_ref, loss_ref):
        i = pl.program_id(0)
        gate = _sigmoid(gp_ref[...])
        ppv = pp_ref[...]
        h3 = h_ref[...] + gate * ppv
        r = lax.rsqrt(jnp.mean(h3 * h3, axis=-1, keepdims=True) + EPS)
        n = h3 * r
        g = g_ref[...]
        err = n * g - tg_ref[...]
        loss = 0.5 * jnp.sum(jnp.mean(err * err, axis=-1, keepdims=True))
        dy = err * (1.0 / d)
        dn = dy * g
        dh = r * (dn - n * jnp.mean(dn * n, axis=-1, keepdims=True))
        dh_ref[...] = dh
        dgp_ref[...] = (dh * ppv * gate * (1.0 - gate)).astype(BF16)
        dpp_ref[...] = (dh * gate).astype(BF16)
        dg = jnp.sum(dy * n, axis=0, keepdims=True)

        @pl.when(i == 0)
        def _():
            dg_ref[...] = dg
            loss_ref[...] = jnp.full(loss_ref.shape, loss, F32)

        @pl.when(i > 0)
        def _():
            dg_ref[...] += dg
            loss_ref[...] += loss

    row = pl.BlockSpec((tm, d), lambda i: (i, 0))
    vec = pl.BlockSpec((1, d), lambda i: (0, 0))
    return pl.pallas_call(
        body,
        name="ple_loss",
        grid=(t // tm,),
        in_specs=[row, row, row, vec, row],
        out_specs=[row, row, row, vec, pl.BlockSpec((8, 128), lambda i: (0, 0))],
        out_shape=[
            jax.ShapeDtypeStruct((t, d), F32),
            jax.ShapeDtypeStruct((t, d), BF16),
            jax.ShapeDtypeStruct((t, d), BF16),
            jax.ShapeDtypeStruct((1, d), F32),
            jax.ShapeDtypeStruct((8, 128), F32),
        ],
        compiler_params=_params("arbitrary"),
    )(gpre, pp, h2, final_norm, target)


def _adamw(w, g, m, v):
    m = ADAM_B1 * m + (1.0 - ADAM_B1) * g
    v = ADAM_B2 * v + (1.0 - ADAM_B2) * (g * g)
    m_hat = m / (1.0 - ADAM_B1 ** ADAM_STEP)
    v_hat = v / (1.0 - ADAM_B2 ** ADAM_STEP)
    return -ADAM_LR * (m_hat / (jnp.sqrt(v_hat) + ADAM_EPS) + ADAM_WD * w), m, v


def _adam_big(w, m, v, own, recv, name):
    r, c = w.shape
    tr = _tile(r, (256, 176))

    def body(w_ref, m_ref, v_ref, own_ref, recv_ref, g_ref, d_ref, nm_ref, nv_ref):
        g = own_ref[...]
        for k in range(3):
            g = g + recv_ref[k].astype(F32)
        g_ref[...] = g
        d_ref[...], nm_ref[...], nv_ref[...] = _adamw(w_ref[...], g, m_ref[...], v_ref[...])

    blk = pl.BlockSpec((tr, c), lambda i: (i, 0))
    return pl.pallas_call(
        body,
        name=name,
        grid=(r // tr,),
        in_specs=[blk, blk, blk, blk, pl.BlockSpec((3, tr, c), lambda i: (0, i, 0))],
        out_specs=[blk] * 4,
        out_shape=[jax.ShapeDtypeStruct((r, c), F32)] * 4,
        compiler_params=_params("parallel"),
    )(w, m, v, own, recv)


def _adam_small(w, g, m, v):
    def body(w_ref, g_ref, m_ref, v_ref, d_ref, nm_ref, nv_ref):
        d_ref[...], nm_ref[...], nv_ref[...] = _adamw(w_ref[...], g_ref[...], m_ref[...], v_ref[...])

    return pl.pallas_call(body, name="adam_small", out_shape=[jax.ShapeDtypeStruct(w.shape, F32)] * 3, compiler_params=_params())(w, g, m, v)


def _cast_bf16(w, name):
    r, c = w.shape
    tr = _tile(r, (256, 176))

    def body(w_ref, o_ref):
        o_ref[...] = w_ref[...].astype(BF16)

    blk = pl.BlockSpec((tr, c), lambda i: (i, 0))
    return pl.pallas_call(
        body, name=name, grid=(r // tr,), in_specs=[blk], out_specs=blk, out_shape=jax.ShapeDtypeStruct((r, c), BF16), compiler_params=_params("parallel")
    )(w)


def _position():
    return lax.axis_index("x"), lax.axis_index("y"), lax.axis_index("c")


def _other_chips(x, y):
    return [(1 - x, y), (x, 1 - y), (1 - x, 1 - y)]


def _shard_of(ref, axis, size, dev):
    start = pl.multiple_of((4 * dev[0] + 2 * dev[1] + dev[2]) * size, 128 if axis == 1 else 16)
    return ref.at[:, pl.ds(start, size)] if axis == 1 else ref.at[pl.ds(start, size), :]


def _all_gather(shards, axes):
    n = len(shards)

    def body(*refs):
        ins, outs = refs[:n], refs[n : 2 * n]
        send_sems, recv_sems, local_sems = refs[2 * n :]
        x, y, c = _position()
        me, sibling = (x, y, c), (x, y, 1 - c)
        chips = _other_chips(x, y)
        firsts, passed, locals_ = [], [], []
        for w in range(n):
            size = shards[w].shape[axes[w]]
            slot = functools.partial(_shard_of, outs[w], axes[w], size)

            def copy(k, block, to, src=None, w=w, slot=slot):
                return pltpu.make_async_remote_copy(
                    src_ref=slot(block) if src is None else src,
                    dst_ref=slot(block),
                    send_sem=send_sems.at[7 * w + k],
                    recv_sem=recv_sems.at[7 * w + k],
                    device_id=to,
                    device_id_type=MESH,
                )

            mine = pltpu.make_async_copy(ins[w], slot(me), local_sems.at[w])
            mine.start()
            locals_.append(mine)
            first = [copy(0, me, sibling, src=ins[w])] + [copy(1 + j, me, (*chip, c), src=ins[w]) for j, chip in enumerate(chips)]
            for cp in first:
                cp.start()
            firsts.append((first, copy))
        for w in range(n):
            first, copy = firsts[w]
            fwd = [copy(4 + j, (*chip, c), sibling) for j, chip in enumerate(chips)]
            for j, chip in enumerate(chips):
                copy(1 + j, (*chip, c), me).wait_recv()
                fwd[j].start()
            passed.append(fwd)
        for w in range(n):
            first, copy = firsts[w]
            copy(0, sibling, me).wait_recv()
            for j, chip in enumerate(chips):
                copy(4 + j, (*chip, 1 - c), me).wait_recv()
            for cp in first + passed[w]:
                cp.wait_send()
            locals_[w].wait()

    def full(s, ax):
        shape = list(s.shape)
        shape[ax] *= N_DEV
        return jax.ShapeDtypeStruct(tuple(shape), s.dtype)

    return pl.pallas_call(
        body,
        name="all_gather_weights",
        in_specs=[ANY] * n,
        out_specs=[ANY] * n,
        out_shape=[full(s, ax) for s, ax in zip(shards, axes)],
        scratch_shapes=[pltpu.SemaphoreType.DMA((7 * n,)), pltpu.SemaphoreType.DMA((7 * n,)), pltpu.SemaphoreType.DMA((n,))],
    )(*shards)


def _add_blocks(ids, grad, landed, axis, size, targets, out_dtype, name):
    rows = size if axis == 0 else grad.shape[0]
    cols = size if axis == 1 else grad.shape[1]
    tr = _tile(rows, (256, 176))
    nr = rows // tr
    nt = len(targets)

    def body(ids_ref, g_ref, l_ref, o_ref):
        o_ref[...] = (g_ref[...] + l_ref[...]).astype(out_dtype)

    if axis == 1:
        g_spec = pl.BlockSpec((tr, cols), lambda k, i, ids: (i, ids[targets[0] + k]))
    else:
        g_spec = pl.BlockSpec((tr, cols), lambda k, i, ids: (ids[targets[0] + k] * nr + i, 0))
    return pl.pallas_call(
        body,
        name=name,
        grid_spec=pltpu.PrefetchScalarGridSpec(
            num_scalar_prefetch=1,
            grid=(nt, nr),
            in_specs=[g_spec, pl.BlockSpec((None, tr, cols), lambda k, i, ids: (ids[4 + targets[0] + k], i, 0))],
            out_specs=pl.BlockSpec((None, tr, cols), lambda k, i, ids: (k, i, 0)),
        ),
        out_shape=jax.ShapeDtypeStruct((nt, rows, cols), out_dtype),
        compiler_params=_params("parallel", "parallel"),
    )(ids, grad, landed)


def _all_reduce_small(vec):
    rows = vec.shape[0]

    def body(v_ref, o_ref, land, send_sems, recv_sems):
        x, y, c = _position()
        mine = 4 * x + 2 * y + c
        copies = []
        for mask in range(1, N_DEV):
            peer = (1 - x if mask & 4 else x, 1 - y if mask & 2 else y, 1 - c if mask & 1 else c)
            copies.append(
                pltpu.make_async_remote_copy(
                    src_ref=v_ref, dst_ref=land.at[mine], send_sem=send_sems.at[mask - 1], recv_sem=recv_sems.at[mask - 1], device_id=peer, device_id_type=MESH
                )
            )
        for cp in copies:
            cp.start()
        land[mine] = v_ref[...]
        for cp in copies:
            cp.wait()
        acc = land[0]
        for k in range(1, N_DEV):
            acc = acc + land[k]
        o_ref[...] = acc

    return pl.pallas_call(
        body,
        name="all_reduce_small",
        out_shape=jax.ShapeDtypeStruct(vec.shape, F32),
        in_specs=[pl.BlockSpec(memory_space=pltpu.VMEM)],
        out_specs=pl.BlockSpec(memory_space=pltpu.VMEM),
        scratch_shapes=[pltpu.VMEM((N_DEV, rows, 128), F32), pltpu.SemaphoreType.DMA((N_DEV - 1,)), pltpu.SemaphoreType.DMA((N_DEV - 1,))],
    )(vec)


def _rows128(a, rows):
    flat = a.reshape(-1)
    return jnp.pad(flat, (0, rows * 128 - flat.shape[0])).reshape(rows, 128)


def _pad_rel(a):
    return jnp.pad(a.reshape(ATT_HEADS, -1)[:, :N_REL], ((0, 0), (0, N_REL_PAD - N_REL)))


SMALL = [("norm_mix", 16), ("lb_logits", 16), ("hg_norm", 8), ("rel_bias", 24), ("norm_ffn", 16), ("conv_b", 88), ("norm_ple", 16), ("final_norm", 16)]
CONV_W_FULL_ROWS = 3 * 2 * D_FF // 128
CONV_W_SHARD_ROWS = 40


def _pack_small(parts):
    return jnp.concatenate([_rows128(_pad_rel(parts[k]) if k == "rel_bias" else parts[k], rows) for k, rows in SMALL], axis=0)


def _unpack_small(packed, shapes):
    out, at = {}, 0
    for k, rows in SMALL:
        blk = packed[at : at + rows]
        at += rows
        if k == "rel_bias":
            out[k] = blk.reshape(ATT_HEADS, N_REL_PAD)[:, :N_REL].reshape(shapes[k])
        else:
            n = 1
            for s in shapes[k]:
                n *= s
            out[k] = blk.reshape(-1)[:n].reshape(shapes[k])
    return out, at


BIG = [("w_in", 1), ("w_out", 0), ("w_up", 1), ("w_down", 0), ("w_ple_gate", 0), ("w_ple_proj", 1)]


HBM = pl.BlockSpec(memory_space=pltpu.HBM)
SEM = pl.BlockSpec(memory_space=pltpu.SEMAPHORE)
EFFECT = pltpu.SideEffectType.DATAFLOW_SIDE_EFFECTING


def _copies(plan, refs, send_sems, recv_sems):
    return [
        pltpu.make_async_remote_copy(src_ref=src, dst_ref=dst, send_sem=send_sems.at[i], recv_sem=recv_sems.at[i], device_id=dev, device_id_type=MESH)
        for i, (src, dst, dev) in enumerate(plan(refs))
    ]


def _split_start(name, arrays, plan, n):
    k = len(arrays)

    def body(*refs):
        for cp in _copies(plan, refs[:k], refs[k], refs[k + 1]):
            cp.start()
        refs[-1][...] = jnp.zeros_like(refs[-1])

    out = pl.pallas_call(
        body,
        name=name,
        out_shape=(pltpu.SemaphoreType.DMA((n,)), pltpu.SemaphoreType.DMA((n,)), *[pltpu.HBM(a.shape, a.dtype) for a in arrays], jax.ShapeDtypeStruct((8, 128), F32)),
        in_specs=[HBM] * k,
        out_specs=(SEM, SEM, *[HBM] * k, pl.BlockSpec(memory_space=pltpu.VMEM)),
        input_output_aliases={i: 2 + i for i in range(k)},
        compiler_params=pltpu.CompilerParams(has_side_effects=EFFECT),
    )(*[pltpu.with_memory_space_constraint(a, pltpu.HBM) for a in arrays])
    return out[0], out[1], list(out[2 : 2 + k]), out[-1]


def _split_wait(name, send, recv, arrays, plan, after):
    k = len(arrays)

    def body(*refs):
        for cp in _copies(plan, refs[:k], refs[k], refs[k + 1]):
            cp.wait_send()
            cp.wait_recv()

    out = pl.pallas_call(
        body,
        name=name,
        out_shape=tuple(pltpu.HBM(a.shape, a.dtype) for a in arrays),
        in_specs=[HBM] * k + [SEM, SEM, ANY],
        out_specs=tuple([HBM] * k),
        input_output_aliases={i: i for i in range(k)},
        compiler_params=pltpu.CompilerParams(has_side_effects=EFFECT),
    )(*arrays, send, recv, after)
    return list(out)


def _cast_into(w, me, axis, name, dep):
    r, c = w.shape
    tr = _tile(r, (256, 176))
    nr = r // tr

    def body(me_ref, w_ref, dep_ref, o_ref):
        o_ref[...] = w_ref[...].astype(BF16)

    if axis == 1:
        shape, o_spec = (r, N_DEV * c), pl.BlockSpec((tr, c), lambda i, me: (i, me[0]))
    else:
        shape, o_spec = (N_DEV * r, c), pl.BlockSpec((tr, c), lambda i, me: (me[0] * nr + i, 0))
    return pl.pallas_call(
        body,
        name=name,
        grid_spec=pltpu.PrefetchScalarGridSpec(
            num_scalar_prefetch=1, grid=(nr,), in_specs=[pl.BlockSpec((tr, c), lambda i, me: (i, 0)), ANY], out_specs=o_spec
        ),
        out_shape=jax.ShapeDtypeStruct(shape, BF16),
        compiler_params=_params("parallel"),
    )(me, w, dep)


LATE = ["w_out", "w_up", "w_down", "w_ple_gate", "w_ple_proj"]
GROUPS = [["w_ple_proj", "w_ple_gate", "w_down"], ["w_up"], ["w_out"], ["w_in"]]
STAGES = ["conv_transpose", "d_mix_out", "hgrn_bwd", "d_norm_mix_out"]


class _Exchange:
    def __init__(self, big, position):
        self.big, self.axis = big, dict(BIG)
        self.size = {k: big[k].shape[self.axis[k]] for k in big}
        self.x, self.y, self.c = position
        chips = [(self.x, self.y)] + _other_chips(self.x, self.y)
        self.ids = jnp.stack([4 * cx + 2 * cy + self.c for cx, cy in chips] + [2 * cx + cy for cx, cy in chips]).astype(jnp.int32)
        self.token, self.grads, self.state, self.wfull = None, {}, {}, {}


    def _slot(self, ref, k, dev):
        return _shard_of(ref, self.axis[k], self.size[k], dev)

    def _plan_gather(self, refs):
        x, y, c = _position()
        me, out = (x, y, c), []
        for k, ref in zip(LATE, refs):
            mine = self._slot(ref, k, me)
            out.append((mine, mine, (x, y, 1 - c)))
            out += [(mine, mine, (*chip, c)) for chip in _other_chips(x, y)]
        return out

    def _plan_forward(self, refs):
        x, y, c = _position()
        out = []
        for k, ref in zip(LATE, refs):
            for chip in _other_chips(x, y):
                block = self._slot(ref, k, (*chip, c))
                out.append((block, block, (x, y, 1 - c)))
        return out

    def _plan_sibling(self, names, refs):
        x, y, c = _position()
        n = len(names)
        return [(self._slot(refs[i], k, (p // 2, p % 2, 1 - c)), refs[n + i].at[p], (x, y, 1 - c)) for i, k in enumerate(names) for p in range(4)]

    def _plan_chips(self, names, refs):
        x, y, c = _position()
        n = len(names)
        return [(refs[i].at[j], refs[n + i].at[j], (*chip, c)) for i in range(n) for j, chip in enumerate(_other_chips(x, y))]


    def gather(self, conv_w):
        w_in, conv_full = _all_gather([_cast_bf16(self.big["w_in"], "cast_w_in"), conv_w], [1, 1])
        self.wfull["w_in"] = w_in
        me = (4 * self.x + 2 * self.y + self.c).astype(jnp.int32).reshape(1)
        fulls = [_cast_into(self.big[k], me, self.axis[k], "cast_" + k, w_in) for k in LATE]
        send, recv, fulls, self.token = _split_start("gather_start", fulls, self._plan_gather, 4 * len(LATE))
        self.late = (send, recv, fulls)
        return conv_full

    def weight(self, k):
        return self.wfull[k]

    def dep(self):
        token, self.token = self.token, None
        return token

    def grad(self, k, g):
        self.grads[k] = g
        for gi, names in enumerate(GROUPS):
            if k == names[-1]:
                plan = functools.partial(self._plan_sibling, names)
                lands = [lax.empty((4, *self._shard_shape(n)), F32) for n in names]
                send, recv, arrays, self.token = _split_start(f"sibling_start_{gi}", [self.grads[n] for n in names] + lands, plan, 4 * len(names))
                self.state[gi] = (send, recv, arrays, plan)

    def done(self, stage, after):
        if stage == "hgrn_fwd":
            send, recv, fulls = self.late
            fulls = _split_wait("gather_wait", send, recv, fulls, self._plan_gather, after)
            send, recv, fulls, self.token = _split_start("forward_start", fulls, self._plan_forward, 3 * len(LATE))
            self.late = (send, recv, fulls)
        elif stage == "att_fwd":
            send, recv, fulls = self.late
            self.wfull.update(zip(LATE, _split_wait("forward_wait", send, recv, fulls, self._plan_forward, after)))
        elif stage in STAGES:
            self._to_chips(STAGES.index(stage), after)

    def _shard_shape(self, k):
        shape = list(self.grads[k].shape)
        shape[self.axis[k]] = self.size[k]
        return tuple(shape)

    def _to_chips(self, gi, after):
        names = GROUPS[gi]
        n = len(names)
        send, recv, arrays, plan = self.state[gi]
        arrays = _split_wait(f"sibling_wait_{gi}", send, recv, arrays, plan, after)
        own, parts = [], []
        for k, g, land in zip(names, arrays[:n], arrays[n:]):
            own.append(_add_blocks(self.ids, g, land, self.axis[k], self.size[k], [0], F32, "add_own_" + k)[0])
            parts.append(_add_blocks(self.ids, g, land, self.axis[k], self.size[k], [1, 2, 3], BF16, "add_send_" + k))
        plan = functools.partial(self._plan_chips, names)
        lands = [lax.empty(part.shape, BF16) for part in parts]
        send, recv, arrays, self.token = _split_start(f"chips_start_{gi}", parts + lands, plan, 3 * n)
        self.state[gi] = (send, recv, arrays, plan, own)

    def finish(self, gi, after):
        names = GROUPS[gi]
        send, recv, arrays, plan, own = self.state[gi]
        arrays = _split_wait(f"chips_wait_{gi}", send, recv, arrays, plan, after)
        return {k: (o, r) for k, o, r in zip(names, own, arrays[len(names) :])}


class _Resident:
    def __init__(self, wfull):
        self.wfull, self.grads = wfull, {}

    def weight(self, k):
        return self.wfull[k]

    def grad(self, k, g):
        self.grads[k] = g

    def dep(self):
        return None

    def done(self, stage, after):
        pass


def _local_step(x, p, target, small, conv_w, ex):
    a1, r1 = _rms_fwd(x, small["norm_mix"], "norm_mix_fwd", dep=ex.dep())
    proj = _matmul(a1, ex.weight("w_in"), "nn", F32, "in_proj")
    bias = _bias_table(small["rel_bias"])
    y_hg, o_hg, states = _hgrn_fwd(proj, small["lb_logits"], small["hg_norm"])
    ex.done("hgrn_fwd", y_hg)
    y_att = _att_fwd(proj, bias, dep=ex.dep())
    ex.done("att_fwd", y_att)
    ycat = jnp.concatenate([y_hg, y_att], axis=1)
    h1 = _matmul(ycat, ex.weight("w_out"), "nn", F32, "out_proj", resid=x)
    a2, r2 = _rms_fwd(h1, small["norm_ffn"], "norm_ffn_fwd")
    u = _matmul(a2, ex.weight("w_up"), "nn", BF16, "up_proj")
    z = _ffn_act_fwd(u, conv_w, small["conv_b"])
    h2 = _matmul(z, ex.weight("w_down"), "nn", F32, "down_proj", tk=2816, resid=h1)
    a3, r3 = _rms_fwd(h2, small["norm_ple"], "norm_ple_fwd")
    gpre = _matmul(a3, ex.weight("w_ple_gate"), "nn", F32, "ple_gate")
    pp = _matmul(p, ex.weight("w_ple_proj"), "nn", F32, "ple_proj")
    dh3, dgpre, dpp, d_final, loss = _ple_loss(gpre, pp, h2, small["final_norm"], target)

    ex.grad("w_ple_proj", _matmul(p, dpp, "tn", F32, "d_w_ple_proj", tk=2048))
    ex.grad("w_ple_gate", _matmul(a3, dgpre, "tn", F32, "d_w_ple_gate", tk=2048))
    da3 = _matmul(dgpre, ex.weight("w_ple_gate"), "nt", F32, "d_norm_ple_out")
    dh2, d_ple = _rms_bwd(da3, h2, r3, small["norm_ple"], dh3, "norm_ple_bwd")
    dz = _matmul(dh2, ex.weight("w_down"), "nt", BF16, "d_ffn_act")
    ex.grad("w_down", _matmul(z, dh2, "tn", F32, "d_w_down", tk=2048))
    dc, dcw, dcb = _ffn_act_bwd(u, dz, conv_w, small["conv_b"], dep=ex.dep())
    du = _conv_transpose(dc, conv_w)
    ex.done("conv_transpose", du)
    d_conv_w = jnp.concatenate([dcw[0], dcw[1]], axis=1)
    d_conv_b = jnp.concatenate([dcb[0], dcb[1]], axis=1)
    ex.grad("w_up", _matmul(a2, du, "tn", F32, "d_w_up", tk=2048, dep=ex.dep()))
    da2 = _matmul(du, ex.weight("w_up"), "nt", F32, "d_norm_ffn_out", tk=2816, dep=ex.dep())
    dh1, d_ffn = _rms_bwd(da2, h1, r2, small["norm_ffn"], dh2, "norm_ffn_bwd")
    dycat = _matmul(dh1, ex.weight("w_out"), "nt", F32, "d_mix_out")
    ex.done("d_mix_out", dycat)
    ex.grad("w_out", _matmul(ycat, dh1, "tn", F32, "d_w_out", tk=2048, dep=ex.dep()))
    dp_hg, d_lb, d_hgn = _hgrn_bwd(proj, small["lb_logits"], small["hg_norm"], o_hg, dycat, states, dep=ex.dep())
    ex.done("hgrn_bwd", d_lb)
    dq_att, dk_att, dv_att, gsum = _att_bwd(proj, bias, dycat, dep=ex.dep())
    d_rel = _rel_bias_grad(gsum)
    dproj = jnp.concatenate([dp_hg[0], dp_hg[1], dp_hg[2], dp_hg[3], dq_att, dk_att, dv_att], axis=1)
    ex.grad("w_in", _matmul(a1, dproj, "tn", F32, "d_w_in", tk=2048))
    da1 = _matmul(dproj, ex.weight("w_in"), "nt", F32, "d_norm_mix_out", tk=1792, dep=ex.dep())
    ex.done("d_norm_mix_out", da1)
    dx, d_mix = _rms_bwd(da1, x, r1, small["norm_mix"], dh1, "norm_mix_bwd", dep=ex.dep())
    d_small = {
        "norm_mix": d_mix, "lb_logits": d_lb, "hg_norm": d_hgn, "rel_bias": d_rel, "norm_ffn": d_ffn,
        "conv_b": d_conv_b, "norm_ple": d_ple, "final_norm": d_final,
    }
    return loss, dx, d_small, d_conv_w


def kernel(x, p, norm_mix, w_in, lb_logits, hg_norm, rel_bias, w_out, norm_ffn, w_up, conv_w, conv_b, w_down, norm_ple, w_ple_gate, w_ple_proj, final_norm, loss_target, m_norm_mix, m_w_in, m_lb_logits, m_hg_norm, m_rel_bias, m_w_out, m_norm_ffn, m_w_up, m_conv_w, m_conv_b, m_w_down, m_norm_ple, m_w_ple_gate, m_w_ple_proj, m_final_norm, v_norm_mix, v_w_in, v_lb_logits, v_hg_norm, v_rel_bias, v_w_out, v_norm_ffn, v_w_up, v_conv_w, v_conv_b, v_w_down, v_norm_ple, v_w_ple_gate, v_w_ple_proj, v_final_norm):
    given = dict(locals())
    mx, my, mc = _position()
    me = 4 * mx + 2 * my + mc
    big = {k: given[k][0] for k, _ in BIG}
    ex = _Exchange(big, (mx, my, mc))
    conv_w_full = ex.gather(conv_w[0])

    small = {
        "norm_mix": norm_mix, "lb_logits": lb_logits, "hg_norm": hg_norm, "rel_bias": rel_bias[0], "norm_ffn": norm_ffn,
        "conv_b": conv_b, "norm_ple": norm_ple, "final_norm": final_norm.reshape(1, -1),
    }
    loss, dx, d_small, d_conv_w = _local_step(x[0], p[0, 0], loss_target[0], small, conv_w_full, ex)

    packed = jnp.concatenate([_pack_small(d_small), _rows128(d_conv_w, CONV_W_FULL_ROWS), _rows128(loss[0:1, 0:1], 8)], axis=0)
    reduced = _all_reduce_small(packed)

    out = {}
    for gi in range(len(GROUPS)):
        for k, (o, r) in ex.finish(gi, reduced).items():
            g, d, nm, nv = _adam_big(big[k], given["m_" + k][0], given["v_" + k][0], o, r, "adam_" + k)
            out[k] = tuple(a[None] for a in (g, d, nm, nv))
    shapes = {k: given[k].shape for k, _ in SMALL}
    g_small, at = _unpack_small(reduced, shapes)
    g_conv_full = reduced[at : at + CONV_W_FULL_ROWS].reshape(3, 2 * D_FF)
    total_loss = reduced[at + CONV_W_FULL_ROWS, 0]
    cw = conv_w.shape[2]
    g_conv = lax.dynamic_slice_in_dim(g_conv_full, me * cw, cw, axis=1)

    def pack_with_conv(parts, conv_part):
        return jnp.concatenate([_pack_small(parts), _rows128(conv_part, CONV_W_SHARD_ROWS)], axis=0)

    d_pk, m_pk, v_pk = _adam_small(
        pack_with_conv({k: given[k] for k, _ in SMALL}, conv_w),
        pack_with_conv(g_small, g_conv),
        pack_with_conv({k: given["m_" + k] for k, _ in SMALL}, m_conv_w),
        pack_with_conv({k: given["v_" + k] for k, _ in SMALL}, v_conv_w),
    )
    for name, pk in (("d", d_pk), ("m", m_pk), ("v", v_pk)):
        parts, at = _unpack_small(pk, shapes)
        parts["conv_w"] = pk[at : at + CONV_W_SHARD_ROWS].reshape(-1)[: 3 * cw].reshape(conv_w.shape)
        for k, a in parts.items():
            out.setdefault(k, {})
            out[k][name] = a
    for k, _ in SMALL:
        out[k]["g"] = g_small[k]
    out["conv_w"]["g"] = g_conv.reshape(conv_w.shape)

    order = ["norm_mix", "w_in", "lb_logits", "hg_norm", "rel_bias", "w_out", "norm_ffn", "w_up", "conv_w", "conv_b", "w_down", "norm_ple", "w_ple_gate", "w_ple_proj", "final_norm"]

    def pick(k, what):
        return out[k][what] if isinstance(out[k], dict) else out[k][{"g": 0, "d": 1, "m": 2, "v": 3}[what]]

    return (total_loss, dx[None], *[pick(k, "g") for k in order], *[pick(k, "d") for k in order], *[pick(k, "m") for k in order], *[pick(k, "v") for k in order])
```

```python
import functools

import jax
import jax.numpy as jnp
from jax import lax
from jax.experimental import pallas as pl
from jax.experimental.pallas import tpu as pltpu

F32 = jnp.float32
BF16 = jnp.bfloat16

D_MODEL = 2048
CHUNK = 64
HG_HEADS = 8
HEAD_DIM = 128
HG_WIDTH = HG_HEADS * HEAD_DIM
ATT_HEADS = 8
ATT_WIDTH = ATT_HEADS * HEAD_DIM
LEFT_CHUNKS = 8
PAD = LEFT_CHUNKS * CHUNK
BAND = PAD + CHUNK
REL_CLIP = 128
N_REL = 2 * REL_CLIP + 1
N_REL_PAD = 384
D_FF = 5632
EPS = 1e-6
ATT_SCALE = HEAD_DIM ** -0.5
SUB = 16
HG_BLOCK = 4
Q_BLOCK = 4 * CHUNK
K_BLOCK = Q_BLOCK + PAD
DIAG = 1024

ADAM_LR = 0.001
ADAM_B1 = 0.9
ADAM_B2 = 0.999
ADAM_EPS = 1e-08
ADAM_WD = 0.01
ADAM_STEP = 10

N_DEV = 8
VMEM_LIMIT = 48 * 1024 * 1024
MESH = pl.DeviceIdType.MESH
ANY = pl.BlockSpec(memory_space=pl.ANY)
HIGHEST = lax.Precision.HIGHEST

NN = (((1,), (0,)), ((), ()))
NT = (((1,), (1,)), ((), ()))
TN = (((0,), (0,)), ((), ()))


def _params(*sem):
    return pltpu.CompilerParams(dimension_semantics=sem if sem else None, vmem_limit_bytes=VMEM_LIMIT)


def _pallas(body, n_in, dep, **kw):
    if dep is None:
        return pl.pallas_call(body, **kw)

    def body_after(*refs):
        body(*refs[:n_in], *refs[n_in + 1 :])

    call = pl.pallas_call(body_after, **dict(kw, in_specs=list(kw["in_specs"]) + [ANY]))
    return lambda *ops: call(*ops, dep)


def _dot(a, b, dims=NN):
    return lax.dot_general(a, b, dims, preferred_element_type=F32)


def _dot3(a, b, dims=NN):
    a_hi, b_hi = a.astype(BF16), b.astype(BF16)
    a_lo, b_lo = (a - a_hi.astype(F32)).astype(BF16), (b - b_hi.astype(F32)).astype(BF16)
    return _dot(a_hi, b_hi, dims) + (_dot(a_hi, b_lo, dims) + _dot(a_lo, b_hi, dims))


def _sigmoid(x):
    return 1.0 / (1.0 + jnp.exp(-x))


def _tile(n, prefs):
    for t in prefs:
        if n % t == 0:
            return t
    return n


def _matmul(a, b, mode, out_dtype, name, tm=512, tn=1024, tk=None, resid=None, dep=None):
    if mode == "nn":
        (m, k), n = a.shape, b.shape[1]
    elif mode == "nt":
        (m, k), n = a.shape, b.shape[0]
    else:
        (k, m), n = a.shape, b.shape[1]
    tm = _tile(m, (tm, 256, 128))
    tn = _tile(n, (tn, 512, 256, 128))
    tk = k if tk is None else _tile(k, (tk,))
    nk = k // tk
    dims = {"nn": NN, "nt": NT, "tn": TN}[mode]
    a_spec = pl.BlockSpec((tk, tm), lambda i, j, s: (s, i)) if mode == "tn" else pl.BlockSpec((tm, tk), lambda i, j, s: (i, s))
    b_spec = pl.BlockSpec((tn, tk), lambda i, j, s: (j, s)) if mode == "nt" else pl.BlockSpec((tk, tn), lambda i, j, s: (s, j))
    o_spec = pl.BlockSpec((tm, tn), lambda i, j, s: (i, j))
    has_res = resid is not None

    def body(*refs):
        a_ref, b_ref = refs[0], refs[1]
        o_ref = refs[2 + has_res]
        part = _dot(a_ref[...].astype(BF16), b_ref[...].astype(BF16), dims)

        def finish(acc):
            if has_res:
                acc = acc + refs[2][...]
            o_ref[...] = acc.astype(out_dtype)

        if nk == 1:
            finish(part)
        else:
            acc_ref = refs[-1]
            s = pl.program_id(2)

            @pl.when(s == 0)
            def _():
                acc_ref[...] = part

            @pl.when(s > 0)
            def _():
                acc_ref[...] += part

            @pl.when(s == nk - 1)
            def _():
                finish(acc_ref[...])

    return _pallas(
        body,
        2 + has_res,
        dep,
        name=name,
        grid=(m // tm, n // tn, nk),
        in_specs=[a_spec, b_spec] + ([o_spec] if has_res else []),
        out_specs=o_spec,
        out_shape=jax.ShapeDtypeStruct((m, n), out_dtype),
        scratch_shapes=[pltpu.VMEM((tm, tn), F32)] if nk > 1 else [],
        compiler_params=_params("parallel", "parallel", "arbitrary"),
    )(*([a, b] + ([resid] if has_res else [])))


def _rms_fwd(x, g, name, dep=None):
    t, d = x.shape
    tm = _tile(t, (256,))

    def body(x_ref, g_ref, a_ref, r_ref):
        xv = x_ref[...]
        r = lax.rsqrt(jnp.mean(xv * xv, axis=-1, keepdims=True) + EPS)
        a_ref[...] = (xv * r * g_ref[...]).astype(BF16)
        r_ref[...] = r

    row = pl.BlockSpec((tm, d), lambda i: (i, 0))
    return _pallas(
        body,
        2,
        dep,
        name=name,
        grid=(t // tm,),
        in_specs=[row, pl.BlockSpec((1, d), lambda i: (0, 0))],
        out_specs=[row, pl.BlockSpec((tm, 1), lambda i: (i, 0))],
        out_shape=[jax.ShapeDtypeStruct((t, d), BF16), jax.ShapeDtypeStruct((t, 1), F32)],
        compiler_params=_params("parallel"),
    )(x, g)


def _rms_bwd(da, x, r, g, resid, name, dep=None):
    t, d = x.shape
    tm = _tile(t, (256,))

    def body(da_ref, x_ref, r_ref, g_ref, res_ref, dx_ref, dg_ref):
        i = pl.program_id(0)
        rv = r_ref[...]
        n = x_ref[...] * rv
        dav = da_ref[...]
        dn = dav * g_ref[...]
        dx_ref[...] = rv * (dn - n * jnp.mean(dn * n, axis=-1, keepdims=True)) + res_ref[...]
        part = jnp.sum(dav * n, axis=0, keepdims=True)

        @pl.when(i == 0)
        def _():
            dg_ref[...] = part

        @pl.when(i > 0)
        def _():
            dg_ref[...] += part

    row = pl.BlockSpec((tm, d), lambda i: (i, 0))
    vec = pl.BlockSpec((1, d), lambda i: (0, 0))
    return _pallas(
        body,
        5,
        dep,
        name=name,
        grid=(t // tm,),
        in_specs=[row, row, pl.BlockSpec((tm, 1), lambda i: (i, 0)), vec, row],
        out_specs=[row, vec],
        out_shape=[jax.ShapeDtypeStruct((t, d), F32), jax.ShapeDtypeStruct((1, d), F32)],
        compiler_params=_params("arbitrary"),
    )(da, x, r, g, resid)


def _tri(n, upper):
    r = lax.broadcasted_iota(jnp.int32, (n, n), 0)
    c = lax.broadcasted_iota(jnp.int32, (n, n), 1)
    return jnp.where((c >= r) if upper else (c <= r), 1.0, 0.0).astype(F32)


def _hgrn_gates(q, fp, lbl):
    l0, l1 = lbl[0:1, :], lbl[1:2, :]
    mx = jnp.maximum(l0, l1)
    e0, e1 = jnp.exp(l0 - mx), jnp.exp(l1 - mx)
    lb = e0 / (e0 + e1)
    sig = _sigmoid(fp)
    f = lb + (1.0 - lb) * sig
    kk = (1.0 - lb) * _sigmoid(-fp)
    sq = _sigmoid(q)
    b = jnp.dot(_tri(CHUNK, False), jnp.log(f), precision=HIGHEST, preferred_element_type=F32)
    return lb, sig, f, kk, sq, q * sq, b


def _intra_blocks(b):
    out = []
    for lo in range(0, CHUNK, SUB):
        hi = lo + SUB
        br = b[lo + SUB // 2 : lo + SUB // 2 + 1, :]
        row = lax.broadcasted_iota(jnp.int32, (SUB, hi), 0) + lo
        col = lax.broadcasted_iota(jnp.int32, (SUB, hi), 1)
        out.append((lo, hi, jnp.exp(b[lo:hi] - br), jnp.exp(br - b[:hi]), col <= row))
    return out


def _hgrn_fwd(proj, lb_logits, hg_norm):
    t = proj.shape[0]
    nc = t // CHUNK

    def body(q_ref, f_ref, i_ref, g_ref, lbl_ref, hgn_ref, y_ref, o_ref, st_ref, s_scr):
        c = pl.program_id(1)

        @pl.when(c == 0)
        def _():
            s_scr[...] = jnp.zeros_like(s_scr)

        for j in range(HG_BLOCK):
            cols = slice(j * HEAD_DIM, (j + 1) * HEAD_DIM)
            st = s_scr[j]
            st_ref[j] = st
            _, _, _, kk, _, qf, b = _hgrn_gates(q_ref[:, cols], f_ref[:, cols], lbl_ref[:, cols])
            vb = i_ref[:, cols].astype(BF16)
            bl = b[CHUNK - 1 : CHUNK, :]
            o = _dot((qf * jnp.exp(b)).astype(BF16), st.astype(BF16), NT)
            parts = []
            for lo, hi, ea, eb, mask in _intra_blocks(b):
                p = jnp.where(mask, _dot((qf[lo:hi] * ea).astype(BF16), (kk[:hi] * eb).astype(BF16), NT), 0.0)
                parts.append(_dot(p.astype(BF16), vb[:hi]))
            o = o + jnp.concatenate(parts, axis=0)
            s_scr[j] = st * jnp.exp(bl) + _dot(vb, (kk * jnp.exp(bl - b)).astype(BF16), TN)
            o_ref[:, cols] = o
            r = lax.rsqrt(jnp.mean(o * o, axis=-1, keepdims=True) + EPS)
            gg = g_ref[:, cols]
            y_ref[:, cols] = ((o * r * hgn_ref[...]) * (gg * _sigmoid(gg))).astype(BF16)

    wide = HG_BLOCK * HEAD_DIM
    groups = HG_HEADS // HG_BLOCK

    def col(k):
        return pl.BlockSpec((CHUNK, wide), lambda g, c: (c, k * groups + g))

    out = pl.BlockSpec((CHUNK, wide), lambda g, c: (c, g))
    return pl.pallas_call(
        body,
        name="hgrn_fwd",
        grid=(groups, nc),
        in_specs=[col(0), col(1), col(2), col(3), pl.BlockSpec((2, wide), lambda g, c: (0, g)), pl.BlockSpec((1, HEAD_DIM), lambda g, c: (0, 0))],
        out_specs=[out, out, pl.BlockSpec((HG_BLOCK, None, HEAD_DIM, HEAD_DIM), lambda g, c: (g, c, 0, 0))],
        out_shape=[
            jax.ShapeDtypeStruct((t, HG_WIDTH), BF16),
            jax.ShapeDtypeStruct((t, HG_WIDTH), F32),
            jax.ShapeDtypeStruct((HG_HEADS, nc, HEAD_DIM, HEAD_DIM), F32),
        ],
        scratch_shapes=[pltpu.VMEM((HG_BLOCK, HEAD_DIM, HEAD_DIM), F32)],
        compiler_params=_params("arbitrary", "arbitrary"),
    )(proj, proj, proj, proj, lb_logits, hg_norm)


def _hgrn_bwd(proj, lb_logits, hg_norm, o_hg, dycat, states, dep=None):
    t = proj.shape[0]
    nc = t // CHUNK

    def body(q_ref, f_ref, i_ref, g_ref, lbl_ref, hgn_ref, o_ref, dy_ref, st_ref, dp_ref, dlbl_ref, dhgn_ref, dst_scr, dlb_scr):
        h = pl.program_id(0)
        c = pl.program_id(1)

        @pl.when(c == 0)
        def _():
            dst_scr[...] = jnp.zeros_like(dst_scr)
            dlb_scr[...] = jnp.zeros_like(dlb_scr)

        @pl.when((c == 0) & (h == 0))
        def _():
            dhgn_ref[...] = jnp.zeros_like(dhgn_ref)

        hgn = hgn_ref[...]
        rows = lax.broadcasted_iota(jnp.int32, (CHUNK, HEAD_DIM), 0)
        for j in range(HG_BLOCK):
            cols = slice(j * HEAD_DIM, (j + 1) * HEAD_DIM)
            q, fp, gg = q_ref[:, cols], f_ref[:, cols], g_ref[:, cols]
            lb, sig, f, kk, sq, qf, b = _hgrn_gates(q, fp, lbl_ref[:, cols])
            o, dy = o_ref[:, cols], dy_ref[:, cols]
            sg = _sigmoid(gg)
            r = lax.rsqrt(jnp.mean(o * o, axis=-1, keepdims=True) + EPS)
            n = o * r
            don = dy * (gg * sg)
            dgg = dy * (n * hgn) * (sg * (1.0 + gg * (1.0 - sg)))
            dhgn_ref[...] += jnp.sum(don * n, axis=0, keepdims=True)
            dn = don * hgn
            do = r * (dn - n * jnp.mean(dn * n, axis=-1, keepdims=True))
            dob = do.astype(BF16)
            vi = i_ref[:, cols]
            vb = vi.astype(BF16)
            st, dstn = st_ref[j], dst_scr[j]
            bl = b[CHUNK - 1 : CHUNK, :]
            e_b, e_bl, e_l = jnp.exp(b), jnp.exp(bl - b), jnp.exp(bl)
            dq_acc = _dot3(do, st) * e_b
            dk_inter = _dot3(vi, dstn) * e_bl
            dk_acc = dk_inter
            dv_acc = _dot((kk * e_bl).astype(BF16), dstn.astype(BF16), NT)
            dst_scr[j] = dstn * e_l + _dot(dob, (qf * e_b).astype(BF16), TN)
            db_last = e_l * jnp.sum(st * dstn, axis=0, keepdims=True) + jnp.sum(kk * dk_inter, axis=0, keepdims=True)
            dq_parts = []
            for lo, hi, ea, eb, mask in _intra_blocks(b):
                a, bk = qf[lo:hi] * ea, kk[:hi] * eb
                p = jnp.where(mask, _dot(a.astype(BF16), bk.astype(BF16), NT), 0.0)
                dp = jnp.where(mask, _dot3(do[lo:hi], vi[:hi], NT), 0.0)
                dq_parts.append(_dot3(dp, bk) * ea)
                dki = _dot3(dp, a, TN) * eb
                dvi = _dot(p.astype(BF16), dob[lo:hi], TN)
                if hi < CHUNK:
                    zeros = jnp.zeros((CHUNK - hi, HEAD_DIM), F32)
                    dki = jnp.concatenate([dki, zeros], axis=0)
                    dvi = jnp.concatenate([dvi, zeros], axis=0)
                dk_acc = dk_acc + dki
                dv_acc = dv_acc + dvi
            dq_acc = dq_acc + jnp.concatenate(dq_parts, axis=0)
            db = qf * dq_acc - kk * dk_acc + jnp.where(rows == CHUNK - 1, db_last, 0.0)
            dlf = jnp.dot(_tri(CHUNK, True), db, precision=HIGHEST, preferred_element_type=F32)
            dfk = dlf / f - dk_acc
            dp_ref[0, :, cols] = (dq_acc * (sq * (1.0 + q * (1.0 - sq)))).astype(BF16)
            dp_ref[1, :, cols] = ((1.0 - lb) * dfk * sig * (1.0 - sig)).astype(BF16)
            dp_ref[2, :, cols] = dv_acc.astype(BF16)
            dp_ref[3, :, cols] = dgg.astype(BF16)
            dlb_scr[:, cols] += jnp.sum(dfk * (1.0 - sig), axis=0, keepdims=True)

            @pl.when(c == nc - 1)
            def _(lb=lb, cols=cols):
                dl0 = dlb_scr[:, cols] * lb * (1.0 - lb)
                dlbl_ref[0:1, cols] = dl0
                dlbl_ref[1:2, cols] = -dl0

    wide = HG_BLOCK * HEAD_DIM
    groups = HG_HEADS // HG_BLOCK

    def col(k):
        return pl.BlockSpec((CHUNK, wide), lambda g, c: (nc - 1 - c, k * groups + g))

    blk = pl.BlockSpec((CHUNK, wide), lambda g, c: (nc - 1 - c, g))
    return _pallas(
        body,
        9,
        dep,
        name="hgrn_bwd",
        grid=(groups, nc),
        in_specs=[
            col(0), col(1), col(2), col(3),
            pl.BlockSpec((2, wide), lambda g, c: (0, g)),
            pl.BlockSpec((1, HEAD_DIM), lambda g, c: (0, 0)),
            blk, blk,
            pl.BlockSpec((HG_BLOCK, None, HEAD_DIM, HEAD_DIM), lambda g, c: (g, nc - 1 - c, 0, 0)),
        ],
        out_specs=[
            pl.BlockSpec((4, CHUNK, wide), lambda g, c: (0, nc - 1 - c, g)),
            pl.BlockSpec((2, wide), lambda g, c: (0, g)),
            pl.BlockSpec((1, HEAD_DIM), lambda g, c: (0, 0)),
        ],
        out_shape=[
            jax.ShapeDtypeStruct((4, t, HG_WIDTH), BF16),
            jax.ShapeDtypeStruct((2, HG_WIDTH), F32),
            jax.ShapeDtypeStruct((1, HEAD_DIM), F32),
        ],
        scratch_shapes=[pltpu.VMEM((HG_BLOCK, HEAD_DIM, HEAD_DIM), F32), pltpu.VMEM((1, wide), F32)],
        compiler_params=_params("arbitrary", "arbitrary"),
    )(proj, proj, proj, proj, lb_logits, hg_norm, o_hg, dycat, states)


def _diagonal_slots(shift):
    i = lax.broadcasted_iota(jnp.int32, (N_REL_PAD, DIAG), 0)
    u = lax.broadcasted_iota(jnp.int32, (N_REL_PAD, DIAG), 1)
    offset = u - shift if shift else jnp.where(u < K_BLOCK, u, u - DIAG)
    return jnp.where(jnp.clip(PAD - offset, -REL_CLIP, REL_CLIP) + REL_CLIP == i, 1.0, 0.0).astype(BF16)


def _split3(x):
    hi = x.astype(BF16)
    mid = (x - hi.astype(F32)).astype(BF16)
    return hi, mid, (x - hi.astype(F32) - mid.astype(F32)).astype(BF16)


def _bias_table(rel_bias):
    def body(rb_ref, o_ref, diag):
        h = pl.program_id(0)

        @pl.when(h == 0)
        def _():
            hi, mid, lo = _split3(rb_ref[...])
            slots = _diagonal_slots(0)
            diag[...] = _dot(hi, slots) + (_dot(mid, slots) + _dot(lo, slots))

        rows = jnp.broadcast_to(diag[pl.ds(h, 1), :], (Q_BLOCK, DIAG))
        o_ref[...] = pltpu.roll(rows, 0, 1, stride=1, stride_axis=0)[:, :K_BLOCK]

    return pl.pallas_call(
        body,
        name="bias_table",
        grid=(ATT_HEADS,),
        in_specs=[pl.BlockSpec((ATT_HEADS, N_REL_PAD), lambda h: (0, 0))],
        out_specs=pl.BlockSpec((None, Q_BLOCK, K_BLOCK), lambda h: (h, 0, 0)),
        out_shape=jax.ShapeDtypeStruct((ATT_HEADS, Q_BLOCK, K_BLOCK), F32),
        scratch_shapes=[pltpu.VMEM((ATT_HEADS, DIAG), F32)],
        compiler_params=_params("arbitrary"),
    )(rel_bias)


def _att_probs(q_ref, kpad, bias_ref, blk):
    qs = (q_ref[...] * ATT_SCALE).astype(BF16)
    start = pl.multiple_of(blk * Q_BLOCK, Q_BLOCK)
    kb = kpad[pl.ds(start, K_BLOCK), :]
    s = _dot(qs, kb, NT) + bias_ref[...]
    row = lax.broadcasted_iota(jnp.int32, (Q_BLOCK, K_BLOCK), 0)
    col = lax.broadcasted_iota(jnp.int32, (Q_BLOCK, K_BLOCK), 1)
    first = row - (row & (CHUNK - 1))
    valid = (col >= first) & (col < first + BAND) & (col + (blk * Q_BLOCK - PAD) >= 0)
    s = jnp.where(valid, s, jnp.finfo(F32).min)
    e = jnp.exp(s - jnp.max(s, axis=-1, keepdims=True))
    return qs, kb, start, e / jnp.sum(e, axis=-1, keepdims=True)


def _fill_padded(dst, src):
    dst[0:PAD, :] = jnp.zeros((PAD, HEAD_DIM), BF16)
    dst[PAD:, :] = src[...].astype(BF16)


def _att_fwd(proj, bias, dep=None):
    t = proj.shape[0]
    nb = t // Q_BLOCK

    def body(q_ref, k_ref, v_ref, bias_ref, y_ref, kpad, vpad):
        c = pl.program_id(1)

        @pl.when(c == 0)
        def _():
            _fill_padded(kpad, k_ref)
            _fill_padded(vpad, v_ref)

        _, _, start, p = _att_probs(q_ref, kpad, bias_ref, c)
        y_ref[...] = _dot(p.astype(BF16), vpad[pl.ds(start, K_BLOCK), :]).astype(BF16)

    base = 4 * HG_HEADS
    return _pallas(
        body,
        4,
        dep,
        name="att_fwd",
        grid=(ATT_HEADS, nb),
        in_specs=[
            pl.BlockSpec((Q_BLOCK, HEAD_DIM), lambda h, c: (c, base + h)),
            pl.BlockSpec((t, HEAD_DIM), lambda h, c: (0, base + ATT_HEADS + h)),
            pl.BlockSpec((t, HEAD_DIM), lambda h, c: (0, base + 2 * ATT_HEADS + h)),
            pl.BlockSpec((None, Q_BLOCK, K_BLOCK), lambda h, c: (h, 0, 0)),
        ],
        out_specs=pl.BlockSpec((Q_BLOCK, HEAD_DIM), lambda h, c: (c, h)),
        out_shape=jax.ShapeDtypeStruct((t, ATT_WIDTH), BF16),
        scratch_shapes=[pltpu.VMEM((t + PAD, HEAD_DIM), BF16), pltpu.VMEM((t + PAD, HEAD_DIM), BF16)],
        compiler_params=_params("arbitrary", "arbitrary"),
    )(proj, proj, proj, bias)


def _att_bwd(proj, bias, dycat, dep=None):
    t = proj.shape[0]
    nb = t // Q_BLOCK

    def body(q_ref, k_ref, v_ref, bias_ref, dy_ref, dq_ref, dk_ref, dv_ref, g_ref, kpad, vpad, dkacc, dvacc):
        c = pl.program_id(1)

        @pl.when(c == 0)
        def _():
            _fill_padded(kpad, k_ref)
            _fill_padded(vpad, v_ref)
            dkacc[...] = jnp.zeros_like(dkacc)
            dvacc[...] = jnp.zeros_like(dvacc)
            g_ref[...] = jnp.zeros_like(g_ref)

        qs, kb, start, p = _att_probs(q_ref, kpad, bias_ref, c)
        band = pl.ds(start, K_BLOCK)
        dyb = dy_ref[...].astype(BF16)
        dvacc[band, :] += _dot(p.astype(BF16), dyb, TN)
        dp = _dot(dyb, vpad[band, :], NT)
        ds = p * (dp - jnp.sum(dp * p, axis=-1, keepdims=True))
        g_ref[...] += ds
        dsb = ds.astype(BF16)
        dq_ref[...] = (_dot(dsb, kb) * ATT_SCALE).astype(BF16)
        dkacc[band, :] += _dot(dsb, qs, TN)

        @pl.when(c == nb - 1)
        def _():
            dk_ref[...] = dkacc[PAD:, :].astype(BF16)
            dv_ref[...] = dvacc[PAD:, :].astype(BF16)

    base = 4 * HG_HEADS
    whole = pl.BlockSpec((t, HEAD_DIM), lambda h, c: (0, h))
    return _pallas(
        body,
        5,
        dep,
        name="att_bwd",
        grid=(ATT_HEADS, nb),
        in_specs=[
            pl.BlockSpec((Q_BLOCK, HEAD_DIM), lambda h, c: (c, base + h)),
            pl.BlockSpec((t, HEAD_DIM), lambda h, c: (0, base + ATT_HEADS + h)),
            pl.BlockSpec((t, HEAD_DIM), lambda h, c: (0, base + 2 * ATT_HEADS + h)),
            pl.BlockSpec((None, Q_BLOCK, K_BLOCK), lambda h, c: (h, 0, 0)),
            pl.BlockSpec((Q_BLOCK, HEAD_DIM), lambda h, c: (c, HG_HEADS + h)),
        ],
        out_specs=[pl.BlockSpec((Q_BLOCK, HEAD_DIM), lambda h, c: (c, h)), whole, whole, pl.BlockSpec((None, Q_BLOCK, K_BLOCK), lambda h, c: (h, 0, 0))],
        out_shape=[
            jax.ShapeDtypeStruct((t, ATT_WIDTH), BF16),
            jax.ShapeDtypeStruct((t, ATT_WIDTH), BF16),
            jax.ShapeDtypeStruct((t, ATT_WIDTH), BF16),
            jax.ShapeDtypeStruct((ATT_HEADS, Q_BLOCK, K_BLOCK), F32),
        ],
        scratch_shapes=[
            pltpu.VMEM((t + PAD, HEAD_DIM), BF16),
            pltpu.VMEM((t + PAD, HEAD_DIM), BF16),
            pltpu.VMEM((t + PAD, HEAD_DIM), F32),
            pltpu.VMEM((t + PAD, HEAD_DIM), F32),
        ],
        compiler_params=_params("arbitrary", "arbitrary"),
    )(proj, proj, proj, bias, dycat)


def _rel_bias_grad(gsum):
    def body(g_ref, o_ref):
        r = lax.broadcasted_iota(jnp.int32, (Q_BLOCK, Q_BLOCK), 0)
        c = lax.broadcasted_iota(jnp.int32, (Q_BLOCK, Q_BLOCK), 1)
        flip = jnp.where(r + c == Q_BLOCK - 1, 1.0, 0.0).astype(BF16)
        sums = []
        for h in range(ATT_HEADS):
            hi, mid, lo = _split3(g_ref[h])
            rev = _dot(flip, hi) + (_dot(flip, mid) + _dot(flip, lo))
            wide = jnp.concatenate([rev, jnp.zeros((Q_BLOCK, DIAG - K_BLOCK), F32)], axis=1)
            sums.append(jnp.sum(pltpu.roll(wide, 0, 1, stride=1, stride_axis=0), axis=0, keepdims=True))
        hi, mid, lo = _split3(jnp.concatenate(sums, axis=0))
        slots = _diagonal_slots(Q_BLOCK - 1)
        o_ref[...] = _dot(hi, slots, NT) + (_dot(mid, slots, NT) + _dot(lo, slots, NT))

    return pl.pallas_call(
        body,
        name="rel_bias_grad",
        out_shape=jax.ShapeDtypeStruct((ATT_HEADS, N_REL_PAD), F32),
        compiler_params=_params(),
    )(gsum)


HALO = 16


def _ffn_tiles(t):
    tm = _tile(t, (512,))
    tc = 256
    return tm, tc, D_FF // tc


def _shift_down(x, halo, k, tm):
    rows = lax.broadcasted_iota(jnp.int32, x.shape, 0)
    out = jnp.where(rows >= k, pltpu.roll(x, k, 0), halo[HALO - 1 : HALO, :])
    if k == 2:
        out = jnp.where(rows == 0, halo[HALO - 2 : HALO - 1, :], out)
    return out


def _shift_up(x, halo, k, tm):
    rows = lax.broadcasted_iota(jnp.int32, x.shape, 0)
    out = jnp.where(rows < tm - k, pltpu.roll(x, tm - k, 0), halo[0:1, :])
    if k == 2:
        out = jnp.where(rows == tm - 1, halo[1:2, :], out)
    return out


def _conv_taps(u_ref, halo_ref, i, tm):
    u = u_ref[...].astype(F32)
    halo = jnp.where(i > 0, halo_ref[...].astype(F32), 0.0)
    return _shift_down(u, halo, 2, tm), _shift_down(u, halo, 1, tm), u


def _ffn_in_specs(tm, tc, nj):
    before = lambda off: pl.BlockSpec((HALO, tc), lambda j, i: (jnp.maximum(i * (tm // HALO) - 1, 0), off + j))
    tile = lambda off: pl.BlockSpec((tm, tc), lambda j, i: (i, off + j))
    vec = lambda rows, off: pl.BlockSpec((rows, tc), lambda j, i: (0, off + j))
    return tile, before, vec


def _ffn_act_fwd(u, conv_w, conv_b):
    t = u.shape[0]
    tm, tc, nj = _ffn_tiles(t)
    tile, before, vec = _ffn_in_specs(tm, tc, nj)

    def body(ug_ref, hg_ref, uv_ref, hv_ref, wg_ref, wv_ref, bg_ref, bv_ref, z_ref):
        i = pl.program_id(1)
        g2, g1, g0 = _conv_taps(ug_ref, hg_ref, i, tm)
        v2, v1, v0 = _conv_taps(uv_ref, hv_ref, i, tm)
        wg, wv = wg_ref[...], wv_ref[...]
        gate = bg_ref[...] + wg[0:1] * g2 + wg[1:2] * g1 + wg[2:3] * g0
        val = bv_ref[...] + wv[0:1] * v2 + wv[1:2] * v1 + wv[2:3] * v0
        z_ref[...] = (gate * _sigmoid(gate) * val).astype(BF16)

    return pl.pallas_call(
        body,
        name="ffn_act_fwd",
        grid=(nj, t // tm),
        in_specs=[tile(0), before(0), tile(nj), before(nj), vec(3, 0), vec(3, nj), vec(1, 0), vec(1, nj)],
        out_specs=pl.BlockSpec((tm, tc), lambda j, i: (i, j)),
        out_shape=jax.ShapeDtypeStruct((t, D_FF), BF16),
        compiler_params=_params("parallel", "parallel"),
    )(u, u, u, u, conv_w, conv_w, conv_b, conv_b)


def _ffn_act_bwd(u, dz, conv_w, conv_b, dep=None):
    t = u.shape[0]
    tm, tc, nj = _ffn_tiles(t)
    tile, before, vec = _ffn_in_specs(tm, tc, nj)

    def body(ug_ref, hg_ref, uv_ref, hv_ref, wg_ref, wv_ref, bg_ref, bv_ref, dz_ref, dc_ref, dw_ref, db_ref):
        i = pl.program_id(1)
        gt = _conv_taps(ug_ref, hg_ref, i, tm)
        vt = _conv_taps(uv_ref, hv_ref, i, tm)
        wg, wv = wg_ref[...], wv_ref[...]
        gate = bg_ref[...] + wg[0:1] * gt[0] + wg[1:2] * gt[1] + wg[2:3] * gt[2]
        val = bv_ref[...] + wv[0:1] * vt[0] + wv[1:2] * vt[1] + wv[2:3] * vt[2]
        dz = dz_ref[...].astype(F32)
        sg = _sigmoid(gate)
        dgate = dz * val * (sg * (1.0 + gate * (1.0 - sg)))
        dval = dz * (gate * sg)
        dc_ref[0] = dgate
        dc_ref[1] = dval

        @pl.when(i == 0)
        def _():
            dw_ref[...] = jnp.zeros_like(dw_ref)
            db_ref[...] = jnp.zeros_like(db_ref)

        for half, (d, taps) in enumerate(((dgate, gt), (dval, vt))):
            for k, tap in enumerate(taps):
                dw_ref[half, k : k + 1, :] += jnp.sum(d * tap, axis=0, keepdims=True)
            db_ref[half] += jnp.sum(d, axis=0, keepdims=True)

    return _pallas(
        body,
        9,
        dep,
        name="ffn_act_bwd",
        grid=(nj, t // tm),
        in_specs=[tile(0), before(0), tile(nj), before(nj), vec(3, 0), vec(3, nj), vec(1, 0), vec(1, nj), tile(0)],
        out_specs=[
            pl.BlockSpec((2, tm, tc), lambda j, i: (0, i, j)),
            pl.BlockSpec((2, 3, tc), lambda j, i: (0, 0, j)),
            pl.BlockSpec((2, 1, tc), lambda j, i: (0, 0, j)),
        ],
        out_shape=[
            jax.ShapeDtypeStruct((2, t, D_FF), F32),
            jax.ShapeDtypeStruct((2, 3, D_FF), F32),
            jax.ShapeDtypeStruct((2, 1, D_FF), F32),
        ],
        compiler_params=_params("parallel", "arbitrary"),
    )(u, u, u, u, conv_w, conv_w, conv_b, conv_b, dz)


def _conv_transpose(dc, conv_w):
    t = dc.shape[1]
    tm, tc, nj = _ffn_tiles(t)
    nt = t // tm
    rows = 8

    def body(d_ref, h_ref, w_ref, o_ref):
        i = pl.program_id(2)
        d = d_ref[...]
        halo = jnp.where(i < nt - 1, h_ref[...], 0.0)
        w = w_ref[...]
        o_ref[...] = (w[2:3] * d + w[1:2] * _shift_up(d, halo, 1, tm) + w[0:1] * _shift_up(d, halo, 2, tm)).astype(BF16)

    return pl.pallas_call(
        body,
        name="conv_transpose",
        grid=(2, nj, nt),
        in_specs=[
            pl.BlockSpec((None, tm, tc), lambda s, j, i: (s, i, j)),
            pl.BlockSpec((None, rows, tc), lambda s, j, i: (s, jnp.minimum((i + 1) * (tm // rows), t // rows - 1), j)),
            pl.BlockSpec((3, tc), lambda s, j, i: (0, s * nj + j)),
        ],
        out_specs=pl.BlockSpec((tm, tc), lambda s, j, i: (i, s * nj + j)),
        out_shape=jax.ShapeDtypeStruct((t, 2 * D_FF), BF16),
        compiler_params=_params("parallel", "parallel", "parallel"),
    )(dc, dc, conv_w)


def _ple_loss(gpre, pp, h2, final_norm, target):
    t, d = h2.shape
    tm = _tile(t, (256,))

    def body(gp_ref, pp_ref, h_ref, g_ref, tg_ref, dh_ref, dgp_ref, dpp_ref, dg_ref, loss_ref):
        i = pl.program_id(0)
        gate = _sigmoid(gp_ref[...])
        ppv = pp_ref[...]
        h3 = h_ref[...] + gate * ppv
        r = lax.rsqrt(jnp.mean(h3 * h3, axis=-1, keepdims=True) + EPS)
        n = h3 * r
        g = g_ref[...]
        err = n * g - tg_ref[...]
        loss = 0.5 * jnp.sum(jnp.mean(err * err, axis=-1, keepdims=True))
        dy = err * (1.0 / d)
        dn = dy * g
        dh = r * (dn - n * jnp.mean(dn * n, axis=-1, keepdims=True))
        dh_ref[...] = dh
        dgp_ref[...] = (dh * ppv * gate * (1.0 - gate)).astype(BF16)
        dpp_ref[...] = (dh * gate).astype(BF16)
        dg = jnp.sum(dy * n, axis=0, keepdims=True)

        @pl.when(i == 0)
        def _():
            dg_ref[...] = dg
            loss_ref[...] = jnp.full(loss_ref.shape, loss, F32)

        @pl.when(i > 0)
        def _():
            dg_ref[...] += dg
            loss_ref[...] += loss

    row = pl.BlockSpec((tm, d), lambda i: (i, 0))
    vec = pl.BlockSpec((1, d), lambda i: (0, 0))
    return pl.pallas_call(
        body,
        name="ple_loss",
        grid=(t // tm,),
        in_specs=[row, row, row, vec, row],
        out_specs=[row, row, row, vec, pl.BlockSpec((8, 128), lambda i: (0, 0))],
        out_shape=[
            jax.ShapeDtypeStruct((t, d), F32),
            jax.ShapeDtypeStruct((t, d), BF16),
            jax.ShapeDtypeStruct((t, d), BF16),
            jax.ShapeDtypeStruct((1, d), F32),
            jax.ShapeDtypeStruct((8, 128), F32),
        ],
        compiler_params=_params("arbitrary"),
    )(gpre, pp, h2, final_norm, target)


def _adamw(w, g, m, v):
    m = ADAM_B1 * m + (1.0 - ADAM_B1) * g
    v = ADAM_B2 * v + (1.0 - ADAM_B2) * (g * g)
    m_hat = m / (1.0 - ADAM_B1 ** ADAM_STEP)
    v_hat = v / (1.0 - ADAM_B2 ** ADAM_STEP)
    return -ADAM_LR * (m_hat / (jnp.sqrt(v_hat) + ADAM_EPS) + ADAM_WD * w), m, v


def _adam_big(w, m, v, own, recv, name):
    r, c = w.shape
    tr = _tile(r, (256, 176))

    def body(w_ref, m_ref, v_ref, own_ref, recv_ref, g_ref, d_ref, nm_ref, nv_ref):
        g = own_ref[...]
        for k in range(3):
            g = g + recv_ref[k].astype(F32)
        g_ref[...] = g
        d_ref[...], nm_ref[...], nv_ref[...] = _adamw(w_ref[...], g, m_ref[...], v_ref[...])

    blk = pl.BlockSpec((tr, c), lambda i: (i, 0))
    return pl.pallas_call(
        body,
        name=name,
        grid=(r // tr,),
        in_specs=[blk, blk, blk, blk, pl.BlockSpec((3, tr, c), lambda i: (0, i, 0))],
        out_specs=[blk] * 4,
        out_shape=[jax.ShapeDtypeStruct((r, c), F32)] * 4,
        compiler_params=_params("parallel"),
    )(w, m, v, own, recv)


def _adam_small(w, g, m, v):
    def body(w_ref, g_ref, m_ref, v_ref, d_ref, nm_ref, nv_ref):
        d_ref[...], nm_ref[...], nv_ref[...] = _adamw(w_ref[...], g_ref[...], m_ref[...], v_ref[...])

    return pl.pallas_call(body, name="adam_small", out_shape=[jax.ShapeDtypeStruct(w.shape, F32)] * 3, compiler_params=_params())(w, g, m, v)


def _cast_bf16(w, name):
    r, c = w.shape
    tr = _tile(r, (256, 176))

    def body(w_ref, o_ref):
        o_ref[...] = w_ref[...].astype(BF16)

    blk = pl.BlockSpec((tr, c), lambda i: (i, 0))
    return pl.pallas_call(
        body, name=name, grid=(r // tr,), in_specs=[blk], out_specs=blk, out_shape=jax.ShapeDtypeStruct((r, c), BF16), compiler_params=_params("parallel")
    )(w)


def _position():
    return lax.axis_index("x"), lax.axis_index("y"), lax.axis_index("c")


def _other_chips(x, y):
    return [(1 - x, y), (x, 1 - y), (1 - x, 1 - y)]


def _shard_of(ref, axis, size, dev):
    start = pl.multiple_of((4 * dev[0] + 2 * dev[1] + dev[2]) * size, 128 if axis == 1 else 16)
    return ref.at[:, pl.ds(start, size)] if axis == 1 else ref.at[pl.ds(start, size), :]


def _all_gather(shards, axes):
    n = len(shards)

    def body(*refs):
        ins, outs = refs[:n], refs[n : 2 * n]
        send_sems, recv_sems, local_sems = refs[2 * n :]
        x, y, c = _position()
        me, sibling = (x, y, c), (x, y, 1 - c)
        chips = _other_chips(x, y)
        firsts, passed, locals_ = [], [], []
        for w in range(n):
            size = shards[w].shape[axes[w]]
            slot = functools.partial(_shard_of, outs[w], axes[w], size)

            def copy(k, block, to, src=None, w=w, slot=slot):
                return pltpu.make_async_remote_copy(
                    src_ref=slot(block) if src is None else src,
                    dst_ref=slot(block),
                    send_sem=send_sems.at[7 * w + k],
                    recv_sem=recv_sems.at[7 * w + k],
                    device_id=to,
                    device_id_type=MESH,
                )

            mine = pltpu.make_async_copy(ins[w], slot(me), local_sems.at[w])
            mine.start()
            locals_.append(mine)
            first = [copy(0, me, sibling, src=ins[w])] + [copy(1 + j, me, (*chip, c), src=ins[w]) for j, chip in enumerate(chips)]
            for cp in first:
                cp.start()
            firsts.append((first, copy))
        for w in range(n):
            first, copy = firsts[w]
            fwd = [copy(4 + j, (*chip, c), sibling) for j, chip in enumerate(chips)]
            for j, chip in enumerate(chips):
                copy(1 + j, (*chip, c), me).wait_recv()
                fwd[j].start()
            passed.append(fwd)
        for w in range(n):
            first, copy = firsts[w]
            copy(0, sibling, me).wait_recv()
            for j, chip in enumerate(chips):
                copy(4 + j, (*chip, 1 - c), me).wait_recv()
            for cp in first + passed[w]:
                cp.wait_send()
            locals_[w].wait()

    def full(s, ax):
        shape = list(s.shape)
        shape[ax] *= N_DEV
        return jax.ShapeDtypeStruct(tuple(shape), s.dtype)

    return pl.pallas_call(
        body,
        name="all_gather_weights",
        in_specs=[ANY] * n,
        out_specs=[ANY] * n,
        out_shape=[full(s, ax) for s, ax in zip(shards, axes)],
        scratch_shapes=[pltpu.SemaphoreType.DMA((7 * n,)), pltpu.SemaphoreType.DMA((7 * n,)), pltpu.SemaphoreType.DMA((n,))],
    )(*shards)


def _add_blocks(ids, grad, landed, axis, size, targets, out_dtype, name):
    rows = size if axis == 0 else grad.shape[0]
    cols = size if axis == 1 else grad.shape[1]
    tr = _tile(rows, (256, 176))
    nr = rows // tr
    nt = len(targets)

    def body(ids_ref, g_ref, l_ref, o_ref):
        o_ref[...] = (g_ref[...] + l_ref[...]).astype(out_dtype)

    if axis == 1:
        g_spec = pl.BlockSpec((tr, cols), lambda k, i, ids: (i, ids[targets[0] + k]))
    else:
        g_spec = pl.BlockSpec((tr, cols), lambda k, i, ids: (ids[targets[0] + k] * nr + i, 0))
    return pl.pallas_call(
        body,
        name=name,
        grid_spec=pltpu.PrefetchScalarGridSpec(
            num_scalar_prefetch=1,
            grid=(nt, nr),
            in_specs=[g_spec, pl.BlockSpec((None, tr, cols), lambda k, i, ids: (ids[4 + targets[0] + k], i, 0))],
            out_specs=pl.BlockSpec((None, tr, cols), lambda k, i, ids: (k, i, 0)),
        ),
        out_shape=jax.ShapeDtypeStruct((nt, rows, cols), out_dtype),
        compiler_params=_params("parallel", "parallel"),
    )(ids, grad, landed)


def _all_reduce_small(vec):
    rows = vec.shape[0]

    def body(v_ref, o_ref, land, send_sems, recv_sems):
        x, y, c = _position()
        mine = 4 * x + 2 * y + c
        copies = []
        for mask in range(1, N_DEV):
            peer = (1 - x if mask & 4 else x, 1 - y if mask & 2 else y, 1 - c if mask & 1 else c)
            copies.append(
                pltpu.make_async_remote_copy(
                    src_ref=v_ref, dst_ref=land.at[mine], send_sem=send_sems.at[mask - 1], recv_sem=recv_sems.at[mask - 1], device_id=peer, device_id_type=MESH
                )
            )
        for cp in copies:
            cp.start()
        land[mine] = v_ref[...]
        for cp in copies:
            cp.wait()
        acc = land[0]
        for k in range(1, N_DEV):
            acc = acc + land[k]
        o_ref[...] = acc

    return pl.pallas_call(
        body,
        name="all_reduce_small",
        out_shape=jax.ShapeDtypeStruct(vec.shape, F32),
        in_specs=[pl.BlockSpec(memory_space=pltpu.VMEM)],
        out_specs=pl.BlockSpec(memory_space=pltpu.VMEM),
        scratch_shapes=[pltpu.VMEM((N_DEV, rows, 128), F32), pltpu.SemaphoreType.DMA((N_DEV - 1,)), pltpu.SemaphoreType.DMA((N_DEV - 1,))],
    )(vec)


def _rows128(a, rows):
    flat = a.reshape(-1)
    return jnp.pad(flat, (0, rows * 128 - flat.shape[0])).reshape(rows, 128)


def _pad_rel(a):
    return jnp.pad(a.reshape(ATT_HEADS, -1)[:, :N_REL], ((0, 0), (0, N_REL_PAD - N_REL)))


SMALL = [("norm_mix", 16), ("lb_logits", 16), ("hg_norm", 8), ("rel_bias", 24), ("norm_ffn", 16), ("conv_b", 88), ("norm_ple", 16), ("final_norm", 16)]
CONV_W_FULL_ROWS = 3 * 2 * D_FF // 128
CONV_W_SHARD_ROWS = 40


def _pack_small(parts):
    return jnp.concatenate([_rows128(_pad_rel(parts[k]) if k == "rel_bias" else parts[k], rows) for k, rows in SMALL], axis=0)


def _unpack_small(packed, shapes):
    out, at = {}, 0
    for k, rows in SMALL:
        blk = packed[at : at + rows]
        at += rows
        if k == "rel_bias":
            out[k] = blk.reshape(ATT_HEADS, N_REL_PAD)[:, :N_REL].reshape(shapes[k])
        else:
            n = 1
            for s in shapes[k]:
                n *= s
            out[k] = blk.reshape(-1)[:n].reshape(shapes[k])
    return out, at


BIG = [("w_in", 1), ("w_out", 0), ("w_up", 1), ("w_down", 0), ("w_ple_gate", 0), ("w_ple_proj", 1)]


HBM = pl.BlockSpec(memory_space=pltpu.HBM)
SEM = pl.BlockSpec(memory_space=pltpu.SEMAPHORE)
EFFECT = pltpu.SideEffectType.DATAFLOW_SIDE_EFFECTING


def _copies(plan, refs, send_sems, recv_sems):
    return [
        pltpu.make_async_remote_copy(src_ref=src, dst_ref=dst, send_sem=send_sems.at[i], recv_sem=recv_sems.at[i], device_id=dev, device_id_type=MESH)
        for i, (src, dst, dev) in enumerate(plan(refs))
    ]


def _split_start(name, arrays, plan, n):
    k = len(arrays)

    def body(*refs):
        for cp in _copies(plan, refs[:k], refs[k], refs[k + 1]):
            cp.start()
        refs[-1][...] = jnp.zeros_like(refs[-1])

    out = pl.pallas_call(
        body,
        name=name,
        out_shape=(pltpu.SemaphoreType.DMA((n,)), pltpu.SemaphoreType.DMA((n,)), *[pltpu.HBM(a.shape, a.dtype) for a in arrays], jax.ShapeDtypeStruct((8, 128), F32)),
        in_specs=[HBM] * k,
        out_specs=(SEM, SEM, *[HBM] * k, pl.BlockSpec(memory_space=pltpu.VMEM)),
        input_output_aliases={i: 2 + i for i in range(k)},
        compiler_params=pltpu.CompilerParams(has_side_effects=EFFECT),
    )(*[pltpu.with_memory_space_constraint(a, pltpu.HBM) for a in arrays])
    return out[0], out[1], list(out[2 : 2 + k]), out[-1]


def _split_wait(name, send, recv, arrays, plan, after):
    k = len(arrays)

    def body(*refs):
        for cp in _copies(plan, refs[:k], refs[k], refs[k + 1]):
            cp.wait_send()
            cp.wait_recv()

    out = pl.pallas_call(
        body,
        name=name,
        out_shape=tuple(pltpu.HBM(a.shape, a.dtype) for a in arrays),
        in_specs=[HBM] * k + [SEM, SEM, ANY],
        out_specs=tuple([HBM] * k),
        input_output_aliases={i: i for i in range(k)},
        compiler_params=pltpu.CompilerParams(has_side_effects=EFFECT),
    )(*arrays, send, recv, after)
    return list(out)


def _cast_into(w, me, axis, name, dep):
    r, c = w.shape
    tr = _tile(r, (256, 176))
    nr = r // tr

    def body(me_ref, w_ref, dep_ref, o_ref):
        o_ref[...] = w_ref[...].astype(BF16)

    if axis == 1:
        shape, o_spec = (r, N_DEV * c), pl.BlockSpec((tr, c), lambda i, me: (i, me[0]))
    else:
        shape, o_spec = (N_DEV * r, c), pl.BlockSpec((tr, c), lambda i, me: (me[0] * nr + i, 0))
    return pl.pallas_call(
        body,
        name=name,
        grid_spec=pltpu.PrefetchScalarGridSpec(
            num_scalar_prefetch=1, grid=(nr,), in_specs=[pl.BlockSpec((tr, c), lambda i, me: (i, 0)), ANY], out_specs=o_spec
        ),
        out_shape=jax.ShapeDtypeStruct(shape, BF16),
        compiler_params=_params("parallel"),
    )(me, w, dep)


LATE = ["w_out", "w_up", "w_down", "w_ple_gate", "w_ple_proj"]
GROUPS = [["w_ple_proj", "w_ple_gate", "w_down"], ["w_up"], ["w_out"], ["w_in"]]
STAGES = ["conv_transpose", "d_mix_out", "hgrn_bwd", "d_norm_mix_out"]


class _Exchange:
    def __init__(self, big, position):
        self.big, self.axis = big, dict(BIG)
        self.size = {k: big[k].shape[self.axis[k]] for k in big}
        self.x, self.y, self.c = position
        chips = [(self.x, self.y)] + _other_chips(self.x, self.y)
        self.ids = jnp.stack([4 * cx + 2 * cy + self.c for cx, cy in chips] + [2 * cx + cy for cx, cy in chips]).astype(jnp.int32)
        self.token, self.grads, self.state, self.wfull = None, {}, {}, {}


    def _slot(self, ref, k, dev):
        return _shard_of(ref, self.axis[k], self.size[k], dev)

    def _plan_gather(self, refs):
        x, y, c = _position()
        me, out = (x, y, c), []
        for k, ref in zip(LATE, refs):
            mine = self._slot(ref, k, me)
            out.append((mine, mine, (x, y, 1 - c)))
            out += [(mine, mine, (*chip, c)) for chip in _other_chips(x, y)]
        return out

    def _plan_forward(self, refs):
        x, y, c = _position()
        out = []
        for k, ref in zip(LATE, refs):
            for chip in _other_chips(x, y):
                block = self._slot(ref, k, (*chip, c))
                out.append((block, block, (x, y, 1 - c)))
        return out

    def _plan_sibling(self, names, refs):
        x, y, c = _position()
        n = len(names)
        return [(self._slot(refs[i], k, (p // 2, p % 2, 1 - c)), refs[n + i].at[p], (x, y, 1 - c)) for i, k in enumerate(names) for p in range(4)]

    def _plan_chips(self, names, refs):
        x, y, c = _position()
        n = len(names)
        return [(refs[i].at[j], refs[n + i].at[j], (*chip, c)) for i in range(n) for j, chip in enumerate(_other_chips(x, y))]


    def gather(self, conv_w):
        w_in, conv_full = _all_gather([_cast_bf16(self.big["w_in"], "cast_w_in"), conv_w], [1, 1])
        self.wfull["w_in"] = w_in
        me = (4 * self.x + 2 * self.y + self.c).astype(jnp.int32).reshape(1)
        fulls = [_cast_into(self.big[k], me, self.axis[k], "cast_" + k, w_in) for k in LATE]
        send, recv, fulls, self.token = _split_start("gather_start", fulls, self._plan_gather, 4 * len(LATE))
        self.late = (send, recv, fulls)
        return conv_full

    def weight(self, k):
        return self.wfull[k]

    def dep(self):
        token, self.token = self.token, None
        return token

    def grad(self, k, g):
        self.grads[k] = g
        for gi, names in enumerate(GROUPS):
            if k == names[-1]:
                plan = functools.partial(self._plan_sibling, names)
                lands = [lax.empty((4, *self._shard_shape(n)), F32) for n in names]
                send, recv, arrays, self.token = _split_start(f"sibling_start_{gi}", [self.grads[n] for n in names] + lands, plan, 4 * len(names))
                self.state[gi] = (send, recv, arrays, plan)

    def done(self, stage, after):
        if stage == "hgrn_fwd":
            send, recv, fulls = self.late
            fulls = _split_wait("gather_wait", send, recv, fulls, self._plan_gather, after)
            send, recv, fulls, self.token = _split_start("forward_start", fulls, self._plan_forward, 3 * len(LATE))
            self.late = (send, recv, fulls)
        elif stage == "att_fwd":
            send, recv, fulls = self.late
            self.wfull.update(zip(LATE, _split_wait("forward_wait", send, recv, fulls, self._plan_forward, after)))
        elif stage in STAGES:
            self._to_chips(STAGES.index(stage), after)

    def _shard_shape(self, k):
        shape = list(self.grads[k].shape)
        shape[self.axis[k]] = self.size[k]
        return tuple(shape)

    def _to_chips(self, gi, after):
        names = GROUPS[gi]
        n = len(names)
        send, recv, arrays, plan = self.state[gi]
        arrays = _split_wait(f"sibling_wait_{gi}", send, recv, arrays, plan, after)
        own, parts = [], []
        for k, g, land in zip(names, arrays[:n], arrays[n:]):
            own.append(_add_blocks(self.ids, g, land, self.axis[k], self.size[k], [0], F32, "add_own_" + k)[0])
            parts.append(_add_blocks(self.ids, g, land, self.axis[k], self.size[k], [1, 2, 3], BF16, "add_send_" + k))
        plan = functools.partial(self._plan_chips, names)
        lands = [lax.empty(part.shape, BF16) for part in parts]
        send, recv, arrays, self.token = _split_start(f"chips_start_{gi}", parts + lands, plan, 3 * n)
        self.state[gi] = (send, recv, arrays, plan, own)

    def finish(self, gi, after):
        names = GROUPS[gi]
        send, recv, arrays, plan, own = self.state[gi]
        arrays = _split_wait(f"chips_wait_{gi}", send, recv, arrays, plan, after)
        return {k: (o, r) for k, o, r in zip(names, own, arrays[len(names) :])}


class _Resident:
    def __init__(self, wfull):
        self.wfull, self.grads = wfull, {}

    def weight(self, k):
        return self.wfull[k]

    def grad(self, k, g):
        self.grads[k] = g

    def dep(self):
        return None

    def done(self, stage, after):
        pass


def _local_step(x, p, target, small, conv_w, ex):
    a1, r1 = _rms_fwd(x, small["norm_mix"], "norm_mix_fwd", dep=ex.dep())
    proj = _matmul(a1, ex.weight("w_in"), "nn", F32, "in_proj")
    bias = _bias_table(jnp.pad(small["rel_bias"], ((0, 0), (0, N_REL_PAD - N_REL))))
    y_hg, o_hg, states = _hgrn_fwd(proj, small["lb_logits"], small["hg_norm"])
    ex.done("hgrn_fwd", y_hg)
    y_att = _att_fwd(proj, bias, dep=ex.dep())
    ex.done("att_fwd", y_att)
    ycat = jnp.concatenate([y_hg, y_att], axis=1)
    h1 = _matmul(ycat, ex.weight("w_out"), "nn", F32, "out_proj", resid=x)
    a2, r2 = _rms_fwd(h1, small["norm_ffn"], "norm_ffn_fwd")
    u = _matmul(a2, ex.weight("w_up"), "nn", BF16, "up_proj")
    z = _ffn_act_fwd(u, conv_w, small["conv_b"])
    h2 = _matmul(z, ex.weight("w_down"), "nn", F32, "down_proj", tk=2816, resid=h1)
    a3, r3 = _rms_fwd(h2, small["norm_ple"], "norm_ple_fwd")
    gpre = _matmul(a3, ex.weight("w_ple_gate"), "nn", F32, "ple_gate")
    pp = _matmul(p, ex.weight("w_ple_proj"), "nn", F32, "ple_proj")
    dh3, dgpre, dpp, d_final, loss = _ple_loss(gpre, pp, h2, small["final_norm"], target)

    ex.grad("w_ple_proj", _matmul(p, dpp, "tn", F32, "d_w_ple_proj", tk=2048))
    ex.grad("w_ple_gate", _matmul(a3, dgpre, "tn", F32, "d_w_ple_gate", tk=2048))
    da3 = _matmul(dgpre, ex.weight("w_ple_gate"), "nt", F32, "d_norm_ple_out")
    dh2, d_ple = _rms_bwd(da3, h2, r3, small["norm_ple"], dh3, "norm_ple_bwd")
    dz = _matmul(dh2, ex.weight("w_down"), "nt", BF16, "d_ffn_act")
    ex.grad("w_down", _matmul(z, dh2, "tn", F32, "d_w_down", tk=2048))
    dc, dcw, dcb = _ffn_act_bwd(u, dz, conv_w, small["conv_b"], dep=ex.dep())
    du = _conv_transpose(dc, conv_w)
    ex.done("conv_transpose", du)
    d_conv_w = jnp.concatenate([dcw[0], dcw[1]], axis=1)
    d_conv_b = jnp.concatenate([dcb[0], dcb[1]], axis=1)
    ex.grad("w_up", _matmul(a2, du, "tn", F32, "d_w_up", tk=2048, dep=ex.dep()))
    da2 = _matmul(du, ex.weight("w_up"), "nt", F32, "d_norm_ffn_out", tk=2816, dep=ex.dep())
    dh1, d_ffn = _rms_bwd(da2, h1, r2, small["norm_ffn"], dh2, "norm_ffn_bwd")
    dycat = _matmul(dh1, ex.weight("w_out"), "nt", F32, "d_mix_out")
    ex.done("d_mix_out", dycat)
    ex.grad("w_out", _matmul(ycat, dh1, "tn", F32, "d_w_out", tk=2048, dep=ex.dep()))
    dp_hg, d_lb, d_hgn = _hgrn_bwd(proj, small["lb_logits"], small["hg_norm"], o_hg, dycat, states, dep=ex.dep())
    ex.done("hgrn_bwd", d_lb)
    dq_att, dk_att, dv_att, gsum = _att_bwd(proj, bias, dycat, dep=ex.dep())
    d_rel = _rel_bias_grad(gsum)
    dproj = jnp.concatenate([dp_hg[0], dp_hg[1], dp_hg[2], dp_hg[3], dq_att, dk_att, dv_att], axis=1)
    ex.grad("w_in", _matmul(a1, dproj, "tn", F32, "d_w_in", tk=2048))
    da1 = _matmul(dproj, ex.weight("w_in"), "nt", F32, "d_norm_mix_out", tk=1792, dep=ex.dep())
    ex.done("d_norm_mix_out", da1)
    dx, d_mix = _rms_bwd(da1, x, r1, small["norm_mix"], dh1, "norm_mix_bwd", dep=ex.dep())
    d_small = {
        "norm_mix": d_mix, "lb_logits": d_lb, "hg_norm": d_hgn, "rel_bias": d_rel, "norm_ffn": d_ffn,
        "conv_b": d_conv_b, "norm_ple": d_ple, "final_norm": d_final,
    }
    return loss, dx, d_small, d_conv_w


def kernel(x, p, norm_mix, w_in, lb_logits, hg_norm, rel_bias, w_out, norm_ffn, w_up, conv_w, conv_b, w_down, norm_ple, w_ple_gate, w_ple_proj, final_norm, loss_target, m_norm_mix, m_w_in, m_lb_logits, m_hg_norm, m_rel_bias, m_w_out, m_norm_ffn, m_w_up, m_conv_w, m_conv_b, m_w_down, m_norm_ple, m_w_ple_gate, m_w_ple_proj, m_final_norm, v_norm_mix, v_w_in, v_lb_logits, v_hg_norm, v_rel_bias, v_w_out, v_norm_ffn, v_w_up, v_conv_w, v_conv_b, v_w_down, v_norm_ple, v_w_ple_gate, v_w_ple_proj, v_final_norm):
    given = dict(locals())
    mx, my, mc = _position()
    me = 4 * mx + 2 * my + mc
    big = {k: given[k][0] for k, _ in BIG}
    ex = _Exchange(big, (mx, my, mc))
    conv_w_full = ex.gather(conv_w[0])

    small = {
        "norm_mix": norm_mix, "lb_logits": lb_logits, "hg_norm": hg_norm, "rel_bias": rel_bias[0], "norm_ffn": norm_ffn,
        "conv_b": conv_b, "norm_ple": norm_ple, "final_norm": final_norm.reshape(1, -1),
    }
    loss, dx, d_small, d_conv_w = _local_step(x[0], p[0, 0], loss_target[0], small, conv_w_full, ex)

    packed = jnp.concatenate([_pack_small(d_small), _rows128(d_conv_w, CONV_W_FULL_ROWS), _rows128(loss[0:1, 0:1], 8)], axis=0)
    reduced = _all_reduce_small(packed)

    out = {}
    for gi in range(len(GROUPS)):
        for k, (o, r) in ex.finish(gi, reduced).items():
            g, d, nm, nv = _adam_big(big[k], given["m_" + k][0], given["v_" + k][0], o, r, "adam_" + k)
            out[k] = tuple(a[None] for a in (g, d, nm, nv))
    shapes = {k: given[k].shape for k, _ in SMALL}
    g_small, at = _unpack_small(reduced, shapes)
    g_conv_full = reduced[at : at + CONV_W_FULL_ROWS].reshape(3, 2 * D_FF)
    total_loss = reduced[at + CONV_W_FULL_ROWS, 0]
    cw = conv_w.shape[2]
    g_conv = lax.dynamic_slice_in_dim(g_conv_full, me * cw, cw, axis=1)

    def pack_with_conv(parts, conv_part):
        return jnp.concatenate([_pack_small(parts), _rows128(conv_part, CONV_W_SHARD_ROWS)], axis=0)

    d_pk, m_pk, v_pk = _adam_small(
        pack_with_conv({k: given[k] for k, _ in SMALL}, conv_w),
        pack_with_conv(g_small, g_conv),
        pack_with_conv({k: given["m_" + k] for k, _ in SMALL}, m_conv_w),
        pack_with_conv({k: given["v_" + k] for k, _ in SMALL}, v_conv_w),
    )
    for name, pk in (("d", d_pk), ("m", m_pk), ("v", v_pk)):
        parts, at = _unpack_small(pk, shapes)
        parts["conv_w"] = pk[at : at + CONV_W_SHARD_ROWS].reshape(-1)[: 3 * cw].reshape(conv_w.shape)
        for k, a in parts.items():
            out.setdefault(k, {})
            out[k][name] = a
    for k, _ in SMALL:
        out[k]["g"] = g_small[k]
    out["conv_w"]["g"] = g_conv.reshape(conv_w.shape)

    order = ["norm_mix", "w_in", "lb_logits", "hg_norm", "rel_bias", "w_out", "norm_ffn", "w_up", "conv_w", "conv_b", "w_down", "norm_ple", "w_ple_gate", "w_ple_proj", "final_norm"]

    def pick(k, what):
        return out[k][what] if isinstance(out[k], dict) else out[k][{"g": 0, "d": 1, "m": 2, "v": 3}[what]]

    return (total_loss, dx[None], *[pick(k, "g") for k in order], *[pick(k, "d") for k in order], *[pick(k, "m") for k in order], *[pick(k, "v") for k in order])
```

```python
import functools

import jax
import jax.numpy as jnp
from jax import lax
from jax.experimental import pallas as pl
from jax.experimental.pallas import tpu as pltpu

F32 = jnp.float32
BF16 = jnp.bfloat16

D_MODEL = 2048
CHUNK = 64
HG_HEADS = 8
HEAD_DIM = 128
HG_WIDTH = HG_HEADS * HEAD_DIM
ATT_HEADS = 8
ATT_WIDTH = ATT_HEADS * HEAD_DIM
LEFT_CHUNKS = 8
PAD = LEFT_CHUNKS * CHUNK
BAND = PAD + CHUNK
REL_CLIP = 128
N_REL = 2 * REL_CLIP + 1
N_REL_PAD = 384
D_FF = 5632
EPS = 1e-6
ATT_SCALE = HEAD_DIM ** -0.5
SUB = 32
HG_BLOCK = 4
Q_BLOCK = 4 * CHUNK
K_BLOCK = Q_BLOCK + PAD
DIAG = 1024

ADAM_LR = 0.001
ADAM_B1 = 0.9
ADAM_B2 = 0.999
ADAM_EPS = 1e-08
ADAM_WD = 0.01
ADAM_STEP = 10

N_DEV = 8
VMEM_LIMIT = 48 * 1024 * 1024
MESH = pl.DeviceIdType.MESH
ANY = pl.BlockSpec(memory_space=pl.ANY)
HIGHEST = lax.Precision.HIGHEST

NN = (((1,), (0,)), ((), ()))
NT = (((1,), (1,)), ((), ()))
TN = (((0,), (0,)), ((), ()))


def _params(*sem):
    return pltpu.CompilerParams(dimension_semantics=sem if sem else None, vmem_limit_bytes=VMEM_LIMIT)


def _pallas(body, n_in, dep, **kw):
    if dep is None:
        return pl.pallas_call(body, **kw)

    def body_after(*refs):
        body(*refs[:n_in], *refs[n_in + 1 :])

    call = pl.pallas_call(body_after, **dict(kw, in_specs=list(kw["in_specs"]) + [ANY]))
    return lambda *ops: call(*ops, dep)


def _dot(a, b, dims=NN):
    return lax.dot_general(a, b, dims, preferred_element_type=F32)


def _dot3(a, b, dims=NN):
    a_hi, b_hi = a.astype(BF16), b.astype(BF16)
    a_lo, b_lo = (a - a_hi.astype(F32)).astype(BF16), (b - b_hi.astype(F32)).astype(BF16)
    return _dot(a_hi, b_hi, dims) + (_dot(a_hi, b_lo, dims) + _dot(a_lo, b_hi, dims))


def _sigmoid(x):
    return 1.0 / (1.0 + jnp.exp(-x))


def _tile(n, prefs):
    for t in prefs:
        if n % t == 0:
            return t
    return n


def _matmul(a, b, mode, out_dtype, name, tm=512, tn=1024, tk=None, resid=None, dep=None):
    if mode == "nn":
        (m, k), n = a.shape, b.shape[1]
    elif mode == "nt":
        (m, k), n = a.shape, b.shape[0]
    else:
        (k, m), n = a.shape, b.shape[1]
    tm = _tile(m, (tm, 256, 128))
    tn = _tile(n, (tn, 512, 256, 128))
    tk = k if tk is None else _tile(k, (tk,))
    nk = k // tk
    dims = {"nn": NN, "nt": NT, "tn": TN}[mode]
    a_spec = pl.BlockSpec((tk, tm), lambda i, j, s: (s, i)) if mode == "tn" else pl.BlockSpec((tm, tk), lambda i, j, s: (i, s))
    b_spec = pl.BlockSpec((tn, tk), lambda i, j, s: (j, s)) if mode == "nt" else pl.BlockSpec((tk, tn), lambda i, j, s: (s, j))
    o_spec = pl.BlockSpec((tm, tn), lambda i, j, s: (i, j))
    has_res = resid is not None

    def body(*refs):
        a_ref, b_ref = refs[0], refs[1]
        o_ref = refs[2 + has_res]
        part = _dot(a_ref[...].astype(BF16), b_ref[...].astype(BF16), dims)

        def finish(acc):
            if has_res:
                acc = acc + refs[2][...]
            o_ref[...] = acc.astype(out_dtype)

        if nk == 1:
            finish(part)
        else:
            acc_ref = refs[-1]
            s = pl.program_id(2)

            @pl.when(s == 0)
            def _():
                acc_ref[...] = part

            @pl.when(s > 0)
            def _():
                acc_ref[...] += part

            @pl.when(s == nk - 1)
            def _():
                finish(acc_ref[...])

    return _pallas(
        body,
        2 + has_res,
        dep,
        name=name,
        grid=(m // tm, n // tn, nk),
        in_specs=[a_spec, b_spec] + ([o_spec] if has_res else []),
        out_specs=o_spec,
        out_shape=jax.ShapeDtypeStruct((m, n), out_dtype),
        scratch_shapes=[pltpu.VMEM((tm, tn), F32)] if nk > 1 else [],
        compiler_params=_params("parallel", "parallel", "arbitrary"),
    )(*([a, b] + ([resid] if has_res else [])))


def _rms_fwd(x, g, name, dep=None):
    t, d = x.shape
    tm = _tile(t, (256,))

    def body(x_ref, g_ref, a_ref, r_ref):
        xv = x_ref[...]
        r = lax.rsqrt(jnp.mean(xv * xv, axis=-1, keepdims=True) + EPS)
        a_ref[...] = (xv * r * g_ref[...]).astype(BF16)
        r_ref[...] = r

    row = pl.BlockSpec((tm, d), lambda i: (i, 0))
    return _pallas(
        body,
        2,
        dep,
        name=name,
        grid=(t // tm,),
        in_specs=[row, pl.BlockSpec((1, d), lambda i: (0, 0))],
        out_specs=[row, pl.BlockSpec((tm, 1), lambda i: (i, 0))],
        out_shape=[jax.ShapeDtypeStruct((t, d), BF16), jax.ShapeDtypeStruct((t, 1), F32)],
        compiler_params=_params("parallel"),
    )(x, g)


def _rms_bwd(da, x, r, g, resid, name, dep=None):
    t, d = x.shape
    tm = _tile(t, (256,))

    def body(da_ref, x_ref, r_ref, g_ref, res_ref, dx_ref, dg_ref):
        i = pl.program_id(0)
        rv = r_ref[...]
        n = x_ref[...] * rv
        dav = da_ref[...]
        dn = dav * g_ref[...]
        dx_ref[...] = rv * (dn - n * jnp.mean(dn * n, axis=-1, keepdims=True)) + res_ref[...]
        part = jnp.sum(dav * n, axis=0, keepdims=True)

        @pl.when(i == 0)
        def _():
            dg_ref[...] = part

        @pl.when(i > 0)
        def _():
            dg_ref[...] += part

    row = pl.BlockSpec((tm, d), lambda i: (i, 0))
    vec = pl.BlockSpec((1, d), lambda i: (0, 0))
    return _pallas(
        body,
        5,
        dep,
        name=name,
        grid=(t // tm,),
        in_specs=[row, row, pl.BlockSpec((tm, 1), lambda i: (i, 0)), vec, row],
        out_specs=[row, vec],
        out_shape=[jax.ShapeDtypeStruct((t, d), F32), jax.ShapeDtypeStruct((1, d), F32)],
        compiler_params=_params("arbitrary"),
    )(da, x, r, g, resid)


def _tri(n, upper):
    r = lax.broadcasted_iota(jnp.int32, (n, n), 0)
    c = lax.broadcasted_iota(jnp.int32, (n, n), 1)
    return jnp.where((c >= r) if upper else (c <= r), 1.0, 0.0).astype(F32)


def _hgrn_gates(q, fp, lbl):
    l0, l1 = lbl[0:1, :], lbl[1:2, :]
    mx = jnp.maximum(l0, l1)
    e0, e1 = jnp.exp(l0 - mx), jnp.exp(l1 - mx)
    lb = e0 / (e0 + e1)
    sig = _sigmoid(fp)
    f = lb + (1.0 - lb) * sig
    kk = (1.0 - lb) * _sigmoid(-fp)
    sq = _sigmoid(q)
    b = jnp.dot(_tri(CHUNK, False), jnp.log(f), precision=HIGHEST, preferred_element_type=F32)
    return lb, sig, f, kk, sq, q * sq, b


def _heads(x):
    return [x[:, j * HEAD_DIM : (j + 1) * HEAD_DIM] for j in range(x.shape[1] // HEAD_DIM)]


def _wide(parts):
    return jnp.concatenate(parts, axis=1)


def _intra_blocks(b):
    out = []
    for lo in range(0, CHUNK, SUB):
        hi = lo + SUB
        br = b[lo + SUB // 2 : lo + SUB // 2 + 1, :]
        row = lax.broadcasted_iota(jnp.int32, (SUB, hi), 0) + lo
        col = lax.broadcasted_iota(jnp.int32, (SUB, hi), 1)
        out.append((lo, hi, jnp.exp(b[lo:hi] - br), jnp.exp(br - b[:hi]), col <= row))
    return out


def _hgrn_fwd(proj, lb_logits, hg_norm):
    t = proj.shape[0]
    nc = t // CHUNK

    def body(q_ref, f_ref, i_ref, g_ref, lbl_ref, hgn_ref, y_ref, o_ref, st_ref, s_scr):
        c = pl.program_id(1)

        @pl.when(c == 0)
        def _():
            s_scr[...] = jnp.zeros_like(s_scr)

        hs = range(HG_BLOCK)
        sts = [s_scr[j] for j in hs]
        _, _, _, kk, _, qf, b = _hgrn_gates(q_ref[...], f_ref[...], lbl_ref[...])
        vb = _heads(i_ref[...].astype(BF16))
        bl = b[CHUNK - 1 : CHUNK, :]
        qe = _heads((qf * jnp.exp(b)).astype(BF16))
        kd = _heads((kk * jnp.exp(bl - b)).astype(BF16))
        decay = _heads(jnp.exp(bl))
        o = [_dot(qe[j], sts[j].astype(BF16), NT) for j in hs]
        parts = [[] for _ in hs]
        for lo, hi, ea, eb, mask in _intra_blocks(b):
            a, bk = _heads((qf[lo:hi] * ea).astype(BF16)), _heads((kk[:hi] * eb).astype(BF16))
            p = [jnp.where(mask, _dot(a[j], bk[j], NT), 0.0).astype(BF16) for j in hs]
            for j in hs:
                parts[j].append(_dot(p[j], vb[j][:hi]))
        o = [o[j] + jnp.concatenate(parts[j], axis=0) for j in hs]
        new = [sts[j] * decay[j] + _dot(vb[j], kd[j], TN) for j in hs]
        hgn = hgn_ref[...]
        on = [o[j] * lax.rsqrt(jnp.mean(o[j] * o[j], axis=-1, keepdims=True) + EPS) * hgn for j in hs]
        gg = g_ref[...]
        for j in hs:
            st_ref[j] = sts[j]
            s_scr[j] = new[j]
        o_ref[...] = _wide(o)
        y_ref[...] = (_wide(on) * (gg * _sigmoid(gg))).astype(BF16)

    wide = HG_BLOCK * HEAD_DIM
    groups = HG_HEADS // HG_BLOCK

    def col(k):
        return pl.BlockSpec((CHUNK, wide), lambda g, c: (c, k * groups + g))

    out = pl.BlockSpec((CHUNK, wide), lambda g, c: (c, g))
    return pl.pallas_call(
        body,
        name="hgrn_fwd",
        grid=(groups, nc),
        in_specs=[col(0), col(1), col(2), col(3), pl.BlockSpec((2, wide), lambda g, c: (0, g)), pl.BlockSpec((1, HEAD_DIM), lambda g, c: (0, 0))],
        out_specs=[out, out, pl.BlockSpec((HG_BLOCK, None, HEAD_DIM, HEAD_DIM), lambda g, c: (g, c, 0, 0))],
        out_shape=[
            jax.ShapeDtypeStruct((t, HG_WIDTH), BF16),
            jax.ShapeDtypeStruct((t, HG_WIDTH), F32),
            jax.ShapeDtypeStruct((HG_HEADS, nc, HEAD_DIM, HEAD_DIM), F32),
        ],
        scratch_shapes=[pltpu.VMEM((HG_BLOCK, HEAD_DIM, HEAD_DIM), F32)],
        compiler_params=_params("arbitrary", "arbitrary"),
    )(proj, proj, proj, proj, lb_logits, hg_norm)


def _hgrn_bwd(proj, lb_logits, hg_norm, o_hg, dycat, states, dep=None):
    t = proj.shape[0]
    nc = t // CHUNK

    def body(q_ref, f_ref, i_ref, g_ref, lbl_ref, hgn_ref, o_ref, dy_ref, st_ref, dp_ref, dlbl_ref, dhgn_ref, dst_scr, dlb_scr):
        h = pl.program_id(0)
        c = pl.program_id(1)

        @pl.when(c == 0)
        def _():
            dst_scr[...] = jnp.zeros_like(dst_scr)
            dlb_scr[...] = jnp.zeros_like(dlb_scr)

        @pl.when((c == 0) & (h == 0))
        def _():
            dhgn_ref[...] = jnp.zeros_like(dhgn_ref)

        hs = range(HG_BLOCK)
        hgn = _wide([hgn_ref[...]] * HG_BLOCK)
        q, fp, gg, vi = q_ref[...], f_ref[...], g_ref[...], i_ref[...]
        lb, sig, f, kk, sq, qf, b = _hgrn_gates(q, fp, lbl_ref[...])
        o, dy = o_ref[...], dy_ref[...]
        sg = _sigmoid(gg)
        n = _wide([oh * lax.rsqrt(jnp.mean(oh * oh, axis=-1, keepdims=True) + EPS) for oh in _heads(o)])
        don = dy * (gg * sg)
        dgg = dy * (n * hgn) * (sg * (1.0 + gg * (1.0 - sg)))
        d_hgn = sum(_heads(jnp.sum(don * n, axis=0, keepdims=True)))
        dn = don * hgn
        do = _wide(
            [
                lax.rsqrt(jnp.mean(oh * oh, axis=-1, keepdims=True) + EPS) * (dnh - nh * jnp.mean(dnh * nh, axis=-1, keepdims=True))
                for oh, dnh, nh in zip(_heads(o), _heads(dn), _heads(n))
            ]
        )
        sts = [st_ref[j] for j in hs]
        dstn = [dst_scr[j] for j in hs]
        bl = b[CHUNK - 1 : CHUNK, :]
        e_b, e_bl, e_l = jnp.exp(b), jnp.exp(bl - b), jnp.exp(bl)
        doh, vih = _heads(do), _heads(vi)
        dobh = _heads(do.astype(BF16))
        dq_acc = _wide([_dot3(doh[j], sts[j]) for j in hs]) * e_b
        dk_inter = _wide([_dot3(vih[j], dstn[j]) for j in hs]) * e_bl
        dk_acc = dk_inter
        kd = _heads((kk * e_bl).astype(BF16))
        dv_acc = _wide([_dot(kd[j], dstn[j].astype(BF16), NT) for j in hs])
        qe, decay = _heads((qf * e_b).astype(BF16)), _heads(e_l)
        dst_new = [dstn[j] * decay[j] + _dot(dobh[j], qe[j], TN) for j in hs]
        db_last = e_l * _wide([jnp.sum(sts[j] * dstn[j], axis=0, keepdims=True) for j in hs]) + jnp.sum(kk * dk_inter, axis=0, keepdims=True)
        dq_parts = []
        for lo, hi, ea, eb, mask in _intra_blocks(b):
            a, bk = qf[lo:hi] * ea, kk[:hi] * eb
            ah, bkh = _heads(a), _heads(bk)
            abh, bkbh = _heads(a.astype(BF16)), _heads(bk.astype(BF16))
            p = [jnp.where(mask, _dot(abh[j], bkbh[j], NT), 0.0).astype(BF16) for j in hs]
            dp = [jnp.where(mask, _dot3(doh[j][lo:hi], vih[j][:hi], NT), 0.0) for j in hs]
            dq_parts.append(_wide([_dot3(dp[j], bkh[j]) for j in hs]) * ea)
            dki = _wide([_dot3(dp[j], ah[j], TN) for j in hs]) * eb
            dvi = _wide([_dot(p[j], dobh[j][lo:hi], TN) for j in hs])
            if hi < CHUNK:
                zeros = jnp.zeros((CHUNK - hi, HG_BLOCK * HEAD_DIM), F32)
                dki = jnp.concatenate([dki, zeros], axis=0)
                dvi = jnp.concatenate([dvi, zeros], axis=0)
            dk_acc = dk_acc + dki
            dv_acc = dv_acc + dvi
        dq_acc = dq_acc + jnp.concatenate(dq_parts, axis=0)
        rows = lax.broadcasted_iota(jnp.int32, dq_acc.shape, 0)
        db = qf * dq_acc - kk * dk_acc + jnp.where(rows == CHUNK - 1, db_last, 0.0)
        dlf = jnp.dot(_tri(CHUNK, True), db, precision=HIGHEST, preferred_element_type=F32)
        dfk = dlf / f - dk_acc
        dp_ref[0] = (dq_acc * (sq * (1.0 + q * (1.0 - sq)))).astype(BF16)
        dp_ref[1] = ((1.0 - lb) * dfk * sig * (1.0 - sig)).astype(BF16)
        dp_ref[2] = dv_acc.astype(BF16)
        dp_ref[3] = dgg.astype(BF16)
        dlb_scr[...] += jnp.sum(dfk * (1.0 - sig), axis=0, keepdims=True)
        dhgn_ref[...] += d_hgn
        for j in hs:
            dst_scr[j] = dst_new[j]

        @pl.when(c == nc - 1)
        def _():
            dl0 = dlb_scr[...] * lb * (1.0 - lb)
            dlbl_ref[0:1, :] = dl0
            dlbl_ref[1:2, :] = -dl0

    wide = HG_BLOCK * HEAD_DIM
    groups = HG_HEADS // HG_BLOCK

    def col(k):
        return pl.BlockSpec((CHUNK, wide), lambda g, c: (nc - 1 - c, k * groups + g))

    blk = pl.BlockSpec((CHUNK, wide), lambda g, c: (nc - 1 - c, g))
    return _pallas(
        body,
        9,
        dep,
        name="hgrn_bwd",
        grid=(groups, nc),
        in_specs=[
            col(0), col(1), col(2), col(3),
            pl.BlockSpec((2, wide), lambda g, c: (0, g)),
            pl.BlockSpec((1, HEAD_DIM), lambda g, c: (0, 0)),
            blk, blk,
            pl.BlockSpec((HG_BLOCK, None, HEAD_DIM, HEAD_DIM), lambda g, c: (g, nc - 1 - c, 0, 0)),
        ],
        out_specs=[
            pl.BlockSpec((4, CHUNK, wide), lambda g, c: (0, nc - 1 - c, g)),
            pl.BlockSpec((2, wide), lambda g, c: (0, g)),
            pl.BlockSpec((1, HEAD_DIM), lambda g, c: (0, 0)),
        ],
        out_shape=[
            jax.ShapeDtypeStruct((4, t, HG_WIDTH), BF16),
            jax.ShapeDtypeStruct((2, HG_WIDTH), F32),
            jax.ShapeDtypeStruct((1, HEAD_DIM), F32),
        ],
        scratch_shapes=[pltpu.VMEM((HG_BLOCK, HEAD_DIM, HEAD_DIM), F32), pltpu.VMEM((1, wide), F32)],
        compiler_params=_params("arbitrary", "arbitrary"),
    )(proj, proj, proj, proj, lb_logits, hg_norm, o_hg, dycat, states)


def _diagonal_slots(shift):
    i = lax.broadcasted_iota(jnp.int32, (N_REL_PAD, DIAG), 0)
    u = lax.broadcasted_iota(jnp.int32, (N_REL_PAD, DIAG), 1)
    offset = u - shift if shift else jnp.where(u < K_BLOCK, u, u - DIAG)
    return jnp.where(jnp.clip(PAD - offset, -REL_CLIP, REL_CLIP) + REL_CLIP == i, 1.0, 0.0).astype(BF16)


def _split3(x):
    hi = x.astype(BF16)
    mid = (x - hi.astype(F32)).astype(BF16)
    return hi, mid, (x - hi.astype(F32) - mid.astype(F32)).astype(BF16)


def _bias_table(rel_bias):
    def body(rb_ref, o_ref, diag):
        h = pl.program_id(0)

        @pl.when(h == 0)
        def _():
            hi, mid, lo = _split3(rb_ref[...])
            slots = _diagonal_slots(0)
            diag[...] = _dot(hi, slots) + (_dot(mid, slots) + _dot(lo, slots))

        rows = jnp.broadcast_to(diag[pl.ds(h, 1), :], (Q_BLOCK, DIAG))
        o_ref[...] = pltpu.roll(rows, 0, 1, stride=1, stride_axis=0)[:, :K_BLOCK]

    return pl.pallas_call(
        body,
        name="bias_table",
        grid=(ATT_HEADS,),
        in_specs=[pl.BlockSpec((ATT_HEADS, N_REL_PAD), lambda h: (0, 0))],
        out_specs=pl.BlockSpec((None, Q_BLOCK, K_BLOCK), lambda h: (h, 0, 0)),
        out_shape=jax.ShapeDtypeStruct((ATT_HEADS, Q_BLOCK, K_BLOCK), F32),
        scratch_shapes=[pltpu.VMEM((ATT_HEADS, DIAG), F32)],
        compiler_params=_params("arbitrary"),
    )(rel_bias)


def _att_probs(q_ref, kpad, bias_ref, blk):
    qs = (q_ref[...] * ATT_SCALE).astype(BF16)
    start = pl.multiple_of(blk * Q_BLOCK, Q_BLOCK)
    kb = kpad[pl.ds(start, K_BLOCK), :]
    s = _dot(qs, kb, NT) + bias_ref[...]
    row = lax.broadcasted_iota(jnp.int32, (Q_BLOCK, K_BLOCK), 0)
    col = lax.broadcasted_iota(jnp.int32, (Q_BLOCK, K_BLOCK), 1)
    first = row - (row & (CHUNK - 1))
    valid = (col >= first) & (col < first + BAND) & (col + (blk * Q_BLOCK - PAD) >= 0)
    s = jnp.where(valid, s, jnp.finfo(F32).min)
    e = jnp.exp(s - jnp.max(s, axis=-1, keepdims=True))
    return qs, kb, start, e / jnp.sum(e, axis=-1, keepdims=True)


def _fill_padded(dst, src):
    dst[0:PAD, :] = jnp.zeros((PAD, HEAD_DIM), BF16)
    dst[PAD:, :] = src[...].astype(BF16)


def _att_fwd(proj, bias, dep=None):
    t = proj.shape[0]
    nb = t // Q_BLOCK

    def body(q_ref, k_ref, v_ref, bias_ref, y_ref, kpad, vpad):
        c = pl.program_id(1)

        @pl.when(c == 0)
        def _():
            _fill_padded(kpad, k_ref)
            _fill_padded(vpad, v_ref)

        _, _, start, p = _att_probs(q_ref, kpad, bias_ref, c)
        y_ref[...] = _dot(p.astype(BF16), vpad[pl.ds(start, K_BLOCK), :]).astype(BF16)

    base = 4 * HG_HEADS
    return _pallas(
        body,
        4,
        dep,
        name="att_fwd",
        grid=(ATT_HEADS, nb),
        in_specs=[
            pl.BlockSpec((Q_BLOCK, HEAD_DIM), lambda h, c: (c, base + h)),
            pl.BlockSpec((t, HEAD_DIM), lambda h, c: (0, base + ATT_HEADS + h)),
            pl.BlockSpec((t, HEAD_DIM), lambda h, c: (0, base + 2 * ATT_HEADS + h)),
            pl.BlockSpec((None, Q_BLOCK, K_BLOCK), lambda h, c: (h, 0, 0)),
        ],
        out_specs=pl.BlockSpec((Q_BLOCK, HEAD_DIM), lambda h, c: (c, h)),
        out_shape=jax.ShapeDtypeStruct((t, ATT_WIDTH), BF16),
        scratch_shapes=[pltpu.VMEM((t + PAD, HEAD_DIM), BF16), pltpu.VMEM((t + PAD, HEAD_DIM), BF16)],
        compiler_params=_params("arbitrary", "arbitrary"),
    )(proj, proj, proj, bias)


def _att_bwd(proj, bias, dycat, dep=None):
    t = proj.shape[0]
    nb = t // Q_BLOCK

    def body(q_ref, k_ref, v_ref, bias_ref, dy_ref, dq_ref, dk_ref, dv_ref, g_ref, kpad, vpad, dkacc, dvacc):
        c = pl.program_id(1)

        @pl.when(c == 0)
        def _():
            _fill_padded(kpad, k_ref)
            _fill_padded(vpad, v_ref)
            dkacc[...] = jnp.zeros_like(dkacc)
            dvacc[...] = jnp.zeros_like(dvacc)
            g_ref[...] = jnp.zeros_like(g_ref)

        qs, kb, start, p = _att_probs(q_ref, kpad, bias_ref, c)
        band = pl.ds(start, K_BLOCK)
        dyb = dy_ref[...].astype(BF16)
        dvacc[band, :] += _dot(p.astype(BF16), dyb, TN)
        dp = _dot(dyb, vpad[band, :], NT)
        ds = p * (dp - jnp.sum(dp * p, axis=-1, keepdims=True))
        g_ref[...] += ds
        dsb = ds.astype(BF16)
        dq_ref[...] = (_dot(dsb, kb) * ATT_SCALE).astype(BF16)
        dkacc[band, :] += _dot(dsb, qs, TN)

        @pl.when(c == nb - 1)
        def _():
            dk_ref[...] = dkacc[PAD:, :].astype(BF16)
            dv_ref[...] = dvacc[PAD:, :].astype(BF16)

    base = 4 * HG_HEADS
    whole = pl.BlockSpec((t, HEAD_DIM), lambda h, c: (0, h))
    return _pallas(
        body,
        5,
        dep,
        name="att_bwd",
        grid=(ATT_HEADS, nb),
        in_specs=[
            pl.BlockSpec((Q_BLOCK, HEAD_DIM), lambda h, c: (c, base + h)),
            pl.BlockSpec((t, HEAD_DIM), lambda h, c: (0, base + ATT_HEADS + h)),
            pl.BlockSpec((t, HEAD_DIM), lambda h, c: (0, base + 2 * ATT_HEADS + h)),
            pl.BlockSpec((None, Q_BLOCK, K_BLOCK), lambda h, c: (h, 0, 0)),
            pl.BlockSpec((Q_BLOCK, HEAD_DIM), lambda h, c: (c, HG_HEADS + h)),
        ],
        out_specs=[pl.BlockSpec((Q_BLOCK, HEAD_DIM), lambda h, c: (c, h)), whole, whole, pl.BlockSpec((None, Q_BLOCK, K_BLOCK), lambda h, c: (h, 0, 0))],
        out_shape=[
            jax.ShapeDtypeStruct((t, ATT_WIDTH), BF16),
            jax.ShapeDtypeStruct((t, ATT_WIDTH), BF16),
            jax.ShapeDtypeStruct((t, ATT_WIDTH), BF16),
            jax.ShapeDtypeStruct((ATT_HEADS, Q_BLOCK, K_BLOCK), F32),
        ],
        scratch_shapes=[
            pltpu.VMEM((t + PAD, HEAD_DIM), BF16),
            pltpu.VMEM((t + PAD, HEAD_DIM), BF16),
            pltpu.VMEM((t + PAD, HEAD_DIM), F32),
            pltpu.VMEM((t + PAD, HEAD_DIM), F32),
        ],
        compiler_params=_params("arbitrary", "arbitrary"),
    )(proj, proj, proj, bias, dycat)


def _rel_bias_grad(gsum):
    def body(g_ref, o_ref):
        r = lax.broadcasted_iota(jnp.int32, (Q_BLOCK, Q_BLOCK), 0)
        c = lax.broadcasted_iota(jnp.int32, (Q_BLOCK, Q_BLOCK), 1)
        flip = jnp.where(r + c == Q_BLOCK - 1, 1.0, 0.0).astype(BF16)
        sums = []
        for h in range(ATT_HEADS):
            hi, mid, lo = _split3(g_ref[h])
            rev = _dot(flip, hi) + (_dot(flip, mid) + _dot(flip, lo))
            wide = jnp.concatenate([rev, jnp.zeros((Q_BLOCK, DIAG - K_BLOCK), F32)], axis=1)
            sums.append(jnp.sum(pltpu.roll(wide, 0, 1, stride=1, stride_axis=0), axis=0, keepdims=True))
        hi, mid, lo = _split3(jnp.concatenate(sums, axis=0))
        slots = _diagonal_slots(Q_BLOCK - 1)
        o_ref[...] = _dot(hi, slots, NT) + (_dot(mid, slots, NT) + _dot(lo, slots, NT))

    return pl.pallas_call(
        body,
        name="rel_bias_grad",
        out_shape=jax.ShapeDtypeStruct((ATT_HEADS, N_REL_PAD), F32),
        compiler_params=_params(),
    )(gsum)


HALO = 16


def _ffn_tiles(t):
    tm = _tile(t, (512,))
    tc = 256
    return tm, tc, D_FF // tc


def _shift_down(x, halo, k, tm):
    rows = lax.broadcasted_iota(jnp.int32, x.shape, 0)
    out = jnp.where(rows >= k, pltpu.roll(x, k, 0), halo[HALO - 1 : HALO, :])
    if k == 2:
        out = jnp.where(rows == 0, halo[HALO - 2 : HALO - 1, :], out)
    return out


def _shift_up(x, halo, k, tm):
    rows = lax.broadcasted_iota(jnp.int32, x.shape, 0)
    out = jnp.where(rows < tm - k, pltpu.roll(x, tm - k, 0), halo[0:1, :])
    if k == 2:
        out = jnp.where(rows == tm - 1, halo[1:2, :], out)
    return out


def _conv_taps(u_ref, halo_ref, i, tm):
    u = u_ref[...].astype(F32)
    halo = jnp.where(i > 0, halo_ref[...].astype(F32), 0.0)
    return _shift_down(u, halo, 2, tm), _shift_down(u, halo, 1, tm), u


def _ffn_in_specs(tm, tc, nj):
    before = lambda off: pl.BlockSpec((HALO, tc), lambda j, i: (jnp.maximum(i * (tm // HALO) - 1, 0), off + j))
    tile = lambda off: pl.BlockSpec((tm, tc), lambda j, i: (i, off + j))
    vec = lambda rows, off: pl.BlockSpec((rows, tc), lambda j, i: (0, off + j))
    return tile, before, vec


def _ffn_act_fwd(u, conv_w, conv_b):
    t = u.shape[0]
    tm, tc, nj = _ffn_tiles(t)
    tile, before, vec = _ffn_in_specs(tm, tc, nj)

    def body(ug_ref, hg_ref, uv_ref, hv_ref, wg_ref, wv_ref, bg_ref, bv_ref, z_ref):
        i = pl.program_id(1)
        g2, g1, g0 = _conv_taps(ug_ref, hg_ref, i, tm)
        v2, v1, v0 = _conv_taps(uv_ref, hv_ref, i, tm)
        wg, wv = wg_ref[...], wv_ref[...]
        gate = bg_ref[...] + wg[0:1] * g2 + wg[1:2] * g1 + wg[2:3] * g0
        val = bv_ref[...] + wv[0:1] * v2 + wv[1:2] * v1 + wv[2:3] * v0
        z_ref[...] = (gate * _sigmoid(gate) * val).astype(BF16)

    return pl.pallas_call(
        body,
        name="ffn_act_fwd",
        grid=(nj, t // tm),
        in_specs=[tile(0), before(0), tile(nj), before(nj), vec(3, 0), vec(3, nj), vec(1, 0), vec(1, nj)],
        out_specs=pl.BlockSpec((tm, tc), lambda j, i: (i, j)),
        out_shape=jax.ShapeDtypeStruct((t, D_FF), BF16),
        compiler_params=_params("parallel", "parallel"),
    )(u, u, u, u, conv_w, conv_w, conv_b, conv_b)


def _ffn_act_bwd(u, dz, conv_w, conv_b, dep=None):
    t = u.shape[0]
    tm, tc, nj = _ffn_tiles(t)
    tile, before, vec = _ffn_in_specs(tm, tc, nj)

    def body(ug_ref, hg_ref, uv_ref, hv_ref, wg_ref, wv_ref, bg_ref, bv_ref, dz_ref, dc_ref, dw_ref, db_ref):
        i = pl.program_id(1)
        gt = _conv_taps(ug_ref, hg_ref, i, tm)
        vt = _conv_taps(uv_ref, hv_ref, i, tm)
        wg, wv = wg_ref[...], wv_ref[...]
        gate = bg_ref[...] + wg[0:1] * gt[0] + wg[1:2] * gt[1] + wg[2:3] * gt[2]
        val = bv_ref[...] + wv[0:1] * vt[0] + wv[1:2] * vt[1] + wv[2:3] * vt[2]
        dz = dz_ref[...].astype(F32)
        sg = _sigmoid(gate)
        dgate = dz * val * (sg * (1.0 + gate * (1.0 - sg)))
        dval = dz * (gate * sg)
        dc_ref[0] = dgate
        dc_ref[1] = dval

        @pl.when(i == 0)
        def _():
            dw_ref[...] = jnp.zeros_like(dw_ref)
            db_ref[...] = jnp.zeros_like(db_ref)

        for half, (d, taps) in enumerate(((dgate, gt), (dval, vt))):
            for k, tap in enumerate(taps):
                dw_ref[half, k : k + 1, :] += jnp.sum(d * tap, axis=0, keepdims=True)
            db_ref[half] += jnp.sum(d, axis=0, keepdims=True)

    return _pallas(
        body,
        9,
        dep,
        name="ffn_act_bwd",
        grid=(nj, t // tm),
        in_specs=[tile(0), before(0), tile(nj), before(nj), vec(3, 0), vec(3, nj), vec(1, 0), vec(1, nj), tile(0)],
        out_specs=[
            pl.BlockSpec((2, tm, tc), lambda j, i: (0, i, j)),
            pl.BlockSpec((2, 3, tc), lambda j, i: (0, 0, j)),
            pl.BlockSpec((2, 1, tc), lambda j, i: (0, 0, j)),
        ],
        out_shape=[
            jax.ShapeDtypeStruct((2, t, D_FF), F32),
            jax.ShapeDtypeStruct((2, 3, D_FF), F32),
            jax.ShapeDtypeStruct((2, 1, D_FF), F32),
        ],
        compiler_params=_params("parallel", "arbitrary"),
    )(u, u, u, u, conv_w, conv_w, conv_b, conv_b, dz)


def _conv_transpose(dc, conv_w):
    t = dc.shape[1]
    tm, tc, nj = _ffn_tiles(t)
    nt = t // tm
    rows = 8

    def body(d_ref, h_ref, w_ref, o_ref):
        i = pl.program_id(2)
        d = d_ref[...]
        halo = jnp.where(i < nt - 1, h_ref[...], 0.0)
        w = w_ref[...]
        o_ref[...] = (w[2:3] * d + w[1:2] * _shift_up(d, halo, 1, tm) + w[0:1] * _shift_up(d, halo, 2, tm)).astype(BF16)

    return pl.pallas_call(
        body,
        name="conv_transpose",
        grid=(2, nj, nt),
        in_specs=[
            pl.BlockSpec((None, tm, tc), lambda s, j, i: (s, i, j)),
            pl.BlockSpec((None, rows, tc), lambda s, j, i: (s, jnp.minimum((i + 1) * (tm // rows), t // rows - 1), j)),
            pl.BlockSpec((3, tc), lambda s, j, i: (0, s * nj + j)),
        ],
        out_specs=pl.BlockSpec((tm, tc), lambda s, j, i: (i, s * nj + j)),
        out_shape=jax.ShapeDtypeStruct((t, 2 * D_FF), BF16),
        compiler_params=_params("parallel", "parallel", "parallel"),
    )(dc, dc, conv_w)


def _ple_loss(gpre, pp, h2, final_norm, target):
    t, d = h2.shape
    tm = _tile(t, (256,))

    def body(gp_ref, pp_ref, h_ref, g_ref, tg_ref, dh_ref, dgp_ref, dpp_ref, dg_ref, loss_ref):
        i = pl.program_id(0)
        gate = _sigmoid(gp_ref[...])
        ppv = pp_ref[...]
        h3 = h_ref[...] + gate * ppv
        r = lax.rsqrt(jnp.mean(h3 * h3, axis=-1, keepdims=True) + EPS)
        n = h3 * r
        g = g_ref[...]
        err = n * g - tg_ref[...]
        loss = 0.5 * jnp.sum(jnp.mean(err * err, axis=-1, keepdims=True))
        dy = err * (1.0 / d)
        dn = dy * g
        dh = r * (dn - n * jnp.mean(dn * n, axis=-1, keepdims=True))
        dh_ref[...] = dh
        dgp_ref[...] = (dh * ppv * gate * (1.0 - gate)).astype(BF16)
        dpp_ref[...] = (dh * gate).astype(BF16)
        dg = jnp.sum(dy * n, axis=0, keepdims=True)

        @pl.when(i == 0)
        def _():
            dg_ref[...] = dg
            loss_ref[...] = jnp.full(loss_ref.shape, loss, F32)

        @pl.when(i > 0)
        def _():
            dg_ref[...] += dg
            loss_ref[...] += loss

    row = pl.BlockSpec((tm, d), lambda i: (i, 0))
    vec = pl.BlockSpec((1, d), lambda i: (0, 0))
    return pl.pallas_call(
        body,
        name="ple_loss",
        grid=(t // tm,),
        in_specs=[row, row, row, vec, row],
        out_specs=[row, row, row, vec, pl.BlockSpec((8, 128), lambda i: (0, 0))],
        out_shape=[
            jax.ShapeDtypeStruct((t, d), F32),
            jax.ShapeDtypeStruct((t, d), BF16),
            jax.ShapeDtypeStruct((t, d), BF16),
            jax.ShapeDtypeStruct((1, d), F32),
            jax.ShapeDtypeStruct((8, 128), F32),
        ],
        compiler_params=_params("arbitrary"),
    )(gpre, pp, h2, final_norm, target)


def _adamw(w, g, m, v):
    m = ADAM_B1 * m + (1.0 - ADAM_B1) * g
    v = ADAM_B2 * v + (1.0 - ADAM_B2) * (g * g)
    m_hat = m / (1.0 - ADAM_B1 ** ADAM_STEP)
    v_hat = v / (1.0 - ADAM_B2 ** ADAM_STEP)
    return -ADAM_LR * (m_hat / (jnp.sqrt(v_hat) + ADAM_EPS) + ADAM_WD * w), m, v


def _adam_big(w, m, v, own, recv, name):
    r, c = w.shape
    tr = _tile(r, (256, 176))

    def body(w_ref, m_ref, v_ref, own_ref, recv_ref, g_ref, d_ref, nm_ref, nv_ref):
        g = own_ref[...]
        for k in range(3):
            g = g + recv_ref[k].astype(F32)
        g_ref[...] = g
        d_ref[...], nm_ref[...], nv_ref[...] = _adamw(w_ref[...], g, m_ref[...], v_ref[...])

    blk = pl.BlockSpec((tr, c), lambda i: (i, 0))
    return pl.pallas_call(
        body,
        name=name,
        grid=(r // tr,),
        in_specs=[blk, blk, blk, blk, pl.BlockSpec((3, tr, c), lambda i: (0, i, 0))],
        out_specs=[blk] * 4,
        out_shape=[jax.ShapeDtypeStruct((r, c), F32)] * 4,
        compiler_params=_params("parallel"),
    )(w, m, v, own, recv)


def _adam_small(w, g, m, v):
    def body(w_ref, g_ref, m_ref, v_ref, d_ref, nm_ref, nv_ref):
        d_ref[...], nm_ref[...], nv_ref[...] = _adamw(w_ref[...], g_ref[...], m_ref[...], v_ref[...])

    return pl.pallas_call(body, name="adam_small", out_shape=[jax.ShapeDtypeStruct(w.shape, F32)] * 3, compiler_params=_params())(w, g, m, v)


def _cast_bf16(w, name):
    r, c = w.shape
    tr = _tile(r, (256, 176))

    def body(w_ref, o_ref):
        o_ref[...] = w_ref[...].astype(BF16)

    blk = pl.BlockSpec((tr, c), lambda i: (i, 0))
    return pl.pallas_call(
        body, name=name, grid=(r // tr,), in_specs=[blk], out_specs=blk, out_shape=jax.ShapeDtypeStruct((r, c), BF16), compiler_params=_params("parallel")
    )(w)


def _position():
    return lax.axis_index("x"), lax.axis_index("y"), lax.axis_index("c")


def _other_chips(x, y):
    return [(1 - x, y), (x, 1 - y), (1 - x, 1 - y)]


def _shard_of(ref, axis, size, dev):
    start = pl.multiple_of((4 * dev[0] + 2 * dev[1] + dev[2]) * size, 128 if axis == 1 else 16)
    return ref.at[:, pl.ds(start, size)] if axis == 1 else ref.at[pl.ds(start, size), :]


def _all_gather(shards, axes):
    n = len(shards)

    def body(*refs):
        ins, outs = refs[:n], refs[n : 2 * n]
        send_sems, recv_sems, local_sems = refs[2 * n :]
        x, y, c = _position()
        me, sibling = (x, y, c), (x, y, 1 - c)
        chips = _other_chips(x, y)
        firsts, passed, locals_ = [], [], []
        for w in range(n):
            size = shards[w].shape[axes[w]]
            slot = functools.partial(_shard_of, outs[w], axes[w], size)

            def copy(k, block, to, src=None, w=w, slot=slot):
                return pltpu.make_async_remote_copy(
                    src_ref=slot(block) if src is None else src,
                    dst_ref=slot(block),
                    send_sem=send_sems.at[7 * w + k],
                    recv_sem=recv_sems.at[7 * w + k],
                    device_id=to,
                    device_id_type=MESH,
                )

            mine = pltpu.make_async_copy(ins[w], slot(me), local_sems.at[w])
            mine.start()
            locals_.append(mine)
            first = [copy(0, me, sibling, src=ins[w])] + [copy(1 + j, me, (*chip, c), src=ins[w]) for j, chip in enumerate(chips)]
            for cp in first:
                cp.start()
            firsts.append((first, copy))
        for w in range(n):
            first, copy = firsts[w]
            fwd = [copy(4 + j, (*chip, c), sibling) for j, chip in enumerate(chips)]
            for j, chip in enumerate(chips):
                copy(1 + j, (*chip, c), me).wait_recv()
                fwd[j].start()
            passed.append(fwd)
        for w in range(n):
            first, copy = firsts[w]
            copy(0, sibling, me).wait_recv()
            for j, chip in enumerate(chips):
                copy(4 + j, (*chip, 1 - c), me).wait_recv()
            for cp in first + passed[w]:
                cp.wait_send()
            locals_[w].wait()

    def full(s, ax):
        shape = list(s.shape)
        shape[ax] *= N_DEV
        return jax.ShapeDtypeStruct(tuple(shape), s.dtype)

    return pl.pallas_call(
        body,
        name="all_gather_weights",
        in_specs=[ANY] * n,
        out_specs=[ANY] * n,
        out_shape=[full(s, ax) for s, ax in zip(shards, axes)],
        scratch_shapes=[pltpu.SemaphoreType.DMA((7 * n,)), pltpu.SemaphoreType.DMA((7 * n,)), pltpu.SemaphoreType.DMA((n,))],
    )(*shards)


def _add_blocks(ids, grad, landed, axis, size, targets, out_dtype, name):
    rows = size if axis == 0 else grad.shape[0]
    cols = size if axis == 1 else grad.shape[1]
    tr = _tile(rows, (256, 176))
    nr = rows // tr
    nt = len(targets)

    def body(ids_ref, g_ref, l_ref, o_ref):
        o_ref[...] = (g_ref[...] + l_ref[...]).astype(out_dtype)

    if axis == 1:
        g_spec = pl.BlockSpec((tr, cols), lambda k, i, ids: (i, ids[targets[0] + k]))
    else:
        g_spec = pl.BlockSpec((tr, cols), lambda k, i, ids: (ids[targets[0] + k] * nr + i, 0))
    return pl.pallas_call(
        body,
        name=name,
        grid_spec=pltpu.PrefetchScalarGridSpec(
            num_scalar_prefetch=1,
            grid=(nt, nr),
            in_specs=[g_spec, pl.BlockSpec((None, tr, cols), lambda k, i, ids: (ids[4 + targets[0] + k], i, 0))],
            out_specs=pl.BlockSpec((None, tr, cols), lambda k, i, ids: (k, i, 0)),
        ),
        out_shape=jax.ShapeDtypeStruct((nt, rows, cols), out_dtype),
        compiler_params=_params("parallel", "parallel"),
    )(ids, grad, landed)


def _all_reduce_small(vec):
    rows = vec.shape[0]

    def body(v_ref, o_ref, land, send_sems, recv_sems):
        x, y, c = _position()
        mine = 4 * x + 2 * y + c
        copies = []
        for mask in range(1, N_DEV):
            peer = (1 - x if mask & 4 else x, 1 - y if mask & 2 else y, 1 - c if mask & 1 else c)
            copies.append(
                pltpu.make_async_remote_copy(
                    src_ref=v_ref, dst_ref=land.at[mine], send_sem=send_sems.at[mask - 1], recv_sem=recv_sems.at[mask - 1], device_id=peer, device_id_type=MESH
                )
            )
        for cp in copies:
            cp.start()
        land[mine] = v_ref[...]
        for cp in copies:
            cp.wait()
        acc = land[0]
        for k in range(1, N_DEV):
            acc = acc + land[k]
        o_ref[...] = acc

    return pl.pallas_call(
        body,
        name="all_reduce_small",
        out_shape=jax.ShapeDtypeStruct(vec.shape, F32),
        in_specs=[pl.BlockSpec(memory_space=pltpu.VMEM)],
        out_specs=pl.BlockSpec(memory_space=pltpu.VMEM),
        scratch_shapes=[pltpu.VMEM((N_DEV, rows, 128), F32), pltpu.SemaphoreType.DMA((N_DEV - 1,)), pltpu.SemaphoreType.DMA((N_DEV - 1,))],
    )(vec)


def _rows128(a, rows):
    flat = a.reshape(-1)
    return jnp.pad(flat, (0, rows * 128 - flat.shape[0])).reshape(rows, 128)


def _pad_rel(a):
    return jnp.pad(a.reshape(ATT_HEADS, -1)[:, :N_REL], ((0, 0), (0, N_REL_PAD - N_REL)))


SMALL = [("norm_mix", 16), ("lb_logits", 16), ("hg_norm", 8), ("rel_bias", 24), ("norm_ffn", 16), ("conv_b", 88), ("norm_ple", 16), ("final_norm", 16)]
CONV_W_FULL_ROWS = 3 * 2 * D_FF // 128
CONV_W_SHARD_ROWS = 40


def _pack_small(parts):
    return jnp.concatenate([_rows128(_pad_rel(parts[k]) if k == "rel_bias" else parts[k], rows) for k, rows in SMALL], axis=0)


def _unpack_small(packed, shapes):
    out, at = {}, 0
    for k, rows in SMALL:
        blk = packed[at : at + rows]
        at += rows
        if k == "rel_bias":
            out[k] = blk.reshape(ATT_HEADS, N_REL_PAD)[:, :N_REL].reshape(shapes[k])
        else:
            n = 1
            for s in shapes[k]:
                n *= s
            out[k] = blk.reshape(-1)[:n].reshape(shapes[k])
    return out, at


BIG = [("w_in", 1), ("w_out", 0), ("w_up", 1), ("w_down", 0), ("w_ple_gate", 0), ("w_ple_proj", 1)]


HBM = pl.BlockSpec(memory_space=pltpu.HBM)
SEM = pl.BlockSpec(memory_space=pltpu.SEMAPHORE)
EFFECT = pltpu.SideEffectType.DATAFLOW_SIDE_EFFECTING


def _copies(plan, refs, send_sems, recv_sems):
    return [
        pltpu.make_async_remote_copy(src_ref=src, dst_ref=dst, send_sem=send_sems.at[i], recv_sem=recv_sems.at[i], device_id=dev, device_id_type=MESH)
        for i, (src, dst, dev) in enumerate(plan(refs))
    ]


def _split_start(name, arrays, plan, n):
    k = len(arrays)

    def body(*refs):
        for cp in _copies(plan, refs[:k], refs[k], refs[k + 1]):
            cp.start()
        refs[-1][...] = jnp.zeros_like(refs[-1])

    out = pl.pallas_call(
        body,
        name=name,
        out_shape=(pltpu.SemaphoreType.DMA((n,)), pltpu.SemaphoreType.DMA((n,)), *[pltpu.HBM(a.shape, a.dtype) for a in arrays], jax.ShapeDtypeStruct((8, 128), F32)),
        in_specs=[HBM] * k,
        out_specs=(SEM, SEM, *[HBM] * k, pl.BlockSpec(memory_space=pltpu.VMEM)),
        input_output_aliases={i: 2 + i for i in range(k)},
        compiler_params=pltpu.CompilerParams(has_side_effects=EFFECT),
    )(*[pltpu.with_memory_space_constraint(a, pltpu.HBM) for a in arrays])
    return out[0], out[1], list(out[2 : 2 + k]), out[-1]


def _split_wait(name, send, recv, arrays, plan, after):
    k = len(arrays)

    def body(*refs):
        for cp in _copies(plan, refs[:k], refs[k], refs[k + 1]):
            cp.wait_send()
            cp.wait_recv()

    out = pl.pallas_call(
        body,
        name=name,
        out_shape=tuple(pltpu.HBM(a.shape, a.dtype) for a in arrays),
        in_specs=[HBM] * k + [SEM, SEM, ANY],
        out_specs=tuple([HBM] * k),
        input_output_aliases={i: i for i in range(k)},
        compiler_params=pltpu.CompilerParams(has_side_effects=EFFECT),
    )(*arrays, send, recv, after)
    return list(out)


def _cast_into(w, me, axis, name, dep):
    r, c = w.shape
    tr = _tile(r, (256, 176))
    nr = r // tr

    def body(me_ref, w_ref, dep_ref, o_ref):
        o_ref[...] = w_ref[...].astype(BF16)

    if axis == 1:
        shape, o_spec = (r, N_DEV * c), pl.BlockSpec((tr, c), lambda i, me: (i, me[0]))
    else:
        shape, o_spec = (N_DEV * r, c), pl.BlockSpec((tr, c), lambda i, me: (me[0] * nr + i, 0))
    return pl.pallas_call(
        body,
        name=name,
        grid_spec=pltpu.PrefetchScalarGridSpec(
            num_scalar_prefetch=1, grid=(nr,), in_specs=[pl.BlockSpec((tr, c), lambda i, me: (i, 0)), ANY], out_specs=o_spec
        ),
        out_shape=jax.ShapeDtypeStruct(shape, BF16),
        compiler_params=_params("parallel"),
    )(me, w, dep)


LATE = ["w_out", "w_up", "w_down", "w_ple_gate", "w_ple_proj"]
GROUPS = [["w_ple_proj", "w_ple_gate", "w_down"], ["w_up"], ["w_out"], ["w_in"]]
STAGES = ["conv_transpose", "d_mix_out", "hgrn_bwd", "d_norm_mix_out"]


class _Exchange:
    def __init__(self, big, position):
        self.big, self.axis = big, dict(BIG)
        self.size = {k: big[k].shape[self.axis[k]] for k in big}
        self.x, self.y, self.c = position
        chips = [(self.x, self.y)] + _other_chips(self.x, self.y)
        self.ids = jnp.stack([4 * cx + 2 * cy + self.c for cx, cy in chips] + [2 * cx + cy for cx, cy in chips]).astype(jnp.int32)
        self.token, self.grads, self.state, self.wfull = None, {}, {}, {}


    def _slot(self, ref, k, dev):
        return _shard_of(ref, self.axis[k], self.size[k], dev)

    def _plan_gather(self, refs):
        x, y, c = _position()
        me, out = (x, y, c), []
        for k, ref in zip(LATE, refs):
            mine = self._slot(ref, k, me)
            out.append((mine, mine, (x, y, 1 - c)))
            out += [(mine, mine, (*chip, c)) for chip in _other_chips(x, y)]
        return out

    def _plan_forward(self, refs):
        x, y, c = _position()
        out = []
        for k, ref in zip(LATE, refs):
            for chip in _other_chips(x, y):
                block = self._slot(ref, k, (*chip, c))
                out.append((block, block, (x, y, 1 - c)))
        return out

    def _plan_sibling(self, names, refs):
        x, y, c = _position()
        n = len(names)
        return [(self._slot(refs[i], k, (p // 2, p % 2, 1 - c)), refs[n + i].at[p], (x, y, 1 - c)) for i, k in enumerate(names) for p in range(4)]

    def _plan_chips(self, names, refs):
        x, y, c = _position()
        n = len(names)
        return [(refs[i].at[j], refs[n + i].at[j], (*chip, c)) for i in range(n) for j, chip in enumerate(_other_chips(x, y))]


    def gather(self, conv_w):
        w_in, conv_full = _all_gather([_cast_bf16(self.big["w_in"], "cast_w_in"), conv_w], [1, 1])
        self.wfull["w_in"] = w_in
        me = (4 * self.x + 2 * self.y + self.c).astype(jnp.int32).reshape(1)
        fulls = [_cast_into(self.big[k], me, self.axis[k], "cast_" + k, w_in) for k in LATE]
        send, recv, fulls, self.token = _split_start("gather_start", fulls, self._plan_gather, 4 * len(LATE))
        self.late = (send, recv, fulls)
        return conv_full

    def weight(self, k):
        return self.wfull[k]

    def dep(self):
        token, self.token = self.token, None
        return token

    def grad(self, k, g):
        self.grads[k] = g
        for gi, names in enumerate(GROUPS):
            if k == names[-1]:
                plan = functools.partial(self._plan_sibling, names)
                lands = [lax.empty((4, *self._shard_shape(n)), F32) for n in names]
                send, recv, arrays, self.token = _split_start(f"sibling_start_{gi}", [self.grads[n] for n in names] + lands, plan, 4 * len(names))
                self.state[gi] = (send, recv, arrays, plan)

    def done(self, stage, after):
        if stage == "hgrn_fwd":
            send, recv, fulls = self.late
            fulls = _split_wait("gather_wait", send, recv, fulls, self._plan_gather, after)
            send, recv, fulls, self.token = _split_start("forward_start", fulls, self._plan_forward, 3 * len(LATE))
            self.late = (send, recv, fulls)
        elif stage == "att_fwd":
            send, recv, fulls = self.late
            self.wfull.update(zip(LATE, _split_wait("forward_wait", send, recv, fulls, self._plan_forward, after)))
        elif stage in STAGES:
            self._to_chips(STAGES.index(stage), after)

    def _shard_shape(self, k):
        shape = list(self.grads[k].shape)
        shape[self.axis[k]] = self.size[k]
        return tuple(shape)

    def _to_chips(self, gi, after):
        names = GROUPS[gi]
        n = len(names)
        send, recv, arrays, plan = self.state[gi]
        arrays = _split_wait(f"sibling_wait_{gi}", send, recv, arrays, plan, after)
        own, parts = [], []
        for k, g, land in zip(names, arrays[:n], arrays[n:]):
            own.append(_add_blocks(self.ids, g, land, self.axis[k], self.size[k], [0], F32, "add_own_" + k)[0])
            parts.append(_add_blocks(self.ids, g, land, self.axis[k], self.size[k], [1, 2, 3], BF16, "add_send_" + k))
        plan = functools.partial(self._plan_chips, names)
        lands = [lax.empty(part.shape, BF16) for part in parts]
        send, recv, arrays, self.token = _split_start(f"chips_start_{gi}", parts + lands, plan, 3 * n)
        self.state[gi] = (send, recv, arrays, plan, own)

    def finish(self, gi, after):
        names = GROUPS[gi]
        send, recv, arrays, plan, own = self.state[gi]
        arrays = _split_wait(f"chips_wait_{gi}", send, recv, arrays, plan, after)
        return {k: (o, r) for k, o, r in zip(names, own, arrays[len(names) :])}


class _Resident:
    def __init__(self, wfull):
        self.wfull, self.grads = wfull, {}

    def weight(self, k):
        return self.wfull[k]

    def grad(self, k, g):
        self.grads[k] = g

    def dep(self):
        return None

    def done(self, stage, after):
        pass


def _local_step(x, p, target, small, conv_w, ex):
    a1, r1 = _rms_fwd(x, small["norm_mix"], "norm_mix_fwd", dep=ex.dep())
    proj = _matmul(a1, ex.weight("w_in"), "nn", F32, "in_proj")
    bias = _bias_table(jnp.pad(small["rel_bias"], ((0, 0), (0, N_REL_PAD - N_REL))))
    y_hg, o_hg, states = _hgrn_fwd(proj, small["lb_logits"], small["hg_norm"])
    ex.done("hgrn_fwd", y_hg)
    y_att = _att_fwd(proj, bias, dep=ex.dep())
    ex.done("att_fwd", y_att)
    ycat = jnp.concatenate([y_hg, y_att], axis=1)
    h1 = _matmul(ycat, ex.weight("w_out"), "nn", F32, "out_proj", resid=x)
    a2, r2 = _rms_fwd(h1, small["norm_ffn"], "norm_ffn_fwd")
    u = _matmul(a2, ex.weight("w_up"), "nn", BF16, "up_proj")
    z = _ffn_act_fwd(u, conv_w, small["conv_b"])
    h2 = _matmul(z, ex.weight("w_down"), "nn", F32, "down_proj", tk=2816, resid=h1)
    a3, r3 = _rms_fwd(h2, small["norm_ple"], "norm_ple_fwd")
    gpre = _matmul(a3, ex.weight("w_ple_gate"), "nn", F32, "ple_gate")
    pp = _matmul(p, ex.weight("w_ple_proj"), "nn", F32, "ple_proj")
    dh3, dgpre, dpp, d_final, loss = _ple_loss(gpre, pp, h2, small["final_norm"], target)

    ex.grad("w_ple_proj", _matmul(p, dpp, "tn", F32, "d_w_ple_proj", tk=2048))
    ex.grad("w_ple_gate", _matmul(a3, dgpre, "tn", F32, "d_w_ple_gate", tk=2048))
    da3 = _matmul(dgpre, ex.weight("w_ple_gate"), "nt", F32, "d_norm_ple_out")
    dh2, d_ple = _rms_bwd(da3, h2, r3, small["norm_ple"], dh3, "norm_ple_bwd")
    dz = _matmul(dh2, ex.weight("w_down"), "nt", BF16, "d_ffn_act")
    ex.grad("w_down", _matmul(z, dh2, "tn", F32, "d_w_down", tk=2048))
    dc, dcw, dcb = _ffn_act_bwd(u, dz, conv_w, small["conv_b"], dep=ex.dep())
    du = _conv_transpose(dc, conv_w)
    ex.done("conv_transpose", du)
    d_conv_w = jnp.concatenate([dcw[0], dcw[1]], axis=1)
    d_conv_b = jnp.concatenate([dcb[0], dcb[1]], axis=1)
    ex.grad("w_up", _matmul(a2, du, "tn", F32, "d_w_up", tk=2048, dep=ex.dep()))
    da2 = _matmul(du, ex.weight("w_up"), "nt", F32, "d_norm_ffn_out", tk=2816, dep=ex.dep())
    dh1, d_ffn = _rms_bwd(da2, h1, r2, small["norm_ffn"], dh2, "norm_ffn_bwd")
    dycat = _matmul(dh1, ex.weight("w_out"), "nt", F32, "d_mix_out")
    ex.done("d_mix_out", dycat)
    ex.grad("w_out", _matmul(ycat, dh1, "tn", F32, "d_w_out", tk=2048, dep=ex.dep()))
    dp_hg, d_lb, d_hgn = _hgrn_bwd(proj, small["lb_logits"], small["hg_norm"], o_hg, dycat, states, dep=ex.dep())
    ex.done("hgrn_bwd", d_lb)
    dq_att, dk_att, dv_att, gsum = _att_bwd(proj, bias, dycat, dep=ex.dep())
    d_rel = _rel_bias_grad(gsum)
    dproj = jnp.concatenate([dp_hg[0], dp_hg[1], dp_hg[2], dp_hg[3], dq_att, dk_att, dv_att], axis=1)
    ex.grad("w_in", _matmul(a1, dproj, "tn", F32, "d_w_in", tk=2048))
    da1 = _matmul(dproj, ex.weight("w_in"), "nt", F32, "d_norm_mix_out", tk=1792, dep=ex.dep())
    ex.done("d_norm_mix_out", da1)
    dx, d_mix = _rms_bwd(da1, x, r1, small["norm_mix"], dh1, "norm_mix_bwd", dep=ex.dep())
    d_small = {
        "norm_mix": d_mix, "lb_logits": d_lb, "hg_norm": d_hgn, "rel_bias": d_rel, "norm_ffn": d_ffn,
        "conv_b": d_conv_b, "norm_ple": d_ple, "final_norm": d_final,
    }
    return loss, dx, d_small, d_conv_w


def kernel(x, p, norm_mix, w_in, lb_logits, hg_norm, rel_bias, w_out, norm_ffn, w_up, conv_w, conv_b, w_down, norm_ple, w_ple_gate, w_ple_proj, final_norm, loss_target, m_norm_mix, m_w_in, m_lb_logits, m_hg_norm, m_rel_bias, m_w_out, m_norm_ffn, m_w_up, m_conv_w, m_conv_b, m_w_down, m_norm_ple, m_w_ple_gate, m_w_ple_proj, m_final_norm, v_norm_mix, v_w_in, v_lb_logits, v_hg_norm, v_rel_bias, v_w_out, v_norm_ffn, v_w_up, v_conv_w, v_conv_b, v_w_down, v_norm_ple, v_w_ple_gate, v_w_ple_proj, v_final_norm):
    given = dict(locals())
    mx, my, mc = _position()
    me = 4 * mx + 2 * my + mc
    big = {k: given[k][0] for k, _ in BIG}
    ex = _Exchange(big, (mx, my, mc))
    conv_w_full = ex.gather(conv_w[0])

    small = {
        "norm_mix": norm_mix, "lb_logits": lb_logits, "hg_norm": hg_norm, "rel_bias": rel_bias[0], "norm_ffn": norm_ffn,
        "conv_b": conv_b, "norm_ple": norm_ple, "final_norm": final_norm.reshape(1, -1),
    }
    loss, dx, d_small, d_conv_w = _local_step(x[0], p[0, 0], loss_target[0], small, conv_w_full, ex)

    packed = jnp.concatenate([_pack_small(d_small), _rows128(d_conv_w, CONV_W_FULL_ROWS), _rows128(loss[0:1, 0:1], 8)], axis=0)
    reduced = _all_reduce_small(packed)

    out = {}
    for gi in range(len(GROUPS)):
        for k, (o, r) in ex.finish(gi, reduced).items():
            g, d, nm, nv = _adam_big(big[k], given["m_" + k][0], given["v_" + k][0], o, r, "adam_" + k)
            out[k] = tuple(a[None] for a in (g, d, nm, nv))
    shapes = {k: given[k].shape for k, _ in SMALL}
    g_small, at = _unpack_small(reduced, shapes)
    g_conv_full = reduced[at : at + CONV_W_FULL_ROWS].reshape(3, 2 * D_FF)
    total_loss = reduced[at + CONV_W_FULL_ROWS, 0]
    cw = conv_w.shape[2]
    g_conv = lax.dynamic_slice_in_dim(g_conv_full, me * cw, cw, axis=1)

    def pack_with_conv(parts, conv_part):
        return jnp.concatenate([_pack_small(parts), _rows128(conv_part, CONV_W_SHARD_ROWS)], axis=0)

    d_pk, m_pk, v_pk = _adam_small(
        pack_with_conv({k: given[k] for k, _ in SMALL}, conv_w),
        pack_with_conv(g_small, g_conv),
        pack_with_conv({k: given["m_" + k] for k, _ in SMALL}, m_conv_w),
        pack_with_conv({k: given["v_" + k] for k, _ in SMALL}, v_conv_w),
    )
    for name, pk in (("d", d_pk), ("m", m_pk), ("v", v_pk)):
        parts, at = _unpack_small(pk, shapes)
        parts["conv_w"] = pk[at : at + CONV_W_SHARD_ROWS].reshape(-1)[: 3 * cw].reshape(conv_w.shape)
        for k, a in parts.items():
            out.setdefault(k, {})
            out[k][name] = a
    for k, _ in SMALL:
        out[k]["g"] = g_small[k]
    out["conv_w"]["g"] = g_conv.reshape(conv_w.shape)

    order = ["norm_mix", "w_in", "lb_logits", "hg_norm", "rel_bias", "w_out", "norm_ffn", "w_up", "conv_w", "conv_b", "w_down", "norm_ple", "w_ple_gate", "w_ple_proj", "final_norm"]

    def pick(k, what):
        return out[k][what] if isinstance(out[k], dict) else out[k][{"g": 0, "d": 1, "m": 2, "v": 3}[what]]

    return (total_loss, dx[None], *[pick(k, "g") for k in order], *[pick(k, "d") for k in order], *[pick(k, "m") for k in order], *[pick(k, "v") for k in order])
```

```python
import functools

import jax
import jax.numpy as jnp
from jax import lax
from jax.experimental import pallas as pl
from jax.experimental.pallas import tpu as pltpu

F32 = jnp.float32
BF16 = jnp.bfloat16

D_MODEL = 2048
CHUNK = 64
HG_HEADS = 8
HEAD_DIM = 128
HG_WIDTH = HG_HEADS * HEAD_DIM
ATT_HEADS = 8
ATT_WIDTH = ATT_HEADS * HEAD_DIM
LEFT_CHUNKS = 8
PAD = LEFT_CHUNKS * CHUNK
BAND = PAD + CHUNK
REL_CLIP = 128
N_REL = 2 * REL_CLIP + 1
N_REL_PAD = 384
D_FF = 5632
EPS = 1e-6
ATT_SCALE = HEAD_DIM ** -0.5
SUB = 32
HG_BLOCK = 4
Q_BLOCK = 4 * CHUNK
K_BLOCK = Q_BLOCK + PAD
DIAG = 1024

ADAM_LR = 0.001
ADAM_B1 = 0.9
ADAM_B2 = 0.999
ADAM_EPS = 1e-08
ADAM_WD = 0.01
ADAM_STEP = 10

N_DEV = 8
VMEM_LIMIT = 48 * 1024 * 1024
MESH = pl.DeviceIdType.MESH
ANY = pl.BlockSpec(memory_space=pl.ANY)
HIGHEST = lax.Precision.HIGHEST

NN = (((1,), (0,)), ((), ()))
NT = (((1,), (1,)), ((), ()))
TN = (((0,), (0,)), ((), ()))


def _params(*sem):
    return pltpu.CompilerParams(dimension_semantics=sem if sem else None, vmem_limit_bytes=VMEM_LIMIT)


def _pallas(body, n_in, dep, **kw):
    if dep is None:
        return pl.pallas_call(body, **kw)

    def body_after(*refs):
        body(*refs[:n_in], *refs[n_in + 1 :])

    call = pl.pallas_call(body_after, **dict(kw, in_specs=list(kw["in_specs"]) + [ANY]))
    return lambda *ops: call(*ops, dep)


def _dot(a, b, dims=NN):
    return lax.dot_general(a, b, dims, preferred_element_type=F32)


def _dot3(a, b, dims=NN):
    a_hi, b_hi = a.astype(BF16), b.astype(BF16)
    a_lo, b_lo = (a - a_hi.astype(F32)).astype(BF16), (b - b_hi.astype(F32)).astype(BF16)
    return _dot(a_hi, b_hi, dims) + (_dot(a_hi, b_lo, dims) + _dot(a_lo, b_hi, dims))


def _sigmoid(x):
    return 1.0 / (1.0 + jnp.exp(-x))


def _tile(n, prefs):
    for t in prefs:
        if n % t == 0:
            return t
    return n


def _matmul(a, b, mode, out_dtype, name, tm=512, tn=1024, tk=None, resid=None, dep=None):
    if mode == "nn":
        (m, k), n = a.shape, b.shape[1]
    elif mode == "nt":
        (m, k), n = a.shape, b.shape[0]
    else:
        (k, m), n = a.shape, b.shape[1]
    tm = _tile(m, (tm, 256, 128))
    tn = _tile(n, (tn, 512, 256, 128))
    tk = k if tk is None else _tile(k, (tk,))
    nk = k // tk
    dims = {"nn": NN, "nt": NT, "tn": TN}[mode]
    a_spec = pl.BlockSpec((tk, tm), lambda i, j, s: (s, i)) if mode == "tn" else pl.BlockSpec((tm, tk), lambda i, j, s: (i, s))
    b_spec = pl.BlockSpec((tn, tk), lambda i, j, s: (j, s)) if mode == "nt" else pl.BlockSpec((tk, tn), lambda i, j, s: (s, j))
    o_spec = pl.BlockSpec((tm, tn), lambda i, j, s: (i, j))
    has_res = resid is not None

    def body(*refs):
        a_ref, b_ref = refs[0], refs[1]
        o_ref = refs[2 + has_res]
        part = _dot(a_ref[...].astype(BF16), b_ref[...].astype(BF16), dims)

        def finish(acc):
            if has_res:
                acc = acc + refs[2][...]
            o_ref[...] = acc.astype(out_dtype)

        if nk == 1:
            finish(part)
        else:
            acc_ref = refs[-1]
            s = pl.program_id(2)

            @pl.when(s == 0)
            def _():
                acc_ref[...] = part

            @pl.when(s > 0)
            def _():
                acc_ref[...] += part

            @pl.when(s == nk - 1)
            def _():
                finish(acc_ref[...])

    return _pallas(
        body,
        2 + has_res,
        dep,
        name=name,
        grid=(m // tm, n // tn, nk),
        in_specs=[a_spec, b_spec] + ([o_spec] if has_res else []),
        out_specs=o_spec,
        out_shape=jax.ShapeDtypeStruct((m, n), out_dtype),
        scratch_shapes=[pltpu.VMEM((tm, tn), F32)] if nk > 1 else [],
        compiler_params=_params("parallel", "parallel", "arbitrary"),
    )(*([a, b] + ([resid] if has_res else [])))


def _rms_fwd(x, g, name, dep=None):
    t, d = x.shape
    tm = _tile(t, (256,))

    def body(x_ref, g_ref, a_ref, r_ref):
        xv = x_ref[...]
        r = lax.rsqrt(jnp.mean(xv * xv, axis=-1, keepdims=True) + EPS)
        a_ref[...] = (xv * r * g_ref[...]).astype(BF16)
        r_ref[...] = r

    row = pl.BlockSpec((tm, d), lambda i: (i, 0))
    return _pallas(
        body,
        2,
        dep,
        name=name,
        grid=(t // tm,),
        in_specs=[row, pl.BlockSpec((1, d), lambda i: (0, 0))],
        out_specs=[row, pl.BlockSpec((tm, 1), lambda i: (i, 0))],
        out_shape=[jax.ShapeDtypeStruct((t, d), BF16), jax.ShapeDtypeStruct((t, 1), F32)],
        compiler_params=_params("parallel"),
    )(x, g)


def _rms_bwd(da, x, r, g, resid, name, dep=None):
    t, d = x.shape
    tm = _tile(t, (256,))

    def body(da_ref, x_ref, r_ref, g_ref, res_ref, dx_ref, dg_ref):
        i = pl.program_id(0)
        rv = r_ref[...]
        n = x_ref[...] * rv
        dav = da_ref[...]
        dn = dav * g_ref[...]
        dx_ref[...] = rv * (dn - n * jnp.mean(dn * n, axis=-1, keepdims=True)) + res_ref[...]
        part = jnp.sum(dav * n, axis=0, keepdims=True)

        @pl.when(i == 0)
        def _():
            dg_ref[...] = part

        @pl.when(i > 0)
        def _():
            dg_ref[...] += part

    row = pl.BlockSpec((tm, d), lambda i: (i, 0))
    vec = pl.BlockSpec((1, d), lambda i: (0, 0))
    return _pallas(
        body,
        5,
        dep,
        name=name,
        grid=(t // tm,),
        in_specs=[row, row, pl.BlockSpec((tm, 1), lambda i: (i, 0)), vec, row],
        out_specs=[row, vec],
        out_shape=[jax.ShapeDtypeStruct((t, d), F32), jax.ShapeDtypeStruct((1, d), F32)],
        compiler_params=_params("arbitrary"),
    )(da, x, r, g, resid)


def _tri(n, upper):
    r = lax.broadcasted_iota(jnp.int32, (n, n), 0)
    c = lax.broadcasted_iota(jnp.int32, (n, n), 1)
    return jnp.where((c >= r) if upper else (c <= r), 1.0, 0.0).astype(F32)


def _hgrn_gates(q, fp, lbl):
    l0, l1 = lbl[0:1, :], lbl[1:2, :]
    mx = jnp.maximum(l0, l1)
    e0, e1 = jnp.exp(l0 - mx), jnp.exp(l1 - mx)
    lb = e0 / (e0 + e1)
    sig = _sigmoid(fp)
    f = lb + (1.0 - lb) * sig
    kk = (1.0 - lb) * _sigmoid(-fp)
    sq = _sigmoid(q)
    b = jnp.dot(_tri(CHUNK, False), jnp.log(f), precision=HIGHEST, preferred_element_type=F32)
    return lb, sig, f, kk, sq, q * sq, b


def _heads(x):
    return [x[:, j * HEAD_DIM : (j + 1) * HEAD_DIM] for j in range(x.shape[1] // HEAD_DIM)]


def _wide(parts):
    return jnp.concatenate(parts, axis=1)


def _intra_blocks(b):
    out = []
    for lo in range(0, CHUNK, SUB):
        hi = lo + SUB
        br = b[lo + SUB // 2 : lo + SUB // 2 + 1, :]
        row = lax.broadcasted_iota(jnp.int32, (SUB, hi), 0) + lo
        col = lax.broadcasted_iota(jnp.int32, (SUB, hi), 1)
        out.append((lo, hi, jnp.exp(b[lo:hi] - br), jnp.exp(br - b[:hi]), col <= row))
    return out


def _hgrn_fwd(proj, lb_logits, hg_norm):
    t = proj.shape[0]
    nc = t // CHUNK

    def body(q_ref, f_ref, i_ref, g_ref, lbl_ref, hgn_ref, y_ref, o_ref, st_ref, s_scr):
        c = pl.program_id(1)

        @pl.when(c == 0)
        def _():
            s_scr[...] = jnp.zeros_like(s_scr)

        hs = range(HG_BLOCK)
        sts = [s_scr[j] for j in hs]
        _, _, _, kk, _, qf, b = _hgrn_gates(q_ref[...], f_ref[...], lbl_ref[...])
        vb = _heads(i_ref[...].astype(BF16))
        bl = b[CHUNK - 1 : CHUNK, :]
        qe = _heads((qf * jnp.exp(b)).astype(BF16))
        kd = _heads((kk * jnp.exp(bl - b)).astype(BF16))
        decay = _heads(jnp.exp(bl))
        o = [_dot(qe[j], sts[j].astype(BF16), NT) for j in hs]
        parts = [[] for _ in hs]
        for lo, hi, ea, eb, mask in _intra_blocks(b):
            a, bk = _heads((qf[lo:hi] * ea).astype(BF16)), _heads((kk[:hi] * eb).astype(BF16))
            p = [jnp.where(mask, _dot(a[j], bk[j], NT), 0.0).astype(BF16) for j in hs]
            for j in hs:
                parts[j].append(_dot(p[j], vb[j][:hi]))
        o = [o[j] + jnp.concatenate(parts[j], axis=0) for j in hs]
        new = [sts[j] * decay[j] + _dot(vb[j], kd[j], TN) for j in hs]
        hgn = hgn_ref[...]
        on = [o[j] * lax.rsqrt(jnp.mean(o[j] * o[j], axis=-1, keepdims=True) + EPS) * hgn for j in hs]
        gg = g_ref[...]
        for j in hs:
            st_ref[j] = sts[j]
            s_scr[j] = new[j]
        o_ref[...] = _wide(o)
        y_ref[...] = (_wide(on) * (gg * _sigmoid(gg))).astype(BF16)

    wide = HG_BLOCK * HEAD_DIM
    groups = HG_HEADS // HG_BLOCK

    def col(k):
        return pl.BlockSpec((CHUNK, wide), lambda g, c: (c, k * groups + g))

    out = pl.BlockSpec((CHUNK, wide), lambda g, c: (c, g))
    return pl.pallas_call(
        body,
        name="hgrn_fwd",
        grid=(groups, nc),
        in_specs=[col(0), col(1), col(2), col(3), pl.BlockSpec((2, wide), lambda g, c: (0, g)), pl.BlockSpec((1, HEAD_DIM), lambda g, c: (0, 0))],
        out_specs=[out, out, pl.BlockSpec((HG_BLOCK, None, HEAD_DIM, HEAD_DIM), lambda g, c: (g, c, 0, 0))],
        out_shape=[
            jax.ShapeDtypeStruct((t, HG_WIDTH), BF16),
            jax.ShapeDtypeStruct((t, HG_WIDTH), F32),
            jax.ShapeDtypeStruct((HG_HEADS, nc, HEAD_DIM, HEAD_DIM), F32),
        ],
        scratch_shapes=[pltpu.VMEM((HG_BLOCK, HEAD_DIM, HEAD_DIM), F32)],
        compiler_params=_params("arbitrary", "arbitrary"),
    )(proj, proj, proj, proj, lb_logits, hg_norm)


def _hgrn_bwd(proj, lb_logits, hg_norm, o_hg, dycat, states, dep=None):
    t = proj.shape[0]
    nc = t // CHUNK

    def body(q_ref, f_ref, i_ref, g_ref, lbl_ref, hgn_ref, o_ref, dy_ref, st_ref, dp_ref, dlbl_ref, dhgn_ref, dst_scr, dlb_scr):
        h = pl.program_id(0)
        c = pl.program_id(1)

        @pl.when(c == 0)
        def _():
            dst_scr[...] = jnp.zeros_like(dst_scr)
            dlb_scr[...] = jnp.zeros_like(dlb_scr)

        @pl.when((c == 0) & (h == 0))
        def _():
            dhgn_ref[...] = jnp.zeros_like(dhgn_ref)

        hs = range(HG_BLOCK)
        hgn = _wide([hgn_ref[...]] * HG_BLOCK)
        q, fp, gg, vi = q_ref[...], f_ref[...], g_ref[...], i_ref[...]
        lb, sig, f, kk, sq, qf, b = _hgrn_gates(q, fp, lbl_ref[...])
        o, dy = o_ref[...], dy_ref[...]
        sg = _sigmoid(gg)
        n = _wide([oh * lax.rsqrt(jnp.mean(oh * oh, axis=-1, keepdims=True) + EPS) for oh in _heads(o)])
        don = dy * (gg * sg)
        dgg = dy * (n * hgn) * (sg * (1.0 + gg * (1.0 - sg)))
        d_hgn = sum(_heads(jnp.sum(don * n, axis=0, keepdims=True)))
        dn = don * hgn
        do = _wide(
            [
                lax.rsqrt(jnp.mean(oh * oh, axis=-1, keepdims=True) + EPS) * (dnh - nh * jnp.mean(dnh * nh, axis=-1, keepdims=True))
                for oh, dnh, nh in zip(_heads(o), _heads(dn), _heads(n))
            ]
        )
        sts = [st_ref[j] for j in hs]
        dstn = [dst_scr[j] for j in hs]
        bl = b[CHUNK - 1 : CHUNK, :]
        e_b, e_bl, e_l = jnp.exp(b), jnp.exp(bl - b), jnp.exp(bl)
        doh, vih = _heads(do), _heads(vi)
        dobh = _heads(do.astype(BF16))
        dq_acc = _wide([_dot3(doh[j], sts[j]) for j in hs]) * e_b
        dk_inter = _wide([_dot3(vih[j], dstn[j]) for j in hs]) * e_bl
        dk_acc = dk_inter
        kd = _heads((kk * e_bl).astype(BF16))
        dv_acc = _wide([_dot(kd[j], dstn[j].astype(BF16), NT) for j in hs])
        qe, decay = _heads((qf * e_b).astype(BF16)), _heads(e_l)
        dst_new = [dstn[j] * decay[j] + _dot(dobh[j], qe[j], TN) for j in hs]
        db_last = e_l * _wide([jnp.sum(sts[j] * dstn[j], axis=0, keepdims=True) for j in hs]) + jnp.sum(kk * dk_inter, axis=0, keepdims=True)
        dq_parts = []
        for lo, hi, ea, eb, mask in _intra_blocks(b):
            a, bk = qf[lo:hi] * ea, kk[:hi] * eb
            ah, bkh = _heads(a), _heads(bk)
            abh, bkbh = _heads(a.astype(BF16)), _heads(bk.astype(BF16))
            p = [jnp.where(mask, _dot(abh[j], bkbh[j], NT), 0.0).astype(BF16) for j in hs]
            dp = [jnp.where(mask, _dot3(doh[j][lo:hi], vih[j][:hi], NT), 0.0) for j in hs]
            dq_parts.append(_wide([_dot3(dp[j], bkh[j]) for j in hs]) * ea)
            dki = _wide([_dot3(dp[j], ah[j], TN) for j in hs]) * eb
            dvi = _wide([_dot(p[j], dobh[j][lo:hi], TN) for j in hs])
            if hi < CHUNK:
                zeros = jnp.zeros((CHUNK - hi, HG_BLOCK * HEAD_DIM), F32)
                dki = jnp.concatenate([dki, zeros], axis=0)
                dvi = jnp.concatenate([dvi, zeros], axis=0)
            dk_acc = dk_acc + dki
            dv_acc = dv_acc + dvi
        dq_acc = dq_acc + jnp.concatenate(dq_parts, axis=0)
        rows = lax.broadcasted_iota(jnp.int32, dq_acc.shape, 0)
        db = qf * dq_acc - kk * dk_acc + jnp.where(rows == CHUNK - 1, db_last, 0.0)
        dlf = jnp.dot(_tri(CHUNK, True), db, precision=HIGHEST, preferred_element_type=F32)
        dfk = dlf / f - dk_acc
        dp_ref[0] = (dq_acc * (sq * (1.0 + q * (1.0 - sq)))).astype(BF16)
        dp_ref[1] = ((1.0 - lb) * dfk * sig * (1.0 - sig)).astype(BF16)
        dp_ref[2] = dv_acc.astype(BF16)
        dp_ref[3] = dgg.astype(BF16)
        dlb_scr[...] += jnp.sum(dfk * (1.0 - sig), axis=0, keepdims=True)
        dhgn_ref[...] += d_hgn
        for j in hs:
            dst_scr[j] = dst_new[j]

        @pl.when(c == nc - 1)
        def _():
            dl0 = dlb_scr[...] * lb * (1.0 - lb)
            dlbl_ref[0:1, :] = dl0
            dlbl_ref[1:2, :] = -dl0

    wide = HG_BLOCK * HEAD_DIM
    groups = HG_HEADS // HG_BLOCK

    def col(k):
        return pl.BlockSpec((CHUNK, wide), lambda g, c: (nc - 1 - c, k * groups + g))

    blk = pl.BlockSpec((CHUNK, wide), lambda g, c: (nc - 1 - c, g))
    return _pallas(
        body,
        9,
        dep,
        name="hgrn_bwd",
        grid=(groups, nc),
        in_specs=[
            col(0), col(1), col(2), col(3),
            pl.BlockSpec((2, wide), lambda g, c: (0, g)),
            pl.BlockSpec((1, HEAD_DIM), lambda g, c: (0, 0)),
            blk, blk,
            pl.BlockSpec((HG_BLOCK, None, HEAD_DIM, HEAD_DIM), lambda g, c: (g, nc - 1 - c, 0, 0)),
        ],
        out_specs=[
            pl.BlockSpec((4, CHUNK, wide), lambda g, c: (0, nc - 1 - c, g)),
            pl.BlockSpec((2, wide), lambda g, c: (0, g)),
            pl.BlockSpec((1, HEAD_DIM), lambda g, c: (0, 0)),
        ],
        out_shape=[
            jax.ShapeDtypeStruct((4, t, HG_WIDTH), BF16),
            jax.ShapeDtypeStruct((2, HG_WIDTH), F32),
            jax.ShapeDtypeStruct((1, HEAD_DIM), F32),
        ],
        scratch_shapes=[pltpu.VMEM((HG_BLOCK, HEAD_DIM, HEAD_DIM), F32), pltpu.VMEM((1, wide), F32)],
        compiler_params=_params("arbitrary", "arbitrary"),
    )(proj, proj, proj, proj, lb_logits, hg_norm, o_hg, dycat, states)


def _diagonal_slots(shift):
    i = lax.broadcasted_iota(jnp.int32, (N_REL_PAD, DIAG), 0)
    u = lax.broadcasted_iota(jnp.int32, (N_REL_PAD, DIAG), 1)
    offset = u - shift if shift else jnp.where(u < K_BLOCK, u, u - DIAG)
    return jnp.where(jnp.clip(PAD - offset, -REL_CLIP, REL_CLIP) + REL_CLIP == i, 1.0, 0.0).astype(BF16)


def _split3(x):
    hi = x.astype(BF16)
    mid = (x - hi.astype(F32)).astype(BF16)
    return hi, mid, (x - hi.astype(F32) - mid.astype(F32)).astype(BF16)


def _bias_table(rel_bias):
    def body(rb_ref, o_ref, diag):
        h = pl.program_id(0)

        @pl.when(h == 0)
        def _():
            hi, mid, lo = _split3(rb_ref[...])
            slots = _diagonal_slots(0)
            diag[...] = _dot(hi, slots) + (_dot(mid, slots) + _dot(lo, slots))

        rows = jnp.broadcast_to(diag[pl.ds(h, 1), :], (Q_BLOCK, DIAG))
        o_ref[...] = pltpu.roll(rows, 0, 1, stride=1, stride_axis=0)[:, :K_BLOCK]

    return pl.pallas_call(
        body,
        name="bias_table",
        grid=(ATT_HEADS,),
        in_specs=[pl.BlockSpec((ATT_HEADS, N_REL_PAD), lambda h: (0, 0))],
        out_specs=pl.BlockSpec((None, Q_BLOCK, K_BLOCK), lambda h: (h, 0, 0)),
        out_shape=jax.ShapeDtypeStruct((ATT_HEADS, Q_BLOCK, K_BLOCK), F32),
        scratch_shapes=[pltpu.VMEM((ATT_HEADS, DIAG), F32)],
        compiler_params=_params("arbitrary"),
    )(rel_bias)


def _att_probs(q_ref, kpad, bias_ref, blk):
    qs = (q_ref[...] * ATT_SCALE).astype(BF16)
    start = pl.multiple_of(blk * Q_BLOCK, Q_BLOCK)
    kb = kpad[pl.ds(start, K_BLOCK), :]
    s = _dot(qs, kb, NT) + bias_ref[...]
    row = lax.broadcasted_iota(jnp.int32, (Q_BLOCK, K_BLOCK), 0)
    col = lax.broadcasted_iota(jnp.int32, (Q_BLOCK, K_BLOCK), 1)
    first = row - (row & (CHUNK - 1))
    valid = (col >= first) & (col < first + BAND) & (col + (blk * Q_BLOCK - PAD) >= 0)
    s = jnp.where(valid, s, jnp.finfo(F32).min)
    e = jnp.exp(s - jnp.max(s, axis=-1, keepdims=True))
    return qs, kb, start, e / jnp.sum(e, axis=-1, keepdims=True)


def _fill_padded(dst, src):
    dst[0:PAD, :] = jnp.zeros((PAD, HEAD_DIM), BF16)
    dst[PAD:, :] = src[...].astype(BF16)


def _att_fwd(proj, bias, dep=None):
    t = proj.shape[0]
    nb = t // Q_BLOCK

    def body(q_ref, k_ref, v_ref, bias_ref, y_ref, kpad, vpad):
        c = pl.program_id(1)

        @pl.when(c == 0)
        def _():
            _fill_padded(kpad, k_ref)
            _fill_padded(vpad, v_ref)

        _, _, start, p = _att_probs(q_ref, kpad, bias_ref, c)
        y_ref[...] = _dot(p.astype(BF16), vpad[pl.ds(start, K_BLOCK), :]).astype(BF16)

    base = 4 * HG_HEADS
    return _pallas(
        body,
        4,
        dep,
        name="att_fwd",
        grid=(ATT_HEADS, nb),
        in_specs=[
            pl.BlockSpec((Q_BLOCK, HEAD_DIM), lambda h, c: (c, base + h)),
            pl.BlockSpec((t, HEAD_DIM), lambda h, c: (0, base + ATT_HEADS + h)),
            pl.BlockSpec((t, HEAD_DIM), lambda h, c: (0, base + 2 * ATT_HEADS + h)),
            pl.BlockSpec((None, Q_BLOCK, K_BLOCK), lambda h, c: (h, 0, 0)),
        ],
        out_specs=pl.BlockSpec((Q_BLOCK, HEAD_DIM), lambda h, c: (c, h)),
        out_shape=jax.ShapeDtypeStruct((t, ATT_WIDTH), BF16),
        scratch_shapes=[pltpu.VMEM((t + PAD, HEAD_DIM), BF16), pltpu.VMEM((t + PAD, HEAD_DIM), BF16)],
        compiler_params=_params("arbitrary", "arbitrary"),
    )(proj, proj, proj, bias)


def _att_bwd(proj, bias, dycat, dep=None):
    t = proj.shape[0]
    nb = t // Q_BLOCK

    def body(q_ref, k_ref, v_ref, bias_ref, dy_ref, dq_ref, dk_ref, dv_ref, g_ref, kpad, vpad, dkacc, dvacc):
        c = pl.program_id(1)

        @pl.when(c == 0)
        def _():
            _fill_padded(kpad, k_ref)
            _fill_padded(vpad, v_ref)
            dkacc[...] = jnp.zeros_like(dkacc)
            dvacc[...] = jnp.zeros_like(dvacc)
            g_ref[...] = jnp.zeros_like(g_ref)

        qs, kb, start, p = _att_probs(q_ref, kpad, bias_ref, c)
        band = pl.ds(start, K_BLOCK)
        dyb = dy_ref[...].astype(BF16)
        dvacc[band, :] += _dot(p.astype(BF16), dyb, TN)
        dp = _dot(dyb, vpad[band, :], NT)
        ds = p * (dp - jnp.sum(dp * p, axis=-1, keepdims=True))
        g_ref[...] += ds
        dsb = ds.astype(BF16)
        dq_ref[...] = (_dot(dsb, kb) * ATT_SCALE).astype(BF16)
        dkacc[band, :] += _dot(dsb, qs, TN)

        @pl.when(c == nb - 1)
        def _():
            dk_ref[...] = dkacc[PAD:, :].astype(BF16)
            dv_ref[...] = dvacc[PAD:, :].astype(BF16)

    base = 4 * HG_HEADS
    whole = pl.BlockSpec((t, HEAD_DIM), lambda h, c: (0, h))
    return _pallas(
        body,
        5,
        dep,
        name="att_bwd",
        grid=(ATT_HEADS, nb),
        in_specs=[
            pl.BlockSpec((Q_BLOCK, HEAD_DIM), lambda h, c: (c, base + h)),
            pl.BlockSpec((t, HEAD_DIM), lambda h, c: (0, base + ATT_HEADS + h)),
            pl.BlockSpec((t, HEAD_DIM), lambda h, c: (0, base + 2 * ATT_HEADS + h)),
            pl.BlockSpec((None, Q_BLOCK, K_BLOCK), lambda h, c: (h, 0, 0)),
            pl.BlockSpec((Q_BLOCK, HEAD_DIM), lambda h, c: (c, HG_HEADS + h)),
        ],
        out_specs=[pl.BlockSpec((Q_BLOCK, HEAD_DIM), lambda h, c: (c, h)), whole, whole, pl.BlockSpec((None, Q_BLOCK, K_BLOCK), lambda h, c: (h, 0, 0))],
        out_shape=[
            jax.ShapeDtypeStruct((t, ATT_WIDTH), BF16),
            jax.ShapeDtypeStruct((t, ATT_WIDTH), BF16),
            jax.ShapeDtypeStruct((t, ATT_WIDTH), BF16),
            jax.ShapeDtypeStruct((ATT_HEADS, Q_BLOCK, K_BLOCK), F32),
        ],
        scratch_shapes=[
            pltpu.VMEM((t + PAD, HEAD_DIM), BF16),
            pltpu.VMEM((t + PAD, HEAD_DIM), BF16),
            pltpu.VMEM((t + PAD, HEAD_DIM), F32),
            pltpu.VMEM((t + PAD, HEAD_DIM), F32),
        ],
        compiler_params=_params("arbitrary", "arbitrary"),
    )(proj, proj, proj, bias, dycat)


def _rel_bias_grad(gsum):
    def body(g_ref, o_ref):
        r = lax.broadcasted_iota(jnp.int32, (Q_BLOCK, Q_BLOCK), 0)
        c = lax.broadcasted_iota(jnp.int32, (Q_BLOCK, Q_BLOCK), 1)
        flip = jnp.where(r + c == Q_BLOCK - 1, 1.0, 0.0).astype(BF16)
        sums = []
        for h in range(ATT_HEADS):
            hi, mid, lo = _split3(g_ref[h])
            rev = _dot(flip, hi) + (_dot(flip, mid) + _dot(flip, lo))
            wide = jnp.concatenate([rev, jnp.zeros((Q_BLOCK, DIAG - K_BLOCK), F32)], axis=1)
            sums.append(jnp.sum(pltpu.roll(wide, 0, 1, stride=1, stride_axis=0), axis=0, keepdims=True))
        hi, mid, lo = _split3(jnp.concatenate(sums, axis=0))
        slots = _diagonal_slots(Q_BLOCK - 1)
        o_ref[...] = _dot(hi, slots, NT) + (_dot(mid, slots, NT) + _dot(lo, slots, NT))

    return pl.pallas_call(
        body,
        name="rel_bias_grad",
        out_shape=jax.ShapeDtypeStruct((ATT_HEADS, N_REL_PAD), F32),
        compiler_params=_params(),
    )(gsum)


HALO = 16


FF_TILE = 1408
FF_TILES = D_FF // FF_TILE


def _interleave_cols(a):
    lead = a.shape[:-1]
    return jnp.swapaxes(a.reshape(*lead, 2, FF_TILES, FF_TILE), -3, -2).reshape(*lead, 2 * D_FF)


def _deinterleave_cols(a):
    lead = a.shape[:-1]
    return jnp.swapaxes(a.reshape(*lead, FF_TILES, 2, FF_TILE), -3, -2).reshape(*lead, 2 * D_FF)


def _ffn_specs(t, tm):
    wide = 2 * FF_TILE
    tile = pl.BlockSpec((tm, wide), lambda j, i: (i, j))
    before = pl.BlockSpec((HALO, wide), lambda j, i: (jnp.maximum(i * (tm // HALO) - 1, 0), j))
    after = pl.BlockSpec((HALO, wide), lambda j, i: (jnp.minimum((i + 1) * (tm // HALO), t // HALO - 1), j))
    vec = lambda rows: pl.BlockSpec((rows, wide), lambda j, i: (0, j))
    return tile, before, after, vec


def _conv(x, w, b, rows):
    taps = [pltpu.roll(x, 2, 0)[HALO : HALO + rows], pltpu.roll(x, 1, 0)[HALO : HALO + rows], x[HALO : HALO + rows]]
    return b + w[0:1] * taps[0] + w[1:2] * taps[1] + w[2:3] * taps[2], taps


def _ffn_act_fwd(u, conv_w, conv_b):
    t = u.shape[0]
    tm = _tile(t, (128,))
    tile, before, _, vec = _ffn_specs(t, tm)

    def body(u_ref, h_ref, w_ref, b_ref, z_ref):
        first = pl.program_id(1) == 0
        x = jnp.concatenate([jnp.where(first, 0.0, h_ref[...].astype(F32)), u_ref[...].astype(F32)], axis=0)
        c, _ = _conv(x, w_ref[...], b_ref[...], tm)
        gate, val = c[:, :FF_TILE], c[:, FF_TILE:]
        z_ref[...] = (gate * _sigmoid(gate) * val).astype(BF16)

    return pl.pallas_call(
        body,
        name="ffn_act_fwd",
        grid=(FF_TILES, t // tm),
        in_specs=[tile, before, vec(3), vec(1)],
        out_specs=pl.BlockSpec((tm, FF_TILE), lambda j, i: (i, j)),
        out_shape=jax.ShapeDtypeStruct((t, D_FF), BF16),
        compiler_params=_params("parallel", "parallel"),
    )(u, u, conv_w, conv_b)


def _ffn_act_bwd(u, dz, conv_w, conv_b, dep=None):
    t = u.shape[0]
    tm = _tile(t, (128,))
    nt = t // tm
    ext = tm + HALO
    tile, before, after, vec = _ffn_specs(t, tm)

    def body(u_ref, ub_ref, ua_ref, w_ref, b_ref, dz_ref, dza_ref, du_ref, dw_ref, db_ref):
        i = pl.program_id(1)
        first, last = i == 0, i == nt - 1
        parts = [jnp.where(first, 0.0, ub_ref[...].astype(F32)), u_ref[...].astype(F32), jnp.where(last, 0.0, ua_ref[...].astype(F32))]
        w = w_ref[...]
        c, taps = _conv(jnp.concatenate(parts, axis=0), w, b_ref[...], ext)
        gate, val = c[:, :FF_TILE], c[:, FF_TILE:]
        dz = jnp.concatenate([dz_ref[...].astype(F32), jnp.where(last, 0.0, dza_ref[...].astype(F32))], axis=0)
        sg = _sigmoid(gate)
        d = jnp.concatenate([dz * val * (sg * (1.0 + gate * (1.0 - sg))), dz * (gate * sg)], axis=1)
        du = w[2:3] * d[:tm] + w[1:2] * pltpu.roll(d, ext - 1, 0)[:tm] + w[0:1] * pltpu.roll(d, ext - 2, 0)[:tm]
        du_ref[...] = du.astype(BF16)

        @pl.when(first)
        def _():
            dw_ref[...] = jnp.zeros_like(dw_ref)
            db_ref[...] = jnp.zeros_like(db_ref)

        for k, tap in enumerate(taps):
            dw_ref[k : k + 1, :] += jnp.sum(d[:tm] * tap[:tm], axis=0, keepdims=True)
        db_ref[...] += jnp.sum(d[:tm], axis=0, keepdims=True)

    narrow = lambda rows, index: pl.BlockSpec((rows, FF_TILE), index)
    return _pallas(
        body,
        7,
        dep,
        name="ffn_act_bwd",
        grid=(FF_TILES, nt),
        in_specs=[
            tile, before, after, vec(3), vec(1),
            narrow(tm, lambda j, i: (i, j)),
            narrow(HALO, lambda j, i: (jnp.minimum((i + 1) * (tm // HALO), t // HALO - 1), j)),
        ],
        out_specs=[tile, vec(3), vec(1)],
        out_shape=[
            jax.ShapeDtypeStruct((t, 2 * D_FF), BF16),
            jax.ShapeDtypeStruct((3, 2 * D_FF), F32),
            jax.ShapeDtypeStruct((1, 2 * D_FF), F32),
        ],
        compiler_params=_params("parallel", "arbitrary"),
    )(u, u, u, conv_w, conv_b, dz, dz)


def _ple_loss(gpre, pp, h2, final_norm, target):
    t, d = h2.shape
    tm = _tile(t, (256,))

    def body(gp_ref, pp_ref, h_ref, g_ref, tg_ref, dh_ref, dgp_ref, dpp_ref, dg_ref, loss_ref):
        i = pl.program_id(0)
        gate = _sigmoid(gp_ref[...])
        ppv = pp_ref[...]
        h3 = h_ref[...] + gate * ppv
        r = lax.rsqrt(jnp.mean(h3 * h3, axis=-1, keepdims=True) + EPS)
        n = h3 * r
        g = g_ref[...]
        err = n * g - tg_ref[...]
        loss = 0.5 * jnp.sum(jnp.mean(err * err, axis=-1, keepdims=True))
        dy = err * (1.0 / d)
        dn = dy * g
        dh = r * (dn - n * jnp.mean(dn * n, axis=-1, keepdims=True))
        dh_ref[...] = dh
        dgp_ref[...] = (dh * ppv * gate * (1.0 - gate)).astype(BF16)
        dpp_ref[...] = (dh * gate).astype(BF16)
        dg = jnp.sum(dy * n, axis=0, keepdims=True)

        @pl.when(i == 0)
        def _():
            dg_ref[...] = dg
            loss_ref[...] = jnp.full(loss_ref.shape, loss, F32)

        @pl.when(i > 0)
        def _():
            dg_ref[...] += dg
            loss_ref[...] += loss

    row = pl.BlockSpec((tm, d), lambda i: (i, 0))
    vec = pl.BlockSpec((1, d), lambda i: (0, 0))
    return pl.pallas_call(
        body,
        name="ple_loss",
        grid=(t // tm,),
        in_specs=[row, row, row, vec, row],
        out_specs=[row, row, row, vec, pl.BlockSpec((8, 128), lambda i: (0, 0))],
        out_shape=[
            jax.ShapeDtypeStruct((t, d), F32),
            jax.ShapeDtypeStruct((t, d), BF16),
            jax.ShapeDtypeStruct((t, d), BF16),
            jax.ShapeDtypeStruct((1, d), F32),
            jax.ShapeDtypeStruct((8, 128), F32),
        ],
        compiler_params=_params("arbitrary"),
    )(gpre, pp, h2, final_norm, target)


def _adamw(w, g, m, v):
    m = ADAM_B1 * m + (1.0 - ADAM_B1) * g
    v = ADAM_B2 * v + (1.0 - ADAM_B2) * (g * g)
    m_hat = m / (1.0 - ADAM_B1 ** ADAM_STEP)
    v_hat = v / (1.0 - ADAM_B2 ** ADAM_STEP)
    return -ADAM_LR * (m_hat / (jnp.sqrt(v_hat) + ADAM_EPS) + ADAM_WD * w), m, v


def _adam_big(w, m, v, own, recv, name):
    r, c = w.shape
    tr = _tile(r, (256, 176))

    def body(w_ref, m_ref, v_ref, own_ref, recv_ref, g_ref, d_ref, nm_ref, nv_ref):
        g = own_ref[...]
        for k in range(3):
            g = g + recv_ref[k].astype(F32)
        g_ref[...] = g
        d_ref[...], nm_ref[...], nv_ref[...] = _adamw(w_ref[...], g, m_ref[...], v_ref[...])

    blk = pl.BlockSpec((tr, c), lambda i: (i, 0))
    return pl.pallas_call(
        body,
        name=name,
        grid=(r // tr,),
        in_specs=[blk, blk, blk, blk, pl.BlockSpec((3, tr, c), lambda i: (0, i, 0))],
        out_specs=[blk] * 4,
        out_shape=[jax.ShapeDtypeStruct((r, c), F32)] * 4,
        compiler_params=_params("parallel"),
    )(w, m, v, own, recv)


def _adam_small(w, g, m, v):
    def body(w_ref, g_ref, m_ref, v_ref, d_ref, nm_ref, nv_ref):
        d_ref[...], nm_ref[...], nv_ref[...] = _adamw(w_ref[...], g_ref[...], m_ref[...], v_ref[...])

    return pl.pallas_call(body, name="adam_small", out_shape=[jax.ShapeDtypeStruct(w.shape, F32)] * 3, compiler_params=_params())(w, g, m, v)


def _cast_bf16(w, name):
    r, c = w.shape
    tr = _tile(r, (256, 176))

    def body(w_ref, o_ref):
        o_ref[...] = w_ref[...].astype(BF16)

    blk = pl.BlockSpec((tr, c), lambda i: (i, 0))
    return pl.pallas_call(
        body, name=name, grid=(r // tr,), in_specs=[blk], out_specs=blk, out_shape=jax.ShapeDtypeStruct((r, c), BF16), compiler_params=_params("parallel")
    )(w)


def _position():
    return lax.axis_index("x"), lax.axis_index("y"), lax.axis_index("c")


def _other_chips(x, y):
    return [(1 - x, y), (x, 1 - y), (1 - x, 1 - y)]


def _block_index(dev, interleaved):
    x, y, c = dev
    return 4 * y + 2 * c + x if interleaved else 4 * x + 2 * y + c


def _shard_of(ref, axis, size, dev, interleaved=False):
    start = pl.multiple_of(_block_index(dev, interleaved) * size, 128 if axis == 1 else 16)
    return ref.at[:, pl.ds(start, size)] if axis == 1 else ref.at[pl.ds(start, size), :]


def _all_gather(shards, axes, interleaved):
    n = len(shards)

    def body(*refs):
        ins, outs = refs[:n], refs[n : 2 * n]
        send_sems, recv_sems, local_sems = refs[2 * n :]
        x, y, c = _position()
        me, sibling = (x, y, c), (x, y, 1 - c)
        chips = _other_chips(x, y)
        firsts, passed, locals_ = [], [], []
        for w in range(n):
            size = shards[w].shape[axes[w]]
            slot = functools.partial(_shard_of, outs[w], axes[w], size, interleaved=interleaved[w])

            def copy(k, block, to, src=None, w=w, slot=slot):
                return pltpu.make_async_remote_copy(
                    src_ref=slot(block) if src is None else src,
                    dst_ref=slot(block),
                    send_sem=send_sems.at[7 * w + k],
                    recv_sem=recv_sems.at[7 * w + k],
                    device_id=to,
                    device_id_type=MESH,
                )

            mine = pltpu.make_async_copy(ins[w], slot(me), local_sems.at[w])
            mine.start()
            locals_.append(mine)
            first = [copy(0, me, sibling, src=ins[w])] + [copy(1 + j, me, (*chip, c), src=ins[w]) for j, chip in enumerate(chips)]
            for cp in first:
                cp.start()
            firsts.append((first, copy))
        for w in range(n):
            first, copy = firsts[w]
            fwd = [copy(4 + j, (*chip, c), sibling) for j, chip in enumerate(chips)]
            for j, chip in enumerate(chips):
                copy(1 + j, (*chip, c), me).wait_recv()
                fwd[j].start()
            passed.append(fwd)
        for w in range(n):
            first, copy = firsts[w]
            copy(0, sibling, me).wait_recv()
            for j, chip in enumerate(chips):
                copy(4 + j, (*chip, 1 - c), me).wait_recv()
            for cp in first + passed[w]:
                cp.wait_send()
            locals_[w].wait()

    def full(s, ax):
        shape = list(s.shape)
        shape[ax] *= N_DEV
        return jax.ShapeDtypeStruct(tuple(shape), s.dtype)

    return pl.pallas_call(
        body,
        name="all_gather_weights",
        in_specs=[ANY] * n,
        out_specs=[ANY] * n,
        out_shape=[full(s, ax) for s, ax in zip(shards, axes)],
        scratch_shapes=[pltpu.SemaphoreType.DMA((7 * n,)), pltpu.SemaphoreType.DMA((7 * n,)), pltpu.SemaphoreType.DMA((n,))],
    )(*shards)


def _add_blocks(ids, grad, landed, axis, size, targets, out_dtype, name):
    rows = size if axis == 0 else grad.shape[0]
    cols = size if axis == 1 else grad.shape[1]
    tr = _tile(rows, (256, 176))
    nr = rows // tr
    nt = len(targets)

    def body(ids_ref, g_ref, l_ref, o_ref):
        o_ref[...] = (g_ref[...] + l_ref[...]).astype(out_dtype)

    if axis == 1:
        g_spec = pl.BlockSpec((tr, cols), lambda k, i, ids: (i, ids[targets[0] + k]))
    else:
        g_spec = pl.BlockSpec((tr, cols), lambda k, i, ids: (ids[targets[0] + k] * nr + i, 0))
    return pl.pallas_call(
        body,
        name=name,
        grid_spec=pltpu.PrefetchScalarGridSpec(
            num_scalar_prefetch=1,
            grid=(nt, nr),
            in_specs=[g_spec, pl.BlockSpec((None, tr, cols), lambda k, i, ids: (ids[4 + targets[0] + k], i, 0))],
            out_specs=pl.BlockSpec((None, tr, cols), lambda k, i, ids: (k, i, 0)),
        ),
        out_shape=jax.ShapeDtypeStruct((nt, rows, cols), out_dtype),
        compiler_params=_params("parallel", "parallel"),
    )(ids, grad, landed)


def _all_reduce_small(vec):
    rows = vec.shape[0]

    def body(v_ref, o_ref, land, send_sems, recv_sems):
        x, y, c = _position()
        mine = 4 * x + 2 * y + c
        copies = []
        for mask in range(1, N_DEV):
            peer = (1 - x if mask & 4 else x, 1 - y if mask & 2 else y, 1 - c if mask & 1 else c)
            copies.append(
                pltpu.make_async_remote_copy(
                    src_ref=v_ref, dst_ref=land.at[mine], send_sem=send_sems.at[mask - 1], recv_sem=recv_sems.at[mask - 1], device_id=peer, device_id_type=MESH
                )
            )
        for cp in copies:
            cp.start()
        land[mine] = v_ref[...]
        for cp in copies:
            cp.wait()
        acc = land[0]
        for k in range(1, N_DEV):
            acc = acc + land[k]
        o_ref[...] = acc

    return pl.pallas_call(
        body,
        name="all_reduce_small",
        out_shape=jax.ShapeDtypeStruct(vec.shape, F32),
        in_specs=[pl.BlockSpec(memory_space=pltpu.VMEM)],
        out_specs=pl.BlockSpec(memory_space=pltpu.VMEM),
        scratch_shapes=[pltpu.VMEM((N_DEV, rows, 128), F32), pltpu.SemaphoreType.DMA((N_DEV - 1,)), pltpu.SemaphoreType.DMA((N_DEV - 1,))],
    )(vec)


def _rows128(a, rows):
    flat = a.reshape(-1)
    return jnp.pad(flat, (0, rows * 128 - flat.shape[0])).reshape(rows, 128)


def _pad_rel(a):
    return jnp.pad(a.reshape(ATT_HEADS, -1)[:, :N_REL], ((0, 0), (0, N_REL_PAD - N_REL)))


SMALL = [("norm_mix", 16), ("lb_logits", 16), ("hg_norm", 8), ("rel_bias", 24), ("norm_ffn", 16), ("conv_b", 88), ("norm_ple", 16), ("final_norm", 16)]
CONV_W_FULL_ROWS = 3 * 2 * D_FF // 128
CONV_W_SHARD_ROWS = 40


def _pack_small(parts):
    return jnp.concatenate([_rows128(_pad_rel(parts[k]) if k == "rel_bias" else parts[k], rows) for k, rows in SMALL], axis=0)


def _unpack_small(packed, shapes):
    out, at = {}, 0
    for k, rows in SMALL:
        blk = packed[at : at + rows]
        at += rows
        if k == "rel_bias":
            out[k] = blk.reshape(ATT_HEADS, N_REL_PAD)[:, :N_REL].reshape(shapes[k])
        else:
            n = 1
            for s in shapes[k]:
                n *= s
            out[k] = blk.reshape(-1)[:n].reshape(shapes[k])
    return out, at


BIG = [("w_in", 1), ("w_out", 0), ("w_up", 1), ("w_down", 0), ("w_ple_gate", 0), ("w_ple_proj", 1)]


HBM = pl.BlockSpec(memory_space=pltpu.HBM)
SEM = pl.BlockSpec(memory_space=pltpu.SEMAPHORE)
EFFECT = pltpu.SideEffectType.DATAFLOW_SIDE_EFFECTING


def _copies(plan, refs, send_sems, recv_sems):
    return [
        pltpu.make_async_remote_copy(src_ref=src, dst_ref=dst, send_sem=send_sems.at[i], recv_sem=recv_sems.at[i], device_id=dev, device_id_type=MESH)
        for i, (src, dst, dev) in enumerate(plan(refs))
    ]


def _split_start(name, arrays, plan, n):
    k = len(arrays)

    def body(*refs):
        for cp in _copies(plan, refs[:k], refs[k], refs[k + 1]):
            cp.start()
        refs[-1][...] = jnp.zeros_like(refs[-1])

    out = pl.pallas_call(
        body,
        name=name,
        out_shape=(pltpu.SemaphoreType.DMA((n,)), pltpu.SemaphoreType.DMA((n,)), *[pltpu.HBM(a.shape, a.dtype) for a in arrays], jax.ShapeDtypeStruct((8, 128), F32)),
        in_specs=[HBM] * k,
        out_specs=(SEM, SEM, *[HBM] * k, pl.BlockSpec(memory_space=pltpu.VMEM)),
        input_output_aliases={i: 2 + i for i in range(k)},
        compiler_params=pltpu.CompilerParams(has_side_effects=EFFECT),
    )(*[pltpu.with_memory_space_constraint(a, pltpu.HBM) for a in arrays])
    return out[0], out[1], list(out[2 : 2 + k]), out[-1]


def _split_wait(name, send, recv, arrays, plan, after):
    k = len(arrays)

    def body(*refs):
        for cp in _copies(plan, refs[:k], refs[k], refs[k + 1]):
            cp.wait_send()
            cp.wait_recv()

    out = pl.pallas_call(
        body,
        name=name,
        out_shape=tuple(pltpu.HBM(a.shape, a.dtype) for a in arrays),
        in_specs=[HBM] * k + [SEM, SEM, ANY],
        out_specs=tuple([HBM] * k),
        input_output_aliases={i: i for i in range(k)},
        compiler_params=pltpu.CompilerParams(has_side_effects=EFFECT),
    )(*arrays, send, recv, after)
    return list(out)


def _cast_into(w, me, axis, name, dep):
    r, c = w.shape
    tr = _tile(r, (256, 176))
    nr = r // tr

    def body(me_ref, w_ref, dep_ref, o_ref):
        o_ref[...] = w_ref[...].astype(BF16)

    if axis == 1:
        shape, o_spec = (r, N_DEV * c), pl.BlockSpec((tr, c), lambda i, me: (i, me[0]))
    else:
        shape, o_spec = (N_DEV * r, c), pl.BlockSpec((tr, c), lambda i, me: (me[0] * nr + i, 0))
    return pl.pallas_call(
        body,
        name=name,
        grid_spec=pltpu.PrefetchScalarGridSpec(
            num_scalar_prefetch=1, grid=(nr,), in_specs=[pl.BlockSpec((tr, c), lambda i, me: (i, 0)), ANY], out_specs=o_spec
        ),
        out_shape=jax.ShapeDtypeStruct(shape, BF16),
        compiler_params=_params("parallel"),
    )(me, w, dep)


LATE = ["w_out", "w_up", "w_down", "w_ple_gate", "w_ple_proj"]
GROUPS = [["w_ple_proj", "w_ple_gate", "w_down"], ["w_up"], ["w_out"], ["w_in"]]
STAGES = ["ffn_act_bwd", "d_mix_out", "hgrn_bwd", "d_norm_mix_out"]
INTERLEAVED = {"w_up"}


class _Exchange:
    def __init__(self, big, position):
        self.big, self.axis = big, dict(BIG)
        self.size = {k: big[k].shape[self.axis[k]] for k in big}
        self.x, self.y, self.c = position
        chips = [(self.x, self.y)] + _other_chips(self.x, self.y)
        landed = [2 * cx + cy for cx, cy in chips]
        self.ids = {
            flag: jnp.stack([_block_index((cx, cy, self.c), flag) for cx, cy in chips] + landed).astype(jnp.int32) for flag in (False, True)
        }
        self.token, self.grads, self.state, self.wfull = None, {}, {}, {}


    def _slot(self, ref, k, dev):
        return _shard_of(ref, self.axis[k], self.size[k], dev, interleaved=k in INTERLEAVED)

    def _plan_gather(self, refs):
        x, y, c = _position()
        me, out = (x, y, c), []
        for k, ref in zip(LATE, refs):
            mine = self._slot(ref, k, me)
            out.append((mine, mine, (x, y, 1 - c)))
            out += [(mine, mine, (*chip, c)) for chip in _other_chips(x, y)]
        return out

    def _plan_forward(self, refs):
        x, y, c = _position()
        out = []
        for k, ref in zip(LATE, refs):
            for chip in _other_chips(x, y):
                block = self._slot(ref, k, (*chip, c))
                out.append((block, block, (x, y, 1 - c)))
        return out

    def _plan_sibling(self, names, refs):
        x, y, c = _position()
        n = len(names)
        return [(self._slot(refs[i], k, (p // 2, p % 2, 1 - c)), refs[n + i].at[p], (x, y, 1 - c)) for i, k in enumerate(names) for p in range(4)]

    def _plan_chips(self, names, refs):
        x, y, c = _position()
        n = len(names)
        return [(refs[i].at[j], refs[n + i].at[j], (*chip, c)) for i in range(n) for j, chip in enumerate(_other_chips(x, y))]


    def gather(self, conv_w):
        w_in, conv_full = _all_gather([_cast_bf16(self.big["w_in"], "cast_w_in"), conv_w], [1, 1], [False, True])
        self.wfull["w_in"] = w_in
        me = {flag: _block_index((self.x, self.y, self.c), flag).astype(jnp.int32).reshape(1) for flag in (False, True)}
        fulls = [_cast_into(self.big[k], me[k in INTERLEAVED], self.axis[k], "cast_" + k, w_in) for k in LATE]
        send, recv, fulls, self.token = _split_start("gather_start", fulls, self._plan_gather, 4 * len(LATE))
        self.late = (send, recv, fulls)
        return conv_full

    def weight(self, k):
        return self.wfull[k]

    def dep(self):
        token, self.token = self.token, None
        return token

    def grad(self, k, g):
        self.grads[k] = g
        for gi, names in enumerate(GROUPS):
            if k == names[-1]:
                plan = functools.partial(self._plan_sibling, names)
                lands = [lax.empty((4, *self._shard_shape(n)), F32) for n in names]
                send, recv, arrays, self.token = _split_start(f"sibling_start_{gi}", [self.grads[n] for n in names] + lands, plan, 4 * len(names))
                self.state[gi] = (send, recv, arrays, plan)

    def done(self, stage, after):
        if stage == "hgrn_fwd":
            send, recv, fulls = self.late
            fulls = _split_wait("gather_wait", send, recv, fulls, self._plan_gather, after)
            send, recv, fulls, self.token = _split_start("forward_start", fulls, self._plan_forward, 3 * len(LATE))
            self.late = (send, recv, fulls)
        elif stage == "att_fwd":
            send, recv, fulls = self.late
            self.wfull.update(zip(LATE, _split_wait("forward_wait", send, recv, fulls, self._plan_forward, after)))
        elif stage in STAGES:
            self._to_chips(STAGES.index(stage), after)

    def _shard_shape(self, k):
        shape = list(self.grads[k].shape)
        shape[self.axis[k]] = self.size[k]
        return tuple(shape)

    def _to_chips(self, gi, after):
        names = GROUPS[gi]
        n = len(names)
        send, recv, arrays, plan = self.state[gi]
        arrays = _split_wait(f"sibling_wait_{gi}", send, recv, arrays, plan, after)
        own, parts = [], []
        for k, g, land in zip(names, arrays[:n], arrays[n:]):
            ids = self.ids[k in INTERLEAVED]
            own.append(_add_blocks(ids, g, land, self.axis[k], self.size[k], [0], F32, "add_own_" + k)[0])
            parts.append(_add_blocks(ids, g, land, self.axis[k], self.size[k], [1, 2, 3], BF16, "add_send_" + k))
        plan = functools.partial(self._plan_chips, names)
        lands = [lax.empty(part.shape, BF16) for part in parts]
        send, recv, arrays, self.token = _split_start(f"chips_start_{gi}", parts + lands, plan, 3 * n)
        self.state[gi] = (send, recv, arrays, plan, own)

    def finish(self, gi, after):
        names = GROUPS[gi]
        send, recv, arrays, plan, own = self.state[gi]
        arrays = _split_wait(f"chips_wait_{gi}", send, recv, arrays, plan, after)
        return {k: (o, r) for k, o, r in zip(names, own, arrays[len(names) :])}


class _Resident:
    def __init__(self, wfull):
        self.wfull, self.grads = wfull, {}

    def weight(self, k):
        return self.wfull[k]

    def grad(self, k, g):
        self.grads[k] = g

    def dep(self):
        return None

    def done(self, stage, after):
        pass


def _local_step(x, p, target, small, conv_w, ex):
    a1, r1 = _rms_fwd(x, small["norm_mix"], "norm_mix_fwd", dep=ex.dep())
    proj = _matmul(a1, ex.weight("w_in"), "nn", F32, "in_proj")
    bias = _bias_table(jnp.pad(small["rel_bias"], ((0, 0), (0, N_REL_PAD - N_REL))))
    y_hg, o_hg, states = _hgrn_fwd(proj, small["lb_logits"], small["hg_norm"])
    ex.done("hgrn_fwd", y_hg)
    y_att = _att_fwd(proj, bias, dep=ex.dep())
    ex.done("att_fwd", y_att)
    ycat = jnp.concatenate([y_hg, y_att], axis=1)
    h1 = _matmul(ycat, ex.weight("w_out"), "nn", F32, "out_proj", resid=x)
    a2, r2 = _rms_fwd(h1, small["norm_ffn"], "norm_ffn_fwd")
    u = _matmul(a2, ex.weight("w_up"), "nn", BF16, "up_proj")
    conv_b = _interleave_cols(small["conv_b"])
    z = _ffn_act_fwd(u, conv_w, conv_b)
    h2 = _matmul(z, ex.weight("w_down"), "nn", F32, "down_proj", tk=2816, resid=h1)
    a3, r3 = _rms_fwd(h2, small["norm_ple"], "norm_ple_fwd")
    gpre = _matmul(a3, ex.weight("w_ple_gate"), "nn", F32, "ple_gate")
    pp = _matmul(p, ex.weight("w_ple_proj"), "nn", F32, "ple_proj")
    dh3, dgpre, dpp, d_final, loss = _ple_loss(gpre, pp, h2, small["final_norm"], target)

    ex.grad("w_ple_proj", _matmul(p, dpp, "tn", F32, "d_w_ple_proj", tk=2048))
    ex.grad("w_ple_gate", _matmul(a3, dgpre, "tn", F32, "d_w_ple_gate", tk=2048))
    da3 = _matmul(dgpre, ex.weight("w_ple_gate"), "nt", F32, "d_norm_ple_out")
    dh2, d_ple = _rms_bwd(da3, h2, r3, small["norm_ple"], dh3, "norm_ple_bwd")
    dz = _matmul(dh2, ex.weight("w_down"), "nt", BF16, "d_ffn_act")
    ex.grad("w_down", _matmul(z, dh2, "tn", F32, "d_w_down", tk=2048))
    du, dcw, dcb = _ffn_act_bwd(u, dz, conv_w, conv_b, dep=ex.dep())
    ex.done("ffn_act_bwd", du)
    d_conv_w, d_conv_b = _deinterleave_cols(dcw), _deinterleave_cols(dcb)
    ex.grad("w_up", _matmul(a2, du, "tn", F32, "d_w_up", tk=2048, dep=ex.dep()))
    da2 = _matmul(du, ex.weight("w_up"), "nt", F32, "d_norm_ffn_out", tk=2816, dep=ex.dep())
    dh1, d_ffn = _rms_bwd(da2, h1, r2, small["norm_ffn"], dh2, "norm_ffn_bwd")
    dycat = _matmul(dh1, ex.weight("w_out"), "nt", F32, "d_mix_out")
    ex.done("d_mix_out", dycat)
    ex.grad("w_out", _matmul(ycat, dh1, "tn", F32, "d_w_out", tk=2048, dep=ex.dep()))
    dp_hg, d_lb, d_hgn = _hgrn_bwd(proj, small["lb_logits"], small["hg_norm"], o_hg, dycat, states, dep=ex.dep())
    ex.done("hgrn_bwd", d_lb)
    dq_att, dk_att, dv_att, gsum = _att_bwd(proj, bias, dycat, dep=ex.dep())
    d_rel = _rel_bias_grad(gsum)
    dproj = jnp.concatenate([dp_hg[0], dp_hg[1], dp_hg[2], dp_hg[3], dq_att, dk_att, dv_att], axis=1)
    ex.grad("w_in", _matmul(a1, dproj, "tn", F32, "d_w_in", tk=2048))
    da1 = _matmul(dproj, ex.weight("w_in"), "nt", F32, "d_norm_mix_out", tk=1792, dep=ex.dep())
    ex.done("d_norm_mix_out", da1)
    dx, d_mix = _rms_bwd(da1, x, r1, small["norm_mix"], dh1, "norm_mix_bwd", dep=ex.dep())
    d_small = {
        "norm_mix": d_mix, "lb_logits": d_lb, "hg_norm": d_hgn, "rel_bias": d_rel, "norm_ffn": d_ffn,
        "conv_b": d_conv_b, "norm_ple": d_ple, "final_norm": d_final,
    }
    return loss, dx, d_small, d_conv_w


def kernel(x, p, norm_mix, w_in, lb_logits, hg_norm, rel_bias, w_out, norm_ffn, w_up, conv_w, conv_b, w_down, norm_ple, w_ple_gate, w_ple_proj, final_norm, loss_target, m_norm_mix, m_w_in, m_lb_logits, m_hg_norm, m_rel_bias, m_w_out, m_norm_ffn, m_w_up, m_conv_w, m_conv_b, m_w_down, m_norm_ple, m_w_ple_gate, m_w_ple_proj, m_final_norm, v_norm_mix, v_w_in, v_lb_logits, v_hg_norm, v_rel_bias, v_w_out, v_norm_ffn, v_w_up, v_conv_w, v_conv_b, v_w_down, v_norm_ple, v_w_ple_gate, v_w_ple_proj, v_final_norm):
    given = dict(locals())
    mx, my, mc = _position()
    me = 4 * mx + 2 * my + mc
    big = {k: given[k][0] for k, _ in BIG}
    ex = _Exchange(big, (mx, my, mc))
    conv_w_full = ex.gather(conv_w[0])

    small = {
        "norm_mix": norm_mix, "lb_logits": lb_logits, "hg_norm": hg_norm, "rel_bias": rel_bias[0], "norm_ffn": norm_ffn,
        "conv_b": conv_b, "norm_ple": norm_ple, "final_norm": final_norm.reshape(1, -1),
    }
    loss, dx, d_small, d_conv_w = _local_step(x[0], p[0, 0], loss_target[0], small, conv_w_full, ex)

    packed = jnp.concatenate([_pack_small(d_small), _rows128(d_conv_w, CONV_W_FULL_ROWS), _rows128(loss[0:1, 0:1], 8)], axis=0)
    reduced = _all_reduce_small(packed)

    out = {}
    for gi in range(len(GROUPS)):
        for k, (o, r) in ex.finish(gi, reduced).items():
            g, d, nm, nv = _adam_big(big[k], given["m_" + k][0], given["v_" + k][0], o, r, "adam_" + k)
            out[k] = tuple(a[None] for a in (g, d, nm, nv))
    shapes = {k: given[k].shape for k, _ in SMALL}
    g_small, at = _unpack_small(reduced, shapes)
    g_conv_full = reduced[at : at + CONV_W_FULL_ROWS].reshape(3, 2 * D_FF)
    total_loss = reduced[at + CONV_W_FULL_ROWS, 0]
    cw = conv_w.shape[2]
    g_conv = lax.dynamic_slice_in_dim(g_conv_full, me * cw, cw, axis=1)

    def pack_with_conv(parts, conv_part):
        return jnp.concatenate([_pack_small(parts), _rows128(conv_part, CONV_W_SHARD_ROWS)], axis=0)

    d_pk, m_pk, v_pk = _adam_small(
        pack_with_conv({k: given[k] for k, _ in SMALL}, conv_w),
        pack_with_conv(g_small, g_conv),
        pack_with_conv({k: given["m_" + k] for k, _ in SMALL}, m_conv_w),
        pack_with_conv({k: given["v_" + k] for k, _ in SMALL}, v_conv_w),
    )
    for name, pk in (("d", d_pk), ("m", m_pk), ("v", v_pk)):
        parts, at = _unpack_small(pk, shapes)
        parts["conv_w"] = pk[at : at + CONV_W_SHARD_ROWS].reshape(-1)[: 3 * cw].reshape(conv_w.shape)
        for k, a in parts.items():
            out.setdefault(k, {})
            out[k][name] = a
    for k, _ in SMALL:
        out[k]["g"] = g_small[k]
    out["conv_w"]["g"] = g_conv.reshape(conv_w.shape)

    order = ["norm_mix", "w_in", "lb_logits", "hg_norm", "rel_bias", "w_out", "norm_ffn", "w_up", "conv_w", "conv_b", "w_down", "norm_ple", "w_ple_gate", "w_ple_proj", "final_norm"]

    def pick(k, what):
        return out[k][what] if isinstance(out[k], dict) else out[k][{"g": 0, "d": 1, "m": 2, "v": 3}[what]]

    return (total_loss, dx[None], *[pick(k, "g") for k in order], *[pick(k, "d") for k in order], *[pick(k, "m") for k in order], *[pick(k, "v") for k in order])
```

```python
import functools

import jax
import jax.numpy as jnp
from jax import lax
from jax.experimental import pallas as pl
from jax.experimental.pallas import tpu as pltpu

F32 = jnp.float32
BF16 = jnp.bfloat16

D_MODEL = 2048
CHUNK = 64
HG_HEADS = 8
HEAD_DIM = 128
HG_WIDTH = HG_HEADS * HEAD_DIM
ATT_HEADS = 8
ATT_WIDTH = ATT_HEADS * HEAD_DIM
LEFT_CHUNKS = 8
PAD = LEFT_CHUNKS * CHUNK
BAND = PAD + CHUNK
REL_CLIP = 128
N_REL = 2 * REL_CLIP + 1
N_REL_PAD = 384
D_FF = 5632
EPS = 1e-6
ATT_SCALE = HEAD_DIM ** -0.5
SUB = 32
HG_BLOCK = 8
Q_BLOCK = 4 * CHUNK
K_BLOCK = Q_BLOCK + PAD
DIAG = 1024

ADAM_LR = 0.001
ADAM_B1 = 0.9
ADAM_B2 = 0.999
ADAM_EPS = 1e-08
ADAM_WD = 0.01
ADAM_STEP = 10

N_DEV = 8
VMEM_LIMIT = 48 * 1024 * 1024
MESH = pl.DeviceIdType.MESH
ANY = pl.BlockSpec(memory_space=pl.ANY)
HIGHEST = lax.Precision.HIGHEST

NN = (((1,), (0,)), ((), ()))
NT = (((1,), (1,)), ((), ()))
TN = (((0,), (0,)), ((), ()))


def _params(*sem):
    return pltpu.CompilerParams(dimension_semantics=sem if sem else None, vmem_limit_bytes=VMEM_LIMIT)


def _pallas(body, n_in, dep, **kw):
    if dep is None:
        return pl.pallas_call(body, **kw)

    def body_after(*refs):
        body(*refs[:n_in], *refs[n_in + 1 :])

    call = pl.pallas_call(body_after, **dict(kw, in_specs=list(kw["in_specs"]) + [ANY]))
    return lambda *ops: call(*ops, dep)


def _dot(a, b, dims=NN):
    return lax.dot_general(a, b, dims, preferred_element_type=F32)


def _dot3(a, b, dims=NN):
    a_hi, b_hi = a.astype(BF16), b.astype(BF16)
    a_lo, b_lo = (a - a_hi.astype(F32)).astype(BF16), (b - b_hi.astype(F32)).astype(BF16)
    return _dot(a_hi, b_hi, dims) + (_dot(a_hi, b_lo, dims) + _dot(a_lo, b_hi, dims))


def _sigmoid(x):
    return 1.0 / (1.0 + jnp.exp(-x))


def _tile(n, prefs):
    for t in prefs:
        if n % t == 0:
            return t
    return n


def _matmul(a, b, mode, out_dtype, name, tm=1024, tn=1024, tk=None, resid=None, dep=None):
    if mode == "nn":
        (m, k), n = a.shape, b.shape[1]
    elif mode == "nt":
        (m, k), n = a.shape, b.shape[0]
    else:
        (k, m), n = a.shape, b.shape[1]
    tm = _tile(m, (tm, 512, 256, 128))
    tn = _tile(n, (tn, 1408, 512, 256, 128))
    tk = k if tk is None else _tile(k, (tk,))
    nk = k // tk
    dims = {"nn": NN, "nt": NT, "tn": TN}[mode]
    a_spec = pl.BlockSpec((tk, tm), lambda i, j, s: (s, i)) if mode == "tn" else pl.BlockSpec((tm, tk), lambda i, j, s: (i, s))
    b_spec = pl.BlockSpec((tn, tk), lambda i, j, s: (j, s)) if mode == "nt" else pl.BlockSpec((tk, tn), lambda i, j, s: (s, j))
    o_spec = pl.BlockSpec((tm, tn), lambda i, j, s: (i, j))
    has_res = resid is not None

    def body(*refs):
        a_ref, b_ref = refs[0], refs[1]
        o_ref = refs[2 + has_res]
        part = _dot(a_ref[...].astype(BF16), b_ref[...].astype(BF16), dims)

        def finish(acc):
            if has_res:
                acc = acc + refs[2][...]
            o_ref[...] = acc.astype(out_dtype)

        if nk == 1:
            finish(part)
        else:
            acc_ref = refs[-1]
            s = pl.program_id(2)

            @pl.when(s == 0)
            def _():
                acc_ref[...] = part

            @pl.when(s > 0)
            def _():
                acc_ref[...] += part

            @pl.when(s == nk - 1)
            def _():
                finish(acc_ref[...])

    return _pallas(
        body,
        2 + has_res,
        dep,
        name=name,
        grid=(m // tm, n // tn, nk),
        in_specs=[a_spec, b_spec] + ([o_spec] if has_res else []),
        out_specs=o_spec,
        out_shape=jax.ShapeDtypeStruct((m, n), out_dtype),
        scratch_shapes=[pltpu.VMEM((tm, tn), F32)] if nk > 1 else [],
        compiler_params=_params("parallel", "parallel", "arbitrary"),
    )(*([a, b] + ([resid] if has_res else [])))


def _rms_fwd(x, g, name, dep=None):
    t, d = x.shape
    tm = _tile(t, (256,))

    def body(x_ref, g_ref, a_ref, r_ref):
        xv = x_ref[...]
        r = lax.rsqrt(jnp.mean(xv * xv, axis=-1, keepdims=True) + EPS)
        a_ref[...] = (xv * r * g_ref[...]).astype(BF16)
        r_ref[...] = r

    row = pl.BlockSpec((tm, d), lambda i: (i, 0))
    return _pallas(
        body,
        2,
        dep,
        name=name,
        grid=(t // tm,),
        in_specs=[row, pl.BlockSpec((1, d), lambda i: (0, 0))],
        out_specs=[row, pl.BlockSpec((tm, 1), lambda i: (i, 0))],
        out_shape=[jax.ShapeDtypeStruct((t, d), BF16), jax.ShapeDtypeStruct((t, 1), F32)],
        compiler_params=_params("parallel"),
    )(x, g)


def _rms_bwd(da, x, r, g, resid, name, dep=None):
    t, d = x.shape
    tm = _tile(t, (256,))

    def body(da_ref, x_ref, r_ref, g_ref, res_ref, dx_ref, dg_ref):
        i = pl.program_id(0)
        rv = r_ref[...]
        n = x_ref[...] * rv
        dav = da_ref[...]
        dn = dav * g_ref[...]
        dx_ref[...] = rv * (dn - n * jnp.mean(dn * n, axis=-1, keepdims=True)) + res_ref[...]
        part = jnp.sum(dav * n, axis=0, keepdims=True)

        @pl.when(i == 0)
        def _():
            dg_ref[...] = part

        @pl.when(i > 0)
        def _():
            dg_ref[...] += part

    row = pl.BlockSpec((tm, d), lambda i: (i, 0))
    vec = pl.BlockSpec((1, d), lambda i: (0, 0))
    return _pallas(
        body,
        5,
        dep,
        name=name,
        grid=(t // tm,),
        in_specs=[row, row, pl.BlockSpec((tm, 1), lambda i: (i, 0)), vec, row],
        out_specs=[row, vec],
        out_shape=[jax.ShapeDtypeStruct((t, d), F32), jax.ShapeDtypeStruct((1, d), F32)],
        compiler_params=_params("arbitrary"),
    )(da, x, r, g, resid)


def _tri(n, upper):
    r = lax.broadcasted_iota(jnp.int32, (n, n), 0)
    c = lax.broadcasted_iota(jnp.int32, (n, n), 1)
    return jnp.where((c >= r) if upper else (c <= r), 1.0, 0.0).astype(F32)


def _hgrn_gates(q, fp, lbl):
    l0, l1 = lbl[0:1, :], lbl[1:2, :]
    mx = jnp.maximum(l0, l1)
    e0, e1 = jnp.exp(l0 - mx), jnp.exp(l1 - mx)
    lb = e0 / (e0 + e1)
    sig = _sigmoid(fp)
    f = lb + (1.0 - lb) * sig
    kk = (1.0 - lb) * _sigmoid(-fp)
    sq = _sigmoid(q)
    b = jnp.dot(_tri(CHUNK, False), jnp.log(f), precision=HIGHEST, preferred_element_type=F32)
    return lb, sig, f, kk, sq, q * sq, b


def _heads(x):
    return [x[:, j * HEAD_DIM : (j + 1) * HEAD_DIM] for j in range(x.shape[1] // HEAD_DIM)]


def _wide(parts):
    return jnp.concatenate(parts, axis=1)


def _intra_blocks(b):
    out = []
    for lo in range(0, CHUNK, SUB):
        hi = lo + SUB
        br = b[lo + SUB // 2 : lo + SUB // 2 + 1, :]
        row = lax.broadcasted_iota(jnp.int32, (SUB, hi), 0) + lo
        col = lax.broadcasted_iota(jnp.int32, (SUB, hi), 1)
        out.append((lo, hi, jnp.exp(b[lo:hi] - br), jnp.exp(br - b[:hi]), col <= row))
    return out


def _hgrn_fwd(proj, lb_logits, hg_norm):
    t = proj.shape[0]
    nc = t // CHUNK

    def body(q_ref, f_ref, i_ref, g_ref, lbl_ref, hgn_ref, y_ref, o_ref, st_ref, s_scr):
        c = pl.program_id(1)

        @pl.when(c == 0)
        def _():
            s_scr[...] = jnp.zeros_like(s_scr)

        hs = range(HG_BLOCK)
        sts = [s_scr[j] for j in hs]
        _, _, _, kk, _, qf, b = _hgrn_gates(q_ref[...], f_ref[...], lbl_ref[...])
        vb = _heads(i_ref[...].astype(BF16))
        bl = b[CHUNK - 1 : CHUNK, :]
        qe = _heads((qf * jnp.exp(b)).astype(BF16))
        kd = _heads((kk * jnp.exp(bl - b)).astype(BF16))
        decay = _heads(jnp.exp(bl))
        o = [_dot(qe[j], sts[j].astype(BF16), NT) for j in hs]
        parts = [[] for _ in hs]
        for lo, hi, ea, eb, mask in _intra_blocks(b):
            a, bk = _heads((qf[lo:hi] * ea).astype(BF16)), _heads((kk[:hi] * eb).astype(BF16))
            p = [jnp.where(mask, _dot(a[j], bk[j], NT), 0.0).astype(BF16) for j in hs]
            for j in hs:
                parts[j].append(_dot(p[j], vb[j][:hi]))
        o = [o[j] + jnp.concatenate(parts[j], axis=0) for j in hs]
        new = [sts[j] * decay[j] + _dot(vb[j], kd[j], TN) for j in hs]
        hgn = hgn_ref[...]
        on = [o[j] * lax.rsqrt(jnp.mean(o[j] * o[j], axis=-1, keepdims=True) + EPS) * hgn for j in hs]
        gg = g_ref[...]
        for j in hs:
            st_ref[j] = sts[j]
            s_scr[j] = new[j]
        o_ref[...] = _wide(o)
        y_ref[...] = (_wide(on) * (gg * _sigmoid(gg))).astype(BF16)

    wide = HG_BLOCK * HEAD_DIM
    groups = HG_HEADS // HG_BLOCK

    def col(k):
        return pl.BlockSpec((CHUNK, wide), lambda g, c: (c, k * groups + g))

    out = pl.BlockSpec((CHUNK, wide), lambda g, c: (c, g))
    return pl.pallas_call(
        body,
        name="hgrn_fwd",
        grid=(groups, nc),
        in_specs=[col(0), col(1), col(2), col(3), pl.BlockSpec((2, wide), lambda g, c: (0, g)), pl.BlockSpec((1, HEAD_DIM), lambda g, c: (0, 0))],
        out_specs=[out, out, pl.BlockSpec((HG_BLOCK, None, HEAD_DIM, HEAD_DIM), lambda g, c: (g, c, 0, 0))],
        out_shape=[
            jax.ShapeDtypeStruct((t, HG_WIDTH), BF16),
            jax.ShapeDtypeStruct((t, HG_WIDTH), F32),
            jax.ShapeDtypeStruct((HG_HEADS, nc, HEAD_DIM, HEAD_DIM), F32),
        ],
        scratch_shapes=[pltpu.VMEM((HG_BLOCK, HEAD_DIM, HEAD_DIM), F32)],
        compiler_params=_params("arbitrary", "arbitrary"),
    )(proj, proj, proj, proj, lb_logits, hg_norm)


def _hgrn_bwd(proj, lb_logits, hg_norm, o_hg, dycat, states, dep=None):
    t = proj.shape[0]
    nc = t // CHUNK

    def body(q_ref, f_ref, i_ref, g_ref, lbl_ref, hgn_ref, o_ref, dy_ref, st_ref, dp_ref, dlbl_ref, dhgn_ref, dst_scr, dlb_scr):
        h = pl.program_id(0)
        c = pl.program_id(1)

        @pl.when(c == 0)
        def _():
            dst_scr[...] = jnp.zeros_like(dst_scr)
            dlb_scr[...] = jnp.zeros_like(dlb_scr)

        @pl.when((c == 0) & (h == 0))
        def _():
            dhgn_ref[...] = jnp.zeros_like(dhgn_ref)

        hs = range(HG_BLOCK)
        hgn = _wide([hgn_ref[...]] * HG_BLOCK)
        q, fp, gg, vi = q_ref[...], f_ref[...], g_ref[...], i_ref[...]
        lb, sig, f, kk, sq, qf, b = _hgrn_gates(q, fp, lbl_ref[...])
        o, dy = o_ref[...], dy_ref[...]
        sg = _sigmoid(gg)
        n = _wide([oh * lax.rsqrt(jnp.mean(oh * oh, axis=-1, keepdims=True) + EPS) for oh in _heads(o)])
        don = dy * (gg * sg)
        dgg = dy * (n * hgn) * (sg * (1.0 + gg * (1.0 - sg)))
        d_hgn = sum(_heads(jnp.sum(don * n, axis=0, keepdims=True)))
        dn = don * hgn
        do = _wide(
            [
                lax.rsqrt(jnp.mean(oh * oh, axis=-1, keepdims=True) + EPS) * (dnh - nh * jnp.mean(dnh * nh, axis=-1, keepdims=True))
                for oh, dnh, nh in zip(_heads(o), _heads(dn), _heads(n))
            ]
        )
        sts = [st_ref[j] for j in hs]
        dstn = [dst_scr[j] for j in hs]
        bl = b[CHUNK - 1 : CHUNK, :]
        e_b, e_bl, e_l = jnp.exp(b), jnp.exp(bl - b), jnp.exp(bl)
        doh, vih = _heads(do), _heads(vi)
        dobh = _heads(do.astype(BF16))
        dq_acc = _wide([_dot3(doh[j], sts[j]) for j in hs]) * e_b
        dk_inter = _wide([_dot3(vih[j], dstn[j]) for j in hs]) * e_bl
        dk_acc = dk_inter
        kd = _heads((kk * e_bl).astype(BF16))
        dv_acc = _wide([_dot(kd[j], dstn[j].astype(BF16), NT) for j in hs])
        qe, decay = _heads((qf * e_b).astype(BF16)), _heads(e_l)
        dst_new = [dstn[j] * decay[j] + _dot(dobh[j], qe[j], TN) for j in hs]
        db_last = e_l * _wide([jnp.sum(sts[j] * dstn[j], axis=0, keepdims=True) for j in hs]) + jnp.sum(kk * dk_inter, axis=0, keepdims=True)
        dq_parts = []
        for lo, hi, ea, eb, mask in _intra_blocks(b):
            a, bk = qf[lo:hi] * ea, kk[:hi] * eb
            ah, bkh = _heads(a), _heads(bk)
            abh, bkbh = _heads(a.astype(BF16)), _heads(bk.astype(BF16))
            p = [jnp.where(mask, _dot(abh[j], bkbh[j], NT), 0.0).astype(BF16) for j in hs]
            dp = [jnp.where(mask, _dot3(doh[j][lo:hi], vih[j][:hi], NT), 0.0) for j in hs]
            dq_parts.append(_wide([_dot3(dp[j], bkh[j]) for j in hs]) * ea)
            dki = _wide([_dot3(dp[j], ah[j], TN) for j in hs]) * eb
            dvi = _wide([_dot(p[j], dobh[j][lo:hi], TN) for j in hs])
            if hi < CHUNK:
                zeros = jnp.zeros((CHUNK - hi, HG_BLOCK * HEAD_DIM), F32)
                dki = jnp.concatenate([dki, zeros], axis=0)
                dvi = jnp.concatenate([dvi, zeros], axis=0)
            dk_acc = dk_acc + dki
            dv_acc = dv_acc + dvi
        dq_acc = dq_acc + jnp.concatenate(dq_parts, axis=0)
        rows = lax.broadcasted_iota(jnp.int32, dq_acc.shape, 0)
        db = qf * dq_acc - kk * dk_acc + jnp.where(rows == CHUNK - 1, db_last, 0.0)
        dlf = jnp.dot(_tri(CHUNK, True), db, precision=HIGHEST, preferred_element_type=F32)
        dfk = dlf / f - dk_acc
        dp_ref[0] = (dq_acc * (sq * (1.0 + q * (1.0 - sq)))).astype(BF16)
        dp_ref[1] = ((1.0 - lb) * dfk * sig * (1.0 - sig)).astype(BF16)
        dp_ref[2] = dv_acc.astype(BF16)
        dp_ref[3] = dgg.astype(BF16)
        dlb_scr[...] += jnp.sum(dfk * (1.0 - sig), axis=0, keepdims=True)
        dhgn_ref[...] += d_hgn
        for j in hs:
            dst_scr[j] = dst_new[j]

        @pl.when(c == nc - 1)
        def _():
            dl0 = dlb_scr[...] * lb * (1.0 - lb)
            dlbl_ref[0:1, :] = dl0
            dlbl_ref[1:2, :] = -dl0

    wide = HG_BLOCK * HEAD_DIM
    groups = HG_HEADS // HG_BLOCK

    def col(k):
        return pl.BlockSpec((CHUNK, wide), lambda g, c: (nc - 1 - c, k * groups + g))

    blk = pl.BlockSpec((CHUNK, wide), lambda g, c: (nc - 1 - c, g))
    return _pallas(
        body,
        9,
        dep,
        name="hgrn_bwd",
        grid=(groups, nc),
        in_specs=[
            col(0), col(1), col(2), col(3),
            pl.BlockSpec((2, wide), lambda g, c: (0, g)),
            pl.BlockSpec((1, HEAD_DIM), lambda g, c: (0, 0)),
            blk, blk,
            pl.BlockSpec((HG_BLOCK, None, HEAD_DIM, HEAD_DIM), lambda g, c: (g, nc - 1 - c, 0, 0)),
        ],
        out_specs=[
            pl.BlockSpec((4, CHUNK, wide), lambda g, c: (0, nc - 1 - c, g)),
            pl.BlockSpec((2, wide), lambda g, c: (0, g)),
            pl.BlockSpec((1, HEAD_DIM), lambda g, c: (0, 0)),
        ],
        out_shape=[
            jax.ShapeDtypeStruct((4, t, HG_WIDTH), BF16),
            jax.ShapeDtypeStruct((2, HG_WIDTH), F32),
            jax.ShapeDtypeStruct((1, HEAD_DIM), F32),
        ],
        scratch_shapes=[pltpu.VMEM((HG_BLOCK, HEAD_DIM, HEAD_DIM), F32), pltpu.VMEM((1, wide), F32)],
        compiler_params=_params("arbitrary", "arbitrary"),
    )(proj, proj, proj, proj, lb_logits, hg_norm, o_hg, dycat, states)


def _diagonal_slots(shift):
    i = lax.broadcasted_iota(jnp.int32, (N_REL_PAD, DIAG), 0)
    u = lax.broadcasted_iota(jnp.int32, (N_REL_PAD, DIAG), 1)
    offset = u - shift if shift else jnp.where(u < K_BLOCK, u, u - DIAG)
    return jnp.where(jnp.clip(PAD - offset, -REL_CLIP, REL_CLIP) + REL_CLIP == i, 1.0, 0.0).astype(BF16)


def _split3(x):
    hi = x.astype(BF16)
    mid = (x - hi.astype(F32)).astype(BF16)
    return hi, mid, (x - hi.astype(F32) - mid.astype(F32)).astype(BF16)


def _bias_table(rel_bias):
    def body(rb_ref, o_ref, diag):
        h = pl.program_id(0)

        @pl.when(h == 0)
        def _():
            hi, mid, lo = _split3(rb_ref[...])
            slots = _diagonal_slots(0)
            diag[...] = _dot(hi, slots) + (_dot(mid, slots) + _dot(lo, slots))

        rows = jnp.broadcast_to(diag[pl.ds(h, 1), :], (Q_BLOCK, DIAG))
        o_ref[...] = pltpu.roll(rows, 0, 1, stride=1, stride_axis=0)[:, :K_BLOCK]

    return pl.pallas_call(
        body,
        name="bias_table",
        grid=(ATT_HEADS,),
        in_specs=[pl.BlockSpec((ATT_HEADS, N_REL_PAD), lambda h: (0, 0))],
        out_specs=pl.BlockSpec((None, Q_BLOCK, K_BLOCK), lambda h: (h, 0, 0)),
        out_shape=jax.ShapeDtypeStruct((ATT_HEADS, Q_BLOCK, K_BLOCK), F32),
        scratch_shapes=[pltpu.VMEM((ATT_HEADS, DIAG), F32)],
        compiler_params=_params("arbitrary"),
    )(rel_bias)


def _att_probs(q_ref, kpad, bias_ref, blk):
    qs = (q_ref[...] * ATT_SCALE).astype(BF16)
    start = pl.multiple_of(blk * Q_BLOCK, Q_BLOCK)
    kb = kpad[pl.ds(start, K_BLOCK), :]
    s = _dot(qs, kb, NT) + bias_ref[...]
    row = lax.broadcasted_iota(jnp.int32, (Q_BLOCK, K_BLOCK), 0)
    col = lax.broadcasted_iota(jnp.int32, (Q_BLOCK, K_BLOCK), 1)
    first = row - (row & (CHUNK - 1))
    valid = (col >= first) & (col < first + BAND) & (col + (blk * Q_BLOCK - PAD) >= 0)
    s = jnp.where(valid, s, jnp.finfo(F32).min)
    e = jnp.exp(s - jnp.max(s, axis=-1, keepdims=True))
    return qs, kb, start, e / jnp.sum(e, axis=-1, keepdims=True)


def _fill_padded(dst, src):
    dst[0:PAD, :] = jnp.zeros((PAD, HEAD_DIM), BF16)
    dst[PAD:, :] = src[...].astype(BF16)


def _att_fwd(proj, bias, dep=None):
    t = proj.shape[0]
    nb = t // Q_BLOCK

    def body(q_ref, k_ref, v_ref, bias_ref, y_ref, kpad, vpad):
        c = pl.program_id(1)

        @pl.when(c == 0)
        def _():
            _fill_padded(kpad, k_ref)
            _fill_padded(vpad, v_ref)

        _, _, start, p = _att_probs(q_ref, kpad, bias_ref, c)
        y_ref[...] = _dot(p.astype(BF16), vpad[pl.ds(start, K_BLOCK), :]).astype(BF16)

    base = 4 * HG_HEADS
    return _pallas(
        body,
        4,
        dep,
        name="att_fwd",
        grid=(ATT_HEADS, nb),
        in_specs=[
            pl.BlockSpec((Q_BLOCK, HEAD_DIM), lambda h, c: (c, base + h)),
            pl.BlockSpec((t, HEAD_DIM), lambda h, c: (0, base + ATT_HEADS + h)),
            pl.BlockSpec((t, HEAD_DIM), lambda h, c: (0, base + 2 * ATT_HEADS + h)),
            pl.BlockSpec((None, Q_BLOCK, K_BLOCK), lambda h, c: (h, 0, 0)),
        ],
        out_specs=pl.BlockSpec((Q_BLOCK, HEAD_DIM), lambda h, c: (c, h)),
        out_shape=jax.ShapeDtypeStruct((t, ATT_WIDTH), BF16),
        scratch_shapes=[pltpu.VMEM((t + PAD, HEAD_DIM), BF16), pltpu.VMEM((t + PAD, HEAD_DIM), BF16)],
        compiler_params=_params("arbitrary", "arbitrary"),
    )(proj, proj, proj, bias)


def _att_bwd(proj, bias, dycat, dep=None):
    t = proj.shape[0]
    nb = t // Q_BLOCK

    def body(q_ref, k_ref, v_ref, bias_ref, dy_ref, dq_ref, dk_ref, dv_ref, g_ref, kpad, vpad, dkacc, dvacc):
        c = pl.program_id(1)

        @pl.when(c == 0)
        def _():
            _fill_padded(kpad, k_ref)
            _fill_padded(vpad, v_ref)
            dkacc[...] = jnp.zeros_like(dkacc)
            dvacc[...] = jnp.zeros_like(dvacc)
            g_ref[...] = jnp.zeros_like(g_ref)

        qs, kb, start, p = _att_probs(q_ref, kpad, bias_ref, c)
        band = pl.ds(start, K_BLOCK)
        dyb = dy_ref[...].astype(BF16)
        dvacc[band, :] += _dot(p.astype(BF16), dyb, TN)
        dp = _dot(dyb, vpad[band, :], NT)
        ds = p * (dp - jnp.sum(dp * p, axis=-1, keepdims=True))
        g_ref[...] += ds
        dsb = ds.astype(BF16)
        dq_ref[...] = (_dot(dsb, kb) * ATT_SCALE).astype(BF16)
        dkacc[band, :] += _dot(dsb, qs, TN)

        @pl.when(c == nb - 1)
        def _():
            dk_ref[...] = dkacc[PAD:, :].astype(BF16)
            dv_ref[...] = dvacc[PAD:, :].astype(BF16)

    base = 4 * HG_HEADS
    whole = pl.BlockSpec((t, HEAD_DIM), lambda h, c: (0, h))
    return _pallas(
        body,
        5,
        dep,
        name="att_bwd",
        grid=(ATT_HEADS, nb),
        in_specs=[
            pl.BlockSpec((Q_BLOCK, HEAD_DIM), lambda h, c: (c, base + h)),
            pl.BlockSpec((t, HEAD_DIM), lambda h, c: (0, base + ATT_HEADS + h)),
            pl.BlockSpec((t, HEAD_DIM), lambda h, c: (0, base + 2 * ATT_HEADS + h)),
            pl.BlockSpec((None, Q_BLOCK, K_BLOCK), lambda h, c: (h, 0, 0)),
            pl.BlockSpec((Q_BLOCK, HEAD_DIM), lambda h, c: (c, HG_HEADS + h)),
        ],
        out_specs=[pl.BlockSpec((Q_BLOCK, HEAD_DIM), lambda h, c: (c, h)), whole, whole, pl.BlockSpec((None, Q_BLOCK, K_BLOCK), lambda h, c: (h, 0, 0))],
        out_shape=[
            jax.ShapeDtypeStruct((t, ATT_WIDTH), BF16),
            jax.ShapeDtypeStruct((t, ATT_WIDTH), BF16),
            jax.ShapeDtypeStruct((t, ATT_WIDTH), BF16),
            jax.ShapeDtypeStruct((ATT_HEADS, Q_BLOCK, K_BLOCK), F32),
        ],
        scratch_shapes=[
            pltpu.VMEM((t + PAD, HEAD_DIM), BF16),
            pltpu.VMEM((t + PAD, HEAD_DIM), BF16),
            pltpu.VMEM((t + PAD, HEAD_DIM), F32),
            pltpu.VMEM((t + PAD, HEAD_DIM), F32),
        ],
        compiler_params=_params("arbitrary", "arbitrary"),
    )(proj, proj, proj, bias, dycat)


def _rel_bias_grad(gsum):
    def body(g_ref, o_ref):
        r = lax.broadcasted_iota(jnp.int32, (Q_BLOCK, Q_BLOCK), 0)
        c = lax.broadcasted_iota(jnp.int32, (Q_BLOCK, Q_BLOCK), 1)
        flip = jnp.where(r + c == Q_BLOCK - 1, 1.0, 0.0).astype(BF16)
        sums = []
        for h in range(ATT_HEADS):
            hi, mid, lo = _split3(g_ref[h])
            rev = _dot(flip, hi) + (_dot(flip, mid) + _dot(flip, lo))
            wide = jnp.concatenate([rev, jnp.zeros((Q_BLOCK, DIAG - K_BLOCK), F32)], axis=1)
            sums.append(jnp.sum(pltpu.roll(wide, 0, 1, stride=1, stride_axis=0), axis=0, keepdims=True))
        hi, mid, lo = _split3(jnp.concatenate(sums, axis=0))
        slots = _diagonal_slots(Q_BLOCK - 1)
        o_ref[...] = _dot(hi, slots, NT) + (_dot(mid, slots, NT) + _dot(lo, slots, NT))

    return pl.pallas_call(
        body,
        name="rel_bias_grad",
        out_shape=jax.ShapeDtypeStruct((ATT_HEADS, N_REL_PAD), F32),
        compiler_params=_params(),
    )(gsum)


HALO = 16


FF_TILE = 1408
FF_TILES = D_FF // FF_TILE


def _interleave_cols(a):
    lead = a.shape[:-1]
    return jnp.swapaxes(a.reshape(*lead, 2, FF_TILES, FF_TILE), -3, -2).reshape(*lead, 2 * D_FF)


def _deinterleave_cols(a):
    lead = a.shape[:-1]
    return jnp.swapaxes(a.reshape(*lead, FF_TILES, 2, FF_TILE), -3, -2).reshape(*lead, 2 * D_FF)


def _ffn_specs(t, tm):
    wide = 2 * FF_TILE
    tile = pl.BlockSpec((tm, wide), lambda j, i: (i, j))
    before = pl.BlockSpec((HALO, wide), lambda j, i: (jnp.maximum(i * (tm // HALO) - 1, 0), j))
    after = pl.BlockSpec((HALO, wide), lambda j, i: (jnp.minimum((i + 1) * (tm // HALO), t // HALO - 1), j))
    vec = lambda rows: pl.BlockSpec((rows, wide), lambda j, i: (0, j))
    return tile, before, after, vec


def _conv(x, w, b, rows):
    taps = [pltpu.roll(x, 2, 0)[HALO : HALO + rows], pltpu.roll(x, 1, 0)[HALO : HALO + rows], x[HALO : HALO + rows]]
    return b + w[0:1] * taps[0] + w[1:2] * taps[1] + w[2:3] * taps[2], taps


def _ffn_act_fwd(u, conv_w, conv_b):
    t = u.shape[0]
    tm = _tile(t, (128,))
    tile, before, _, vec = _ffn_specs(t, tm)

    def body(u_ref, h_ref, w_ref, b_ref, z_ref):
        first = pl.program_id(1) == 0
        x = jnp.concatenate([jnp.where(first, 0.0, h_ref[...].astype(F32)), u_ref[...].astype(F32)], axis=0)
        c, _ = _conv(x, w_ref[...], b_ref[...], tm)
        gate, val = c[:, :FF_TILE], c[:, FF_TILE:]
        z_ref[...] = (gate * _sigmoid(gate) * val).astype(BF16)

    return pl.pallas_call(
        body,
        name="ffn_act_fwd",
        grid=(FF_TILES, t // tm),
        in_specs=[tile, before, vec(3), vec(1)],
        out_specs=pl.BlockSpec((tm, FF_TILE), lambda j, i: (i, j)),
        out_shape=jax.ShapeDtypeStruct((t, D_FF), BF16),
        compiler_params=_params("parallel", "parallel"),
    )(u, u, conv_w, conv_b)


def _ffn_act_bwd(u, dz, conv_w, conv_b, dep=None):
    t = u.shape[0]
    tm = _tile(t, (128,))
    nt = t // tm
    ext = tm + HALO
    tile, before, after, vec = _ffn_specs(t, tm)

    def body(u_ref, ub_ref, ua_ref, w_ref, b_ref, dz_ref, dza_ref, du_ref, dw_ref, db_ref):
        i = pl.program_id(1)
        first, last = i == 0, i == nt - 1
        parts = [jnp.where(first, 0.0, ub_ref[...].astype(F32)), u_ref[...].astype(F32), jnp.where(last, 0.0, ua_ref[...].astype(F32))]
        w = w_ref[...]
        c, taps = _conv(jnp.concatenate(parts, axis=0), w, b_ref[...], ext)
        gate, val = c[:, :FF_TILE], c[:, FF_TILE:]
        dz = jnp.concatenate([dz_ref[...].astype(F32), jnp.where(last, 0.0, dza_ref[...].astype(F32))], axis=0)
        sg = _sigmoid(gate)
        d = jnp.concatenate([dz * val * (sg * (1.0 + gate * (1.0 - sg))), dz * (gate * sg)], axis=1)
        du = w[2:3] * d[:tm] + w[1:2] * pltpu.roll(d, ext - 1, 0)[:tm] + w[0:1] * pltpu.roll(d, ext - 2, 0)[:tm]
        du_ref[...] = du.astype(BF16)

        @pl.when(first)
        def _():
            dw_ref[...] = jnp.zeros_like(dw_ref)
            db_ref[...] = jnp.zeros_like(db_ref)

        for k, tap in enumerate(taps):
            dw_ref[k : k + 1, :] += jnp.sum(d[:tm] * tap[:tm], axis=0, keepdims=True)
        db_ref[...] += jnp.sum(d[:tm], axis=0, keepdims=True)

    narrow = lambda rows, index: pl.BlockSpec((rows, FF_TILE), index)
    return _pallas(
        body,
        7,
        dep,
        name="ffn_act_bwd",
        grid=(FF_TILES, nt),
        in_specs=[
            tile, before, after, vec(3), vec(1),
            narrow(tm, lambda j, i: (i, j)),
            narrow(HALO, lambda j, i: (jnp.minimum((i + 1) * (tm // HALO), t // HALO - 1), j)),
        ],
        out_specs=[tile, vec(3), vec(1)],
        out_shape=[
            jax.ShapeDtypeStruct((t, 2 * D_FF), BF16),
            jax.ShapeDtypeStruct((3, 2 * D_FF), F32),
            jax.ShapeDtypeStruct((1, 2 * D_FF), F32),
        ],
        compiler_params=_params("parallel", "arbitrary"),
    )(u, u, u, conv_w, conv_b, dz, dz)


def _ple_loss(gpre, pp, h2, final_norm, target):
    t, d = h2.shape
    tm = _tile(t, (256,))

    def body(gp_ref, pp_ref, h_ref, g_ref, tg_ref, dh_ref, dgp_ref, dpp_ref, dg_ref, loss_ref):
        i = pl.program_id(0)
        gate = _sigmoid(gp_ref[...])
        ppv = pp_ref[...]
        h3 = h_ref[...] + gate * ppv
        r = lax.rsqrt(jnp.mean(h3 * h3, axis=-1, keepdims=True) + EPS)
        n = h3 * r
        g = g_ref[...]
        err = n * g - tg_ref[...]
        loss = 0.5 * jnp.sum(jnp.mean(err * err, axis=-1, keepdims=True))
        dy = err * (1.0 / d)
        dn = dy * g
        dh = r * (dn - n * jnp.mean(dn * n, axis=-1, keepdims=True))
        dh_ref[...] = dh
        dgp_ref[...] = (dh * ppv * gate * (1.0 - gate)).astype(BF16)
        dpp_ref[...] = (dh * gate).astype(BF16)
        dg = jnp.sum(dy * n, axis=0, keepdims=True)

        @pl.when(i == 0)
        def _():
            dg_ref[...] = dg
            loss_ref[...] = jnp.full(loss_ref.shape, loss, F32)

        @pl.when(i > 0)
        def _():
            dg_ref[...] += dg
            loss_ref[...] += loss

    row = pl.BlockSpec((tm, d), lambda i: (i, 0))
    vec = pl.BlockSpec((1, d), lambda i: (0, 0))
    return pl.pallas_call(
        body,
        name="ple_loss",
        grid=(t // tm,),
        in_specs=[row, row, row, vec, row],
        out_specs=[row, row, row, vec, pl.BlockSpec((8, 128), lambda i: (0, 0))],
        out_shape=[
            jax.ShapeDtypeStruct((t, d), F32),
            jax.ShapeDtypeStruct((t, d), BF16),
            jax.ShapeDtypeStruct((t, d), BF16),
            jax.ShapeDtypeStruct((1, d), F32),
            jax.ShapeDtypeStruct((8, 128), F32),
        ],
        compiler_params=_params("arbitrary"),
    )(gpre, pp, h2, final_norm, target)


def _adamw(w, g, m, v):
    m = ADAM_B1 * m + (1.0 - ADAM_B1) * g
    v = ADAM_B2 * v + (1.0 - ADAM_B2) * (g * g)
    m_hat = m / (1.0 - ADAM_B1 ** ADAM_STEP)
    v_hat = v / (1.0 - ADAM_B2 ** ADAM_STEP)
    return -ADAM_LR * (m_hat / (jnp.sqrt(v_hat) + ADAM_EPS) + ADAM_WD * w), m, v


def _adam_big(w, m, v, own, recv, name):
    r, c = w.shape
    tr = _tile(r, (256, 176))

    def body(w_ref, m_ref, v_ref, own_ref, recv_ref, g_ref, d_ref, nm_ref, nv_ref):
        g = own_ref[...]
        for k in range(3):
            g = g + recv_ref[k].astype(F32)
        g_ref[...] = g
        d_ref[...], nm_ref[...], nv_ref[...] = _adamw(w_ref[...], g, m_ref[...], v_ref[...])

    blk = pl.BlockSpec((tr, c), lambda i: (i, 0))
    return pl.pallas_call(
        body,
        name=name,
        grid=(r // tr,),
        in_specs=[blk, blk, blk, blk, pl.BlockSpec((3, tr, c), lambda i: (0, i, 0))],
        out_specs=[blk] * 4,
        out_shape=[jax.ShapeDtypeStruct((r, c), F32)] * 4,
        compiler_params=_params("parallel"),
    )(w, m, v, own, recv)


def _adam_small(w, g, m, v):
    def body(w_ref, g_ref, m_ref, v_ref, d_ref, nm_ref, nv_ref):
        d_ref[...], nm_ref[...], nv_ref[...] = _adamw(w_ref[...], g_ref[...], m_ref[...], v_ref[...])

    return pl.pallas_call(body, name="adam_small", out_shape=[jax.ShapeDtypeStruct(w.shape, F32)] * 3, compiler_params=_params())(w, g, m, v)


def _cast_bf16(w, name):
    r, c = w.shape
    tr = _tile(r, (256, 176))

    def body(w_ref, o_ref):
        o_ref[...] = w_ref[...].astype(BF16)

    blk = pl.BlockSpec((tr, c), lambda i: (i, 0))
    return pl.pallas_call(
        body, name=name, grid=(r // tr,), in_specs=[blk], out_specs=blk, out_shape=jax.ShapeDtypeStruct((r, c), BF16), compiler_params=_params("parallel")
    )(w)


def _position():
    return lax.axis_index("x"), lax.axis_index("y"), lax.axis_index("c")


def _other_chips(x, y):
    return [(1 - x, y), (x, 1 - y), (1 - x, 1 - y)]


def _block_index(dev, interleaved):
    x, y, c = dev
    return 4 * y + 2 * c + x if interleaved else 4 * x + 2 * y + c


def _shard_of(ref, axis, size, dev, interleaved=False):
    start = pl.multiple_of(_block_index(dev, interleaved) * size, 128 if axis == 1 else 16)
    return ref.at[:, pl.ds(start, size)] if axis == 1 else ref.at[pl.ds(start, size), :]


def _all_gather(shards, axes, interleaved):
    n = len(shards)

    def body(*refs):
        ins, outs = refs[:n], refs[n : 2 * n]
        send_sems, recv_sems, local_sems = refs[2 * n :]
        x, y, c = _position()
        me, sibling = (x, y, c), (x, y, 1 - c)
        chips = _other_chips(x, y)
        firsts, passed, locals_ = [], [], []
        for w in range(n):
            size = shards[w].shape[axes[w]]
            slot = functools.partial(_shard_of, outs[w], axes[w], size, interleaved=interleaved[w])

            def copy(k, block, to, src=None, w=w, slot=slot):
                return pltpu.make_async_remote_copy(
                    src_ref=slot(block) if src is None else src,
                    dst_ref=slot(block),
                    send_sem=send_sems.at[7 * w + k],
                    recv_sem=recv_sems.at[7 * w + k],
                    device_id=to,
                    device_id_type=MESH,
                )

            mine = pltpu.make_async_copy(ins[w], slot(me), local_sems.at[w])
            mine.start()
            locals_.append(mine)
            first = [copy(0, me, sibling, src=ins[w])] + [copy(1 + j, me, (*chip, c), src=ins[w]) for j, chip in enumerate(chips)]
            for cp in first:
                cp.start()
            firsts.append((first, copy))
        for w in range(n):
            first, copy = firsts[w]
            fwd = [copy(4 + j, (*chip, c), sibling) for j, chip in enumerate(chips)]
            for j, chip in enumerate(chips):
                copy(1 + j, (*chip, c), me).wait_recv()
                fwd[j].start()
            passed.append(fwd)
        for w in range(n):
            first, copy = firsts[w]
            copy(0, sibling, me).wait_recv()
            for j, chip in enumerate(chips):
                copy(4 + j, (*chip, 1 - c), me).wait_recv()
            for cp in first + passed[w]:
                cp.wait_send()
            locals_[w].wait()

    def full(s, ax):
        shape = list(s.shape)
        shape[ax] *= N_DEV
        return jax.ShapeDtypeStruct(tuple(shape), s.dtype)

    return pl.pallas_call(
        body,
        name="all_gather_weights",
        in_specs=[ANY] * n,
        out_specs=[ANY] * n,
        out_shape=[full(s, ax) for s, ax in zip(shards, axes)],
        scratch_shapes=[pltpu.SemaphoreType.DMA((7 * n,)), pltpu.SemaphoreType.DMA((7 * n,)), pltpu.SemaphoreType.DMA((n,))],
    )(*shards)


def _add_blocks(ids, grad, landed, axis, size, targets, out_dtype, name):
    rows = size if axis == 0 else grad.shape[0]
    cols = size if axis == 1 else grad.shape[1]
    tr = _tile(rows, (256, 176))
    nr = rows // tr
    nt = len(targets)

    def body(ids_ref, g_ref, l_ref, o_ref):
        o_ref[...] = (g_ref[...] + l_ref[...]).astype(out_dtype)

    if axis == 1:
        g_spec = pl.BlockSpec((tr, cols), lambda k, i, ids: (i, ids[targets[0] + k]))
    else:
        g_spec = pl.BlockSpec((tr, cols), lambda k, i, ids: (ids[targets[0] + k] * nr + i, 0))
    return pl.pallas_call(
        body,
        name=name,
        grid_spec=pltpu.PrefetchScalarGridSpec(
            num_scalar_prefetch=1,
            grid=(nt, nr),
            in_specs=[g_spec, pl.BlockSpec((None, tr, cols), lambda k, i, ids: (ids[4 + targets[0] + k], i, 0))],
            out_specs=pl.BlockSpec((None, tr, cols), lambda k, i, ids: (k, i, 0)),
        ),
        out_shape=jax.ShapeDtypeStruct((nt, rows, cols), out_dtype),
        compiler_params=_params("parallel", "parallel"),
    )(ids, grad, landed)


def _all_reduce_small(vec):
    rows = vec.shape[0]

    def body(v_ref, o_ref, land, send_sems, recv_sems):
        x, y, c = _position()
        mine = 4 * x + 2 * y + c
        copies = []
        for mask in range(1, N_DEV):
            peer = (1 - x if mask & 4 else x, 1 - y if mask & 2 else y, 1 - c if mask & 1 else c)
            copies.append(
                pltpu.make_async_remote_copy(
                    src_ref=v_ref, dst_ref=land.at[mine], send_sem=send_sems.at[mask - 1], recv_sem=recv_sems.at[mask - 1], device_id=peer, device_id_type=MESH
                )
            )
        for cp in copies:
            cp.start()
        land[mine] = v_ref[...]
        for cp in copies:
            cp.wait()
        acc = land[0]
        for k in range(1, N_DEV):
            acc = acc + land[k]
        o_ref[...] = acc

    return pl.pallas_call(
        body,
        name="all_reduce_small",
        out_shape=jax.ShapeDtypeStruct(vec.shape, F32),
        in_specs=[pl.BlockSpec(memory_space=pltpu.VMEM)],
        out_specs=pl.BlockSpec(memory_space=pltpu.VMEM),
        scratch_shapes=[pltpu.VMEM((N_DEV, rows, 128), F32), pltpu.SemaphoreType.DMA((N_DEV - 1,)), pltpu.SemaphoreType.DMA((N_DEV - 1,))],
    )(vec)


def _rows128(a, rows):
    flat = a.reshape(-1)
    return jnp.pad(flat, (0, rows * 128 - flat.shape[0])).reshape(rows, 128)


def _pad_rel(a):
    return jnp.pad(a.reshape(ATT_HEADS, -1)[:, :N_REL], ((0, 0), (0, N_REL_PAD - N_REL)))


SMALL = [("norm_mix", 16), ("lb_logits", 16), ("hg_norm", 8), ("rel_bias", 24), ("norm_ffn", 16), ("conv_b", 88), ("norm_ple", 16), ("final_norm", 16)]
CONV_W_FULL_ROWS = 3 * 2 * D_FF // 128
CONV_W_SHARD_ROWS = 40


def _pack_small(parts):
    return jnp.concatenate([_rows128(_pad_rel(parts[k]) if k == "rel_bias" else parts[k], rows) for k, rows in SMALL], axis=0)


def _unpack_small(packed, shapes):
    out, at = {}, 0
    for k, rows in SMALL:
        blk = packed[at : at + rows]
        at += rows
        if k == "rel_bias":
            out[k] = blk.reshape(ATT_HEADS, N_REL_PAD)[:, :N_REL].reshape(shapes[k])
        else:
            n = 1
            for s in shapes[k]:
                n *= s
            out[k] = blk.reshape(-1)[:n].reshape(shapes[k])
    return out, at


BIG = [("w_in", 1), ("w_out", 0), ("w_up", 1), ("w_down", 0), ("w_ple_gate", 0), ("w_ple_proj", 1)]


HBM = pl.BlockSpec(memory_space=pltpu.HBM)
SEM = pl.BlockSpec(memory_space=pltpu.SEMAPHORE)
EFFECT = pltpu.SideEffectType.DATAFLOW_SIDE_EFFECTING


def _copies(plan, refs, send_sems, recv_sems):
    return [
        pltpu.make_async_remote_copy(src_ref=src, dst_ref=dst, send_sem=send_sems.at[i], recv_sem=recv_sems.at[i], device_id=dev, device_id_type=MESH)
        for i, (src, dst, dev) in enumerate(plan(refs))
    ]


def _split_start(name, arrays, plan, n):
    k = len(arrays)

    def body(*refs):
        for cp in _copies(plan, refs[:k], refs[k], refs[k + 1]):
            cp.start()
        refs[-1][...] = jnp.zeros_like(refs[-1])

    out = pl.pallas_call(
        body,
        name=name,
        out_shape=(pltpu.SemaphoreType.DMA((n,)), pltpu.SemaphoreType.DMA((n,)), *[pltpu.HBM(a.shape, a.dtype) for a in arrays], jax.ShapeDtypeStruct((8, 128), F32)),
        in_specs=[HBM] * k,
        out_specs=(SEM, SEM, *[HBM] * k, pl.BlockSpec(memory_space=pltpu.VMEM)),
        input_output_aliases={i: 2 + i for i in range(k)},
        compiler_params=pltpu.CompilerParams(has_side_effects=EFFECT),
    )(*[pltpu.with_memory_space_constraint(a, pltpu.HBM) for a in arrays])
    return out[0], out[1], list(out[2 : 2 + k]), out[-1]


def _split_wait(name, send, recv, arrays, plan, after):
    k = len(arrays)

    def body(*refs):
        for cp in _copies(plan, refs[:k], refs[k], refs[k + 1]):
            cp.wait_send()
            cp.wait_recv()

    out = pl.pallas_call(
        body,
        name=name,
        out_shape=tuple(pltpu.HBM(a.shape, a.dtype) for a in arrays),
        in_specs=[HBM] * k + [SEM, SEM, ANY],
        out_specs=tuple([HBM] * k),
        input_output_aliases={i: i for i in range(k)},
        compiler_params=pltpu.CompilerParams(has_side_effects=EFFECT),
    )(*arrays, send, recv, after)
    return list(out)


def _cast_into(w, me, axis, name, dep):
    r, c = w.shape
    tr = _tile(r, (256, 176))
    nr = r // tr

    def body(me_ref, w_ref, dep_ref, o_ref):
        o_ref[...] = w_ref[...].astype(BF16)

    if axis == 1:
        shape, o_spec = (r, N_DEV * c), pl.BlockSpec((tr, c), lambda i, me: (i, me[0]))
    else:
        shape, o_spec = (N_DEV * r, c), pl.BlockSpec((tr, c), lambda i, me: (me[0] * nr + i, 0))
    return pl.pallas_call(
        body,
        name=name,
        grid_spec=pltpu.PrefetchScalarGridSpec(
            num_scalar_prefetch=1, grid=(nr,), in_specs=[pl.BlockSpec((tr, c), lambda i, me: (i, 0)), ANY], out_specs=o_spec
        ),
        out_shape=jax.ShapeDtypeStruct(shape, BF16),
        compiler_params=_params("parallel"),
    )(me, w, dep)


LATE = ["w_out", "w_up", "w_down", "w_ple_gate", "w_ple_proj"]
DIRECT = ["w_out"]
FORWARDED = [k for k in LATE if k not in DIRECT]
GROUPS = [["w_ple_proj", "w_ple_gate", "w_down"], ["w_up"], ["w_out"], ["w_in"]]
STAGES = ["ffn_act_bwd", "d_mix_out", "hgrn_bwd", "d_norm_mix_out"]
INTERLEAVED = {"w_up"}


class _Exchange:
    def __init__(self, big, position):
        self.big, self.axis = big, dict(BIG)
        self.size = {k: big[k].shape[self.axis[k]] for k in big}
        self.x, self.y, self.c = position
        chips = [(self.x, self.y)] + _other_chips(self.x, self.y)
        landed = [2 * cx + cy for cx, cy in chips]
        self.ids = {
            flag: jnp.stack([_block_index((cx, cy, self.c), flag) for cx, cy in chips] + landed).astype(jnp.int32) for flag in (False, True)
        }
        self.token, self.grads, self.state, self.wfull = None, {}, {}, {}


    def _slot(self, ref, k, dev):
        return _shard_of(ref, self.axis[k], self.size[k], dev, interleaved=k in INTERLEAVED)

    def _plan_gather(self, refs):
        x, y, c = _position()
        me, out = (x, y, c), []
        for k, ref in zip(LATE, refs):
            mine = self._slot(ref, k, me)
            out.append((mine, mine, (x, y, 1 - c)))
            out += [(mine, mine, (*chip, c)) for chip in _other_chips(x, y)]
            if k in DIRECT:
                out += [(mine, mine, (*chip, 1 - c)) for chip in _other_chips(x, y)]
        return out

    def _plan_forward(self, refs):
        x, y, c = _position()
        out = []
        for k, ref in zip(FORWARDED, refs):
            for chip in _other_chips(x, y):
                block = self._slot(ref, k, (*chip, c))
                out.append((block, block, (x, y, 1 - c)))
        return out

    def _plan_sibling(self, names, refs):
        x, y, c = _position()
        n = len(names)
        return [(self._slot(refs[i], k, (p // 2, p % 2, 1 - c)), refs[n + i].at[p], (x, y, 1 - c)) for i, k in enumerate(names) for p in range(4)]

    def _plan_chips(self, names, refs):
        x, y, c = _position()
        n = len(names)
        return [(refs[i].at[j], refs[n + i].at[j], (*chip, c)) for i in range(n) for j, chip in enumerate(_other_chips(x, y))]


    def gather(self, conv_w):
        w_in, conv_full = _all_gather([_cast_bf16(self.big["w_in"], "cast_w_in"), conv_w], [1, 1], [False, True])
        self.wfull["w_in"] = w_in
        me = {flag: _block_index((self.x, self.y, self.c), flag).astype(jnp.int32).reshape(1) for flag in (False, True)}
        fulls = [_cast_into(self.big[k], me[k in INTERLEAVED], self.axis[k], "cast_" + k, w_in) for k in LATE]
        send, recv, fulls, self.token = _split_start("gather_start", fulls, self._plan_gather, 4 * len(LATE) + 3 * len(DIRECT))
        self.late = (send, recv, fulls)
        return conv_full

    def weight(self, k):
        return self.wfull[k]

    def dep(self):
        token, self.token = self.token, None
        return token

    def grad(self, k, g):
        self.grads[k] = g
        for gi, names in enumerate(GROUPS):
            if k == names[-1]:
                plan = functools.partial(self._plan_sibling, names)
                lands = [lax.empty((4, *self._shard_shape(n)), F32) for n in names]
                send, recv, arrays, self.token = _split_start(f"sibling_start_{gi}", [self.grads[n] for n in names] + lands, plan, 4 * len(names))
                self.state[gi] = (send, recv, arrays, plan)

    def done(self, stage, after):
        if stage == "att_fwd":
            send, recv, fulls = self.late
            arrived = dict(zip(LATE, _split_wait("gather_wait", send, recv, fulls, self._plan_gather, after)))
            self.wfull.update({k: arrived[k] for k in DIRECT})
            fulls = [arrived[k] for k in FORWARDED]
            send, recv, fulls, self.token = _split_start("forward_start", fulls, self._plan_forward, 3 * len(FORWARDED))
            self.late = (send, recv, fulls)
        elif stage == "norm_ffn_fwd":
            send, recv, fulls = self.late
            self.wfull.update(zip(FORWARDED, _split_wait("forward_wait", send, recv, fulls, self._plan_forward, after)))
        elif stage in STAGES:
            self._to_chips(STAGES.index(stage), after)

    def _shard_shape(self, k):
        shape = list(self.grads[k].shape)
        shape[self.axis[k]] = self.size[k]
        return tuple(shape)

    def _to_chips(self, gi, after):
        names = GROUPS[gi]
        n = len(names)
        send, recv, arrays, plan = self.state[gi]
        arrays = _split_wait(f"sibling_wait_{gi}", send, recv, arrays, plan, after)
        own, parts = [], []
        for k, g, land in zip(names, arrays[:n], arrays[n:]):
            ids = self.ids[k in INTERLEAVED]
            own.append(_add_blocks(ids, g, land, self.axis[k], self.size[k], [0], F32, "add_own_" + k)[0])
            parts.append(_add_blocks(ids, g, land, self.axis[k], self.size[k], [1, 2, 3], BF16, "add_send_" + k))
        plan = functools.partial(self._plan_chips, names)
        lands = [lax.empty(part.shape, BF16) for part in parts]
        send, recv, arrays, self.token = _split_start(f"chips_start_{gi}", parts + lands, plan, 3 * n)
        self.state[gi] = (send, recv, arrays, plan, own)

    def finish(self, gi, after):
        names = GROUPS[gi]
        send, recv, arrays, plan, own = self.state[gi]
        arrays = _split_wait(f"chips_wait_{gi}", send, recv, arrays, plan, after)
        return {k: (o, r) for k, o, r in zip(names, own, arrays[len(names) :])}


class _Resident:
    def __init__(self, wfull):
        self.wfull, self.grads = wfull, {}

    def weight(self, k):
        return self.wfull[k]

    def grad(self, k, g):
        self.grads[k] = g

    def dep(self):
        return None

    def done(self, stage, after):
        pass


def _local_step(x, p, target, small, conv_w, ex):
    a1, r1 = _rms_fwd(x, small["norm_mix"], "norm_mix_fwd", dep=ex.dep())
    proj = _matmul(a1, ex.weight("w_in"), "nn", F32, "in_proj")
    bias = _bias_table(jnp.pad(small["rel_bias"], ((0, 0), (0, N_REL_PAD - N_REL))))
    y_hg, o_hg, states = _hgrn_fwd(proj, small["lb_logits"], small["hg_norm"])
    y_att = _att_fwd(proj, bias, dep=ex.dep())
    ex.done("att_fwd", y_att)
    ycat = jnp.concatenate([y_hg, y_att], axis=1)
    h1 = _matmul(ycat, ex.weight("w_out"), "nn", F32, "out_proj", resid=x, dep=ex.dep())
    a2, r2 = _rms_fwd(h1, small["norm_ffn"], "norm_ffn_fwd")
    ex.done("norm_ffn_fwd", a2)
    u = _matmul(a2, ex.weight("w_up"), "nn", BF16, "up_proj")
    conv_b = _interleave_cols(small["conv_b"])
    z = _ffn_act_fwd(u, conv_w, conv_b)
    h2 = _matmul(z, ex.weight("w_down"), "nn", F32, "down_proj", tk=2816, resid=h1)
    a3, r3 = _rms_fwd(h2, small["norm_ple"], "norm_ple_fwd")
    gpre = _matmul(a3, ex.weight("w_ple_gate"), "nn", F32, "ple_gate")
    pp = _matmul(p, ex.weight("w_ple_proj"), "nn", F32, "ple_proj")
    dh3, dgpre, dpp, d_final, loss = _ple_loss(gpre, pp, h2, small["final_norm"], target)

    ex.grad("w_ple_proj", _matmul(p, dpp, "tn", F32, "d_w_ple_proj", tk=2048))
    ex.grad("w_ple_gate", _matmul(a3, dgpre, "tn", F32, "d_w_ple_gate", tk=2048))
    da3 = _matmul(dgpre, ex.weight("w_ple_gate"), "nt", F32, "d_norm_ple_out")
    dh2, d_ple = _rms_bwd(da3, h2, r3, small["norm_ple"], dh3, "norm_ple_bwd")
    dz = _matmul(dh2, ex.weight("w_down"), "nt", BF16, "d_ffn_act")
    ex.grad("w_down", _matmul(z, dh2, "tn", F32, "d_w_down", tk=2048))
    du, dcw, dcb = _ffn_act_bwd(u, dz, conv_w, conv_b, dep=ex.dep())
    ex.done("ffn_act_bwd", du)
    d_conv_w, d_conv_b = _deinterleave_cols(dcw), _deinterleave_cols(dcb)
    ex.grad("w_up", _matmul(a2, du, "tn", F32, "d_w_up", tk=2048, dep=ex.dep()))
    da2 = _matmul(du, ex.weight("w_up"), "nt", F32, "d_norm_ffn_out", tk=2816, dep=ex.dep())
    dh1, d_ffn = _rms_bwd(da2, h1, r2, small["norm_ffn"], dh2, "norm_ffn_bwd")
    dycat = _matmul(dh1, ex.weight("w_out"), "nt", F32, "d_mix_out")
    ex.done("d_mix_out", dycat)
    ex.grad("w_out", _matmul(ycat, dh1, "tn", F32, "d_w_out", tk=2048, dep=ex.dep()))
    dp_hg, d_lb, d_hgn = _hgrn_bwd(proj, small["lb_logits"], small["hg_norm"], o_hg, dycat, states, dep=ex.dep())
    ex.done("hgrn_bwd", d_lb)
    dq_att, dk_att, dv_att, gsum = _att_bwd(proj, bias, dycat, dep=ex.dep())
    d_rel = _rel_bias_grad(gsum)
    dproj = jnp.concatenate([dp_hg[0], dp_hg[1], dp_hg[2], dp_hg[3], dq_att, dk_att, dv_att], axis=1)
    ex.grad("w_in", _matmul(a1, dproj, "tn", F32, "d_w_in", tk=2048))
    da1 = _matmul(dproj, ex.weight("w_in"), "nt", F32, "d_norm_mix_out", tk=1792, dep=ex.dep())
    ex.done("d_norm_mix_out", da1)
    dx, d_mix = _rms_bwd(da1, x, r1, small["norm_mix"], dh1, "norm_mix_bwd", dep=ex.dep())
    d_small = {
        "norm_mix": d_mix, "lb_logits": d_lb, "hg_norm": d_hgn, "rel_bias": d_rel, "norm_ffn": d_ffn,
        "conv_b": d_conv_b, "norm_ple": d_ple, "final_norm": d_final,
    }
    return loss, dx, d_small, d_conv_w


def kernel(x, p, norm_mix, w_in, lb_logits, hg_norm, rel_bias, w_out, norm_ffn, w_up, conv_w, conv_b, w_down, norm_ple, w_ple_gate, w_ple_proj, final_norm, loss_target, m_norm_mix, m_w_in, m_lb_logits, m_hg_norm, m_rel_bias, m_w_out, m_norm_ffn, m_w_up, m_conv_w, m_conv_b, m_w_down, m_norm_ple, m_w_ple_gate, m_w_ple_proj, m_final_norm, v_norm_mix, v_w_in, v_lb_logits, v_hg_norm, v_rel_bias, v_w_out, v_norm_ffn, v_w_up, v_conv_w, v_conv_b, v_w_down, v_norm_ple, v_w_ple_gate, v_w_ple_proj, v_final_norm):
    given = dict(locals())
    mx, my, mc = _position()
    me = 4 * mx + 2 * my + mc
    big = {k: given[k][0] for k, _ in BIG}
    ex = _Exchange(big, (mx, my, mc))
    conv_w_full = ex.gather(conv_w[0])

    small = {
        "norm_mix": norm_mix, "lb_logits": lb_logits, "hg_norm": hg_norm, "rel_bias": rel_bias[0], "norm_ffn": norm_ffn,
        "conv_b": conv_b, "norm_ple": norm_ple, "final_norm": final_norm.reshape(1, -1),
    }
    loss, dx, d_small, d_conv_w = _local_step(x[0], p[0, 0], loss_target[0], small, conv_w_full, ex)

    packed = jnp.concatenate([_pack_small(d_small), _rows128(d_conv_w, CONV_W_FULL_ROWS), _rows128(loss[0:1, 0:1], 8)], axis=0)
    reduced = _all_reduce_small(packed)

    out = {}
    for gi in range(len(GROUPS)):
        for k, (o, r) in ex.finish(gi, reduced).items():
            g, d, nm, nv = _adam_big(big[k], given["m_" + k][0], given["v_" + k][0], o, r, "adam_" + k)
            out[k] = tuple(a[None] for a in (g, d, nm, nv))
    shapes = {k: given[k].shape for k, _ in SMALL}
    g_small, at = _unpack_small(reduced, shapes)
    g_conv_full = reduced[at : at + CONV_W_FULL_ROWS].reshape(3, 2 * D_FF)
    total_loss = reduced[at + CONV_W_FULL_ROWS, 0]
    cw = conv_w.shape[2]
    g_conv = lax.dynamic_slice_in_dim(g_conv_full, me * cw, cw, axis=1)

    def pack_with_conv(parts, conv_part):
        return jnp.concatenate([_pack_small(parts), _rows128(conv_part, CONV_W_SHARD_ROWS)], axis=0)

    d_pk, m_pk, v_pk = _adam_small(
        pack_with_conv({k: given[k] for k, _ in SMALL}, conv_w),
        pack_with_conv(g_small, g_conv),
        pack_with_conv({k: given["m_" + k] for k, _ in SMALL}, m_conv_w),
        pack_with_conv({k: given["v_" + k] for k, _ in SMALL}, v_conv_w),
    )
    for name, pk in (("d", d_pk), ("m", m_pk), ("v", v_pk)):
        parts, at = _unpack_small(pk, shapes)
        parts["conv_w"] = pk[at : at + CONV_W_SHARD_ROWS].reshape(-1)[: 3 * cw].reshape(conv_w.shape)
        for k, a in parts.items():
            out.setdefault(k, {})
            out[k][name] = a
    for k, _ in SMALL:
        out[k]["g"] = g_small[k]
    out["conv_w"]["g"] = g_conv.reshape(conv_w.shape)

    order = ["norm_mix", "w_in", "lb_logits", "hg_norm", "rel_bias", "w_out", "norm_ffn", "w_up", "conv_w", "conv_b", "w_down", "norm_ple", "w_ple_gate", "w_ple_proj", "final_norm"]

    def pick(k, what):
        return out[k][what] if isinstance(out[k], dict) else out[k][{"g": 0, "d": 1, "m": 2, "v": 3}[what]]

    return (total_loss, dx[None], *[pick(k, "g") for k in order], *[pick(k, "d") for k in order], *[pick(k, "m") for k in order], *[pick(k, "v") for k in order])
```

```python
import functools

import jax
import jax.numpy as jnp
from jax import lax
from jax.experimental import pallas as pl
from jax.experimental.pallas import tpu as pltpu

F32 = jnp.float32
BF16 = jnp.bfloat16

D_MODEL = 2048
CHUNK = 64
HG_HEADS = 8
HEAD_DIM = 128
HG_WIDTH = HG_HEADS * HEAD_DIM
ATT_HEADS = 8
ATT_WIDTH = ATT_HEADS * HEAD_DIM
LEFT_CHUNKS = 8
PAD = LEFT_CHUNKS * CHUNK
BAND = PAD + CHUNK
REL_CLIP = 128
N_REL = 2 * REL_CLIP + 1
N_REL_PAD = 384
D_FF = 5632
EPS = 1e-6
ATT_SCALE = HEAD_DIM ** -0.5
SUB = 32
HG_BLOCK = 8
Q_BLOCK = 4 * CHUNK
K_BLOCK = Q_BLOCK + PAD
DIAG = 1024

ADAM_LR = 0.001
ADAM_B1 = 0.9
ADAM_B2 = 0.999
ADAM_EPS = 1e-08
ADAM_WD = 0.01
ADAM_STEP = 10

N_DEV = 8
VMEM_LIMIT = 48 * 1024 * 1024
MESH = pl.DeviceIdType.MESH
ANY = pl.BlockSpec(memory_space=pl.ANY)
HIGHEST = lax.Precision.HIGHEST

NN = (((1,), (0,)), ((), ()))
NT = (((1,), (1,)), ((), ()))
TN = (((0,), (0,)), ((), ()))


def _params(*sem):
    return pltpu.CompilerParams(dimension_semantics=sem if sem else None, vmem_limit_bytes=VMEM_LIMIT)


def _pallas(body, n_in, dep, **kw):
    if dep is None:
        return pl.pallas_call(body, **kw)

    def body_after(*refs):
        body(*refs[:n_in], *refs[n_in + 1 :])

    call = pl.pallas_call(body_after, **dict(kw, in_specs=list(kw["in_specs"]) + [ANY]))
    return lambda *ops: call(*ops, dep)


def _dot(a, b, dims=NN):
    return lax.dot_general(a, b, dims, preferred_element_type=F32)


def _dot3(a, b, dims=NN):
    a_hi, b_hi = a.astype(BF16), b.astype(BF16)
    a_lo, b_lo = (a - a_hi.astype(F32)).astype(BF16), (b - b_hi.astype(F32)).astype(BF16)
    return _dot(a_hi, b_hi, dims) + (_dot(a_hi, b_lo, dims) + _dot(a_lo, b_hi, dims))


def _sigmoid(x):
    return 1.0 / (1.0 + jnp.exp(-x))


def _tile(n, prefs):
    for t in prefs:
        if n % t == 0:
            return t
    return n


def _matmul(a, b, mode, out_dtype, name, tm=1024, tn=1024, tk=None, resid=None, dep=None):
    if mode == "nn":
        (m, k), n = a.shape, b.shape[1]
    elif mode == "nt":
        (m, k), n = a.shape, b.shape[0]
    else:
        (k, m), n = a.shape, b.shape[1]
    tm = _tile(m, (tm, 512, 256, 128))
    tn = _tile(n, (tn, 1408, 512, 256, 128))
    tk = k if tk is None else _tile(k, (tk,))
    nk = k // tk
    dims = {"nn": NN, "nt": NT, "tn": TN}[mode]
    a_spec = pl.BlockSpec((tk, tm), lambda i, j, s: (s, i)) if mode == "tn" else pl.BlockSpec((tm, tk), lambda i, j, s: (i, s))
    b_spec = pl.BlockSpec((tn, tk), lambda i, j, s: (j, s)) if mode == "nt" else pl.BlockSpec((tk, tn), lambda i, j, s: (s, j))
    o_spec = pl.BlockSpec((tm, tn), lambda i, j, s: (i, j))
    has_res = resid is not None

    def body(*refs):
        a_ref, b_ref = refs[0], refs[1]
        o_ref = refs[2 + has_res]
        part = _dot(a_ref[...].astype(BF16), b_ref[...].astype(BF16), dims)

        def finish(acc):
            if has_res:
                acc = acc + refs[2][...]
            o_ref[...] = acc.astype(out_dtype)

        if nk == 1:
            finish(part)
        else:
            acc_ref = refs[-1]
            s = pl.program_id(2)

            @pl.when(s == 0)
            def _():
                acc_ref[...] = part

            @pl.when(s > 0)
            def _():
                acc_ref[...] += part

            @pl.when(s == nk - 1)
            def _():
                finish(acc_ref[...])

    return _pallas(
        body,
        2 + has_res,
        dep,
        name=name,
        grid=(m // tm, n // tn, nk),
        in_specs=[a_spec, b_spec] + ([o_spec] if has_res else []),
        out_specs=o_spec,
        out_shape=jax.ShapeDtypeStruct((m, n), out_dtype),
        scratch_shapes=[pltpu.VMEM((tm, tn), F32)] if nk > 1 else [],
        compiler_params=_params("parallel", "parallel", "arbitrary"),
    )(*([a, b] + ([resid] if has_res else [])))


def _rms_fwd(x, g, name, dep=None):
    t, d = x.shape
    tm = _tile(t, (256,))

    def body(x_ref, g_ref, a_ref, r_ref):
        xv = x_ref[...]
        r = lax.rsqrt(jnp.mean(xv * xv, axis=-1, keepdims=True) + EPS)
        a_ref[...] = (xv * r * g_ref[...]).astype(BF16)
        r_ref[...] = r

    row = pl.BlockSpec((tm, d), lambda i: (i, 0))
    return _pallas(
        body,
        2,
        dep,
        name=name,
        grid=(t // tm,),
        in_specs=[row, pl.BlockSpec((1, d), lambda i: (0, 0))],
        out_specs=[row, pl.BlockSpec((tm, 1), lambda i: (i, 0))],
        out_shape=[jax.ShapeDtypeStruct((t, d), BF16), jax.ShapeDtypeStruct((t, 1), F32)],
        compiler_params=_params("parallel"),
    )(x, g)


def _rms_bwd(da, x, r, g, resid, name, dep=None):
    t, d = x.shape
    tm = _tile(t, (256,))

    def body(da_ref, x_ref, r_ref, g_ref, res_ref, dx_ref, dg_ref):
        i = pl.program_id(0)
        rv = r_ref[...]
        n = x_ref[...] * rv
        dav = da_ref[...]
        dn = dav * g_ref[...]
        dx_ref[...] = rv * (dn - n * jnp.mean(dn * n, axis=-1, keepdims=True)) + res_ref[...]
        part = jnp.sum(dav * n, axis=0, keepdims=True)

        @pl.when(i == 0)
        def _():
            dg_ref[...] = part

        @pl.when(i > 0)
        def _():
            dg_ref[...] += part

    row = pl.BlockSpec((tm, d), lambda i: (i, 0))
    vec = pl.BlockSpec((1, d), lambda i: (0, 0))
    return _pallas(
        body,
        5,
        dep,
        name=name,
        grid=(t // tm,),
        in_specs=[row, row, pl.BlockSpec((tm, 1), lambda i: (i, 0)), vec, row],
        out_specs=[row, vec],
        out_shape=[jax.ShapeDtypeStruct((t, d), F32), jax.ShapeDtypeStruct((1, d), F32)],
        compiler_params=_params("arbitrary"),
    )(da, x, r, g, resid)


def _tri(n, upper):
    r = lax.broadcasted_iota(jnp.int32, (n, n), 0)
    c = lax.broadcasted_iota(jnp.int32, (n, n), 1)
    return jnp.where((c >= r) if upper else (c <= r), 1.0, 0.0).astype(F32)


def _hgrn_gates(q, fp, lbl):
    l0, l1 = lbl[0:1, :], lbl[1:2, :]
    mx = jnp.maximum(l0, l1)
    e0, e1 = jnp.exp(l0 - mx), jnp.exp(l1 - mx)
    lb = e0 / (e0 + e1)
    sig = _sigmoid(fp)
    f = lb + (1.0 - lb) * sig
    kk = (1.0 - lb) * _sigmoid(-fp)
    sq = _sigmoid(q)
    b = jnp.dot(_tri(CHUNK, False), jnp.log(f), precision=HIGHEST, preferred_element_type=F32)
    return lb, sig, f, kk, sq, q * sq, b


def _heads(x):
    return [x[:, j * HEAD_DIM : (j + 1) * HEAD_DIM] for j in range(x.shape[1] // HEAD_DIM)]


def _wide(parts):
    return jnp.concatenate(parts, axis=1)


def _intra_blocks(b):
    out = []
    for lo in range(0, CHUNK, SUB):
        hi = lo + SUB
        br = b[lo + SUB // 2 : lo + SUB // 2 + 1, :]
        row = lax.broadcasted_iota(jnp.int32, (SUB, hi), 0) + lo
        col = lax.broadcasted_iota(jnp.int32, (SUB, hi), 1)
        out.append((lo, hi, jnp.exp(b[lo:hi] - br), jnp.exp(br - b[:hi]), col <= row))
    return out


def _hgrn_fwd(proj, lb_logits, hg_norm):
    t = proj.shape[0]
    nc = t // CHUNK

    def body(q_ref, f_ref, i_ref, g_ref, lbl_ref, hgn_ref, y_ref, o_ref, st_ref, s_scr):
        c = pl.program_id(1)

        @pl.when(c == 0)
        def _():
            s_scr[...] = jnp.zeros_like(s_scr)

        hs = range(HG_BLOCK)
        sts = [s_scr[j] for j in hs]
        _, _, _, kk, _, qf, b = _hgrn_gates(q_ref[...], f_ref[...], lbl_ref[...])
        vb = _heads(i_ref[...].astype(BF16))
        bl = b[CHUNK - 1 : CHUNK, :]
        qe = _heads((qf * jnp.exp(b)).astype(BF16))
        kd = _heads((kk * jnp.exp(bl - b)).astype(BF16))
        decay = _heads(jnp.exp(bl))
        o = [_dot(qe[j], sts[j].astype(BF16), NT) for j in hs]
        parts = [[] for _ in hs]
        for lo, hi, ea, eb, mask in _intra_blocks(b):
            a, bk = _heads((qf[lo:hi] * ea).astype(BF16)), _heads((kk[:hi] * eb).astype(BF16))
            p = [jnp.where(mask, _dot(a[j], bk[j], NT), 0.0).astype(BF16) for j in hs]
            for j in hs:
                parts[j].append(_dot(p[j], vb[j][:hi]))
        o = [o[j] + jnp.concatenate(parts[j], axis=0) for j in hs]
        new = [sts[j] * decay[j] + _dot(vb[j], kd[j], TN) for j in hs]
        hgn = hgn_ref[...]
        on = [o[j] * lax.rsqrt(jnp.mean(o[j] * o[j], axis=-1, keepdims=True) + EPS) * hgn for j in hs]
        gg = g_ref[...]
        for j in hs:
            st_ref[j] = sts[j]
            s_scr[j] = new[j]
        o_ref[...] = _wide(o)
        y_ref[...] = (_wide(on) * (gg * _sigmoid(gg))).astype(BF16)

    wide = HG_BLOCK * HEAD_DIM
    groups = HG_HEADS // HG_BLOCK

    def col(k):
        return pl.BlockSpec((CHUNK, wide), lambda g, c: (c, k * groups + g))

    out = pl.BlockSpec((CHUNK, wide), lambda g, c: (c, g))
    return pl.pallas_call(
        body,
        name="hgrn_fwd",
        grid=(groups, nc),
        in_specs=[col(0), col(1), col(2), col(3), pl.BlockSpec((2, wide), lambda g, c: (0, g)), pl.BlockSpec((1, HEAD_DIM), lambda g, c: (0, 0))],
        out_specs=[out, out, pl.BlockSpec((HG_BLOCK, None, HEAD_DIM, HEAD_DIM), lambda g, c: (g, c, 0, 0))],
        out_shape=[
            jax.ShapeDtypeStruct((t, HG_WIDTH), BF16),
            jax.ShapeDtypeStruct((t, HG_WIDTH), F32),
            jax.ShapeDtypeStruct((HG_HEADS, nc, HEAD_DIM, HEAD_DIM), F32),
        ],
        scratch_shapes=[pltpu.VMEM((HG_BLOCK, HEAD_DIM, HEAD_DIM), F32)],
        compiler_params=_params("arbitrary", "arbitrary"),
    )(proj, proj, proj, proj, lb_logits, hg_norm)


def _hgrn_bwd(proj, lb_logits, hg_norm, o_hg, dycat, states, dep=None):
    t = proj.shape[0]
    nc = t // CHUNK

    def body(q_ref, f_ref, i_ref, g_ref, lbl_ref, hgn_ref, o_ref, dy_ref, st_ref, dp_ref, dlbl_ref, dhgn_ref, dst_scr, dlb_scr):
        h = pl.program_id(0)
        c = pl.program_id(1)

        @pl.when(c == 0)
        def _():
            dst_scr[...] = jnp.zeros_like(dst_scr)
            dlb_scr[...] = jnp.zeros_like(dlb_scr)

        @pl.when((c == 0) & (h == 0))
        def _():
            dhgn_ref[...] = jnp.zeros_like(dhgn_ref)

        hs = range(HG_BLOCK)
        hgn = _wide([hgn_ref[...]] * HG_BLOCK)
        q, fp, gg, vi = q_ref[...], f_ref[...], g_ref[...], i_ref[...]
        lb, sig, f, kk, sq, qf, b = _hgrn_gates(q, fp, lbl_ref[...])
        o, dy = o_ref[...], dy_ref[...]
        sg = _sigmoid(gg)
        n = _wide([oh * lax.rsqrt(jnp.mean(oh * oh, axis=-1, keepdims=True) + EPS) for oh in _heads(o)])
        don = dy * (gg * sg)
        dgg = dy * (n * hgn) * (sg * (1.0 + gg * (1.0 - sg)))
        d_hgn = sum(_heads(jnp.sum(don * n, axis=0, keepdims=True)))
        dn = don * hgn
        do = _wide(
            [
                lax.rsqrt(jnp.mean(oh * oh, axis=-1, keepdims=True) + EPS) * (dnh - nh * jnp.mean(dnh * nh, axis=-1, keepdims=True))
                for oh, dnh, nh in zip(_heads(o), _heads(dn), _heads(n))
            ]
        )
        sts = [st_ref[j] for j in hs]
        dstn = [dst_scr[j] for j in hs]
        bl = b[CHUNK - 1 : CHUNK, :]
        e_b, e_bl, e_l = jnp.exp(b), jnp.exp(bl - b), jnp.exp(bl)
        doh, vih = _heads(do), _heads(vi)
        dobh = _heads(do.astype(BF16))
        dq_acc = _wide([_dot3(doh[j], sts[j]) for j in hs]) * e_b
        dk_inter = _wide([_dot3(vih[j], dstn[j]) for j in hs]) * e_bl
        dk_acc = dk_inter
        kd = _heads((kk * e_bl).astype(BF16))
        dv_acc = _wide([_dot(kd[j], dstn[j].astype(BF16), NT) for j in hs])
        qe, decay = _heads((qf * e_b).astype(BF16)), _heads(e_l)
        dst_new = [dstn[j] * decay[j] + _dot(dobh[j], qe[j], TN) for j in hs]
        db_last = e_l * _wide([jnp.sum(sts[j] * dstn[j], axis=0, keepdims=True) for j in hs]) + jnp.sum(kk * dk_inter, axis=0, keepdims=True)
        dq_parts = []
        for lo, hi, ea, eb, mask in _intra_blocks(b):
            a, bk = qf[lo:hi] * ea, kk[:hi] * eb
            ah, bkh = _heads(a), _heads(bk)
            abh, bkbh = _heads(a.astype(BF16)), _heads(bk.astype(BF16))
            p = [jnp.where(mask, _dot(abh[j], bkbh[j], NT), 0.0).astype(BF16) for j in hs]
            dp = [jnp.where(mask, _dot3(doh[j][lo:hi], vih[j][:hi], NT), 0.0) for j in hs]
            dq_parts.append(_wide([_dot3(dp[j], bkh[j]) for j in hs]) * ea)
            dki = _wide([_dot3(dp[j], ah[j], TN) for j in hs]) * eb
            dvi = _wide([_dot(p[j], dobh[j][lo:hi], TN) for j in hs])
            if hi < CHUNK:
                zeros = jnp.zeros((CHUNK - hi, HG_BLOCK * HEAD_DIM), F32)
                dki = jnp.concatenate([dki, zeros], axis=0)
                dvi = jnp.concatenate([dvi, zeros], axis=0)
            dk_acc = dk_acc + dki
            dv_acc = dv_acc + dvi
        dq_acc = dq_acc + jnp.concatenate(dq_parts, axis=0)
        rows = lax.broadcasted_iota(jnp.int32, dq_acc.shape, 0)
        db = qf * dq_acc - kk * dk_acc + jnp.where(rows == CHUNK - 1, db_last, 0.0)
        dlf = jnp.dot(_tri(CHUNK, True), db, precision=HIGHEST, preferred_element_type=F32)
        dfk = dlf / f - dk_acc
        dp_ref[0] = (dq_acc * (sq * (1.0 + q * (1.0 - sq)))).astype(BF16)
        dp_ref[1] = ((1.0 - lb) * dfk * sig * (1.0 - sig)).astype(BF16)
        dp_ref[2] = dv_acc.astype(BF16)
        dp_ref[3] = dgg.astype(BF16)
        dlb_scr[...] += jnp.sum(dfk * (1.0 - sig), axis=0, keepdims=True)
        dhgn_ref[...] += d_hgn
        for j in hs:
            dst_scr[j] = dst_new[j]

        @pl.when(c == nc - 1)
        def _():
            dl0 = dlb_scr[...] * lb * (1.0 - lb)
            dlbl_ref[0:1, :] = dl0
            dlbl_ref[1:2, :] = -dl0

    wide = HG_BLOCK * HEAD_DIM
    groups = HG_HEADS // HG_BLOCK

    def col(k):
        return pl.BlockSpec((CHUNK, wide), lambda g, c: (nc - 1 - c, k * groups + g))

    blk = pl.BlockSpec((CHUNK, wide), lambda g, c: (nc - 1 - c, g))
    return _pallas(
        body,
        9,
        dep,
        name="hgrn_bwd",
        grid=(groups, nc),
        in_specs=[
            col(0), col(1), col(2), col(3),
            pl.BlockSpec((2, wide), lambda g, c: (0, g)),
            pl.BlockSpec((1, HEAD_DIM), lambda g, c: (0, 0)),
            blk, blk,
            pl.BlockSpec((HG_BLOCK, None, HEAD_DIM, HEAD_DIM), lambda g, c: (g, nc - 1 - c, 0, 0)),
        ],
        out_specs=[
            pl.BlockSpec((4, CHUNK, wide), lambda g, c: (0, nc - 1 - c, g)),
            pl.BlockSpec((2, wide), lambda g, c: (0, g)),
            pl.BlockSpec((1, HEAD_DIM), lambda g, c: (0, 0)),
        ],
        out_shape=[
            jax.ShapeDtypeStruct((4, t, HG_WIDTH), BF16),
            jax.ShapeDtypeStruct((2, HG_WIDTH), F32),
            jax.ShapeDtypeStruct((1, HEAD_DIM), F32),
        ],
        scratch_shapes=[pltpu.VMEM((HG_BLOCK, HEAD_DIM, HEAD_DIM), F32), pltpu.VMEM((1, wide), F32)],
        compiler_params=_params("arbitrary", "arbitrary"),
    )(proj, proj, proj, proj, lb_logits, hg_norm, o_hg, dycat, states)


def _diagonal_slots(shift):
    i = lax.broadcasted_iota(jnp.int32, (N_REL_PAD, DIAG), 0)
    u = lax.broadcasted_iota(jnp.int32, (N_REL_PAD, DIAG), 1)
    offset = u - shift if shift else jnp.where(u < K_BLOCK, u, u - DIAG)
    return jnp.where(jnp.clip(PAD - offset, -REL_CLIP, REL_CLIP) + REL_CLIP == i, 1.0, 0.0).astype(BF16)


def _split3(x):
    hi = x.astype(BF16)
    mid = (x - hi.astype(F32)).astype(BF16)
    return hi, mid, (x - hi.astype(F32) - mid.astype(F32)).astype(BF16)


def _bias_table(rel_bias):
    def body(rb_ref, o_ref, diag):
        h = pl.program_id(0)

        @pl.when(h == 0)
        def _():
            hi, mid, lo = _split3(rb_ref[...])
            slots = _diagonal_slots(0)
            diag[...] = _dot(hi, slots) + (_dot(mid, slots) + _dot(lo, slots))

        rows = jnp.broadcast_to(diag[pl.ds(h, 1), :], (Q_BLOCK, DIAG))
        o_ref[...] = pltpu.roll(rows, 0, 1, stride=1, stride_axis=0)[:, :K_BLOCK]

    return pl.pallas_call(
        body,
        name="bias_table",
        grid=(ATT_HEADS,),
        in_specs=[pl.BlockSpec((ATT_HEADS, N_REL_PAD), lambda h: (0, 0))],
        out_specs=pl.BlockSpec((None, Q_BLOCK, K_BLOCK), lambda h: (h, 0, 0)),
        out_shape=jax.ShapeDtypeStruct((ATT_HEADS, Q_BLOCK, K_BLOCK), F32),
        scratch_shapes=[pltpu.VMEM((ATT_HEADS, DIAG), F32)],
        compiler_params=_params("arbitrary"),
    )(rel_bias)


def _att_probs(q_ref, kpad, bias_ref, blk):
    qs = (q_ref[...] * ATT_SCALE).astype(BF16)
    start = pl.multiple_of(blk * Q_BLOCK, Q_BLOCK)
    kb = kpad[pl.ds(start, K_BLOCK), :]
    s = _dot(qs, kb, NT) + bias_ref[...]
    row = lax.broadcasted_iota(jnp.int32, (Q_BLOCK, K_BLOCK), 0)
    col = lax.broadcasted_iota(jnp.int32, (Q_BLOCK, K_BLOCK), 1)
    first = row - (row & (CHUNK - 1))
    valid = (col >= first) & (col < first + BAND) & (col + (blk * Q_BLOCK - PAD) >= 0)
    s = jnp.where(valid, s, jnp.finfo(F32).min)
    e = jnp.exp(s - jnp.max(s, axis=-1, keepdims=True))
    return qs, kb, start, e / jnp.sum(e, axis=-1, keepdims=True)


def _fill_padded(dst, src):
    dst[0:PAD, :] = jnp.zeros((PAD, HEAD_DIM), BF16)
    dst[PAD:, :] = src[...].astype(BF16)


def _att_fwd(proj, bias, dep=None):
    t = proj.shape[0]
    nb = t // Q_BLOCK

    def body(q_ref, k_ref, v_ref, bias_ref, y_ref, kpad, vpad):
        c = pl.program_id(1)

        @pl.when(c == 0)
        def _():
            _fill_padded(kpad, k_ref)
            _fill_padded(vpad, v_ref)

        _, _, start, p = _att_probs(q_ref, kpad, bias_ref, c)
        y_ref[...] = _dot(p.astype(BF16), vpad[pl.ds(start, K_BLOCK), :]).astype(BF16)

    base = 4 * HG_HEADS
    return _pallas(
        body,
        4,
        dep,
        name="att_fwd",
        grid=(ATT_HEADS, nb),
        in_specs=[
            pl.BlockSpec((Q_BLOCK, HEAD_DIM), lambda h, c: (c, base + h)),
            pl.BlockSpec((t, HEAD_DIM), lambda h, c: (0, base + ATT_HEADS + h)),
            pl.BlockSpec((t, HEAD_DIM), lambda h, c: (0, base + 2 * ATT_HEADS + h)),
            pl.BlockSpec((None, Q_BLOCK, K_BLOCK), lambda h, c: (h, 0, 0)),
        ],
        out_specs=pl.BlockSpec((Q_BLOCK, HEAD_DIM), lambda h, c: (c, h)),
        out_shape=jax.ShapeDtypeStruct((t, ATT_WIDTH), BF16),
        scratch_shapes=[pltpu.VMEM((t + PAD, HEAD_DIM), BF16), pltpu.VMEM((t + PAD, HEAD_DIM), BF16)],
        compiler_params=_params("arbitrary", "arbitrary"),
    )(proj, proj, proj, bias)


def _att_bwd(proj, bias, dycat, dep=None):
    t = proj.shape[0]
    nb = t // Q_BLOCK

    def body(q_ref, k_ref, v_ref, bias_ref, dy_ref, dq_ref, dk_ref, dv_ref, g_ref, kpad, vpad, dkacc, dvacc):
        c = pl.program_id(1)

        @pl.when(c == 0)
        def _():
            _fill_padded(kpad, k_ref)
            _fill_padded(vpad, v_ref)
            dkacc[...] = jnp.zeros_like(dkacc)
            dvacc[...] = jnp.zeros_like(dvacc)
            g_ref[...] = jnp.zeros_like(g_ref)

        qs, kb, start, p = _att_probs(q_ref, kpad, bias_ref, c)
        band = pl.ds(start, K_BLOCK)
        dyb = dy_ref[...].astype(BF16)
        dvacc[band, :] += _dot(p.astype(BF16), dyb, TN)
        dp = _dot(dyb, vpad[band, :], NT)
        ds = p * (dp - jnp.sum(dp * p, axis=-1, keepdims=True))
        g_ref[...] += ds
        dsb = ds.astype(BF16)
        dq_ref[...] = (_dot(dsb, kb) * ATT_SCALE).astype(BF16)
        dkacc[band, :] += _dot(dsb, qs, TN)

        @pl.when(c == nb - 1)
        def _():
            dk_ref[...] = dkacc[PAD:, :].astype(BF16)
            dv_ref[...] = dvacc[PAD:, :].astype(BF16)

    base = 4 * HG_HEADS
    whole = pl.BlockSpec((t, HEAD_DIM), lambda h, c: (0, h))
    return _pallas(
        body,
        5,
        dep,
        name="att_bwd",
        grid=(ATT_HEADS, nb),
        in_specs=[
            pl.BlockSpec((Q_BLOCK, HEAD_DIM), lambda h, c: (c, base + h)),
            pl.BlockSpec((t, HEAD_DIM), lambda h, c: (0, base + ATT_HEADS + h)),
            pl.BlockSpec((t, HEAD_DIM), lambda h, c: (0, base + 2 * ATT_HEADS + h)),
            pl.BlockSpec((None, Q_BLOCK, K_BLOCK), lambda h, c: (h, 0, 0)),
            pl.BlockSpec((Q_BLOCK, HEAD_DIM), lambda h, c: (c, HG_HEADS + h)),
        ],
        out_specs=[pl.BlockSpec((Q_BLOCK, HEAD_DIM), lambda h, c: (c, h)), whole, whole, pl.BlockSpec((None, Q_BLOCK, K_BLOCK), lambda h, c: (h, 0, 0))],
        out_shape=[
            jax.ShapeDtypeStruct((t, ATT_WIDTH), BF16),
            jax.ShapeDtypeStruct((t, ATT_WIDTH), BF16),
            jax.ShapeDtypeStruct((t, ATT_WIDTH), BF16),
            jax.ShapeDtypeStruct((ATT_HEADS, Q_BLOCK, K_BLOCK), F32),
        ],
        scratch_shapes=[
            pltpu.VMEM((t + PAD, HEAD_DIM), BF16),
            pltpu.VMEM((t + PAD, HEAD_DIM), BF16),
            pltpu.VMEM((t + PAD, HEAD_DIM), F32),
            pltpu.VMEM((t + PAD, HEAD_DIM), F32),
        ],
        compiler_params=_params("arbitrary", "arbitrary"),
    )(proj, proj, proj, bias, dycat)


def _rel_bias_grad(gsum):
    def body(g_ref, o_ref):
        r = lax.broadcasted_iota(jnp.int32, (Q_BLOCK, Q_BLOCK), 0)
        c = lax.broadcasted_iota(jnp.int32, (Q_BLOCK, Q_BLOCK), 1)
        flip = jnp.where(r + c == Q_BLOCK - 1, 1.0, 0.0).astype(BF16)
        sums = []
        for h in range(ATT_HEADS):
            hi, mid, lo = _split3(g_ref[h])
            rev = _dot(flip, hi) + (_dot(flip, mid) + _dot(flip, lo))
            wide = jnp.concatenate([rev, jnp.zeros((Q_BLOCK, DIAG - K_BLOCK), F32)], axis=1)
            sums.append(jnp.sum(pltpu.roll(wide, 0, 1, stride=1, stride_axis=0), axis=0, keepdims=True))
        hi, mid, lo = _split3(jnp.concatenate(sums, axis=0))
        slots = _diagonal_slots(Q_BLOCK - 1)
        o_ref[...] = _dot(hi, slots, NT) + (_dot(mid, slots, NT) + _dot(lo, slots, NT))

    return pl.pallas_call(
        body,
        name="rel_bias_grad",
        out_shape=jax.ShapeDtypeStruct((ATT_HEADS, N_REL_PAD), F32),
        compiler_params=_params(),
    )(gsum)


HALO = 16


FF_TILE = 1408
FF_TILES = D_FF // FF_TILE


def _interleave_cols(a):
    lead = a.shape[:-1]
    return jnp.swapaxes(a.reshape(*lead, 2, FF_TILES, FF_TILE), -3, -2).reshape(*lead, 2 * D_FF)


def _deinterleave_cols(a):
    lead = a.shape[:-1]
    return jnp.swapaxes(a.reshape(*lead, FF_TILES, 2, FF_TILE), -3, -2).reshape(*lead, 2 * D_FF)


def _ffn_specs(t, tm):
    wide = 2 * FF_TILE
    tile = pl.BlockSpec((tm, wide), lambda j, i: (i, j))
    before = pl.BlockSpec((HALO, wide), lambda j, i: (jnp.maximum(i * (tm // HALO) - 1, 0), j))
    after = pl.BlockSpec((HALO, wide), lambda j, i: (jnp.minimum((i + 1) * (tm // HALO), t // HALO - 1), j))
    vec = lambda rows: pl.BlockSpec((rows, wide), lambda j, i: (0, j))
    return tile, before, after, vec


def _conv(x, w, b, rows):
    taps = [pltpu.roll(x, 2, 0)[HALO : HALO + rows], pltpu.roll(x, 1, 0)[HALO : HALO + rows], x[HALO : HALO + rows]]
    return b + w[0:1] * taps[0] + w[1:2] * taps[1] + w[2:3] * taps[2], taps


def _ffn_act_fwd(u, conv_w, conv_b):
    t = u.shape[0]
    tm = _tile(t, (128,))
    tile, before, _, vec = _ffn_specs(t, tm)

    def body(u_ref, h_ref, w_ref, b_ref, z_ref):
        first = pl.program_id(1) == 0
        x = jnp.concatenate([jnp.where(first, 0.0, h_ref[...].astype(F32)), u_ref[...].astype(F32)], axis=0)
        c, _ = _conv(x, w_ref[...], b_ref[...], tm)
        gate, val = c[:, :FF_TILE], c[:, FF_TILE:]
        z_ref[...] = (gate * _sigmoid(gate) * val).astype(BF16)

    return pl.pallas_call(
        body,
        name="ffn_act_fwd",
        grid=(FF_TILES, t // tm),
        in_specs=[tile, before, vec(3), vec(1)],
        out_specs=pl.BlockSpec((tm, FF_TILE), lambda j, i: (i, j)),
        out_shape=jax.ShapeDtypeStruct((t, D_FF), BF16),
        compiler_params=_params("parallel", "parallel"),
    )(u, u, conv_w, conv_b)


def _ffn_act_bwd(u, dz, conv_w, conv_b, dep=None):
    t = u.shape[0]
    tm = _tile(t, (128,))
    nt = t // tm
    ext = tm + HALO
    tile, before, after, vec = _ffn_specs(t, tm)

    def body(u_ref, ub_ref, ua_ref, w_ref, b_ref, dz_ref, dza_ref, du_ref, dw_ref, db_ref):
        i = pl.program_id(1)
        first, last = i == 0, i == nt - 1
        parts = [jnp.where(first, 0.0, ub_ref[...].astype(F32)), u_ref[...].astype(F32), jnp.where(last, 0.0, ua_ref[...].astype(F32))]
        w = w_ref[...]
        c, taps = _conv(jnp.concatenate(parts, axis=0), w, b_ref[...], ext)
        gate, val = c[:, :FF_TILE], c[:, FF_TILE:]
        dz = jnp.concatenate([dz_ref[...].astype(F32), jnp.where(last, 0.0, dza_ref[...].astype(F32))], axis=0)
        sg = _sigmoid(gate)
        d = jnp.concatenate([dz * val * (sg * (1.0 + gate * (1.0 - sg))), dz * (gate * sg)], axis=1)
        du = w[2:3] * d[:tm] + w[1:2] * pltpu.roll(d, ext - 1, 0)[:tm] + w[0:1] * pltpu.roll(d, ext - 2, 0)[:tm]
        du_ref[...] = du.astype(BF16)

        @pl.when(first)
        def _():
            dw_ref[...] = jnp.zeros_like(dw_ref)
            db_ref[...] = jnp.zeros_like(db_ref)

        for k, tap in enumerate(taps):
            dw_ref[k : k + 1, :] += jnp.sum(d[:tm] * tap[:tm], axis=0, keepdims=True)
        db_ref[...] += jnp.sum(d[:tm], axis=0, keepdims=True)

    narrow = lambda rows, index: pl.BlockSpec((rows, FF_TILE), index)
    return _pallas(
        body,
        7,
        dep,
        name="ffn_act_bwd",
        grid=(FF_TILES, nt),
        in_specs=[
            tile, before, after, vec(3), vec(1),
            narrow(tm, lambda j, i: (i, j)),
            narrow(HALO, lambda j, i: (jnp.minimum((i + 1) * (tm // HALO), t // HALO - 1), j)),
        ],
        out_specs=[tile, vec(3), vec(1)],
        out_shape=[
            jax.ShapeDtypeStruct((t, 2 * D_FF), BF16),
            jax.ShapeDtypeStruct((3, 2 * D_FF), F32),
            jax.ShapeDtypeStruct((1, 2 * D_FF), F32),
        ],
        compiler_params=_params("parallel", "arbitrary"),
    )(u, u, u, conv_w, conv_b, dz, dz)


def _ple_loss(gpre, pp, h2, final_norm, target):
    t, d = h2.shape
    tm = _tile(t, (256,))

    def body(gp_ref, pp_ref, h_ref, g_ref, tg_ref, dh_ref, dgp_ref, dpp_ref, dg_ref, loss_ref):
        i = pl.program_id(0)
        gate = _sigmoid(gp_ref[...])
        ppv = pp_ref[...]
        h3 = h_ref[...] + gate * ppv
        r = lax.rsqrt(jnp.mean(h3 * h3, axis=-1, keepdims=True) + EPS)
        n = h3 * r
        g = g_ref[...]
        err = n * g - tg_ref[...]
        loss = 0.5 * jnp.sum(jnp.mean(err * err, axis=-1, keepdims=True))
        dy = err * (1.0 / d)
        dn = dy * g
        dh = r * (dn - n * jnp.mean(dn * n, axis=-1, keepdims=True))
        dh_ref[...] = dh
        dgp_ref[...] = (dh * ppv * gate * (1.0 - gate)).astype(BF16)
        dpp_ref[...] = (dh * gate).astype(BF16)
        dg = jnp.sum(dy * n, axis=0, keepdims=True)

        @pl.when(i == 0)
        def _():
            dg_ref[...] = dg
            loss_ref[...] = jnp.full(loss_ref.shape, loss, F32)

        @pl.when(i > 0)
        def _():
            dg_ref[...] += dg
            loss_ref[...] += loss

    row = pl.BlockSpec((tm, d), lambda i: (i, 0))
    vec = pl.BlockSpec((1, d), lambda i: (0, 0))
    return pl.pallas_call(
        body,
        name="ple_loss",
        grid=(t // tm,),
        in_specs=[row, row, row, vec, row],
        out_specs=[row, row, row, vec, pl.BlockSpec((8, 128), lambda i: (0, 0))],
        out_shape=[
            jax.ShapeDtypeStruct((t, d), F32),
            jax.ShapeDtypeStruct((t, d), BF16),
            jax.ShapeDtypeStruct((t, d), BF16),
            jax.ShapeDtypeStruct((1, d), F32),
            jax.ShapeDtypeStruct((8, 128), F32),
        ],
        compiler_params=_params("arbitrary"),
    )(gpre, pp, h2, final_norm, target)


def _adamw(w, g, m, v):
    m = ADAM_B1 * m + (1.0 - ADAM_B1) * g
    v = ADAM_B2 * v + (1.0 - ADAM_B2) * (g * g)
    m_hat = m / (1.0 - ADAM_B1 ** ADAM_STEP)
    v_hat = v / (1.0 - ADAM_B2 ** ADAM_STEP)
    return -ADAM_LR * (m_hat / (jnp.sqrt(v_hat) + ADAM_EPS) + ADAM_WD * w), m, v


def _adam_big(w, m, v, own, recv, name):
    r, c = w.shape
    tr = _tile(r, (256, 176))

    def body(w_ref, m_ref, v_ref, own_ref, recv_ref, g_ref, d_ref, nm_ref, nv_ref):
        g = own_ref[...]
        for k in range(3):
            g = g + recv_ref[k].astype(F32)
        g_ref[...] = g
        d_ref[...], nm_ref[...], nv_ref[...] = _adamw(w_ref[...], g, m_ref[...], v_ref[...])

    blk = pl.BlockSpec((tr, c), lambda i: (i, 0))
    return pl.pallas_call(
        body,
        name=name,
        grid=(r // tr,),
        in_specs=[blk, blk, blk, blk, pl.BlockSpec((3, tr, c), lambda i: (0, i, 0))],
        out_specs=[blk] * 4,
        out_shape=[jax.ShapeDtypeStruct((r, c), F32)] * 4,
        compiler_params=_params("parallel"),
    )(w, m, v, own, recv)


def _adam_small(w, g, m, v):
    def body(w_ref, g_ref, m_ref, v_ref, d_ref, nm_ref, nv_ref):
        d_ref[...], nm_ref[...], nv_ref[...] = _adamw(w_ref[...], g_ref[...], m_ref[...], v_ref[...])

    return pl.pallas_call(body, name="adam_small", out_shape=[jax.ShapeDtypeStruct(w.shape, F32)] * 3, compiler_params=_params())(w, g, m, v)


def _cast_bf16(w, name):
    r, c = w.shape
    tr = _tile(r, (256, 176))

    def body(w_ref, o_ref):
        o_ref[...] = w_ref[...].astype(BF16)

    blk = pl.BlockSpec((tr, c), lambda i: (i, 0))
    return pl.pallas_call(
        body, name=name, grid=(r // tr,), in_specs=[blk], out_specs=blk, out_shape=jax.ShapeDtypeStruct((r, c), BF16), compiler_params=_params("parallel")
    )(w)


def _position():
    return lax.axis_index("x"), lax.axis_index("y"), lax.axis_index("c")


def _other_chips(x, y):
    return [(1 - x, y), (x, 1 - y), (1 - x, 1 - y)]


def _block_index(dev, interleaved):
    x, y, c = dev
    return 4 * y + 2 * c + x if interleaved else 4 * x + 2 * y + c


def _shard_of(ref, axis, size, dev, interleaved=False):
    start = pl.multiple_of(_block_index(dev, interleaved) * size, 128 if axis == 1 else 16)
    return ref.at[:, pl.ds(start, size)] if axis == 1 else ref.at[pl.ds(start, size), :]


def _all_gather(shards, axes, interleaved):
    n = len(shards)

    def body(*refs):
        ins, outs = refs[:n], refs[n : 2 * n]
        send_sems, recv_sems, local_sems = refs[2 * n :]
        x, y, c = _position()
        me, sibling = (x, y, c), (x, y, 1 - c)
        chips = _other_chips(x, y)
        firsts, passed, locals_ = [], [], []
        for w in range(n):
            size = shards[w].shape[axes[w]]
            slot = functools.partial(_shard_of, outs[w], axes[w], size, interleaved=interleaved[w])

            def copy(k, block, to, src=None, w=w, slot=slot):
                return pltpu.make_async_remote_copy(
                    src_ref=slot(block) if src is None else src,
                    dst_ref=slot(block),
                    send_sem=send_sems.at[7 * w + k],
                    recv_sem=recv_sems.at[7 * w + k],
                    device_id=to,
                    device_id_type=MESH,
                )

            mine = pltpu.make_async_copy(ins[w], slot(me), local_sems.at[w])
            mine.start()
            locals_.append(mine)
            first = [copy(0, me, sibling, src=ins[w])] + [copy(1 + j, me, (*chip, c), src=ins[w]) for j, chip in enumerate(chips)]
            for cp in first:
                cp.start()
            firsts.append((first, copy))
        for w in range(n):
            first, copy = firsts[w]
            fwd = [copy(4 + j, (*chip, c), sibling) for j, chip in enumerate(chips)]
            for j, chip in enumerate(chips):
                copy(1 + j, (*chip, c), me).wait_recv()
                fwd[j].start()
            passed.append(fwd)
        for w in range(n):
            first, copy = firsts[w]
            copy(0, sibling, me).wait_recv()
            for j, chip in enumerate(chips):
                copy(4 + j, (*chip, 1 - c), me).wait_recv()
            for cp in first + passed[w]:
                cp.wait_send()
            locals_[w].wait()

    def full(s, ax):
        shape = list(s.shape)
        shape[ax] *= N_DEV
        return jax.ShapeDtypeStruct(tuple(shape), s.dtype)

    return pl.pallas_call(
        body,
        name="all_gather_weights",
        in_specs=[ANY] * n,
        out_specs=[ANY] * n,
        out_shape=[full(s, ax) for s, ax in zip(shards, axes)],
        scratch_shapes=[pltpu.SemaphoreType.DMA((7 * n,)), pltpu.SemaphoreType.DMA((7 * n,)), pltpu.SemaphoreType.DMA((n,))],
    )(*shards)


def _add_blocks(ids, grad, landed, axis, size, targets, out_dtype, name):
    rows = size if axis == 0 else grad.shape[0]
    cols = size if axis == 1 else grad.shape[1]
    tr = _tile(rows, (256, 176))
    nr = rows // tr
    nt = len(targets)

    def body(ids_ref, g_ref, l_ref, o_ref):
        o_ref[...] = (g_ref[...] + l_ref[...]).astype(out_dtype)

    if axis == 1:
        g_spec = pl.BlockSpec((tr, cols), lambda k, i, ids: (i, ids[targets[0] + k]))
    else:
        g_spec = pl.BlockSpec((tr, cols), lambda k, i, ids: (ids[targets[0] + k] * nr + i, 0))
    return pl.pallas_call(
        body,
        name=name,
        grid_spec=pltpu.PrefetchScalarGridSpec(
            num_scalar_prefetch=1,
            grid=(nt, nr),
            in_specs=[g_spec, pl.BlockSpec((None, tr, cols), lambda k, i, ids: (ids[4 + targets[0] + k], i, 0))],
            out_specs=pl.BlockSpec((None, tr, cols), lambda k, i, ids: (k, i, 0)),
        ),
        out_shape=jax.ShapeDtypeStruct((nt, rows, cols), out_dtype),
        compiler_params=_params("parallel", "parallel"),
    )(ids, grad, landed)


def _all_reduce_small(vec):
    rows = vec.shape[0]

    def body(v_ref, o_ref, land, send_sems, recv_sems):
        x, y, c = _position()
        mine = 4 * x + 2 * y + c
        copies = []
        for mask in range(1, N_DEV):
            peer = (1 - x if mask & 4 else x, 1 - y if mask & 2 else y, 1 - c if mask & 1 else c)
            copies.append(
                pltpu.make_async_remote_copy(
                    src_ref=v_ref, dst_ref=land.at[mine], send_sem=send_sems.at[mask - 1], recv_sem=recv_sems.at[mask - 1], device_id=peer, device_id_type=MESH
                )
            )
        for cp in copies:
            cp.start()
        land[mine] = v_ref[...]
        for cp in copies:
            cp.wait()
        acc = land[0]
        for k in range(1, N_DEV):
            acc = acc + land[k]
        o_ref[...] = acc

    return pl.pallas_call(
        body,
        name="all_reduce_small",
        out_shape=jax.ShapeDtypeStruct(vec.shape, F32),
        in_specs=[pl.BlockSpec(memory_space=pltpu.VMEM)],
        out_specs=pl.BlockSpec(memory_space=pltpu.VMEM),
        scratch_shapes=[pltpu.VMEM((N_DEV, rows, 128), F32), pltpu.SemaphoreType.DMA((N_DEV - 1,)), pltpu.SemaphoreType.DMA((N_DEV - 1,))],
    )(vec)


def _rows128(a, rows):
    flat = a.reshape(-1)
    return jnp.pad(flat, (0, rows * 128 - flat.shape[0])).reshape(rows, 128)


def _pad_rel(a):
    return jnp.pad(a.reshape(ATT_HEADS, -1)[:, :N_REL], ((0, 0), (0, N_REL_PAD - N_REL)))


SMALL = [("norm_mix", 16), ("lb_logits", 16), ("hg_norm", 8), ("rel_bias", 24), ("norm_ffn", 16), ("conv_b", 88), ("norm_ple", 16), ("final_norm", 16)]
CONV_W_FULL_ROWS = 3 * 2 * D_FF // 128
CONV_W_SHARD_ROWS = 40


def _pack_small(parts):
    return jnp.concatenate([_rows128(_pad_rel(parts[k]) if k == "rel_bias" else parts[k], rows) for k, rows in SMALL], axis=0)


def _unpack_small(packed, shapes):
    out, at = {}, 0
    for k, rows in SMALL:
        blk = packed[at : at + rows]
        at += rows
        if k == "rel_bias":
            out[k] = blk.reshape(ATT_HEADS, N_REL_PAD)[:, :N_REL].reshape(shapes[k])
        else:
            n = 1
            for s in shapes[k]:
                n *= s
            out[k] = blk.reshape(-1)[:n].reshape(shapes[k])
    return out, at


BIG = [("w_in", 1), ("w_out", 0), ("w_up", 1), ("w_down", 0), ("w_ple_gate", 0), ("w_ple_proj", 1)]


HBM = pl.BlockSpec(memory_space=pltpu.HBM)
SEM = pl.BlockSpec(memory_space=pltpu.SEMAPHORE)
EFFECT = pltpu.SideEffectType.DATAFLOW_SIDE_EFFECTING


def _copies(plan, refs, send_sems, recv_sems):
    return [
        pltpu.make_async_remote_copy(src_ref=src, dst_ref=dst, send_sem=send_sems.at[i], recv_sem=recv_sems.at[i], device_id=dev, device_id_type=MESH)
        for i, (src, dst, dev) in enumerate(plan(refs))
    ]


def _split_start(name, arrays, plan, n):
    k = len(arrays)

    def body(*refs):
        for cp in _copies(plan, refs[:k], refs[k], refs[k + 1]):
            cp.start()
        refs[-1][...] = jnp.zeros_like(refs[-1])

    out = pl.pallas_call(
        body,
        name=name,
        out_shape=(pltpu.SemaphoreType.DMA((n,)), pltpu.SemaphoreType.DMA((n,)), *[pltpu.HBM(a.shape, a.dtype) for a in arrays], jax.ShapeDtypeStruct((8, 128), F32)),
        in_specs=[HBM] * k,
        out_specs=(SEM, SEM, *[HBM] * k, pl.BlockSpec(memory_space=pltpu.VMEM)),
        input_output_aliases={i: 2 + i for i in range(k)},
        compiler_params=pltpu.CompilerParams(has_side_effects=EFFECT),
    )(*[pltpu.with_memory_space_constraint(a, pltpu.HBM) for a in arrays])
    return out[0], out[1], list(out[2 : 2 + k]), out[-1]


def _split_wait(name, send, recv, arrays, plan, after):
    k = len(arrays)

    def body(*refs):
        for cp in _copies(plan, refs[:k], refs[k], refs[k + 1]):
            cp.wait_send()
            cp.wait_recv()

    out = pl.pallas_call(
        body,
        name=name,
        out_shape=tuple(pltpu.HBM(a.shape, a.dtype) for a in arrays),
        in_specs=[HBM] * k + [SEM, SEM, ANY],
        out_specs=tuple([HBM] * k),
        input_output_aliases={i: i for i in range(k)},
        compiler_params=pltpu.CompilerParams(has_side_effects=EFFECT),
    )(*arrays, send, recv, after)
    return list(out)


def _cast_into(w, me, axis, name, dep):
    r, c = w.shape
    tr = _tile(r, (256, 176))
    nr = r // tr

    def body(me_ref, w_ref, dep_ref, o_ref):
        o_ref[...] = w_ref[...].astype(BF16)

    if axis == 1:
        shape, o_spec = (r, N_DEV * c), pl.BlockSpec((tr, c), lambda i, me: (i, me[0]))
    else:
        shape, o_spec = (N_DEV * r, c), pl.BlockSpec((tr, c), lambda i, me: (me[0] * nr + i, 0))
    return pl.pallas_call(
        body,
        name=name,
        grid_spec=pltpu.PrefetchScalarGridSpec(
            num_scalar_prefetch=1, grid=(nr,), in_specs=[pl.BlockSpec((tr, c), lambda i, me: (i, 0)), ANY], out_specs=o_spec
        ),
        out_shape=jax.ShapeDtypeStruct(shape, BF16),
        compiler_params=_params("parallel"),
    )(me, w, dep)


GATHER = [
    (["w_out"], "att_fwd", None),
    (["w_up"], "att_fwd", "norm_ffn_fwd"),
    (["w_down", "w_ple_gate", "w_ple_proj"], "norm_ffn_fwd", "ffn_act_fwd"),
]
GROUPS = [["w_ple_proj", "w_ple_gate", "w_down"], ["w_up"], ["w_out"], ["w_in"]]
STAGES = ["ffn_act_bwd", "d_mix_out", "hgrn_bwd", "d_norm_mix_out"]
INTERLEAVED = {"w_up"}


class _Exchange:
    def __init__(self, big, position):
        self.big, self.axis = big, dict(BIG)
        self.size = {k: big[k].shape[self.axis[k]] for k in big}
        self.x, self.y, self.c = position
        chips = [(self.x, self.y)] + _other_chips(self.x, self.y)
        landed = [2 * cx + cy for cx, cy in chips]
        self.ids = {
            flag: jnp.stack([_block_index((cx, cy, self.c), flag) for cx, cy in chips] + landed).astype(jnp.int32) for flag in (False, True)
        }
        self.token, self.grads, self.state, self.wfull = None, {}, {}, {}


    def _slot(self, ref, k, dev):
        return _shard_of(ref, self.axis[k], self.size[k], dev, interleaved=k in INTERLEAVED)

    def _plan_gather(self, names, direct, refs):
        x, y, c = _position()
        me, out = (x, y, c), []
        for k, ref in zip(names, refs):
            mine = self._slot(ref, k, me)
            out.append((mine, mine, (x, y, 1 - c)))
            out += [(mine, mine, (*chip, c)) for chip in _other_chips(x, y)]
            if direct:
                out += [(mine, mine, (*chip, 1 - c)) for chip in _other_chips(x, y)]
        return out

    def _plan_forward(self, names, refs):
        x, y, c = _position()
        out = []
        for k, ref in zip(names, refs):
            for chip in _other_chips(x, y):
                block = self._slot(ref, k, (*chip, c))
                out.append((block, block, (x, y, 1 - c)))
        return out

    def _plan_sibling(self, names, refs):
        x, y, c = _position()
        n = len(names)
        return [(self._slot(refs[i], k, (p // 2, p % 2, 1 - c)), refs[n + i].at[p], (x, y, 1 - c)) for i, k in enumerate(names) for p in range(4)]

    def _plan_chips(self, names, refs):
        x, y, c = _position()
        n = len(names)
        return [(refs[i].at[j], refs[n + i].at[j], (*chip, c)) for i in range(n) for j, chip in enumerate(_other_chips(x, y))]


    def gather(self, conv_w):
        w_in, conv_full = _all_gather([_cast_bf16(self.big["w_in"], "cast_w_in"), conv_w], [1, 1], [False, True])
        self.wfull["w_in"] = w_in
        me = {flag: _block_index((self.x, self.y, self.c), flag).astype(jnp.int32).reshape(1) for flag in (False, True)}
        self.late = {}
        for gi, (names, _, forwarded) in enumerate(GATHER):
            fulls = [_cast_into(self.big[k], me[k in INTERLEAVED], self.axis[k], "cast_" + k, w_in) for k in names]
            plan = functools.partial(self._plan_gather, names, forwarded is None)
            send, recv, fulls, self.token = _split_start(f"gather_start_{gi}", fulls, plan, (7 if forwarded is None else 4) * len(names))
            self.late[gi] = (send, recv, fulls, plan)
        return conv_full

    def weight(self, k):
        return self.wfull[k]

    def dep(self):
        token, self.token = self.token, None
        return token

    def grad(self, k, g):
        self.grads[k] = g
        for gi, names in enumerate(GROUPS):
            if k == names[-1]:
                plan = functools.partial(self._plan_sibling, names)
                lands = [lax.empty((4, *self._shard_shape(n)), F32) for n in names]
                send, recv, arrays, self.token = _split_start(f"sibling_start_{gi}", [self.grads[n] for n in names] + lands, plan, 4 * len(names))
                self.state[gi] = (send, recv, arrays, plan)

    def done(self, stage, after):
        for gi, (names, _, forwarded) in enumerate(GATHER):
            if forwarded == stage:
                send, recv, fulls, plan = self.late[gi]
                self.wfull.update(zip(names, _split_wait(f"forward_wait_{gi}", send, recv, fulls, plan, after)))
        for gi, (names, arrived, forwarded) in enumerate(GATHER):
            if arrived == stage:
                send, recv, fulls, plan = self.late[gi]
                fulls = _split_wait(f"gather_wait_{gi}", send, recv, fulls, plan, after)
                if forwarded is None:
                    self.wfull.update(zip(names, fulls))
                else:
                    plan = functools.partial(self._plan_forward, names)
                    send, recv, fulls, self.token = _split_start(f"forward_start_{gi}", fulls, plan, 3 * len(names))
                    self.late[gi] = (send, recv, fulls, plan)
        if stage in STAGES:
            self._to_chips(STAGES.index(stage), after)

    def _shard_shape(self, k):
        shape = list(self.grads[k].shape)
        shape[self.axis[k]] = self.size[k]
        return tuple(shape)

    def _to_chips(self, gi, after):
        names = GROUPS[gi]
        n = len(names)
        send, recv, arrays, plan = self.state[gi]
        arrays = _split_wait(f"sibling_wait_{gi}", send, recv, arrays, plan, after)
        own, parts = [], []
        for k, g, land in zip(names, arrays[:n], arrays[n:]):
            ids = self.ids[k in INTERLEAVED]
            own.append(_add_blocks(ids, g, land, self.axis[k], self.size[k], [0], F32, "add_own_" + k)[0])
            parts.append(_add_blocks(ids, g, land, self.axis[k], self.size[k], [1, 2, 3], BF16, "add_send_" + k))
        plan = functools.partial(self._plan_chips, names)
        lands = [lax.empty(part.shape, BF16) for part in parts]
        send, recv, arrays, self.token = _split_start(f"chips_start_{gi}", parts + lands, plan, 3 * n)
        self.state[gi] = (send, recv, arrays, plan, own)

    def finish(self, gi, after):
        names = GROUPS[gi]
        send, recv, arrays, plan, own = self.state[gi]
        arrays = _split_wait(f"chips_wait_{gi}", send, recv, arrays, plan, after)
        return {k: (o, r) for k, o, r in zip(names, own, arrays[len(names) :])}


class _Resident:
    def __init__(self, wfull):
        self.wfull, self.grads = wfull, {}

    def weight(self, k):
        return self.wfull[k]

    def grad(self, k, g):
        self.grads[k] = g

    def dep(self):
        return None

    def done(self, stage, after):
        pass


def _local_step(x, p, target, small, conv_w, ex):
    a1, r1 = _rms_fwd(x, small["norm_mix"], "norm_mix_fwd", dep=ex.dep())
    proj = _matmul(a1, ex.weight("w_in"), "nn", F32, "in_proj")
    bias = _bias_table(jnp.pad(small["rel_bias"], ((0, 0), (0, N_REL_PAD - N_REL))))
    y_hg, o_hg, states = _hgrn_fwd(proj, small["lb_logits"], small["hg_norm"])
    y_att = _att_fwd(proj, bias, dep=ex.dep())
    ex.done("att_fwd", y_att)
    ycat = jnp.concatenate([y_hg, y_att], axis=1)
    h1 = _matmul(ycat, ex.weight("w_out"), "nn", F32, "out_proj", resid=x, dep=ex.dep())
    a2, r2 = _rms_fwd(h1, small["norm_ffn"], "norm_ffn_fwd")
    ex.done("norm_ffn_fwd", a2)
    u = _matmul(a2, ex.weight("w_up"), "nn", BF16, "up_proj")
    conv_b = _interleave_cols(small["conv_b"])
    z = _ffn_act_fwd(u, conv_w, conv_b)
    ex.done("ffn_act_fwd", z)
    h2 = _matmul(z, ex.weight("w_down"), "nn", F32, "down_proj", tk=2816, resid=h1)
    a3, r3 = _rms_fwd(h2, small["norm_ple"], "norm_ple_fwd")
    gpre = _matmul(a3, ex.weight("w_ple_gate"), "nn", F32, "ple_gate")
    pp = _matmul(p, ex.weight("w_ple_proj"), "nn", F32, "ple_proj")
    dh3, dgpre, dpp, d_final, loss = _ple_loss(gpre, pp, h2, small["final_norm"], target)

    ex.grad("w_ple_proj", _matmul(p, dpp, "tn", F32, "d_w_ple_proj", tk=2048))
    ex.grad("w_ple_gate", _matmul(a3, dgpre, "tn", F32, "d_w_ple_gate", tk=2048))
    da3 = _matmul(dgpre, ex.weight("w_ple_gate"), "nt", F32, "d_norm_ple_out")
    dh2, d_ple = _rms_bwd(da3, h2, r3, small["norm_ple"], dh3, "norm_ple_bwd")
    dz = _matmul(dh2, ex.weight("w_down"), "nt", BF16, "d_ffn_act")
    ex.grad("w_down", _matmul(z, dh2, "tn", F32, "d_w_down", tk=2048))
    du, dcw, dcb = _ffn_act_bwd(u, dz, conv_w, conv_b, dep=ex.dep())
    ex.done("ffn_act_bwd", du)
    d_conv_w, d_conv_b = _deinterleave_cols(dcw), _deinterleave_cols(dcb)
    ex.grad("w_up", _matmul(a2, du, "tn", F32, "d_w_up", tk=2048, dep=ex.dep()))
    da2 = _matmul(du, ex.weight("w_up"), "nt", F32, "d_norm_ffn_out", tk=2816, dep=ex.dep())
    dh1, d_ffn = _rms_bwd(da2, h1, r2, small["norm_ffn"], dh2, "norm_ffn_bwd")
    dycat = _matmul(dh1, ex.weight("w_out"), "nt", F32, "d_mix_out")
    ex.done("d_mix_out", dycat)
    ex.grad("w_out", _matmul(ycat, dh1, "tn", F32, "d_w_out", tk=2048, dep=ex.dep()))
    dp_hg, d_lb, d_hgn = _hgrn_bwd(proj, small["lb_logits"], small["hg_norm"], o_hg, dycat, states, dep=ex.dep())
    ex.done("hgrn_bwd", d_lb)
    dq_att, dk_att, dv_att, gsum = _att_bwd(proj, bias, dycat, dep=ex.dep())
    d_rel = _rel_bias_grad(gsum)
    dproj = jnp.concatenate([dp_hg[0], dp_hg[1], dp_hg[2], dp_hg[3], dq_att, dk_att, dv_att], axis=1)
    ex.grad("w_in", _matmul(a1, dproj, "tn", F32, "d_w_in", tk=2048))
    da1 = _matmul(dproj, ex.weight("w_in"), "nt", F32, "d_norm_mix_out", tk=1792, dep=ex.dep())
    ex.done("d_norm_mix_out", da1)
    dx, d_mix = _rms_bwd(da1, x, r1, small["norm_mix"], dh1, "norm_mix_bwd", dep=ex.dep())
    d_small = {
        "norm_mix": d_mix, "lb_logits": d_lb, "hg_norm": d_hgn, "rel_bias": d_rel, "norm_ffn": d_ffn,
        "conv_b": d_conv_b, "norm_ple": d_ple, "final_norm": d_final,
    }
    return loss, dx, d_small, d_conv_w


def kernel(x, p, norm_mix, w_in, lb_logits, hg_norm, rel_bias, w_out, norm_ffn, w_up, conv_w, conv_b, w_down, norm_ple, w_ple_gate, w_ple_proj, final_norm, loss_target, m_norm_mix, m_w_in, m_lb_logits, m_hg_norm, m_rel_bias, m_w_out, m_norm_ffn, m_w_up, m_conv_w, m_conv_b, m_w_down, m_norm_ple, m_w_ple_gate, m_w_ple_proj, m_final_norm, v_norm_mix, v_w_in, v_lb_logits, v_hg_norm, v_rel_bias, v_w_out, v_norm_ffn, v_w_up, v_conv_w, v_conv_b, v_w_down, v_norm_ple, v_w_ple_gate, v_w_ple_proj, v_final_norm):
    given = dict(locals())
    mx, my, mc = _position()
    me = 4 * mx + 2 * my + mc
    big = {k: given[k][0] for k, _ in BIG}
    ex = _Exchange(big, (mx, my, mc))
    conv_w_full = ex.gather(conv_w[0])

    small = {
        "norm_mix": norm_mix, "lb_logits": lb_logits, "hg_norm": hg_norm, "rel_bias": rel_bias[0], "norm_ffn": norm_ffn,
        "conv_b": conv_b, "norm_ple": norm_ple, "final_norm": final_norm.reshape(1, -1),
    }
    loss, dx, d_small, d_conv_w = _local_step(x[0], p[0, 0], loss_target[0], small, conv_w_full, ex)

    packed = jnp.concatenate([_pack_small(d_small), _rows128(d_conv_w, CONV_W_FULL_ROWS), _rows128(loss[0:1, 0:1], 8)], axis=0)
    reduced = _all_reduce_small(packed)

    out = {}
    for gi in range(len(GROUPS)):
        for k, (o, r) in ex.finish(gi, reduced).items():
            g, d, nm, nv = _adam_big(big[k], given["m_" + k][0], given["v_" + k][0], o, r, "adam_" + k)
            out[k] = tuple(a[None] for a in (g, d, nm, nv))
    shapes = {k: given[k].shape for k, _ in SMALL}
    g_small, at = _unpack_small(reduced, shapes)
    g_conv_full = reduced[at : at + CONV_W_FULL_ROWS].reshape(3, 2 * D_FF)
    total_loss = reduced[at + CONV_W_FULL_ROWS, 0]
    cw = conv_w.shape[2]
    g_conv = lax.dynamic_slice_in_dim(g_conv_full, me * cw, cw, axis=1)

    def pack_with_conv(parts, conv_part):
        return jnp.concatenate([_pack_small(parts), _rows128(conv_part, CONV_W_SHARD_ROWS)], axis=0)

    d_pk, m_pk, v_pk = _adam_small(
        pack_with_conv({k: given[k] for k, _ in SMALL}, conv_w),
        pack_with_conv(g_small, g_conv),
        pack_with_conv({k: given["m_" + k] for k, _ in SMALL}, m_conv_w),
        pack_with_conv({k: given["v_" + k] for k, _ in SMALL}, v_conv_w),
    )
    for name, pk in (("d", d_pk), ("m", m_pk), ("v", v_pk)):
        parts, at = _unpack_small(pk, shapes)
        parts["conv_w"] = pk[at : at + CONV_W_SHARD_ROWS].reshape(-1)[: 3 * cw].reshape(conv_w.shape)
        for k, a in parts.items():
            out.setdefault(k, {})
            out[k][name] = a
    for k, _ in SMALL:
        out[k]["g"] = g_small[k]
    out["conv_w"]["g"] = g_conv.reshape(conv_w.shape)

    order = ["norm_mix", "w_in", "lb_logits", "hg_norm", "rel_bias", "w_out", "norm_ffn", "w_up", "conv_w", "conv_b", "w_down", "norm_ple", "w_ple_gate", "w_ple_proj", "final_norm"]

    def pick(k, what):
        return out[k][what] if isinstance(out[k], dict) else out[k][{"g": 0, "d": 1, "m": 2, "v": 3}[what]]

    return (total_loss, dx[None], *[pick(k, "g") for k in order], *[pick(k, "d") for k in order], *[pick(k, "m") for k in order], *[pick(k, "v") for k in order])
```

```python
import functools

import jax
import jax.numpy as jnp
from jax import lax
from jax.experimental import pallas as pl
from jax.experimental.pallas import tpu as pltpu

F32 = jnp.float32
BF16 = jnp.bfloat16

D_MODEL = 2048
CHUNK = 64
HG_HEADS = 8
HEAD_DIM = 128
HG_WIDTH = HG_HEADS * HEAD_DIM
ATT_HEADS = 8
ATT_WIDTH = ATT_HEADS * HEAD_DIM
LEFT_CHUNKS = 8
PAD = LEFT_CHUNKS * CHUNK
BAND = PAD + CHUNK
REL_CLIP = 128
N_REL = 2 * REL_CLIP + 1
N_REL_PAD = 384
D_FF = 5632
EPS = 1e-6
ATT_SCALE = HEAD_DIM ** -0.5
SUB = 32
HG_BLOCK = 8
Q_BLOCK = 4 * CHUNK
K_BLOCK = Q_BLOCK + PAD
DIAG = 1024

ADAM_LR = 0.001
ADAM_B1 = 0.9
ADAM_B2 = 0.999
ADAM_EPS = 1e-08
ADAM_WD = 0.01
ADAM_STEP = 10

N_DEV = 8
VMEM_LIMIT = 48 * 1024 * 1024
MESH = pl.DeviceIdType.MESH
ANY = pl.BlockSpec(memory_space=pl.ANY)
HIGHEST = lax.Precision.HIGHEST

NN = (((1,), (0,)), ((), ()))
NT = (((1,), (1,)), ((), ()))
TN = (((0,), (0,)), ((), ()))


def _params(*sem):
    return pltpu.CompilerParams(dimension_semantics=sem if sem else None, vmem_limit_bytes=VMEM_LIMIT)


def _pallas(body, n_in, dep, **kw):
    if dep is None:
        return pl.pallas_call(body, **kw)

    def body_after(*refs):
        body(*refs[:n_in], *refs[n_in + 1 :])

    call = pl.pallas_call(body_after, **dict(kw, in_specs=list(kw["in_specs"]) + [ANY]))
    return lambda *ops: call(*ops, dep)


def _dot(a, b, dims=NN):
    return lax.dot_general(a, b, dims, preferred_element_type=F32)


def _dot3(a, b, dims=NN):
    a_hi, b_hi = a.astype(BF16), b.astype(BF16)
    a_lo, b_lo = (a - a_hi.astype(F32)).astype(BF16), (b - b_hi.astype(F32)).astype(BF16)
    return _dot(a_hi, b_hi, dims) + (_dot(a_hi, b_lo, dims) + _dot(a_lo, b_hi, dims))


def _sigmoid(x):
    return 1.0 / (1.0 + jnp.exp(-x))


def _tile(n, prefs):
    for t in prefs:
        if n % t == 0:
            return t
    return n


def _matmul(a, b, mode, out_dtype, name, tm=1024, tn=1024, tk=None, resid=None, dep=None):
    if mode == "nn":
        (m, k), n = a.shape, b.shape[1]
    elif mode == "nt":
        (m, k), n = a.shape, b.shape[0]
    else:
        (k, m), n = a.shape, b.shape[1]
    tm = _tile(m, (tm, 512, 256, 128))
    tn = _tile(n, (tn, 1408, 512, 256, 128))
    tk = k if tk is None else _tile(k, (tk,))
    nk = k // tk
    dims = {"nn": NN, "nt": NT, "tn": TN}[mode]
    a_spec = pl.BlockSpec((tk, tm), lambda i, j, s: (s, i)) if mode == "tn" else pl.BlockSpec((tm, tk), lambda i, j, s: (i, s))
    b_spec = pl.BlockSpec((tn, tk), lambda i, j, s: (j, s)) if mode == "nt" else pl.BlockSpec((tk, tn), lambda i, j, s: (s, j))
    o_spec = pl.BlockSpec((tm, tn), lambda i, j, s: (i, j))
    has_res = resid is not None

    def body(*refs):
        a_ref, b_ref = refs[0], refs[1]
        o_ref = refs[2 + has_res]
        part = _dot(a_ref[...].astype(BF16), b_ref[...].astype(BF16), dims)

        def finish(acc):
            if has_res:
                acc = acc + refs[2][...]
            o_ref[...] = acc.astype(out_dtype)

        if nk == 1:
            finish(part)
        else:
            acc_ref = refs[-1]
            s = pl.program_id(2)

            @pl.when(s == 0)
            def _():
                acc_ref[...] = part

            @pl.when(s > 0)
            def _():
                acc_ref[...] += part

            @pl.when(s == nk - 1)
            def _():
                finish(acc_ref[...])

    return _pallas(
        body,
        2 + has_res,
        dep,
        name=name,
        grid=(m // tm, n // tn, nk),
        in_specs=[a_spec, b_spec] + ([o_spec] if has_res else []),
        out_specs=o_spec,
        out_shape=jax.ShapeDtypeStruct((m, n), out_dtype),
        scratch_shapes=[pltpu.VMEM((tm, tn), F32)] if nk > 1 else [],
        compiler_params=_params("parallel", "parallel", "arbitrary"),
    )(*([a, b] + ([resid] if has_res else [])))


def _rms_fwd(x, g, name, dep=None):
    t, d = x.shape
    tm = _tile(t, (256,))

    def body(x_ref, g_ref, a_ref, r_ref):
        xv = x_ref[...]
        r = lax.rsqrt(jnp.mean(xv * xv, axis=-1, keepdims=True) + EPS)
        a_ref[...] = (xv * r * g_ref[...]).astype(BF16)
        r_ref[...] = r

    row = pl.BlockSpec((tm, d), lambda i: (i, 0))
    return _pallas(
        body,
        2,
        dep,
        name=name,
        grid=(t // tm,),
        in_specs=[row, pl.BlockSpec((1, d), lambda i: (0, 0))],
        out_specs=[row, pl.BlockSpec((tm, 1), lambda i: (i, 0))],
        out_shape=[jax.ShapeDtypeStruct((t, d), BF16), jax.ShapeDtypeStruct((t, 1), F32)],
        compiler_params=_params("parallel"),
    )(x, g)


def _rms_bwd(da, x, r, g, resid, name, dep=None):
    t, d = x.shape
    tm = _tile(t, (256,))

    def body(da_ref, x_ref, r_ref, g_ref, res_ref, dx_ref, dg_ref):
        i = pl.program_id(0)
        rv = r_ref[...]
        n = x_ref[...] * rv
        dav = da_ref[...]
        dn = dav * g_ref[...]
        dx_ref[...] = rv * (dn - n * jnp.mean(dn * n, axis=-1, keepdims=True)) + res_ref[...]
        part = jnp.sum(dav * n, axis=0, keepdims=True)

        @pl.when(i == 0)
        def _():
            dg_ref[...] = part

        @pl.when(i > 0)
        def _():
            dg_ref[...] += part

    row = pl.BlockSpec((tm, d), lambda i: (i, 0))
    vec = pl.BlockSpec((1, d), lambda i: (0, 0))
    return _pallas(
        body,
        5,
        dep,
        name=name,
        grid=(t // tm,),
        in_specs=[row, row, pl.BlockSpec((tm, 1), lambda i: (i, 0)), vec, row],
        out_specs=[row, vec],
        out_shape=[jax.ShapeDtypeStruct((t, d), F32), jax.ShapeDtypeStruct((1, d), F32)],
        compiler_params=_params("arbitrary"),
    )(da, x, r, g, resid)


def _tri(n, upper):
    r = lax.broadcasted_iota(jnp.int32, (n, n), 0)
    c = lax.broadcasted_iota(jnp.int32, (n, n), 1)
    return jnp.where((c >= r) if upper else (c <= r), 1.0, 0.0).astype(F32)


def _hgrn_gates(q, fp, lbl):
    l0, l1 = lbl[0:1, :], lbl[1:2, :]
    mx = jnp.maximum(l0, l1)
    e0, e1 = jnp.exp(l0 - mx), jnp.exp(l1 - mx)
    lb = e0 / (e0 + e1)
    sig = _sigmoid(fp)
    f = lb + (1.0 - lb) * sig
    kk = (1.0 - lb) * _sigmoid(-fp)
    sq = _sigmoid(q)
    b = jnp.dot(_tri(CHUNK, False), jnp.log(f), precision=HIGHEST, preferred_element_type=F32)
    return lb, sig, f, kk, sq, q * sq, b


def _heads(x):
    return [x[:, j * HEAD_DIM : (j + 1) * HEAD_DIM] for j in range(x.shape[1] // HEAD_DIM)]


def _wide(parts):
    return jnp.concatenate(parts, axis=1)


def _intra_blocks(b):
    out = []
    for lo in range(0, CHUNK, SUB):
        hi = lo + SUB
        br = b[lo + SUB // 2 : lo + SUB // 2 + 1, :]
        row = lax.broadcasted_iota(jnp.int32, (SUB, hi), 0) + lo
        col = lax.broadcasted_iota(jnp.int32, (SUB, hi), 1)
        out.append((lo, hi, jnp.exp(b[lo:hi] - br), jnp.exp(br - b[:hi]), col <= row))
    return out


def _hgrn_fwd(proj, lb_logits, hg_norm):
    t = proj.shape[0]
    nc = t // CHUNK

    def body(q_ref, f_ref, i_ref, g_ref, lbl_ref, hgn_ref, y_ref, o_ref, st_ref, s_scr):
        c = pl.program_id(1)

        @pl.when(c == 0)
        def _():
            s_scr[...] = jnp.zeros_like(s_scr)

        hs = range(HG_BLOCK)
        sts = [s_scr[j] for j in hs]
        _, _, _, kk, _, qf, b = _hgrn_gates(q_ref[...], f_ref[...], lbl_ref[...])
        vb = _heads(i_ref[...].astype(BF16))
        bl = b[CHUNK - 1 : CHUNK, :]
        qe = _heads((qf * jnp.exp(b)).astype(BF16))
        kd = _heads((kk * jnp.exp(bl - b)).astype(BF16))
        decay = _heads(jnp.exp(bl))
        o = [_dot(qe[j], sts[j].astype(BF16), NT) for j in hs]
        parts = [[] for _ in hs]
        for lo, hi, ea, eb, mask in _intra_blocks(b):
            a, bk = _heads((qf[lo:hi] * ea).astype(BF16)), _heads((kk[:hi] * eb).astype(BF16))
            p = [jnp.where(mask, _dot(a[j], bk[j], NT), 0.0).astype(BF16) for j in hs]
            for j in hs:
                parts[j].append(_dot(p[j], vb[j][:hi]))
        o = [o[j] + jnp.concatenate(parts[j], axis=0) for j in hs]
        new = [sts[j] * decay[j] + _dot(vb[j], kd[j], TN) for j in hs]
        hgn = hgn_ref[...]
        on = [o[j] * lax.rsqrt(jnp.mean(o[j] * o[j], axis=-1, keepdims=True) + EPS) * hgn for j in hs]
        gg = g_ref[...]
        for j in hs:
            st_ref[j] = sts[j]
            s_scr[j] = new[j]
        o_ref[...] = _wide(o)
        y_ref[...] = (_wide(on) * (gg * _sigmoid(gg))).astype(BF16)

    wide = HG_BLOCK * HEAD_DIM
    groups = HG_HEADS // HG_BLOCK

    def col(k):
        return pl.BlockSpec((CHUNK, wide), lambda g, c: (c, k * groups + g))

    out = pl.BlockSpec((CHUNK, wide), lambda g, c: (c, g))
    return pl.pallas_call(
        body,
        name="hgrn_fwd",
        grid=(groups, nc),
        in_specs=[col(0), col(1), col(2), col(3), pl.BlockSpec((2, wide), lambda g, c: (0, g)), pl.BlockSpec((1, HEAD_DIM), lambda g, c: (0, 0))],
        out_specs=[out, out, pl.BlockSpec((HG_BLOCK, None, HEAD_DIM, HEAD_DIM), lambda g, c: (g, c, 0, 0))],
        out_shape=[
            jax.ShapeDtypeStruct((t, HG_WIDTH), BF16),
            jax.ShapeDtypeStruct((t, HG_WIDTH), F32),
            jax.ShapeDtypeStruct((HG_HEADS, nc, HEAD_DIM, HEAD_DIM), F32),
        ],
        scratch_shapes=[pltpu.VMEM((HG_BLOCK, HEAD_DIM, HEAD_DIM), F32)],
        compiler_params=_params("arbitrary", "arbitrary"),
    )(proj, proj, proj, proj, lb_logits, hg_norm)


def _hgrn_bwd(proj, lb_logits, hg_norm, o_hg, dycat, states, dep=None):
    t = proj.shape[0]
    nc = t // CHUNK

    def body(q_ref, f_ref, i_ref, g_ref, lbl_ref, hgn_ref, o_ref, dy_ref, st_ref, dp_ref, dlbl_ref, dhgn_ref, dst_scr, dlb_scr):
        h = pl.program_id(0)
        c = pl.program_id(1)

        @pl.when(c == 0)
        def _():
            dst_scr[...] = jnp.zeros_like(dst_scr)
            dlb_scr[...] = jnp.zeros_like(dlb_scr)

        @pl.when((c == 0) & (h == 0))
        def _():
            dhgn_ref[...] = jnp.zeros_like(dhgn_ref)

        hs = range(HG_BLOCK)
        hgn = _wide([hgn_ref[...]] * HG_BLOCK)
        q, fp, gg, vi = q_ref[...], f_ref[...], g_ref[...], i_ref[...]
        lb, sig, f, kk, sq, qf, b = _hgrn_gates(q, fp, lbl_ref[...])
        o, dy = o_ref[...], dy_ref[...]
        sg = _sigmoid(gg)
        n = _wide([oh * lax.rsqrt(jnp.mean(oh * oh, axis=-1, keepdims=True) + EPS) for oh in _heads(o)])
        don = dy * (gg * sg)
        dgg = dy * (n * hgn) * (sg * (1.0 + gg * (1.0 - sg)))
        d_hgn = sum(_heads(jnp.sum(don * n, axis=0, keepdims=True)))
        dn = don * hgn
        do = _wide(
            [
                lax.rsqrt(jnp.mean(oh * oh, axis=-1, keepdims=True) + EPS) * (dnh - nh * jnp.mean(dnh * nh, axis=-1, keepdims=True))
                for oh, dnh, nh in zip(_heads(o), _heads(dn), _heads(n))
            ]
        )
        sts = [st_ref[j] for j in hs]
        dstn = [dst_scr[j] for j in hs]
        bl = b[CHUNK - 1 : CHUNK, :]
        e_b, e_bl, e_l = jnp.exp(b), jnp.exp(bl - b), jnp.exp(bl)
        doh, vih = _heads(do), _heads(vi)
        dobh = _heads(do.astype(BF16))
        dq_acc = _wide([_dot3(doh[j], sts[j]) for j in hs]) * e_b
        dk_inter = _wide([_dot3(vih[j], dstn[j]) for j in hs]) * e_bl
        dk_acc = dk_inter
        kd = _heads((kk * e_bl).astype(BF16))
        dv_acc = _wide([_dot(kd[j], dstn[j].astype(BF16), NT) for j in hs])
        qe, decay = _heads((qf * e_b).astype(BF16)), _heads(e_l)
        dst_new = [dstn[j] * decay[j] + _dot(dobh[j], qe[j], TN) for j in hs]
        db_last = e_l * _wide([jnp.sum(sts[j] * dstn[j], axis=0, keepdims=True) for j in hs]) + jnp.sum(kk * dk_inter, axis=0, keepdims=True)
        dq_parts = []
        for lo, hi, ea, eb, mask in _intra_blocks(b):
            a, bk = qf[lo:hi] * ea, kk[:hi] * eb
            ah, bkh = _heads(a), _heads(bk)
            abh, bkbh = _heads(a.astype(BF16)), _heads(bk.astype(BF16))
            p = [jnp.where(mask, _dot(abh[j], bkbh[j], NT), 0.0).astype(BF16) for j in hs]
            dp = [jnp.where(mask, _dot3(doh[j][lo:hi], vih[j][:hi], NT), 0.0) for j in hs]
            dq_parts.append(_wide([_dot3(dp[j], bkh[j]) for j in hs]) * ea)
            dki = _wide([_dot3(dp[j], ah[j], TN) for j in hs]) * eb
            dvi = _wide([_dot(p[j], dobh[j][lo:hi], TN) for j in hs])
            if hi < CHUNK:
                zeros = jnp.zeros((CHUNK - hi, HG_BLOCK * HEAD_DIM), F32)
                dki = jnp.concatenate([dki, zeros], axis=0)
                dvi = jnp.concatenate([dvi, zeros], axis=0)
            dk_acc = dk_acc + dki
            dv_acc = dv_acc + dvi
        dq_acc = dq_acc + jnp.concatenate(dq_parts, axis=0)
        rows = lax.broadcasted_iota(jnp.int32, dq_acc.shape, 0)
        db = qf * dq_acc - kk * dk_acc + jnp.where(rows == CHUNK - 1, db_last, 0.0)
        dlf = jnp.dot(_tri(CHUNK, True), db, precision=HIGHEST, preferred_element_type=F32)
        dfk = dlf / f - dk_acc
        dp_ref[0] = (dq_acc * (sq * (1.0 + q * (1.0 - sq)))).astype(BF16)
        dp_ref[1] = ((1.0 - lb) * dfk * sig * (1.0 - sig)).astype(BF16)
        dp_ref[2] = dv_acc.astype(BF16)
        dp_ref[3] = dgg.astype(BF16)
        dlb_scr[...] += jnp.sum(dfk * (1.0 - sig), axis=0, keepdims=True)
        dhgn_ref[...] += d_hgn
        for j in hs:
            dst_scr[j] = dst_new[j]

        @pl.when(c == nc - 1)
        def _():
            dl0 = dlb_scr[...] * lb * (1.0 - lb)
            dlbl_ref[0:1, :] = dl0
            dlbl_ref[1:2, :] = -dl0

    wide = HG_BLOCK * HEAD_DIM
    groups = HG_HEADS // HG_BLOCK

    def col(k):
        return pl.BlockSpec((CHUNK, wide), lambda g, c: (nc - 1 - c, k * groups + g))

    blk = pl.BlockSpec((CHUNK, wide), lambda g, c: (nc - 1 - c, g))
    return _pallas(
        body,
        9,
        dep,
        name="hgrn_bwd",
        grid=(groups, nc),
        in_specs=[
            col(0), col(1), col(2), col(3),
            pl.BlockSpec((2, wide), lambda g, c: (0, g)),
            pl.BlockSpec((1, HEAD_DIM), lambda g, c: (0, 0)),
            blk, blk,
            pl.BlockSpec((HG_BLOCK, None, HEAD_DIM, HEAD_DIM), lambda g, c: (g, nc - 1 - c, 0, 0)),
        ],
        out_specs=[
            pl.BlockSpec((4, CHUNK, wide), lambda g, c: (0, nc - 1 - c, g)),
            pl.BlockSpec((2, wide), lambda g, c: (0, g)),
            pl.BlockSpec((1, HEAD_DIM), lambda g, c: (0, 0)),
        ],
        out_shape=[
            jax.ShapeDtypeStruct((4, t, HG_WIDTH), BF16),
            jax.ShapeDtypeStruct((2, HG_WIDTH), F32),
            jax.ShapeDtypeStruct((1, HEAD_DIM), F32),
        ],
        scratch_shapes=[pltpu.VMEM((HG_BLOCK, HEAD_DIM, HEAD_DIM), F32), pltpu.VMEM((1, wide), F32)],
        compiler_params=_params("arbitrary", "arbitrary"),
    )(proj, proj, proj, proj, lb_logits, hg_norm, o_hg, dycat, states)


def _diagonal_slots(shift):
    i = lax.broadcasted_iota(jnp.int32, (N_REL_PAD, DIAG), 0)
    u = lax.broadcasted_iota(jnp.int32, (N_REL_PAD, DIAG), 1)
    offset = u - shift if shift else jnp.where(u < K_BLOCK, u, u - DIAG)
    return jnp.where(jnp.clip(PAD - offset, -REL_CLIP, REL_CLIP) + REL_CLIP == i, 1.0, 0.0).astype(BF16)


def _split3(x):
    hi = x.astype(BF16)
    mid = (x - hi.astype(F32)).astype(BF16)
    return hi, mid, (x - hi.astype(F32) - mid.astype(F32)).astype(BF16)


def _bias_table(rel_bias):
    def body(rb_ref, o_ref, diag):
        h = pl.program_id(0)

        @pl.when(h == 0)
        def _():
            hi, mid, lo = _split3(rb_ref[...])
            slots = _diagonal_slots(0)
            diag[...] = _dot(hi, slots) + (_dot(mid, slots) + _dot(lo, slots))

        rows = jnp.broadcast_to(diag[pl.ds(h, 1), :], (Q_BLOCK, DIAG))
        o_ref[...] = pltpu.roll(rows, 0, 1, stride=1, stride_axis=0)[:, :K_BLOCK]

    return pl.pallas_call(
        body,
        name="bias_table",
        grid=(ATT_HEADS,),
        in_specs=[pl.BlockSpec((ATT_HEADS, N_REL_PAD), lambda h: (0, 0))],
        out_specs=pl.BlockSpec((None, Q_BLOCK, K_BLOCK), lambda h: (h, 0, 0)),
        out_shape=jax.ShapeDtypeStruct((ATT_HEADS, Q_BLOCK, K_BLOCK), F32),
        scratch_shapes=[pltpu.VMEM((ATT_HEADS, DIAG), F32)],
        compiler_params=_params("arbitrary"),
    )(rel_bias)


def _att_probs(q_ref, kpad, bias_ref, blk):
    qs = (q_ref[...] * ATT_SCALE).astype(BF16)
    start = pl.multiple_of(blk * Q_BLOCK, Q_BLOCK)
    kb = kpad[pl.ds(start, K_BLOCK), :]
    s = _dot(qs, kb, NT) + bias_ref[...]
    row = lax.broadcasted_iota(jnp.int32, (Q_BLOCK, K_BLOCK), 0)
    col = lax.broadcasted_iota(jnp.int32, (Q_BLOCK, K_BLOCK), 1)
    first = row - (row & (CHUNK - 1))
    valid = (col >= first) & (col < first + BAND) & (col + (blk * Q_BLOCK - PAD) >= 0)
    s = jnp.where(valid, s, jnp.finfo(F32).min)
    e = jnp.exp(s - jnp.max(s, axis=-1, keepdims=True))
    return qs, kb, start, e / jnp.sum(e, axis=-1, keepdims=True)


def _fill_padded(dst, src):
    dst[0:PAD, :] = jnp.zeros((PAD, HEAD_DIM), BF16)
    dst[PAD:, :] = src[...].astype(BF16)


def _att_fwd(proj, bias, dep=None):
    t = proj.shape[0]
    nb = t // Q_BLOCK

    def body(q_ref, k_ref, v_ref, bias_ref, y_ref, kpad, vpad):
        c = pl.program_id(1)

        @pl.when(c == 0)
        def _():
            _fill_padded(kpad, k_ref)
            _fill_padded(vpad, v_ref)

        _, _, start, p = _att_probs(q_ref, kpad, bias_ref, c)
        y_ref[...] = _dot(p.astype(BF16), vpad[pl.ds(start, K_BLOCK), :]).astype(BF16)

    base = 4 * HG_HEADS
    return _pallas(
        body,
        4,
        dep,
        name="att_fwd",
        grid=(ATT_HEADS, nb),
        in_specs=[
            pl.BlockSpec((Q_BLOCK, HEAD_DIM), lambda h, c: (c, base + h)),
            pl.BlockSpec((t, HEAD_DIM), lambda h, c: (0, base + ATT_HEADS + h)),
            pl.BlockSpec((t, HEAD_DIM), lambda h, c: (0, base + 2 * ATT_HEADS + h)),
            pl.BlockSpec((None, Q_BLOCK, K_BLOCK), lambda h, c: (h, 0, 0)),
        ],
        out_specs=pl.BlockSpec((Q_BLOCK, HEAD_DIM), lambda h, c: (c, h)),
        out_shape=jax.ShapeDtypeStruct((t, ATT_WIDTH), BF16),
        scratch_shapes=[pltpu.VMEM((t + PAD, HEAD_DIM), BF16), pltpu.VMEM((t + PAD, HEAD_DIM), BF16)],
        compiler_params=_params("arbitrary", "arbitrary"),
    )(proj, proj, proj, bias)


def _att_bwd(proj, bias, dycat, dep=None):
    t = proj.shape[0]
    nb = t // Q_BLOCK

    def body(q_ref, k_ref, v_ref, bias_ref, dy_ref, dq_ref, dk_ref, dv_ref, g_ref, kpad, vpad, dkacc, dvacc):
        c = pl.program_id(1)

        @pl.when(c == 0)
        def _():
            _fill_padded(kpad, k_ref)
            _fill_padded(vpad, v_ref)
            dkacc[...] = jnp.zeros_like(dkacc)
            dvacc[...] = jnp.zeros_like(dvacc)
            g_ref[...] = jnp.zeros_like(g_ref)

        qs, kb, start, p = _att_probs(q_ref, kpad, bias_ref, c)
        band = pl.ds(start, K_BLOCK)
        dyb = dy_ref[...].astype(BF16)
        dvacc[band, :] += _dot(p.astype(BF16), dyb, TN)
        dp = _dot(dyb, vpad[band, :], NT)
        ds = p * (dp - jnp.sum(dp * p, axis=-1, keepdims=True))
        g_ref[...] += ds
        dsb = ds.astype(BF16)
        dq_ref[...] = (_dot(dsb, kb) * ATT_SCALE).astype(BF16)
        dkacc[band, :] += _dot(dsb, qs, TN)

        @pl.when(c == nb - 1)
        def _():
            dk_ref[...] = dkacc[PAD:, :].astype(BF16)
            dv_ref[...] = dvacc[PAD:, :].astype(BF16)

    base = 4 * HG_HEADS
    whole = pl.BlockSpec((t, HEAD_DIM), lambda h, c: (0, h))
    return _pallas(
        body,
        5,
        dep,
        name="att_bwd",
        grid=(ATT_HEADS, nb),
        in_specs=[
            pl.BlockSpec((Q_BLOCK, HEAD_DIM), lambda h, c: (c, base + h)),
            pl.BlockSpec((t, HEAD_DIM), lambda h, c: (0, base + ATT_HEADS + h)),
            pl.BlockSpec((t, HEAD_DIM), lambda h, c: (0, base + 2 * ATT_HEADS + h)),
            pl.BlockSpec((None, Q_BLOCK, K_BLOCK), lambda h, c: (h, 0, 0)),
            pl.BlockSpec((Q_BLOCK, HEAD_DIM), lambda h, c: (c, HG_HEADS + h)),
        ],
        out_specs=[pl.BlockSpec((Q_BLOCK, HEAD_DIM), lambda h, c: (c, h)), whole, whole, pl.BlockSpec((None, Q_BLOCK, K_BLOCK), lambda h, c: (h, 0, 0))],
        out_shape=[
            jax.ShapeDtypeStruct((t, ATT_WIDTH), BF16),
            jax.ShapeDtypeStruct((t, ATT_WIDTH), BF16),
            jax.ShapeDtypeStruct((t, ATT_WIDTH), BF16),
            jax.ShapeDtypeStruct((ATT_HEADS, Q_BLOCK, K_BLOCK), F32),
        ],
        scratch_shapes=[
            pltpu.VMEM((t + PAD, HEAD_DIM), BF16),
            pltpu.VMEM((t + PAD, HEAD_DIM), BF16),
            pltpu.VMEM((t + PAD, HEAD_DIM), F32),
            pltpu.VMEM((t + PAD, HEAD_DIM), F32),
        ],
        compiler_params=_params("arbitrary", "arbitrary"),
    )(proj, proj, proj, bias, dycat)


def _rel_bias_grad(gsum):
    def body(g_ref, o_ref):
        r = lax.broadcasted_iota(jnp.int32, (Q_BLOCK, Q_BLOCK), 0)
        c = lax.broadcasted_iota(jnp.int32, (Q_BLOCK, Q_BLOCK), 1)
        flip = jnp.where(r + c == Q_BLOCK - 1, 1.0, 0.0).astype(BF16)
        sums = []
        for h in range(ATT_HEADS):
            hi, mid, lo = _split3(g_ref[h])
            rev = _dot(flip, hi) + (_dot(flip, mid) + _dot(flip, lo))
            wide = jnp.concatenate([rev, jnp.zeros((Q_BLOCK, DIAG - K_BLOCK), F32)], axis=1)
            sums.append(jnp.sum(pltpu.roll(wide, 0, 1, stride=1, stride_axis=0), axis=0, keepdims=True))
        hi, mid, lo = _split3(jnp.concatenate(sums, axis=0))
        slots = _diagonal_slots(Q_BLOCK - 1)
        o_ref[...] = _dot(hi, slots, NT) + (_dot(mid, slots, NT) + _dot(lo, slots, NT))

    return pl.pallas_call(
        body,
        name="rel_bias_grad",
        out_shape=jax.ShapeDtypeStruct((ATT_HEADS, N_REL_PAD), F32),
        compiler_params=_params(),
    )(gsum)


HALO = 16


FF_TILE = 1408
FF_TILES = D_FF // FF_TILE


def _interleave_cols(a):
    lead = a.shape[:-1]
    return jnp.swapaxes(a.reshape(*lead, 2, FF_TILES, FF_TILE), -3, -2).reshape(*lead, 2 * D_FF)


def _deinterleave_cols(a):
    lead = a.shape[:-1]
    return jnp.swapaxes(a.reshape(*lead, FF_TILES, 2, FF_TILE), -3, -2).reshape(*lead, 2 * D_FF)


def _ffn_specs(t, tm):
    wide = 2 * FF_TILE
    tile = pl.BlockSpec((tm, wide), lambda j, i: (i, j))
    before = pl.BlockSpec((HALO, wide), lambda j, i: (jnp.maximum(i * (tm // HALO) - 1, 0), j))
    after = pl.BlockSpec((HALO, wide), lambda j, i: (jnp.minimum((i + 1) * (tm // HALO), t // HALO - 1), j))
    vec = lambda rows: pl.BlockSpec((rows, wide), lambda j, i: (0, j))
    return tile, before, after, vec


def _conv(x, w, b, rows):
    taps = [pltpu.roll(x, 2, 0)[HALO : HALO + rows], pltpu.roll(x, 1, 0)[HALO : HALO + rows], x[HALO : HALO + rows]]
    return b + w[0:1] * taps[0] + w[1:2] * taps[1] + w[2:3] * taps[2], taps


def _ffn_act_fwd(u, conv_w, conv_b):
    t = u.shape[0]
    tm = _tile(t, (128,))
    tile, before, _, vec = _ffn_specs(t, tm)

    def body(u_ref, h_ref, w_ref, b_ref, z_ref):
        first = pl.program_id(1) == 0
        x = jnp.concatenate([jnp.where(first, 0.0, h_ref[...].astype(F32)), u_ref[...].astype(F32)], axis=0)
        c, _ = _conv(x, w_ref[...], b_ref[...], tm)
        gate, val = c[:, :FF_TILE], c[:, FF_TILE:]
        z_ref[...] = (gate * _sigmoid(gate) * val).astype(BF16)

    return pl.pallas_call(
        body,
        name="ffn_act_fwd",
        grid=(FF_TILES, t // tm),
        in_specs=[tile, before, vec(3), vec(1)],
        out_specs=pl.BlockSpec((tm, FF_TILE), lambda j, i: (i, j)),
        out_shape=jax.ShapeDtypeStruct((t, D_FF), BF16),
        compiler_params=_params("parallel", "parallel"),
    )(u, u, conv_w, conv_b)


def _ffn_act_bwd(u, dz, conv_w, conv_b, dep=None):
    t = u.shape[0]
    tm = _tile(t, (128,))
    nt = t // tm
    ext = tm + HALO
    tile, before, after, vec = _ffn_specs(t, tm)

    def body(u_ref, ub_ref, ua_ref, w_ref, b_ref, dz_ref, dza_ref, du_ref, dw_ref, db_ref):
        i = pl.program_id(1)
        first, last = i == 0, i == nt - 1
        parts = [jnp.where(first, 0.0, ub_ref[...].astype(F32)), u_ref[...].astype(F32), jnp.where(last, 0.0, ua_ref[...].astype(F32))]
        w = w_ref[...]
        c, taps = _conv(jnp.concatenate(parts, axis=0), w, b_ref[...], ext)
        gate, val = c[:, :FF_TILE], c[:, FF_TILE:]
        dz = jnp.concatenate([dz_ref[...].astype(F32), jnp.where(last, 0.0, dza_ref[...].astype(F32))], axis=0)
        sg = _sigmoid(gate)
        d = jnp.concatenate([dz * val * (sg * (1.0 + gate * (1.0 - sg))), dz * (gate * sg)], axis=1)
        du = w[2:3] * d[:tm] + w[1:2] * pltpu.roll(d, ext - 1, 0)[:tm] + w[0:1] * pltpu.roll(d, ext - 2, 0)[:tm]
        du_ref[...] = du.astype(BF16)

        @pl.when(first)
        def _():
            dw_ref[...] = jnp.zeros_like(dw_ref)
            db_ref[...] = jnp.zeros_like(db_ref)

        for k, tap in enumerate(taps):
            dw_ref[k : k + 1, :] += jnp.sum(d[:tm] * tap[:tm], axis=0, keepdims=True)
        db_ref[...] += jnp.sum(d[:tm], axis=0, keepdims=True)

    narrow = lambda rows, index: pl.BlockSpec((rows, FF_TILE), index)
    return _pallas(
        body,
        7,
        dep,
        name="ffn_act_bwd",
        grid=(FF_TILES, nt),
        in_specs=[
            tile, before, after, vec(3), vec(1),
            narrow(tm, lambda j, i: (i, j)),
            narrow(HALO, lambda j, i: (jnp.minimum((i + 1) * (tm // HALO), t // HALO - 1), j)),
        ],
        out_specs=[tile, vec(3), vec(1)],
        out_shape=[
            jax.ShapeDtypeStruct((t, 2 * D_FF), BF16),
            jax.ShapeDtypeStruct((3, 2 * D_FF), F32),
            jax.ShapeDtypeStruct((1, 2 * D_FF), F32),
        ],
        compiler_params=_params("parallel", "arbitrary"),
    )(u, u, u, conv_w, conv_b, dz, dz)


def _ple_loss(gpre, pp, h2, final_norm, target):
    t, d = h2.shape
    tm = _tile(t, (256,))

    def body(gp_ref, pp_ref, h_ref, g_ref, tg_ref, dh_ref, dgp_ref, dpp_ref, dg_ref, loss_ref):
        i = pl.program_id(0)
        gate = _sigmoid(gp_ref[...])
        ppv = pp_ref[...]
        h3 = h_ref[...] + gate * ppv
        r = lax.rsqrt(jnp.mean(h3 * h3, axis=-1, keepdims=True) + EPS)
        n = h3 * r
        g = g_ref[...]
        err = n * g - tg_ref[...]
        loss = 0.5 * jnp.sum(jnp.mean(err * err, axis=-1, keepdims=True))
        dy = err * (1.0 / d)
        dn = dy * g
        dh = r * (dn - n * jnp.mean(dn * n, axis=-1, keepdims=True))
        dh_ref[...] = dh
        dgp_ref[...] = (dh * ppv * gate * (1.0 - gate)).astype(BF16)
        dpp_ref[...] = (dh * gate).astype(BF16)
        dg = jnp.sum(dy * n, axis=0, keepdims=True)

        @pl.when(i == 0)
        def _():
            dg_ref[...] = dg
            loss_ref[...] = jnp.full(loss_ref.shape, loss, F32)

        @pl.when(i > 0)
        def _():
            dg_ref[...] += dg
            loss_ref[...] += loss

    row = pl.BlockSpec((tm, d), lambda i: (i, 0))
    vec = pl.BlockSpec((1, d), lambda i: (0, 0))
    return pl.pallas_call(
        body,
        name="ple_loss",
        grid=(t // tm,),
        in_specs=[row, row, row, vec, row],
        out_specs=[row, row, row, vec, pl.BlockSpec((8, 128), lambda i: (0, 0))],
        out_shape=[
            jax.ShapeDtypeStruct((t, d), F32),
            jax.ShapeDtypeStruct((t, d), BF16),
            jax.ShapeDtypeStruct((t, d), BF16),
            jax.ShapeDtypeStruct((1, d), F32),
            jax.ShapeDtypeStruct((8, 128), F32),
        ],
        compiler_params=_params("arbitrary"),
    )(gpre, pp, h2, final_norm, target)


def _adamw(w, g, m, v):
    m = ADAM_B1 * m + (1.0 - ADAM_B1) * g
    v = ADAM_B2 * v + (1.0 - ADAM_B2) * (g * g)
    m_hat = m / (1.0 - ADAM_B1 ** ADAM_STEP)
    v_hat = v / (1.0 - ADAM_B2 ** ADAM_STEP)
    return -ADAM_LR * (m_hat / (jnp.sqrt(v_hat) + ADAM_EPS) + ADAM_WD * w), m, v


def _adam_big(w, m, v, own, recv, name):
    r, c = w.shape
    tr = _tile(r, (256, 176))

    def body(w_ref, m_ref, v_ref, own_ref, recv_ref, g_ref, d_ref, nm_ref, nv_ref):
        g = own_ref[...]
        for k in range(3):
            g = g + recv_ref[k].astype(F32)
        g_ref[...] = g
        d_ref[...], nm_ref[...], nv_ref[...] = _adamw(w_ref[...], g, m_ref[...], v_ref[...])

    blk = pl.BlockSpec((tr, c), lambda i: (i, 0))
    return pl.pallas_call(
        body,
        name=name,
        grid=(r // tr,),
        in_specs=[blk, blk, blk, blk, pl.BlockSpec((3, tr, c), lambda i: (0, i, 0))],
        out_specs=[blk] * 4,
        out_shape=[jax.ShapeDtypeStruct((r, c), F32)] * 4,
        compiler_params=_params("parallel"),
    )(w, m, v, own, recv)


def _adam_small(w, g, m, v):
    def body(w_ref, g_ref, m_ref, v_ref, d_ref, nm_ref, nv_ref):
        d_ref[...], nm_ref[...], nv_ref[...] = _adamw(w_ref[...], g_ref[...], m_ref[...], v_ref[...])

    return pl.pallas_call(body, name="adam_small", out_shape=[jax.ShapeDtypeStruct(w.shape, F32)] * 3, compiler_params=_params())(w, g, m, v)


def _cast_bf16(w, name):
    r, c = w.shape
    tr = _tile(r, (256, 176))

    def body(w_ref, o_ref):
        o_ref[...] = w_ref[...].astype(BF16)

    blk = pl.BlockSpec((tr, c), lambda i: (i, 0))
    return pl.pallas_call(
        body, name=name, grid=(r // tr,), in_specs=[blk], out_specs=blk, out_shape=jax.ShapeDtypeStruct((r, c), BF16), compiler_params=_params("parallel")
    )(w)


def _position():
    return lax.axis_index("x"), lax.axis_index("y"), lax.axis_index("c")


def _other_chips(x, y):
    return [(1 - x, y), (x, 1 - y), (1 - x, 1 - y)]


def _block_index(dev, interleaved):
    x, y, c = dev
    return 4 * y + 2 * c + x if interleaved else 4 * x + 2 * y + c


def _shard_of(ref, axis, size, dev, interleaved=False):
    start = pl.multiple_of(_block_index(dev, interleaved) * size, 128 if axis == 1 else 16)
    return ref.at[:, pl.ds(start, size)] if axis == 1 else ref.at[pl.ds(start, size), :]


def _all_gather(shards, axes, interleaved):
    n = len(shards)

    def body(*refs):
        ins, outs = refs[:n], refs[n : 2 * n]
        send_sems, recv_sems, local_sems = refs[2 * n :]
        x, y, c = _position()
        me, sibling = (x, y, c), (x, y, 1 - c)
        chips = _other_chips(x, y)
        firsts, passed, locals_ = [], [], []
        for w in range(n):
            size = shards[w].shape[axes[w]]
            slot = functools.partial(_shard_of, outs[w], axes[w], size, interleaved=interleaved[w])

            def copy(k, block, to, src=None, w=w, slot=slot):
                return pltpu.make_async_remote_copy(
                    src_ref=slot(block) if src is None else src,
                    dst_ref=slot(block),
                    send_sem=send_sems.at[7 * w + k],
                    recv_sem=recv_sems.at[7 * w + k],
                    device_id=to,
                    device_id_type=MESH,
                )

            mine = pltpu.make_async_copy(ins[w], slot(me), local_sems.at[w])
            mine.start()
            locals_.append(mine)
            first = [copy(0, me, sibling, src=ins[w])] + [copy(1 + j, me, (*chip, c), src=ins[w]) for j, chip in enumerate(chips)]
            for cp in first:
                cp.start()
            firsts.append((first, copy))
        for w in range(n):
            first, copy = firsts[w]
            fwd = [copy(4 + j, (*chip, c), sibling) for j, chip in enumerate(chips)]
            for j, chip in enumerate(chips):
                copy(1 + j, (*chip, c), me).wait_recv()
                fwd[j].start()
            passed.append(fwd)
        for w in range(n):
            first, copy = firsts[w]
            copy(0, sibling, me).wait_recv()
            for j, chip in enumerate(chips):
                copy(4 + j, (*chip, 1 - c), me).wait_recv()
            for cp in first + passed[w]:
                cp.wait_send()
            locals_[w].wait()

    def full(s, ax):
        shape = list(s.shape)
        shape[ax] *= N_DEV
        return jax.ShapeDtypeStruct(tuple(shape), s.dtype)

    return pl.pallas_call(
        body,
        name="all_gather_weights",
        in_specs=[ANY] * n,
        out_specs=[ANY] * n,
        out_shape=[full(s, ax) for s, ax in zip(shards, axes)],
        scratch_shapes=[pltpu.SemaphoreType.DMA((7 * n,)), pltpu.SemaphoreType.DMA((7 * n,)), pltpu.SemaphoreType.DMA((n,))],
    )(*shards)


def _add_blocks(ids, grad, landed, axis, size, targets, out_dtype, name):
    rows = size if axis == 0 else grad.shape[0]
    cols = size if axis == 1 else grad.shape[1]
    tr = _tile(rows, (256, 176))
    nr = rows // tr
    nt = len(targets)

    def body(ids_ref, g_ref, l_ref, o_ref):
        o_ref[...] = (g_ref[...] + l_ref[...]).astype(out_dtype)

    if axis == 1:
        g_spec = pl.BlockSpec((tr, cols), lambda k, i, ids: (i, ids[targets[0] + k]))
    else:
        g_spec = pl.BlockSpec((tr, cols), lambda k, i, ids: (ids[targets[0] + k] * nr + i, 0))
    return pl.pallas_call(
        body,
        name=name,
        grid_spec=pltpu.PrefetchScalarGridSpec(
            num_scalar_prefetch=1,
            grid=(nt, nr),
            in_specs=[g_spec, pl.BlockSpec((None, tr, cols), lambda k, i, ids: (ids[4 + targets[0] + k], i, 0))],
            out_specs=pl.BlockSpec((None, tr, cols), lambda k, i, ids: (k, i, 0)),
        ),
        out_shape=jax.ShapeDtypeStruct((nt, rows, cols), out_dtype),
        compiler_params=_params("parallel", "parallel"),
    )(ids, grad, landed)


def _all_reduce_small(vec):
    rows = vec.shape[0]

    def body(v_ref, o_ref, land, send_sems, recv_sems):
        x, y, c = _position()
        mine = 4 * x + 2 * y + c
        copies = []
        for mask in range(1, N_DEV):
            peer = (1 - x if mask & 4 else x, 1 - y if mask & 2 else y, 1 - c if mask & 1 else c)
            copies.append(
                pltpu.make_async_remote_copy(
                    src_ref=v_ref, dst_ref=land.at[mine], send_sem=send_sems.at[mask - 1], recv_sem=recv_sems.at[mask - 1], device_id=peer, device_id_type=MESH
                )
            )
        for cp in copies:
            cp.start()
        land[mine] = v_ref[...]
        for cp in copies:
            cp.wait()
        acc = land[0]
        for k in range(1, N_DEV):
            acc = acc + land[k]
        o_ref[...] = acc

    return pl.pallas_call(
        body,
        name="all_reduce_small",
        out_shape=jax.ShapeDtypeStruct(vec.shape, F32),
        in_specs=[pl.BlockSpec(memory_space=pltpu.VMEM)],
        out_specs=pl.BlockSpec(memory_space=pltpu.VMEM),
        scratch_shapes=[pltpu.VMEM((N_DEV, rows, 128), F32), pltpu.SemaphoreType.DMA((N_DEV - 1,)), pltpu.SemaphoreType.DMA((N_DEV - 1,))],
    )(vec)


def _rows128(a, rows):
    flat = a.reshape(-1)
    return jnp.pad(flat, (0, rows * 128 - flat.shape[0])).reshape(rows, 128)


def _pad_rel(a):
    return jnp.pad(a.reshape(ATT_HEADS, -1)[:, :N_REL], ((0, 0), (0, N_REL_PAD - N_REL)))


SMALL = [("norm_mix", 16), ("lb_logits", 16), ("hg_norm", 8), ("rel_bias", 24), ("norm_ffn", 16), ("conv_b", 88), ("norm_ple", 16), ("final_norm", 16)]
CONV_W_FULL_ROWS = 3 * 2 * D_FF // 128
CONV_W_SHARD_ROWS = 40


def _pack_small(parts):
    return jnp.concatenate([_rows128(_pad_rel(parts[k]) if k == "rel_bias" else parts[k], rows) for k, rows in SMALL], axis=0)


def _unpack_small(packed, shapes):
    out, at = {}, 0
    for k, rows in SMALL:
        blk = packed[at : at + rows]
        at += rows
        if k == "rel_bias":
            out[k] = blk.reshape(ATT_HEADS, N_REL_PAD)[:, :N_REL].reshape(shapes[k])
        else:
            n = 1
            for s in shapes[k]:
                n *= s
            out[k] = blk.reshape(-1)[:n].reshape(shapes[k])
    return out, at


BIG = [("w_in", 1), ("w_out", 0), ("w_up", 1), ("w_down", 0), ("w_ple_gate", 0), ("w_ple_proj", 1)]


HBM = pl.BlockSpec(memory_space=pltpu.HBM)
SEM = pl.BlockSpec(memory_space=pltpu.SEMAPHORE)
EFFECT = pltpu.SideEffectType.DATAFLOW_SIDE_EFFECTING


def _copies(plan, refs, send_sems, recv_sems):
    return [
        pltpu.make_async_remote_copy(src_ref=src, dst_ref=dst, send_sem=send_sems.at[i], recv_sem=recv_sems.at[i], device_id=dev, device_id_type=MESH)
        for i, (src, dst, dev) in enumerate(plan(refs))
    ]


def _split_start(name, arrays, plan, n):
    k = len(arrays)

    def body(*refs):
        for cp in _copies(plan, refs[:k], refs[k], refs[k + 1]):
            cp.start()
        refs[-1][...] = jnp.zeros_like(refs[-1])

    out = pl.pallas_call(
        body,
        name=name,
        out_shape=(pltpu.SemaphoreType.DMA((n,)), pltpu.SemaphoreType.DMA((n,)), *[pltpu.HBM(a.shape, a.dtype) for a in arrays], jax.ShapeDtypeStruct((8, 128), F32)),
        in_specs=[HBM] * k,
        out_specs=(SEM, SEM, *[HBM] * k, pl.BlockSpec(memory_space=pltpu.VMEM)),
        input_output_aliases={i: 2 + i for i in range(k)},
        compiler_params=pltpu.CompilerParams(has_side_effects=EFFECT),
    )(*[pltpu.with_memory_space_constraint(a, pltpu.HBM) for a in arrays])
    return out[0], out[1], list(out[2 : 2 + k]), out[-1]


def _split_wait(name, send, recv, arrays, plan, after):
    k = len(arrays)

    def body(*refs):
        for cp in _copies(plan, refs[:k], refs[k], refs[k + 1]):
            cp.wait_send()
            cp.wait_recv()

    out = pl.pallas_call(
        body,
        name=name,
        out_shape=tuple(pltpu.HBM(a.shape, a.dtype) for a in arrays),
        in_specs=[HBM] * k + [SEM, SEM, ANY],
        out_specs=tuple([HBM] * k),
        input_output_aliases={i: i for i in range(k)},
        compiler_params=pltpu.CompilerParams(has_side_effects=EFFECT),
    )(*arrays, send, recv, after)
    return list(out)


def _cast_into(w, me, axis, name, dep):
    r, c = w.shape
    tr = _tile(r, (256, 176))
    nr = r // tr

    def body(me_ref, w_ref, dep_ref, o_ref):
        o_ref[...] = w_ref[...].astype(BF16)

    if axis == 1:
        shape, o_spec = (r, N_DEV * c), pl.BlockSpec((tr, c), lambda i, me: (i, me[0]))
    else:
        shape, o_spec = (N_DEV * r, c), pl.BlockSpec((tr, c), lambda i, me: (me[0] * nr + i, 0))
    return pl.pallas_call(
        body,
        name=name,
        grid_spec=pltpu.PrefetchScalarGridSpec(
            num_scalar_prefetch=1, grid=(nr,), in_specs=[pl.BlockSpec((tr, c), lambda i, me: (i, 0)), ANY], out_specs=o_spec
        ),
        out_shape=jax.ShapeDtypeStruct(shape, BF16),
        compiler_params=_params("parallel"),
    )(me, w, dep)


GATHER = [
    (["w_out"], None, "att_fwd", None),
    (["w_up"], None, "att_fwd", "norm_ffn_fwd"),
    (["w_down", "w_ple_gate", "w_ple_proj"], "att_fwd", "up_proj", "ffn_act_fwd"),
]
GROUPS = [["w_ple_proj", "w_ple_gate", "w_down"], ["w_up"], ["w_out"], ["w_in"]]
STAGES = ["ffn_act_bwd", "d_mix_out", "hgrn_bwd", "d_norm_mix_out"]
INTERLEAVED = {"w_up"}


class _Exchange:
    def __init__(self, big, position):
        self.big, self.axis = big, dict(BIG)
        self.size = {k: big[k].shape[self.axis[k]] for k in big}
        self.x, self.y, self.c = position
        chips = [(self.x, self.y)] + _other_chips(self.x, self.y)
        landed = [2 * cx + cy for cx, cy in chips]
        self.ids = {
            flag: jnp.stack([_block_index((cx, cy, self.c), flag) for cx, cy in chips] + landed).astype(jnp.int32) for flag in (False, True)
        }
        self.token, self.grads, self.state, self.wfull = None, {}, {}, {}


    def _slot(self, ref, k, dev):
        return _shard_of(ref, self.axis[k], self.size[k], dev, interleaved=k in INTERLEAVED)

    def _plan_gather(self, names, direct, refs):
        x, y, c = _position()
        me, out = (x, y, c), []
        for k, ref in zip(names, refs):
            mine = self._slot(ref, k, me)
            out.append((mine, mine, (x, y, 1 - c)))
            out += [(mine, mine, (*chip, c)) for chip in _other_chips(x, y)]
            if direct:
                out += [(mine, mine, (*chip, 1 - c)) for chip in _other_chips(x, y)]
        return out

    def _plan_forward(self, names, refs):
        x, y, c = _position()
        out = []
        for k, ref in zip(names, refs):
            for chip in _other_chips(x, y):
                block = self._slot(ref, k, (*chip, c))
                out.append((block, block, (x, y, 1 - c)))
        return out

    def _plan_sibling(self, names, refs):
        x, y, c = _position()
        n = len(names)
        return [(self._slot(refs[i], k, (p // 2, p % 2, 1 - c)), refs[n + i].at[p], (x, y, 1 - c)) for i, k in enumerate(names) for p in range(4)]

    def _plan_chips(self, names, refs):
        x, y, c = _position()
        n = len(names)
        return [(refs[i].at[j], refs[n + i].at[j], (*chip, c)) for i in range(n) for j, chip in enumerate(_other_chips(x, y))]


    def gather(self, conv_w):
        w_in, conv_full = _all_gather([_cast_bf16(self.big["w_in"], "cast_w_in"), conv_w], [1, 1], [False, True])
        self.wfull["w_in"] = w_in
        me = {flag: _block_index((self.x, self.y, self.c), flag).astype(jnp.int32).reshape(1) for flag in (False, True)}
        self.late, self.unsent = {}, {}
        for gi, (names, *_) in enumerate(GATHER):
            self.unsent[gi] = [_cast_into(self.big[k], me[k in INTERLEAVED], self.axis[k], "cast_" + k, w_in) for k in names]
        self._issue(None)
        return conv_full

    def _issue(self, stage):
        for gi, (names, issued, _, forwarded) in enumerate(GATHER):
            if issued == stage:
                plan = functools.partial(self._plan_gather, names, forwarded is None)
                copies = (7 if forwarded is None else 4) * len(names)
                send, recv, fulls, self.token = _split_start(f"gather_start_{gi}", self.unsent.pop(gi), plan, copies)
                self.late[gi] = (send, recv, fulls, plan)

    def weight(self, k):
        return self.wfull[k]

    def dep(self):
        token, self.token = self.token, None
        return token

    def grad(self, k, g):
        self.grads[k] = g
        for gi, names in enumerate(GROUPS):
            if k == names[-1]:
                plan = functools.partial(self._plan_sibling, names)
                lands = [lax.empty((4, *self._shard_shape(n)), F32) for n in names]
                send, recv, arrays, self.token = _split_start(f"sibling_start_{gi}", [self.grads[n] for n in names] + lands, plan, 4 * len(names))
                self.state[gi] = (send, recv, arrays, plan)

    def done(self, stage, after):
        for gi, (names, _, _, forwarded) in enumerate(GATHER):
            if forwarded == stage:
                send, recv, fulls, plan = self.late[gi]
                self.wfull.update(zip(names, _split_wait(f"forward_wait_{gi}", send, recv, fulls, plan, after)))
        for gi, (names, _, arrived, forwarded) in enumerate(GATHER):
            if arrived == stage:
                send, recv, fulls, plan = self.late[gi]
                fulls = _split_wait(f"gather_wait_{gi}", send, recv, fulls, plan, after)
                if forwarded is None:
                    self.wfull.update(zip(names, fulls))
                else:
                    plan = functools.partial(self._plan_forward, names)
                    send, recv, fulls, self.token = _split_start(f"forward_start_{gi}", fulls, plan, 3 * len(names))
                    self.late[gi] = (send, recv, fulls, plan)
        self._issue(stage)
        if stage in STAGES:
            self._to_chips(STAGES.index(stage), after)

    def _shard_shape(self, k):
        shape = list(self.grads[k].shape)
        shape[self.axis[k]] = self.size[k]
        return tuple(shape)

    def _to_chips(self, gi, after):
        names = GROUPS[gi]
        n = len(names)
        send, recv, arrays, plan = self.state[gi]
        arrays = _split_wait(f"sibling_wait_{gi}", send, recv, arrays, plan, after)
        own, parts = [], []
        for k, g, land in zip(names, arrays[:n], arrays[n:]):
            ids = self.ids[k in INTERLEAVED]
            own.append(_add_blocks(ids, g, land, self.axis[k], self.size[k], [0], F32, "add_own_" + k)[0])
            parts.append(_add_blocks(ids, g, land, self.axis[k], self.size[k], [1, 2, 3], BF16, "add_send_" + k))
        plan = functools.partial(self._plan_chips, names)
        lands = [lax.empty(part.shape, BF16) for part in parts]
        send, recv, arrays, self.token = _split_start(f"chips_start_{gi}", parts + lands, plan, 3 * n)
        self.state[gi] = (send, recv, arrays, plan, own)

    def finish(self, gi, after):
        names = GROUPS[gi]
        send, recv, arrays, plan, own = self.state[gi]
        arrays = _split_wait(f"chips_wait_{gi}", send, recv, arrays, plan, after)
        return {k: (o, r) for k, o, r in zip(names, own, arrays[len(names) :])}


class _Resident:
    def __init__(self, wfull):
        self.wfull, self.grads = wfull, {}

    def weight(self, k):
        return self.wfull[k]

    def grad(self, k, g):
        self.grads[k] = g

    def dep(self):
        return None

    def done(self, stage, after):
        pass


def _local_step(x, p, target, small, conv_w, ex):
    a1, r1 = _rms_fwd(x, small["norm_mix"], "norm_mix_fwd", dep=ex.dep())
    proj = _matmul(a1, ex.weight("w_in"), "nn", F32, "in_proj")
    bias = _bias_table(jnp.pad(small["rel_bias"], ((0, 0), (0, N_REL_PAD - N_REL))))
    y_hg, o_hg, states = _hgrn_fwd(proj, small["lb_logits"], small["hg_norm"])
    y_att = _att_fwd(proj, bias, dep=ex.dep())
    ex.done("att_fwd", y_att)
    ycat = jnp.concatenate([y_hg, y_att], axis=1)
    h1 = _matmul(ycat, ex.weight("w_out"), "nn", F32, "out_proj", resid=x, dep=ex.dep())
    a2, r2 = _rms_fwd(h1, small["norm_ffn"], "norm_ffn_fwd")
    ex.done("norm_ffn_fwd", a2)
    u = _matmul(a2, ex.weight("w_up"), "nn", BF16, "up_proj")
    conv_b = _interleave_cols(small["conv_b"])
    ex.done("up_proj", u)
    z = _ffn_act_fwd(u, conv_w, conv_b)
    ex.done("ffn_act_fwd", z)
    h2 = _matmul(z, ex.weight("w_down"), "nn", F32, "down_proj", tk=2816, resid=h1)
    a3, r3 = _rms_fwd(h2, small["norm_ple"], "norm_ple_fwd")
    gpre = _matmul(a3, ex.weight("w_ple_gate"), "nn", F32, "ple_gate")
    pp = _matmul(p, ex.weight("w_ple_proj"), "nn", F32, "ple_proj")
    dh3, dgpre, dpp, d_final, loss = _ple_loss(gpre, pp, h2, small["final_norm"], target)

    ex.grad("w_ple_proj", _matmul(p, dpp, "tn", F32, "d_w_ple_proj", tk=2048))
    ex.grad("w_ple_gate", _matmul(a3, dgpre, "tn", F32, "d_w_ple_gate", tk=2048))
    da3 = _matmul(dgpre, ex.weight("w_ple_gate"), "nt", F32, "d_norm_ple_out")
    dh2, d_ple = _rms_bwd(da3, h2, r3, small["norm_ple"], dh3, "norm_ple_bwd")
    dz = _matmul(dh2, ex.weight("w_down"), "nt", BF16, "d_ffn_act")
    ex.grad("w_down", _matmul(z, dh2, "tn", F32, "d_w_down", tk=2048))
    du, dcw, dcb = _ffn_act_bwd(u, dz, conv_w, conv_b, dep=ex.dep())
    ex.done("ffn_act_bwd", du)
    d_conv_w, d_conv_b = _deinterleave_cols(dcw), _deinterleave_cols(dcb)
    ex.grad("w_up", _matmul(a2, du, "tn", F32, "d_w_up", tk=2048, dep=ex.dep()))
    da2 = _matmul(du, ex.weight("w_up"), "nt", F32, "d_norm_ffn_out", tk=2816, dep=ex.dep())
    dh1, d_ffn = _rms_bwd(da2, h1, r2, small["norm_ffn"], dh2, "norm_ffn_bwd")
    dycat = _matmul(dh1, ex.weight("w_out"), "nt", F32, "d_mix_out")
    ex.done("d_mix_out", dycat)
    ex.grad("w_out", _matmul(ycat, dh1, "tn", F32, "d_w_out", tk=2048, dep=ex.dep()))
    dp_hg, d_lb, d_hgn = _hgrn_bwd(proj, small["lb_logits"], small["hg_norm"], o_hg, dycat, states, dep=ex.dep())
    ex.done("hgrn_bwd", d_lb)
    dq_att, dk_att, dv_att, gsum = _att_bwd(proj, bias, dycat, dep=ex.dep())
    d_rel = _rel_bias_grad(gsum)
    dproj = jnp.concatenate([dp_hg[0], dp_hg[1], dp_hg[2], dp_hg[3], dq_att, dk_att, dv_att], axis=1)
    ex.grad("w_in", _matmul(a1, dproj, "tn", F32, "d_w_in", tk=2048))
    da1 = _matmul(dproj, ex.weight("w_in"), "nt", F32, "d_norm_mix_out", tk=1792, dep=ex.dep())
    ex.done("d_norm_mix_out", da1)
    dx, d_mix = _rms_bwd(da1, x, r1, small["norm_mix"], dh1, "norm_mix_bwd", dep=ex.dep())
    d_small = {
        "norm_mix": d_mix, "lb_logits": d_lb, "hg_norm": d_hgn, "rel_bias": d_rel, "norm_ffn": d_ffn,
        "conv_b": d_conv_b, "norm_ple": d_ple, "final_norm": d_final,
    }
    return loss, dx, d_small, d_conv_w


def kernel(x, p, norm_mix, w_in, lb_logits, hg_norm, rel_bias, w_out, norm_ffn, w_up, conv_w, conv_b, w_down, norm_ple, w_ple_gate, w_ple_proj, final_norm, loss_target, m_norm_mix, m_w_in, m_lb_logits, m_hg_norm, m_rel_bias, m_w_out, m_norm_ffn, m_w_up, m_conv_w, m_conv_b, m_w_down, m_norm_ple, m_w_ple_gate, m_w_ple_proj, m_final_norm, v_norm_mix, v_w_in, v_lb_logits, v_hg_norm, v_rel_bias, v_w_out, v_norm_ffn, v_w_up, v_conv_w, v_conv_b, v_w_down, v_norm_ple, v_w_ple_gate, v_w_ple_proj, v_final_norm):
    given = dict(locals())
    mx, my, mc = _position()
    me = 4 * mx + 2 * my + mc
    big = {k: given[k][0] for k, _ in BIG}
    ex = _Exchange(big, (mx, my, mc))
    conv_w_full = ex.gather(conv_w[0])

    small = {
        "norm_mix": norm_mix, "lb_logits": lb_logits, "hg_norm": hg_norm, "rel_bias": rel_bias[0], "norm_ffn": norm_ffn,
        "conv_b": conv_b, "norm_ple": norm_ple, "final_norm": final_norm.reshape(1, -1),
    }
    loss, dx, d_small, d_conv_w = _local_step(x[0], p[0, 0], loss_target[0], small, conv_w_full, ex)

    packed = jnp.concatenate([_pack_small(d_small), _rows128(d_conv_w, CONV_W_FULL_ROWS), _rows128(loss[0:1, 0:1], 8)], axis=0)
    reduced = _all_reduce_small(packed)

    out = {}
    for gi in range(len(GROUPS)):
        for k, (o, r) in ex.finish(gi, reduced).items():
            g, d, nm, nv = _adam_big(big[k], given["m_" + k][0], given["v_" + k][0], o, r, "adam_" + k)
            out[k] = tuple(a[None] for a in (g, d, nm, nv))
    shapes = {k: given[k].shape for k, _ in SMALL}
    g_small, at = _unpack_small(reduced, shapes)
    g_conv_full = reduced[at : at + CONV_W_FULL_ROWS].reshape(3, 2 * D_FF)
    total_loss = reduced[at + CONV_W_FULL_ROWS, 0]
    cw = conv_w.shape[2]
    g_conv = lax.dynamic_slice_in_dim(g_conv_full, me * cw, cw, axis=1)

    def pack_with_conv(parts, conv_part):
        return jnp.concatenate([_pack_small(parts), _rows128(conv_part, CONV_W_SHARD_ROWS)], axis=0)

    d_pk, m_pk, v_pk = _adam_small(
        pack_with_conv({k: given[k] for k, _ in SMALL}, conv_w),
        pack_with_conv(g_small, g_conv),
        pack_with_conv({k: given["m_" + k] for k, _ in SMALL}, m_conv_w),
        pack_with_conv({k: given["v_" + k] for k, _ in SMALL}, v_conv_w),
    )
    for name, pk in (("d", d_pk), ("m", m_pk), ("v", v_pk)):
        parts, at = _unpack_small(pk, shapes)
        parts["conv_w"] = pk[at : at + CONV_W_SHARD_ROWS].reshape(-1)[: 3 * cw].reshape(conv_w.shape)
        for k, a in parts.items():
            out.setdefault(k, {})
            out[k][name] = a
    for k, _ in SMALL:
        out[k]["g"] = g_small[k]
    out["conv_w"]["g"] = g_conv.reshape(conv_w.shape)

    order = ["norm_mix", "w_in", "lb_logits", "hg_norm", "rel_bias", "w_out", "norm_ffn", "w_up", "conv_w", "conv_b", "w_down", "norm_ple", "w_ple_gate", "w_ple_proj", "final_norm"]

    def pick(k, what):
        return out[k][what] if isinstance(out[k], dict) else out[k][{"g": 0, "d": 1, "m": 2, "v": 3}[what]]

    return (total_loss, dx[None], *[pick(k, "g") for k in order], *[pick(k, "d") for k in order], *[pick(k, "m") for k in order], *[pick(k, "v") for k in order])
```

```python
import functools

import jax
import jax.numpy as jnp
from jax import lax
from jax.experimental import pallas as pl
from jax.experimental.pallas import tpu as pltpu

F32 = jnp.float32
BF16 = jnp.bfloat16

D_MODEL = 2048
CHUNK = 64
HG_HEADS = 8
HEAD_DIM = 128
HG_WIDTH = HG_HEADS * HEAD_DIM
ATT_HEADS = 8
ATT_WIDTH = ATT_HEADS * HEAD_DIM
LEFT_CHUNKS = 8
PAD = LEFT_CHUNKS * CHUNK
BAND = PAD + CHUNK
REL_CLIP = 128
N_REL = 2 * REL_CLIP + 1
N_REL_PAD = 384
D_FF = 5632
EPS = 1e-6
ATT_SCALE = HEAD_DIM ** -0.5
SUB = 32
HG_BLOCK = 8
Q_BLOCK = 4 * CHUNK
K_BLOCK = Q_BLOCK + PAD
DIAG = 1024

ADAM_LR = 0.001
ADAM_B1 = 0.9
ADAM_B2 = 0.999
ADAM_EPS = 1e-08
ADAM_WD = 0.01
ADAM_STEP = 10

N_DEV = 8
VMEM_LIMIT = 48 * 1024 * 1024
MESH = pl.DeviceIdType.MESH
ANY = pl.BlockSpec(memory_space=pl.ANY)
HIGHEST = lax.Precision.HIGHEST

NN = (((1,), (0,)), ((), ()))
NT = (((1,), (1,)), ((), ()))
TN = (((0,), (0,)), ((), ()))


def _params(*sem):
    return pltpu.CompilerParams(dimension_semantics=sem if sem else None, vmem_limit_bytes=VMEM_LIMIT)


def _pallas(body, n_in, dep, **kw):
    deps = [] if dep is None else list(dep)
    if not deps:
        return pl.pallas_call(body, **kw)

    def body_after(*refs):
        body(*refs[:n_in], *refs[n_in + len(deps) :])

    call = pl.pallas_call(body_after, **dict(kw, in_specs=list(kw["in_specs"]) + [ANY] * len(deps)))
    return lambda *ops: call(*ops, *deps)


def _dot(a, b, dims=NN):
    return lax.dot_general(a, b, dims, preferred_element_type=F32)


def _dot3(a, b, dims=NN):
    a_hi, b_hi = a.astype(BF16), b.astype(BF16)
    a_lo, b_lo = (a - a_hi.astype(F32)).astype(BF16), (b - b_hi.astype(F32)).astype(BF16)
    return _dot(a_hi, b_hi, dims) + (_dot(a_hi, b_lo, dims) + _dot(a_lo, b_hi, dims))


def _sigmoid(x):
    return 1.0 / (1.0 + jnp.exp(-x))


def _tile(n, prefs):
    for t in prefs:
        if n % t == 0:
            return t
    return n


def _matmul(a, b, mode, out_dtype, name, tm=1024, tn=1024, tk=None, resid=None, dep=None):
    if mode == "nn":
        (m, k), n = a.shape, b.shape[1]
    elif mode == "nt":
        (m, k), n = a.shape, b.shape[0]
    else:
        (k, m), n = a.shape, b.shape[1]
    tm = _tile(m, (tm, 512, 256, 128))
    tn = _tile(n, (tn, 1408, 512, 256, 128))
    tk = k if tk is None else _tile(k, (tk,))
    nk = k // tk
    dims = {"nn": NN, "nt": NT, "tn": TN}[mode]
    a_spec = pl.BlockSpec((tk, tm), lambda i, j, s: (s, i)) if mode == "tn" else pl.BlockSpec((tm, tk), lambda i, j, s: (i, s))
    b_spec = pl.BlockSpec((tn, tk), lambda i, j, s: (j, s)) if mode == "nt" else pl.BlockSpec((tk, tn), lambda i, j, s: (s, j))
    o_spec = pl.BlockSpec((tm, tn), lambda i, j, s: (i, j))
    has_res = resid is not None

    def body(*refs):
        a_ref, b_ref = refs[0], refs[1]
        o_ref = refs[2 + has_res]
        part = _dot(a_ref[...].astype(BF16), b_ref[...].astype(BF16), dims)

        def finish(acc):
            if has_res:
                acc = acc + refs[2][...]
            o_ref[...] = acc.astype(out_dtype)

        if nk == 1:
            finish(part)
        else:
            acc_ref = refs[-1]
            s = pl.program_id(2)

            @pl.when(s == 0)
            def _():
                acc_ref[...] = part

            @pl.when(s > 0)
            def _():
                acc_ref[...] += part

            @pl.when(s == nk - 1)
            def _():
                finish(acc_ref[...])

    return _pallas(
        body,
        2 + has_res,
        dep,
        name=name,
        grid=(m // tm, n // tn, nk),
        in_specs=[a_spec, b_spec] + ([o_spec] if has_res else []),
        out_specs=o_spec,
        out_shape=jax.ShapeDtypeStruct((m, n), out_dtype),
        scratch_shapes=[pltpu.VMEM((tm, tn), F32)] if nk > 1 else [],
        compiler_params=_params("parallel", "parallel", "arbitrary"),
    )(*([a, b] + ([resid] if has_res else [])))


def _rms_fwd(x, g, name, dep=None):
    t, d = x.shape
    tm = _tile(t, (256,))

    def body(x_ref, g_ref, a_ref, r_ref):
        xv = x_ref[...]
        r = lax.rsqrt(jnp.mean(xv * xv, axis=-1, keepdims=True) + EPS)
        a_ref[...] = (xv * r * g_ref[...]).astype(BF16)
        r_ref[...] = r

    row = pl.BlockSpec((tm, d), lambda i: (i, 0))
    return _pallas(
        body,
        2,
        dep,
        name=name,
        grid=(t // tm,),
        in_specs=[row, pl.BlockSpec((1, d), lambda i: (0, 0))],
        out_specs=[row, pl.BlockSpec((tm, 1), lambda i: (i, 0))],
        out_shape=[jax.ShapeDtypeStruct((t, d), BF16), jax.ShapeDtypeStruct((t, 1), F32)],
        compiler_params=_params("parallel"),
    )(x, g)


def _rms_bwd(da, x, r, g, resid, name, dep=None):
    t, d = x.shape
    tm = _tile(t, (256,))

    def body(da_ref, x_ref, r_ref, g_ref, res_ref, dx_ref, dg_ref):
        i = pl.program_id(0)
        rv = r_ref[...]
        n = x_ref[...] * rv
        dav = da_ref[...]
        dn = dav * g_ref[...]
        dx_ref[...] = rv * (dn - n * jnp.mean(dn * n, axis=-1, keepdims=True)) + res_ref[...]
        part = jnp.sum(dav * n, axis=0, keepdims=True)

        @pl.when(i == 0)
        def _():
            dg_ref[...] = part

        @pl.when(i > 0)
        def _():
            dg_ref[...] += part

    row = pl.BlockSpec((tm, d), lambda i: (i, 0))
    vec = pl.BlockSpec((1, d), lambda i: (0, 0))
    return _pallas(
        body,
        5,
        dep,
        name=name,
        grid=(t // tm,),
        in_specs=[row, row, pl.BlockSpec((tm, 1), lambda i: (i, 0)), vec, row],
        out_specs=[row, vec],
        out_shape=[jax.ShapeDtypeStruct((t, d), F32), jax.ShapeDtypeStruct((1, d), F32)],
        compiler_params=_params("arbitrary"),
    )(da, x, r, g, resid)


def _tri(n, upper):
    r = lax.broadcasted_iota(jnp.int32, (n, n), 0)
    c = lax.broadcasted_iota(jnp.int32, (n, n), 1)
    return jnp.where((c >= r) if upper else (c <= r), 1.0, 0.0).astype(F32)


def _hgrn_gates(q, fp, lbl):
    l0, l1 = lbl[0:1, :], lbl[1:2, :]
    mx = jnp.maximum(l0, l1)
    e0, e1 = jnp.exp(l0 - mx), jnp.exp(l1 - mx)
    lb = e0 / (e0 + e1)
    sig = _sigmoid(fp)
    f = lb + (1.0 - lb) * sig
    kk = (1.0 - lb) * _sigmoid(-fp)
    sq = _sigmoid(q)
    b = jnp.dot(_tri(CHUNK, False), jnp.log(f), precision=HIGHEST, preferred_element_type=F32)
    return lb, sig, f, kk, sq, q * sq, b


def _heads(x):
    return [x[:, j * HEAD_DIM : (j + 1) * HEAD_DIM] for j in range(x.shape[1] // HEAD_DIM)]


def _wide(parts):
    return jnp.concatenate(parts, axis=1)


def _intra_blocks(b):
    out = []
    for lo in range(0, CHUNK, SUB):
        hi = lo + SUB
        br = b[lo + SUB // 2 : lo + SUB // 2 + 1, :]
        row = lax.broadcasted_iota(jnp.int32, (SUB, hi), 0) + lo
        col = lax.broadcasted_iota(jnp.int32, (SUB, hi), 1)
        out.append((lo, hi, jnp.exp(b[lo:hi] - br), jnp.exp(br - b[:hi]), col <= row))
    return out


def _hgrn_fwd(proj, lb_logits, hg_norm):
    t = proj.shape[0]
    nc = t // CHUNK

    def body(q_ref, f_ref, i_ref, g_ref, lbl_ref, hgn_ref, y_ref, o_ref, st_ref, s_scr):
        c = pl.program_id(1)

        @pl.when(c == 0)
        def _():
            s_scr[...] = jnp.zeros_like(s_scr)

        hs = range(HG_BLOCK)
        sts = [s_scr[j] for j in hs]
        _, _, _, kk, _, qf, b = _hgrn_gates(q_ref[...], f_ref[...], lbl_ref[...])
        vb = _heads(i_ref[...].astype(BF16))
        bl = b[CHUNK - 1 : CHUNK, :]
        qe = _heads((qf * jnp.exp(b)).astype(BF16))
        kd = _heads((kk * jnp.exp(bl - b)).astype(BF16))
        decay = _heads(jnp.exp(bl))
        o = [_dot(qe[j], sts[j].astype(BF16), NT) for j in hs]
        parts = [[] for _ in hs]
        for lo, hi, ea, eb, mask in _intra_blocks(b):
            a, bk = _heads((qf[lo:hi] * ea).astype(BF16)), _heads((kk[:hi] * eb).astype(BF16))
            p = [jnp.where(mask, _dot(a[j], bk[j], NT), 0.0).astype(BF16) for j in hs]
            for j in hs:
                parts[j].append(_dot(p[j], vb[j][:hi]))
        o = [o[j] + jnp.concatenate(parts[j], axis=0) for j in hs]
        new = [sts[j] * decay[j] + _dot(vb[j], kd[j], TN) for j in hs]
        hgn = hgn_ref[...]
        on = [o[j] * lax.rsqrt(jnp.mean(o[j] * o[j], axis=-1, keepdims=True) + EPS) * hgn for j in hs]
        gg = g_ref[...]
        for j in hs:
            st_ref[j] = sts[j]
            s_scr[j] = new[j]
        o_ref[...] = _wide(o)
        y_ref[...] = (_wide(on) * (gg * _sigmoid(gg))).astype(BF16)

    wide = HG_BLOCK * HEAD_DIM
    groups = HG_HEADS // HG_BLOCK

    def col(k):
        return pl.BlockSpec((CHUNK, wide), lambda g, c: (c, k * groups + g))

    out = pl.BlockSpec((CHUNK, wide), lambda g, c: (c, g))
    return pl.pallas_call(
        body,
        name="hgrn_fwd",
        grid=(groups, nc),
        in_specs=[col(0), col(1), col(2), col(3), pl.BlockSpec((2, wide), lambda g, c: (0, g)), pl.BlockSpec((1, HEAD_DIM), lambda g, c: (0, 0))],
        out_specs=[out, out, pl.BlockSpec((HG_BLOCK, None, HEAD_DIM, HEAD_DIM), lambda g, c: (g, c, 0, 0))],
        out_shape=[
            jax.ShapeDtypeStruct((t, HG_WIDTH), BF16),
            jax.ShapeDtypeStruct((t, HG_WIDTH), F32),
            jax.ShapeDtypeStruct((HG_HEADS, nc, HEAD_DIM, HEAD_DIM), F32),
        ],
        scratch_shapes=[pltpu.VMEM((HG_BLOCK, HEAD_DIM, HEAD_DIM), F32)],
        compiler_params=_params("arbitrary", "arbitrary"),
    )(proj, proj, proj, proj, lb_logits, hg_norm)


def _hgrn_bwd(proj, lb_logits, hg_norm, o_hg, dycat, states, dep=None):
    t = proj.shape[0]
    nc = t // CHUNK

    def body(q_ref, f_ref, i_ref, g_ref, lbl_ref, hgn_ref, o_ref, dy_ref, st_ref, dp_ref, dlbl_ref, dhgn_ref, dst_scr, dlb_scr):
        h = pl.program_id(0)
        c = pl.program_id(1)

        @pl.when(c == 0)
        def _():
            dst_scr[...] = jnp.zeros_like(dst_scr)
            dlb_scr[...] = jnp.zeros_like(dlb_scr)

        @pl.when((c == 0) & (h == 0))
        def _():
            dhgn_ref[...] = jnp.zeros_like(dhgn_ref)

        hs = range(HG_BLOCK)
        hgn = _wide([hgn_ref[...]] * HG_BLOCK)
        q, fp, gg, vi = q_ref[...], f_ref[...], g_ref[...], i_ref[...]
        lb, sig, f, kk, sq, qf, b = _hgrn_gates(q, fp, lbl_ref[...])
        o, dy = o_ref[...], dy_ref[...]
        sg = _sigmoid(gg)
        n = _wide([oh * lax.rsqrt(jnp.mean(oh * oh, axis=-1, keepdims=True) + EPS) for oh in _heads(o)])
        don = dy * (gg * sg)
        dgg = dy * (n * hgn) * (sg * (1.0 + gg * (1.0 - sg)))
        d_hgn = sum(_heads(jnp.sum(don * n, axis=0, keepdims=True)))
        dn = don * hgn
        do = _wide(
            [
                lax.rsqrt(jnp.mean(oh * oh, axis=-1, keepdims=True) + EPS) * (dnh - nh * jnp.mean(dnh * nh, axis=-1, keepdims=True))
                for oh, dnh, nh in zip(_heads(o), _heads(dn), _heads(n))
            ]
        )
        sts = [st_ref[j] for j in hs]
        dstn = [dst_scr[j] for j in hs]
        bl = b[CHUNK - 1 : CHUNK, :]
        e_b, e_bl, e_l = jnp.exp(b), jnp.exp(bl - b), jnp.exp(bl)
        doh, vih = _heads(do), _heads(vi)
        dobh = _heads(do.astype(BF16))
        dq_acc = _wide([_dot3(doh[j], sts[j]) for j in hs]) * e_b
        dk_inter = _wide([_dot3(vih[j], dstn[j]) for j in hs]) * e_bl
        dk_acc = dk_inter
        kd = _heads((kk * e_bl).astype(BF16))
        dv_acc = _wide([_dot(kd[j], dstn[j].astype(BF16), NT) for j in hs])
        qe, decay = _heads((qf * e_b).astype(BF16)), _heads(e_l)
        dst_new = [dstn[j] * decay[j] + _dot(dobh[j], qe[j], TN) for j in hs]
        db_last = e_l * _wide([jnp.sum(sts[j] * dstn[j], axis=0, keepdims=True) for j in hs]) + jnp.sum(kk * dk_inter, axis=0, keepdims=True)
        dq_parts = []
        for lo, hi, ea, eb, mask in _intra_blocks(b):
            a, bk = qf[lo:hi] * ea, kk[:hi] * eb
            ah, bkh = _heads(a), _heads(bk)
            abh, bkbh = _heads(a.astype(BF16)), _heads(bk.astype(BF16))
            p = [jnp.where(mask, _dot(abh[j], bkbh[j], NT), 0.0).astype(BF16) for j in hs]
            dp = [jnp.where(mask, _dot3(doh[j][lo:hi], vih[j][:hi], NT), 0.0) for j in hs]
            dq_parts.append(_wide([_dot3(dp[j], bkh[j]) for j in hs]) * ea)
            dki = _wide([_dot3(dp[j], ah[j], TN) for j in hs]) * eb
            dvi = _wide([_dot(p[j], dobh[j][lo:hi], TN) for j in hs])
            if hi < CHUNK:
                zeros = jnp.zeros((CHUNK - hi, HG_BLOCK * HEAD_DIM), F32)
                dki = jnp.concatenate([dki, zeros], axis=0)
                dvi = jnp.concatenate([dvi, zeros], axis=0)
            dk_acc = dk_acc + dki
            dv_acc = dv_acc + dvi
        dq_acc = dq_acc + jnp.concatenate(dq_parts, axis=0)
        rows = lax.broadcasted_iota(jnp.int32, dq_acc.shape, 0)
        db = qf * dq_acc - kk * dk_acc + jnp.where(rows == CHUNK - 1, db_last, 0.0)
        dlf = jnp.dot(_tri(CHUNK, True), db, precision=HIGHEST, preferred_element_type=F32)
        dfk = dlf / f - dk_acc
        dp_ref[0] = (dq_acc * (sq * (1.0 + q * (1.0 - sq)))).astype(BF16)
        dp_ref[1] = ((1.0 - lb) * dfk * sig * (1.0 - sig)).astype(BF16)
        dp_ref[2] = dv_acc.astype(BF16)
        dp_ref[3] = dgg.astype(BF16)
        dlb_scr[...] += jnp.sum(dfk * (1.0 - sig), axis=0, keepdims=True)
        dhgn_ref[...] += d_hgn
        for j in hs:
            dst_scr[j] = dst_new[j]

        @pl.when(c == nc - 1)
        def _():
            dl0 = dlb_scr[...] * lb * (1.0 - lb)
            dlbl_ref[0:1, :] = dl0
            dlbl_ref[1:2, :] = -dl0

    wide = HG_BLOCK * HEAD_DIM
    groups = HG_HEADS // HG_BLOCK

    def col(k):
        return pl.BlockSpec((CHUNK, wide), lambda g, c: (nc - 1 - c, k * groups + g))

    blk = pl.BlockSpec((CHUNK, wide), lambda g, c: (nc - 1 - c, g))
    return _pallas(
        body,
        9,
        dep,
        name="hgrn_bwd",
        grid=(groups, nc),
        in_specs=[
            col(0), col(1), col(2), col(3),
            pl.BlockSpec((2, wide), lambda g, c: (0, g)),
            pl.BlockSpec((1, HEAD_DIM), lambda g, c: (0, 0)),
            blk, blk,
            pl.BlockSpec((HG_BLOCK, None, HEAD_DIM, HEAD_DIM), lambda g, c: (g, nc - 1 - c, 0, 0)),
        ],
        out_specs=[
            pl.BlockSpec((4, CHUNK, wide), lambda g, c: (0, nc - 1 - c, g)),
            pl.BlockSpec((2, wide), lambda g, c: (0, g)),
            pl.BlockSpec((1, HEAD_DIM), lambda g, c: (0, 0)),
        ],
        out_shape=[
            jax.ShapeDtypeStruct((4, t, HG_WIDTH), BF16),
            jax.ShapeDtypeStruct((2, HG_WIDTH), F32),
            jax.ShapeDtypeStruct((1, HEAD_DIM), F32),
        ],
        scratch_shapes=[pltpu.VMEM((HG_BLOCK, HEAD_DIM, HEAD_DIM), F32), pltpu.VMEM((1, wide), F32)],
        compiler_params=_params("arbitrary", "arbitrary"),
    )(proj, proj, proj, proj, lb_logits, hg_norm, o_hg, dycat, states)


def _diagonal_slots(shift):
    i = lax.broadcasted_iota(jnp.int32, (N_REL_PAD, DIAG), 0)
    u = lax.broadcasted_iota(jnp.int32, (N_REL_PAD, DIAG), 1)
    offset = u - shift if shift else jnp.where(u < K_BLOCK, u, u - DIAG)
    return jnp.where(jnp.clip(PAD - offset, -REL_CLIP, REL_CLIP) + REL_CLIP == i, 1.0, 0.0).astype(BF16)


def _split3(x):
    hi = x.astype(BF16)
    mid = (x - hi.astype(F32)).astype(BF16)
    return hi, mid, (x - hi.astype(F32) - mid.astype(F32)).astype(BF16)


def _bias_table(rel_bias):
    def body(rb_ref, o_ref, diag):
        h = pl.program_id(0)

        @pl.when(h == 0)
        def _():
            hi, mid, lo = _split3(rb_ref[...])
            slots = _diagonal_slots(0)
            diag[...] = _dot(hi, slots) + (_dot(mid, slots) + _dot(lo, slots))

        rows = jnp.broadcast_to(diag[pl.ds(h, 1), :], (Q_BLOCK, DIAG))
        o_ref[...] = pltpu.roll(rows, 0, 1, stride=1, stride_axis=0)[:, :K_BLOCK]

    return pl.pallas_call(
        body,
        name="bias_table",
        grid=(ATT_HEADS,),
        in_specs=[pl.BlockSpec((ATT_HEADS, N_REL_PAD), lambda h: (0, 0))],
        out_specs=pl.BlockSpec((None, Q_BLOCK, K_BLOCK), lambda h: (h, 0, 0)),
        out_shape=jax.ShapeDtypeStruct((ATT_HEADS, Q_BLOCK, K_BLOCK), F32),
        scratch_shapes=[pltpu.VMEM((ATT_HEADS, DIAG), F32)],
        compiler_params=_params("arbitrary"),
    )(rel_bias)


def _att_probs(q_ref, kpad, bias_ref, blk):
    qs = (q_ref[...] * ATT_SCALE).astype(BF16)
    start = pl.multiple_of(blk * Q_BLOCK, Q_BLOCK)
    kb = kpad[pl.ds(start, K_BLOCK), :]
    s = _dot(qs, kb, NT) + bias_ref[...]
    row = lax.broadcasted_iota(jnp.int32, (Q_BLOCK, K_BLOCK), 0)
    col = lax.broadcasted_iota(jnp.int32, (Q_BLOCK, K_BLOCK), 1)
    first = row - (row & (CHUNK - 1))
    valid = (col >= first) & (col < first + BAND) & (col + (blk * Q_BLOCK - PAD) >= 0)
    s = jnp.where(valid, s, jnp.finfo(F32).min)
    e = jnp.exp(s - jnp.max(s, axis=-1, keepdims=True))
    return qs, kb, start, e / jnp.sum(e, axis=-1, keepdims=True)


def _fill_padded(dst, src):
    dst[0:PAD, :] = jnp.zeros((PAD, HEAD_DIM), BF16)
    dst[PAD:, :] = src[...].astype(BF16)


def _att_fwd(proj, bias, dep=None):
    t = proj.shape[0]
    nb = t // Q_BLOCK

    def body(q_ref, k_ref, v_ref, bias_ref, y_ref, kpad, vpad):
        c = pl.program_id(1)

        @pl.when(c == 0)
        def _():
            _fill_padded(kpad, k_ref)
            _fill_padded(vpad, v_ref)

        _, _, start, p = _att_probs(q_ref, kpad, bias_ref, c)
        y_ref[...] = _dot(p.astype(BF16), vpad[pl.ds(start, K_BLOCK), :]).astype(BF16)

    base = 4 * HG_HEADS
    return _pallas(
        body,
        4,
        dep,
        name="att_fwd",
        grid=(ATT_HEADS, nb),
        in_specs=[
            pl.BlockSpec((Q_BLOCK, HEAD_DIM), lambda h, c: (c, base + h)),
            pl.BlockSpec((t, HEAD_DIM), lambda h, c: (0, base + ATT_HEADS + h)),
            pl.BlockSpec((t, HEAD_DIM), lambda h, c: (0, base + 2 * ATT_HEADS + h)),
            pl.BlockSpec((None, Q_BLOCK, K_BLOCK), lambda h, c: (h, 0, 0)),
        ],
        out_specs=pl.BlockSpec((Q_BLOCK, HEAD_DIM), lambda h, c: (c, h)),
        out_shape=jax.ShapeDtypeStruct((t, ATT_WIDTH), BF16),
        scratch_shapes=[pltpu.VMEM((t + PAD, HEAD_DIM), BF16), pltpu.VMEM((t + PAD, HEAD_DIM), BF16)],
        compiler_params=_params("arbitrary", "arbitrary"),
    )(proj, proj, proj, bias)


def _att_bwd(proj, bias, dycat, dep=None):
    t = proj.shape[0]
    nb = t // Q_BLOCK

    def body(q_ref, k_ref, v_ref, bias_ref, dy_ref, dq_ref, dk_ref, dv_ref, g_ref, kpad, vpad, dkacc, dvacc):
        c = pl.program_id(1)

        @pl.when(c == 0)
        def _():
            _fill_padded(kpad, k_ref)
            _fill_padded(vpad, v_ref)
            dkacc[...] = jnp.zeros_like(dkacc)
            dvacc[...] = jnp.zeros_like(dvacc)
            g_ref[...] = jnp.zeros_like(g_ref)

        qs, kb, start, p = _att_probs(q_ref, kpad, bias_ref, c)
        band = pl.ds(start, K_BLOCK)
        dyb = dy_ref[...].astype(BF16)
        dvacc[band, :] += _dot(p.astype(BF16), dyb, TN)
        dp = _dot(dyb, vpad[band, :], NT)
        ds = p * (dp - jnp.sum(dp * p, axis=-1, keepdims=True))
        g_ref[...] += ds
        dsb = ds.astype(BF16)
        dq_ref[...] = (_dot(dsb, kb) * ATT_SCALE).astype(BF16)
        dkacc[band, :] += _dot(dsb, qs, TN)

        @pl.when(c == nb - 1)
        def _():
            dk_ref[...] = dkacc[PAD:, :].astype(BF16)
            dv_ref[...] = dvacc[PAD:, :].astype(BF16)

    base = 4 * HG_HEADS
    whole = pl.BlockSpec((t, HEAD_DIM), lambda h, c: (0, h))
    return _pallas(
        body,
        5,
        dep,
        name="att_bwd",
        grid=(ATT_HEADS, nb),
        in_specs=[
            pl.BlockSpec((Q_BLOCK, HEAD_DIM), lambda h, c: (c, base + h)),
            pl.BlockSpec((t, HEAD_DIM), lambda h, c: (0, base + ATT_HEADS + h)),
            pl.BlockSpec((t, HEAD_DIM), lambda h, c: (0, base + 2 * ATT_HEADS + h)),
            pl.BlockSpec((None, Q_BLOCK, K_BLOCK), lambda h, c: (h, 0, 0)),
            pl.BlockSpec((Q_BLOCK, HEAD_DIM), lambda h, c: (c, HG_HEADS + h)),
        ],
        out_specs=[pl.BlockSpec((Q_BLOCK, HEAD_DIM), lambda h, c: (c, h)), whole, whole, pl.BlockSpec((None, Q_BLOCK, K_BLOCK), lambda h, c: (h, 0, 0))],
        out_shape=[
            jax.ShapeDtypeStruct((t, ATT_WIDTH), BF16),
            jax.ShapeDtypeStruct((t, ATT_WIDTH), BF16),
            jax.ShapeDtypeStruct((t, ATT_WIDTH), BF16),
            jax.ShapeDtypeStruct((ATT_HEADS, Q_BLOCK, K_BLOCK), F32),
        ],
        scratch_shapes=[
            pltpu.VMEM((t + PAD, HEAD_DIM), BF16),
            pltpu.VMEM((t + PAD, HEAD_DIM), BF16),
            pltpu.VMEM((t + PAD, HEAD_DIM), F32),
            pltpu.VMEM((t + PAD, HEAD_DIM), F32),
        ],
        compiler_params=_params("arbitrary", "arbitrary"),
    )(proj, proj, proj, bias, dycat)


def _rel_bias_grad(gsum):
    def body(g_ref, o_ref):
        r = lax.broadcasted_iota(jnp.int32, (Q_BLOCK, Q_BLOCK), 0)
        c = lax.broadcasted_iota(jnp.int32, (Q_BLOCK, Q_BLOCK), 1)
        flip = jnp.where(r + c == Q_BLOCK - 1, 1.0, 0.0).astype(BF16)
        sums = []
        for h in range(ATT_HEADS):
            hi, mid, lo = _split3(g_ref[h])
            rev = _dot(flip, hi) + (_dot(flip, mid) + _dot(flip, lo))
            wide = jnp.concatenate([rev, jnp.zeros((Q_BLOCK, DIAG - K_BLOCK), F32)], axis=1)
            sums.append(jnp.sum(pltpu.roll(wide, 0, 1, stride=1, stride_axis=0), axis=0, keepdims=True))
        hi, mid, lo = _split3(jnp.concatenate(sums, axis=0))
        slots = _diagonal_slots(Q_BLOCK - 1)
        o_ref[...] = _dot(hi, slots, NT) + (_dot(mid, slots, NT) + _dot(lo, slots, NT))

    return pl.pallas_call(
        body,
        name="rel_bias_grad",
        out_shape=jax.ShapeDtypeStruct((ATT_HEADS, N_REL_PAD), F32),
        compiler_params=_params(),
    )(gsum)


HALO = 16


FF_TILE = 1408
FF_TILES = D_FF // FF_TILE


def _interleave_cols(a):
    lead = a.shape[:-1]
    return jnp.swapaxes(a.reshape(*lead, 2, FF_TILES, FF_TILE), -3, -2).reshape(*lead, 2 * D_FF)


def _deinterleave_cols(a):
    lead = a.shape[:-1]
    return jnp.swapaxes(a.reshape(*lead, FF_TILES, 2, FF_TILE), -3, -2).reshape(*lead, 2 * D_FF)


def _ffn_specs(t, tm):
    wide = 2 * FF_TILE
    tile = pl.BlockSpec((tm, wide), lambda j, i: (i, j))
    before = pl.BlockSpec((HALO, wide), lambda j, i: (jnp.maximum(i * (tm // HALO) - 1, 0), j))
    after = pl.BlockSpec((HALO, wide), lambda j, i: (jnp.minimum((i + 1) * (tm // HALO), t // HALO - 1), j))
    vec = lambda rows: pl.BlockSpec((rows, wide), lambda j, i: (0, j))
    return tile, before, after, vec


def _conv(x, w, b, rows):
    taps = [pltpu.roll(x, 2, 0)[HALO : HALO + rows], pltpu.roll(x, 1, 0)[HALO : HALO + rows], x[HALO : HALO + rows]]
    return b + w[0:1] * taps[0] + w[1:2] * taps[1] + w[2:3] * taps[2], taps


def _ffn_act_fwd(u, conv_w, conv_b):
    t = u.shape[0]
    tm = _tile(t, (128,))
    tile, before, _, vec = _ffn_specs(t, tm)

    def body(u_ref, h_ref, w_ref, b_ref, z_ref):
        first = pl.program_id(1) == 0
        x = jnp.concatenate([jnp.where(first, 0.0, h_ref[...].astype(F32)), u_ref[...].astype(F32)], axis=0)
        c, _ = _conv(x, w_ref[...], b_ref[...], tm)
        gate, val = c[:, :FF_TILE], c[:, FF_TILE:]
        z_ref[...] = (gate * _sigmoid(gate) * val).astype(BF16)

    return pl.pallas_call(
        body,
        name="ffn_act_fwd",
        grid=(FF_TILES, t // tm),
        in_specs=[tile, before, vec(3), vec(1)],
        out_specs=pl.BlockSpec((tm, FF_TILE), lambda j, i: (i, j)),
        out_shape=jax.ShapeDtypeStruct((t, D_FF), BF16),
        compiler_params=_params("parallel", "parallel"),
    )(u, u, conv_w, conv_b)


def _ffn_act_bwd(u, dz, conv_w, conv_b, dep=None):
    t = u.shape[0]
    tm = _tile(t, (128,))
    nt = t // tm
    ext = tm + HALO
    tile, before, after, vec = _ffn_specs(t, tm)

    def body(u_ref, ub_ref, ua_ref, w_ref, b_ref, dz_ref, dza_ref, du_ref, dw_ref, db_ref):
        i = pl.program_id(1)
        first, last = i == 0, i == nt - 1
        parts = [jnp.where(first, 0.0, ub_ref[...].astype(F32)), u_ref[...].astype(F32), jnp.where(last, 0.0, ua_ref[...].astype(F32))]
        w = w_ref[...]
        c, taps = _conv(jnp.concatenate(parts, axis=0), w, b_ref[...], ext)
        gate, val = c[:, :FF_TILE], c[:, FF_TILE:]
        dz = jnp.concatenate([dz_ref[...].astype(F32), jnp.where(last, 0.0, dza_ref[...].astype(F32))], axis=0)
        sg = _sigmoid(gate)
        d = jnp.concatenate([dz * val * (sg * (1.0 + gate * (1.0 - sg))), dz * (gate * sg)], axis=1)
        du = w[2:3] * d[:tm] + w[1:2] * pltpu.roll(d, ext - 1, 0)[:tm] + w[0:1] * pltpu.roll(d, ext - 2, 0)[:tm]
        du_ref[...] = du.astype(BF16)

        @pl.when(first)
        def _():
            dw_ref[...] = jnp.zeros_like(dw_ref)
            db_ref[...] = jnp.zeros_like(db_ref)

        for k, tap in enumerate(taps):
            dw_ref[k : k + 1, :] += jnp.sum(d[:tm] * tap[:tm], axis=0, keepdims=True)
        db_ref[...] += jnp.sum(d[:tm], axis=0, keepdims=True)

    narrow = lambda rows, index: pl.BlockSpec((rows, FF_TILE), index)
    return _pallas(
        body,
        7,
        dep,
        name="ffn_act_bwd",
        grid=(FF_TILES, nt),
        in_specs=[
            tile, before, after, vec(3), vec(1),
            narrow(tm, lambda j, i: (i, j)),
            narrow(HALO, lambda j, i: (jnp.minimum((i + 1) * (tm // HALO), t // HALO - 1), j)),
        ],
        out_specs=[tile, vec(3), vec(1)],
        out_shape=[
            jax.ShapeDtypeStruct((t, 2 * D_FF), BF16),
            jax.ShapeDtypeStruct((3, 2 * D_FF), F32),
            jax.ShapeDtypeStruct((1, 2 * D_FF), F32),
        ],
        compiler_params=_params("parallel", "arbitrary"),
    )(u, u, u, conv_w, conv_b, dz, dz)


def _ple_loss(gpre, pp, h2, final_norm, target):
    t, d = h2.shape
    tm = _tile(t, (256,))

    def body(gp_ref, pp_ref, h_ref, g_ref, tg_ref, dh_ref, dgp_ref, dpp_ref, dg_ref, loss_ref):
        i = pl.program_id(0)
        gate = _sigmoid(gp_ref[...])
        ppv = pp_ref[...]
        h3 = h_ref[...] + gate * ppv
        r = lax.rsqrt(jnp.mean(h3 * h3, axis=-1, keepdims=True) + EPS)
        n = h3 * r
        g = g_ref[...]
        err = n * g - tg_ref[...]
        loss = 0.5 * jnp.sum(jnp.mean(err * err, axis=-1, keepdims=True))
        dy = err * (1.0 / d)
        dn = dy * g
        dh = r * (dn - n * jnp.mean(dn * n, axis=-1, keepdims=True))
        dh_ref[...] = dh
        dgp_ref[...] = (dh * ppv * gate * (1.0 - gate)).astype(BF16)
        dpp_ref[...] = (dh * gate).astype(BF16)
        dg = jnp.sum(dy * n, axis=0, keepdims=True)

        @pl.when(i == 0)
        def _():
            dg_ref[...] = dg
            loss_ref[...] = jnp.full(loss_ref.shape, loss, F32)

        @pl.when(i > 0)
        def _():
            dg_ref[...] += dg
            loss_ref[...] += loss

    row = pl.BlockSpec((tm, d), lambda i: (i, 0))
    vec = pl.BlockSpec((1, d), lambda i: (0, 0))
    return pl.pallas_call(
        body,
        name="ple_loss",
        grid=(t // tm,),
        in_specs=[row, row, row, vec, row],
        out_specs=[row, row, row, vec, pl.BlockSpec((8, 128), lambda i: (0, 0))],
        out_shape=[
            jax.ShapeDtypeStruct((t, d), F32),
            jax.ShapeDtypeStruct((t, d), BF16),
            jax.ShapeDtypeStruct((t, d), BF16),
            jax.ShapeDtypeStruct((1, d), F32),
            jax.ShapeDtypeStruct((8, 128), F32),
        ],
        compiler_params=_params("arbitrary"),
    )(gpre, pp, h2, final_norm, target)


def _adamw(w, g, m, v):
    m = ADAM_B1 * m + (1.0 - ADAM_B1) * g
    v = ADAM_B2 * v + (1.0 - ADAM_B2) * (g * g)
    m_hat = m / (1.0 - ADAM_B1 ** ADAM_STEP)
    v_hat = v / (1.0 - ADAM_B2 ** ADAM_STEP)
    return -ADAM_LR * (m_hat / (jnp.sqrt(v_hat) + ADAM_EPS) + ADAM_WD * w), m, v


def _adam_big(w, m, v, own, recv, name):
    r, c = w.shape
    tr = _tile(r, (256, 176))

    def body(w_ref, m_ref, v_ref, own_ref, recv_ref, g_ref, d_ref, nm_ref, nv_ref):
        g = own_ref[...]
        for k in range(3):
            g = g + recv_ref[k].astype(F32)
        g_ref[...] = g
        d_ref[...], nm_ref[...], nv_ref[...] = _adamw(w_ref[...], g, m_ref[...], v_ref[...])

    blk = pl.BlockSpec((tr, c), lambda i: (i, 0))
    return pl.pallas_call(
        body,
        name=name,
        grid=(r // tr,),
        in_specs=[blk, blk, blk, blk, pl.BlockSpec((3, tr, c), lambda i: (0, i, 0))],
        out_specs=[blk] * 4,
        out_shape=[jax.ShapeDtypeStruct((r, c), F32)] * 4,
        compiler_params=_params("parallel"),
    )(w, m, v, own, recv)


def _adam_small(w, g, m, v):
    def body(w_ref, g_ref, m_ref, v_ref, d_ref, nm_ref, nv_ref):
        d_ref[...], nm_ref[...], nv_ref[...] = _adamw(w_ref[...], g_ref[...], m_ref[...], v_ref[...])

    return pl.pallas_call(body, name="adam_small", out_shape=[jax.ShapeDtypeStruct(w.shape, F32)] * 3, compiler_params=_params())(w, g, m, v)


def _cast_bf16(w, name):
    r, c = w.shape
    tr = _tile(r, (256, 176))

    def body(w_ref, o_ref):
        o_ref[...] = w_ref[...].astype(BF16)

    blk = pl.BlockSpec((tr, c), lambda i: (i, 0))
    return pl.pallas_call(
        body, name=name, grid=(r // tr,), in_specs=[blk], out_specs=blk, out_shape=jax.ShapeDtypeStruct((r, c), BF16), compiler_params=_params("parallel")
    )(w)


def _position():
    return lax.axis_index("x"), lax.axis_index("y"), lax.axis_index("c")


def _other_chips(x, y):
    return [(1 - x, y), (x, 1 - y), (1 - x, 1 - y)]


def _block_index(dev, interleaved):
    x, y, c = dev
    return 4 * y + 2 * c + x if interleaved else 4 * x + 2 * y + c


def _shard_of(ref, axis, size, dev, interleaved=False):
    start = pl.multiple_of(_block_index(dev, interleaved) * size, 128 if axis == 1 else 16)
    return ref.at[:, pl.ds(start, size)] if axis == 1 else ref.at[pl.ds(start, size), :]


def _all_gather(shards, axes, interleaved):
    n = len(shards)

    def body(*refs):
        ins, outs = refs[:n], refs[n : 2 * n]
        send_sems, recv_sems, local_sems = refs[2 * n :]
        x, y, c = _position()
        me, sibling = (x, y, c), (x, y, 1 - c)
        chips = _other_chips(x, y)
        firsts, passed, locals_ = [], [], []
        for w in range(n):
            size = shards[w].shape[axes[w]]
            slot = functools.partial(_shard_of, outs[w], axes[w], size, interleaved=interleaved[w])

            def copy(k, block, to, src=None, w=w, slot=slot):
                return pltpu.make_async_remote_copy(
                    src_ref=slot(block) if src is None else src,
                    dst_ref=slot(block),
                    send_sem=send_sems.at[7 * w + k],
                    recv_sem=recv_sems.at[7 * w + k],
                    device_id=to,
                    device_id_type=MESH,
                )

            mine = pltpu.make_async_copy(ins[w], slot(me), local_sems.at[w])
            mine.start()
            locals_.append(mine)
            first = [copy(0, me, sibling, src=ins[w])] + [copy(1 + j, me, (*chip, c), src=ins[w]) for j, chip in enumerate(chips)]
            for cp in first:
                cp.start()
            firsts.append((first, copy))
        for w in range(n):
            first, copy = firsts[w]
            fwd = [copy(4 + j, (*chip, c), sibling) for j, chip in enumerate(chips)]
            for j, chip in enumerate(chips):
                copy(1 + j, (*chip, c), me).wait_recv()
                fwd[j].start()
            passed.append(fwd)
        for w in range(n):
            first, copy = firsts[w]
            copy(0, sibling, me).wait_recv()
            for j, chip in enumerate(chips):
                copy(4 + j, (*chip, 1 - c), me).wait_recv()
            for cp in first + passed[w]:
                cp.wait_send()
            locals_[w].wait()

    def full(s, ax):
        shape = list(s.shape)
        shape[ax] *= N_DEV
        return jax.ShapeDtypeStruct(tuple(shape), s.dtype)

    return pl.pallas_call(
        body,
        name="all_gather_weights",
        in_specs=[ANY] * n,
        out_specs=[ANY] * n,
        out_shape=[full(s, ax) for s, ax in zip(shards, axes)],
        scratch_shapes=[pltpu.SemaphoreType.DMA((7 * n,)), pltpu.SemaphoreType.DMA((7 * n,)), pltpu.SemaphoreType.DMA((n,))],
    )(*shards)


def _add_blocks(ids, grad, landed, axis, size, targets, out_dtype, name):
    rows = size if axis == 0 else grad.shape[0]
    cols = size if axis == 1 else grad.shape[1]
    tr = _tile(rows, (256, 176))
    nr = rows // tr
    nt = len(targets)

    def body(ids_ref, g_ref, l_ref, o_ref):
        o_ref[...] = (g_ref[...] + l_ref[...]).astype(out_dtype)

    if axis == 1:
        g_spec = pl.BlockSpec((tr, cols), lambda k, i, ids: (i, ids[targets[0] + k]))
    else:
        g_spec = pl.BlockSpec((tr, cols), lambda k, i, ids: (ids[targets[0] + k] * nr + i, 0))
    return pl.pallas_call(
        body,
        name=name,
        grid_spec=pltpu.PrefetchScalarGridSpec(
            num_scalar_prefetch=1,
            grid=(nt, nr),
            in_specs=[g_spec, pl.BlockSpec((None, tr, cols), lambda k, i, ids: (ids[4 + targets[0] + k], i, 0))],
            out_specs=pl.BlockSpec((None, tr, cols), lambda k, i, ids: (k, i, 0)),
        ),
        out_shape=jax.ShapeDtypeStruct((nt, rows, cols), out_dtype),
        compiler_params=_params("parallel", "parallel"),
    )(ids, grad, landed)


def _all_reduce_small(vec, name):
    rows = vec.shape[0]

    def body(v_ref, o_ref, land, send_sems, recv_sems):
        x, y, c = _position()
        mine = 4 * x + 2 * y + c
        copies = []
        for mask in range(1, N_DEV):
            peer = (1 - x if mask & 4 else x, 1 - y if mask & 2 else y, 1 - c if mask & 1 else c)
            copies.append(
                pltpu.make_async_remote_copy(
                    src_ref=v_ref, dst_ref=land.at[mine], send_sem=send_sems.at[mask - 1], recv_sem=recv_sems.at[mask - 1], device_id=peer, device_id_type=MESH
                )
            )
        for cp in copies:
            cp.start()
        land[mine] = v_ref[...]
        for cp in copies:
            cp.wait()
        acc = land[0]
        for k in range(1, N_DEV):
            acc = acc + land[k]
        o_ref[...] = acc

    return pl.pallas_call(
        body,
        name=name,
        out_shape=jax.ShapeDtypeStruct(vec.shape, F32),
        in_specs=[pl.BlockSpec(memory_space=pltpu.VMEM)],
        out_specs=pl.BlockSpec(memory_space=pltpu.VMEM),
        scratch_shapes=[pltpu.VMEM((N_DEV, rows, 128), F32), pltpu.SemaphoreType.DMA((N_DEV - 1,)), pltpu.SemaphoreType.DMA((N_DEV - 1,))],
    )(vec)


def _rows128(a, rows):
    flat = a.reshape(-1)
    return jnp.pad(flat, (0, rows * 128 - flat.shape[0])).reshape(rows, 128)


def _pad_rel(a):
    return jnp.pad(a.reshape(ATT_HEADS, -1)[:, :N_REL], ((0, 0), (0, N_REL_PAD - N_REL)))


SMALL = [("norm_mix", 16), ("lb_logits", 16), ("hg_norm", 8), ("rel_bias", 24), ("norm_ffn", 16), ("conv_b", 88), ("norm_ple", 16), ("final_norm", 16)]
CONV_W_FULL_ROWS = 3 * 2 * D_FF // 128
CONV_W_SHARD_ROWS = 40


def _pack_small(parts):
    return jnp.concatenate([_rows128(_pad_rel(parts[k]) if k == "rel_bias" else parts[k], rows) for k, rows in SMALL], axis=0)


def _unpack_small(packed, shapes):
    out, at = {}, 0
    for k, rows in SMALL:
        blk = packed[at : at + rows]
        at += rows
        if k == "rel_bias":
            out[k] = blk.reshape(ATT_HEADS, N_REL_PAD)[:, :N_REL].reshape(shapes[k])
        else:
            n = 1
            for s in shapes[k]:
                n *= s
            out[k] = blk.reshape(-1)[:n].reshape(shapes[k])
    return out, at


BIG = [("w_in", 1), ("w_out", 0), ("w_up", 1), ("w_down", 0), ("w_ple_gate", 0), ("w_ple_proj", 1)]


HBM = pl.BlockSpec(memory_space=pltpu.HBM)
SEM = pl.BlockSpec(memory_space=pltpu.SEMAPHORE)
EFFECT = pltpu.SideEffectType.DATAFLOW_SIDE_EFFECTING


def _copies(plan, refs, send_sems, recv_sems):
    return [
        pltpu.make_async_remote_copy(src_ref=src, dst_ref=dst, send_sem=send_sems.at[i], recv_sem=recv_sems.at[i], device_id=dev, device_id_type=MESH)
        for i, (src, dst, dev) in enumerate(plan(refs))
    ]


def _split_start(name, arrays, plan, n):
    k = len(arrays)

    def body(*refs):
        for cp in _copies(plan, refs[:k], refs[k], refs[k + 1]):
            cp.start()
        refs[-1][...] = jnp.zeros_like(refs[-1])

    out = pl.pallas_call(
        body,
        name=name,
        out_shape=(pltpu.SemaphoreType.DMA((n,)), pltpu.SemaphoreType.DMA((n,)), *[pltpu.HBM(a.shape, a.dtype) for a in arrays], jax.ShapeDtypeStruct((8, 128), F32)),
        in_specs=[HBM] * k,
        out_specs=(SEM, SEM, *[HBM] * k, pl.BlockSpec(memory_space=pltpu.VMEM)),
        input_output_aliases={i: 2 + i for i in range(k)},
        compiler_params=pltpu.CompilerParams(has_side_effects=EFFECT),
    )(*[pltpu.with_memory_space_constraint(a, pltpu.HBM) for a in arrays])
    return out[0], out[1], list(out[2 : 2 + k]), out[-1]


def _split_wait(name, send, recv, arrays, plan, after):
    k = len(arrays)

    def body(*refs):
        for cp in _copies(plan, refs[:k], refs[k], refs[k + 1]):
            cp.wait_send()
            cp.wait_recv()

    out = pl.pallas_call(
        body,
        name=name,
        out_shape=tuple(pltpu.HBM(a.shape, a.dtype) for a in arrays),
        in_specs=[HBM] * k + [SEM, SEM, ANY],
        out_specs=tuple([HBM] * k),
        input_output_aliases={i: i for i in range(k)},
        compiler_params=pltpu.CompilerParams(has_side_effects=EFFECT),
    )(*arrays, send, recv, after)
    return list(out)


def _cast_into(w, me, axis, name, dep):
    r, c = w.shape
    tr = _tile(r, (256, 176))
    nr = r // tr

    def body(me_ref, w_ref, dep_ref, o_ref):
        o_ref[...] = w_ref[...].astype(BF16)

    if axis == 1:
        shape, o_spec = (r, N_DEV * c), pl.BlockSpec((tr, c), lambda i, me: (i, me[0]))
    else:
        shape, o_spec = (N_DEV * r, c), pl.BlockSpec((tr, c), lambda i, me: (me[0] * nr + i, 0))
    return pl.pallas_call(
        body,
        name=name,
        grid_spec=pltpu.PrefetchScalarGridSpec(
            num_scalar_prefetch=1, grid=(nr,), in_specs=[pl.BlockSpec((tr, c), lambda i, me: (i, 0)), ANY], out_specs=o_spec
        ),
        out_shape=jax.ShapeDtypeStruct(shape, BF16),
        compiler_params=_params("parallel"),
    )(me, w, dep)


GATHER = [
    (["w_out"], None, "att_fwd", None),
    (["w_up"], None, "att_fwd", "norm_ffn_fwd"),
    (["w_down", "w_ple_gate", "w_ple_proj"], "att_fwd", "up_proj", "ffn_act_fwd"),
]
GROUPS = [["w_ple_proj", "w_ple_gate", "w_down"], ["w_up"], ["w_out"], ["w_in"]]
STAGES = ["ffn_act_bwd", "d_mix_out", "hgrn_bwd", "d_norm_mix_out"]
INTERLEAVED = {"w_up"}


class _Exchange:
    def __init__(self, big, position):
        self.big, self.axis = big, dict(BIG)
        self.size = {k: big[k].shape[self.axis[k]] for k in big}
        self.x, self.y, self.c = position
        chips = [(self.x, self.y)] + _other_chips(self.x, self.y)
        landed = [2 * cx + cy for cx, cy in chips]
        self.ids = {
            flag: jnp.stack([_block_index((cx, cy, self.c), flag) for cx, cy in chips] + landed).astype(jnp.int32) for flag in (False, True)
        }
        self.tokens, self.grads, self.state, self.wfull = [], {}, {}, {}


    def _slot(self, ref, k, dev):
        return _shard_of(ref, self.axis[k], self.size[k], dev, interleaved=k in INTERLEAVED)

    def _plan_gather(self, names, direct, refs):
        x, y, c = _position()
        me, out = (x, y, c), []
        for k, ref in zip(names, refs):
            mine = self._slot(ref, k, me)
            out.append((mine, mine, (x, y, 1 - c)))
            out += [(mine, mine, (*chip, c)) for chip in _other_chips(x, y)]
            if direct:
                out += [(mine, mine, (*chip, 1 - c)) for chip in _other_chips(x, y)]
        return out

    def _plan_forward(self, names, refs):
        x, y, c = _position()
        out = []
        for k, ref in zip(names, refs):
            for chip in _other_chips(x, y):
                block = self._slot(ref, k, (*chip, c))
                out.append((block, block, (x, y, 1 - c)))
        return out

    def _plan_sibling(self, names, refs):
        x, y, c = _position()
        n = len(names)
        return [(self._slot(refs[i], k, (p // 2, p % 2, 1 - c)), refs[n + i].at[p], (x, y, 1 - c)) for i, k in enumerate(names) for p in range(4)]

    def _plan_chips(self, names, refs):
        x, y, c = _position()
        n = len(names)
        return [(refs[i].at[j], refs[n + i].at[j], (*chip, c)) for i in range(n) for j, chip in enumerate(_other_chips(x, y))]


    def gather(self, conv_w):
        w_in, conv_full = _all_gather([_cast_bf16(self.big["w_in"], "cast_w_in"), conv_w], [1, 1], [False, True])
        self.wfull["w_in"] = w_in
        me = {flag: _block_index((self.x, self.y, self.c), flag).astype(jnp.int32).reshape(1) for flag in (False, True)}
        self.late, self.unsent = {}, {}
        after = w_in
        for gi, (names, issued, *_) in enumerate(GATHER):
            self.unsent[gi] = [_cast_into(self.big[k], me[k in INTERLEAVED], self.axis[k], "cast_" + k, after) for k in names]
            if issued is None:
                self._issue(None)
                after = self.tokens[-1]
        return conv_full

    def _issue(self, stage):
        for gi, (names, issued, _, forwarded) in enumerate(GATHER):
            if issued == stage and gi in self.unsent:
                plan = functools.partial(self._plan_gather, names, forwarded is None)
                copies = (7 if forwarded is None else 4) * len(names)
                send, recv, fulls, token = _split_start(f"gather_start_{gi}", self.unsent.pop(gi), plan, copies)
                self.tokens.append(token)
                self.late[gi] = (send, recv, fulls, plan)

    def weight(self, k):
        return self.wfull[k]

    def dep(self):
        tokens, self.tokens = self.tokens, []
        return tokens

    def reduce(self, vec, name):
        return _all_reduce_small(vec, name)

    def grad(self, k, g):
        self.grads[k] = g
        for gi, names in enumerate(GROUPS):
            if k == names[-1]:
                plan = functools.partial(self._plan_sibling, names)
                lands = [lax.empty((4, *self._shard_shape(n)), F32) for n in names]
                send, recv, arrays, token = _split_start(f"sibling_start_{gi}", [self.grads[n] for n in names] + lands, plan, 4 * len(names))
                self.tokens.append(token)
                self.state[gi] = (send, recv, arrays, plan)

    def done(self, stage, after):
        for gi, (names, _, _, forwarded) in enumerate(GATHER):
            if forwarded == stage:
                send, recv, fulls, plan = self.late[gi]
                self.wfull.update(zip(names, _split_wait(f"forward_wait_{gi}", send, recv, fulls, plan, after)))
        for gi, (names, _, arrived, forwarded) in enumerate(GATHER):
            if arrived == stage:
                send, recv, fulls, plan = self.late[gi]
                fulls = _split_wait(f"gather_wait_{gi}", send, recv, fulls, plan, after)
                if forwarded is None:
                    self.wfull.update(zip(names, fulls))
                else:
                    plan = functools.partial(self._plan_forward, names)
                    send, recv, fulls, token = _split_start(f"forward_start_{gi}", fulls, plan, 3 * len(names))
                    self.tokens.append(token)
                    self.late[gi] = (send, recv, fulls, plan)
        self._issue(stage)
        if stage in STAGES:
            self._to_chips(STAGES.index(stage), after)

    def _shard_shape(self, k):
        shape = list(self.grads[k].shape)
        shape[self.axis[k]] = self.size[k]
        return tuple(shape)

    def _to_chips(self, gi, after):
        names = GROUPS[gi]
        n = len(names)
        send, recv, arrays, plan = self.state[gi]
        arrays = _split_wait(f"sibling_wait_{gi}", send, recv, arrays, plan, after)
        own, parts = [], []
        for k, g, land in zip(names, arrays[:n], arrays[n:]):
            ids = self.ids[k in INTERLEAVED]
            own.append(_add_blocks(ids, g, land, self.axis[k], self.size[k], [0], F32, "add_own_" + k)[0])
            parts.append(_add_blocks(ids, g, land, self.axis[k], self.size[k], [1, 2, 3], BF16, "add_send_" + k))
        plan = functools.partial(self._plan_chips, names)
        lands = [lax.empty(part.shape, BF16) for part in parts]
        send, recv, arrays, token = _split_start(f"chips_start_{gi}", parts + lands, plan, 3 * n)
        self.tokens.append(token)
        self.state[gi] = (send, recv, arrays, plan, own)

    def finish(self, gi, after):
        names = GROUPS[gi]
        send, recv, arrays, plan, own = self.state[gi]
        arrays = _split_wait(f"chips_wait_{gi}", send, recv, arrays, plan, after)
        return {k: (o, r) for k, o, r in zip(names, own, arrays[len(names) :])}


class _Resident:
    def __init__(self, wfull):
        self.wfull, self.grads = wfull, {}

    def weight(self, k):
        return self.wfull[k]

    def grad(self, k, g):
        self.grads[k] = g

    def dep(self):
        return None

    def reduce(self, vec, name):
        return vec

    def done(self, stage, after):
        pass


def _local_step(x, p, target, small, conv_w, ex):
    a1, r1 = _rms_fwd(x, small["norm_mix"], "norm_mix_fwd", dep=ex.dep())
    proj = _matmul(a1, ex.weight("w_in"), "nn", F32, "in_proj")
    bias = _bias_table(jnp.pad(small["rel_bias"], ((0, 0), (0, N_REL_PAD - N_REL))))
    y_hg, o_hg, states = _hgrn_fwd(proj, small["lb_logits"], small["hg_norm"])
    y_att = _att_fwd(proj, bias, dep=ex.dep())
    ex.done("att_fwd", y_att)
    ycat = jnp.concatenate([y_hg, y_att], axis=1)
    h1 = _matmul(ycat, ex.weight("w_out"), "nn", F32, "out_proj", resid=x, dep=ex.dep())
    a2, r2 = _rms_fwd(h1, small["norm_ffn"], "norm_ffn_fwd")
    ex.done("norm_ffn_fwd", a2)
    u = _matmul(a2, ex.weight("w_up"), "nn", BF16, "up_proj")
    conv_b = _interleave_cols(small["conv_b"])
    ex.done("up_proj", u)
    z = _ffn_act_fwd(u, conv_w, conv_b)
    ex.done("ffn_act_fwd", z)
    h2 = _matmul(z, ex.weight("w_down"), "nn", F32, "down_proj", tk=2816, resid=h1)
    a3, r3 = _rms_fwd(h2, small["norm_ple"], "norm_ple_fwd")
    gpre = _matmul(a3, ex.weight("w_ple_gate"), "nn", F32, "ple_gate")
    pp = _matmul(p, ex.weight("w_ple_proj"), "nn", F32, "ple_proj")
    dh3, dgpre, dpp, d_final, loss = _ple_loss(gpre, pp, h2, small["final_norm"], target)

    ex.grad("w_ple_proj", _matmul(p, dpp, "tn", F32, "d_w_ple_proj", tk=2048))
    ex.grad("w_ple_gate", _matmul(a3, dgpre, "tn", F32, "d_w_ple_gate", tk=2048))
    da3 = _matmul(dgpre, ex.weight("w_ple_gate"), "nt", F32, "d_norm_ple_out")
    dh2, d_ple = _rms_bwd(da3, h2, r3, small["norm_ple"], dh3, "norm_ple_bwd")
    dz = _matmul(dh2, ex.weight("w_down"), "nt", BF16, "d_ffn_act")
    ex.grad("w_down", _matmul(z, dh2, "tn", F32, "d_w_down", tk=2048))
    du, dcw, dcb = _ffn_act_bwd(u, dz, conv_w, conv_b, dep=ex.dep())
    ex.done("ffn_act_bwd", du)
    d_conv_w, d_conv_b = _deinterleave_cols(dcw), _deinterleave_cols(dcb)
    ex.grad("w_up", _matmul(a2, du, "tn", F32, "d_w_up", tk=2048, dep=ex.dep()))
    da2 = _matmul(du, ex.weight("w_up"), "nt", F32, "d_norm_ffn_out", tk=2816, dep=ex.dep())
    dh1, d_ffn = _rms_bwd(da2, h1, r2, small["norm_ffn"], dh2, "norm_ffn_bwd")
    dycat = _matmul(dh1, ex.weight("w_out"), "nt", F32, "d_mix_out")
    ex.done("d_mix_out", dycat)
    ex.grad("w_out", _matmul(ycat, dh1, "tn", F32, "d_w_out", tk=2048, dep=ex.dep()))
    dp_hg, d_lb, d_hgn = _hgrn_bwd(proj, small["lb_logits"], small["hg_norm"], o_hg, dycat, states, dep=ex.dep())
    ex.done("hgrn_bwd", d_lb)
    dq_att, dk_att, dv_att, gsum = _att_bwd(proj, bias, dycat, dep=ex.dep())
    d_rel = _rel_bias_grad(gsum)
    d_small = {
        "norm_mix": jnp.zeros_like(small["norm_mix"]), "lb_logits": d_lb, "hg_norm": d_hgn, "rel_bias": d_rel, "norm_ffn": d_ffn,
        "conv_b": d_conv_b, "norm_ple": d_ple, "final_norm": d_final,
    }
    packed = jnp.concatenate([_pack_small(d_small), _rows128(d_conv_w, CONV_W_FULL_ROWS), _rows128(loss[0:1, 0:1], 8)], axis=0)
    early = ex.reduce(packed, "all_reduce_small")
    dproj = jnp.concatenate([dp_hg[0], dp_hg[1], dp_hg[2], dp_hg[3], dq_att, dk_att, dv_att], axis=1)
    ex.grad("w_in", _matmul(a1, dproj, "tn", F32, "d_w_in", tk=2048, dep=[early]))
    da1 = _matmul(dproj, ex.weight("w_in"), "nt", F32, "d_norm_mix_out", tk=1792, dep=ex.dep())
    dx, d_mix = _rms_bwd(da1, x, r1, small["norm_mix"], dh1, "norm_mix_bwd")
    rows = dict(SMALL)["norm_mix"]
    late = ex.reduce(_rows128(d_mix, rows), "all_reduce_norm_mix")
    ex.done("d_norm_mix_out", late)
    return dx, jnp.concatenate([late, early[rows:]], axis=0)


def kernel(x, p, norm_mix, w_in, lb_logits, hg_norm, rel_bias, w_out, norm_ffn, w_up, conv_w, conv_b, w_down, norm_ple, w_ple_gate, w_ple_proj, final_norm, loss_target, m_norm_mix, m_w_in, m_lb_logits, m_hg_norm, m_rel_bias, m_w_out, m_norm_ffn, m_w_up, m_conv_w, m_conv_b, m_w_down, m_norm_ple, m_w_ple_gate, m_w_ple_proj, m_final_norm, v_norm_mix, v_w_in, v_lb_logits, v_hg_norm, v_rel_bias, v_w_out, v_norm_ffn, v_w_up, v_conv_w, v_conv_b, v_w_down, v_norm_ple, v_w_ple_gate, v_w_ple_proj, v_final_norm):
    given = dict(locals())
    mx, my, mc = _position()
    me = 4 * mx + 2 * my + mc
    big = {k: given[k][0] for k, _ in BIG}
    ex = _Exchange(big, (mx, my, mc))
    conv_w_full = ex.gather(conv_w[0])

    small = {
        "norm_mix": norm_mix, "lb_logits": lb_logits, "hg_norm": hg_norm, "rel_bias": rel_bias[0], "norm_ffn": norm_ffn,
        "conv_b": conv_b, "norm_ple": norm_ple, "final_norm": final_norm.reshape(1, -1),
    }
    dx, reduced = _local_step(x[0], p[0, 0], loss_target[0], small, conv_w_full, ex)

    out = {}
    for gi in range(len(GROUPS)):
        for k, (o, r) in ex.finish(gi, reduced).items():
            g, d, nm, nv = _adam_big(big[k], given["m_" + k][0], given["v_" + k][0], o, r, "adam_" + k)
            out[k] = tuple(a[None] for a in (g, d, nm, nv))
    shapes = {k: given[k].shape for k, _ in SMALL}
    g_small, at = _unpack_small(reduced, shapes)
    g_conv_full = reduced[at : at + CONV_W_FULL_ROWS].reshape(3, 2 * D_FF)
    total_loss = reduced[at + CONV_W_FULL_ROWS, 0]
    cw = conv_w.shape[2]
    g_conv = lax.dynamic_slice_in_dim(g_conv_full, me * cw, cw, axis=1)

    def pack_with_conv(parts, conv_part):
        return jnp.concatenate([_pack_small(parts), _rows128(conv_part, CONV_W_SHARD_ROWS)], axis=0)

    d_pk, m_pk, v_pk = _adam_small(
        pack_with_conv({k: given[k] for k, _ in SMALL}, conv_w),
        pack_with_conv(g_small, g_conv),
        pack_with_conv({k: given["m_" + k] for k, _ in SMALL}, m_conv_w),
        pack_with_conv({k: given["v_" + k] for k, _ in SMALL}, v_conv_w),
    )
    for name, pk in (("d", d_pk), ("m", m_pk), ("v", v_pk)):
        parts, at = _unpack_small(pk, shapes)
        parts["conv_w"] = pk[at : at + CONV_W_SHARD_ROWS].reshape(-1)[: 3 * cw].reshape(conv_w.shape)
        for k, a in parts.items():
            out.setdefault(k, {})
            out[k][name] = a
    for k, _ in SMALL:
        out[k]["g"] = g_small[k]
    out["conv_w"]["g"] = g_conv.reshape(conv_w.shape)

    order = ["norm_mix", "w_in", "lb_logits", "hg_norm", "rel_bias", "w_out", "norm_ffn", "w_up", "conv_w", "conv_b", "w_down", "norm_ple", "w_ple_gate", "w_ple_proj", "final_norm"]

    def pick(k, what):
        return out[k][what] if isinstance(out[k], dict) else out[k][{"g": 0, "d": 1, "m": 2, "v": 3}[what]]

    return (total_loss, dx[None], *[pick(k, "g") for k in order], *[pick(k, "d") for k in order], *[pick(k, "m") for k in order], *[pick(k, "v") for k in order])
```

```python
import functools

import jax
import jax.numpy as jnp
from jax import lax
from jax.experimental import pallas as pl
from jax.experimental.pallas import tpu as pltpu

F32 = jnp.float32
BF16 = jnp.bfloat16

D_MODEL = 2048
CHUNK = 64
HG_HEADS = 8
HEAD_DIM = 128
HG_WIDTH = HG_HEADS * HEAD_DIM
ATT_HEADS = 8
ATT_WIDTH = ATT_HEADS * HEAD_DIM
LEFT_CHUNKS = 8
PAD = LEFT_CHUNKS * CHUNK
BAND = PAD + CHUNK
REL_CLIP = 128
N_REL = 2 * REL_CLIP + 1
N_REL_PAD = 384
D_FF = 5632
EPS = 1e-6
ATT_SCALE = HEAD_DIM ** -0.5
SUB = 32
HG_BLOCK = 8
Q_BLOCK = 4 * CHUNK
K_BLOCK = Q_BLOCK + PAD
DIAG = 1024
MASKED = -1e30

ADAM_LR = 0.001
ADAM_B1 = 0.9
ADAM_B2 = 0.999
ADAM_EPS = 1e-08
ADAM_WD = 0.01
ADAM_STEP = 10

N_DEV = 8
VMEM_LIMIT = 48 * 1024 * 1024
MESH = pl.DeviceIdType.MESH
ANY = pl.BlockSpec(memory_space=pl.ANY)
HIGHEST = lax.Precision.HIGHEST

NN = (((1,), (0,)), ((), ()))
NT = (((1,), (1,)), ((), ()))
TN = (((0,), (0,)), ((), ()))


def _params(*sem):
    return pltpu.CompilerParams(dimension_semantics=sem if sem else None, vmem_limit_bytes=VMEM_LIMIT)


def _pallas(body, n_in, dep, **kw):
    deps = [] if dep is None else list(dep)
    if not deps:
        return pl.pallas_call(body, **kw)

    def body_after(*refs):
        body(*refs[:n_in], *refs[n_in + len(deps) :])

    call = pl.pallas_call(body_after, **dict(kw, in_specs=list(kw["in_specs"]) + [ANY] * len(deps)))
    return lambda *ops: call(*ops, *deps)


def _dot(a, b, dims=NN):
    return lax.dot_general(a, b, dims, preferred_element_type=F32)


def _dot3(a, b, dims=NN):
    a_hi, b_hi = a.astype(BF16), b.astype(BF16)
    a_lo, b_lo = (a - a_hi.astype(F32)).astype(BF16), (b - b_hi.astype(F32)).astype(BF16)
    return _dot(a_hi, b_hi, dims) + (_dot(a_hi, b_lo, dims) + _dot(a_lo, b_hi, dims))


def _sigmoid(x):
    return 1.0 / (1.0 + jnp.exp(-x))


def _tile(n, prefs):
    for t in prefs:
        if n % t == 0:
            return t
    return n


def _matmul(a, b, mode, out_dtype, name, tm=1024, tn=1024, tk=None, resid=None, dep=None):
    if mode == "nn":
        (m, k), n = a.shape, b.shape[1]
    elif mode == "nt":
        (m, k), n = a.shape, b.shape[0]
    else:
        (k, m), n = a.shape, b.shape[1]
    tm = _tile(m, (tm, 512, 256, 128))
    tn = _tile(n, (tn, 1408, 512, 256, 128))
    tk = k if tk is None else _tile(k, (tk,))
    nk = k // tk
    dims = {"nn": NN, "nt": NT, "tn": TN}[mode]
    a_spec = pl.BlockSpec((tk, tm), lambda i, j, s: (s, i)) if mode == "tn" else pl.BlockSpec((tm, tk), lambda i, j, s: (i, s))
    b_spec = pl.BlockSpec((tn, tk), lambda i, j, s: (j, s)) if mode == "nt" else pl.BlockSpec((tk, tn), lambda i, j, s: (s, j))
    o_spec = pl.BlockSpec((tm, tn), lambda i, j, s: (i, j))
    has_res = resid is not None

    def body(*refs):
        a_ref, b_ref = refs[0], refs[1]
        o_ref = refs[2 + has_res]
        part = _dot(a_ref[...].astype(BF16), b_ref[...].astype(BF16), dims)

        def finish(acc):
            if has_res:
                acc = acc + refs[2][...]
            o_ref[...] = acc.astype(out_dtype)

        if nk == 1:
            finish(part)
        else:
            acc_ref = refs[-1]
            s = pl.program_id(2)

            @pl.when(s == 0)
            def _():
                acc_ref[...] = part

            @pl.when(s > 0)
            def _():
                acc_ref[...] += part

            @pl.when(s == nk - 1)
            def _():
                finish(acc_ref[...])

    return _pallas(
        body,
        2 + has_res,
        dep,
        name=name,
        grid=(m // tm, n // tn, nk),
        in_specs=[a_spec, b_spec] + ([o_spec] if has_res else []),
        out_specs=o_spec,
        out_shape=jax.ShapeDtypeStruct((m, n), out_dtype),
        scratch_shapes=[pltpu.VMEM((tm, tn), F32)] if nk > 1 else [],
        compiler_params=_params("parallel", "parallel", "arbitrary"),
    )(*([a, b] + ([resid] if has_res else [])))


def _rms_fwd(x, g, name, dep=None):
    t, d = x.shape
    tm = _tile(t, (256,))

    def body(x_ref, g_ref, a_ref, r_ref):
        xv = x_ref[...]
        r = lax.rsqrt(jnp.mean(xv * xv, axis=-1, keepdims=True) + EPS)
        a_ref[...] = (xv * r * g_ref[...]).astype(BF16)
        r_ref[...] = r

    row = pl.BlockSpec((tm, d), lambda i: (i, 0))
    return _pallas(
        body,
        2,
        dep,
        name=name,
        grid=(t // tm,),
        in_specs=[row, pl.BlockSpec((1, d), lambda i: (0, 0))],
        out_specs=[row, pl.BlockSpec((tm, 1), lambda i: (i, 0))],
        out_shape=[jax.ShapeDtypeStruct((t, d), BF16), jax.ShapeDtypeStruct((t, 1), F32)],
        compiler_params=_params("parallel"),
    )(x, g)


def _rms_bwd(da, x, r, g, resid, name, dep=None):
    t, d = x.shape
    tm = _tile(t, (256,))

    def body(da_ref, x_ref, r_ref, g_ref, res_ref, dx_ref, dg_ref):
        i = pl.program_id(0)
        rv = r_ref[...]
        n = x_ref[...] * rv
        dav = da_ref[...]
        dn = dav * g_ref[...]
        dx_ref[...] = rv * (dn - n * jnp.mean(dn * n, axis=-1, keepdims=True)) + res_ref[...]
        part = jnp.sum(dav * n, axis=0, keepdims=True)

        @pl.when(i == 0)
        def _():
            dg_ref[...] = part

        @pl.when(i > 0)
        def _():
            dg_ref[...] += part

    row = pl.BlockSpec((tm, d), lambda i: (i, 0))
    vec = pl.BlockSpec((1, d), lambda i: (0, 0))
    return _pallas(
        body,
        5,
        dep,
        name=name,
        grid=(t // tm,),
        in_specs=[row, row, pl.BlockSpec((tm, 1), lambda i: (i, 0)), vec, row],
        out_specs=[row, vec],
        out_shape=[jax.ShapeDtypeStruct((t, d), F32), jax.ShapeDtypeStruct((1, d), F32)],
        compiler_params=_params("arbitrary"),
    )(da, x, r, g, resid)


def _tri(n, upper):
    r = lax.broadcasted_iota(jnp.int32, (n, n), 0)
    c = lax.broadcasted_iota(jnp.int32, (n, n), 1)
    return jnp.where((c >= r) if upper else (c <= r), 1.0, 0.0).astype(F32)


def _hgrn_gates(q, fp, lbl):
    l0, l1 = lbl[0:1, :], lbl[1:2, :]
    mx = jnp.maximum(l0, l1)
    e0, e1 = jnp.exp(l0 - mx), jnp.exp(l1 - mx)
    lb = e0 / (e0 + e1)
    sig = _sigmoid(fp)
    f = lb + (1.0 - lb) * sig
    kk = (1.0 - lb) * _sigmoid(-fp)
    sq = _sigmoid(q)
    b = jnp.dot(_tri(CHUNK, False), jnp.log(f), precision=HIGHEST, preferred_element_type=F32)
    return lb, sig, f, kk, sq, q * sq, b


def _heads(x):
    return [x[:, j * HEAD_DIM : (j + 1) * HEAD_DIM] for j in range(x.shape[1] // HEAD_DIM)]


def _wide(parts):
    return jnp.concatenate(parts, axis=1)


def _intra_blocks(b):
    out = []
    for lo in range(0, CHUNK, SUB):
        hi = lo + SUB
        br = b[lo + SUB // 2 : lo + SUB // 2 + 1, :]
        row = lax.broadcasted_iota(jnp.int32, (SUB, hi), 0) + lo
        col = lax.broadcasted_iota(jnp.int32, (SUB, hi), 1)
        out.append((lo, hi, jnp.exp(b[lo:hi] - br), jnp.exp(br - b[:hi]), col <= row))
    return out


def _hgrn_fwd(proj, lb_logits, hg_norm):
    t = proj.shape[0]
    nc = t // CHUNK

    def body(q_ref, f_ref, i_ref, g_ref, lbl_ref, hgn_ref, y_ref, o_ref, st_ref, s_scr):
        c = pl.program_id(1)

        @pl.when(c == 0)
        def _():
            s_scr[...] = jnp.zeros_like(s_scr)

        hs = range(HG_BLOCK)
        sts = [s_scr[j] for j in hs]
        _, _, _, kk, _, qf, b = _hgrn_gates(q_ref[...], f_ref[...], lbl_ref[...])
        vb = _heads(i_ref[...].astype(BF16))
        bl = b[CHUNK - 1 : CHUNK, :]
        qe = _heads((qf * jnp.exp(b)).astype(BF16))
        kd = _heads((kk * jnp.exp(bl - b)).astype(BF16))
        decay = _heads(jnp.exp(bl))
        o = [_dot(qe[j], sts[j].astype(BF16), NT) for j in hs]
        parts = [[] for _ in hs]
        for lo, hi, ea, eb, mask in _intra_blocks(b):
            a, bk = _heads((qf[lo:hi] * ea).astype(BF16)), _heads((kk[:hi] * eb).astype(BF16))
            p = [jnp.where(mask, _dot(a[j], bk[j], NT), 0.0).astype(BF16) for j in hs]
            for j in hs:
                parts[j].append(_dot(p[j], vb[j][:hi]))
        o = [o[j] + jnp.concatenate(parts[j], axis=0) for j in hs]
        new = [sts[j] * decay[j] + _dot(vb[j], kd[j], TN) for j in hs]
        hgn = hgn_ref[...]
        on = [o[j] * lax.rsqrt(jnp.mean(o[j] * o[j], axis=-1, keepdims=True) + EPS) * hgn for j in hs]
        gg = g_ref[...]
        for j in hs:
            st_ref[j] = sts[j]
            s_scr[j] = new[j]
        o_ref[...] = _wide(o)
        y_ref[...] = (_wide(on) * (gg * _sigmoid(gg))).astype(BF16)

    wide = HG_BLOCK * HEAD_DIM
    groups = HG_HEADS // HG_BLOCK

    def col(k):
        return pl.BlockSpec((CHUNK, wide), lambda g, c: (c, k * groups + g))

    out = pl.BlockSpec((CHUNK, wide), lambda g, c: (c, g))
    return pl.pallas_call(
        body,
        name="hgrn_fwd",
        grid=(groups, nc),
        in_specs=[col(0), col(1), col(2), col(3), pl.BlockSpec((2, wide), lambda g, c: (0, g)), pl.BlockSpec((1, HEAD_DIM), lambda g, c: (0, 0))],
        out_specs=[out, out, pl.BlockSpec((HG_BLOCK, None, HEAD_DIM, HEAD_DIM), lambda g, c: (g, c, 0, 0))],
        out_shape=[
            jax.ShapeDtypeStruct((t, HG_WIDTH), BF16),
            jax.ShapeDtypeStruct((t, HG_WIDTH), F32),
            jax.ShapeDtypeStruct((HG_HEADS, nc, HEAD_DIM, HEAD_DIM), F32),
        ],
        scratch_shapes=[pltpu.VMEM((HG_BLOCK, HEAD_DIM, HEAD_DIM), F32)],
        compiler_params=_params("arbitrary", "arbitrary"),
    )(proj, proj, proj, proj, lb_logits, hg_norm)


def _hgrn_bwd(proj, lb_logits, hg_norm, o_hg, dycat, states, dep=None):
    t = proj.shape[0]
    nc = t // CHUNK

    def body(q_ref, f_ref, i_ref, g_ref, lbl_ref, hgn_ref, o_ref, dy_ref, st_ref, dp_ref, dlbl_ref, dhgn_ref, dst_scr, dlb_scr):
        h = pl.program_id(0)
        c = pl.program_id(1)

        @pl.when(c == 0)
        def _():
            dst_scr[...] = jnp.zeros_like(dst_scr)
            dlb_scr[...] = jnp.zeros_like(dlb_scr)

        @pl.when((c == 0) & (h == 0))
        def _():
            dhgn_ref[...] = jnp.zeros_like(dhgn_ref)

        hs = range(HG_BLOCK)
        hgn = _wide([hgn_ref[...]] * HG_BLOCK)
        q, fp, gg, vi = q_ref[...], f_ref[...], g_ref[...], i_ref[...]
        lb, sig, f, kk, sq, qf, b = _hgrn_gates(q, fp, lbl_ref[...])
        o, dy = o_ref[...], dy_ref[...]
        sg = _sigmoid(gg)
        n = _wide([oh * lax.rsqrt(jnp.mean(oh * oh, axis=-1, keepdims=True) + EPS) for oh in _heads(o)])
        don = dy * (gg * sg)
        dgg = dy * (n * hgn) * (sg * (1.0 + gg * (1.0 - sg)))
        d_hgn = sum(_heads(jnp.sum(don * n, axis=0, keepdims=True)))
        dn = don * hgn
        do = _wide(
            [
                lax.rsqrt(jnp.mean(oh * oh, axis=-1, keepdims=True) + EPS) * (dnh - nh * jnp.mean(dnh * nh, axis=-1, keepdims=True))
                for oh, dnh, nh in zip(_heads(o), _heads(dn), _heads(n))
            ]
        )
        sts = [st_ref[j] for j in hs]
        dstn = [dst_scr[j] for j in hs]
        bl = b[CHUNK - 1 : CHUNK, :]
        e_b, e_bl, e_l = jnp.exp(b), jnp.exp(bl - b), jnp.exp(bl)
        doh, vih = _heads(do), _heads(vi)
        dobh = _heads(do.astype(BF16))
        dq_acc = _wide([_dot3(doh[j], sts[j]) for j in hs]) * e_b
        dk_inter = _wide([_dot3(vih[j], dstn[j]) for j in hs]) * e_bl
        dk_acc = dk_inter
        kd = _heads((kk * e_bl).astype(BF16))
        dv_acc = _wide([_dot(kd[j], dstn[j].astype(BF16), NT) for j in hs])
        qe, decay = _heads((qf * e_b).astype(BF16)), _heads(e_l)
        dst_new = [dstn[j] * decay[j] + _dot(dobh[j], qe[j], TN) for j in hs]
        db_last = e_l * _wide([jnp.sum(sts[j] * dstn[j], axis=0, keepdims=True) for j in hs]) + jnp.sum(kk * dk_inter, axis=0, keepdims=True)
        dq_parts = []
        for lo, hi, ea, eb, mask in _intra_blocks(b):
            a, bk = qf[lo:hi] * ea, kk[:hi] * eb
            ah, bkh = _heads(a), _heads(bk)
            abh, bkbh = _heads(a.astype(BF16)), _heads(bk.astype(BF16))
            p = [jnp.where(mask, _dot(abh[j], bkbh[j], NT), 0.0).astype(BF16) for j in hs]
            dp = [jnp.where(mask, _dot3(doh[j][lo:hi], vih[j][:hi], NT), 0.0) for j in hs]
            dq_parts.append(_wide([_dot3(dp[j], bkh[j]) for j in hs]) * ea)
            dki = _wide([_dot3(dp[j], ah[j], TN) for j in hs]) * eb
            dvi = _wide([_dot(p[j], dobh[j][lo:hi], TN) for j in hs])
            if hi < CHUNK:
                zeros = jnp.zeros((CHUNK - hi, HG_BLOCK * HEAD_DIM), F32)
                dki = jnp.concatenate([dki, zeros], axis=0)
                dvi = jnp.concatenate([dvi, zeros], axis=0)
            dk_acc = dk_acc + dki
            dv_acc = dv_acc + dvi
        dq_acc = dq_acc + jnp.concatenate(dq_parts, axis=0)
        rows = lax.broadcasted_iota(jnp.int32, dq_acc.shape, 0)
        db = qf * dq_acc - kk * dk_acc + jnp.where(rows == CHUNK - 1, db_last, 0.0)
        dlf = jnp.dot(_tri(CHUNK, True), db, precision=HIGHEST, preferred_element_type=F32)
        dfk = dlf / f - dk_acc
        dp_ref[0] = (dq_acc * (sq * (1.0 + q * (1.0 - sq)))).astype(BF16)
        dp_ref[1] = ((1.0 - lb) * dfk * sig * (1.0 - sig)).astype(BF16)
        dp_ref[2] = dv_acc.astype(BF16)
        dp_ref[3] = dgg.astype(BF16)
        dlb_scr[...] += jnp.sum(dfk * (1.0 - sig), axis=0, keepdims=True)
        dhgn_ref[...] += d_hgn
        for j in hs:
            dst_scr[j] = dst_new[j]

        @pl.when(c == nc - 1)
        def _():
            dl0 = dlb_scr[...] * lb * (1.0 - lb)
            dlbl_ref[0:1, :] = dl0
            dlbl_ref[1:2, :] = -dl0

    wide = HG_BLOCK * HEAD_DIM
    groups = HG_HEADS // HG_BLOCK

    def col(k):
        return pl.BlockSpec((CHUNK, wide), lambda g, c: (nc - 1 - c, k * groups + g))

    blk = pl.BlockSpec((CHUNK, wide), lambda g, c: (nc - 1 - c, g))
    return _pallas(
        body,
        9,
        dep,
        name="hgrn_bwd",
        grid=(groups, nc),
        in_specs=[
            col(0), col(1), col(2), col(3),
            pl.BlockSpec((2, wide), lambda g, c: (0, g)),
            pl.BlockSpec((1, HEAD_DIM), lambda g, c: (0, 0)),
            blk, blk,
            pl.BlockSpec((HG_BLOCK, None, HEAD_DIM, HEAD_DIM), lambda g, c: (g, nc - 1 - c, 0, 0)),
        ],
        out_specs=[
            pl.BlockSpec((4, CHUNK, wide), lambda g, c: (0, nc - 1 - c, g)),
            pl.BlockSpec((2, wide), lambda g, c: (0, g)),
            pl.BlockSpec((1, HEAD_DIM), lambda g, c: (0, 0)),
        ],
        out_shape=[
            jax.ShapeDtypeStruct((4, t, HG_WIDTH), BF16),
            jax.ShapeDtypeStruct((2, HG_WIDTH), F32),
            jax.ShapeDtypeStruct((1, HEAD_DIM), F32),
        ],
        scratch_shapes=[pltpu.VMEM((HG_BLOCK, HEAD_DIM, HEAD_DIM), F32), pltpu.VMEM((1, wide), F32)],
        compiler_params=_params("arbitrary", "arbitrary"),
    )(proj, proj, proj, proj, lb_logits, hg_norm, o_hg, dycat, states)


def _diagonal_slots(shift):
    i = lax.broadcasted_iota(jnp.int32, (N_REL_PAD, DIAG), 0)
    u = lax.broadcasted_iota(jnp.int32, (N_REL_PAD, DIAG), 1)
    offset = u - shift if shift else jnp.where(u < K_BLOCK, u, u - DIAG)
    return jnp.where(jnp.clip(PAD - offset, -REL_CLIP, REL_CLIP) + REL_CLIP == i, 1.0, 0.0).astype(BF16)


def _split3(x):
    hi = x.astype(BF16)
    mid = (x - hi.astype(F32)).astype(BF16)
    return hi, mid, (x - hi.astype(F32) - mid.astype(F32)).astype(BF16)


def _bias_table(rel_bias):
    def body(rb_ref, o_ref, diag):
        h = pl.program_id(0)

        @pl.when(h == 0)
        def _():
            hi, mid, lo = _split3(rb_ref[...])
            slots = _diagonal_slots(0)
            diag[...] = _dot(hi, slots) + (_dot(mid, slots) + _dot(lo, slots))

        rows = jnp.broadcast_to(diag[pl.ds(h, 1), :], (Q_BLOCK, DIAG))
        row = lax.broadcasted_iota(jnp.int32, (Q_BLOCK, K_BLOCK), 0)
        col = lax.broadcasted_iota(jnp.int32, (Q_BLOCK, K_BLOCK), 1)
        first = row - (row & (CHUNK - 1))
        seen = (col >= first) & (col < first + BAND)
        o_ref[...] = jnp.where(seen, pltpu.roll(rows, 0, 1, stride=1, stride_axis=0)[:, :K_BLOCK], MASKED)

    return pl.pallas_call(
        body,
        name="bias_table",
        grid=(ATT_HEADS,),
        in_specs=[pl.BlockSpec((ATT_HEADS, N_REL_PAD), lambda h: (0, 0))],
        out_specs=pl.BlockSpec((None, Q_BLOCK, K_BLOCK), lambda h: (h, 0, 0)),
        out_shape=jax.ShapeDtypeStruct((ATT_HEADS, Q_BLOCK, K_BLOCK), F32),
        scratch_shapes=[pltpu.VMEM((ATT_HEADS, DIAG), F32)],
        compiler_params=_params("arbitrary"),
    )(rel_bias)


def _att_probs(q_ref, kpad, bias_ref, blk):
    qs = (q_ref[...] * ATT_SCALE).astype(BF16)
    start = pl.multiple_of(blk * Q_BLOCK, Q_BLOCK)
    kb = kpad[pl.ds(start, K_BLOCK), :]
    s = _dot(qs, kb, NT) + bias_ref[...]
    col = lax.broadcasted_iota(jnp.int32, (Q_BLOCK, K_BLOCK), 1)
    s = jnp.where(col >= PAD - blk * Q_BLOCK, s, MASKED)
    e = jnp.exp(s - jnp.max(s, axis=-1, keepdims=True))
    return qs, kb, start, e / jnp.sum(e, axis=-1, keepdims=True)


def _fill_padded(dst, src):
    dst[0:PAD, :] = jnp.zeros((PAD, HEAD_DIM), BF16)
    dst[PAD:, :] = src[...].astype(BF16)


def _att_fwd(proj, bias, dep=None):
    t = proj.shape[0]
    nb = t // Q_BLOCK

    def body(q_ref, k_ref, v_ref, bias_ref, y_ref, kpad, vpad):
        c = pl.program_id(1)

        @pl.when(c == 0)
        def _():
            _fill_padded(kpad, k_ref)
            _fill_padded(vpad, v_ref)

        _, _, start, p = _att_probs(q_ref, kpad, bias_ref, c)
        y_ref[...] = _dot(p.astype(BF16), vpad[pl.ds(start, K_BLOCK), :]).astype(BF16)

    base = 4 * HG_HEADS
    return _pallas(
        body,
        4,
        dep,
        name="att_fwd",
        grid=(ATT_HEADS, nb),
        in_specs=[
            pl.BlockSpec((Q_BLOCK, HEAD_DIM), lambda h, c: (c, base + h)),
            pl.BlockSpec((t, HEAD_DIM), lambda h, c: (0, base + ATT_HEADS + h)),
            pl.BlockSpec((t, HEAD_DIM), lambda h, c: (0, base + 2 * ATT_HEADS + h)),
            pl.BlockSpec((None, Q_BLOCK, K_BLOCK), lambda h, c: (h, 0, 0)),
        ],
        out_specs=pl.BlockSpec((Q_BLOCK, HEAD_DIM), lambda h, c: (c, h)),
        out_shape=jax.ShapeDtypeStruct((t, ATT_WIDTH), BF16),
        scratch_shapes=[pltpu.VMEM((t + PAD, HEAD_DIM), BF16), pltpu.VMEM((t + PAD, HEAD_DIM), BF16)],
        compiler_params=_params("arbitrary", "arbitrary"),
    )(proj, proj, proj, bias)


def _att_bwd(proj, bias, dycat, dep=None):
    t = proj.shape[0]
    nb = t // Q_BLOCK

    def body(q_ref, k_ref, v_ref, bias_ref, dy_ref, dq_ref, dk_ref, dv_ref, g_ref, kpad, vpad, dkacc, dvacc):
        c = pl.program_id(1)

        @pl.when(c == 0)
        def _():
            _fill_padded(kpad, k_ref)
            _fill_padded(vpad, v_ref)
            dkacc[...] = jnp.zeros_like(dkacc)
            dvacc[...] = jnp.zeros_like(dvacc)
            g_ref[...] = jnp.zeros_like(g_ref)

        qs, kb, start, p = _att_probs(q_ref, kpad, bias_ref, c)
        band = pl.ds(start, K_BLOCK)
        dyb = dy_ref[...].astype(BF16)
        dvacc[band, :] += _dot(p.astype(BF16), dyb, TN)
        dp = _dot(dyb, vpad[band, :], NT)
        ds = p * (dp - jnp.sum(dp * p, axis=-1, keepdims=True))
        g_ref[...] += ds
        dsb = ds.astype(BF16)
        dq_ref[...] = (_dot(dsb, kb) * ATT_SCALE).astype(BF16)
        dkacc[band, :] += _dot(dsb, qs, TN)

        @pl.when(c == nb - 1)
        def _():
            dk_ref[...] = dkacc[PAD:, :].astype(BF16)
            dv_ref[...] = dvacc[PAD:, :].astype(BF16)

    base = 4 * HG_HEADS
    whole = pl.BlockSpec((t, HEAD_DIM), lambda h, c: (0, h))
    return _pallas(
        body,
        5,
        dep,
        name="att_bwd",
        grid=(ATT_HEADS, nb),
        in_specs=[
            pl.BlockSpec((Q_BLOCK, HEAD_DIM), lambda h, c: (c, base + h)),
            pl.BlockSpec((t, HEAD_DIM), lambda h, c: (0, base + ATT_HEADS + h)),
            pl.BlockSpec((t, HEAD_DIM), lambda h, c: (0, base + 2 * ATT_HEADS + h)),
            pl.BlockSpec((None, Q_BLOCK, K_BLOCK), lambda h, c: (h, 0, 0)),
            pl.BlockSpec((Q_BLOCK, HEAD_DIM), lambda h, c: (c, HG_HEADS + h)),
        ],
        out_specs=[pl.BlockSpec((Q_BLOCK, HEAD_DIM), lambda h, c: (c, h)), whole, whole, pl.BlockSpec((None, Q_BLOCK, K_BLOCK), lambda h, c: (h, 0, 0))],
        out_shape=[
            jax.ShapeDtypeStruct((t, ATT_WIDTH), BF16),
            jax.ShapeDtypeStruct((t, ATT_WIDTH), BF16),
            jax.ShapeDtypeStruct((t, ATT_WIDTH), BF16),
            jax.ShapeDtypeStruct((ATT_HEADS, Q_BLOCK, K_BLOCK), F32),
        ],
        scratch_shapes=[
            pltpu.VMEM((t + PAD, HEAD_DIM), BF16),
            pltpu.VMEM((t + PAD, HEAD_DIM), BF16),
            pltpu.VMEM((t + PAD, HEAD_DIM), F32),
            pltpu.VMEM((t + PAD, HEAD_DIM), F32),
        ],
        compiler_params=_params("arbitrary", "arbitrary"),
    )(proj, proj, proj, bias, dycat)


def _rel_bias_grad(gsum):
    def body(g_ref, o_ref):
        r = lax.broadcasted_iota(jnp.int32, (Q_BLOCK, Q_BLOCK), 0)
        c = lax.broadcasted_iota(jnp.int32, (Q_BLOCK, Q_BLOCK), 1)
        flip = jnp.where(r + c == Q_BLOCK - 1, 1.0, 0.0).astype(BF16)
        sums = []
        for h in range(ATT_HEADS):
            hi, mid, lo = _split3(g_ref[h])
            rev = _dot(flip, hi) + (_dot(flip, mid) + _dot(flip, lo))
            wide = jnp.concatenate([rev, jnp.zeros((Q_BLOCK, DIAG - K_BLOCK), F32)], axis=1)
            sums.append(jnp.sum(pltpu.roll(wide, 0, 1, stride=1, stride_axis=0), axis=0, keepdims=True))
        hi, mid, lo = _split3(jnp.concatenate(sums, axis=0))
        slots = _diagonal_slots(Q_BLOCK - 1)
        o_ref[...] = _dot(hi, slots, NT) + (_dot(mid, slots, NT) + _dot(lo, slots, NT))

    return pl.pallas_call(
        body,
        name="rel_bias_grad",
        out_shape=jax.ShapeDtypeStruct((ATT_HEADS, N_REL_PAD), F32),
        compiler_params=_params(),
    )(gsum)


HALO = 16


FF_TILE = 1408
FF_TILES = D_FF // FF_TILE


def _interleave_cols(a):
    lead = a.shape[:-1]
    return jnp.swapaxes(a.reshape(*lead, 2, FF_TILES, FF_TILE), -3, -2).reshape(*lead, 2 * D_FF)


def _deinterleave_cols(a):
    lead = a.shape[:-1]
    return jnp.swapaxes(a.reshape(*lead, FF_TILES, 2, FF_TILE), -3, -2).reshape(*lead, 2 * D_FF)


def _ffn_specs(t, tm):
    wide = 2 * FF_TILE
    tile = pl.BlockSpec((tm, wide), lambda j, i: (i, j))
    before = pl.BlockSpec((HALO, wide), lambda j, i: (jnp.maximum(i * (tm // HALO) - 1, 0), j))
    after = pl.BlockSpec((HALO, wide), lambda j, i: (jnp.minimum((i + 1) * (tm // HALO), t // HALO - 1), j))
    vec = lambda rows: pl.BlockSpec((rows, wide), lambda j, i: (0, j))
    return tile, before, after, vec


def _conv(x, w, b, rows):
    taps = [pltpu.roll(x, 2, 0)[HALO : HALO + rows], pltpu.roll(x, 1, 0)[HALO : HALO + rows], x[HALO : HALO + rows]]
    return b + w[0:1] * taps[0] + w[1:2] * taps[1] + w[2:3] * taps[2], taps


def _ffn_act_fwd(u, conv_w, conv_b):
    t = u.shape[0]
    tm = _tile(t, (128,))
    tile, before, _, vec = _ffn_specs(t, tm)

    def body(u_ref, h_ref, w_ref, b_ref, z_ref):
        first = pl.program_id(1) == 0
        x = jnp.concatenate([jnp.where(first, 0.0, h_ref[...].astype(F32)), u_ref[...].astype(F32)], axis=0)
        c, _ = _conv(x, w_ref[...], b_ref[...], tm)
        gate, val = c[:, :FF_TILE], c[:, FF_TILE:]
        z_ref[...] = (gate * _sigmoid(gate) * val).astype(BF16)

    return pl.pallas_call(
        body,
        name="ffn_act_fwd",
        grid=(FF_TILES, t // tm),
        in_specs=[tile, before, vec(3), vec(1)],
        out_specs=pl.BlockSpec((tm, FF_TILE), lambda j, i: (i, j)),
        out_shape=jax.ShapeDtypeStruct((t, D_FF), BF16),
        compiler_params=_params("parallel", "parallel"),
    )(u, u, conv_w, conv_b)


def _ffn_act_bwd(u, dz, conv_w, conv_b, dep=None):
    t = u.shape[0]
    tm = _tile(t, (128,))
    nt = t // tm
    ext = tm + HALO
    tile, before, after, vec = _ffn_specs(t, tm)

    def body(u_ref, ub_ref, ua_ref, w_ref, b_ref, dz_ref, dza_ref, du_ref, dw_ref, db_ref):
        i = pl.program_id(1)
        first, last = i == 0, i == nt - 1
        parts = [jnp.where(first, 0.0, ub_ref[...].astype(F32)), u_ref[...].astype(F32), jnp.where(last, 0.0, ua_ref[...].astype(F32))]
        w = w_ref[...]
        c, taps = _conv(jnp.concatenate(parts, axis=0), w, b_ref[...], ext)
        gate, val = c[:, :FF_TILE], c[:, FF_TILE:]
        dz = jnp.concatenate([dz_ref[...].astype(F32), jnp.where(last, 0.0, dza_ref[...].astype(F32))], axis=0)
        sg = _sigmoid(gate)
        d = jnp.concatenate([dz * val * (sg * (1.0 + gate * (1.0 - sg))), dz * (gate * sg)], axis=1)
        du = w[2:3] * d[:tm] + w[1:2] * pltpu.roll(d, ext - 1, 0)[:tm] + w[0:1] * pltpu.roll(d, ext - 2, 0)[:tm]
        du_ref[...] = du.astype(BF16)

        @pl.when(first)
        def _():
            dw_ref[...] = jnp.zeros_like(dw_ref)
            db_ref[...] = jnp.zeros_like(db_ref)

        for k, tap in enumerate(taps):
            dw_ref[k : k + 1, :] += jnp.sum(d[:tm] * tap[:tm], axis=0, keepdims=True)
        db_ref[...] += jnp.sum(d[:tm], axis=0, keepdims=True)

    narrow = lambda rows, index: pl.BlockSpec((rows, FF_TILE), index)
    return _pallas(
        body,
        7,
        dep,
        name="ffn_act_bwd",
        grid=(FF_TILES, nt),
        in_specs=[
            tile, before, after, vec(3), vec(1),
            narrow(tm, lambda j, i: (i, j)),
            narrow(HALO, lambda j, i: (jnp.minimum((i + 1) * (tm // HALO), t // HALO - 1), j)),
        ],
        out_specs=[tile, vec(3), vec(1)],
        out_shape=[
            jax.ShapeDtypeStruct((t, 2 * D_FF), BF16),
            jax.ShapeDtypeStruct((3, 2 * D_FF), F32),
            jax.ShapeDtypeStruct((1, 2 * D_FF), F32),
        ],
        compiler_params=_params("parallel", "arbitrary"),
    )(u, u, u, conv_w, conv_b, dz, dz)


def _ple_loss(gpre, pp, h2, final_norm, target):
    t, d = h2.shape
    tm = _tile(t, (256,))

    def body(gp_ref, pp_ref, h_ref, g_ref, tg_ref, dh_ref, dgp_ref, dpp_ref, dg_ref, loss_ref):
        i = pl.program_id(0)
        gate = _sigmoid(gp_ref[...])
        ppv = pp_ref[...]
        h3 = h_ref[...] + gate * ppv
        r = lax.rsqrt(jnp.mean(h3 * h3, axis=-1, keepdims=True) + EPS)
        n = h3 * r
        g = g_ref[...]
        err = n * g - tg_ref[...]
        loss = 0.5 * jnp.sum(jnp.mean(err * err, axis=-1, keepdims=True))
        dy = err * (1.0 / d)
        dn = dy * g
        dh = r * (dn - n * jnp.mean(dn * n, axis=-1, keepdims=True))
        dh_ref[...] = dh
        dgp_ref[...] = (dh * ppv * gate * (1.0 - gate)).astype(BF16)
        dpp_ref[...] = (dh * gate).astype(BF16)
        dg = jnp.sum(dy * n, axis=0, keepdims=True)

        @pl.when(i == 0)
        def _():
            dg_ref[...] = dg
            loss_ref[...] = jnp.full(loss_ref.shape, loss, F32)

        @pl.when(i > 0)
        def _():
            dg_ref[...] += dg
            loss_ref[...] += loss

    row = pl.BlockSpec((tm, d), lambda i: (i, 0))
    vec = pl.BlockSpec((1, d), lambda i: (0, 0))
    return pl.pallas_call(
        body,
        name="ple_loss",
        grid=(t // tm,),
        in_specs=[row, row, row, vec, row],
        out_specs=[row, row, row, vec, pl.BlockSpec((8, 128), lambda i: (0, 0))],
        out_shape=[
            jax.ShapeDtypeStruct((t, d), F32),
            jax.ShapeDtypeStruct((t, d), BF16),
            jax.ShapeDtypeStruct((t, d), BF16),
            jax.ShapeDtypeStruct((1, d), F32),
            jax.ShapeDtypeStruct((8, 128), F32),
        ],
        compiler_params=_params("arbitrary"),
    )(gpre, pp, h2, final_norm, target)


def _adamw(w, g, m, v):
    m = ADAM_B1 * m + (1.0 - ADAM_B1) * g
    v = ADAM_B2 * v + (1.0 - ADAM_B2) * (g * g)
    m_hat = m / (1.0 - ADAM_B1 ** ADAM_STEP)
    v_hat = v / (1.0 - ADAM_B2 ** ADAM_STEP)
    return -ADAM_LR * (m_hat / (jnp.sqrt(v_hat) + ADAM_EPS) + ADAM_WD * w), m, v


def _adam_big(w, m, v, own, recv, name):
    r, c = w.shape
    tr = _tile(r, (256, 176))

    def body(w_ref, m_ref, v_ref, own_ref, recv_ref, g_ref, d_ref, nm_ref, nv_ref):
        g = own_ref[...]
        for k in range(3):
            g = g + recv_ref[k].astype(F32)
        g_ref[...] = g
        d_ref[...], nm_ref[...], nv_ref[...] = _adamw(w_ref[...], g, m_ref[...], v_ref[...])

    blk = pl.BlockSpec((tr, c), lambda i: (i, 0))
    return pl.pallas_call(
        body,
        name=name,
        grid=(r // tr,),
        in_specs=[blk, blk, blk, blk, pl.BlockSpec((3, tr, c), lambda i: (0, i, 0))],
        out_specs=[blk] * 4,
        out_shape=[jax.ShapeDtypeStruct((r, c), F32)] * 4,
        compiler_params=_params("parallel"),
    )(w, m, v, own, recv)


def _adam_small(w, g, m, v):
    def body(w_ref, g_ref, m_ref, v_ref, d_ref, nm_ref, nv_ref):
        d_ref[...], nm_ref[...], nv_ref[...] = _adamw(w_ref[...], g_ref[...], m_ref[...], v_ref[...])

    return pl.pallas_call(body, name="adam_small", out_shape=[jax.ShapeDtypeStruct(w.shape, F32)] * 3, compiler_params=_params())(w, g, m, v)


def _cast_bf16(w, name):
    r, c = w.shape
    tr = _tile(r, (256, 176))

    def body(w_ref, o_ref):
        o_ref[...] = w_ref[...].astype(BF16)

    blk = pl.BlockSpec((tr, c), lambda i: (i, 0))
    return pl.pallas_call(
        body, name=name, grid=(r // tr,), in_specs=[blk], out_specs=blk, out_shape=jax.ShapeDtypeStruct((r, c), BF16), compiler_params=_params("parallel")
    )(w)


def _position():
    return lax.axis_index("x"), lax.axis_index("y"), lax.axis_index("c")


def _other_chips(x, y):
    return [(1 - x, y), (x, 1 - y), (1 - x, 1 - y)]


def _block_index(dev, interleaved):
    x, y, c = dev
    return 4 * y + 2 * c + x if interleaved else 4 * x + 2 * y + c


def _shard_of(ref, axis, size, dev, interleaved=False):
    start = pl.multiple_of(_block_index(dev, interleaved) * size, 128 if axis == 1 else 16)
    return ref.at[:, pl.ds(start, size)] if axis == 1 else ref.at[pl.ds(start, size), :]


def _all_gather(shards, axes, interleaved):
    n = len(shards)

    def body(*refs):
        ins, outs = refs[:n], refs[n : 2 * n]
        send_sems, recv_sems, local_sems = refs[2 * n :]
        x, y, c = _position()
        me, sibling = (x, y, c), (x, y, 1 - c)
        chips = _other_chips(x, y)
        firsts, passed, locals_ = [], [], []
        for w in range(n):
            size = shards[w].shape[axes[w]]
            slot = functools.partial(_shard_of, outs[w], axes[w], size, interleaved=interleaved[w])

            def copy(k, block, to, src=None, w=w, slot=slot):
                return pltpu.make_async_remote_copy(
                    src_ref=slot(block) if src is None else src,
                    dst_ref=slot(block),
                    send_sem=send_sems.at[7 * w + k],
                    recv_sem=recv_sems.at[7 * w + k],
                    device_id=to,
                    device_id_type=MESH,
                )

            mine = pltpu.make_async_copy(ins[w], slot(me), local_sems.at[w])
            mine.start()
            locals_.append(mine)
            first = [copy(0, me, sibling, src=ins[w])] + [copy(1 + j, me, (*chip, c), src=ins[w]) for j, chip in enumerate(chips)]
            for cp in first:
                cp.start()
            firsts.append((first, copy))
        for w in range(n):
            first, copy = firsts[w]
            fwd = [copy(4 + j, (*chip, c), sibling) for j, chip in enumerate(chips)]
            for j, chip in enumerate(chips):
                copy(1 + j, (*chip, c), me).wait_recv()
                fwd[j].start()
            passed.append(fwd)
        for w in range(n):
            first, copy = firsts[w]
            copy(0, sibling, me).wait_recv()
            for j, chip in enumerate(chips):
                copy(4 + j, (*chip, 1 - c), me).wait_recv()
            for cp in first + passed[w]:
                cp.wait_send()
            locals_[w].wait()

    def full(s, ax):
        shape = list(s.shape)
        shape[ax] *= N_DEV
        return jax.ShapeDtypeStruct(tuple(shape), s.dtype)

    return pl.pallas_call(
        body,
        name="all_gather_weights",
        in_specs=[ANY] * n,
        out_specs=[ANY] * n,
        out_shape=[full(s, ax) for s, ax in zip(shards, axes)],
        scratch_shapes=[pltpu.SemaphoreType.DMA((7 * n,)), pltpu.SemaphoreType.DMA((7 * n,)), pltpu.SemaphoreType.DMA((n,))],
    )(*shards)


def _add_blocks(ids, grad, landed, axis, size, targets, out_dtype, name):
    rows = size if axis == 0 else grad.shape[0]
    cols = size if axis == 1 else grad.shape[1]
    tr = _tile(rows, (256, 176))
    nr = rows // tr
    nt = len(targets)

    def body(ids_ref, g_ref, l_ref, o_ref):
        o_ref[...] = (g_ref[...] + l_ref[...]).astype(out_dtype)

    if axis == 1:
        g_spec = pl.BlockSpec((tr, cols), lambda k, i, ids: (i, ids[targets[0] + k]))
    else:
        g_spec = pl.BlockSpec((tr, cols), lambda k, i, ids: (ids[targets[0] + k] * nr + i, 0))
    return pl.pallas_call(
        body,
        name=name,
        grid_spec=pltpu.PrefetchScalarGridSpec(
            num_scalar_prefetch=1,
            grid=(nt, nr),
            in_specs=[g_spec, pl.BlockSpec((None, tr, cols), lambda k, i, ids: (ids[4 + targets[0] + k], i, 0))],
            out_specs=pl.BlockSpec((None, tr, cols), lambda k, i, ids: (k, i, 0)),
        ),
        out_shape=jax.ShapeDtypeStruct((nt, rows, cols), out_dtype),
        compiler_params=_params("parallel", "parallel"),
    )(ids, grad, landed)


def _all_reduce_small(vec, name):
    rows = vec.shape[0]

    def body(v_ref, o_ref, land, send_sems, recv_sems):
        x, y, c = _position()
        mine = 4 * x + 2 * y + c
        copies = []
        for mask in range(1, N_DEV):
            peer = (1 - x if mask & 4 else x, 1 - y if mask & 2 else y, 1 - c if mask & 1 else c)
            copies.append(
                pltpu.make_async_remote_copy(
                    src_ref=v_ref, dst_ref=land.at[mine], send_sem=send_sems.at[mask - 1], recv_sem=recv_sems.at[mask - 1], device_id=peer, device_id_type=MESH
                )
            )
        for cp in copies:
            cp.start()
        land[mine] = v_ref[...]
        for cp in copies:
            cp.wait()
        acc = land[0]
        for k in range(1, N_DEV):
            acc = acc + land[k]
        o_ref[...] = acc

    return pl.pallas_call(
        body,
        name=name,
        out_shape=jax.ShapeDtypeStruct(vec.shape, F32),
        in_specs=[pl.BlockSpec(memory_space=pltpu.VMEM)],
        out_specs=pl.BlockSpec(memory_space=pltpu.VMEM),
        scratch_shapes=[pltpu.VMEM((N_DEV, rows, 128), F32), pltpu.SemaphoreType.DMA((N_DEV - 1,)), pltpu.SemaphoreType.DMA((N_DEV - 1,))],
    )(vec)


def _rows128(a, rows):
    flat = a.reshape(-1)
    return jnp.pad(flat, (0, rows * 128 - flat.shape[0])).reshape(rows, 128)


def _pad_rel(a):
    return jnp.pad(a.reshape(ATT_HEADS, -1)[:, :N_REL], ((0, 0), (0, N_REL_PAD - N_REL)))


SMALL = [("norm_mix", 16), ("lb_logits", 16), ("hg_norm", 8), ("rel_bias", 24), ("norm_ffn", 16), ("conv_b", 88), ("norm_ple", 16), ("final_norm", 16)]
CONV_W_FULL_ROWS = 3 * 2 * D_FF // 128
CONV_W_SHARD_ROWS = 40


def _pack_small(parts):
    return jnp.concatenate([_rows128(_pad_rel(parts[k]) if k == "rel_bias" else parts[k], rows) for k, rows in SMALL], axis=0)


def _unpack_small(packed, shapes):
    out, at = {}, 0
    for k, rows in SMALL:
        blk = packed[at : at + rows]
        at += rows
        if k == "rel_bias":
            out[k] = blk.reshape(ATT_HEADS, N_REL_PAD)[:, :N_REL].reshape(shapes[k])
        else:
            n = 1
            for s in shapes[k]:
                n *= s
            out[k] = blk.reshape(-1)[:n].reshape(shapes[k])
    return out, at


BIG = [("w_in", 1), ("w_out", 0), ("w_up", 1), ("w_down", 0), ("w_ple_gate", 0), ("w_ple_proj", 1)]


HBM = pl.BlockSpec(memory_space=pltpu.HBM)
SEM = pl.BlockSpec(memory_space=pltpu.SEMAPHORE)
EFFECT = pltpu.SideEffectType.DATAFLOW_SIDE_EFFECTING


def _copies(plan, refs, send_sems, recv_sems):
    return [
        pltpu.make_async_remote_copy(src_ref=src, dst_ref=dst, send_sem=send_sems.at[i], recv_sem=recv_sems.at[i], device_id=dev, device_id_type=MESH)
        for i, (src, dst, dev) in enumerate(plan(refs))
    ]


def _split_start(name, arrays, plan, n):
    k = len(arrays)

    def body(*refs):
        for cp in _copies(plan, refs[:k], refs[k], refs[k + 1]):
            cp.start()
        refs[-1][...] = jnp.zeros_like(refs[-1])

    out = pl.pallas_call(
        body,
        name=name,
        out_shape=(pltpu.SemaphoreType.DMA((n,)), pltpu.SemaphoreType.DMA((n,)), *[pltpu.HBM(a.shape, a.dtype) for a in arrays], jax.ShapeDtypeStruct((8, 128), F32)),
        in_specs=[HBM] * k,
        out_specs=(SEM, SEM, *[HBM] * k, pl.BlockSpec(memory_space=pltpu.VMEM)),
        input_output_aliases={i: 2 + i for i in range(k)},
        compiler_params=pltpu.CompilerParams(has_side_effects=EFFECT),
    )(*[pltpu.with_memory_space_constraint(a, pltpu.HBM) for a in arrays])
    return out[0], out[1], list(out[2 : 2 + k]), out[-1]


def _split_wait(name, send, recv, arrays, plan, after):
    k = len(arrays)

    def body(*refs):
        for cp in _copies(plan, refs[:k], refs[k], refs[k + 1]):
            cp.wait_send()
            cp.wait_recv()

    out = pl.pallas_call(
        body,
        name=name,
        out_shape=tuple(pltpu.HBM(a.shape, a.dtype) for a in arrays),
        in_specs=[HBM] * k + [SEM, SEM, ANY],
        out_specs=tuple([HBM] * k),
        input_output_aliases={i: i for i in range(k)},
        compiler_params=pltpu.CompilerParams(has_side_effects=EFFECT),
    )(*arrays, send, recv, after)
    return list(out)


def _cast_into(w, me, axis, name, dep):
    r, c = w.shape
    tr = _tile(r, (256, 176))
    nr = r // tr

    def body(me_ref, w_ref, dep_ref, o_ref):
        o_ref[...] = w_ref[...].astype(BF16)

    if axis == 1:
        shape, o_spec = (r, N_DEV * c), pl.BlockSpec((tr, c), lambda i, me: (i, me[0]))
    else:
        shape, o_spec = (N_DEV * r, c), pl.BlockSpec((tr, c), lambda i, me: (me[0] * nr + i, 0))
    return pl.pallas_call(
        body,
        name=name,
        grid_spec=pltpu.PrefetchScalarGridSpec(
            num_scalar_prefetch=1, grid=(nr,), in_specs=[pl.BlockSpec((tr, c), lambda i, me: (i, 0)), ANY], out_specs=o_spec
        ),
        out_shape=jax.ShapeDtypeStruct(shape, BF16),
        compiler_params=_params("parallel"),
    )(me, w, dep)


GATHER = [
    (["w_out"], None, "att_fwd", None),
    (["w_up"], None, "att_fwd", "norm_ffn_fwd"),
    (["w_down", "w_ple_gate", "w_ple_proj"], "att_fwd", "up_proj", "ffn_act_fwd"),
]
GROUPS = [["w_ple_proj", "w_ple_gate", "w_down"], ["w_up"], ["w_out"], ["w_in"]]
STAGES = ["ffn_act_bwd", "d_mix_out", "hgrn_bwd", "d_norm_mix_out"]
INTERLEAVED = {"w_up"}


class _Exchange:
    def __init__(self, big, position):
        self.big, self.axis = big, dict(BIG)
        self.size = {k: big[k].shape[self.axis[k]] for k in big}
        self.x, self.y, self.c = position
        chips = [(self.x, self.y)] + _other_chips(self.x, self.y)
        landed = [2 * cx + cy for cx, cy in chips]
        self.ids = {
            flag: jnp.stack([_block_index((cx, cy, self.c), flag) for cx, cy in chips] + landed).astype(jnp.int32) for flag in (False, True)
        }
        self.tokens, self.grads, self.state, self.wfull = [], {}, {}, {}


    def _slot(self, ref, k, dev):
        return _shard_of(ref, self.axis[k], self.size[k], dev, interleaved=k in INTERLEAVED)

    def _plan_gather(self, names, direct, refs):
        x, y, c = _position()
        me, out = (x, y, c), []
        for k, ref in zip(names, refs):
            mine = self._slot(ref, k, me)
            out.append((mine, mine, (x, y, 1 - c)))
            out += [(mine, mine, (*chip, c)) for chip in _other_chips(x, y)]
            if direct:
                out += [(mine, mine, (*chip, 1 - c)) for chip in _other_chips(x, y)]
        return out

    def _plan_forward(self, names, refs):
        x, y, c = _position()
        out = []
        for k, ref in zip(names, refs):
            for chip in _other_chips(x, y):
                block = self._slot(ref, k, (*chip, c))
                out.append((block, block, (x, y, 1 - c)))
        return out

    def _plan_sibling(self, names, refs):
        x, y, c = _position()
        n = len(names)
        return [(self._slot(refs[i], k, (p // 2, p % 2, 1 - c)), refs[n + i].at[p], (x, y, 1 - c)) for i, k in enumerate(names) for p in range(4)]

    def _plan_chips(self, names, refs):
        x, y, c = _position()
        n = len(names)
        return [(refs[i].at[j], refs[n + i].at[j], (*chip, c)) for i in range(n) for j, chip in enumerate(_other_chips(x, y))]


    def gather(self, conv_w):
        w_in, conv_full = _all_gather([_cast_bf16(self.big["w_in"], "cast_w_in"), conv_w], [1, 1], [False, True])
        self.wfull["w_in"] = w_in
        me = {flag: _block_index((self.x, self.y, self.c), flag).astype(jnp.int32).reshape(1) for flag in (False, True)}
        self.late, self.unsent = {}, {}
        after = w_in
        for gi, (names, issued, *_) in enumerate(GATHER):
            self.unsent[gi] = [_cast_into(self.big[k], me[k in INTERLEAVED], self.axis[k], "cast_" + k, after) for k in names]
            if issued is None:
                self._issue(None)
                after = self.tokens[-1]
        return conv_full

    def _issue(self, stage):
        for gi, (names, issued, _, forwarded) in enumerate(GATHER):
            if issued == stage and gi in self.unsent:
                plan = functools.partial(self._plan_gather, names, forwarded is None)
                copies = (7 if forwarded is None else 4) * len(names)
                send, recv, fulls, token = _split_start(f"gather_start_{gi}", self.unsent.pop(gi), plan, copies)
                self.tokens.append(token)
                self.late[gi] = (send, recv, fulls, plan)

    def weight(self, k):
        return self.wfull[k]

    def dep(self):
        tokens, self.tokens = self.tokens, []
        return tokens

    def reduce(self, vec, name):
        return _all_reduce_small(vec, name)

    def grad(self, k, g):
        self.grads[k] = g
        for gi, names in enumerate(GROUPS):
            if k == names[-1]:
                plan = functools.partial(self._plan_sibling, names)
                lands = [lax.empty((4, *self._shard_shape(n)), F32) for n in names]
                send, recv, arrays, token = _split_start(f"sibling_start_{gi}", [self.grads[n] for n in names] + lands, plan, 4 * len(names))
                self.tokens.append(token)
                self.state[gi] = (send, recv, arrays, plan)

    def done(self, stage, after):
        for gi, (names, _, _, forwarded) in enumerate(GATHER):
            if forwarded == stage:
                send, recv, fulls, plan = self.late[gi]
                self.wfull.update(zip(names, _split_wait(f"forward_wait_{gi}", send, recv, fulls, plan, after)))
        for gi, (names, _, arrived, forwarded) in enumerate(GATHER):
            if arrived == stage:
                send, recv, fulls, plan = self.late[gi]
                fulls = _split_wait(f"gather_wait_{gi}", send, recv, fulls, plan, after)
                if forwarded is None:
                    self.wfull.update(zip(names, fulls))
                else:
                    plan = functools.partial(self._plan_forward, names)
                    send, recv, fulls, token = _split_start(f"forward_start_{gi}", fulls, plan, 3 * len(names))
                    self.tokens.append(token)
                    self.late[gi] = (send, recv, fulls, plan)
        self._issue(stage)
        if stage in STAGES:
            self._to_chips(STAGES.index(stage), after)

    def _shard_shape(self, k):
        shape = list(self.grads[k].shape)
        shape[self.axis[k]] = self.size[k]
        return tuple(shape)

    def _to_chips(self, gi, after):
        names = GROUPS[gi]
        n = len(names)
        send, recv, arrays, plan = self.state[gi]
        arrays = _split_wait(f"sibling_wait_{gi}", send, recv, arrays, plan, after)
        own, parts = [], []
        for k, g, land in zip(names, arrays[:n], arrays[n:]):
            ids = self.ids[k in INTERLEAVED]
            own.append(_add_blocks(ids, g, land, self.axis[k], self.size[k], [0], F32, "add_own_" + k)[0])
            parts.append(_add_blocks(ids, g, land, self.axis[k], self.size[k], [1, 2, 3], BF16, "add_send_" + k))
        plan = functools.partial(self._plan_chips, names)
        lands = [lax.empty(part.shape, BF16) for part in parts]
        send, recv, arrays, token = _split_start(f"chips_start_{gi}", parts + lands, plan, 3 * n)
        self.tokens.append(token)
        self.state[gi] = (send, recv, arrays, plan, own)

    def finish(self, gi, after):
        names = GROUPS[gi]
        send, recv, arrays, plan, own = self.state[gi]
        arrays = _split_wait(f"chips_wait_{gi}", send, recv, arrays, plan, after)
        return {k: (o, r) for k, o, r in zip(names, own, arrays[len(names) :])}


class _Resident:
    def __init__(self, wfull):
        self.wfull, self.grads = wfull, {}

    def weight(self, k):
        return self.wfull[k]

    def grad(self, k, g):
        self.grads[k] = g

    def dep(self):
        return None

    def reduce(self, vec, name):
        return vec

    def done(self, stage, after):
        pass


def _local_step(x, p, target, small, conv_w, ex):
    a1, r1 = _rms_fwd(x, small["norm_mix"], "norm_mix_fwd", dep=ex.dep())
    proj = _matmul(a1, ex.weight("w_in"), "nn", F32, "in_proj")
    bias = _bias_table(jnp.pad(small["rel_bias"], ((0, 0), (0, N_REL_PAD - N_REL))))
    y_hg, o_hg, states = _hgrn_fwd(proj, small["lb_logits"], small["hg_norm"])
    y_att = _att_fwd(proj, bias, dep=ex.dep())
    ex.done("att_fwd", y_att)
    ycat = jnp.concatenate([y_hg, y_att], axis=1)
    h1 = _matmul(ycat, ex.weight("w_out"), "nn", F32, "out_proj", resid=x, dep=ex.dep())
    a2, r2 = _rms_fwd(h1, small["norm_ffn"], "norm_ffn_fwd")
    ex.done("norm_ffn_fwd", a2)
    u = _matmul(a2, ex.weight("w_up"), "nn", BF16, "up_proj")
    conv_b = _interleave_cols(small["conv_b"])
    ex.done("up_proj", u)
    z = _ffn_act_fwd(u, conv_w, conv_b)
    ex.done("ffn_act_fwd", z)
    h2 = _matmul(z, ex.weight("w_down"), "nn", F32, "down_proj", tk=2816, resid=h1)
    a3, r3 = _rms_fwd(h2, small["norm_ple"], "norm_ple_fwd")
    gpre = _matmul(a3, ex.weight("w_ple_gate"), "nn", F32, "ple_gate")
    pp = _matmul(p, ex.weight("w_ple_proj"), "nn", F32, "ple_proj")
    dh3, dgpre, dpp, d_final, loss = _ple_loss(gpre, pp, h2, small["final_norm"], target)

    ex.grad("w_ple_proj", _matmul(p, dpp, "tn", F32, "d_w_ple_proj", tk=2048))
    ex.grad("w_ple_gate", _matmul(a3, dgpre, "tn", F32, "d_w_ple_gate", tk=2048))
    da3 = _matmul(dgpre, ex.weight("w_ple_gate"), "nt", F32, "d_norm_ple_out")
    dh2, d_ple = _rms_bwd(da3, h2, r3, small["norm_ple"], dh3, "norm_ple_bwd")
    dz = _matmul(dh2, ex.weight("w_down"), "nt", BF16, "d_ffn_act")
    ex.grad("w_down", _matmul(z, dh2, "tn", F32, "d_w_down", tk=2048))
    du, dcw, dcb = _ffn_act_bwd(u, dz, conv_w, conv_b, dep=ex.dep())
    ex.done("ffn_act_bwd", du)
    d_conv_w, d_conv_b = _deinterleave_cols(dcw), _deinterleave_cols(dcb)
    ex.grad("w_up", _matmul(a2, du, "tn", F32, "d_w_up", tk=2048, dep=ex.dep()))
    da2 = _matmul(du, ex.weight("w_up"), "nt", F32, "d_norm_ffn_out", tk=2816, dep=ex.dep())
    dh1, d_ffn = _rms_bwd(da2, h1, r2, small["norm_ffn"], dh2, "norm_ffn_bwd")
    dycat = _matmul(dh1, ex.weight("w_out"), "nt", F32, "d_mix_out")
    ex.done("d_mix_out", dycat)
    ex.grad("w_out", _matmul(ycat, dh1, "tn", F32, "d_w_out", tk=2048, dep=ex.dep()))
    dp_hg, d_lb, d_hgn = _hgrn_bwd(proj, small["lb_logits"], small["hg_norm"], o_hg, dycat, states, dep=ex.dep())
    ex.done("hgrn_bwd", d_lb)
    dq_att, dk_att, dv_att, gsum = _att_bwd(proj, bias, dycat, dep=ex.dep())
    d_rel = _rel_bias_grad(gsum)
    d_small = {
        "norm_mix": jnp.zeros_like(small["norm_mix"]), "lb_logits": d_lb, "hg_norm": d_hgn, "rel_bias": d_rel, "norm_ffn": d_ffn,
        "conv_b": d_conv_b, "norm_ple": d_ple, "final_norm": d_final,
    }
    packed = jnp.concatenate([_pack_small(d_small), _rows128(d_conv_w, CONV_W_FULL_ROWS), _rows128(loss[0:1, 0:1], 8)], axis=0)
    early = ex.reduce(packed, "all_reduce_small")
    dproj = jnp.concatenate([dp_hg[0], dp_hg[1], dp_hg[2], dp_hg[3], dq_att, dk_att, dv_att], axis=1)
    ex.grad("w_in", _matmul(a1, dproj, "tn", F32, "d_w_in", tk=2048, dep=[early]))
    da1 = _matmul(dproj, ex.weight("w_in"), "nt", F32, "d_norm_mix_out", tk=1792, dep=ex.dep())
    dx, d_mix = _rms_bwd(da1, x, r1, small["norm_mix"], dh1, "norm_mix_bwd")
    rows = dict(SMALL)["norm_mix"]
    late = ex.reduce(_rows128(d_mix, rows), "all_reduce_norm_mix")
    ex.done("d_norm_mix_out", late)
    return dx, jnp.concatenate([late, early[rows:]], axis=0)


def kernel(x, p, norm_mix, w_in, lb_logits, hg_norm, rel_bias, w_out, norm_ffn, w_up, conv_w, conv_b, w_down, norm_ple, w_ple_gate, w_ple_proj, final_norm, loss_target, m_norm_mix, m_w_in, m_lb_logits, m_hg_norm, m_rel_bias, m_w_out, m_norm_ffn, m_w_up, m_conv_w, m_conv_b, m_w_down, m_norm_ple, m_w_ple_gate, m_w_ple_proj, m_final_norm, v_norm_mix, v_w_in, v_lb_logits, v_hg_norm, v_rel_bias, v_w_out, v_norm_ffn, v_w_up, v_conv_w, v_conv_b, v_w_down, v_norm_ple, v_w_ple_gate, v_w_ple_proj, v_final_norm):
    given = dict(locals())
    mx, my, mc = _position()
    me = 4 * mx + 2 * my + mc
    big = {k: given[k][0] for k, _ in BIG}
    ex = _Exchange(big, (mx, my, mc))
    conv_w_full = ex.gather(conv_w[0])

    small = {
        "norm_mix": norm_mix, "lb_logits": lb_logits, "hg_norm": hg_norm, "rel_bias": rel_bias[0], "norm_ffn": norm_ffn,
        "conv_b": conv_b, "norm_ple": norm_ple, "final_norm": final_norm.reshape(1, -1),
    }
    dx, reduced = _local_step(x[0], p[0, 0], loss_target[0], small, conv_w_full, ex)

    out = {}
    shapes = {k: given[k].shape for k, _ in SMALL}
    g_small, at = _unpack_small(reduced, shapes)
    g_conv_full = reduced[at : at + CONV_W_FULL_ROWS].reshape(3, 2 * D_FF)
    total_loss = reduced[at + CONV_W_FULL_ROWS, 0]
    cw = conv_w.shape[2]
    g_conv = lax.dynamic_slice_in_dim(g_conv_full, me * cw, cw, axis=1)

    def pack_with_conv(parts, conv_part):
        return jnp.concatenate([_pack_small(parts), _rows128(conv_part, CONV_W_SHARD_ROWS)], axis=0)

    d_pk, m_pk, v_pk = _adam_small(
        pack_with_conv({k: given[k] for k, _ in SMALL}, conv_w),
        pack_with_conv(g_small, g_conv),
        pack_with_conv({k: given["m_" + k] for k, _ in SMALL}, m_conv_w),
        pack_with_conv({k: given["v_" + k] for k, _ in SMALL}, v_conv_w),
    )
    for name, pk in (("d", d_pk), ("m", m_pk), ("v", v_pk)):
        parts, at = _unpack_small(pk, shapes)
        parts["conv_w"] = pk[at : at + CONV_W_SHARD_ROWS].reshape(-1)[: 3 * cw].reshape(conv_w.shape)
        for k, a in parts.items():
            out.setdefault(k, {})
            out[k][name] = a
    for k, _ in SMALL:
        out[k]["g"] = g_small[k]
    out["conv_w"]["g"] = g_conv.reshape(conv_w.shape)

    after = v_pk
    for gi in range(len(GROUPS)):
        for k, (o, r) in ex.finish(gi, after).items():
            g, d, nm, nv = _adam_big(big[k], given["m_" + k][0], given["v_" + k][0], o, r, "adam_" + k)
            out[k] = tuple(a[None] for a in (g, d, nm, nv))
            after = nv

    order = ["norm_mix", "w_in", "lb_logits", "hg_norm", "rel_bias", "w_out", "norm_ffn", "w_up", "conv_w", "conv_b", "w_down", "norm_ple", "w_ple_gate", "w_ple_proj", "final_norm"]

    def pick(k, what):
        return out[k][what] if isinstance(out[k], dict) else out[k][{"g": 0, "d": 1, "m": 2, "v": 3}[what]]

    return (total_loss, dx[None], *[pick(k, "g") for k in order], *[pick(k, "d") for k in order], *[pick(k, "m") for k in order], *[pick(k, "v") for k in order])
```

```python
import functools

import jax
import jax.numpy as jnp
from jax import lax
from jax.experimental import pallas as pl
from jax.experimental.pallas import tpu as pltpu

F32 = jnp.float32
BF16 = jnp.bfloat16

D_MODEL = 2048
CHUNK = 64
HG_HEADS = 8
HEAD_DIM = 128
HG_WIDTH = HG_HEADS * HEAD_DIM
ATT_HEADS = 8
ATT_WIDTH = ATT_HEADS * HEAD_DIM
LEFT_CHUNKS = 8
PAD = LEFT_CHUNKS * CHUNK
BAND = PAD + CHUNK
REL_CLIP = 128
N_REL = 2 * REL_CLIP + 1
N_REL_PAD = 384
D_FF = 5632
EPS = 1e-6
ATT_SCALE = HEAD_DIM ** -0.5
SUB = 32
HG_BLOCK = 8
Q_BLOCK = 4 * CHUNK
K_BLOCK = Q_BLOCK + PAD
DIAG = 1024
MASKED = -1e30

ADAM_LR = 0.001
ADAM_B1 = 0.9
ADAM_B2 = 0.999
ADAM_EPS = 1e-08
ADAM_WD = 0.01
ADAM_STEP = 10

N_DEV = 8
VMEM_LIMIT = 48 * 1024 * 1024
MESH = pl.DeviceIdType.MESH
ANY = pl.BlockSpec(memory_space=pl.ANY)
HIGHEST = lax.Precision.HIGHEST

NN = (((1,), (0,)), ((), ()))
NT = (((1,), (1,)), ((), ()))
TN = (((0,), (0,)), ((), ()))


def _params(*sem):
    return pltpu.CompilerParams(dimension_semantics=sem if sem else None, vmem_limit_bytes=VMEM_LIMIT)


def _pallas(body, n_in, dep, **kw):
    deps = [] if dep is None else list(dep)
    if not deps:
        return pl.pallas_call(body, **kw)

    def body_after(*refs):
        body(*refs[:n_in], *refs[n_in + len(deps) :])

    call = pl.pallas_call(body_after, **dict(kw, in_specs=list(kw["in_specs"]) + [ANY] * len(deps)))
    return lambda *ops: call(*ops, *deps)


def _dot(a, b, dims=NN):
    return lax.dot_general(a, b, dims, preferred_element_type=F32)


def _dot3(a, b, dims=NN):
    a_hi, b_hi = a.astype(BF16), b.astype(BF16)
    a_lo, b_lo = (a - a_hi.astype(F32)).astype(BF16), (b - b_hi.astype(F32)).astype(BF16)
    return _dot(a_hi, b_hi, dims) + (_dot(a_hi, b_lo, dims) + _dot(a_lo, b_hi, dims))


def _sigmoid(x):
    return 1.0 / (1.0 + jnp.exp(-x))


def _tile(n, prefs):
    for t in prefs:
        if n % t == 0:
            return t
    return n


def _matmul(a, b, mode, out_dtype, name, tm=1024, tn=1024, tk=None, resid=None, dep=None):
    if mode == "nn":
        (m, k), n = a.shape, b.shape[1]
    elif mode == "nt":
        (m, k), n = a.shape, b.shape[0]
    else:
        (k, m), n = a.shape, b.shape[1]
    tm = _tile(m, (tm, 512, 256, 128))
    tn = _tile(n, (tn, 1408, 512, 256, 128))
    tk = k if tk is None else _tile(k, (tk,))
    nk = k // tk
    dims = {"nn": NN, "nt": NT, "tn": TN}[mode]
    a_spec = pl.BlockSpec((tk, tm), lambda i, j, s: (s, i)) if mode == "tn" else pl.BlockSpec((tm, tk), lambda i, j, s: (i, s))
    b_spec = pl.BlockSpec((tn, tk), lambda i, j, s: (j, s)) if mode == "nt" else pl.BlockSpec((tk, tn), lambda i, j, s: (s, j))
    o_spec = pl.BlockSpec((tm, tn), lambda i, j, s: (i, j))
    has_res = resid is not None

    def body(*refs):
        a_ref, b_ref = refs[0], refs[1]
        o_ref = refs[2 + has_res]
        part = _dot(a_ref[...].astype(BF16), b_ref[...].astype(BF16), dims)

        def finish(acc):
            if has_res:
                acc = acc + refs[2][...]
            o_ref[...] = acc.astype(out_dtype)

        if nk == 1:
            finish(part)
        else:
            acc_ref = refs[-1]
            s = pl.program_id(2)

            @pl.when(s == 0)
            def _():
                acc_ref[...] = part

            @pl.when(s > 0)
            def _():
                acc_ref[...] += part

            @pl.when(s == nk - 1)
            def _():
                finish(acc_ref[...])

    return _pallas(
        body,
        2 + has_res,
        dep,
        name=name,
        grid=(m // tm, n // tn, nk),
        in_specs=[a_spec, b_spec] + ([o_spec] if has_res else []),
        out_specs=o_spec,
        out_shape=jax.ShapeDtypeStruct((m, n), out_dtype),
        scratch_shapes=[pltpu.VMEM((tm, tn), F32)] if nk > 1 else [],
        compiler_params=_params("parallel", "parallel", "arbitrary"),
    )(*([a, b] + ([resid] if has_res else [])))


def _rms_fwd(x, g, name, dep=None):
    t, d = x.shape
    tm = _tile(t, (256,))

    def body(x_ref, g_ref, a_ref, r_ref):
        xv = x_ref[...]
        r = lax.rsqrt(jnp.mean(xv * xv, axis=-1, keepdims=True) + EPS)
        a_ref[...] = (xv * r * g_ref[...]).astype(BF16)
        r_ref[...] = r

    row = pl.BlockSpec((tm, d), lambda i: (i, 0))
    return _pallas(
        body,
        2,
        dep,
        name=name,
        grid=(t // tm,),
        in_specs=[row, pl.BlockSpec((1, d), lambda i: (0, 0))],
        out_specs=[row, pl.BlockSpec((tm, 1), lambda i: (i, 0))],
        out_shape=[jax.ShapeDtypeStruct((t, d), BF16), jax.ShapeDtypeStruct((t, 1), F32)],
        compiler_params=_params("parallel"),
    )(x, g)


def _rms_bwd(da, x, r, g, resid, name, dep=None):
    t, d = x.shape
    tm = _tile(t, (256,))

    def body(da_ref, x_ref, r_ref, g_ref, res_ref, dx_ref, dg_ref):
        i = pl.program_id(0)
        rv = r_ref[...]
        n = x_ref[...] * rv
        dav = da_ref[...]
        dn = dav * g_ref[...]
        dx_ref[...] = rv * (dn - n * jnp.mean(dn * n, axis=-1, keepdims=True)) + res_ref[...]
        part = jnp.sum(dav * n, axis=0, keepdims=True)

        @pl.when(i == 0)
        def _():
            dg_ref[...] = part

        @pl.when(i > 0)
        def _():
            dg_ref[...] += part

    row = pl.BlockSpec((tm, d), lambda i: (i, 0))
    vec = pl.BlockSpec((1, d), lambda i: (0, 0))
    return _pallas(
        body,
        5,
        dep,
        name=name,
        grid=(t // tm,),
        in_specs=[row, row, pl.BlockSpec((tm, 1), lambda i: (i, 0)), vec, row],
        out_specs=[row, vec],
        out_shape=[jax.ShapeDtypeStruct((t, d), F32), jax.ShapeDtypeStruct((1, d), F32)],
        compiler_params=_params("arbitrary"),
    )(da, x, r, g, resid)


def _tri(n, upper):
    r = lax.broadcasted_iota(jnp.int32, (n, n), 0)
    c = lax.broadcasted_iota(jnp.int32, (n, n), 1)
    return jnp.where((c >= r) if upper else (c <= r), 1.0, 0.0).astype(F32)


def _hgrn_gates(q, fp, lbl):
    l0, l1 = lbl[0:1, :], lbl[1:2, :]
    mx = jnp.maximum(l0, l1)
    e0, e1 = jnp.exp(l0 - mx), jnp.exp(l1 - mx)
    lb = e0 / (e0 + e1)
    sig = _sigmoid(fp)
    f = lb + (1.0 - lb) * sig
    kk = (1.0 - lb) * _sigmoid(-fp)
    sq = _sigmoid(q)
    b = jnp.dot(_tri(CHUNK, False), jnp.log(f), precision=HIGHEST, preferred_element_type=F32)
    return lb, sig, f, kk, sq, q * sq, b


def _heads(x):
    return [x[:, j * HEAD_DIM : (j + 1) * HEAD_DIM] for j in range(x.shape[1] // HEAD_DIM)]


def _wide(parts):
    return jnp.concatenate(parts, axis=1)


def _intra_blocks(b):
    out = []
    for lo in range(0, CHUNK, SUB):
        hi = lo + SUB
        br = b[lo + SUB // 2 : lo + SUB // 2 + 1, :]
        row = lax.broadcasted_iota(jnp.int32, (SUB, hi), 0) + lo
        col = lax.broadcasted_iota(jnp.int32, (SUB, hi), 1)
        out.append((lo, hi, jnp.exp(b[lo:hi] - br), jnp.exp(br - b[:hi]), col <= row))
    return out


def _hgrn_fwd(proj, lb_logits, hg_norm):
    t = proj.shape[0]
    nc = t // CHUNK

    def body(q_ref, f_ref, i_ref, g_ref, lbl_ref, hgn_ref, y_ref, o_ref, st_ref, s_scr):
        c = pl.program_id(1)

        @pl.when(c == 0)
        def _():
            s_scr[...] = jnp.zeros_like(s_scr)

        hs = range(HG_BLOCK)
        sts = [s_scr[j] for j in hs]
        _, _, _, kk, _, qf, b = _hgrn_gates(q_ref[...], f_ref[...], lbl_ref[...])
        vb = _heads(i_ref[...].astype(BF16))
        bl = b[CHUNK - 1 : CHUNK, :]
        qe = _heads((qf * jnp.exp(b)).astype(BF16))
        kd = _heads((kk * jnp.exp(bl - b)).astype(BF16))
        decay = _heads(jnp.exp(bl))
        o = [_dot(qe[j], sts[j].astype(BF16), NT) for j in hs]
        parts = [[] for _ in hs]
        for lo, hi, ea, eb, mask in _intra_blocks(b):
            a, bk = _heads((qf[lo:hi] * ea).astype(BF16)), _heads((kk[:hi] * eb).astype(BF16))
            p = [jnp.where(mask, _dot(a[j], bk[j], NT), 0.0).astype(BF16) for j in hs]
            for j in hs:
                parts[j].append(_dot(p[j], vb[j][:hi]))
        o = [o[j] + jnp.concatenate(parts[j], axis=0) for j in hs]
        new = [sts[j] * decay[j] + _dot(vb[j], kd[j], TN) for j in hs]
        hgn = hgn_ref[...]
        on = [o[j] * lax.rsqrt(jnp.mean(o[j] * o[j], axis=-1, keepdims=True) + EPS) * hgn for j in hs]
        gg = g_ref[...]
        for j in hs:
            st_ref[j] = sts[j]
            s_scr[j] = new[j]
        o_ref[...] = _wide(o)
        y_ref[...] = (_wide(on) * (gg * _sigmoid(gg))).astype(BF16)

    wide = HG_BLOCK * HEAD_DIM
    groups = HG_HEADS // HG_BLOCK

    def col(k):
        return pl.BlockSpec((CHUNK, wide), lambda g, c: (c, k * groups + g))

    out = pl.BlockSpec((CHUNK, wide), lambda g, c: (c, g))
    return pl.pallas_call(
        body,
        name="hgrn_fwd",
        grid=(groups, nc),
        in_specs=[col(0), col(1), col(2), col(3), pl.BlockSpec((2, wide), lambda g, c: (0, g)), pl.BlockSpec((1, HEAD_DIM), lambda g, c: (0, 0))],
        out_specs=[out, out, pl.BlockSpec((HG_BLOCK, None, HEAD_DIM, HEAD_DIM), lambda g, c: (g, c, 0, 0))],
        out_shape=[
            jax.ShapeDtypeStruct((t, HG_WIDTH + ATT_WIDTH), BF16),
            jax.ShapeDtypeStruct((t, HG_WIDTH), F32),
            jax.ShapeDtypeStruct((HG_HEADS, nc, HEAD_DIM, HEAD_DIM), F32),
        ],
        scratch_shapes=[pltpu.VMEM((HG_BLOCK, HEAD_DIM, HEAD_DIM), F32)],
        compiler_params=_params("arbitrary", "arbitrary"),
    )(proj, proj, proj, proj, lb_logits, hg_norm)


def _hgrn_bwd(proj, lb_logits, hg_norm, o_hg, dycat, states, dep=None):
    t = proj.shape[0]
    nc = t // CHUNK

    def body(q_ref, f_ref, i_ref, g_ref, lbl_ref, hgn_ref, o_ref, dy_ref, st_ref, dp_ref, dlbl_ref, dhgn_ref, dst_scr, dlb_scr):
        h = pl.program_id(0)
        c = pl.program_id(1)

        @pl.when(c == 0)
        def _():
            dst_scr[...] = jnp.zeros_like(dst_scr)
            dlb_scr[...] = jnp.zeros_like(dlb_scr)

        @pl.when((c == 0) & (h == 0))
        def _():
            dhgn_ref[...] = jnp.zeros_like(dhgn_ref)

        hs = range(HG_BLOCK)
        hgn = _wide([hgn_ref[...]] * HG_BLOCK)
        q, fp, gg, vi = q_ref[...], f_ref[...], g_ref[...], i_ref[...]
        lb, sig, f, kk, sq, qf, b = _hgrn_gates(q, fp, lbl_ref[...])
        o, dy = o_ref[...], dy_ref[...]
        sg = _sigmoid(gg)
        n = _wide([oh * lax.rsqrt(jnp.mean(oh * oh, axis=-1, keepdims=True) + EPS) for oh in _heads(o)])
        don = dy * (gg * sg)
        dgg = dy * (n * hgn) * (sg * (1.0 + gg * (1.0 - sg)))
        d_hgn = sum(_heads(jnp.sum(don * n, axis=0, keepdims=True)))
        dn = don * hgn
        do = _wide(
            [
                lax.rsqrt(jnp.mean(oh * oh, axis=-1, keepdims=True) + EPS) * (dnh - nh * jnp.mean(dnh * nh, axis=-1, keepdims=True))
                for oh, dnh, nh in zip(_heads(o), _heads(dn), _heads(n))
            ]
        )
        sts = [st_ref[j] for j in hs]
        dstn = [dst_scr[j] for j in hs]
        bl = b[CHUNK - 1 : CHUNK, :]
        e_b, e_bl, e_l = jnp.exp(b), jnp.exp(bl - b), jnp.exp(bl)
        doh, vih = _heads(do), _heads(vi)
        dobh = _heads(do.astype(BF16))
        dq_acc = _wide([_dot3(doh[j], sts[j]) for j in hs]) * e_b
        dk_inter = _wide([_dot3(vih[j], dstn[j]) for j in hs]) * e_bl
        dk_acc = dk_inter
        kd = _heads((kk * e_bl).astype(BF16))
        dv_acc = _wide([_dot(kd[j], dstn[j].astype(BF16), NT) for j in hs])
        qe, decay = _heads((qf * e_b).astype(BF16)), _heads(e_l)
        dst_new = [dstn[j] * decay[j] + _dot(dobh[j], qe[j], TN) for j in hs]
        db_last = e_l * _wide([jnp.sum(sts[j] * dstn[j], axis=0, keepdims=True) for j in hs]) + jnp.sum(kk * dk_inter, axis=0, keepdims=True)
        dq_parts = []
        for lo, hi, ea, eb, mask in _intra_blocks(b):
            a, bk = qf[lo:hi] * ea, kk[:hi] * eb
            ah, bkh = _heads(a), _heads(bk)
            abh, bkbh = _heads(a.astype(BF16)), _heads(bk.astype(BF16))
            p = [jnp.where(mask, _dot(abh[j], bkbh[j], NT), 0.0).astype(BF16) for j in hs]
            dp = [jnp.where(mask, _dot3(doh[j][lo:hi], vih[j][:hi], NT), 0.0) for j in hs]
            dq_parts.append(_wide([_dot3(dp[j], bkh[j]) for j in hs]) * ea)
            dki = _wide([_dot3(dp[j], ah[j], TN) for j in hs]) * eb
            dvi = _wide([_dot(p[j], dobh[j][lo:hi], TN) for j in hs])
            if hi < CHUNK:
                zeros = jnp.zeros((CHUNK - hi, HG_BLOCK * HEAD_DIM), F32)
                dki = jnp.concatenate([dki, zeros], axis=0)
                dvi = jnp.concatenate([dvi, zeros], axis=0)
            dk_acc = dk_acc + dki
            dv_acc = dv_acc + dvi
        dq_acc = dq_acc + jnp.concatenate(dq_parts, axis=0)
        rows = lax.broadcasted_iota(jnp.int32, dq_acc.shape, 0)
        db = qf * dq_acc - kk * dk_acc + jnp.where(rows == CHUNK - 1, db_last, 0.0)
        dlf = jnp.dot(_tri(CHUNK, True), db, precision=HIGHEST, preferred_element_type=F32)
        dfk = dlf / f - dk_acc
        for k, part in enumerate((dq_acc * (sq * (1.0 + q * (1.0 - sq))), (1.0 - lb) * dfk * sig * (1.0 - sig), dv_acc, dgg)):
            dp_ref[:, k * HG_WIDTH : (k + 1) * HG_WIDTH] = part.astype(BF16)
        dlb_scr[...] += jnp.sum(dfk * (1.0 - sig), axis=0, keepdims=True)
        dhgn_ref[...] += d_hgn
        for j in hs:
            dst_scr[j] = dst_new[j]

        @pl.when(c == nc - 1)
        def _():
            dl0 = dlb_scr[...] * lb * (1.0 - lb)
            dlbl_ref[0:1, :] = dl0
            dlbl_ref[1:2, :] = -dl0

    wide = HG_BLOCK * HEAD_DIM
    groups = HG_HEADS // HG_BLOCK

    def col(k):
        return pl.BlockSpec((CHUNK, wide), lambda g, c: (nc - 1 - c, k * groups + g))

    blk = pl.BlockSpec((CHUNK, wide), lambda g, c: (nc - 1 - c, g))
    assert groups == 1, "d(q, f, i, g) are written as one contiguous column range of the in_proj gradient"
    return _pallas(
        body,
        9,
        dep,
        name="hgrn_bwd",
        grid=(groups, nc),
        in_specs=[
            col(0), col(1), col(2), col(3),
            pl.BlockSpec((2, wide), lambda g, c: (0, g)),
            pl.BlockSpec((1, HEAD_DIM), lambda g, c: (0, 0)),
            blk, blk,
            pl.BlockSpec((HG_BLOCK, None, HEAD_DIM, HEAD_DIM), lambda g, c: (g, nc - 1 - c, 0, 0)),
        ],
        out_specs=[
            pl.BlockSpec((CHUNK, 4 * HG_WIDTH), lambda g, c: (nc - 1 - c, 0)),
            pl.BlockSpec((2, wide), lambda g, c: (0, g)),
            pl.BlockSpec((1, HEAD_DIM), lambda g, c: (0, 0)),
        ],
        out_shape=[
            jax.ShapeDtypeStruct((t, 4 * HG_WIDTH + 3 * ATT_WIDTH), BF16),
            jax.ShapeDtypeStruct((2, HG_WIDTH), F32),
            jax.ShapeDtypeStruct((1, HEAD_DIM), F32),
        ],
        scratch_shapes=[pltpu.VMEM((HG_BLOCK, HEAD_DIM, HEAD_DIM), F32), pltpu.VMEM((1, wide), F32)],
        compiler_params=_params("arbitrary", "arbitrary"),
    )(proj, proj, proj, proj, lb_logits, hg_norm, o_hg, dycat, states)


def _diagonal_slots(shift):
    i = lax.broadcasted_iota(jnp.int32, (N_REL_PAD, DIAG), 0)
    u = lax.broadcasted_iota(jnp.int32, (N_REL_PAD, DIAG), 1)
    offset = u - shift if shift else jnp.where(u < K_BLOCK, u, u - DIAG)
    return jnp.where(jnp.clip(PAD - offset, -REL_CLIP, REL_CLIP) + REL_CLIP == i, 1.0, 0.0).astype(BF16)


def _split3(x):
    hi = x.astype(BF16)
    mid = (x - hi.astype(F32)).astype(BF16)
    return hi, mid, (x - hi.astype(F32) - mid.astype(F32)).astype(BF16)


def _bias_table(rel_bias):
    def body(rb_ref, o_ref, diag):
        h = pl.program_id(0)

        @pl.when(h == 0)
        def _():
            hi, mid, lo = _split3(rb_ref[...])
            slots = _diagonal_slots(0)
            diag[...] = _dot(hi, slots) + (_dot(mid, slots) + _dot(lo, slots))

        rows = jnp.broadcast_to(diag[pl.ds(h, 1), :], (Q_BLOCK, DIAG))
        row = lax.broadcasted_iota(jnp.int32, (Q_BLOCK, K_BLOCK), 0)
        col = lax.broadcasted_iota(jnp.int32, (Q_BLOCK, K_BLOCK), 1)
        first = row - (row & (CHUNK - 1))
        seen = (col >= first) & (col < first + BAND)
        o_ref[...] = jnp.where(seen, pltpu.roll(rows, 0, 1, stride=1, stride_axis=0)[:, :K_BLOCK], MASKED)

    return pl.pallas_call(
        body,
        name="bias_table",
        grid=(ATT_HEADS,),
        in_specs=[pl.BlockSpec((ATT_HEADS, N_REL_PAD), lambda h: (0, 0))],
        out_specs=pl.BlockSpec((None, Q_BLOCK, K_BLOCK), lambda h: (h, 0, 0)),
        out_shape=jax.ShapeDtypeStruct((ATT_HEADS, Q_BLOCK, K_BLOCK), F32),
        scratch_shapes=[pltpu.VMEM((ATT_HEADS, DIAG), F32)],
        compiler_params=_params("arbitrary"),
    )(rel_bias)


def _att_probs(q_ref, kpad, bias_ref, blk):
    qs = (q_ref[...] * ATT_SCALE).astype(BF16)
    start = pl.multiple_of(blk * Q_BLOCK, Q_BLOCK)
    kb = kpad[pl.ds(start, K_BLOCK), :]
    s = _dot(qs, kb, NT) + bias_ref[...]
    col = lax.broadcasted_iota(jnp.int32, (Q_BLOCK, K_BLOCK), 1)
    s = jnp.where(col >= PAD - blk * Q_BLOCK, s, MASKED)
    e = jnp.exp(s - jnp.max(s, axis=-1, keepdims=True))
    return qs, kb, start, e * (1.0 / jnp.sum(e, axis=-1, keepdims=True))


def _fill_padded(dst, src):
    dst[0:PAD, :] = jnp.zeros((PAD, HEAD_DIM), BF16)
    dst[PAD:, :] = src[...].astype(BF16)


def _att_fwd(proj, bias, dep=None):
    t = proj.shape[0]
    nb = t // Q_BLOCK

    def body(q_ref, k_ref, v_ref, bias_ref, y_ref, kpad, vpad):
        c = pl.program_id(1)

        @pl.when(c == 0)
        def _():
            _fill_padded(kpad, k_ref)
            _fill_padded(vpad, v_ref)

        _, _, start, p = _att_probs(q_ref, kpad, bias_ref, c)
        y_ref[...] = _dot(p.astype(BF16), vpad[pl.ds(start, K_BLOCK), :]).astype(BF16)

    base = 4 * HG_HEADS
    return _pallas(
        body,
        4,
        dep,
        name="att_fwd",
        grid=(ATT_HEADS, nb),
        in_specs=[
            pl.BlockSpec((Q_BLOCK, HEAD_DIM), lambda h, c: (c, base + h)),
            pl.BlockSpec((t, HEAD_DIM), lambda h, c: (0, base + ATT_HEADS + h)),
            pl.BlockSpec((t, HEAD_DIM), lambda h, c: (0, base + 2 * ATT_HEADS + h)),
            pl.BlockSpec((None, Q_BLOCK, K_BLOCK), lambda h, c: (h, 0, 0)),
        ],
        out_specs=pl.BlockSpec((Q_BLOCK, HEAD_DIM), lambda h, c: (c, h)),
        out_shape=jax.ShapeDtypeStruct((t, ATT_WIDTH), BF16),
        scratch_shapes=[pltpu.VMEM((t + PAD, HEAD_DIM), BF16), pltpu.VMEM((t + PAD, HEAD_DIM), BF16)],
        compiler_params=_params("arbitrary", "arbitrary"),
    )(proj, proj, proj, bias)


def _att_bwd(proj, bias, dycat, dep=None):
    t = proj.shape[0]
    nb = t // Q_BLOCK

    def body(q_ref, k_ref, v_ref, bias_ref, dy_ref, dq_ref, dk_ref, dv_ref, g_ref, kpad, vpad, dkacc, dvacc):
        c = pl.program_id(1)

        @pl.when(c == 0)
        def _():
            _fill_padded(kpad, k_ref)
            _fill_padded(vpad, v_ref)
            dkacc[...] = jnp.zeros_like(dkacc)
            dvacc[...] = jnp.zeros_like(dvacc)
            g_ref[...] = jnp.zeros_like(g_ref)

        qs, kb, start, p = _att_probs(q_ref, kpad, bias_ref, c)
        band = pl.ds(start, K_BLOCK)
        dyb = dy_ref[...].astype(BF16)
        dvacc[band, :] += _dot(p.astype(BF16), dyb, TN)
        dp = _dot(dyb, vpad[band, :], NT)
        ds = p * (dp - jnp.sum(dp * p, axis=-1, keepdims=True))
        g_ref[...] += ds
        dsb = ds.astype(BF16)
        dq_ref[...] = (_dot(dsb, kb) * ATT_SCALE).astype(BF16)
        dkacc[band, :] += _dot(dsb, qs, TN)

        @pl.when(c == nb - 1)
        def _():
            dk_ref[...] = dkacc[PAD:, :].astype(BF16)
            dv_ref[...] = dvacc[PAD:, :].astype(BF16)

    base = 4 * HG_HEADS
    whole = pl.BlockSpec((t, HEAD_DIM), lambda h, c: (0, h))
    return _pallas(
        body,
        5,
        dep,
        name="att_bwd",
        grid=(ATT_HEADS, nb),
        in_specs=[
            pl.BlockSpec((Q_BLOCK, HEAD_DIM), lambda h, c: (c, base + h)),
            pl.BlockSpec((t, HEAD_DIM), lambda h, c: (0, base + ATT_HEADS + h)),
            pl.BlockSpec((t, HEAD_DIM), lambda h, c: (0, base + 2 * ATT_HEADS + h)),
            pl.BlockSpec((None, Q_BLOCK, K_BLOCK), lambda h, c: (h, 0, 0)),
            pl.BlockSpec((Q_BLOCK, HEAD_DIM), lambda h, c: (c, HG_HEADS + h)),
        ],
        out_specs=[pl.BlockSpec((Q_BLOCK, HEAD_DIM), lambda h, c: (c, h)), whole, whole, pl.BlockSpec((None, Q_BLOCK, K_BLOCK), lambda h, c: (h, 0, 0))],
        out_shape=[
            jax.ShapeDtypeStruct((t, ATT_WIDTH), BF16),
            jax.ShapeDtypeStruct((t, ATT_WIDTH), BF16),
            jax.ShapeDtypeStruct((t, ATT_WIDTH), BF16),
            jax.ShapeDtypeStruct((ATT_HEADS, Q_BLOCK, K_BLOCK), F32),
        ],
        scratch_shapes=[
            pltpu.VMEM((t + PAD, HEAD_DIM), BF16),
            pltpu.VMEM((t + PAD, HEAD_DIM), BF16),
            pltpu.VMEM((t + PAD, HEAD_DIM), F32),
            pltpu.VMEM((t + PAD, HEAD_DIM), F32),
        ],
        compiler_params=_params("arbitrary", "arbitrary"),
    )(proj, proj, proj, bias, dycat)


def _rel_bias_grad(gsum):
    def body(g_ref, o_ref):
        r = lax.broadcasted_iota(jnp.int32, (Q_BLOCK, Q_BLOCK), 0)
        c = lax.broadcasted_iota(jnp.int32, (Q_BLOCK, Q_BLOCK), 1)
        flip = jnp.where(r + c == Q_BLOCK - 1, 1.0, 0.0).astype(BF16)
        sums = []
        for h in range(ATT_HEADS):
            hi, mid, lo = _split3(g_ref[h])
            rev = _dot(flip, hi) + (_dot(flip, mid) + _dot(flip, lo))
            wide = jnp.concatenate([rev, jnp.zeros((Q_BLOCK, DIAG - K_BLOCK), F32)], axis=1)
            sums.append(jnp.sum(pltpu.roll(wide, 0, 1, stride=1, stride_axis=0), axis=0, keepdims=True))
        hi, mid, lo = _split3(jnp.concatenate(sums, axis=0))
        slots = _diagonal_slots(Q_BLOCK - 1)
        o_ref[...] = _dot(hi, slots, NT) + (_dot(mid, slots, NT) + _dot(lo, slots, NT))

    return pl.pallas_call(
        body,
        name="rel_bias_grad",
        out_shape=jax.ShapeDtypeStruct((ATT_HEADS, N_REL_PAD), F32),
        compiler_params=_params(),
    )(gsum)


HALO = 16


FF_TILE = 1408
FF_TILES = D_FF // FF_TILE


def _interleave_cols(a):
    lead = a.shape[:-1]
    return jnp.swapaxes(a.reshape(*lead, 2, FF_TILES, FF_TILE), -3, -2).reshape(*lead, 2 * D_FF)


def _deinterleave_cols(a):
    lead = a.shape[:-1]
    return jnp.swapaxes(a.reshape(*lead, FF_TILES, 2, FF_TILE), -3, -2).reshape(*lead, 2 * D_FF)


def _ffn_specs(t, tm):
    wide = 2 * FF_TILE
    tile = pl.BlockSpec((tm, wide), lambda j, i: (i, j))
    before = pl.BlockSpec((HALO, wide), lambda j, i: (jnp.maximum(i * (tm // HALO) - 1, 0), j))
    after = pl.BlockSpec((HALO, wide), lambda j, i: (jnp.minimum((i + 1) * (tm // HALO), t // HALO - 1), j))
    vec = lambda rows: pl.BlockSpec((rows, wide), lambda j, i: (0, j))
    return tile, before, after, vec


def _conv(x, w, b, rows):
    taps = [pltpu.roll(x, 2, 0)[HALO : HALO + rows], pltpu.roll(x, 1, 0)[HALO : HALO + rows], x[HALO : HALO + rows]]
    return b + w[0:1] * taps[0] + w[1:2] * taps[1] + w[2:3] * taps[2], taps


def _ffn_act_fwd(u, conv_w, conv_b):
    t = u.shape[0]
    tm = _tile(t, (128,))
    tile, before, _, vec = _ffn_specs(t, tm)

    def body(u_ref, h_ref, w_ref, b_ref, z_ref):
        first = pl.program_id(1) == 0
        x = jnp.concatenate([jnp.where(first, 0.0, h_ref[...].astype(F32)), u_ref[...].astype(F32)], axis=0)
        c, _ = _conv(x, w_ref[...], b_ref[...], tm)
        gate, val = c[:, :FF_TILE], c[:, FF_TILE:]
        z_ref[...] = (gate * _sigmoid(gate) * val).astype(BF16)

    return pl.pallas_call(
        body,
        name="ffn_act_fwd",
        grid=(FF_TILES, t // tm),
        in_specs=[tile, before, vec(3), vec(1)],
        out_specs=pl.BlockSpec((tm, FF_TILE), lambda j, i: (i, j)),
        out_shape=jax.ShapeDtypeStruct((t, D_FF), BF16),
        compiler_params=_params("parallel", "parallel"),
    )(u, u, conv_w, conv_b)


def _ffn_act_bwd(u, dz, conv_w, conv_b, dep=None):
    t = u.shape[0]
    tm = _tile(t, (128,))
    nt = t // tm
    ext = tm + HALO
    tile, before, after, vec = _ffn_specs(t, tm)

    def body(u_ref, ub_ref, ua_ref, w_ref, b_ref, dz_ref, dza_ref, du_ref, dw_ref, db_ref):
        i = pl.program_id(1)
        first, last = i == 0, i == nt - 1
        parts = [jnp.where(first, 0.0, ub_ref[...].astype(F32)), u_ref[...].astype(F32), jnp.where(last, 0.0, ua_ref[...].astype(F32))]
        w = w_ref[...]
        c, taps = _conv(jnp.concatenate(parts, axis=0), w, b_ref[...], ext)
        gate, val = c[:, :FF_TILE], c[:, FF_TILE:]
        dz = jnp.concatenate([dz_ref[...].astype(F32), jnp.where(last, 0.0, dza_ref[...].astype(F32))], axis=0)
        sg = _sigmoid(gate)
        d = jnp.concatenate([dz * val * (sg * (1.0 + gate * (1.0 - sg))), dz * (gate * sg)], axis=1)
        du = w[2:3] * d[:tm] + w[1:2] * pltpu.roll(d, ext - 1, 0)[:tm] + w[0:1] * pltpu.roll(d, ext - 2, 0)[:tm]
        du_ref[...] = du.astype(BF16)

        @pl.when(first)
        def _():
            dw_ref[...] = jnp.zeros_like(dw_ref)
            db_ref[...] = jnp.zeros_like(db_ref)

        for k, tap in enumerate(taps):
            dw_ref[k : k + 1, :] += jnp.sum(d[:tm] * tap[:tm], axis=0, keepdims=True)
        db_ref[...] += jnp.sum(d[:tm], axis=0, keepdims=True)

    narrow = lambda rows, index: pl.BlockSpec((rows, FF_TILE), index)
    return _pallas(
        body,
        7,
        dep,
        name="ffn_act_bwd",
        grid=(FF_TILES, nt),
        in_specs=[
            tile, before, after, vec(3), vec(1),
            narrow(tm, lambda j, i: (i, j)),
            narrow(HALO, lambda j, i: (jnp.minimum((i + 1) * (tm // HALO), t // HALO - 1), j)),
        ],
        out_specs=[tile, vec(3), vec(1)],
        out_shape=[
            jax.ShapeDtypeStruct((t, 2 * D_FF), BF16),
            jax.ShapeDtypeStruct((3, 2 * D_FF), F32),
            jax.ShapeDtypeStruct((1, 2 * D_FF), F32),
        ],
        compiler_params=_params("parallel", "arbitrary"),
    )(u, u, u, conv_w, conv_b, dz, dz)


def _ple_loss(gpre, pp, h2, final_norm, target):
    t, d = h2.shape
    tm = _tile(t, (256,))

    def body(gp_ref, pp_ref, h_ref, g_ref, tg_ref, dh_ref, dgp_ref, dpp_ref, dg_ref, loss_ref):
        i = pl.program_id(0)
        gate = _sigmoid(gp_ref[...])
        ppv = pp_ref[...]
        h3 = h_ref[...] + gate * ppv
        r = lax.rsqrt(jnp.mean(h3 * h3, axis=-1, keepdims=True) + EPS)
        n = h3 * r
        g = g_ref[...]
        err = n * g - tg_ref[...]
        loss = 0.5 * jnp.sum(jnp.mean(err * err, axis=-1, keepdims=True))
        dy = err * (1.0 / d)
        dn = dy * g
        dh = r * (dn - n * jnp.mean(dn * n, axis=-1, keepdims=True))
        dh_ref[...] = dh
        dgp_ref[...] = (dh * ppv * gate * (1.0 - gate)).astype(BF16)
        dpp_ref[...] = (dh * gate).astype(BF16)
        dg = jnp.sum(dy * n, axis=0, keepdims=True)

        @pl.when(i == 0)
        def _():
            dg_ref[...] = dg
            loss_ref[...] = jnp.full(loss_ref.shape, loss, F32)

        @pl.when(i > 0)
        def _():
            dg_ref[...] += dg
            loss_ref[...] += loss

    row = pl.BlockSpec((tm, d), lambda i: (i, 0))
    vec = pl.BlockSpec((1, d), lambda i: (0, 0))
    return pl.pallas_call(
        body,
        name="ple_loss",
        grid=(t // tm,),
        in_specs=[row, row, row, vec, row],
        out_specs=[row, row, row, vec, pl.BlockSpec((8, 128), lambda i: (0, 0))],
        out_shape=[
            jax.ShapeDtypeStruct((t, d), F32),
            jax.ShapeDtypeStruct((t, d), BF16),
            jax.ShapeDtypeStruct((t, d), BF16),
            jax.ShapeDtypeStruct((1, d), F32),
            jax.ShapeDtypeStruct((8, 128), F32),
        ],
        compiler_params=_params("arbitrary"),
    )(gpre, pp, h2, final_norm, target)


def _adamw(w, g, m, v):
    m = ADAM_B1 * m + (1.0 - ADAM_B1) * g
    v = ADAM_B2 * v + (1.0 - ADAM_B2) * (g * g)
    m_hat = m / (1.0 - ADAM_B1 ** ADAM_STEP)
    v_hat = v / (1.0 - ADAM_B2 ** ADAM_STEP)
    return -ADAM_LR * (m_hat / (jnp.sqrt(v_hat) + ADAM_EPS) + ADAM_WD * w), m, v


def _adam_big(w, m, v, own, recv, name, dep=None):
    r, c = w.shape
    tr = _tile(r, (256, 176))

    def body(w_ref, m_ref, v_ref, own_ref, recv_ref, g_ref, d_ref, nm_ref, nv_ref):
        g = own_ref[...]
        for k in range(3):
            g = g + recv_ref[k].astype(F32)
        g_ref[...] = g
        d_ref[...], nm_ref[...], nv_ref[...] = _adamw(w_ref[...], g, m_ref[...], v_ref[...])

    blk = pl.BlockSpec((tr, c), lambda i: (i, 0))
    return _pallas(
        body,
        5,
        dep,
        name=name,
        grid=(r // tr,),
        in_specs=[blk, blk, blk, blk, pl.BlockSpec((3, tr, c), lambda i: (0, i, 0))],
        out_specs=[blk] * 4,
        out_shape=[jax.ShapeDtypeStruct((r, c), F32)] * 4,
        compiler_params=_params("parallel"),
    )(w, m, v, own, recv)


def _adam_small(w, g, m, v):
    def body(w_ref, g_ref, m_ref, v_ref, d_ref, nm_ref, nv_ref):
        d_ref[...], nm_ref[...], nv_ref[...] = _adamw(w_ref[...], g_ref[...], m_ref[...], v_ref[...])

    return pl.pallas_call(body, name="adam_small", out_shape=[jax.ShapeDtypeStruct(w.shape, F32)] * 3, compiler_params=_params())(w, g, m, v)


def _cast_bf16(w, name):
    r, c = w.shape
    tr = _tile(r, (256, 176))

    def body(w_ref, o_ref):
        o_ref[...] = w_ref[...].astype(BF16)

    blk = pl.BlockSpec((tr, c), lambda i: (i, 0))
    return pl.pallas_call(
        body, name=name, grid=(r // tr,), in_specs=[blk], out_specs=blk, out_shape=jax.ShapeDtypeStruct((r, c), BF16), compiler_params=_params("parallel")
    )(w)


def _position():
    return lax.axis_index("x"), lax.axis_index("y"), lax.axis_index("c")


def _other_chips(x, y):
    return [(1 - x, y), (x, 1 - y), (1 - x, 1 - y)]


def _block_index(dev, interleaved):
    x, y, c = dev
    return 4 * y + 2 * c + x if interleaved else 4 * x + 2 * y + c


def _shard_of(ref, axis, size, dev, interleaved=False):
    start = pl.multiple_of(_block_index(dev, interleaved) * size, 128 if axis == 1 else 16)
    return ref.at[:, pl.ds(start, size)] if axis == 1 else ref.at[pl.ds(start, size), :]


def _all_gather(shards, axes, interleaved):
    n = len(shards)

    def body(*refs):
        ins, outs = refs[:n], refs[n : 2 * n]
        send_sems, recv_sems, local_sems = refs[2 * n :]
        x, y, c = _position()
        me, sibling = (x, y, c), (x, y, 1 - c)
        chips = _other_chips(x, y)
        firsts, passed, locals_ = [], [], []
        for w in range(n):
            size = shards[w].shape[axes[w]]
            slot = functools.partial(_shard_of, outs[w], axes[w], size, interleaved=interleaved[w])

            def copy(k, block, to, src=None, w=w, slot=slot):
                return pltpu.make_async_remote_copy(
                    src_ref=slot(block) if src is None else src,
                    dst_ref=slot(block),
                    send_sem=send_sems.at[7 * w + k],
                    recv_sem=recv_sems.at[7 * w + k],
                    device_id=to,
                    device_id_type=MESH,
                )

            mine = pltpu.make_async_copy(ins[w], slot(me), local_sems.at[w])
            mine.start()
            locals_.append(mine)
            first = [copy(0, me, sibling, src=ins[w])] + [copy(1 + j, me, (*chip, c), src=ins[w]) for j, chip in enumerate(chips)]
            for cp in first:
                cp.start()
            firsts.append((first, copy))
        for w in range(n):
            first, copy = firsts[w]
            fwd = [copy(4 + j, (*chip, c), sibling) for j, chip in enumerate(chips)]
            for j, chip in enumerate(chips):
                copy(1 + j, (*chip, c), me).wait_recv()
                fwd[j].start()
            passed.append(fwd)
        for w in range(n):
            first, copy = firsts[w]
            copy(0, sibling, me).wait_recv()
            for j, chip in enumerate(chips):
                copy(4 + j, (*chip, 1 - c), me).wait_recv()
            for cp in first + passed[w]:
                cp.wait_send()
            locals_[w].wait()

    def full(s, ax):
        shape = list(s.shape)
        shape[ax] *= N_DEV
        return jax.ShapeDtypeStruct(tuple(shape), s.dtype)

    return pl.pallas_call(
        body,
        name="all_gather_weights",
        in_specs=[ANY] * n,
        out_specs=[ANY] * n,
        out_shape=[full(s, ax) for s, ax in zip(shards, axes)],
        scratch_shapes=[pltpu.SemaphoreType.DMA((7 * n,)), pltpu.SemaphoreType.DMA((7 * n,)), pltpu.SemaphoreType.DMA((n,))],
    )(*shards)


def _add_blocks(ids, grad, landed, axis, size, targets, out_dtype, name):
    rows = size if axis == 0 else grad.shape[0]
    cols = size if axis == 1 else grad.shape[1]
    tr = _tile(rows, (256, 176))
    nr = rows // tr
    nt = len(targets)

    def body(ids_ref, g_ref, l_ref, o_ref):
        o_ref[...] = (g_ref[...] + l_ref[...]).astype(out_dtype)

    if axis == 1:
        g_spec = pl.BlockSpec((tr, cols), lambda k, i, ids: (i, ids[targets[0] + k]))
    else:
        g_spec = pl.BlockSpec((tr, cols), lambda k, i, ids: (ids[targets[0] + k] * nr + i, 0))
    return pl.pallas_call(
        body,
        name=name,
        grid_spec=pltpu.PrefetchScalarGridSpec(
            num_scalar_prefetch=1,
            grid=(nt, nr),
            in_specs=[g_spec, pl.BlockSpec((None, tr, cols), lambda k, i, ids: (ids[4 + targets[0] + k], i, 0))],
            out_specs=pl.BlockSpec((None, tr, cols), lambda k, i, ids: (k, i, 0)),
        ),
        out_shape=jax.ShapeDtypeStruct((nt, rows, cols), out_dtype),
        compiler_params=_params("parallel", "parallel"),
    )(ids, grad, landed)


def _all_reduce_small(vec, name):
    rows = vec.shape[0]

    def body(v_ref, o_ref, land, send_sems, recv_sems):
        x, y, c = _position()
        mine = 4 * x + 2 * y + c
        copies = []
        for mask in range(1, N_DEV):
            peer = (1 - x if mask & 4 else x, 1 - y if mask & 2 else y, 1 - c if mask & 1 else c)
            copies.append(
                pltpu.make_async_remote_copy(
                    src_ref=v_ref, dst_ref=land.at[mine], send_sem=send_sems.at[mask - 1], recv_sem=recv_sems.at[mask - 1], device_id=peer, device_id_type=MESH
                )
            )
        for cp in copies:
            cp.start()
        land[mine] = v_ref[...]
        for cp in copies:
            cp.wait()
        acc = land[0]
        for k in range(1, N_DEV):
            acc = acc + land[k]
        o_ref[...] = acc

    return pl.pallas_call(
        body,
        name=name,
        out_shape=jax.ShapeDtypeStruct(vec.shape, F32),
        in_specs=[pl.BlockSpec(memory_space=pltpu.VMEM)],
        out_specs=pl.BlockSpec(memory_space=pltpu.VMEM),
        scratch_shapes=[pltpu.VMEM((N_DEV, rows, 128), F32), pltpu.SemaphoreType.DMA((N_DEV - 1,)), pltpu.SemaphoreType.DMA((N_DEV - 1,))],
    )(vec)


def _rows128(a, rows):
    flat = a.reshape(-1)
    return jnp.pad(flat, (0, rows * 128 - flat.shape[0])).reshape(rows, 128)


def _pad_rel(a):
    return jnp.pad(a.reshape(ATT_HEADS, -1)[:, :N_REL], ((0, 0), (0, N_REL_PAD - N_REL)))


SMALL = [("norm_mix", 16), ("lb_logits", 16), ("hg_norm", 8), ("rel_bias", 24), ("norm_ffn", 16), ("conv_b", 88), ("norm_ple", 16), ("final_norm", 16)]
CONV_W_FULL_ROWS = 3 * 2 * D_FF // 128
CONV_W_SHARD_ROWS = 40


def _pack_small(parts):
    return jnp.concatenate([_rows128(_pad_rel(parts[k]) if k == "rel_bias" else parts[k], rows) for k, rows in SMALL], axis=0)


def _unpack_small(packed, shapes):
    out, at = {}, 0
    for k, rows in SMALL:
        blk = packed[at : at + rows]
        at += rows
        if k == "rel_bias":
            out[k] = blk.reshape(ATT_HEADS, N_REL_PAD)[:, :N_REL].reshape(shapes[k])
        else:
            n = 1
            for s in shapes[k]:
                n *= s
            out[k] = blk.reshape(-1)[:n].reshape(shapes[k])
    return out, at


BIG = [("w_in", 1), ("w_out", 0), ("w_up", 1), ("w_down", 0), ("w_ple_gate", 0), ("w_ple_proj", 1)]


HBM = pl.BlockSpec(memory_space=pltpu.HBM)
SEM = pl.BlockSpec(memory_space=pltpu.SEMAPHORE)
EFFECT = pltpu.SideEffectType.DATAFLOW_SIDE_EFFECTING


def _copies(plan, refs, send_sems, recv_sems):
    return [
        pltpu.make_async_remote_copy(src_ref=src, dst_ref=dst, send_sem=send_sems.at[i], recv_sem=recv_sems.at[i], device_id=dev, device_id_type=MESH)
        for i, (src, dst, dev) in enumerate(plan(refs))
    ]


def _split_start(name, arrays, plan, n):
    k = len(arrays)

    def body(*refs):
        for cp in _copies(plan, refs[:k], refs[k], refs[k + 1]):
            cp.start()
        refs[-1][...] = jnp.zeros_like(refs[-1])

    out = pl.pallas_call(
        body,
        name=name,
        out_shape=(pltpu.SemaphoreType.DMA((n,)), pltpu.SemaphoreType.DMA((n,)), *[pltpu.HBM(a.shape, a.dtype) for a in arrays], jax.ShapeDtypeStruct((8, 128), F32)),
        in_specs=[HBM] * k,
        out_specs=(SEM, SEM, *[HBM] * k, pl.BlockSpec(memory_space=pltpu.VMEM)),
        input_output_aliases={i: 2 + i for i in range(k)},
        compiler_params=pltpu.CompilerParams(has_side_effects=EFFECT),
    )(*[pltpu.with_memory_space_constraint(a, pltpu.HBM) for a in arrays])
    return out[0], out[1], list(out[2 : 2 + k]), out[-1]


def _split_wait(name, send, recv, arrays, plan, after):
    k = len(arrays)

    def body(*refs):
        for cp in _copies(plan, refs[:k], refs[k], refs[k + 1]):
            cp.wait_send()
            cp.wait_recv()

    out = pl.pallas_call(
        body,
        name=name,
        out_shape=tuple(pltpu.HBM(a.shape, a.dtype) for a in arrays),
        in_specs=[HBM] * k + [SEM, SEM, ANY],
        out_specs=tuple([HBM] * k),
        input_output_aliases={i: i for i in range(k)},
        compiler_params=pltpu.CompilerParams(has_side_effects=EFFECT),
    )(*arrays, send, recv, after)
    return list(out)


def _cast_into(w, me, axis, name, dep):
    r, c = w.shape
    tr = _tile(r, (256, 176))
    nr = r // tr

    def body(me_ref, w_ref, dep_ref, o_ref):
        o_ref[...] = w_ref[...].astype(BF16)

    if axis == 1:
        shape, o_spec = (r, N_DEV * c), pl.BlockSpec((tr, c), lambda i, me: (i, me[0]))
    else:
        shape, o_spec = (N_DEV * r, c), pl.BlockSpec((tr, c), lambda i, me: (me[0] * nr + i, 0))
    return pl.pallas_call(
        body,
        name=name,
        grid_spec=pltpu.PrefetchScalarGridSpec(
            num_scalar_prefetch=1, grid=(nr,), in_specs=[pl.BlockSpec((tr, c), lambda i, me: (i, 0)), ANY], out_specs=o_spec
        ),
        out_shape=jax.ShapeDtypeStruct(shape, BF16),
        compiler_params=_params("parallel"),
    )(me, w, dep)


GATHER = [
    (["w_out"], None, "att_fwd", None),
    (["w_up"], None, "att_fwd", "norm_ffn_fwd"),
    (["w_down", "w_ple_gate", "w_ple_proj"], "att_fwd", "up_proj", "ffn_act_fwd"),
]
GROUPS = [["w_ple_proj", "w_ple_gate", "w_down"], ["w_up"], ["w_out"], ["w_in"]]
STAGES = ["ffn_act_bwd", "d_mix_out", "hgrn_bwd", "d_norm_mix_out"]
INTERLEAVED = {"w_up"}


class _Exchange:
    def __init__(self, big, position):
        self.big, self.axis = big, dict(BIG)
        self.size = {k: big[k].shape[self.axis[k]] for k in big}
        self.x, self.y, self.c = position
        chips = [(self.x, self.y)] + _other_chips(self.x, self.y)
        landed = [2 * cx + cy for cx, cy in chips]
        self.ids = {
            flag: jnp.stack([_block_index((cx, cy, self.c), flag) for cx, cy in chips] + landed).astype(jnp.int32) for flag in (False, True)
        }
        self.tokens, self.grads, self.state, self.wfull = [], {}, {}, {}


    def _slot(self, ref, k, dev):
        return _shard_of(ref, self.axis[k], self.size[k], dev, interleaved=k in INTERLEAVED)

    def _plan_gather(self, names, direct, refs):
        x, y, c = _position()
        me, out = (x, y, c), []
        for k, ref in zip(names, refs):
            mine = self._slot(ref, k, me)
            out.append((mine, mine, (x, y, 1 - c)))
            out += [(mine, mine, (*chip, c)) for chip in _other_chips(x, y)]
            if direct:
                out += [(mine, mine, (*chip, 1 - c)) for chip in _other_chips(x, y)]
        return out

    def _plan_forward(self, names, refs):
        x, y, c = _position()
        out = []
        for k, ref in zip(names, refs):
            for chip in _other_chips(x, y):
                block = self._slot(ref, k, (*chip, c))
                out.append((block, block, (x, y, 1 - c)))
        return out

    def _plan_sibling(self, names, refs):
        x, y, c = _position()
        n = len(names)
        return [(self._slot(refs[i], k, (p // 2, p % 2, 1 - c)), refs[n + i].at[p], (x, y, 1 - c)) for i, k in enumerate(names) for p in range(4)]

    def _plan_chips(self, names, refs):
        x, y, c = _position()
        n = len(names)
        return [(refs[i].at[j], refs[n + i].at[j], (*chip, c)) for i in range(n) for j, chip in enumerate(_other_chips(x, y))]


    def gather(self, conv_w):
        w_in, conv_full = _all_gather([_cast_bf16(self.big["w_in"], "cast_w_in"), conv_w], [1, 1], [False, True])
        self.wfull["w_in"] = w_in
        me = {flag: _block_index((self.x, self.y, self.c), flag).astype(jnp.int32).reshape(1) for flag in (False, True)}
        self.late, self.unsent = {}, {}
        after = w_in
        for gi, (names, issued, *_) in enumerate(GATHER):
            self.unsent[gi] = [_cast_into(self.big[k], me[k in INTERLEAVED], self.axis[k], "cast_" + k, after) for k in names]
            if issued is None:
                self._issue(None)
                after = self.tokens[-1]
        return conv_full

    def _issue(self, stage):
        for gi, (names, issued, _, forwarded) in enumerate(GATHER):
            if issued == stage and gi in self.unsent:
                plan = functools.partial(self._plan_gather, names, forwarded is None)
                copies = (7 if forwarded is None else 4) * len(names)
                send, recv, fulls, token = _split_start(f"gather_start_{gi}", self.unsent.pop(gi), plan, copies)
                self.tokens.append(token)
                self.late[gi] = (send, recv, fulls, plan)

    def weight(self, k):
        return self.wfull[k]

    def dep(self):
        tokens, self.tokens = self.tokens, []
        return tokens

    def reduce(self, vec, name):
        return _all_reduce_small(vec, name)

    def grad(self, k, g):
        self.grads[k] = g
        for gi, names in enumerate(GROUPS):
            if k == names[-1]:
                plan = functools.partial(self._plan_sibling, names)
                lands = [lax.empty((4, *self._shard_shape(n)), F32) for n in names]
                send, recv, arrays, token = _split_start(f"sibling_start_{gi}", [self.grads[n] for n in names] + lands, plan, 4 * len(names))
                self.tokens.append(token)
                self.state[gi] = (send, recv, arrays, plan)

    def done(self, stage, after):
        for gi, (names, _, _, forwarded) in enumerate(GATHER):
            if forwarded == stage:
                send, recv, fulls, plan = self.late[gi]
                self.wfull.update(zip(names, _split_wait(f"forward_wait_{gi}", send, recv, fulls, plan, after)))
        for gi, (names, _, arrived, forwarded) in enumerate(GATHER):
            if arrived == stage:
                send, recv, fulls, plan = self.late[gi]
                fulls = _split_wait(f"gather_wait_{gi}", send, recv, fulls, plan, after)
                if forwarded is None:
                    self.wfull.update(zip(names, fulls))
                else:
                    plan = functools.partial(self._plan_forward, names)
                    send, recv, fulls, token = _split_start(f"forward_start_{gi}", fulls, plan, 3 * len(names))
                    self.tokens.append(token)
                    self.late[gi] = (send, recv, fulls, plan)
        self._issue(stage)
        if stage in STAGES:
            self._to_chips(STAGES.index(stage), after)

    def _shard_shape(self, k):
        shape = list(self.grads[k].shape)
        shape[self.axis[k]] = self.size[k]
        return tuple(shape)

    def _to_chips(self, gi, after):
        names = GROUPS[gi]
        n = len(names)
        send, recv, arrays, plan = self.state[gi]
        arrays = _split_wait(f"sibling_wait_{gi}", send, recv, arrays, plan, after)
        own, parts = [], []
        for k, g, land in zip(names, arrays[:n], arrays[n:]):
            ids = self.ids[k in INTERLEAVED]
            own.append(_add_blocks(ids, g, land, self.axis[k], self.size[k], [0], F32, "add_own_" + k)[0])
            parts.append(_add_blocks(ids, g, land, self.axis[k], self.size[k], [1, 2, 3], BF16, "add_send_" + k))
        plan = functools.partial(self._plan_chips, names)
        lands = [lax.empty(part.shape, BF16) for part in parts]
        send, recv, arrays, token = _split_start(f"chips_start_{gi}", parts + lands, plan, 3 * n)
        self.tokens.append(token)
        self.state[gi] = (send, recv, arrays, plan, own)

    def finish(self, gi, after):
        names = GROUPS[gi]
        send, recv, arrays, plan, own = self.state[gi]
        arrays = _split_wait(f"chips_wait_{gi}", send, recv, arrays, plan, after)
        return {k: (o, r) for k, o, r in zip(names, own, arrays[len(names) :])}


class _Resident:
    def __init__(self, wfull):
        self.wfull, self.grads = wfull, {}

    def weight(self, k):
        return self.wfull[k]

    def grad(self, k, g):
        self.grads[k] = g

    def dep(self):
        return None

    def reduce(self, vec, name):
        return vec

    def done(self, stage, after):
        pass


def _local_step(x, p, target, small, conv_w, ex):
    a1, r1 = _rms_fwd(x, small["norm_mix"], "norm_mix_fwd", dep=ex.dep())
    proj = _matmul(a1, ex.weight("w_in"), "nn", F32, "in_proj")
    bias = _bias_table(jnp.pad(small["rel_bias"], ((0, 0), (0, N_REL_PAD - N_REL))))
    y_hg, o_hg, states = _hgrn_fwd(proj, small["lb_logits"], small["hg_norm"])
    y_att = _att_fwd(proj, bias, dep=ex.dep())
    ex.done("att_fwd", y_att)
    ycat = lax.dynamic_update_slice(y_hg, y_att, (0, HG_WIDTH))
    h1 = _matmul(ycat, ex.weight("w_out"), "nn", F32, "out_proj", resid=x, dep=ex.dep())
    a2, r2 = _rms_fwd(h1, small["norm_ffn"], "norm_ffn_fwd")
    ex.done("norm_ffn_fwd", a2)
    u = _matmul(a2, ex.weight("w_up"), "nn", BF16, "up_proj")
    conv_b = _interleave_cols(small["conv_b"])
    ex.done("up_proj", u)
    z = _ffn_act_fwd(u, conv_w, conv_b)
    ex.done("ffn_act_fwd", z)
    h2 = _matmul(z, ex.weight("w_down"), "nn", F32, "down_proj", tk=2816, resid=h1)
    a3, r3 = _rms_fwd(h2, small["norm_ple"], "norm_ple_fwd")
    gpre = _matmul(a3, ex.weight("w_ple_gate"), "nn", F32, "ple_gate")
    pp = _matmul(p, ex.weight("w_ple_proj"), "nn", F32, "ple_proj")
    dh3, dgpre, dpp, d_final, loss = _ple_loss(gpre, pp, h2, small["final_norm"], target)

    ex.grad("w_ple_proj", _matmul(p, dpp, "tn", F32, "d_w_ple_proj", tk=2048))
    ex.grad("w_ple_gate", _matmul(a3, dgpre, "tn", F32, "d_w_ple_gate", tk=2048))
    da3 = _matmul(dgpre, ex.weight("w_ple_gate"), "nt", F32, "d_norm_ple_out")
    dh2, d_ple = _rms_bwd(da3, h2, r3, small["norm_ple"], dh3, "norm_ple_bwd")
    dz = _matmul(dh2, ex.weight("w_down"), "nt", BF16, "d_ffn_act")
    ex.grad("w_down", _matmul(z, dh2, "tn", F32, "d_w_down", tk=2048))
    du, dcw, dcb = _ffn_act_bwd(u, dz, conv_w, conv_b, dep=ex.dep())
    ex.done("ffn_act_bwd", du)
    d_conv_w, d_conv_b = _deinterleave_cols(dcw), _deinterleave_cols(dcb)
    ex.grad("w_up", _matmul(a2, du, "tn", F32, "d_w_up", tk=2048, dep=ex.dep()))
    da2 = _matmul(du, ex.weight("w_up"), "nt", F32, "d_norm_ffn_out", tk=2816, dep=ex.dep())
    dh1, d_ffn = _rms_bwd(da2, h1, r2, small["norm_ffn"], dh2, "norm_ffn_bwd")
    dycat = _matmul(dh1, ex.weight("w_out"), "nt", F32, "d_mix_out")
    ex.done("d_mix_out", dycat)
    ex.grad("w_out", _matmul(ycat, dh1, "tn", F32, "d_w_out", tk=2048, dep=ex.dep()))
    dp_hg, d_lb, d_hgn = _hgrn_bwd(proj, small["lb_logits"], small["hg_norm"], o_hg, dycat, states, dep=ex.dep())
    ex.done("hgrn_bwd", d_lb)
    dq_att, dk_att, dv_att, gsum = _att_bwd(proj, bias, dycat, dep=ex.dep())
    d_rel = _rel_bias_grad(gsum)
    d_small = {
        "norm_mix": jnp.zeros_like(small["norm_mix"]), "lb_logits": d_lb, "hg_norm": d_hgn, "rel_bias": d_rel, "norm_ffn": d_ffn,
        "conv_b": d_conv_b, "norm_ple": d_ple, "final_norm": d_final,
    }
    packed = jnp.concatenate([_pack_small(d_small), _rows128(d_conv_w, CONV_W_FULL_ROWS), _rows128(loss[0:1, 0:1], 8)], axis=0)
    early = ex.reduce(packed, "all_reduce_small")
    dproj = dp_hg
    for k, part in enumerate((dq_att, dk_att, dv_att)):
        dproj = lax.dynamic_update_slice(dproj, part, (0, 4 * HG_WIDTH + k * ATT_WIDTH))
    ex.grad("w_in", _matmul(a1, dproj, "tn", F32, "d_w_in", tk=2048, dep=[early]))
    da1 = _matmul(dproj, ex.weight("w_in"), "nt", F32, "d_norm_mix_out", tk=1792, dep=ex.dep())
    dx, d_mix = _rms_bwd(da1, x, r1, small["norm_mix"], dh1, "norm_mix_bwd")
    rows = dict(SMALL)["norm_mix"]
    late = ex.reduce(_rows128(d_mix, rows), "all_reduce_norm_mix")
    ex.done("d_norm_mix_out", late)
    return dx, jnp.concatenate([late, early[rows:]], axis=0)


def kernel(x, p, norm_mix, w_in, lb_logits, hg_norm, rel_bias, w_out, norm_ffn, w_up, conv_w, conv_b, w_down, norm_ple, w_ple_gate, w_ple_proj, final_norm, loss_target, m_norm_mix, m_w_in, m_lb_logits, m_hg_norm, m_rel_bias, m_w_out, m_norm_ffn, m_w_up, m_conv_w, m_conv_b, m_w_down, m_norm_ple, m_w_ple_gate, m_w_ple_proj, m_final_norm, v_norm_mix, v_w_in, v_lb_logits, v_hg_norm, v_rel_bias, v_w_out, v_norm_ffn, v_w_up, v_conv_w, v_conv_b, v_w_down, v_norm_ple, v_w_ple_gate, v_w_ple_proj, v_final_norm):
    given = dict(locals())
    mx, my, mc = _position()
    me = 4 * mx + 2 * my + mc
    big = {k: given[k][0] for k, _ in BIG}
    ex = _Exchange(big, (mx, my, mc))
    conv_w_full = ex.gather(conv_w[0])

    small = {
        "norm_mix": norm_mix, "lb_logits": lb_logits, "hg_norm": hg_norm, "rel_bias": rel_bias[0], "norm_ffn": norm_ffn,
        "conv_b": conv_b, "norm_ple": norm_ple, "final_norm": final_norm.reshape(1, -1),
    }
    dx, reduced = _local_step(x[0], p[0, 0], loss_target[0], small, conv_w_full, ex)

    out = {}
    shapes = {k: given[k].shape for k, _ in SMALL}
    g_small, at = _unpack_small(reduced, shapes)
    g_conv_full = reduced[at : at + CONV_W_FULL_ROWS].reshape(3, 2 * D_FF)
    total_loss = reduced[at + CONV_W_FULL_ROWS, 0]
    cw = conv_w.shape[2]
    g_conv = lax.dynamic_slice_in_dim(g_conv_full, me * cw, cw, axis=1)

    def pack_with_conv(parts, conv_part):
        return jnp.concatenate([_pack_small(parts), _rows128(conv_part, CONV_W_SHARD_ROWS)], axis=0)

    d_pk, m_pk, v_pk = _adam_small(
        pack_with_conv({k: given[k] for k, _ in SMALL}, conv_w),
        pack_with_conv(g_small, g_conv),
        pack_with_conv({k: given["m_" + k] for k, _ in SMALL}, m_conv_w),
        pack_with_conv({k: given["v_" + k] for k, _ in SMALL}, v_conv_w),
    )
    for name, pk in (("d", d_pk), ("m", m_pk), ("v", v_pk)):
        parts, at = _unpack_small(pk, shapes)
        parts["conv_w"] = pk[at : at + CONV_W_SHARD_ROWS].reshape(-1)[: 3 * cw].reshape(conv_w.shape)
        for k, a in parts.items():
            out.setdefault(k, {})
            out[k][name] = a
    for k, _ in SMALL:
        out[k]["g"] = g_small[k]
    out["conv_w"]["g"] = g_conv.reshape(conv_w.shape)

    after, started = v_pk, ex.dep()
    for gi in range(len(GROUPS)):
        for k, (o, r) in ex.finish(gi, after).items():
            g, d, nm, nv = _adam_big(big[k], given["m_" + k][0], given["v_" + k][0], o, r, "adam_" + k, dep=started)
            out[k] = tuple(a[None] for a in (g, d, nm, nv))
            after = nv

    order = ["norm_mix", "w_in", "lb_logits", "hg_norm", "rel_bias", "w_out", "norm_ffn", "w_up", "conv_w", "conv_b", "w_down", "norm_ple", "w_ple_gate", "w_ple_proj", "final_norm"]

    def pick(k, what):
        return out[k][what] if isinstance(out[k], dict) else out[k][{"g": 0, "d": 1, "m": 2, "v": 3}[what]]

    return (total_loss, dx[None], *[pick(k, "g") for k in order], *[pick(k, "d") for k in order], *[pick(k, "m") for k in order], *[pick(k, "v") for k in order])
```

```python
import functools

import jax
import jax.numpy as jnp
from jax import lax
from jax.experimental import pallas as pl
from jax.experimental.pallas import tpu as pltpu

F32 = jnp.float32
BF16 = jnp.bfloat16

D_MODEL = 2048
CHUNK = 64
HG_HEADS = 8
HEAD_DIM = 128
HG_WIDTH = HG_HEADS * HEAD_DIM
ATT_HEADS = 8
ATT_WIDTH = ATT_HEADS * HEAD_DIM
LEFT_CHUNKS = 8
PAD = LEFT_CHUNKS * CHUNK
BAND = PAD + CHUNK
REL_CLIP = 128
N_REL = 2 * REL_CLIP + 1
N_REL_PAD = 384
D_FF = 5632
EPS = 1e-6
ATT_SCALE = HEAD_DIM ** -0.5
SUB = 32
HG_BLOCK = 8
Q_BLOCK = 4 * CHUNK
K_BLOCK = Q_BLOCK + PAD
DIAG = 1024
MASKED = -1e30

ADAM_LR = 0.001
ADAM_B1 = 0.9
ADAM_B2 = 0.999
ADAM_EPS = 1e-08
ADAM_WD = 0.01
ADAM_STEP = 10

N_DEV = 8
VMEM_LIMIT = 48 * 1024 * 1024
MESH = pl.DeviceIdType.MESH
ANY = pl.BlockSpec(memory_space=pl.ANY)
HIGHEST = lax.Precision.HIGHEST

NN = (((1,), (0,)), ((), ()))
NT = (((1,), (1,)), ((), ()))
TN = (((0,), (0,)), ((), ()))


def _params(*sem):
    return pltpu.CompilerParams(dimension_semantics=sem if sem else None, vmem_limit_bytes=VMEM_LIMIT)


def _pallas(body, n_in, dep, **kw):
    deps = [] if dep is None else list(dep)
    if not deps:
        return pl.pallas_call(body, **kw)

    def body_after(*refs):
        body(*refs[:n_in], *refs[n_in + len(deps) :])

    call = pl.pallas_call(body_after, **dict(kw, in_specs=list(kw["in_specs"]) + [ANY] * len(deps)))
    return lambda *ops: call(*ops, *deps)


def _dot(a, b, dims=NN):
    return lax.dot_general(a, b, dims, preferred_element_type=F32)


def _dot3(a, b, dims=NN):
    a_hi, b_hi = a.astype(BF16), b.astype(BF16)
    a_lo, b_lo = (a - a_hi.astype(F32)).astype(BF16), (b - b_hi.astype(F32)).astype(BF16)
    return _dot(a_hi, b_hi, dims) + (_dot(a_hi, b_lo, dims) + _dot(a_lo, b_hi, dims))


def _sigmoid(x):
    return 1.0 / (1.0 + jnp.exp(-x))


def _tile(n, prefs):
    for t in prefs:
        if n % t == 0:
            return t
    return n


def _matmul(a, b, mode, out_dtype, name, tm=1024, tn=1024, tk=None, resid=None, dep=None):
    if mode == "nn":
        (m, k), n = a.shape, b.shape[1]
    elif mode == "nt":
        (m, k), n = a.shape, b.shape[0]
    else:
        (k, m), n = a.shape, b.shape[1]
    tm = _tile(m, (tm, 512, 256, 128))
    tn = _tile(n, (tn, 1408, 512, 256, 128))
    tk = k if tk is None else _tile(k, (tk,))
    nk = k // tk
    dims = {"nn": NN, "nt": NT, "tn": TN}[mode]
    a_spec = pl.BlockSpec((tk, tm), lambda i, j, s: (s, i)) if mode == "tn" else pl.BlockSpec((tm, tk), lambda i, j, s: (i, s))
    b_spec = pl.BlockSpec((tn, tk), lambda i, j, s: (j, s)) if mode == "nt" else pl.BlockSpec((tk, tn), lambda i, j, s: (s, j))
    o_spec = pl.BlockSpec((tm, tn), lambda i, j, s: (i, j))
    has_res = resid is not None

    def body(*refs):
        a_ref, b_ref = refs[0], refs[1]
        o_ref = refs[2 + has_res]
        part = _dot(a_ref[...].astype(BF16), b_ref[...].astype(BF16), dims)

        def finish(acc):
            if has_res:
                acc = acc + refs[2][...]
            o_ref[...] = acc.astype(out_dtype)

        if nk == 1:
            finish(part)
        else:
            acc_ref = refs[-1]
            s = pl.program_id(2)

            @pl.when(s == 0)
            def _():
                acc_ref[...] = part

            @pl.when(s > 0)
            def _():
                acc_ref[...] += part

            @pl.when(s == nk - 1)
            def _():
                finish(acc_ref[...])

    return _pallas(
        body,
        2 + has_res,
        dep,
        name=name,
        grid=(m // tm, n // tn, nk),
        in_specs=[a_spec, b_spec] + ([o_spec] if has_res else []),
        out_specs=o_spec,
        out_shape=jax.ShapeDtypeStruct((m, n), out_dtype),
        scratch_shapes=[pltpu.VMEM((tm, tn), F32)] if nk > 1 else [],
        compiler_params=_params("parallel", "parallel", "arbitrary"),
    )(*([a, b] + ([resid] if has_res else [])))


def _transpose(x, name):
    r, c = x.shape
    tr, tc = _tile(r, (512,)), _tile(c, (512, 256))

    def body(x_ref, o_ref):
        o_ref[...] = x_ref[...].astype(BF16).T

    return pl.pallas_call(
        body,
        name=name,
        grid=(r // tr, c // tc),
        in_specs=[pl.BlockSpec((tr, tc), lambda i, j: (i, j))],
        out_specs=pl.BlockSpec((tc, tr), lambda i, j: (j, i)),
        out_shape=jax.ShapeDtypeStruct((c, r), BF16),
        compiler_params=_params("parallel", "parallel"),
    )(x)


def _rms_fwd(x, g, name, dep=None):
    t, d = x.shape
    tm = _tile(t, (256,))

    def body(x_ref, g_ref, a_ref, r_ref):
        xv = x_ref[...]
        r = lax.rsqrt(jnp.mean(xv * xv, axis=-1, keepdims=True) + EPS)
        a_ref[...] = (xv * r * g_ref[...]).astype(BF16)
        r_ref[...] = r

    row = pl.BlockSpec((tm, d), lambda i: (i, 0))
    return _pallas(
        body,
        2,
        dep,
        name=name,
        grid=(t // tm,),
        in_specs=[row, pl.BlockSpec((1, d), lambda i: (0, 0))],
        out_specs=[row, pl.BlockSpec((tm, 1), lambda i: (i, 0))],
        out_shape=[jax.ShapeDtypeStruct((t, d), BF16), jax.ShapeDtypeStruct((t, 1), F32)],
        compiler_params=_params("parallel"),
    )(x, g)


def _rms_bwd(da, x, r, g, resid, name, dep=None):
    t, d = x.shape
    tm = _tile(t, (256,))

    def body(da_ref, x_ref, r_ref, g_ref, res_ref, dx_ref, dxb_ref, dg_ref):
        i = pl.program_id(0)
        rv = r_ref[...]
        n = x_ref[...] * rv
        dav = da_ref[...]
        dn = dav * g_ref[...]
        dx = rv * (dn - n * jnp.mean(dn * n, axis=-1, keepdims=True)) + res_ref[...]
        dx_ref[...] = dx
        dxb_ref[...] = dx.astype(BF16)
        part = jnp.sum(dav * n, axis=0, keepdims=True)

        @pl.when(i == 0)
        def _():
            dg_ref[...] = part

        @pl.when(i > 0)
        def _():
            dg_ref[...] += part

    row = pl.BlockSpec((tm, d), lambda i: (i, 0))
    vec = pl.BlockSpec((1, d), lambda i: (0, 0))
    return _pallas(
        body,
        5,
        dep,
        name=name,
        grid=(t // tm,),
        in_specs=[row, row, pl.BlockSpec((tm, 1), lambda i: (i, 0)), vec, row],
        out_specs=[row, row, vec],
        out_shape=[jax.ShapeDtypeStruct((t, d), F32), jax.ShapeDtypeStruct((t, d), BF16), jax.ShapeDtypeStruct((1, d), F32)],
        compiler_params=_params("arbitrary"),
    )(da, x, r, g, resid)


def _tri(n, upper):
    r = lax.broadcasted_iota(jnp.int32, (n, n), 0)
    c = lax.broadcasted_iota(jnp.int32, (n, n), 1)
    return jnp.where((c >= r) if upper else (c <= r), 1.0, 0.0).astype(F32)


def _hgrn_gates(q, fp, lbl):
    l0, l1 = lbl[0:1, :], lbl[1:2, :]
    mx = jnp.maximum(l0, l1)
    e0, e1 = jnp.exp(l0 - mx), jnp.exp(l1 - mx)
    lb = e0 / (e0 + e1)
    sig = _sigmoid(fp)
    f = lb + (1.0 - lb) * sig
    kk = (1.0 - lb) * _sigmoid(-fp)
    sq = _sigmoid(q)
    b = jnp.dot(_tri(CHUNK, False), jnp.log(f), precision=HIGHEST, preferred_element_type=F32)
    return lb, sig, f, kk, sq, q * sq, b


def _heads(x):
    return [x[:, j * HEAD_DIM : (j + 1) * HEAD_DIM] for j in range(x.shape[1] // HEAD_DIM)]


def _wide(parts):
    return jnp.concatenate(parts, axis=1)


def _intra_blocks(b):
    out = []
    for lo in range(0, CHUNK, SUB):
        hi = lo + SUB
        br = b[lo + SUB // 2 : lo + SUB // 2 + 1, :]
        row = lax.broadcasted_iota(jnp.int32, (SUB, hi), 0) + lo
        col = lax.broadcasted_iota(jnp.int32, (SUB, hi), 1)
        out.append((lo, hi, jnp.exp(b[lo:hi] - br), jnp.exp(br - b[:hi]), col <= row))
    return out


def _hgrn_fwd(proj, lb_logits, hg_norm):
    t = proj.shape[0]
    nc = t // CHUNK

    def body(q_ref, f_ref, i_ref, g_ref, lbl_ref, hgn_ref, y_ref, o_ref, st_ref, s_scr):
        c = pl.program_id(1)

        @pl.when(c == 0)
        def _():
            s_scr[...] = jnp.zeros_like(s_scr)

        hs = range(HG_BLOCK)
        sts = [s_scr[j] for j in hs]
        _, _, _, kk, _, qf, b = _hgrn_gates(q_ref[...], f_ref[...], lbl_ref[...])
        vb = _heads(i_ref[...].astype(BF16))
        bl = b[CHUNK - 1 : CHUNK, :]
        qe = _heads((qf * jnp.exp(b)).astype(BF16))
        kd = _heads((kk * jnp.exp(bl - b)).astype(BF16))
        decay = _heads(jnp.exp(bl))
        o = [_dot(qe[j], sts[j].astype(BF16), NT) for j in hs]
        parts = [[] for _ in hs]
        for lo, hi, ea, eb, mask in _intra_blocks(b):
            a, bk = _heads((qf[lo:hi] * ea).astype(BF16)), _heads((kk[:hi] * eb).astype(BF16))
            p = [jnp.where(mask, _dot(a[j], bk[j], NT), 0.0).astype(BF16) for j in hs]
            for j in hs:
                parts[j].append(_dot(p[j], vb[j][:hi]))
        o = [o[j] + jnp.concatenate(parts[j], axis=0) for j in hs]
        new = [sts[j] * decay[j] + _dot(vb[j], kd[j], TN) for j in hs]
        hgn = hgn_ref[...]
        on = [o[j] * lax.rsqrt(jnp.mean(o[j] * o[j], axis=-1, keepdims=True) + EPS) * hgn for j in hs]
        gg = g_ref[...]
        for j in hs:
            st_ref[j] = sts[j]
            s_scr[j] = new[j]
        o_ref[...] = _wide(o)
        y_ref[...] = (_wide(on) * (gg * _sigmoid(gg))).astype(BF16)

    wide = HG_BLOCK * HEAD_DIM
    groups = HG_HEADS // HG_BLOCK

    def col(k):
        return pl.BlockSpec((CHUNK, wide), lambda g, c: (c, k * groups + g))

    out = pl.BlockSpec((CHUNK, wide), lambda g, c: (c, g))
    return pl.pallas_call(
        body,
        name="hgrn_fwd",
        grid=(groups, nc),
        in_specs=[col(0), col(1), col(2), col(3), pl.BlockSpec((2, wide), lambda g, c: (0, g)), pl.BlockSpec((1, HEAD_DIM), lambda g, c: (0, 0))],
        out_specs=[out, out, pl.BlockSpec((HG_BLOCK, None, HEAD_DIM, HEAD_DIM), lambda g, c: (g, c, 0, 0))],
        out_shape=[
            jax.ShapeDtypeStruct((t, HG_WIDTH + ATT_WIDTH), BF16),
            jax.ShapeDtypeStruct((t, HG_WIDTH), F32),
            jax.ShapeDtypeStruct((HG_HEADS, nc, HEAD_DIM, HEAD_DIM), F32),
        ],
        scratch_shapes=[pltpu.VMEM((HG_BLOCK, HEAD_DIM, HEAD_DIM), F32)],
        compiler_params=_params("arbitrary", "arbitrary"),
    )(proj, proj, proj, proj, lb_logits, hg_norm)


def _hgrn_bwd(proj, lb_logits, hg_norm, o_hg, dycat, states, dep=None):
    t = proj.shape[0]
    nc = t // CHUNK

    def body(q_ref, f_ref, i_ref, g_ref, lbl_ref, hgn_ref, o_ref, dy_ref, st_ref, dp_ref, dlbl_ref, dhgn_ref, dst_scr, dlb_scr):
        h = pl.program_id(0)
        c = pl.program_id(1)

        @pl.when(c == 0)
        def _():
            dst_scr[...] = jnp.zeros_like(dst_scr)
            dlb_scr[...] = jnp.zeros_like(dlb_scr)

        @pl.when((c == 0) & (h == 0))
        def _():
            dhgn_ref[...] = jnp.zeros_like(dhgn_ref)

        hs = range(HG_BLOCK)
        hgn = _wide([hgn_ref[...]] * HG_BLOCK)
        q, fp, gg, vi = q_ref[...], f_ref[...], g_ref[...], i_ref[...]
        lb, sig, f, kk, sq, qf, b = _hgrn_gates(q, fp, lbl_ref[...])
        o, dy = o_ref[...], dy_ref[...]
        sg = _sigmoid(gg)
        n = _wide([oh * lax.rsqrt(jnp.mean(oh * oh, axis=-1, keepdims=True) + EPS) for oh in _heads(o)])
        don = dy * (gg * sg)
        dgg = dy * (n * hgn) * (sg * (1.0 + gg * (1.0 - sg)))
        d_hgn = sum(_heads(jnp.sum(don * n, axis=0, keepdims=True)))
        dn = don * hgn
        do = _wide(
            [
                lax.rsqrt(jnp.mean(oh * oh, axis=-1, keepdims=True) + EPS) * (dnh - nh * jnp.mean(dnh * nh, axis=-1, keepdims=True))
                for oh, dnh, nh in zip(_heads(o), _heads(dn), _heads(n))
            ]
        )
        sts = [st_ref[j] for j in hs]
        dstn = [dst_scr[j] for j in hs]
        bl = b[CHUNK - 1 : CHUNK, :]
        e_b, e_bl, e_l = jnp.exp(b), jnp.exp(bl - b), jnp.exp(bl)
        doh, vih = _heads(do), _heads(vi)
        dobh = _heads(do.astype(BF16))
        dq_acc = _wide([_dot3(doh[j], sts[j]) for j in hs]) * e_b
        dk_inter = _wide([_dot3(vih[j], dstn[j]) for j in hs]) * e_bl
        dk_acc = dk_inter
        kd = _heads((kk * e_bl).astype(BF16))
        dv_acc = _wide([_dot(kd[j], dstn[j].astype(BF16), NT) for j in hs])
        qe, decay = _heads((qf * e_b).astype(BF16)), _heads(e_l)
        dst_new = [dstn[j] * decay[j] + _dot(dobh[j], qe[j], TN) for j in hs]
        db_last = e_l * _wide([jnp.sum(sts[j] * dstn[j], axis=0, keepdims=True) for j in hs]) + jnp.sum(kk * dk_inter, axis=0, keepdims=True)
        dq_parts = []
        for lo, hi, ea, eb, mask in _intra_blocks(b):
            a, bk = qf[lo:hi] * ea, kk[:hi] * eb
            ah, bkh = _heads(a), _heads(bk)
            abh, bkbh = _heads(a.astype(BF16)), _heads(bk.astype(BF16))
            p = [jnp.where(mask, _dot(abh[j], bkbh[j], NT), 0.0).astype(BF16) for j in hs]
            dp = [jnp.where(mask, _dot3(doh[j][lo:hi], vih[j][:hi], NT), 0.0) for j in hs]
            dq_parts.append(_wide([_dot3(dp[j], bkh[j]) for j in hs]) * ea)
            dki = _wide([_dot3(dp[j], ah[j], TN) for j in hs]) * eb
            dvi = _wide([_dot(p[j], dobh[j][lo:hi], TN) for j in hs])
            if hi < CHUNK:
                zeros = jnp.zeros((CHUNK - hi, HG_BLOCK * HEAD_DIM), F32)
                dki = jnp.concatenate([dki, zeros], axis=0)
                dvi = jnp.concatenate([dvi, zeros], axis=0)
            dk_acc = dk_acc + dki
            dv_acc = dv_acc + dvi
        dq_acc = dq_acc + jnp.concatenate(dq_parts, axis=0)
        rows = lax.broadcasted_iota(jnp.int32, dq_acc.shape, 0)
        db = qf * dq_acc - kk * dk_acc + jnp.where(rows == CHUNK - 1, db_last, 0.0)
        dlf = jnp.dot(_tri(CHUNK, True), db, precision=HIGHEST, preferred_element_type=F32)
        dfk = dlf / f - dk_acc
        for k, part in enumerate((dq_acc * (sq * (1.0 + q * (1.0 - sq))), (1.0 - lb) * dfk * sig * (1.0 - sig), dv_acc, dgg)):
            dp_ref[:, k * HG_WIDTH : (k + 1) * HG_WIDTH] = part.astype(BF16)
        dlb_scr[...] += jnp.sum(dfk * (1.0 - sig), axis=0, keepdims=True)
        dhgn_ref[...] += d_hgn
        for j in hs:
            dst_scr[j] = dst_new[j]

        @pl.when(c == nc - 1)
        def _():
            dl0 = dlb_scr[...] * lb * (1.0 - lb)
            dlbl_ref[0:1, :] = dl0
            dlbl_ref[1:2, :] = -dl0

    wide = HG_BLOCK * HEAD_DIM
    groups = HG_HEADS // HG_BLOCK

    def col(k):
        return pl.BlockSpec((CHUNK, wide), lambda g, c: (nc - 1 - c, k * groups + g))

    blk = pl.BlockSpec((CHUNK, wide), lambda g, c: (nc - 1 - c, g))
    assert groups == 1, "d(q, f, i, g) are written as one contiguous column range of the in_proj gradient"
    return _pallas(
        body,
        9,
        dep,
        name="hgrn_bwd",
        grid=(groups, nc),
        in_specs=[
            col(0), col(1), col(2), col(3),
            pl.BlockSpec((2, wide), lambda g, c: (0, g)),
            pl.BlockSpec((1, HEAD_DIM), lambda g, c: (0, 0)),
            blk, blk,
            pl.BlockSpec((HG_BLOCK, None, HEAD_DIM, HEAD_DIM), lambda g, c: (g, nc - 1 - c, 0, 0)),
        ],
        out_specs=[
            pl.BlockSpec((CHUNK, 4 * HG_WIDTH), lambda g, c: (nc - 1 - c, 0)),
            pl.BlockSpec((2, wide), lambda g, c: (0, g)),
            pl.BlockSpec((1, HEAD_DIM), lambda g, c: (0, 0)),
        ],
        out_shape=[
            jax.ShapeDtypeStruct((t, 4 * HG_WIDTH + 3 * ATT_WIDTH), BF16),
            jax.ShapeDtypeStruct((2, HG_WIDTH), F32),
            jax.ShapeDtypeStruct((1, HEAD_DIM), F32),
        ],
        scratch_shapes=[pltpu.VMEM((HG_BLOCK, HEAD_DIM, HEAD_DIM), F32), pltpu.VMEM((1, wide), F32)],
        compiler_params=_params("arbitrary", "arbitrary"),
    )(proj, proj, proj, proj, lb_logits, hg_norm, o_hg, dycat, states)


def _diagonal_slots(shift):
    i = lax.broadcasted_iota(jnp.int32, (N_REL_PAD, DIAG), 0)
    u = lax.broadcasted_iota(jnp.int32, (N_REL_PAD, DIAG), 1)
    offset = u - shift if shift else jnp.where(u < K_BLOCK, u, u - DIAG)
    return jnp.where(jnp.clip(PAD - offset, -REL_CLIP, REL_CLIP) + REL_CLIP == i, 1.0, 0.0).astype(BF16)


def _split3(x):
    hi = x.astype(BF16)
    mid = (x - hi.astype(F32)).astype(BF16)
    return hi, mid, (x - hi.astype(F32) - mid.astype(F32)).astype(BF16)


def _bias_table(rel_bias):
    def body(rb_ref, o_ref, diag):
        h = pl.program_id(0)

        @pl.when(h == 0)
        def _():
            hi, mid, lo = _split3(rb_ref[...])
            slots = _diagonal_slots(0)
            diag[...] = _dot(hi, slots) + (_dot(mid, slots) + _dot(lo, slots))

        rows = jnp.broadcast_to(diag[pl.ds(h, 1), :], (Q_BLOCK, DIAG))
        row = lax.broadcasted_iota(jnp.int32, (Q_BLOCK, K_BLOCK), 0)
        col = lax.broadcasted_iota(jnp.int32, (Q_BLOCK, K_BLOCK), 1)
        first = row - (row & (CHUNK - 1))
        seen = (col >= first) & (col < first + BAND)
        o_ref[...] = jnp.where(seen, pltpu.roll(rows, 0, 1, stride=1, stride_axis=0)[:, :K_BLOCK], MASKED)

    return pl.pallas_call(
        body,
        name="bias_table",
        grid=(ATT_HEADS,),
        in_specs=[pl.BlockSpec((ATT_HEADS, N_REL_PAD), lambda h: (0, 0))],
        out_specs=pl.BlockSpec((None, Q_BLOCK, K_BLOCK), lambda h: (h, 0, 0)),
        out_shape=jax.ShapeDtypeStruct((ATT_HEADS, Q_BLOCK, K_BLOCK), F32),
        scratch_shapes=[pltpu.VMEM((ATT_HEADS, DIAG), F32)],
        compiler_params=_params("arbitrary"),
    )(rel_bias)


def _att_probs(q_ref, kpad, bias_ref, blk):
    qs = (q_ref[...] * ATT_SCALE).astype(BF16)
    start = pl.multiple_of(blk * Q_BLOCK, Q_BLOCK)
    kb = kpad[pl.ds(start, K_BLOCK), :]
    s = _dot(qs, kb, NT) + bias_ref[...]
    col = lax.broadcasted_iota(jnp.int32, (Q_BLOCK, K_BLOCK), 1)
    s = jnp.where(col >= PAD - blk * Q_BLOCK, s, MASKED)
    e = jnp.exp(s - jnp.max(s, axis=-1, keepdims=True))
    return qs, kb, start, e * (1.0 / jnp.sum(e, axis=-1, keepdims=True))


def _fill_padded(dst, src):
    dst[0:PAD, :] = jnp.zeros((PAD, HEAD_DIM), BF16)
    dst[PAD:, :] = src[...].astype(BF16)


def _att_fwd(proj, bias, dep=None):
    t = proj.shape[0]
    nb = t // Q_BLOCK

    def body(q_ref, k_ref, v_ref, bias_ref, y_ref, kpad, vpad):
        c = pl.program_id(1)

        @pl.when(c == 0)
        def _():
            _fill_padded(kpad, k_ref)
            _fill_padded(vpad, v_ref)

        _, _, start, p = _att_probs(q_ref, kpad, bias_ref, c)
        y_ref[...] = _dot(p.astype(BF16), vpad[pl.ds(start, K_BLOCK), :]).astype(BF16)

    base = 4 * HG_HEADS
    return _pallas(
        body,
        4,
        dep,
        name="att_fwd",
        grid=(ATT_HEADS, nb),
        in_specs=[
            pl.BlockSpec((Q_BLOCK, HEAD_DIM), lambda h, c: (c, base + h)),
            pl.BlockSpec((t, HEAD_DIM), lambda h, c: (0, base + ATT_HEADS + h)),
            pl.BlockSpec((t, HEAD_DIM), lambda h, c: (0, base + 2 * ATT_HEADS + h)),
            pl.BlockSpec((None, Q_BLOCK, K_BLOCK), lambda h, c: (h, 0, 0)),
        ],
        out_specs=pl.BlockSpec((Q_BLOCK, HEAD_DIM), lambda h, c: (c, h)),
        out_shape=jax.ShapeDtypeStruct((t, ATT_WIDTH), BF16),
        scratch_shapes=[pltpu.VMEM((t + PAD, HEAD_DIM), BF16), pltpu.VMEM((t + PAD, HEAD_DIM), BF16)],
        compiler_params=_params("arbitrary", "arbitrary"),
    )(proj, proj, proj, bias)


def _att_bwd(proj, bias, dycat, dep=None):
    t = proj.shape[0]
    nb = t // Q_BLOCK

    def body(q_ref, k_ref, v_ref, bias_ref, dy_ref, dq_ref, dk_ref, dv_ref, g_ref, kpad, vpad, dkacc, dvacc):
        c = pl.program_id(1)

        @pl.when(c == 0)
        def _():
            _fill_padded(kpad, k_ref)
            _fill_padded(vpad, v_ref)
            dkacc[...] = jnp.zeros_like(dkacc)
            dvacc[...] = jnp.zeros_like(dvacc)
            g_ref[...] = jnp.zeros_like(g_ref)

        qs, kb, start, p = _att_probs(q_ref, kpad, bias_ref, c)
        band = pl.ds(start, K_BLOCK)
        dyb = dy_ref[...].astype(BF16)
        dvacc[band, :] += _dot(p.astype(BF16), dyb, TN)
        dp = _dot(dyb, vpad[band, :], NT)
        ds = p * (dp - jnp.sum(dp * p, axis=-1, keepdims=True))
        g_ref[...] += ds
        dsb = ds.astype(BF16)
        dq_ref[...] = (_dot(dsb, kb) * ATT_SCALE).astype(BF16)
        dkacc[band, :] += _dot(dsb, qs, TN)

        @pl.when(c == nb - 1)
        def _():
            dk_ref[...] = dkacc[PAD:, :].astype(BF16)
            dv_ref[...] = dvacc[PAD:, :].astype(BF16)

    base = 4 * HG_HEADS
    whole = pl.BlockSpec((t, HEAD_DIM), lambda h, c: (0, h))
    return _pallas(
        body,
        5,
        dep,
        name="att_bwd",
        grid=(ATT_HEADS, nb),
        in_specs=[
            pl.BlockSpec((Q_BLOCK, HEAD_DIM), lambda h, c: (c, base + h)),
            pl.BlockSpec((t, HEAD_DIM), lambda h, c: (0, base + ATT_HEADS + h)),
            pl.BlockSpec((t, HEAD_DIM), lambda h, c: (0, base + 2 * ATT_HEADS + h)),
            pl.BlockSpec((None, Q_BLOCK, K_BLOCK), lambda h, c: (h, 0, 0)),
            pl.BlockSpec((Q_BLOCK, HEAD_DIM), lambda h, c: (c, HG_HEADS + h)),
        ],
        out_specs=[pl.BlockSpec((Q_BLOCK, HEAD_DIM), lambda h, c: (c, h)), whole, whole, pl.BlockSpec((None, Q_BLOCK, K_BLOCK), lambda h, c: (h, 0, 0))],
        out_shape=[
            jax.ShapeDtypeStruct((t, ATT_WIDTH), BF16),
            jax.ShapeDtypeStruct((t, ATT_WIDTH), BF16),
            jax.ShapeDtypeStruct((t, ATT_WIDTH), BF16),
            jax.ShapeDtypeStruct((ATT_HEADS, Q_BLOCK, K_BLOCK), F32),
        ],
        scratch_shapes=[
            pltpu.VMEM((t + PAD, HEAD_DIM), BF16),
            pltpu.VMEM((t + PAD, HEAD_DIM), BF16),
            pltpu.VMEM((t + PAD, HEAD_DIM), F32),
            pltpu.VMEM((t + PAD, HEAD_DIM), F32),
        ],
        compiler_params=_params("arbitrary", "arbitrary"),
    )(proj, proj, proj, bias, dycat)


def _rel_bias_grad(gsum):
    def body(g_ref, o_ref):
        r = lax.broadcasted_iota(jnp.int32, (Q_BLOCK, Q_BLOCK), 0)
        c = lax.broadcasted_iota(jnp.int32, (Q_BLOCK, Q_BLOCK), 1)
        flip = jnp.where(r + c == Q_BLOCK - 1, 1.0, 0.0).astype(BF16)
        sums = []
        for h in range(ATT_HEADS):
            hi, mid, lo = _split3(g_ref[h])
            rev = _dot(flip, hi) + (_dot(flip, mid) + _dot(flip, lo))
            wide = jnp.concatenate([rev, jnp.zeros((Q_BLOCK, DIAG - K_BLOCK), F32)], axis=1)
            sums.append(jnp.sum(pltpu.roll(wide, 0, 1, stride=1, stride_axis=0), axis=0, keepdims=True))
        hi, mid, lo = _split3(jnp.concatenate(sums, axis=0))
        slots = _diagonal_slots(Q_BLOCK - 1)
        o_ref[...] = _dot(hi, slots, NT) + (_dot(mid, slots, NT) + _dot(lo, slots, NT))

    return pl.pallas_call(
        body,
        name="rel_bias_grad",
        out_shape=jax.ShapeDtypeStruct((ATT_HEADS, N_REL_PAD), F32),
        compiler_params=_params(),
    )(gsum)


HALO = 16


FF_TILE = 1408
FF_TILES = D_FF // FF_TILE


def _interleave_cols(a):
    lead = a.shape[:-1]
    return jnp.swapaxes(a.reshape(*lead, 2, FF_TILES, FF_TILE), -3, -2).reshape(*lead, 2 * D_FF)


def _deinterleave_cols(a):
    lead = a.shape[:-1]
    return jnp.swapaxes(a.reshape(*lead, FF_TILES, 2, FF_TILE), -3, -2).reshape(*lead, 2 * D_FF)


def _ffn_specs(t, tm):
    wide = 2 * FF_TILE
    tile = pl.BlockSpec((tm, wide), lambda j, i: (i, j))
    before = pl.BlockSpec((HALO, wide), lambda j, i: (jnp.maximum(i * (tm // HALO) - 1, 0), j))
    after = pl.BlockSpec((HALO, wide), lambda j, i: (jnp.minimum((i + 1) * (tm // HALO), t // HALO - 1), j))
    vec = lambda rows: pl.BlockSpec((rows, wide), lambda j, i: (0, j))
    return tile, before, after, vec


def _shifted(x, rows, offsets):
    r = lax.broadcasted_iota(jnp.int32, (rows, x.shape[0]), 0)
    c = lax.broadcasted_iota(jnp.int32, (rows, x.shape[0]), 1)
    pick = jnp.concatenate([jnp.where(c == r + o, 1.0, 0.0).astype(BF16) for o in offsets], axis=0)
    out = _dot(pick, x)
    return [out[k * rows : (k + 1) * rows] for k in range(len(offsets))]


def _conv(x, w, b, rows):
    taps = _shifted(x, rows, [HALO - 2, HALO - 1]) + [x[HALO : HALO + rows].astype(F32)]
    return b + w[0:1] * taps[0] + w[1:2] * taps[1] + w[2:3] * taps[2], taps


def _ffn_act_fwd(u, conv_w, conv_b):
    t = u.shape[0]
    tm = _tile(t, (128,))
    tile, before, _, vec = _ffn_specs(t, tm)

    def body(u_ref, h_ref, w_ref, b_ref, z_ref):
        first = pl.program_id(1) == 0
        halo = h_ref[...]
        x = jnp.concatenate([jnp.where(first, jnp.zeros_like(halo), halo), u_ref[...]], axis=0)
        c, _ = _conv(x, w_ref[...], b_ref[...], tm)
        gate, val = c[:, :FF_TILE], c[:, FF_TILE:]
        z_ref[...] = (gate * _sigmoid(gate) * val).astype(BF16)

    return pl.pallas_call(
        body,
        name="ffn_act_fwd",
        grid=(FF_TILES, t // tm),
        in_specs=[tile, before, vec(3), vec(1)],
        out_specs=pl.BlockSpec((tm, FF_TILE), lambda j, i: (i, j)),
        out_shape=jax.ShapeDtypeStruct((t, D_FF), BF16),
        compiler_params=_params("parallel", "parallel"),
    )(u, u, conv_w, conv_b)


def _ffn_act_bwd(u, dz, conv_w, conv_b, dep=None):
    t = u.shape[0]
    tm = _tile(t, (128,))
    nt = t // tm
    ext = tm + HALO
    tile, before, after, vec = _ffn_specs(t, tm)

    def body(u_ref, ub_ref, ua_ref, w_ref, b_ref, dz_ref, dza_ref, du_ref, dw_ref, db_ref):
        i = pl.program_id(1)
        first, last = i == 0, i == nt - 1
        ub, ua = ub_ref[...], ua_ref[...]
        parts = [jnp.where(first, jnp.zeros_like(ub), ub), u_ref[...], jnp.where(last, jnp.zeros_like(ua), ua)]
        w = w_ref[...]
        c, taps = _conv(jnp.concatenate(parts, axis=0), w, b_ref[...], ext)
        gate, val = c[:, :FF_TILE], c[:, FF_TILE:]
        dz = jnp.concatenate([dz_ref[...].astype(F32), jnp.where(last, 0.0, dza_ref[...].astype(F32))], axis=0)
        sg = _sigmoid(gate)
        d = jnp.concatenate([dz * val * (sg * (1.0 + gate * (1.0 - sg))), dz * (gate * sg)], axis=1)
        d1, d2 = _shifted(d.astype(BF16), tm, [1, 2])
        du_ref[...] = (w[2:3] * d[:tm] + w[1:2] * d1 + w[0:1] * d2).astype(BF16)

        @pl.when(first)
        def _():
            dw_ref[...] = jnp.zeros_like(dw_ref)
            db_ref[...] = jnp.zeros_like(db_ref)

        for k, tap in enumerate(taps):
            dw_ref[k : k + 1, :] += jnp.sum(d[:tm] * tap[:tm], axis=0, keepdims=True)
        db_ref[...] += jnp.sum(d[:tm], axis=0, keepdims=True)

    narrow = lambda rows, index: pl.BlockSpec((rows, FF_TILE), index)
    return _pallas(
        body,
        7,
        dep,
        name="ffn_act_bwd",
        grid=(FF_TILES, nt),
        in_specs=[
            tile, before, after, vec(3), vec(1),
            narrow(tm, lambda j, i: (i, j)),
            narrow(HALO, lambda j, i: (jnp.minimum((i + 1) * (tm // HALO), t // HALO - 1), j)),
        ],
        out_specs=[tile, vec(3), vec(1)],
        out_shape=[
            jax.ShapeDtypeStruct((t, 2 * D_FF), BF16),
            jax.ShapeDtypeStruct((3, 2 * D_FF), F32),
            jax.ShapeDtypeStruct((1, 2 * D_FF), F32),
        ],
        compiler_params=_params("parallel", "arbitrary"),
    )(u, u, u, conv_w, conv_b, dz, dz)


def _ple_loss(gpre, pp, h2, final_norm, target):
    t, d = h2.shape
    tm = _tile(t, (256,))

    def body(gp_ref, pp_ref, h_ref, g_ref, tg_ref, dh_ref, dgp_ref, dpp_ref, dg_ref, loss_ref):
        i = pl.program_id(0)
        gate = _sigmoid(gp_ref[...])
        ppv = pp_ref[...]
        h3 = h_ref[...] + gate * ppv
        r = lax.rsqrt(jnp.mean(h3 * h3, axis=-1, keepdims=True) + EPS)
        n = h3 * r
        g = g_ref[...]
        err = n * g - tg_ref[...]
        loss = 0.5 * jnp.sum(jnp.mean(err * err, axis=-1, keepdims=True))
        dy = err * (1.0 / d)
        dn = dy * g
        dh = r * (dn - n * jnp.mean(dn * n, axis=-1, keepdims=True))
        dh_ref[...] = dh
        dgp_ref[...] = (dh * ppv * gate * (1.0 - gate)).astype(BF16)
        dpp_ref[...] = (dh * gate).astype(BF16)
        dg = jnp.sum(dy * n, axis=0, keepdims=True)

        @pl.when(i == 0)
        def _():
            dg_ref[...] = dg
            loss_ref[...] = jnp.full(loss_ref.shape, loss, F32)

        @pl.when(i > 0)
        def _():
            dg_ref[...] += dg
            loss_ref[...] += loss

    row = pl.BlockSpec((tm, d), lambda i: (i, 0))
    vec = pl.BlockSpec((1, d), lambda i: (0, 0))
    return pl.pallas_call(
        body,
        name="ple_loss",
        grid=(t // tm,),
        in_specs=[row, row, row, vec, row],
        out_specs=[row, row, row, vec, pl.BlockSpec((8, 128), lambda i: (0, 0))],
        out_shape=[
            jax.ShapeDtypeStruct((t, d), F32),
            jax.ShapeDtypeStruct((t, d), BF16),
            jax.ShapeDtypeStruct((t, d), BF16),
            jax.ShapeDtypeStruct((1, d), F32),
            jax.ShapeDtypeStruct((8, 128), F32),
        ],
        compiler_params=_params("arbitrary"),
    )(gpre, pp, h2, final_norm, target)


def _adamw(w, g, m, v):
    m = ADAM_B1 * m + (1.0 - ADAM_B1) * g
    v = ADAM_B2 * v + (1.0 - ADAM_B2) * (g * g)
    m_hat = m / (1.0 - ADAM_B1 ** ADAM_STEP)
    v_hat = v / (1.0 - ADAM_B2 ** ADAM_STEP)
    return -ADAM_LR * (m_hat / (jnp.sqrt(v_hat) + ADAM_EPS) + ADAM_WD * w), m, v


def _adam_big(w, m, v, own, recv, name, dep=None):
    r, c = w.shape
    tr = _tile(r, (256, 176))

    def body(w_ref, m_ref, v_ref, own_ref, recv_ref, g_ref, d_ref, nm_ref, nv_ref):
        g = own_ref[...]
        for k in range(3):
            g = g + recv_ref[k].astype(F32)
        g_ref[...] = g
        d_ref[...], nm_ref[...], nv_ref[...] = _adamw(w_ref[...], g, m_ref[...], v_ref[...])

    blk = pl.BlockSpec((tr, c), lambda i: (i, 0))
    return _pallas(
        body,
        5,
        dep,
        name=name,
        grid=(r // tr,),
        in_specs=[blk, blk, blk, blk, pl.BlockSpec((3, tr, c), lambda i: (0, i, 0))],
        out_specs=[blk] * 4,
        out_shape=[jax.ShapeDtypeStruct((r, c), F32)] * 4,
        compiler_params=_params("parallel"),
    )(w, m, v, own, recv)


def _adam_small(w, g, m, v):
    def body(w_ref, g_ref, m_ref, v_ref, d_ref, nm_ref, nv_ref):
        d_ref[...], nm_ref[...], nv_ref[...] = _adamw(w_ref[...], g_ref[...], m_ref[...], v_ref[...])

    return pl.pallas_call(body, name="adam_small", out_shape=[jax.ShapeDtypeStruct(w.shape, F32)] * 3, compiler_params=_params())(w, g, m, v)


def _cast_bf16(w, name):
    r, c = w.shape
    tr = _tile(r, (256, 176))

    def body(w_ref, o_ref):
        o_ref[...] = w_ref[...].astype(BF16)

    blk = pl.BlockSpec((tr, c), lambda i: (i, 0))
    return pl.pallas_call(
        body, name=name, grid=(r // tr,), in_specs=[blk], out_specs=blk, out_shape=jax.ShapeDtypeStruct((r, c), BF16), compiler_params=_params("parallel")
    )(w)


def _position():
    return lax.axis_index("x"), lax.axis_index("y"), lax.axis_index("c")


def _other_chips(x, y):
    return [(1 - x, y), (x, 1 - y), (1 - x, 1 - y)]


def _block_index(dev, interleaved):
    x, y, c = dev
    return 4 * y + 2 * c + x if interleaved else 4 * x + 2 * y + c


def _shard_of(ref, axis, size, dev, interleaved=False):
    start = pl.multiple_of(_block_index(dev, interleaved) * size, 128 if axis == 1 else 16)
    return ref.at[:, pl.ds(start, size)] if axis == 1 else ref.at[pl.ds(start, size), :]


def _all_gather(shards, axes, interleaved):
    n = len(shards)

    def body(*refs):
        ins, outs = refs[:n], refs[n : 2 * n]
        send_sems, recv_sems, local_sems = refs[2 * n :]
        x, y, c = _position()
        me, sibling = (x, y, c), (x, y, 1 - c)
        chips = _other_chips(x, y)
        firsts, passed, locals_ = [], [], []
        for w in range(n):
            size = shards[w].shape[axes[w]]
            slot = functools.partial(_shard_of, outs[w], axes[w], size, interleaved=interleaved[w])

            def copy(k, block, to, src=None, w=w, slot=slot):
                return pltpu.make_async_remote_copy(
                    src_ref=slot(block) if src is None else src,
                    dst_ref=slot(block),
                    send_sem=send_sems.at[7 * w + k],
                    recv_sem=recv_sems.at[7 * w + k],
                    device_id=to,
                    device_id_type=MESH,
                )

            mine = pltpu.make_async_copy(ins[w], slot(me), local_sems.at[w])
            mine.start()
            locals_.append(mine)
            first = [copy(0, me, sibling, src=ins[w])] + [copy(1 + j, me, (*chip, c), src=ins[w]) for j, chip in enumerate(chips)]
            for cp in first:
                cp.start()
            firsts.append((first, copy))
        for w in range(n):
            first, copy = firsts[w]
            fwd = [copy(4 + j, (*chip, c), sibling) for j, chip in enumerate(chips)]
            for j, chip in enumerate(chips):
                copy(1 + j, (*chip, c), me).wait_recv()
                fwd[j].start()
            passed.append(fwd)
        for w in range(n):
            first, copy = firsts[w]
            copy(0, sibling, me).wait_recv()
            for j, chip in enumerate(chips):
                copy(4 + j, (*chip, 1 - c), me).wait_recv()
            for cp in first + passed[w]:
                cp.wait_send()
            locals_[w].wait()

    def full(s, ax):
        shape = list(s.shape)
        shape[ax] *= N_DEV
        return jax.ShapeDtypeStruct(tuple(shape), s.dtype)

    return pl.pallas_call(
        body,
        name="all_gather_weights",
        in_specs=[ANY] * n,
        out_specs=[ANY] * n,
        out_shape=[full(s, ax) for s, ax in zip(shards, axes)],
        scratch_shapes=[pltpu.SemaphoreType.DMA((7 * n,)), pltpu.SemaphoreType.DMA((7 * n,)), pltpu.SemaphoreType.DMA((n,))],
    )(*shards)


def _add_blocks(ids, grad, landed, axis, size, targets, out_dtype, name):
    rows = size if axis == 0 else grad.shape[0]
    cols = size if axis == 1 else grad.shape[1]
    tr = _tile(rows, (256, 176))
    nr = rows // tr
    nt = len(targets)

    def body(ids_ref, g_ref, l_ref, o_ref):
        o_ref[...] = (g_ref[...] + l_ref[...]).astype(out_dtype)

    if axis == 1:
        g_spec = pl.BlockSpec((tr, cols), lambda k, i, ids: (i, ids[targets[0] + k]))
    else:
        g_spec = pl.BlockSpec((tr, cols), lambda k, i, ids: (ids[targets[0] + k] * nr + i, 0))
    return pl.pallas_call(
        body,
        name=name,
        grid_spec=pltpu.PrefetchScalarGridSpec(
            num_scalar_prefetch=1,
            grid=(nt, nr),
            in_specs=[g_spec, pl.BlockSpec((None, tr, cols), lambda k, i, ids: (ids[4 + targets[0] + k], i, 0))],
            out_specs=pl.BlockSpec((None, tr, cols), lambda k, i, ids: (k, i, 0)),
        ),
        out_shape=jax.ShapeDtypeStruct((nt, rows, cols), out_dtype),
        compiler_params=_params("parallel", "parallel"),
    )(ids, grad, landed)


def _all_reduce_small(vec, name):
    rows = vec.shape[0]

    def body(v_ref, o_ref, land, send_sems, recv_sems):
        x, y, c = _position()
        mine = 4 * x + 2 * y + c
        copies = []
        for mask in range(1, N_DEV):
            peer = (1 - x if mask & 4 else x, 1 - y if mask & 2 else y, 1 - c if mask & 1 else c)
            copies.append(
                pltpu.make_async_remote_copy(
                    src_ref=v_ref, dst_ref=land.at[mine], send_sem=send_sems.at[mask - 1], recv_sem=recv_sems.at[mask - 1], device_id=peer, device_id_type=MESH
                )
            )
        for cp in copies:
            cp.start()
        land[mine] = v_ref[...]
        for cp in copies:
            cp.wait()
        acc = land[0]
        for k in range(1, N_DEV):
            acc = acc + land[k]
        o_ref[...] = acc

    return pl.pallas_call(
        body,
        name=name,
        out_shape=jax.ShapeDtypeStruct(vec.shape, F32),
        in_specs=[pl.BlockSpec(memory_space=pltpu.VMEM)],
        out_specs=pl.BlockSpec(memory_space=pltpu.VMEM),
        scratch_shapes=[pltpu.VMEM((N_DEV, rows, 128), F32), pltpu.SemaphoreType.DMA((N_DEV - 1,)), pltpu.SemaphoreType.DMA((N_DEV - 1,))],
    )(vec)


def _rows128(a, rows):
    flat = a.reshape(-1)
    return jnp.pad(flat, (0, rows * 128 - flat.shape[0])).reshape(rows, 128)


def _pad_rel(a):
    return jnp.pad(a.reshape(ATT_HEADS, -1)[:, :N_REL], ((0, 0), (0, N_REL_PAD - N_REL)))


SMALL = [("norm_mix", 16), ("lb_logits", 16), ("hg_norm", 8), ("rel_bias", 24), ("norm_ffn", 16), ("conv_b", 88), ("norm_ple", 16), ("final_norm", 16)]
CONV_W_FULL_ROWS = 3 * 2 * D_FF // 128
CONV_W_SHARD_ROWS = 40


def _pack_small(parts):
    return jnp.concatenate([_rows128(_pad_rel(parts[k]) if k == "rel_bias" else parts[k], rows) for k, rows in SMALL], axis=0)


def _unpack_small(packed, shapes):
    out, at = {}, 0
    for k, rows in SMALL:
        blk = packed[at : at + rows]
        at += rows
        if k == "rel_bias":
            out[k] = blk.reshape(ATT_HEADS, N_REL_PAD)[:, :N_REL].reshape(shapes[k])
        else:
            n = 1
            for s in shapes[k]:
                n *= s
            out[k] = blk.reshape(-1)[:n].reshape(shapes[k])
    return out, at


BIG = [("w_in", 1), ("w_out", 0), ("w_up", 1), ("w_down", 0), ("w_ple_gate", 0), ("w_ple_proj", 1)]


HBM = pl.BlockSpec(memory_space=pltpu.HBM)
SEM = pl.BlockSpec(memory_space=pltpu.SEMAPHORE)
EFFECT = pltpu.SideEffectType.DATAFLOW_SIDE_EFFECTING


def _copies(plan, refs, send_sems, recv_sems):
    return [
        pltpu.make_async_remote_copy(src_ref=src, dst_ref=dst, send_sem=send_sems.at[i], recv_sem=recv_sems.at[i], device_id=dev, device_id_type=MESH)
        for i, (src, dst, dev) in enumerate(plan(refs))
    ]


def _split_start(name, arrays, plan, n):
    k = len(arrays)

    def body(*refs):
        for cp in _copies(plan, refs[:k], refs[k], refs[k + 1]):
            cp.start()
        refs[-1][...] = jnp.zeros_like(refs[-1])

    out = pl.pallas_call(
        body,
        name=name,
        out_shape=(pltpu.SemaphoreType.DMA((n,)), pltpu.SemaphoreType.DMA((n,)), *[pltpu.HBM(a.shape, a.dtype) for a in arrays], jax.ShapeDtypeStruct((8, 128), F32)),
        in_specs=[HBM] * k,
        out_specs=(SEM, SEM, *[HBM] * k, pl.BlockSpec(memory_space=pltpu.VMEM)),
        input_output_aliases={i: 2 + i for i in range(k)},
        compiler_params=pltpu.CompilerParams(has_side_effects=EFFECT),
    )(*[pltpu.with_memory_space_constraint(a, pltpu.HBM) for a in arrays])
    return out[0], out[1], list(out[2 : 2 + k]), out[-1]


def _split_wait(name, send, recv, arrays, plan, after):
    k = len(arrays)

    def body(*refs):
        for cp in _copies(plan, refs[:k], refs[k], refs[k + 1]):
            cp.wait_send()
            cp.wait_recv()

    out = pl.pallas_call(
        body,
        name=name,
        out_shape=tuple(pltpu.HBM(a.shape, a.dtype) for a in arrays),
        in_specs=[HBM] * k + [SEM, SEM, ANY],
        out_specs=tuple([HBM] * k),
        input_output_aliases={i: i for i in range(k)},
        compiler_params=pltpu.CompilerParams(has_side_effects=EFFECT),
    )(*arrays, send, recv, after)
    return list(out)


def _cast_into(w, me, axis, name, dep):
    r, c = w.shape
    tr = _tile(r, (256, 176))
    nr = r // tr

    def body(me_ref, w_ref, dep_ref, o_ref):
        o_ref[...] = w_ref[...].astype(BF16)

    if axis == 1:
        shape, o_spec = (r, N_DEV * c), pl.BlockSpec((tr, c), lambda i, me: (i, me[0]))
    else:
        shape, o_spec = (N_DEV * r, c), pl.BlockSpec((tr, c), lambda i, me: (me[0] * nr + i, 0))
    return pl.pallas_call(
        body,
        name=name,
        grid_spec=pltpu.PrefetchScalarGridSpec(
            num_scalar_prefetch=1, grid=(nr,), in_specs=[pl.BlockSpec((tr, c), lambda i, me: (i, 0)), ANY], out_specs=o_spec
        ),
        out_shape=jax.ShapeDtypeStruct(shape, BF16),
        compiler_params=_params("parallel"),
    )(me, w, dep)


GATHER = [
    (["w_out"], None, "att_fwd", None),
    (["w_up"], None, "att_fwd", "norm_ffn_fwd"),
    (["w_down", "w_ple_gate", "w_ple_proj"], "att_fwd", "up_proj", "ffn_act_fwd"),
]
GROUPS = [["w_ple_proj", "w_ple_gate", "w_down"], ["w_up"], ["w_out"], ["w_in"]]
STAGES = ["ffn_act_bwd", "d_mix_out", "hgrn_bwd", "d_norm_mix_out"]
INTERLEAVED = {"w_up"}


class _Exchange:
    def __init__(self, big, position):
        self.big, self.axis = big, dict(BIG)
        self.size = {k: big[k].shape[self.axis[k]] for k in big}
        self.x, self.y, self.c = position
        chips = [(self.x, self.y)] + _other_chips(self.x, self.y)
        landed = [2 * cx + cy for cx, cy in chips]
        self.ids = {
            flag: jnp.stack([_block_index((cx, cy, self.c), flag) for cx, cy in chips] + landed).astype(jnp.int32) for flag in (False, True)
        }
        self.tokens, self.grads, self.state, self.wfull = [], {}, {}, {}


    def _slot(self, ref, k, dev):
        return _shard_of(ref, self.axis[k], self.size[k], dev, interleaved=k in INTERLEAVED)

    def _plan_gather(self, names, direct, refs):
        x, y, c = _position()
        me, out = (x, y, c), []
        for k, ref in zip(names, refs):
            mine = self._slot(ref, k, me)
            out.append((mine, mine, (x, y, 1 - c)))
            out += [(mine, mine, (*chip, c)) for chip in _other_chips(x, y)]
            if direct:
                out += [(mine, mine, (*chip, 1 - c)) for chip in _other_chips(x, y)]
        return out

    def _plan_forward(self, names, refs):
        x, y, c = _position()
        out = []
        for k, ref in zip(names, refs):
            for chip in _other_chips(x, y):
                block = self._slot(ref, k, (*chip, c))
                out.append((block, block, (x, y, 1 - c)))
        return out

    def _plan_sibling(self, names, refs):
        x, y, c = _position()
        n = len(names)
        return [(self._slot(refs[i], k, (p // 2, p % 2, 1 - c)), refs[n + i].at[p], (x, y, 1 - c)) for i, k in enumerate(names) for p in range(4)]

    def _plan_chips(self, names, refs):
        x, y, c = _position()
        n = len(names)
        return [(refs[i].at[j], refs[n + i].at[j], (*chip, c)) for i in range(n) for j, chip in enumerate(_other_chips(x, y))]


    def gather(self, conv_w):
        w_in, conv_full = _all_gather([_cast_bf16(self.big["w_in"], "cast_w_in"), conv_w], [1, 1], [False, True])
        self.wfull["w_in"] = w_in
        me = {flag: _block_index((self.x, self.y, self.c), flag).astype(jnp.int32).reshape(1) for flag in (False, True)}
        self.late, self.unsent = {}, {}
        after = w_in
        for gi, (names, issued, *_) in enumerate(GATHER):
            self.unsent[gi] = [_cast_into(self.big[k], me[k in INTERLEAVED], self.axis[k], "cast_" + k, after) for k in names]
            if issued is None:
                self._issue(None)
                after = self.tokens[-1]
        return conv_full

    def _issue(self, stage):
        for gi, (names, issued, _, forwarded) in enumerate(GATHER):
            if issued == stage and gi in self.unsent:
                plan = functools.partial(self._plan_gather, names, forwarded is None)
                copies = (7 if forwarded is None else 4) * len(names)
                send, recv, fulls, token = _split_start(f"gather_start_{gi}", self.unsent.pop(gi), plan, copies)
                self.tokens.append(token)
                self.late[gi] = (send, recv, fulls, plan)

    def weight(self, k):
        return self.wfull[k]

    def dep(self):
        tokens, self.tokens = self.tokens, []
        return tokens

    def reduce(self, vec, name):
        return _all_reduce_small(vec, name)

    def grad(self, k, g):
        self.grads[k] = g
        for gi, names in enumerate(GROUPS):
            if k == names[-1]:
                plan = functools.partial(self._plan_sibling, names)
                lands = [lax.empty((4, *self._shard_shape(n)), F32) for n in names]
                send, recv, arrays, token = _split_start(f"sibling_start_{gi}", [self.grads[n] for n in names] + lands, plan, 4 * len(names))
                self.tokens.append(token)
                self.state[gi] = (send, recv, arrays, plan)

    def done(self, stage, after):
        for gi, (names, _, _, forwarded) in enumerate(GATHER):
            if forwarded == stage:
                send, recv, fulls, plan = self.late[gi]
                self.wfull.update(zip(names, _split_wait(f"forward_wait_{gi}", send, recv, fulls, plan, after)))
        for gi, (names, _, arrived, forwarded) in enumerate(GATHER):
            if arrived == stage:
                send, recv, fulls, plan = self.late[gi]
                fulls = _split_wait(f"gather_wait_{gi}", send, recv, fulls, plan, after)
                if forwarded is None:
                    self.wfull.update(zip(names, fulls))
                else:
                    plan = functools.partial(self._plan_forward, names)
                    send, recv, fulls, token = _split_start(f"forward_start_{gi}", fulls, plan, 3 * len(names))
                    self.tokens.append(token)
                    self.late[gi] = (send, recv, fulls, plan)
        self._issue(stage)
        if stage in STAGES:
            self._to_chips(STAGES.index(stage), after)

    def _shard_shape(self, k):
        shape = list(self.grads[k].shape)
        shape[self.axis[k]] = self.size[k]
        return tuple(shape)

    def _to_chips(self, gi, after):
        names = GROUPS[gi]
        n = len(names)
        send, recv, arrays, plan = self.state[gi]
        arrays = _split_wait(f"sibling_wait_{gi}", send, recv, arrays, plan, after)
        own, parts = [], []
        for k, g, land in zip(names, arrays[:n], arrays[n:]):
            ids = self.ids[k in INTERLEAVED]
            own.append(_add_blocks(ids, g, land, self.axis[k], self.size[k], [0], F32, "add_own_" + k)[0])
            parts.append(_add_blocks(ids, g, land, self.axis[k], self.size[k], [1, 2, 3], BF16, "add_send_" + k))
        plan = functools.partial(self._plan_chips, names)
        lands = [lax.empty(part.shape, BF16) for part in parts]
        send, recv, arrays, token = _split_start(f"chips_start_{gi}", parts + lands, plan, 3 * n)
        self.tokens.append(token)
        self.state[gi] = (send, recv, arrays, plan, own)

    def finish(self, gi, after):
        names = GROUPS[gi]
        send, recv, arrays, plan, own = self.state[gi]
        arrays = _split_wait(f"chips_wait_{gi}", send, recv, arrays, plan, after)
        return {k: (o, r) for k, o, r in zip(names, own, arrays[len(names) :])}


class _Resident:
    def __init__(self, wfull):
        self.wfull, self.grads = wfull, {}

    def weight(self, k):
        return self.wfull[k]

    def grad(self, k, g):
        self.grads[k] = g

    def dep(self):
        return None

    def reduce(self, vec, name):
        return vec

    def done(self, stage, after):
        pass


def _local_step(x, p, target, small, conv_w, ex):
    a1, r1 = _rms_fwd(x, small["norm_mix"], "norm_mix_fwd", dep=ex.dep())
    proj = _matmul(a1, ex.weight("w_in"), "nn", F32, "in_proj")
    bias = _bias_table(jnp.pad(small["rel_bias"], ((0, 0), (0, N_REL_PAD - N_REL))))
    y_hg, o_hg, states = _hgrn_fwd(proj, small["lb_logits"], small["hg_norm"])
    y_att = _att_fwd(proj, bias, dep=ex.dep())
    ex.done("att_fwd", y_att)
    ycat = lax.dynamic_update_slice(y_hg, y_att, (0, HG_WIDTH))
    h1 = _matmul(ycat, ex.weight("w_out"), "nn", F32, "out_proj", resid=x, dep=ex.dep())
    a2, r2 = _rms_fwd(h1, small["norm_ffn"], "norm_ffn_fwd")
    ex.done("norm_ffn_fwd", a2)
    u = _matmul(a2, ex.weight("w_up"), "nn", BF16, "up_proj")
    conv_b = _interleave_cols(small["conv_b"])
    ex.done("up_proj", u)
    z = _ffn_act_fwd(u, conv_w, conv_b)
    ex.done("ffn_act_fwd", z)
    h2 = _matmul(z, ex.weight("w_down"), "nn", F32, "down_proj", tk=2816, resid=h1)
    a3, r3 = _rms_fwd(h2, small["norm_ple"], "norm_ple_fwd")
    gpre = _matmul(a3, ex.weight("w_ple_gate"), "nn", F32, "ple_gate")
    pp = _matmul(p, ex.weight("w_ple_proj"), "nn", F32, "ple_proj")
    dh3, dgpre, dpp, d_final, loss = _ple_loss(gpre, pp, h2, small["final_norm"], target)

    ex.grad("w_ple_proj", _matmul(_transpose(p, "p_t"), dpp, "nn", F32, "d_w_ple_proj", tk=2048))
    ex.grad("w_ple_gate", _matmul(_transpose(a3, "a3_t"), dgpre, "nn", F32, "d_w_ple_gate", tk=2048))
    da3 = _matmul(dgpre, ex.weight("w_ple_gate"), "nt", F32, "d_norm_ple_out")
    dh2, dh2b, d_ple = _rms_bwd(da3, h2, r3, small["norm_ple"], dh3, "norm_ple_bwd")
    dz = _matmul(dh2b, ex.weight("w_down"), "nt", BF16, "d_ffn_act")
    ex.grad("w_down", _matmul(_transpose(z, "z_t"), dh2b, "nn", F32, "d_w_down", tk=2048))
    du, dcw, dcb = _ffn_act_bwd(u, dz, conv_w, conv_b, dep=ex.dep())
    ex.done("ffn_act_bwd", du)
    d_conv_w, d_conv_b = _deinterleave_cols(dcw), _deinterleave_cols(dcb)
    ex.grad("w_up", _matmul(_transpose(a2, "a2_t"), du, "nn", F32, "d_w_up", tk=2048, dep=ex.dep()))
    da2 = _matmul(du, ex.weight("w_up"), "nt", F32, "d_norm_ffn_out", tk=2816, dep=ex.dep())
    dh1, dh1b, d_ffn = _rms_bwd(da2, h1, r2, small["norm_ffn"], dh2, "norm_ffn_bwd")
    dycat = _matmul(dh1b, ex.weight("w_out"), "nt", F32, "d_mix_out")
    ex.done("d_mix_out", dycat)
    ex.grad("w_out", _matmul(_transpose(ycat, "ycat_t"), dh1b, "nn", F32, "d_w_out", tk=2048, dep=ex.dep()))
    dp_hg, d_lb, d_hgn = _hgrn_bwd(proj, small["lb_logits"], small["hg_norm"], o_hg, dycat, states, dep=ex.dep())
    ex.done("hgrn_bwd", d_lb)
    dq_att, dk_att, dv_att, gsum = _att_bwd(proj, bias, dycat, dep=ex.dep())
    d_rel = _rel_bias_grad(gsum)
    d_small = {
        "norm_mix": jnp.zeros_like(small["norm_mix"]), "lb_logits": d_lb, "hg_norm": d_hgn, "rel_bias": d_rel, "norm_ffn": d_ffn,
        "conv_b": d_conv_b, "norm_ple": d_ple, "final_norm": d_final,
    }
    packed = jnp.concatenate([_pack_small(d_small), _rows128(d_conv_w, CONV_W_FULL_ROWS), _rows128(loss[0:1, 0:1], 8)], axis=0)
    early = ex.reduce(packed, "all_reduce_small")
    dproj = dp_hg
    for k, part in enumerate((dq_att, dk_att, dv_att)):
        dproj = lax.dynamic_update_slice(dproj, part, (0, 4 * HG_WIDTH + k * ATT_WIDTH))
    ex.grad("w_in", _matmul(_transpose(a1, "a1_t"), dproj, "nn", F32, "d_w_in", tk=2048, dep=[early]))
    da1 = _matmul(dproj, ex.weight("w_in"), "nt", F32, "d_norm_mix_out", tk=1792, dep=ex.dep())
    dx, _, d_mix = _rms_bwd(da1, x, r1, small["norm_mix"], dh1, "norm_mix_bwd")
    rows = dict(SMALL)["norm_mix"]
    late = ex.reduce(_rows128(d_mix, rows), "all_reduce_norm_mix")
    ex.done("d_norm_mix_out", late)
    return dx, jnp.concatenate([late, early[rows:]], axis=0)


def kernel(x, p, norm_mix, w_in, lb_logits, hg_norm, rel_bias, w_out, norm_ffn, w_up, conv_w, conv_b, w_down, norm_ple, w_ple_gate, w_ple_proj, final_norm, loss_target, m_norm_mix, m_w_in, m_lb_logits, m_hg_norm, m_rel_bias, m_w_out, m_norm_ffn, m_w_up, m_conv_w, m_conv_b, m_w_down, m_norm_ple, m_w_ple_gate, m_w_ple_proj, m_final_norm, v_norm_mix, v_w_in, v_lb_logits, v_hg_norm, v_rel_bias, v_w_out, v_norm_ffn, v_w_up, v_conv_w, v_conv_b, v_w_down, v_norm_ple, v_w_ple_gate, v_w_ple_proj, v_final_norm):
    given = dict(locals())
    mx, my, mc = _position()
    me = 4 * mx + 2 * my + mc
    big = {k: given[k][0] for k, _ in BIG}
    ex = _Exchange(big, (mx, my, mc))
    conv_w_full = ex.gather(conv_w[0])

    small = {
        "norm_mix": norm_mix, "lb_logits": lb_logits, "hg_norm": hg_norm, "rel_bias": rel_bias[0], "norm_ffn": norm_ffn,
        "conv_b": conv_b, "norm_ple": norm_ple, "final_norm": final_norm.reshape(1, -1),
    }
    dx, reduced = _local_step(x[0], p[0, 0], loss_target[0], small, conv_w_full, ex)

    out = {}
    shapes = {k: given[k].shape for k, _ in SMALL}
    g_small, at = _unpack_small(reduced, shapes)
    g_conv_full = reduced[at : at + CONV_W_FULL_ROWS].reshape(3, 2 * D_FF)
    total_loss = reduced[at + CONV_W_FULL_ROWS, 0]
    cw = conv_w.shape[2]
    g_conv = lax.dynamic_slice_in_dim(g_conv_full, me * cw, cw, axis=1)

    def pack_with_conv(parts, conv_part):
        return jnp.concatenate([_pack_small(parts), _rows128(conv_part, CONV_W_SHARD_ROWS)], axis=0)

    d_pk, m_pk, v_pk = _adam_small(
        pack_with_conv({k: given[k] for k, _ in SMALL}, conv_w),
        pack_with_conv(g_small, g_conv),
        pack_with_conv({k: given["m_" + k] for k, _ in SMALL}, m_conv_w),
        pack_with_conv({k: given["v_" + k] for k, _ in SMALL}, v_conv_w),
    )
    for name, pk in (("d", d_pk), ("m", m_pk), ("v", v_pk)):
        parts, at = _unpack_small(pk, shapes)
        parts["conv_w"] = pk[at : at + CONV_W_SHARD_ROWS].reshape(-1)[: 3 * cw].reshape(conv_w.shape)
        for k, a in parts.items():
            out.setdefault(k, {})
            out[k][name] = a
    for k, _ in SMALL:
        out[k]["g"] = g_small[k]
    out["conv_w"]["g"] = g_conv.reshape(conv_w.shape)

    after, started = v_pk, ex.dep()
    for gi in range(len(GROUPS)):
        for k, (o, r) in ex.finish(gi, after).items():
            g, d, nm, nv = _adam_big(big[k], given["m_" + k][0], given["v_" + k][0], o, r, "adam_" + k, dep=started)
            out[k] = tuple(a[None] for a in (g, d, nm, nv))
            after = nv

    order = ["norm_mix", "w_in", "lb_logits", "hg_norm", "rel_bias", "w_out", "norm_ffn", "w_up", "conv_w", "conv_b", "w_down", "norm_ple", "w_ple_gate", "w_ple_proj", "final_norm"]

    def pick(k, what):
        return out[k][what] if isinstance(out[k], dict) else out[k][{"g": 0, "d": 1, "m": 2, "v": 3}[what]]

    return (total_loss, dx[None], *[pick(k, "g") for k in order], *[pick(k, "d") for k in order], *[pick(k, "m") for k in order], *[pick(k, "v") for k in order])
```

```python
import functools

import jax
import jax.numpy as jnp
from jax import lax
from jax.experimental import pallas as pl
from jax.experimental.pallas import tpu as pltpu

F32 = jnp.float32
BF16 = jnp.bfloat16

D_MODEL = 2048
CHUNK = 64
HG_HEADS = 8
HEAD_DIM = 128
HG_WIDTH = HG_HEADS * HEAD_DIM
ATT_HEADS = 8
ATT_WIDTH = ATT_HEADS * HEAD_DIM
LEFT_CHUNKS = 8
PAD = LEFT_CHUNKS * CHUNK
BAND = PAD + CHUNK
REL_CLIP = 128
N_REL = 2 * REL_CLIP + 1
N_REL_PAD = 384
D_FF = 5632
EPS = 1e-6
ATT_SCALE = HEAD_DIM ** -0.5
SUB = 32
HG_BLOCK = 8
Q_BLOCK = 4 * CHUNK
K_BLOCK = Q_BLOCK + PAD
DIAG = 1024
MASKED = -1e30

ADAM_LR = 0.001
ADAM_B1 = 0.9
ADAM_B2 = 0.999
ADAM_EPS = 1e-08
ADAM_WD = 0.01
ADAM_STEP = 10

N_DEV = 8
VMEM_LIMIT = 48 * 1024 * 1024
MESH = pl.DeviceIdType.MESH
ANY = pl.BlockSpec(memory_space=pl.ANY)
HIGHEST = lax.Precision.HIGHEST

NN = (((1,), (0,)), ((), ()))
NT = (((1,), (1,)), ((), ()))
TN = (((0,), (0,)), ((), ()))


def _params(*sem):
    return pltpu.CompilerParams(dimension_semantics=sem if sem else None, vmem_limit_bytes=VMEM_LIMIT)


def _pallas(body, n_in, dep, **kw):
    deps = [] if dep is None else list(dep)
    if not deps:
        return pl.pallas_call(body, **kw)

    def body_after(*refs):
        body(*refs[:n_in], *refs[n_in + len(deps) :])

    call = pl.pallas_call(body_after, **dict(kw, in_specs=list(kw["in_specs"]) + [ANY] * len(deps)))
    return lambda *ops: call(*ops, *deps)


def _dot(a, b, dims=NN):
    return lax.dot_general(a, b, dims, preferred_element_type=F32)


def _dot3(a, b, dims=NN):
    a_hi, b_hi = a.astype(BF16), b.astype(BF16)
    a_lo, b_lo = (a - a_hi.astype(F32)).astype(BF16), (b - b_hi.astype(F32)).astype(BF16)
    return _dot(a_hi, b_hi, dims) + (_dot(a_hi, b_lo, dims) + _dot(a_lo, b_hi, dims))


def _sigmoid(x):
    return 1.0 / (1.0 + jnp.exp(-x))


def _tile(n, prefs):
    for t in prefs:
        if n % t == 0:
            return t
    return n


def _matmul(a, b, mode, out_dtype, name, tm=1024, tn=1024, tk=None, resid=None, dep=None):
    if mode == "nn":
        (m, k), n = a.shape, b.shape[1]
    elif mode == "nt":
        (m, k), n = a.shape, b.shape[0]
    else:
        (k, m), n = a.shape, b.shape[1]
    tm = _tile(m, (tm, 512, 256, 128))
    tn = _tile(n, (tn, 1408, 512, 256, 128))
    tk = k if tk is None else _tile(k, (tk,))
    nk = k // tk
    dims = {"nn": NN, "nt": NT, "tn": TN}[mode]
    a_spec = pl.BlockSpec((tk, tm), lambda i, j, s: (s, i)) if mode == "tn" else pl.BlockSpec((tm, tk), lambda i, j, s: (i, s))
    b_spec = pl.BlockSpec((tn, tk), lambda i, j, s: (j, s)) if mode == "nt" else pl.BlockSpec((tk, tn), lambda i, j, s: (s, j))
    o_spec = pl.BlockSpec((tm, tn), lambda i, j, s: (i, j))
    has_res = resid is not None

    def body(*refs):
        a_ref, b_ref = refs[0], refs[1]
        o_ref = refs[2 + has_res]
        part = _dot(a_ref[...].astype(BF16), b_ref[...].astype(BF16), dims)

        def finish(acc):
            if has_res:
                acc = acc + refs[2][...]
            o_ref[...] = acc.astype(out_dtype)

        if nk == 1:
            finish(part)
        else:
            acc_ref = refs[-1]
            s = pl.program_id(2)

            @pl.when(s == 0)
            def _():
                acc_ref[...] = part

            @pl.when(s > 0)
            def _():
                acc_ref[...] += part

            @pl.when(s == nk - 1)
            def _():
                finish(acc_ref[...])

    return _pallas(
        body,
        2 + has_res,
        dep,
        name=name,
        grid=(m // tm, n // tn, nk),
        in_specs=[a_spec, b_spec] + ([o_spec] if has_res else []),
        out_specs=o_spec,
        out_shape=jax.ShapeDtypeStruct((m, n), out_dtype),
        scratch_shapes=[pltpu.VMEM((tm, tn), F32)] if nk > 1 else [],
        compiler_params=_params("parallel", "parallel", "arbitrary"),
    )(*([a, b] + ([resid] if has_res else [])))


def _rms_fwd(x, g, name, dep=None):
    t, d = x.shape
    tm = _tile(t, (256,))

    def body(x_ref, g_ref, a_ref, r_ref):
        xv = x_ref[...]
        r = lax.rsqrt(jnp.mean(xv * xv, axis=-1, keepdims=True) + EPS)
        a_ref[...] = (xv * r * g_ref[...]).astype(BF16)
        r_ref[...] = r

    row = pl.BlockSpec((tm, d), lambda i: (i, 0))
    return _pallas(
        body,
        2,
        dep,
        name=name,
        grid=(t // tm,),
        in_specs=[row, pl.BlockSpec((1, d), lambda i: (0, 0))],
        out_specs=[row, pl.BlockSpec((tm, 1), lambda i: (i, 0))],
        out_shape=[jax.ShapeDtypeStruct((t, d), BF16), jax.ShapeDtypeStruct((t, 1), F32)],
        compiler_params=_params("parallel"),
    )(x, g)


def _rms_bwd(da, x, r, g, resid, name, dep=None):
    t, d = x.shape
    tm = _tile(t, (256,))

    def body(da_ref, x_ref, r_ref, g_ref, res_ref, dx_ref, dg_ref):
        i = pl.program_id(0)
        rv = r_ref[...]
        n = x_ref[...] * rv
        dav = da_ref[...]
        dn = dav * g_ref[...]
        dx_ref[...] = rv * (dn - n * jnp.mean(dn * n, axis=-1, keepdims=True)) + res_ref[...]
        part = jnp.sum(dav * n, axis=0, keepdims=True)

        @pl.when(i == 0)
        def _():
            dg_ref[...] = part

        @pl.when(i > 0)
        def _():
            dg_ref[...] += part

    row = pl.BlockSpec((tm, d), lambda i: (i, 0))
    vec = pl.BlockSpec((1, d), lambda i: (0, 0))
    return _pallas(
        body,
        5,
        dep,
        name=name,
        grid=(t // tm,),
        in_specs=[row, row, pl.BlockSpec((tm, 1), lambda i: (i, 0)), vec, row],
        out_specs=[row, vec],
        out_shape=[jax.ShapeDtypeStruct((t, d), F32), jax.ShapeDtypeStruct((1, d), F32)],
        compiler_params=_params("arbitrary"),
    )(da, x, r, g, resid)


def _tri(n, upper):
    r = lax.broadcasted_iota(jnp.int32, (n, n), 0)
    c = lax.broadcasted_iota(jnp.int32, (n, n), 1)
    return jnp.where((c >= r) if upper else (c <= r), 1.0, 0.0).astype(F32)


def _hgrn_gates(q, fp, lbl):
    l0, l1 = lbl[0:1, :], lbl[1:2, :]
    mx = jnp.maximum(l0, l1)
    e0, e1 = jnp.exp(l0 - mx), jnp.exp(l1 - mx)
    lb = e0 / (e0 + e1)
    sig = _sigmoid(fp)
    f = lb + (1.0 - lb) * sig
    kk = (1.0 - lb) * _sigmoid(-fp)
    sq = _sigmoid(q)
    b = jnp.dot(_tri(CHUNK, False), jnp.log(f), precision=HIGHEST, preferred_element_type=F32)
    return lb, sig, f, kk, sq, q * sq, b


def _heads(x):
    return [x[:, j * HEAD_DIM : (j + 1) * HEAD_DIM] for j in range(x.shape[1] // HEAD_DIM)]


def _wide(parts):
    return jnp.concatenate(parts, axis=1)


def _intra_blocks(b):
    out = []
    for lo in range(0, CHUNK, SUB):
        hi = lo + SUB
        br = b[lo + SUB // 2 : lo + SUB // 2 + 1, :]
        row = lax.broadcasted_iota(jnp.int32, (SUB, hi), 0) + lo
        col = lax.broadcasted_iota(jnp.int32, (SUB, hi), 1)
        out.append((lo, hi, jnp.exp(b[lo:hi] - br), jnp.exp(br - b[:hi]), col <= row))
    return out


def _hgrn_fwd(proj, lb_logits, hg_norm):
    t = proj.shape[0]
    nc = t // CHUNK

    def body(q_ref, f_ref, i_ref, g_ref, lbl_ref, hgn_ref, y_ref, o_ref, st_ref, s_scr):
        c = pl.program_id(1)

        @pl.when(c == 0)
        def _():
            s_scr[...] = jnp.zeros_like(s_scr)

        hs = range(HG_BLOCK)
        sts = [s_scr[j] for j in hs]
        _, _, _, kk, _, qf, b = _hgrn_gates(q_ref[...], f_ref[...], lbl_ref[...])
        vb = _heads(i_ref[...].astype(BF16))
        bl = b[CHUNK - 1 : CHUNK, :]
        qe = _heads((qf * jnp.exp(b)).astype(BF16))
        kd = _heads((kk * jnp.exp(bl - b)).astype(BF16))
        decay = _heads(jnp.exp(bl))
        o = [_dot(qe[j], sts[j].astype(BF16), NT) for j in hs]
        parts = [[] for _ in hs]
        for lo, hi, ea, eb, mask in _intra_blocks(b):
            a, bk = _heads((qf[lo:hi] * ea).astype(BF16)), _heads((kk[:hi] * eb).astype(BF16))
            p = [jnp.where(mask, _dot(a[j], bk[j], NT), 0.0).astype(BF16) for j in hs]
            for j in hs:
                parts[j].append(_dot(p[j], vb[j][:hi]))
        o = [o[j] + jnp.concatenate(parts[j], axis=0) for j in hs]
        new = [sts[j] * decay[j] + _dot(vb[j], kd[j], TN) for j in hs]
        hgn = hgn_ref[...]
        on = [o[j] * lax.rsqrt(jnp.mean(o[j] * o[j], axis=-1, keepdims=True) + EPS) * hgn for j in hs]
        gg = g_ref[...]
        for j in hs:
            st_ref[j] = sts[j]
            s_scr[j] = new[j]
        o_ref[...] = _wide(o)
        y_ref[...] = (_wide(on) * (gg * _sigmoid(gg))).astype(BF16)

    wide = HG_BLOCK * HEAD_DIM
    groups = HG_HEADS // HG_BLOCK

    def col(k):
        return pl.BlockSpec((CHUNK, wide), lambda g, c: (c, k * groups + g))

    out = pl.BlockSpec((CHUNK, wide), lambda g, c: (c, g))
    return pl.pallas_call(
        body,
        name="hgrn_fwd",
        grid=(groups, nc),
        in_specs=[col(0), col(1), col(2), col(3), pl.BlockSpec((2, wide), lambda g, c: (0, g)), pl.BlockSpec((1, HEAD_DIM), lambda g, c: (0, 0))],
        out_specs=[out, out, pl.BlockSpec((HG_BLOCK, None, HEAD_DIM, HEAD_DIM), lambda g, c: (g, c, 0, 0))],
        out_shape=[
            jax.ShapeDtypeStruct((t, HG_WIDTH + ATT_WIDTH), BF16),
            jax.ShapeDtypeStruct((t, HG_WIDTH), F32),
            jax.ShapeDtypeStruct((HG_HEADS, nc, HEAD_DIM, HEAD_DIM), F32),
        ],
        scratch_shapes=[pltpu.VMEM((HG_BLOCK, HEAD_DIM, HEAD_DIM), F32)],
        compiler_params=_params("arbitrary", "arbitrary"),
    )(proj, proj, proj, proj, lb_logits, hg_norm)


def _hgrn_bwd(proj, lb_logits, hg_norm, o_hg, dycat, states, dep=None):
    t = proj.shape[0]
    nc = t // CHUNK

    def body(q_ref, f_ref, i_ref, g_ref, lbl_ref, hgn_ref, o_ref, dy_ref, st_ref, dp_ref, dlbl_ref, dhgn_ref, dst_scr, dlb_scr):
        h = pl.program_id(0)
        c = pl.program_id(1)

        @pl.when(c == 0)
        def _():
            dst_scr[...] = jnp.zeros_like(dst_scr)
            dlb_scr[...] = jnp.zeros_like(dlb_scr)

        @pl.when((c == 0) & (h == 0))
        def _():
            dhgn_ref[...] = jnp.zeros_like(dhgn_ref)

        hs = range(HG_BLOCK)
        hgn = _wide([hgn_ref[...]] * HG_BLOCK)
        q, fp, gg, vi = q_ref[...], f_ref[...], g_ref[...], i_ref[...]
        lb, sig, f, kk, sq, qf, b = _hgrn_gates(q, fp, lbl_ref[...])
        o, dy = o_ref[...], dy_ref[...]
        sg = _sigmoid(gg)
        n = _wide([oh * lax.rsqrt(jnp.mean(oh * oh, axis=-1, keepdims=True) + EPS) for oh in _heads(o)])
        don = dy * (gg * sg)
        dgg = dy * (n * hgn) * (sg * (1.0 + gg * (1.0 - sg)))
        d_hgn = sum(_heads(jnp.sum(don * n, axis=0, keepdims=True)))
        dn = don * hgn
        do = _wide(
            [
                lax.rsqrt(jnp.mean(oh * oh, axis=-1, keepdims=True) + EPS) * (dnh - nh * jnp.mean(dnh * nh, axis=-1, keepdims=True))
                for oh, dnh, nh in zip(_heads(o), _heads(dn), _heads(n))
            ]
        )
        sts = [st_ref[j] for j in hs]
        dstn = [dst_scr[j] for j in hs]
        bl = b[CHUNK - 1 : CHUNK, :]
        e_b, e_bl, e_l = jnp.exp(b), jnp.exp(bl - b), jnp.exp(bl)
        doh, vih = _heads(do), _heads(vi)
        dobh = _heads(do.astype(BF16))
        dq_acc = _wide([_dot3(doh[j], sts[j]) for j in hs]) * e_b
        dk_inter = _wide([_dot3(vih[j], dstn[j]) for j in hs]) * e_bl
        dk_acc = dk_inter
        kd = _heads((kk * e_bl).astype(BF16))
        dv_acc = _wide([_dot(kd[j], dstn[j].astype(BF16), NT) for j in hs])
        qe, decay = _heads((qf * e_b).astype(BF16)), _heads(e_l)
        dst_new = [dstn[j] * decay[j] + _dot(dobh[j], qe[j], TN) for j in hs]
        db_last = e_l * _wide([jnp.sum(sts[j] * dstn[j], axis=0, keepdims=True) for j in hs]) + jnp.sum(kk * dk_inter, axis=0, keepdims=True)
        dq_parts = []
        for lo, hi, ea, eb, mask in _intra_blocks(b):
            a, bk = qf[lo:hi] * ea, kk[:hi] * eb
            ah, bkh = _heads(a), _heads(bk)
            abh, bkbh = _heads(a.astype(BF16)), _heads(bk.astype(BF16))
            p = [jnp.where(mask, _dot(abh[j], bkbh[j], NT), 0.0).astype(BF16) for j in hs]
            dp = [jnp.where(mask, _dot3(doh[j][lo:hi], vih[j][:hi], NT), 0.0) for j in hs]
            dq_parts.append(_wide([_dot3(dp[j], bkh[j]) for j in hs]) * ea)
            dki = _wide([_dot3(dp[j], ah[j], TN) for j in hs]) * eb
            dvi = _wide([_dot(p[j], dobh[j][lo:hi], TN) for j in hs])
            if hi < CHUNK:
                zeros = jnp.zeros((CHUNK - hi, HG_BLOCK * HEAD_DIM), F32)
                dki = jnp.concatenate([dki, zeros], axis=0)
                dvi = jnp.concatenate([dvi, zeros], axis=0)
            dk_acc = dk_acc + dki
            dv_acc = dv_acc + dvi
        dq_acc = dq_acc + jnp.concatenate(dq_parts, axis=0)
        rows = lax.broadcasted_iota(jnp.int32, dq_acc.shape, 0)
        db = qf * dq_acc - kk * dk_acc + jnp.where(rows == CHUNK - 1, db_last, 0.0)
        dlf = jnp.dot(_tri(CHUNK, True), db, precision=HIGHEST, preferred_element_type=F32)
        dfk = dlf / f - dk_acc
        for k, part in enumerate((dq_acc * (sq * (1.0 + q * (1.0 - sq))), (1.0 - lb) * dfk * sig * (1.0 - sig), dv_acc, dgg)):
            dp_ref[:, k * HG_WIDTH : (k + 1) * HG_WIDTH] = part.astype(BF16)
        dlb_scr[...] += jnp.sum(dfk * (1.0 - sig), axis=0, keepdims=True)
        dhgn_ref[...] += d_hgn
        for j in hs:
            dst_scr[j] = dst_new[j]

        @pl.when(c == nc - 1)
        def _():
            dl0 = dlb_scr[...] * lb * (1.0 - lb)
            dlbl_ref[0:1, :] = dl0
            dlbl_ref[1:2, :] = -dl0

    wide = HG_BLOCK * HEAD_DIM
    groups = HG_HEADS // HG_BLOCK

    def col(k):
        return pl.BlockSpec((CHUNK, wide), lambda g, c: (nc - 1 - c, k * groups + g))

    blk = pl.BlockSpec((CHUNK, wide), lambda g, c: (nc - 1 - c, g))
    assert groups == 1, "d(q, f, i, g) are written as one contiguous column range of the in_proj gradient"
    return _pallas(
        body,
        9,
        dep,
        name="hgrn_bwd",
        grid=(groups, nc),
        in_specs=[
            col(0), col(1), col(2), col(3),
            pl.BlockSpec((2, wide), lambda g, c: (0, g)),
            pl.BlockSpec((1, HEAD_DIM), lambda g, c: (0, 0)),
            blk, blk,
            pl.BlockSpec((HG_BLOCK, None, HEAD_DIM, HEAD_DIM), lambda g, c: (g, nc - 1 - c, 0, 0)),
        ],
        out_specs=[
            pl.BlockSpec((CHUNK, 4 * HG_WIDTH), lambda g, c: (nc - 1 - c, 0)),
            pl.BlockSpec((2, wide), lambda g, c: (0, g)),
            pl.BlockSpec((1, HEAD_DIM), lambda g, c: (0, 0)),
        ],
        out_shape=[
            jax.ShapeDtypeStruct((t, 4 * HG_WIDTH + 3 * ATT_WIDTH), BF16),
            jax.ShapeDtypeStruct((2, HG_WIDTH), F32),
            jax.ShapeDtypeStruct((1, HEAD_DIM), F32),
        ],
        scratch_shapes=[pltpu.VMEM((HG_BLOCK, HEAD_DIM, HEAD_DIM), F32), pltpu.VMEM((1, wide), F32)],
        compiler_params=_params("arbitrary", "arbitrary"),
    )(proj, proj, proj, proj, lb_logits, hg_norm, o_hg, dycat, states)


def _diagonal_slots(shift):
    i = lax.broadcasted_iota(jnp.int32, (N_REL_PAD, DIAG), 0)
    u = lax.broadcasted_iota(jnp.int32, (N_REL_PAD, DIAG), 1)
    offset = u - shift if shift else jnp.where(u < K_BLOCK, u, u - DIAG)
    return jnp.where(jnp.clip(PAD - offset, -REL_CLIP, REL_CLIP) + REL_CLIP == i, 1.0, 0.0).astype(BF16)


def _split3(x):
    hi = x.astype(BF16)
    mid = (x - hi.astype(F32)).astype(BF16)
    return hi, mid, (x - hi.astype(F32) - mid.astype(F32)).astype(BF16)


def _bias_table(rel_bias):
    def body(rb_ref, o_ref, diag):
        h = pl.program_id(0)

        @pl.when(h == 0)
        def _():
            hi, mid, lo = _split3(rb_ref[...])
            slots = _diagonal_slots(0)
            diag[...] = _dot(hi, slots) + (_dot(mid, slots) + _dot(lo, slots))

        rows = jnp.broadcast_to(diag[pl.ds(h, 1), :], (Q_BLOCK, DIAG))
        row = lax.broadcasted_iota(jnp.int32, (Q_BLOCK, K_BLOCK), 0)
        col = lax.broadcasted_iota(jnp.int32, (Q_BLOCK, K_BLOCK), 1)
        first = row - (row & (CHUNK - 1))
        seen = (col >= first) & (col < first + BAND)
        o_ref[...] = jnp.where(seen, pltpu.roll(rows, 0, 1, stride=1, stride_axis=0)[:, :K_BLOCK], MASKED)

    return pl.pallas_call(
        body,
        name="bias_table",
        grid=(ATT_HEADS,),
        in_specs=[pl.BlockSpec((ATT_HEADS, N_REL_PAD), lambda h: (0, 0))],
        out_specs=pl.BlockSpec((None, Q_BLOCK, K_BLOCK), lambda h: (h, 0, 0)),
        out_shape=jax.ShapeDtypeStruct((ATT_HEADS, Q_BLOCK, K_BLOCK), F32),
        scratch_shapes=[pltpu.VMEM((ATT_HEADS, DIAG), F32)],
        compiler_params=_params("arbitrary"),
    )(rel_bias)


def _att_probs(q_ref, kpad, bias_ref, blk):
    qs = (q_ref[...] * ATT_SCALE).astype(BF16)
    start = pl.multiple_of(blk * Q_BLOCK, Q_BLOCK)
    kb = kpad[pl.ds(start, K_BLOCK), :]
    s = _dot(qs, kb, NT) + bias_ref[...]
    col = lax.broadcasted_iota(jnp.int32, (Q_BLOCK, K_BLOCK), 1)
    s = jnp.where(col >= PAD - blk * Q_BLOCK, s, MASKED)
    e = jnp.exp(s - jnp.max(s, axis=-1, keepdims=True))
    return qs, kb, start, e * (1.0 / jnp.sum(e, axis=-1, keepdims=True))


def _fill_padded(dst, src):
    dst[0:PAD, :] = jnp.zeros((PAD, HEAD_DIM), BF16)
    dst[PAD:, :] = src[...].astype(BF16)


def _att_fwd(proj, bias, dep=None):
    t = proj.shape[0]
    nb = t // Q_BLOCK

    def body(q_ref, k_ref, v_ref, bias_ref, y_ref, kpad, vpad):
        c = pl.program_id(1)

        @pl.when(c == 0)
        def _():
            _fill_padded(kpad, k_ref)
            _fill_padded(vpad, v_ref)

        _, _, start, p = _att_probs(q_ref, kpad, bias_ref, c)
        y_ref[...] = _dot(p.astype(BF16), vpad[pl.ds(start, K_BLOCK), :]).astype(BF16)

    base = 4 * HG_HEADS
    return _pallas(
        body,
        4,
        dep,
        name="att_fwd",
        grid=(ATT_HEADS, nb),
        in_specs=[
            pl.BlockSpec((Q_BLOCK, HEAD_DIM), lambda h, c: (c, base + h)),
            pl.BlockSpec((t, HEAD_DIM), lambda h, c: (0, base + ATT_HEADS + h)),
            pl.BlockSpec((t, HEAD_DIM), lambda h, c: (0, base + 2 * ATT_HEADS + h)),
            pl.BlockSpec((None, Q_BLOCK, K_BLOCK), lambda h, c: (h, 0, 0)),
        ],
        out_specs=pl.BlockSpec((Q_BLOCK, HEAD_DIM), lambda h, c: (c, h)),
        out_shape=jax.ShapeDtypeStruct((t, ATT_WIDTH), BF16),
        scratch_shapes=[pltpu.VMEM((t + PAD, HEAD_DIM), BF16), pltpu.VMEM((t + PAD, HEAD_DIM), BF16)],
        compiler_params=_params("arbitrary", "arbitrary"),
    )(proj, proj, proj, bias)


def _att_bwd(proj, bias, dycat, dep=None):
    t = proj.shape[0]
    nb = t // Q_BLOCK

    def body(q_ref, k_ref, v_ref, bias_ref, dy_ref, dq_ref, dk_ref, dv_ref, g_ref, kpad, vpad, dkacc, dvacc):
        c = pl.program_id(1)

        @pl.when(c == 0)
        def _():
            _fill_padded(kpad, k_ref)
            _fill_padded(vpad, v_ref)
            dkacc[...] = jnp.zeros_like(dkacc)
            dvacc[...] = jnp.zeros_like(dvacc)
            g_ref[...] = jnp.zeros_like(g_ref)

        qs, kb, start, p = _att_probs(q_ref, kpad, bias_ref, c)
        band = pl.ds(start, K_BLOCK)
        dyb = dy_ref[...].astype(BF16)
        dvacc[band, :] += _dot(p.astype(BF16), dyb, TN)
        dp = _dot(dyb, vpad[band, :], NT)
        ds = p * (dp - jnp.sum(dp * p, axis=-1, keepdims=True))
        g_ref[...] += ds
        dsb = ds.astype(BF16)
        dq_ref[...] = (_dot(dsb, kb) * ATT_SCALE).astype(BF16)
        dkacc[band, :] += _dot(dsb, qs, TN)

        @pl.when(c == nb - 1)
        def _():
            dk_ref[...] = dkacc[PAD:, :].astype(BF16)
            dv_ref[...] = dvacc[PAD:, :].astype(BF16)

    base = 4 * HG_HEADS
    whole = pl.BlockSpec((t, HEAD_DIM), lambda h, c: (0, h))
    return _pallas(
        body,
        5,
        dep,
        name="att_bwd",
        grid=(ATT_HEADS, nb),
        in_specs=[
            pl.BlockSpec((Q_BLOCK, HEAD_DIM), lambda h, c: (c, base + h)),
            pl.BlockSpec((t, HEAD_DIM), lambda h, c: (0, base + ATT_HEADS + h)),
            pl.BlockSpec((t, HEAD_DIM), lambda h, c: (0, base + 2 * ATT_HEADS + h)),
            pl.BlockSpec((None, Q_BLOCK, K_BLOCK), lambda h, c: (h, 0, 0)),
            pl.BlockSpec((Q_BLOCK, HEAD_DIM), lambda h, c: (c, HG_HEADS + h)),
        ],
        out_specs=[pl.BlockSpec((Q_BLOCK, HEAD_DIM), lambda h, c: (c, h)), whole, whole, pl.BlockSpec((None, Q_BLOCK, K_BLOCK), lambda h, c: (h, 0, 0))],
        out_shape=[
            jax.ShapeDtypeStruct((t, ATT_WIDTH), BF16),
            jax.ShapeDtypeStruct((t, ATT_WIDTH), BF16),
            jax.ShapeDtypeStruct((t, ATT_WIDTH), BF16),
            jax.ShapeDtypeStruct((ATT_HEADS, Q_BLOCK, K_BLOCK), F32),
        ],
        scratch_shapes=[
            pltpu.VMEM((t + PAD, HEAD_DIM), BF16),
            pltpu.VMEM((t + PAD, HEAD_DIM), BF16),
            pltpu.VMEM((t + PAD, HEAD_DIM), F32),
            pltpu.VMEM((t + PAD, HEAD_DIM), F32),
        ],
        compiler_params=_params("arbitrary", "arbitrary"),
    )(proj, proj, proj, bias, dycat)


def _rel_bias_grad(gsum):
    def body(g_ref, o_ref):
        r = lax.broadcasted_iota(jnp.int32, (Q_BLOCK, Q_BLOCK), 0)
        c = lax.broadcasted_iota(jnp.int32, (Q_BLOCK, Q_BLOCK), 1)
        flip = jnp.where(r + c == Q_BLOCK - 1, 1.0, 0.0).astype(BF16)
        sums = []
        for h in range(ATT_HEADS):
            hi, mid, lo = _split3(g_ref[h])
            rev = _dot(flip, hi) + (_dot(flip, mid) + _dot(flip, lo))
            wide = jnp.concatenate([rev, jnp.zeros((Q_BLOCK, DIAG - K_BLOCK), F32)], axis=1)
            sums.append(jnp.sum(pltpu.roll(wide, 0, 1, stride=1, stride_axis=0), axis=0, keepdims=True))
        hi, mid, lo = _split3(jnp.concatenate(sums, axis=0))
        slots = _diagonal_slots(Q_BLOCK - 1)
        o_ref[...] = _dot(hi, slots, NT) + (_dot(mid, slots, NT) + _dot(lo, slots, NT))

    return pl.pallas_call(
        body,
        name="rel_bias_grad",
        out_shape=jax.ShapeDtypeStruct((ATT_HEADS, N_REL_PAD), F32),
        compiler_params=_params(),
    )(gsum)


HALO = 16


FF_TILE = 1408
FF_TILES = D_FF // FF_TILE


def _interleave_cols(a):
    lead = a.shape[:-1]
    return jnp.swapaxes(a.reshape(*lead, 2, FF_TILES, FF_TILE), -3, -2).reshape(*lead, 2 * D_FF)


def _deinterleave_cols(a):
    lead = a.shape[:-1]
    return jnp.swapaxes(a.reshape(*lead, FF_TILES, 2, FF_TILE), -3, -2).reshape(*lead, 2 * D_FF)


def _ffn_specs(t, tm):
    wide = 2 * FF_TILE
    tile = pl.BlockSpec((tm, wide), lambda j, i: (i, j))
    before = pl.BlockSpec((HALO, wide), lambda j, i: (jnp.maximum(i * (tm // HALO) - 1, 0), j))
    after = pl.BlockSpec((HALO, wide), lambda j, i: (jnp.minimum((i + 1) * (tm // HALO), t // HALO - 1), j))
    vec = lambda rows: pl.BlockSpec((rows, wide), lambda j, i: (0, j))
    return tile, before, after, vec


def _shifted(x, rows, offsets):
    r = lax.broadcasted_iota(jnp.int32, (rows, x.shape[0]), 0)
    c = lax.broadcasted_iota(jnp.int32, (rows, x.shape[0]), 1)
    pick = jnp.concatenate([jnp.where(c == r + o, 1.0, 0.0).astype(BF16) for o in offsets], axis=0)
    out = _dot(pick, x)
    return [out[k * rows : (k + 1) * rows] for k in range(len(offsets))]


def _conv(x, w, b, rows):
    taps = _shifted(x, rows, [HALO - 2, HALO - 1]) + [x[HALO : HALO + rows].astype(F32)]
    return b + w[0:1] * taps[0] + w[1:2] * taps[1] + w[2:3] * taps[2], taps


def _ffn_act_fwd(u, conv_w, conv_b):
    t = u.shape[0]
    tm = _tile(t, (128,))
    tile, before, _, vec = _ffn_specs(t, tm)

    def body(u_ref, h_ref, w_ref, b_ref, z_ref):
        first = pl.program_id(1) == 0
        halo = h_ref[...]
        x = jnp.concatenate([jnp.where(first, jnp.zeros_like(halo), halo), u_ref[...]], axis=0)
        c, _ = _conv(x, w_ref[...], b_ref[...], tm)
        gate, val = c[:, :FF_TILE], c[:, FF_TILE:]
        z_ref[...] = (gate * _sigmoid(gate) * val).astype(BF16)

    return pl.pallas_call(
        body,
        name="ffn_act_fwd",
        grid=(FF_TILES, t // tm),
        in_specs=[tile, before, vec(3), vec(1)],
        out_specs=pl.BlockSpec((tm, FF_TILE), lambda j, i: (i, j)),
        out_shape=jax.ShapeDtypeStruct((t, D_FF), BF16),
        compiler_params=_params("parallel", "parallel"),
    )(u, u, conv_w, conv_b)


def _ffn_act_bwd(u, dz, conv_w, conv_b, dep=None):
    t = u.shape[0]
    tm = _tile(t, (128,))
    nt = t // tm
    ext = tm + HALO
    tile, before, after, vec = _ffn_specs(t, tm)

    def body(u_ref, ub_ref, ua_ref, w_ref, b_ref, dz_ref, dza_ref, du_ref, dw_ref, db_ref):
        i = pl.program_id(1)
        first, last = i == 0, i == nt - 1
        ub, ua = ub_ref[...], ua_ref[...]
        parts = [jnp.where(first, jnp.zeros_like(ub), ub), u_ref[...], jnp.where(last, jnp.zeros_like(ua), ua)]
        w = w_ref[...]
        c, taps = _conv(jnp.concatenate(parts, axis=0), w, b_ref[...], ext)
        gate, val = c[:, :FF_TILE], c[:, FF_TILE:]
        dz = jnp.concatenate([dz_ref[...].astype(F32), jnp.where(last, 0.0, dza_ref[...].astype(F32))], axis=0)
        sg = _sigmoid(gate)
        d = jnp.concatenate([dz * val * (sg * (1.0 + gate * (1.0 - sg))), dz * (gate * sg)], axis=1)
        d1, d2 = _shifted(d.astype(BF16), tm, [1, 2])
        du_ref[...] = (w[2:3] * d[:tm] + w[1:2] * d1 + w[0:1] * d2).astype(BF16)

        @pl.when(first)
        def _():
            dw_ref[...] = jnp.zeros_like(dw_ref)
            db_ref[...] = jnp.zeros_like(db_ref)

        for k, tap in enumerate(taps):
            dw_ref[k : k + 1, :] += jnp.sum(d[:tm] * tap[:tm], axis=0, keepdims=True)
        db_ref[...] += jnp.sum(d[:tm], axis=0, keepdims=True)

    narrow = lambda rows, index: pl.BlockSpec((rows, FF_TILE), index)
    return _pallas(
        body,
        7,
        dep,
        name="ffn_act_bwd",
        grid=(FF_TILES, nt),
        in_specs=[
            tile, before, after, vec(3), vec(1),
            narrow(tm, lambda j, i: (i, j)),
            narrow(HALO, lambda j, i: (jnp.minimum((i + 1) * (tm // HALO), t // HALO - 1), j)),
        ],
        out_specs=[tile, vec(3), vec(1)],
        out_shape=[
            jax.ShapeDtypeStruct((t, 2 * D_FF), BF16),
            jax.ShapeDtypeStruct((3, 2 * D_FF), F32),
            jax.ShapeDtypeStruct((1, 2 * D_FF), F32),
        ],
        compiler_params=_params("parallel", "arbitrary"),
    )(u, u, u, conv_w, conv_b, dz, dz)


def _ple_loss(gpre, pp, h2, final_norm, target):
    t, d = h2.shape
    tm = _tile(t, (256,))

    def body(gp_ref, pp_ref, h_ref, g_ref, tg_ref, dh_ref, dgp_ref, dpp_ref, dg_ref, loss_ref):
        i = pl.program_id(0)
        gate = _sigmoid(gp_ref[...])
        ppv = pp_ref[...]
        h3 = h_ref[...] + gate * ppv
        r = lax.rsqrt(jnp.mean(h3 * h3, axis=-1, keepdims=True) + EPS)
        n = h3 * r
        g = g_ref[...]
        err = n * g - tg_ref[...]
        loss = 0.5 * jnp.sum(jnp.mean(err * err, axis=-1, keepdims=True))
        dy = err * (1.0 / d)
        dn = dy * g
        dh = r * (dn - n * jnp.mean(dn * n, axis=-1, keepdims=True))
        dh_ref[...] = dh
        dgp_ref[...] = (dh * ppv * gate * (1.0 - gate)).astype(BF16)
        dpp_ref[...] = (dh * gate).astype(BF16)
        dg = jnp.sum(dy * n, axis=0, keepdims=True)

        @pl.when(i == 0)
        def _():
            dg_ref[...] = dg
            loss_ref[...] = jnp.full(loss_ref.shape, loss, F32)

        @pl.when(i > 0)
        def _():
            dg_ref[...] += dg
            loss_ref[...] += loss

    row = pl.BlockSpec((tm, d), lambda i: (i, 0))
    vec = pl.BlockSpec((1, d), lambda i: (0, 0))
    return pl.pallas_call(
        body,
        name="ple_loss",
        grid=(t // tm,),
        in_specs=[row, row, row, vec, row],
        out_specs=[row, row, row, vec, pl.BlockSpec((8, 128), lambda i: (0, 0))],
        out_shape=[
            jax.ShapeDtypeStruct((t, d), F32),
            jax.ShapeDtypeStruct((t, d), BF16),
            jax.ShapeDtypeStruct((t, d), BF16),
            jax.ShapeDtypeStruct((1, d), F32),
            jax.ShapeDtypeStruct((8, 128), F32),
        ],
        compiler_params=_params("arbitrary"),
    )(gpre, pp, h2, final_norm, target)


def _adamw(w, g, m, v):
    m = ADAM_B1 * m + (1.0 - ADAM_B1) * g
    v = ADAM_B2 * v + (1.0 - ADAM_B2) * (g * g)
    m_hat = m / (1.0 - ADAM_B1 ** ADAM_STEP)
    v_hat = v / (1.0 - ADAM_B2 ** ADAM_STEP)
    return -ADAM_LR * (m_hat / (jnp.sqrt(v_hat) + ADAM_EPS) + ADAM_WD * w), m, v


def _adam_big(w, m, v, own, recv, name, dep=None):
    r, c = w.shape
    tr = _tile(r, (256, 176))

    def body(w_ref, m_ref, v_ref, own_ref, recv_ref, g_ref, d_ref, nm_ref, nv_ref):
        g = own_ref[...]
        for k in range(3):
            g = g + recv_ref[k].astype(F32)
        g_ref[...] = g
        d_ref[...], nm_ref[...], nv_ref[...] = _adamw(w_ref[...], g, m_ref[...], v_ref[...])

    blk = pl.BlockSpec((tr, c), lambda i: (i, 0))
    return _pallas(
        body,
        5,
        dep,
        name=name,
        grid=(r // tr,),
        in_specs=[blk, blk, blk, blk, pl.BlockSpec((3, tr, c), lambda i: (0, i, 0))],
        out_specs=[blk] * 4,
        out_shape=[jax.ShapeDtypeStruct((r, c), F32)] * 4,
        compiler_params=_params("parallel"),
    )(w, m, v, own, recv)


def _adam_small(w, g, m, v):
    def body(w_ref, g_ref, m_ref, v_ref, d_ref, nm_ref, nv_ref):
        d_ref[...], nm_ref[...], nv_ref[...] = _adamw(w_ref[...], g_ref[...], m_ref[...], v_ref[...])

    return pl.pallas_call(body, name="adam_small", out_shape=[jax.ShapeDtypeStruct(w.shape, F32)] * 3, compiler_params=_params())(w, g, m, v)


def _cast_bf16(w, name):
    r, c = w.shape
    tr = _tile(r, (256, 176))

    def body(w_ref, o_ref):
        o_ref[...] = w_ref[...].astype(BF16)

    blk = pl.BlockSpec((tr, c), lambda i: (i, 0))
    return pl.pallas_call(
        body, name=name, grid=(r // tr,), in_specs=[blk], out_specs=blk, out_shape=jax.ShapeDtypeStruct((r, c), BF16), compiler_params=_params("parallel")
    )(w)


def _position():
    return lax.axis_index("x"), lax.axis_index("y"), lax.axis_index("c")


def _other_chips(x, y):
    return [(1 - x, y), (x, 1 - y), (1 - x, 1 - y)]


def _block_index(dev, interleaved):
    x, y, c = dev
    return 4 * y + 2 * c + x if interleaved else 4 * x + 2 * y + c


def _shard_of(ref, axis, size, dev, interleaved=False):
    start = pl.multiple_of(_block_index(dev, interleaved) * size, 128 if axis == 1 else 16)
    return ref.at[:, pl.ds(start, size)] if axis == 1 else ref.at[pl.ds(start, size), :]


def _all_gather(shards, axes, interleaved):
    n = len(shards)

    def body(*refs):
        ins, outs = refs[:n], refs[n : 2 * n]
        send_sems, recv_sems, local_sems = refs[2 * n :]
        x, y, c = _position()
        me, sibling = (x, y, c), (x, y, 1 - c)
        chips = _other_chips(x, y)
        firsts, passed, locals_ = [], [], []
        for w in range(n):
            size = shards[w].shape[axes[w]]
            slot = functools.partial(_shard_of, outs[w], axes[w], size, interleaved=interleaved[w])

            def copy(k, block, to, src=None, w=w, slot=slot):
                return pltpu.make_async_remote_copy(
                    src_ref=slot(block) if src is None else src,
                    dst_ref=slot(block),
                    send_sem=send_sems.at[7 * w + k],
                    recv_sem=recv_sems.at[7 * w + k],
                    device_id=to,
                    device_id_type=MESH,
                )

            mine = pltpu.make_async_copy(ins[w], slot(me), local_sems.at[w])
            mine.start()
            locals_.append(mine)
            first = [copy(0, me, sibling, src=ins[w])] + [copy(1 + j, me, (*chip, c), src=ins[w]) for j, chip in enumerate(chips)]
            for cp in first:
                cp.start()
            firsts.append((first, copy))
        for w in range(n):
            first, copy = firsts[w]
            fwd = [copy(4 + j, (*chip, c), sibling) for j, chip in enumerate(chips)]
            for j, chip in enumerate(chips):
                copy(1 + j, (*chip, c), me).wait_recv()
                fwd[j].start()
            passed.append(fwd)
        for w in range(n):
            first, copy = firsts[w]
            copy(0, sibling, me).wait_recv()
            for j, chip in enumerate(chips):
                copy(4 + j, (*chip, 1 - c), me).wait_recv()
            for cp in first + passed[w]:
                cp.wait_send()
            locals_[w].wait()

    def full(s, ax):
        shape = list(s.shape)
        shape[ax] *= N_DEV
        return jax.ShapeDtypeStruct(tuple(shape), s.dtype)

    return pl.pallas_call(
        body,
        name="all_gather_weights",
        in_specs=[ANY] * n,
        out_specs=[ANY] * n,
        out_shape=[full(s, ax) for s, ax in zip(shards, axes)],
        scratch_shapes=[pltpu.SemaphoreType.DMA((7 * n,)), pltpu.SemaphoreType.DMA((7 * n,)), pltpu.SemaphoreType.DMA((n,))],
    )(*shards)


def _add_blocks(ids, grad, landed, axis, size, targets, out_dtype, name):
    rows = size if axis == 0 else grad.shape[0]
    cols = size if axis == 1 else grad.shape[1]
    tr = _tile(rows, (256, 176))
    nr = rows // tr
    nt = len(targets)

    def body(ids_ref, g_ref, l_ref, o_ref):
        o_ref[...] = (g_ref[...] + l_ref[...]).astype(out_dtype)

    if axis == 1:
        g_spec = pl.BlockSpec((tr, cols), lambda k, i, ids: (i, ids[targets[0] + k]))
    else:
        g_spec = pl.BlockSpec((tr, cols), lambda k, i, ids: (ids[targets[0] + k] * nr + i, 0))
    return pl.pallas_call(
        body,
        name=name,
        grid_spec=pltpu.PrefetchScalarGridSpec(
            num_scalar_prefetch=1,
            grid=(nt, nr),
            in_specs=[g_spec, pl.BlockSpec((None, tr, cols), lambda k, i, ids: (ids[4 + targets[0] + k], i, 0))],
            out_specs=pl.BlockSpec((None, tr, cols), lambda k, i, ids: (k, i, 0)),
        ),
        out_shape=jax.ShapeDtypeStruct((nt, rows, cols), out_dtype),
        compiler_params=_params("parallel", "parallel"),
    )(ids, grad, landed)


def _all_reduce_small(vec, name):
    rows = vec.shape[0]

    def body(v_ref, o_ref, land, send_sems, recv_sems):
        x, y, c = _position()
        mine = 4 * x + 2 * y + c
        copies = []
        for mask in range(1, N_DEV):
            peer = (1 - x if mask & 4 else x, 1 - y if mask & 2 else y, 1 - c if mask & 1 else c)
            copies.append(
                pltpu.make_async_remote_copy(
                    src_ref=v_ref, dst_ref=land.at[mine], send_sem=send_sems.at[mask - 1], recv_sem=recv_sems.at[mask - 1], device_id=peer, device_id_type=MESH
                )
            )
        for cp in copies:
            cp.start()
        land[mine] = v_ref[...]
        for cp in copies:
            cp.wait()
        acc = land[0]
        for k in range(1, N_DEV):
            acc = acc + land[k]
        o_ref[...] = acc

    return pl.pallas_call(
        body,
        name=name,
        out_shape=jax.ShapeDtypeStruct(vec.shape, F32),
        in_specs=[pl.BlockSpec(memory_space=pltpu.VMEM)],
        out_specs=pl.BlockSpec(memory_space=pltpu.VMEM),
        scratch_shapes=[pltpu.VMEM((N_DEV, rows, 128), F32), pltpu.SemaphoreType.DMA((N_DEV - 1,)), pltpu.SemaphoreType.DMA((N_DEV - 1,))],
    )(vec)


def _rows128(a, rows):
    flat = a.reshape(-1)
    return jnp.pad(flat, (0, rows * 128 - flat.shape[0])).reshape(rows, 128)


def _pad_rel(a):
    return jnp.pad(a.reshape(ATT_HEADS, -1)[:, :N_REL], ((0, 0), (0, N_REL_PAD - N_REL)))


SMALL = [("norm_mix", 16), ("lb_logits", 16), ("hg_norm", 8), ("rel_bias", 24), ("norm_ffn", 16), ("conv_b", 88), ("norm_ple", 16), ("final_norm", 16)]
CONV_W_FULL_ROWS = 3 * 2 * D_FF // 128
CONV_W_SHARD_ROWS = 40


def _pack_small(parts):
    return jnp.concatenate([_rows128(_pad_rel(parts[k]) if k == "rel_bias" else parts[k], rows) for k, rows in SMALL], axis=0)


def _unpack_small(packed, shapes):
    out, at = {}, 0
    for k, rows in SMALL:
        blk = packed[at : at + rows]
        at += rows
        if k == "rel_bias":
            out[k] = blk.reshape(ATT_HEADS, N_REL_PAD)[:, :N_REL].reshape(shapes[k])
        else:
            n = 1
            for s in shapes[k]:
                n *= s
            out[k] = blk.reshape(-1)[:n].reshape(shapes[k])
    return out, at


BIG = [("w_in", 1), ("w_out", 0), ("w_up", 1), ("w_down", 0), ("w_ple_gate", 0), ("w_ple_proj", 1)]


HBM = pl.BlockSpec(memory_space=pltpu.HBM)
SEM = pl.BlockSpec(memory_space=pltpu.SEMAPHORE)
EFFECT = pltpu.SideEffectType.DATAFLOW_SIDE_EFFECTING


def _copies(plan, refs, send_sems, recv_sems):
    return [
        pltpu.make_async_remote_copy(src_ref=src, dst_ref=dst, send_sem=send_sems.at[i], recv_sem=recv_sems.at[i], device_id=dev, device_id_type=MESH)
        for i, (src, dst, dev) in enumerate(plan(refs))
    ]


def _split_start(name, arrays, plan, n):
    k = len(arrays)

    def body(*refs):
        for cp in _copies(plan, refs[:k], refs[k], refs[k + 1]):
            cp.start()
        refs[-1][...] = jnp.zeros_like(refs[-1])

    out = pl.pallas_call(
        body,
        name=name,
        out_shape=(pltpu.SemaphoreType.DMA((n,)), pltpu.SemaphoreType.DMA((n,)), *[pltpu.HBM(a.shape, a.dtype) for a in arrays], jax.ShapeDtypeStruct((8, 128), F32)),
        in_specs=[HBM] * k,
        out_specs=(SEM, SEM, *[HBM] * k, pl.BlockSpec(memory_space=pltpu.VMEM)),
        input_output_aliases={i: 2 + i for i in range(k)},
        compiler_params=pltpu.CompilerParams(has_side_effects=EFFECT),
    )(*[pltpu.with_memory_space_constraint(a, pltpu.HBM) for a in arrays])
    return out[0], out[1], list(out[2 : 2 + k]), out[-1]


def _split_wait(name, send, recv, arrays, plan, after):
    k = len(arrays)

    def body(*refs):
        for cp in _copies(plan, refs[:k], refs[k], refs[k + 1]):
            cp.wait_send()
            cp.wait_recv()

    out = pl.pallas_call(
        body,
        name=name,
        out_shape=tuple(pltpu.HBM(a.shape, a.dtype) for a in arrays),
        in_specs=[HBM] * k + [SEM, SEM, ANY],
        out_specs=tuple([HBM] * k),
        input_output_aliases={i: i for i in range(k)},
        compiler_params=pltpu.CompilerParams(has_side_effects=EFFECT),
    )(*arrays, send, recv, after)
    return list(out)


def _cast_into(w, me, axis, name, dep):
    r, c = w.shape
    tr = _tile(r, (256, 176))
    nr = r // tr

    def body(me_ref, w_ref, dep_ref, o_ref):
        o_ref[...] = w_ref[...].astype(BF16)

    if axis == 1:
        shape, o_spec = (r, N_DEV * c), pl.BlockSpec((tr, c), lambda i, me: (i, me[0]))
    else:
        shape, o_spec = (N_DEV * r, c), pl.BlockSpec((tr, c), lambda i, me: (me[0] * nr + i, 0))
    return pl.pallas_call(
        body,
        name=name,
        grid_spec=pltpu.PrefetchScalarGridSpec(
            num_scalar_prefetch=1, grid=(nr,), in_specs=[pl.BlockSpec((tr, c), lambda i, me: (i, 0)), ANY], out_specs=o_spec
        ),
        out_shape=jax.ShapeDtypeStruct(shape, BF16),
        compiler_params=_params("parallel"),
    )(me, w, dep)


GATHER = [
    (["w_out"], None, "att_fwd", None),
    (["w_up"], None, "att_fwd", "norm_ffn_fwd"),
    (["w_down", "w_ple_gate", "w_ple_proj"], "att_fwd", "up_proj", "ffn_act_fwd"),
]
GROUPS = [["w_ple_proj", "w_ple_gate", "w_down"], ["w_up"], ["w_out"], ["w_in"]]
STAGES = ["ffn_act_bwd", "d_mix_out", "hgrn_bwd", "d_norm_mix_out"]
INTERLEAVED = {"w_up"}


class _Exchange:
    def __init__(self, big, position):
        self.big, self.axis = big, dict(BIG)
        self.size = {k: big[k].shape[self.axis[k]] for k in big}
        self.x, self.y, self.c = position
        chips = [(self.x, self.y)] + _other_chips(self.x, self.y)
        landed = [2 * cx + cy for cx, cy in chips]
        self.ids = {
            flag: jnp.stack([_block_index((cx, cy, self.c), flag) for cx, cy in chips] + landed).astype(jnp.int32) for flag in (False, True)
        }
        self.tokens, self.grads, self.state, self.wfull = [], {}, {}, {}


    def _slot(self, ref, k, dev):
        return _shard_of(ref, self.axis[k], self.size[k], dev, interleaved=k in INTERLEAVED)

    def _plan_gather(self, names, direct, refs):
        x, y, c = _position()
        me, out = (x, y, c), []
        for k, ref in zip(names, refs):
            mine = self._slot(ref, k, me)
            out.append((mine, mine, (x, y, 1 - c)))
            out += [(mine, mine, (*chip, c)) for chip in _other_chips(x, y)]
            if direct:
                out += [(mine, mine, (*chip, 1 - c)) for chip in _other_chips(x, y)]
        return out

    def _plan_forward(self, names, refs):
        x, y, c = _position()
        out = []
        for k, ref in zip(names, refs):
            for chip in _other_chips(x, y):
                block = self._slot(ref, k, (*chip, c))
                out.append((block, block, (x, y, 1 - c)))
        return out

    def _plan_sibling(self, names, refs):
        x, y, c = _position()
        n = len(names)
        return [(self._slot(refs[i], k, (p // 2, p % 2, 1 - c)), refs[n + i].at[p], (x, y, 1 - c)) for i, k in enumerate(names) for p in range(4)]

    def _plan_chips(self, names, refs):
        x, y, c = _position()
        n = len(names)
        return [(refs[i].at[j], refs[n + i].at[j], (*chip, c)) for i in range(n) for j, chip in enumerate(_other_chips(x, y))]


    def gather(self, conv_w):
        w_in, conv_full = _all_gather([_cast_bf16(self.big["w_in"], "cast_w_in"), conv_w], [1, 1], [False, True])
        self.wfull["w_in"] = w_in
        me = {flag: _block_index((self.x, self.y, self.c), flag).astype(jnp.int32).reshape(1) for flag in (False, True)}
        self.late, self.unsent = {}, {}
        after = w_in
        for gi, (names, issued, *_) in enumerate(GATHER):
            self.unsent[gi] = [_cast_into(self.big[k], me[k in INTERLEAVED], self.axis[k], "cast_" + k, after) for k in names]
            if issued is None:
                self._issue(None)
                after = self.tokens[-1]
        return conv_full

    def _issue(self, stage):
        for gi, (names, issued, _, forwarded) in enumerate(GATHER):
            if issued == stage and gi in self.unsent:
                plan = functools.partial(self._plan_gather, names, forwarded is None)
                copies = (7 if forwarded is None else 4) * len(names)
                send, recv, fulls, token = _split_start(f"gather_start_{gi}", self.unsent.pop(gi), plan, copies)
                self.tokens.append(token)
                self.late[gi] = (send, recv, fulls, plan)

    def weight(self, k):
        return self.wfull[k]

    def dep(self):
        tokens, self.tokens = self.tokens, []
        return tokens

    def reduce(self, vec, name):
        return _all_reduce_small(vec, name)

    def grad(self, k, g):
        self.grads[k] = g
        for gi, names in enumerate(GROUPS):
            if k == names[-1]:
                plan = functools.partial(self._plan_sibling, names)
                lands = [lax.empty((4, *self._shard_shape(n)), F32) for n in names]
                send, recv, arrays, token = _split_start(f"sibling_start_{gi}", [self.grads[n] for n in names] + lands, plan, 4 * len(names))
                self.tokens.append(token)
                self.state[gi] = (send, recv, arrays, plan)

    def done(self, stage, after):
        for gi, (names, _, _, forwarded) in enumerate(GATHER):
            if forwarded == stage:
                send, recv, fulls, plan = self.late[gi]
                self.wfull.update(zip(names, _split_wait(f"forward_wait_{gi}", send, recv, fulls, plan, after)))
        for gi, (names, _, arrived, forwarded) in enumerate(GATHER):
            if arrived == stage:
                send, recv, fulls, plan = self.late[gi]
                fulls = _split_wait(f"gather_wait_{gi}", send, recv, fulls, plan, after)
                if forwarded is None:
                    self.wfull.update(zip(names, fulls))
                else:
                    plan = functools.partial(self._plan_forward, names)
                    send, recv, fulls, token = _split_start(f"forward_start_{gi}", fulls, plan, 3 * len(names))
                    self.tokens.append(token)
                    self.late[gi] = (send, recv, fulls, plan)
        self._issue(stage)
        if stage in STAGES:
            self._to_chips(STAGES.index(stage), after)

    def _shard_shape(self, k):
        shape = list(self.grads[k].shape)
        shape[self.axis[k]] = self.size[k]
        return tuple(shape)

    def _to_chips(self, gi, after):
        names = GROUPS[gi]
        n = len(names)
        send, recv, arrays, plan = self.state[gi]
        arrays = _split_wait(f"sibling_wait_{gi}", send, recv, arrays, plan, after)
        own, parts = [], []
        for k, g, land in zip(names, arrays[:n], arrays[n:]):
            ids = self.ids[k in INTERLEAVED]
            own.append(_add_blocks(ids, g, land, self.axis[k], self.size[k], [0], F32, "add_own_" + k)[0])
            parts.append(_add_blocks(ids, g, land, self.axis[k], self.size[k], [1, 2, 3], BF16, "add_send_" + k))
        plan = functools.partial(self._plan_chips, names)
        lands = [lax.empty(part.shape, BF16) for part in parts]
        send, recv, arrays, token = _split_start(f"chips_start_{gi}", parts + lands, plan, 3 * n)
        self.tokens.append(token)
        self.state[gi] = (send, recv, arrays, plan, own)

    def finish(self, gi, after):
        names = GROUPS[gi]
        send, recv, arrays, plan, own = self.state[gi]
        arrays = _split_wait(f"chips_wait_{gi}", send, recv, arrays, plan, after)
        return {k: (o, r) for k, o, r in zip(names, own, arrays[len(names) :])}


class _Resident:
    def __init__(self, wfull):
        self.wfull, self.grads = wfull, {}

    def weight(self, k):
        return self.wfull[k]

    def grad(self, k, g):
        self.grads[k] = g

    def dep(self):
        return None

    def reduce(self, vec, name):
        return vec

    def done(self, stage, after):
        pass


def _local_step(x, p, target, small, conv_w, ex):
    a1, r1 = _rms_fwd(x, small["norm_mix"], "norm_mix_fwd", dep=ex.dep())
    proj = _matmul(a1, ex.weight("w_in"), "nn", F32, "in_proj")
    bias = _bias_table(jnp.pad(small["rel_bias"], ((0, 0), (0, N_REL_PAD - N_REL))))
    y_hg, o_hg, states = _hgrn_fwd(proj, small["lb_logits"], small["hg_norm"])
    y_att = _att_fwd(proj, bias, dep=ex.dep())
    ex.done("att_fwd", y_att)
    ycat = lax.dynamic_update_slice(y_hg, y_att, (0, HG_WIDTH))
    h1 = _matmul(ycat, ex.weight("w_out"), "nn", F32, "out_proj", resid=x, dep=ex.dep())
    a2, r2 = _rms_fwd(h1, small["norm_ffn"], "norm_ffn_fwd")
    ex.done("norm_ffn_fwd", a2)
    u = _matmul(a2, ex.weight("w_up"), "nn", BF16, "up_proj")
    conv_b = _interleave_cols(small["conv_b"])
    ex.done("up_proj", u)
    z = _ffn_act_fwd(u, conv_w, conv_b)
    ex.done("ffn_act_fwd", z)
    h2 = _matmul(z, ex.weight("w_down"), "nn", F32, "down_proj", tk=2816, resid=h1)
    a3, r3 = _rms_fwd(h2, small["norm_ple"], "norm_ple_fwd")
    gpre = _matmul(a3, ex.weight("w_ple_gate"), "nn", F32, "ple_gate")
    pp = _matmul(p, ex.weight("w_ple_proj"), "nn", F32, "ple_proj")
    dh3, dgpre, dpp, d_final, loss = _ple_loss(gpre, pp, h2, small["final_norm"], target)

    ex.grad("w_ple_proj", _matmul(p, dpp, "tn", F32, "d_w_ple_proj", tm=512))
    ex.grad("w_ple_gate", _matmul(a3, dgpre, "tn", F32, "d_w_ple_gate", tm=512))
    da3 = _matmul(dgpre, ex.weight("w_ple_gate"), "nt", F32, "d_norm_ple_out")
    dh2, d_ple = _rms_bwd(da3, h2, r3, small["norm_ple"], dh3, "norm_ple_bwd")
    dz = _matmul(dh2, ex.weight("w_down"), "nt", BF16, "d_ffn_act")
    ex.grad("w_down", _matmul(z, dh2, "tn", F32, "d_w_down", tm=512))
    du, dcw, dcb = _ffn_act_bwd(u, dz, conv_w, conv_b, dep=ex.dep())
    ex.done("ffn_act_bwd", du)
    d_conv_w, d_conv_b = _deinterleave_cols(dcw), _deinterleave_cols(dcb)
    ex.grad("w_up", _matmul(a2, du, "tn", F32, "d_w_up", tm=512, dep=ex.dep()))
    da2 = _matmul(du, ex.weight("w_up"), "nt", F32, "d_norm_ffn_out", tk=2816, dep=ex.dep())
    dh1, d_ffn = _rms_bwd(da2, h1, r2, small["norm_ffn"], dh2, "norm_ffn_bwd")
    dycat = _matmul(dh1, ex.weight("w_out"), "nt", F32, "d_mix_out")
    ex.done("d_mix_out", dycat)
    ex.grad("w_out", _matmul(ycat, dh1, "tn", F32, "d_w_out", tm=512, dep=ex.dep()))
    dp_hg, d_lb, d_hgn = _hgrn_bwd(proj, small["lb_logits"], small["hg_norm"], o_hg, dycat, states, dep=ex.dep())
    ex.done("hgrn_bwd", d_lb)
    dq_att, dk_att, dv_att, gsum = _att_bwd(proj, bias, dycat, dep=ex.dep())
    d_rel = _rel_bias_grad(gsum)
    d_small = {
        "norm_mix": jnp.zeros_like(small["norm_mix"]), "lb_logits": d_lb, "hg_norm": d_hgn, "rel_bias": d_rel, "norm_ffn": d_ffn,
        "conv_b": d_conv_b, "norm_ple": d_ple, "final_norm": d_final,
    }
    packed = jnp.concatenate([_pack_small(d_small), _rows128(d_conv_w, CONV_W_FULL_ROWS), _rows128(loss[0:1, 0:1], 8)], axis=0)
    early = ex.reduce(packed, "all_reduce_small")
    dproj = dp_hg
    for k, part in enumerate((dq_att, dk_att, dv_att)):
        dproj = lax.dynamic_update_slice(dproj, part, (0, 4 * HG_WIDTH + k * ATT_WIDTH))
    ex.grad("w_in", _matmul(a1, dproj, "tn", F32, "d_w_in", tm=512, dep=[early]))
    da1 = _matmul(dproj, ex.weight("w_in"), "nt", F32, "d_norm_mix_out", tk=1792, dep=ex.dep())
    dx, d_mix = _rms_bwd(da1, x, r1, small["norm_mix"], dh1, "norm_mix_bwd")
    rows = dict(SMALL)["norm_mix"]
    late = ex.reduce(_rows128(d_mix, rows), "all_reduce_norm_mix")
    ex.done("d_norm_mix_out", late)
    return dx, jnp.concatenate([late, early[rows:]], axis=0)


def kernel(x, p, norm_mix, w_in, lb_logits, hg_norm, rel_bias, w_out, norm_ffn, w_up, conv_w, conv_b, w_down, norm_ple, w_ple_gate, w_ple_proj, final_norm, loss_target, m_norm_mix, m_w_in, m_lb_logits, m_hg_norm, m_rel_bias, m_w_out, m_norm_ffn, m_w_up, m_conv_w, m_conv_b, m_w_down, m_norm_ple, m_w_ple_gate, m_w_ple_proj, m_final_norm, v_norm_mix, v_w_in, v_lb_logits, v_hg_norm, v_rel_bias, v_w_out, v_norm_ffn, v_w_up, v_conv_w, v_conv_b, v_w_down, v_norm_ple, v_w_ple_gate, v_w_ple_proj, v_final_norm):
    given = dict(locals())
    mx, my, mc = _position()
    me = 4 * mx + 2 * my + mc
    big = {k: given[k][0] for k, _ in BIG}
    ex = _Exchange(big, (mx, my, mc))
    conv_w_full = ex.gather(conv_w[0])

    small = {
        "norm_mix": norm_mix, "lb_logits": lb_logits, "hg_norm": hg_norm, "rel_bias": rel_bias[0], "norm_ffn": norm_ffn,
        "conv_b": conv_b, "norm_ple": norm_ple, "final_norm": final_norm.reshape(1, -1),
    }
    dx, reduced = _local_step(x[0], p[0, 0], loss_target[0], small, conv_w_full, ex)

    out = {}
    shapes = {k: given[k].shape for k, _ in SMALL}
    g_small, at = _unpack_small(reduced, shapes)
    g_conv_full = reduced[at : at + CONV_W_FULL_ROWS].reshape(3, 2 * D_FF)
    total_loss = reduced[at + CONV_W_FULL_ROWS, 0]
    cw = conv_w.shape[2]
    g_conv = lax.dynamic_slice_in_dim(g_conv_full, me * cw, cw, axis=1)

    def pack_with_conv(parts, conv_part):
        return jnp.concatenate([_pack_small(parts), _rows128(conv_part, CONV_W_SHARD_ROWS)], axis=0)

    d_pk, m_pk, v_pk = _adam_small(
        pack_with_conv({k: given[k] for k, _ in SMALL}, conv_w),
        pack_with_conv(g_small, g_conv),
        pack_with_conv({k: given["m_" + k] for k, _ in SMALL}, m_conv_w),
        pack_with_conv({k: given["v_" + k] for k, _ in SMALL}, v_conv_w),
    )
    for name, pk in (("d", d_pk), ("m", m_pk), ("v", v_pk)):
        parts, at = _unpack_small(pk, shapes)
        parts["conv_w"] = pk[at : at + CONV_W_SHARD_ROWS].reshape(-1)[: 3 * cw].reshape(conv_w.shape)
        for k, a in parts.items():
            out.setdefault(k, {})
            out[k][name] = a
    for k, _ in SMALL:
        out[k]["g"] = g_small[k]
    out["conv_w"]["g"] = g_conv.reshape(conv_w.shape)

    after, started = v_pk, ex.dep()
    for gi in range(len(GROUPS)):
        for k, (o, r) in ex.finish(gi, after).items():
            g, d, nm, nv = _adam_big(big[k], given["m_" + k][0], given["v_" + k][0], o, r, "adam_" + k, dep=started)
            out[k] = tuple(a[None] for a in (g, d, nm, nv))
            after = nv

    order = ["norm_mix", "w_in", "lb_logits", "hg_norm", "rel_bias", "w_out", "norm_ffn", "w_up", "conv_w", "conv_b", "w_down", "norm_ple", "w_ple_gate", "w_ple_proj", "final_norm"]

    def pick(k, what):
        return out[k][what] if isinstance(out[k], dict) else out[k][{"g": 0, "d": 1, "m": 2, "v": 3}[what]]

    return (total_loss, dx[None], *[pick(k, "g") for k in order], *[pick(k, "d") for k in order], *[pick(k, "m") for k in order], *[pick(k, "v") for k in order])
```

```python
import functools

import jax
import jax.numpy as jnp
from jax import lax
from jax.experimental import pallas as pl
from jax.experimental.pallas import tpu as pltpu

F32 = jnp.float32
BF16 = jnp.bfloat16

D_MODEL = 2048
CHUNK = 64
HG_HEADS = 8
HEAD_DIM = 128
HG_WIDTH = HG_HEADS * HEAD_DIM
ATT_HEADS = 8
ATT_WIDTH = ATT_HEADS * HEAD_DIM
LEFT_CHUNKS = 8
PAD = LEFT_CHUNKS * CHUNK
BAND = PAD + CHUNK
REL_CLIP = 128
N_REL = 2 * REL_CLIP + 1
N_REL_PAD = 384
D_FF = 5632
EPS = 1e-6
ATT_SCALE = HEAD_DIM ** -0.5
SUB = 32
HG_BLOCK = 8
Q_BLOCK = 4 * CHUNK
K_BLOCK = Q_BLOCK + PAD
DIAG = 1024
MASKED = -1e30

ADAM_LR = 0.001
ADAM_B1 = 0.9
ADAM_B2 = 0.999
ADAM_EPS = 1e-08
ADAM_WD = 0.01
ADAM_STEP = 10

N_DEV = 8
VMEM_LIMIT = 48 * 1024 * 1024
MESH = pl.DeviceIdType.MESH
ANY = pl.BlockSpec(memory_space=pl.ANY)
HIGHEST = lax.Precision.HIGHEST

NN = (((1,), (0,)), ((), ()))
NT = (((1,), (1,)), ((), ()))
TN = (((0,), (0,)), ((), ()))


def _params(*sem):
    return pltpu.CompilerParams(dimension_semantics=sem if sem else None, vmem_limit_bytes=VMEM_LIMIT)


def _pallas(body, n_in, dep, **kw):
    deps = [] if dep is None else list(dep)
    if not deps:
        return pl.pallas_call(body, **kw)

    def body_after(*refs):
        body(*refs[:n_in], *refs[n_in + len(deps) :])

    call = pl.pallas_call(body_after, **dict(kw, in_specs=list(kw["in_specs"]) + [ANY] * len(deps)))
    return lambda *ops: call(*ops, *deps)


def _dot(a, b, dims=NN):
    return lax.dot_general(a, b, dims, preferred_element_type=F32)


def _dot3(a, b, dims=NN):
    a_hi, b_hi = a.astype(BF16), b.astype(BF16)
    a_lo, b_lo = (a - a_hi.astype(F32)).astype(BF16), (b - b_hi.astype(F32)).astype(BF16)
    return _dot(a_hi, b_hi, dims) + (_dot(a_hi, b_lo, dims) + _dot(a_lo, b_hi, dims))


def _sigmoid(x):
    return 1.0 / (1.0 + jnp.exp(-x))


def _tile(n, prefs):
    for t in prefs:
        if n % t == 0:
            return t
    return n


def _matmul(a, b, mode, out_dtype, name, tm=1024, tn=1024, tk=None, resid=None, dep=None):
    if mode == "nn":
        (m, k), n = a.shape, b.shape[1]
    elif mode == "nt":
        (m, k), n = a.shape, b.shape[0]
    else:
        (k, m), n = a.shape, b.shape[1]
    tm = _tile(m, (tm, 512, 256, 128))
    tn = _tile(n, (tn, 1408, 512, 256, 128))
    tk = k if tk is None else _tile(k, (tk,))
    nk = k // tk
    dims = {"nn": NN, "nt": NT, "tn": TN}[mode]
    a_spec = pl.BlockSpec((tk, tm), lambda i, j, s: (s, i)) if mode == "tn" else pl.BlockSpec((tm, tk), lambda i, j, s: (i, s))
    b_spec = pl.BlockSpec((tn, tk), lambda i, j, s: (j, s)) if mode == "nt" else pl.BlockSpec((tk, tn), lambda i, j, s: (s, j))
    o_spec = pl.BlockSpec((tm, tn), lambda i, j, s: (i, j))
    has_res = resid is not None

    def body(*refs):
        a_ref, b_ref = refs[0], refs[1]
        o_ref = refs[2 + has_res]
        part = _dot(a_ref[...].astype(BF16), b_ref[...].astype(BF16), dims)

        def finish(acc):
            if has_res:
                acc = acc + refs[2][...]
            o_ref[...] = acc.astype(out_dtype)

        if nk == 1:
            finish(part)
        else:
            acc_ref = refs[-1]
            s = pl.program_id(2)

            @pl.when(s == 0)
            def _():
                acc_ref[...] = part

            @pl.when(s > 0)
            def _():
                acc_ref[...] += part

            @pl.when(s == nk - 1)
            def _():
                finish(acc_ref[...])

    return _pallas(
        body,
        2 + has_res,
        dep,
        name=name,
        grid=(m // tm, n // tn, nk),
        in_specs=[a_spec, b_spec] + ([o_spec] if has_res else []),
        out_specs=o_spec,
        out_shape=jax.ShapeDtypeStruct((m, n), out_dtype),
        scratch_shapes=[pltpu.VMEM((tm, tn), F32)] if nk > 1 else [],
        compiler_params=_params("parallel", "parallel", "arbitrary"),
    )(*([a, b] + ([resid] if has_res else [])))


def _rms_fwd(x, g, name, dep=None):
    t, d = x.shape
    tm = _tile(t, (256,))

    def body(x_ref, g_ref, a_ref, r_ref):
        xv = x_ref[...]
        r = lax.rsqrt(jnp.mean(xv * xv, axis=-1, keepdims=True) + EPS)
        a_ref[...] = (xv * r * g_ref[...]).astype(BF16)
        r_ref[...] = r

    row = pl.BlockSpec((tm, d), lambda i: (i, 0))
    return _pallas(
        body,
        2,
        dep,
        name=name,
        grid=(t // tm,),
        in_specs=[row, pl.BlockSpec((1, d), lambda i: (0, 0))],
        out_specs=[row, pl.BlockSpec((tm, 1), lambda i: (i, 0))],
        out_shape=[jax.ShapeDtypeStruct((t, d), BF16), jax.ShapeDtypeStruct((t, 1), F32)],
        compiler_params=_params("parallel"),
    )(x, g)


def _rms_bwd(da, x, r, g, resid, name, dep=None):
    t, d = x.shape
    tm = _tile(t, (256,))

    def body(da_ref, x_ref, r_ref, g_ref, res_ref, dx_ref, dg_ref):
        i = pl.program_id(0)
        rv = r_ref[...]
        n = x_ref[...] * rv
        dav = da_ref[...]
        dn = dav * g_ref[...]
        dx_ref[...] = rv * (dn - n * jnp.mean(dn * n, axis=-1, keepdims=True)) + res_ref[...]
        part = jnp.sum(dav * n, axis=0, keepdims=True)

        @pl.when(i == 0)
        def _():
            dg_ref[...] = part

        @pl.when(i > 0)
        def _():
            dg_ref[...] += part

    row = pl.BlockSpec((tm, d), lambda i: (i, 0))
    vec = pl.BlockSpec((1, d), lambda i: (0, 0))
    return _pallas(
        body,
        5,
        dep,
        name=name,
        grid=(t // tm,),
        in_specs=[row, row, pl.BlockSpec((tm, 1), lambda i: (i, 0)), vec, row],
        out_specs=[row, vec],
        out_shape=[jax.ShapeDtypeStruct((t, d), F32), jax.ShapeDtypeStruct((1, d), F32)],
        compiler_params=_params("arbitrary"),
    )(da, x, r, g, resid)


def _tri(n, upper):
    r = lax.broadcasted_iota(jnp.int32, (n, n), 0)
    c = lax.broadcasted_iota(jnp.int32, (n, n), 1)
    return jnp.where((c >= r) if upper else (c <= r), 1.0, 0.0).astype(F32)


def _hgrn_gates(q, fp, lbl):
    l0, l1 = lbl[0:1, :], lbl[1:2, :]
    mx = jnp.maximum(l0, l1)
    e0, e1 = jnp.exp(l0 - mx), jnp.exp(l1 - mx)
    lb = e0 / (e0 + e1)
    sig = _sigmoid(fp)
    f = lb + (1.0 - lb) * sig
    kk = (1.0 - lb) * _sigmoid(-fp)
    sq = _sigmoid(q)
    b = jnp.dot(_tri(CHUNK, False), jnp.log(f), precision=HIGHEST, preferred_element_type=F32)
    return lb, sig, f, kk, sq, q * sq, b


def _heads(x):
    return [x[:, j * HEAD_DIM : (j + 1) * HEAD_DIM] for j in range(x.shape[1] // HEAD_DIM)]


def _wide(parts):
    return jnp.concatenate(parts, axis=1)


def _intra_blocks(b):
    out = []
    for lo in range(0, CHUNK, SUB):
        hi = lo + SUB
        br = b[lo + SUB // 2 : lo + SUB // 2 + 1, :]
        row = lax.broadcasted_iota(jnp.int32, (SUB, hi), 0) + lo
        col = lax.broadcasted_iota(jnp.int32, (SUB, hi), 1)
        out.append((lo, hi, jnp.exp(b[lo:hi] - br), jnp.exp(br - b[:hi]), col <= row))
    return out


def _hgrn_fwd(proj, lb_logits, hg_norm):
    t = proj.shape[0]
    nc = t // CHUNK

    def body(q_ref, f_ref, i_ref, g_ref, lbl_ref, hgn_ref, y_ref, o_ref, st_ref, s_scr):
        c = pl.program_id(1)

        @pl.when(c == 0)
        def _():
            s_scr[...] = jnp.zeros_like(s_scr)

        hs = range(HG_BLOCK)
        sts = [s_scr[j] for j in hs]
        _, _, _, kk, _, qf, b = _hgrn_gates(q_ref[...], f_ref[...], lbl_ref[...])
        vb = _heads(i_ref[...].astype(BF16))
        bl = b[CHUNK - 1 : CHUNK, :]
        qe = _heads((qf * jnp.exp(b)).astype(BF16))
        kd = _heads((kk * jnp.exp(bl - b)).astype(BF16))
        decay = _heads(jnp.exp(bl))
        o = [_dot(qe[j], sts[j].astype(BF16), NT) for j in hs]
        parts = [[] for _ in hs]
        for lo, hi, ea, eb, mask in _intra_blocks(b):
            a, bk = _heads((qf[lo:hi] * ea).astype(BF16)), _heads((kk[:hi] * eb).astype(BF16))
            p = [jnp.where(mask, _dot(a[j], bk[j], NT), 0.0).astype(BF16) for j in hs]
            for j in hs:
                parts[j].append(_dot(p[j], vb[j][:hi]))
        o = [o[j] + jnp.concatenate(parts[j], axis=0) for j in hs]
        new = [sts[j] * decay[j] + _dot(vb[j], kd[j], TN) for j in hs]
        hgn = hgn_ref[...]
        on = [o[j] * lax.rsqrt(jnp.mean(o[j] * o[j], axis=-1, keepdims=True) + EPS) * hgn for j in hs]
        gg = g_ref[...]
        for j in hs:
            st_ref[j] = sts[j]
            s_scr[j] = new[j]
        o_ref[...] = _wide(o)
        y_ref[...] = (_wide(on) * (gg * _sigmoid(gg))).astype(BF16)

    wide = HG_BLOCK * HEAD_DIM
    groups = HG_HEADS // HG_BLOCK

    def col(k):
        return pl.BlockSpec((CHUNK, wide), lambda g, c: (c, k * groups + g))

    out = pl.BlockSpec((CHUNK, wide), lambda g, c: (c, g))
    return pl.pallas_call(
        body,
        name="hgrn_fwd",
        grid=(groups, nc),
        in_specs=[col(0), col(1), col(2), col(3), pl.BlockSpec((2, wide), lambda g, c: (0, g)), pl.BlockSpec((1, HEAD_DIM), lambda g, c: (0, 0))],
        out_specs=[out, out, pl.BlockSpec((HG_BLOCK, None, HEAD_DIM, HEAD_DIM), lambda g, c: (g, c, 0, 0))],
        out_shape=[
            jax.ShapeDtypeStruct((t, HG_WIDTH + ATT_WIDTH), BF16),
            jax.ShapeDtypeStruct((t, HG_WIDTH), F32),
            jax.ShapeDtypeStruct((HG_HEADS, nc, HEAD_DIM, HEAD_DIM), F32),
        ],
        scratch_shapes=[pltpu.VMEM((HG_BLOCK, HEAD_DIM, HEAD_DIM), F32)],
        compiler_params=_params("arbitrary", "arbitrary"),
    )(proj, proj, proj, proj, lb_logits, hg_norm)


def _hgrn_bwd(proj, lb_logits, hg_norm, o_hg, dycat, states, dep=None):
    t = proj.shape[0]
    nc = t // CHUNK

    def body(q_ref, f_ref, i_ref, g_ref, lbl_ref, hgn_ref, o_ref, dy_ref, st_ref, dp_ref, dlbl_ref, dhgn_ref, dst_scr, dlb_scr):
        h = pl.program_id(0)
        c = pl.program_id(1)

        @pl.when(c == 0)
        def _():
            dst_scr[...] = jnp.zeros_like(dst_scr)
            dlb_scr[...] = jnp.zeros_like(dlb_scr)

        @pl.when((c == 0) & (h == 0))
        def _():
            dhgn_ref[...] = jnp.zeros_like(dhgn_ref)

        hs = range(HG_BLOCK)
        hgn = _wide([hgn_ref[...]] * HG_BLOCK)
        q, fp, gg, vi = q_ref[...], f_ref[...], g_ref[...], i_ref[...]
        lb, sig, f, kk, sq, qf, b = _hgrn_gates(q, fp, lbl_ref[...])
        o, dy = o_ref[...], dy_ref[...]
        sg = _sigmoid(gg)
        n = _wide([oh * lax.rsqrt(jnp.mean(oh * oh, axis=-1, keepdims=True) + EPS) for oh in _heads(o)])
        don = dy * (gg * sg)
        dgg = dy * (n * hgn) * (sg * (1.0 + gg * (1.0 - sg)))
        d_hgn = sum(_heads(jnp.sum(don * n, axis=0, keepdims=True)))
        dn = don * hgn
        do = _wide(
            [
                lax.rsqrt(jnp.mean(oh * oh, axis=-1, keepdims=True) + EPS) * (dnh - nh * jnp.mean(dnh * nh, axis=-1, keepdims=True))
                for oh, dnh, nh in zip(_heads(o), _heads(dn), _heads(n))
            ]
        )
        sts = [st_ref[j] for j in hs]
        dstn = [dst_scr[j] for j in hs]
        bl = b[CHUNK - 1 : CHUNK, :]
        e_b, e_bl, e_l = jnp.exp(b), jnp.exp(bl - b), jnp.exp(bl)
        doh, vih = _heads(do), _heads(vi)
        dobh = _heads(do.astype(BF16))
        dq_acc = _wide([_dot3(doh[j], sts[j]) for j in hs]) * e_b
        dk_inter = _wide([_dot3(vih[j], dstn[j]) for j in hs]) * e_bl
        dk_acc = dk_inter
        kd = _heads((kk * e_bl).astype(BF16))
        dv_acc = _wide([_dot(kd[j], dstn[j].astype(BF16), NT) for j in hs])
        qe, decay = _heads((qf * e_b).astype(BF16)), _heads(e_l)
        dst_new = [dstn[j] * decay[j] + _dot(dobh[j], qe[j], TN) for j in hs]
        db_last = e_l * _wide([jnp.sum(sts[j] * dstn[j], axis=0, keepdims=True) for j in hs]) + jnp.sum(kk * dk_inter, axis=0, keepdims=True)
        dq_parts = []
        for lo, hi, ea, eb, mask in _intra_blocks(b):
            a, bk = qf[lo:hi] * ea, kk[:hi] * eb
            ah, bkh = _heads(a), _heads(bk)
            abh, bkbh = _heads(a.astype(BF16)), _heads(bk.astype(BF16))
            p = [jnp.where(mask, _dot(abh[j], bkbh[j], NT), 0.0).astype(BF16) for j in hs]
            dp = [jnp.where(mask, _dot3(doh[j][lo:hi], vih[j][:hi], NT), 0.0) for j in hs]
            dq_parts.append(_wide([_dot3(dp[j], bkh[j]) for j in hs]) * ea)
            dki = _wide([_dot3(dp[j], ah[j], TN) for j in hs]) * eb
            dvi = _wide([_dot(p[j], dobh[j][lo:hi], TN) for j in hs])
            if hi < CHUNK:
                zeros = jnp.zeros((CHUNK - hi, HG_BLOCK * HEAD_DIM), F32)
                dki = jnp.concatenate([dki, zeros], axis=0)
                dvi = jnp.concatenate([dvi, zeros], axis=0)
            dk_acc = dk_acc + dki
            dv_acc = dv_acc + dvi
        dq_acc = dq_acc + jnp.concatenate(dq_parts, axis=0)
        rows = lax.broadcasted_iota(jnp.int32, dq_acc.shape, 0)
        db = qf * dq_acc - kk * dk_acc + jnp.where(rows == CHUNK - 1, db_last, 0.0)
        dlf = jnp.dot(_tri(CHUNK, True), db, precision=HIGHEST, preferred_element_type=F32)
        dfk = dlf / f - dk_acc
        for k, part in enumerate((dq_acc * (sq * (1.0 + q * (1.0 - sq))), (1.0 - lb) * dfk * sig * (1.0 - sig), dv_acc, dgg)):
            dp_ref[:, k * HG_WIDTH : (k + 1) * HG_WIDTH] = part.astype(BF16)
        dlb_scr[...] += jnp.sum(dfk * (1.0 - sig), axis=0, keepdims=True)
        dhgn_ref[...] += d_hgn
        for j in hs:
            dst_scr[j] = dst_new[j]

        @pl.when(c == nc - 1)
        def _():
            dl0 = dlb_scr[...] * lb * (1.0 - lb)
            dlbl_ref[0:1, :] = dl0
            dlbl_ref[1:2, :] = -dl0

    wide = HG_BLOCK * HEAD_DIM
    groups = HG_HEADS // HG_BLOCK

    def col(k):
        return pl.BlockSpec((CHUNK, wide), lambda g, c: (nc - 1 - c, k * groups + g))

    blk = pl.BlockSpec((CHUNK, wide), lambda g, c: (nc - 1 - c, g))
    assert groups == 1, "d(q, f, i, g) are written as one contiguous column range of the in_proj gradient"
    return _pallas(
        body,
        9,
        dep,
        name="hgrn_bwd",
        grid=(groups, nc),
        in_specs=[
            col(0), col(1), col(2), col(3),
            pl.BlockSpec((2, wide), lambda g, c: (0, g)),
            pl.BlockSpec((1, HEAD_DIM), lambda g, c: (0, 0)),
            blk, blk,
            pl.BlockSpec((HG_BLOCK, None, HEAD_DIM, HEAD_DIM), lambda g, c: (g, nc - 1 - c, 0, 0)),
        ],
        out_specs=[
            pl.BlockSpec((CHUNK, 4 * HG_WIDTH), lambda g, c: (nc - 1 - c, 0)),
            pl.BlockSpec((2, wide), lambda g, c: (0, g)),
            pl.BlockSpec((1, HEAD_DIM), lambda g, c: (0, 0)),
        ],
        out_shape=[
            jax.ShapeDtypeStruct((t, 4 * HG_WIDTH + 3 * ATT_WIDTH), BF16),
            jax.ShapeDtypeStruct((2, HG_WIDTH), F32),
            jax.ShapeDtypeStruct((1, HEAD_DIM), F32),
        ],
        scratch_shapes=[pltpu.VMEM((HG_BLOCK, HEAD_DIM, HEAD_DIM), F32), pltpu.VMEM((1, wide), F32)],
        compiler_params=_params("arbitrary", "arbitrary"),
    )(proj, proj, proj, proj, lb_logits, hg_norm, o_hg, dycat, states)


def _diagonal_slots(shift):
    i = lax.broadcasted_iota(jnp.int32, (N_REL_PAD, DIAG), 0)
    u = lax.broadcasted_iota(jnp.int32, (N_REL_PAD, DIAG), 1)
    offset = u - shift if shift else jnp.where(u < K_BLOCK, u, u - DIAG)
    return jnp.where(jnp.clip(PAD - offset, -REL_CLIP, REL_CLIP) + REL_CLIP == i, 1.0, 0.0).astype(BF16)


def _split3(x):
    hi = x.astype(BF16)
    mid = (x - hi.astype(F32)).astype(BF16)
    return hi, mid, (x - hi.astype(F32) - mid.astype(F32)).astype(BF16)


def _bias_table(rel_bias, dep=None):
    def body(rb_ref, o_ref, diag):
        h = pl.program_id(0)

        @pl.when(h == 0)
        def _():
            hi, mid, lo = _split3(rb_ref[...])
            slots = _diagonal_slots(0)
            diag[...] = _dot(hi, slots) + (_dot(mid, slots) + _dot(lo, slots))

        rows = jnp.broadcast_to(diag[pl.ds(h, 1), :], (Q_BLOCK, DIAG))
        row = lax.broadcasted_iota(jnp.int32, (Q_BLOCK, K_BLOCK), 0)
        col = lax.broadcasted_iota(jnp.int32, (Q_BLOCK, K_BLOCK), 1)
        first = row - (row & (CHUNK - 1))
        seen = (col >= first) & (col < first + BAND)
        o_ref[...] = jnp.where(seen, pltpu.roll(rows, 0, 1, stride=1, stride_axis=0)[:, :K_BLOCK], MASKED)

    return _pallas(
        body,
        1,
        dep,
        name="bias_table",
        grid=(ATT_HEADS,),
        in_specs=[pl.BlockSpec((ATT_HEADS, N_REL_PAD), lambda h: (0, 0))],
        out_specs=pl.BlockSpec((None, Q_BLOCK, K_BLOCK), lambda h: (h, 0, 0)),
        out_shape=jax.ShapeDtypeStruct((ATT_HEADS, Q_BLOCK, K_BLOCK), F32),
        scratch_shapes=[pltpu.VMEM((ATT_HEADS, DIAG), F32)],
        compiler_params=_params("arbitrary"),
    )(rel_bias)


def _att_probs(q_ref, kpad, bias_ref, blk):
    qs = (q_ref[...] * ATT_SCALE).astype(BF16)
    start = pl.multiple_of(blk * Q_BLOCK, Q_BLOCK)
    kb = kpad[pl.ds(start, K_BLOCK), :]
    s = _dot(qs, kb, NT) + bias_ref[...]
    col = lax.broadcasted_iota(jnp.int32, (Q_BLOCK, K_BLOCK), 1)
    s = jnp.where(col >= PAD - blk * Q_BLOCK, s, MASKED)
    e = jnp.exp(s - jnp.max(s, axis=-1, keepdims=True))
    return qs, kb, start, e * (1.0 / jnp.sum(e, axis=-1, keepdims=True))


def _fill_padded(dst, src):
    dst[0:PAD, :] = jnp.zeros((PAD, HEAD_DIM), BF16)
    dst[PAD:, :] = src[...].astype(BF16)


def _att_fwd(proj, bias, dep=None):
    t = proj.shape[0]
    nb = t // Q_BLOCK

    def body(q_ref, k_ref, v_ref, bias_ref, y_ref, kpad, vpad):
        c = pl.program_id(1)

        @pl.when(c == 0)
        def _():
            _fill_padded(kpad, k_ref)
            _fill_padded(vpad, v_ref)

        _, _, start, p = _att_probs(q_ref, kpad, bias_ref, c)
        y_ref[...] = _dot(p.astype(BF16), vpad[pl.ds(start, K_BLOCK), :]).astype(BF16)

    base = 4 * HG_HEADS
    return _pallas(
        body,
        4,
        dep,
        name="att_fwd",
        grid=(ATT_HEADS, nb),
        in_specs=[
            pl.BlockSpec((Q_BLOCK, HEAD_DIM), lambda h, c: (c, base + h)),
            pl.BlockSpec((t, HEAD_DIM), lambda h, c: (0, base + ATT_HEADS + h)),
            pl.BlockSpec((t, HEAD_DIM), lambda h, c: (0, base + 2 * ATT_HEADS + h)),
            pl.BlockSpec((None, Q_BLOCK, K_BLOCK), lambda h, c: (h, 0, 0)),
        ],
        out_specs=pl.BlockSpec((Q_BLOCK, HEAD_DIM), lambda h, c: (c, h)),
        out_shape=jax.ShapeDtypeStruct((t, ATT_WIDTH), BF16),
        scratch_shapes=[pltpu.VMEM((t + PAD, HEAD_DIM), BF16), pltpu.VMEM((t + PAD, HEAD_DIM), BF16)],
        compiler_params=_params("arbitrary", "arbitrary"),
    )(proj, proj, proj, bias)


def _att_bwd(proj, bias, dycat, dep=None):
    t = proj.shape[0]
    nb = t // Q_BLOCK

    def body(q_ref, k_ref, v_ref, bias_ref, dy_ref, dq_ref, dk_ref, dv_ref, g_ref, kpad, vpad, dkacc, dvacc):
        c = pl.program_id(1)

        @pl.when(c == 0)
        def _():
            _fill_padded(kpad, k_ref)
            _fill_padded(vpad, v_ref)
            dkacc[...] = jnp.zeros_like(dkacc)
            dvacc[...] = jnp.zeros_like(dvacc)
            g_ref[...] = jnp.zeros_like(g_ref)

        qs, kb, start, p = _att_probs(q_ref, kpad, bias_ref, c)
        band = pl.ds(start, K_BLOCK)
        dyb = dy_ref[...].astype(BF16)
        dvacc[band, :] += _dot(p.astype(BF16), dyb, TN)
        dp = _dot(dyb, vpad[band, :], NT)
        ds = p * (dp - jnp.sum(dp * p, axis=-1, keepdims=True))
        g_ref[...] += ds
        dsb = ds.astype(BF16)
        dq_ref[...] = (_dot(dsb, kb) * ATT_SCALE).astype(BF16)
        dkacc[band, :] += _dot(dsb, qs, TN)

        @pl.when(c == nb - 1)
        def _():
            dk_ref[...] = dkacc[PAD:, :].astype(BF16)
            dv_ref[...] = dvacc[PAD:, :].astype(BF16)

    base = 4 * HG_HEADS
    whole = pl.BlockSpec((t, HEAD_DIM), lambda h, c: (0, h))
    return _pallas(
        body,
        5,
        dep,
        name="att_bwd",
        grid=(ATT_HEADS, nb),
        in_specs=[
            pl.BlockSpec((Q_BLOCK, HEAD_DIM), lambda h, c: (c, base + h)),
            pl.BlockSpec((t, HEAD_DIM), lambda h, c: (0, base + ATT_HEADS + h)),
            pl.BlockSpec((t, HEAD_DIM), lambda h, c: (0, base + 2 * ATT_HEADS + h)),
            pl.BlockSpec((None, Q_BLOCK, K_BLOCK), lambda h, c: (h, 0, 0)),
            pl.BlockSpec((Q_BLOCK, HEAD_DIM), lambda h, c: (c, HG_HEADS + h)),
        ],
        out_specs=[pl.BlockSpec((Q_BLOCK, HEAD_DIM), lambda h, c: (c, h)), whole, whole, pl.BlockSpec((None, Q_BLOCK, K_BLOCK), lambda h, c: (h, 0, 0))],
        out_shape=[
            jax.ShapeDtypeStruct((t, ATT_WIDTH), BF16),
            jax.ShapeDtypeStruct((t, ATT_WIDTH), BF16),
            jax.ShapeDtypeStruct((t, ATT_WIDTH), BF16),
            jax.ShapeDtypeStruct((ATT_HEADS, Q_BLOCK, K_BLOCK), F32),
        ],
        scratch_shapes=[
            pltpu.VMEM((t + PAD, HEAD_DIM), BF16),
            pltpu.VMEM((t + PAD, HEAD_DIM), BF16),
            pltpu.VMEM((t + PAD, HEAD_DIM), F32),
            pltpu.VMEM((t + PAD, HEAD_DIM), F32),
        ],
        compiler_params=_params("arbitrary", "arbitrary"),
    )(proj, proj, proj, bias, dycat)


def _rel_bias_grad(gsum):
    def body(g_ref, o_ref):
        r = lax.broadcasted_iota(jnp.int32, (Q_BLOCK, Q_BLOCK), 0)
        c = lax.broadcasted_iota(jnp.int32, (Q_BLOCK, Q_BLOCK), 1)
        flip = jnp.where(r + c == Q_BLOCK - 1, 1.0, 0.0).astype(BF16)
        sums = []
        for h in range(ATT_HEADS):
            hi, mid, lo = _split3(g_ref[h])
            rev = _dot(flip, hi) + (_dot(flip, mid) + _dot(flip, lo))
            wide = jnp.concatenate([rev, jnp.zeros((Q_BLOCK, DIAG - K_BLOCK), F32)], axis=1)
            sums.append(jnp.sum(pltpu.roll(wide, 0, 1, stride=1, stride_axis=0), axis=0, keepdims=True))
        hi, mid, lo = _split3(jnp.concatenate(sums, axis=0))
        slots = _diagonal_slots(Q_BLOCK - 1)
        o_ref[...] = _dot(hi, slots, NT) + (_dot(mid, slots, NT) + _dot(lo, slots, NT))

    return pl.pallas_call(
        body,
        name="rel_bias_grad",
        out_shape=jax.ShapeDtypeStruct((ATT_HEADS, N_REL_PAD), F32),
        compiler_params=_params(),
    )(gsum)


HALO = 16


FF_TILE = 1408
FF_TILES = D_FF // FF_TILE


def _interleave_cols(a):
    lead = a.shape[:-1]
    return jnp.swapaxes(a.reshape(*lead, 2, FF_TILES, FF_TILE), -3, -2).reshape(*lead, 2 * D_FF)


def _deinterleave_cols(a):
    lead = a.shape[:-1]
    return jnp.swapaxes(a.reshape(*lead, FF_TILES, 2, FF_TILE), -3, -2).reshape(*lead, 2 * D_FF)


def _ffn_specs(t, tm):
    wide = 2 * FF_TILE
    tile = pl.BlockSpec((tm, wide), lambda j, i: (i, j))
    before = pl.BlockSpec((HALO, wide), lambda j, i: (jnp.maximum(i * (tm // HALO) - 1, 0), j))
    after = pl.BlockSpec((HALO, wide), lambda j, i: (jnp.minimum((i + 1) * (tm // HALO), t // HALO - 1), j))
    vec = lambda rows: pl.BlockSpec((rows, wide), lambda j, i: (0, j))
    return tile, before, after, vec


def _shifted(x, rows, offsets):
    r = lax.broadcasted_iota(jnp.int32, (rows, x.shape[0]), 0)
    c = lax.broadcasted_iota(jnp.int32, (rows, x.shape[0]), 1)
    pick = jnp.concatenate([jnp.where(c == r + o, 1.0, 0.0).astype(BF16) for o in offsets], axis=0)
    out = _dot(pick, x)
    return [out[k * rows : (k + 1) * rows] for k in range(len(offsets))]


def _conv(x, w, b, rows):
    taps = _shifted(x, rows, [HALO - 2, HALO - 1]) + [x[HALO : HALO + rows].astype(F32)]
    return b + w[0:1] * taps[0] + w[1:2] * taps[1] + w[2:3] * taps[2], taps


def _ffn_act_fwd(u, conv_w, conv_b):
    t = u.shape[0]
    tm = _tile(t, (128,))
    tile, before, _, vec = _ffn_specs(t, tm)

    def body(u_ref, h_ref, w_ref, b_ref, z_ref):
        first = pl.program_id(1) == 0
        halo = h_ref[...]
        x = jnp.concatenate([jnp.where(first, jnp.zeros_like(halo), halo), u_ref[...]], axis=0)
        c, _ = _conv(x, w_ref[...], b_ref[...], tm)
        gate, val = c[:, :FF_TILE], c[:, FF_TILE:]
        z_ref[...] = (gate * _sigmoid(gate) * val).astype(BF16)

    return pl.pallas_call(
        body,
        name="ffn_act_fwd",
        grid=(FF_TILES, t // tm),
        in_specs=[tile, before, vec(3), vec(1)],
        out_specs=pl.BlockSpec((tm, FF_TILE), lambda j, i: (i, j)),
        out_shape=jax.ShapeDtypeStruct((t, D_FF), BF16),
        compiler_params=_params("parallel", "parallel"),
    )(u, u, conv_w, conv_b)


def _ffn_act_bwd(u, dz, conv_w, conv_b, dep=None):
    t = u.shape[0]
    tm = _tile(t, (128,))
    nt = t // tm
    ext = tm + HALO
    tile, before, after, vec = _ffn_specs(t, tm)

    def body(u_ref, ub_ref, ua_ref, w_ref, b_ref, dz_ref, dza_ref, du_ref, dw_ref, db_ref):
        i = pl.program_id(1)
        first, last = i == 0, i == nt - 1
        ub, ua = ub_ref[...], ua_ref[...]
        parts = [jnp.where(first, jnp.zeros_like(ub), ub), u_ref[...], jnp.where(last, jnp.zeros_like(ua), ua)]
        w = w_ref[...]
        c, taps = _conv(jnp.concatenate(parts, axis=0), w, b_ref[...], ext)
        gate, val = c[:, :FF_TILE], c[:, FF_TILE:]
        dz = jnp.concatenate([dz_ref[...].astype(F32), jnp.where(last, 0.0, dza_ref[...].astype(F32))], axis=0)
        sg = _sigmoid(gate)
        d = jnp.concatenate([dz * val * (sg * (1.0 + gate * (1.0 - sg))), dz * (gate * sg)], axis=1)
        d1, d2 = _shifted(d.astype(BF16), tm, [1, 2])
        du_ref[...] = (w[2:3] * d[:tm] + w[1:2] * d1 + w[0:1] * d2).astype(BF16)

        @pl.when(first)
        def _():
            dw_ref[...] = jnp.zeros_like(dw_ref)
            db_ref[...] = jnp.zeros_like(db_ref)

        for k, tap in enumerate(taps):
            dw_ref[k : k + 1, :] += jnp.sum(d[:tm] * tap[:tm], axis=0, keepdims=True)
        db_ref[...] += jnp.sum(d[:tm], axis=0, keepdims=True)

    narrow = lambda rows, index: pl.BlockSpec((rows, FF_TILE), index)
    return _pallas(
        body,
        7,
        dep,
        name="ffn_act_bwd",
        grid=(FF_TILES, nt),
        in_specs=[
            tile, before, after, vec(3), vec(1),
            narrow(tm, lambda j, i: (i, j)),
            narrow(HALO, lambda j, i: (jnp.minimum((i + 1) * (tm // HALO), t // HALO - 1), j)),
        ],
        out_specs=[tile, vec(3), vec(1)],
        out_shape=[
            jax.ShapeDtypeStruct((t, 2 * D_FF), BF16),
            jax.ShapeDtypeStruct((3, 2 * D_FF), F32),
            jax.ShapeDtypeStruct((1, 2 * D_FF), F32),
        ],
        compiler_params=_params("parallel", "arbitrary"),
    )(u, u, u, conv_w, conv_b, dz, dz)


def _ple_loss(gpre, pp, h2, final_norm, target):
    t, d = h2.shape
    tm = _tile(t, (256,))

    def body(gp_ref, pp_ref, h_ref, g_ref, tg_ref, dh_ref, dgp_ref, dpp_ref, dg_ref, loss_ref):
        i = pl.program_id(0)
        gate = _sigmoid(gp_ref[...])
        ppv = pp_ref[...]
        h3 = h_ref[...] + gate * ppv
        r = lax.rsqrt(jnp.mean(h3 * h3, axis=-1, keepdims=True) + EPS)
        n = h3 * r
        g = g_ref[...]
        err = n * g - tg_ref[...]
        loss = 0.5 * jnp.sum(jnp.mean(err * err, axis=-1, keepdims=True))
        dy = err * (1.0 / d)
        dn = dy * g
        dh = r * (dn - n * jnp.mean(dn * n, axis=-1, keepdims=True))
        dh_ref[...] = dh
        dgp_ref[...] = (dh * ppv * gate * (1.0 - gate)).astype(BF16)
        dpp_ref[...] = (dh * gate).astype(BF16)
        dg = jnp.sum(dy * n, axis=0, keepdims=True)

        @pl.when(i == 0)
        def _():
            dg_ref[...] = dg
            loss_ref[...] = jnp.full(loss_ref.shape, loss, F32)

        @pl.when(i > 0)
        def _():
            dg_ref[...] += dg
            loss_ref[...] += loss

    row = pl.BlockSpec((tm, d), lambda i: (i, 0))
    vec = pl.BlockSpec((1, d), lambda i: (0, 0))
    return pl.pallas_call(
        body,
        name="ple_loss",
        grid=(t // tm,),
        in_specs=[row, row, row, vec, row],
        out_specs=[row, row, row, vec, pl.BlockSpec((8, 128), lambda i: (0, 0))],
        out_shape=[
            jax.ShapeDtypeStruct((t, d), F32),
            jax.ShapeDtypeStruct((t, d), BF16),
            jax.ShapeDtypeStruct((t, d), BF16),
            jax.ShapeDtypeStruct((1, d), F32),
            jax.ShapeDtypeStruct((8, 128), F32),
        ],
        compiler_params=_params("arbitrary"),
    )(gpre, pp, h2, final_norm, target)


def _adamw(w, g, m, v):
    m = ADAM_B1 * m + (1.0 - ADAM_B1) * g
    v = ADAM_B2 * v + (1.0 - ADAM_B2) * (g * g)
    m_hat = m / (1.0 - ADAM_B1 ** ADAM_STEP)
    v_hat = v / (1.0 - ADAM_B2 ** ADAM_STEP)
    return -ADAM_LR * (m_hat / (jnp.sqrt(v_hat) + ADAM_EPS) + ADAM_WD * w), m, v


def _adam_big(w, m, v, own, recv, name, dep=None):
    r, c = w.shape
    tr = _tile(r, (256, 176))

    def body(w_ref, m_ref, v_ref, own_ref, recv_ref, g_ref, d_ref, nm_ref, nv_ref):
        g = own_ref[...]
        for k in range(3):
            g = g + recv_ref[k].astype(F32)
        g_ref[...] = g
        d_ref[...], nm_ref[...], nv_ref[...] = _adamw(w_ref[...], g, m_ref[...], v_ref[...])

    blk = pl.BlockSpec((tr, c), lambda i: (i, 0))
    return _pallas(
        body,
        5,
        dep,
        name=name,
        grid=(r // tr,),
        in_specs=[blk, blk, blk, blk, pl.BlockSpec((3, tr, c), lambda i: (0, i, 0))],
        out_specs=[blk] * 4,
        out_shape=[jax.ShapeDtypeStruct((r, c), F32)] * 4,
        compiler_params=_params("parallel"),
    )(w, m, v, own, recv)


def _adam_small(w, g, m, v):
    def body(w_ref, g_ref, m_ref, v_ref, d_ref, nm_ref, nv_ref):
        d_ref[...], nm_ref[...], nv_ref[...] = _adamw(w_ref[...], g_ref[...], m_ref[...], v_ref[...])

    return pl.pallas_call(body, name="adam_small", out_shape=[jax.ShapeDtypeStruct(w.shape, F32)] * 3, compiler_params=_params())(w, g, m, v)


def _cast_bf16(w, name):
    r, c = w.shape
    tr = _tile(r, (256, 176))

    def body(w_ref, o_ref):
        o_ref[...] = w_ref[...].astype(BF16)

    blk = pl.BlockSpec((tr, c), lambda i: (i, 0))
    return pl.pallas_call(
        body, name=name, grid=(r // tr,), in_specs=[blk], out_specs=blk, out_shape=jax.ShapeDtypeStruct((r, c), BF16), compiler_params=_params("parallel")
    )(w)


def _position():
    return lax.axis_index("x"), lax.axis_index("y"), lax.axis_index("c")


def _other_chips(x, y):
    return [(1 - x, y), (x, 1 - y), (1 - x, 1 - y)]


def _block_index(dev, interleaved):
    x, y, c = dev
    return 4 * y + 2 * c + x if interleaved else 4 * x + 2 * y + c


def _shard_of(ref, axis, size, dev, interleaved=False):
    start = pl.multiple_of(_block_index(dev, interleaved) * size, 128 if axis == 1 else 16)
    return ref.at[:, pl.ds(start, size)] if axis == 1 else ref.at[pl.ds(start, size), :]


def _all_gather(shards, axes, interleaved):
    n = len(shards)

    def body(*refs):
        ins, outs = refs[:n], refs[n : 2 * n]
        send_sems, recv_sems, local_sems = refs[2 * n :]
        x, y, c = _position()
        me, sibling = (x, y, c), (x, y, 1 - c)
        chips = _other_chips(x, y)
        firsts, passed, locals_ = [], [], []
        for w in range(n):
            size = shards[w].shape[axes[w]]
            slot = functools.partial(_shard_of, outs[w], axes[w], size, interleaved=interleaved[w])

            def copy(k, block, to, src=None, w=w, slot=slot):
                return pltpu.make_async_remote_copy(
                    src_ref=slot(block) if src is None else src,
                    dst_ref=slot(block),
                    send_sem=send_sems.at[7 * w + k],
                    recv_sem=recv_sems.at[7 * w + k],
                    device_id=to,
                    device_id_type=MESH,
                )

            mine = pltpu.make_async_copy(ins[w], slot(me), local_sems.at[w])
            mine.start()
            locals_.append(mine)
            first = [copy(0, me, sibling, src=ins[w])] + [copy(1 + j, me, (*chip, c), src=ins[w]) for j, chip in enumerate(chips)]
            for cp in first:
                cp.start()
            firsts.append((first, copy))
        for w in range(n):
            first, copy = firsts[w]
            fwd = [copy(4 + j, (*chip, c), sibling) for j, chip in enumerate(chips)]
            for j, chip in enumerate(chips):
                copy(1 + j, (*chip, c), me).wait_recv()
                fwd[j].start()
            passed.append(fwd)
        for w in range(n):
            first, copy = firsts[w]
            copy(0, sibling, me).wait_recv()
            for j, chip in enumerate(chips):
                copy(4 + j, (*chip, 1 - c), me).wait_recv()
            for cp in first + passed[w]:
                cp.wait_send()
            locals_[w].wait()

    def full(s, ax):
        shape = list(s.shape)
        shape[ax] *= N_DEV
        return jax.ShapeDtypeStruct(tuple(shape), s.dtype)

    return pl.pallas_call(
        body,
        name="all_gather_weights",
        in_specs=[ANY] * n,
        out_specs=[ANY] * n,
        out_shape=[full(s, ax) for s, ax in zip(shards, axes)],
        scratch_shapes=[pltpu.SemaphoreType.DMA((7 * n,)), pltpu.SemaphoreType.DMA((7 * n,)), pltpu.SemaphoreType.DMA((n,))],
    )(*shards)


def _add_blocks(ids, grad, landed, axis, size, targets, out_dtype, name):
    rows = size if axis == 0 else grad.shape[0]
    cols = size if axis == 1 else grad.shape[1]
    tr = _tile(rows, (256, 176))
    nr = rows // tr
    nt = len(targets)

    def body(ids_ref, g_ref, l_ref, o_ref):
        o_ref[...] = (g_ref[...] + l_ref[...]).astype(out_dtype)

    if axis == 1:
        g_spec = pl.BlockSpec((tr, cols), lambda k, i, ids: (i, ids[targets[0] + k]))
    else:
        g_spec = pl.BlockSpec((tr, cols), lambda k, i, ids: (ids[targets[0] + k] * nr + i, 0))
    return pl.pallas_call(
        body,
        name=name,
        grid_spec=pltpu.PrefetchScalarGridSpec(
            num_scalar_prefetch=1,
            grid=(nt, nr),
            in_specs=[g_spec, pl.BlockSpec((None, tr, cols), lambda k, i, ids: (ids[4 + targets[0] + k], i, 0))],
            out_specs=pl.BlockSpec((None, tr, cols), lambda k, i, ids: (k, i, 0)),
        ),
        out_shape=jax.ShapeDtypeStruct((nt, rows, cols), out_dtype),
        compiler_params=_params("parallel", "parallel"),
    )(ids, grad, landed)


def _all_reduce_small(vec, name):
    rows = vec.shape[0]

    def body(v_ref, o_ref, land, send_sems, recv_sems):
        x, y, c = _position()
        mine = 4 * x + 2 * y + c
        copies = []
        for mask in range(1, N_DEV):
            peer = (1 - x if mask & 4 else x, 1 - y if mask & 2 else y, 1 - c if mask & 1 else c)
            copies.append(
                pltpu.make_async_remote_copy(
                    src_ref=v_ref, dst_ref=land.at[mine], send_sem=send_sems.at[mask - 1], recv_sem=recv_sems.at[mask - 1], device_id=peer, device_id_type=MESH
                )
            )
        for cp in copies:
            cp.start()
        land[mine] = v_ref[...]
        for cp in copies:
            cp.wait()
        acc = land[0]
        for k in range(1, N_DEV):
            acc = acc + land[k]
        o_ref[...] = acc

    return pl.pallas_call(
        body,
        name=name,
        out_shape=jax.ShapeDtypeStruct(vec.shape, F32),
        in_specs=[pl.BlockSpec(memory_space=pltpu.VMEM)],
        out_specs=pl.BlockSpec(memory_space=pltpu.VMEM),
        scratch_shapes=[pltpu.VMEM((N_DEV, rows, 128), F32), pltpu.SemaphoreType.DMA((N_DEV - 1,)), pltpu.SemaphoreType.DMA((N_DEV - 1,))],
    )(vec)


def _rows128(a, rows):
    flat = a.reshape(-1)
    return jnp.pad(flat, (0, rows * 128 - flat.shape[0])).reshape(rows, 128)


def _pad_rel(a):
    return jnp.pad(a.reshape(ATT_HEADS, -1)[:, :N_REL], ((0, 0), (0, N_REL_PAD - N_REL)))


SMALL = [("norm_mix", 16), ("lb_logits", 16), ("hg_norm", 8), ("rel_bias", 24), ("norm_ffn", 16), ("conv_b", 88), ("norm_ple", 16), ("final_norm", 16)]
CONV_W_FULL_ROWS = 3 * 2 * D_FF // 128
CONV_W_SHARD_ROWS = 40


def _pack_small(parts):
    return jnp.concatenate([_rows128(_pad_rel(parts[k]) if k == "rel_bias" else parts[k], rows) for k, rows in SMALL], axis=0)


def _unpack_small(packed, shapes):
    out, at = {}, 0
    for k, rows in SMALL:
        blk = packed[at : at + rows]
        at += rows
        if k == "rel_bias":
            out[k] = blk.reshape(ATT_HEADS, N_REL_PAD)[:, :N_REL].reshape(shapes[k])
        else:
            n = 1
            for s in shapes[k]:
                n *= s
            out[k] = blk.reshape(-1)[:n].reshape(shapes[k])
    return out, at


BIG = [("w_in", 1), ("w_out", 0), ("w_up", 1), ("w_down", 0), ("w_ple_gate", 0), ("w_ple_proj", 1)]


HBM = pl.BlockSpec(memory_space=pltpu.HBM)
SEM = pl.BlockSpec(memory_space=pltpu.SEMAPHORE)
EFFECT = pltpu.SideEffectType.DATAFLOW_SIDE_EFFECTING


def _copies(plan, refs, send_sems, recv_sems):
    return [
        pltpu.make_async_remote_copy(src_ref=src, dst_ref=dst, send_sem=send_sems.at[i], recv_sem=recv_sems.at[i], device_id=dev, device_id_type=MESH)
        for i, (src, dst, dev) in enumerate(plan(refs))
    ]


def _split_start(name, arrays, plan, n):
    k = len(arrays)

    def body(*refs):
        for cp in _copies(plan, refs[:k], refs[k], refs[k + 1]):
            cp.start()
        refs[-1][...] = jnp.zeros_like(refs[-1])

    out = pl.pallas_call(
        body,
        name=name,
        out_shape=(pltpu.SemaphoreType.DMA((n,)), pltpu.SemaphoreType.DMA((n,)), *[pltpu.HBM(a.shape, a.dtype) for a in arrays], jax.ShapeDtypeStruct((8, 128), F32)),
        in_specs=[HBM] * k,
        out_specs=(SEM, SEM, *[HBM] * k, pl.BlockSpec(memory_space=pltpu.VMEM)),
        input_output_aliases={i: 2 + i for i in range(k)},
        compiler_params=pltpu.CompilerParams(has_side_effects=EFFECT),
    )(*[pltpu.with_memory_space_constraint(a, pltpu.HBM) for a in arrays])
    return out[0], out[1], list(out[2 : 2 + k]), out[-1]


def _split_wait(name, send, recv, arrays, plan, after):
    k = len(arrays)

    def body(*refs):
        for cp in _copies(plan, refs[:k], refs[k], refs[k + 1]):
            cp.wait_send()
            cp.wait_recv()

    out = pl.pallas_call(
        body,
        name=name,
        out_shape=tuple(pltpu.HBM(a.shape, a.dtype) for a in arrays),
        in_specs=[HBM] * k + [SEM, SEM, ANY],
        out_specs=tuple([HBM] * k),
        input_output_aliases={i: i for i in range(k)},
        compiler_params=pltpu.CompilerParams(has_side_effects=EFFECT),
    )(*arrays, send, recv, after)
    return list(out)


def _cast_into(w, me, axis, name, dep, dtype):
    r, c = w.shape
    tr = _tile(r, (256, 176))
    nr = r // tr
    deps = [] if dep is None else [dep]

    def body(me_ref, w_ref, *rest):
        rest[-1][...] = w_ref[...].astype(dtype)

    if axis == 1:
        shape, o_spec = (r, N_DEV * c), pl.BlockSpec((tr, c), lambda i, me: (i, me[0]))
    else:
        shape, o_spec = (N_DEV * r, c), pl.BlockSpec((tr, c), lambda i, me: (me[0] * nr + i, 0))
    return pl.pallas_call(
        body,
        name=name,
        grid_spec=pltpu.PrefetchScalarGridSpec(
            num_scalar_prefetch=1, grid=(nr,), in_specs=[pl.BlockSpec((tr, c), lambda i, me: (i, 0))] + [ANY] * len(deps), out_specs=o_spec
        ),
        out_shape=jax.ShapeDtypeStruct(shape, dtype),
        compiler_params=_params("parallel"),
    )(me, w, *deps)


GATHER = [
    (["w_in"], None, "norm_mix_fwd", "bias_table"),
    (["w_out"], "norm_mix_fwd", "att_fwd", None),
    (["w_up", "conv_w"], "norm_mix_fwd", "att_fwd", "norm_ffn_fwd"),
    (["w_down", "w_ple_gate", "w_ple_proj"], "att_fwd", "up_proj", "ffn_act_fwd"),
]
GROUPS = [["w_ple_proj", "w_ple_gate", "w_down"], ["w_up"], ["w_out"], ["w_in"]]
STAGES = ["ffn_act_bwd", "d_mix_out", "hgrn_bwd", "d_norm_mix_out"]
INTERLEAVED = {"w_up", "conv_w"}


class _Exchange:
    def __init__(self, big, conv_w, position):
        self.big, self.axis = big, dict(BIG, conv_w=1)
        self.shards = dict(big, conv_w=conv_w)
        self.size = {k: w.shape[self.axis[k]] for k, w in self.shards.items()}
        self.x, self.y, self.c = position
        chips = [(self.x, self.y)] + _other_chips(self.x, self.y)
        landed = [2 * cx + cy for cx, cy in chips]
        self.ids = {
            flag: jnp.stack([_block_index((cx, cy, self.c), flag) for cx, cy in chips] + landed).astype(jnp.int32) for flag in (False, True)
        }
        self.tokens, self.grads, self.state, self.wfull = [], {}, {}, {}


    def _slot(self, ref, k, dev):
        return _shard_of(ref, self.axis[k], self.size[k], dev, interleaved=k in INTERLEAVED)

    def _plan_gather(self, names, direct, refs):
        x, y, c = _position()
        me, out = (x, y, c), []
        for k, ref in zip(names, refs):
            mine = self._slot(ref, k, me)
            out.append((mine, mine, (x, y, 1 - c)))
            out += [(mine, mine, (*chip, c)) for chip in _other_chips(x, y)]
            if direct:
                out += [(mine, mine, (*chip, 1 - c)) for chip in _other_chips(x, y)]
        return out

    def _plan_forward(self, names, refs):
        x, y, c = _position()
        out = []
        for k, ref in zip(names, refs):
            for chip in _other_chips(x, y):
                block = self._slot(ref, k, (*chip, c))
                out.append((block, block, (x, y, 1 - c)))
        return out

    def _plan_sibling(self, names, refs):
        x, y, c = _position()
        n = len(names)
        return [(self._slot(refs[i], k, (p // 2, p % 2, 1 - c)), refs[n + i].at[p], (x, y, 1 - c)) for i, k in enumerate(names) for p in range(4)]

    def _plan_chips(self, names, refs):
        x, y, c = _position()
        n = len(names)
        return [(refs[i].at[j], refs[n + i].at[j], (*chip, c)) for i in range(n) for j, chip in enumerate(_other_chips(x, y))]


    def gather(self):
        me = {flag: _block_index((self.x, self.y, self.c), flag).astype(jnp.int32).reshape(1) for flag in (False, True)}
        self.late, self.unsent = {}, {}
        after = None
        for gi, (names, issued, *_) in enumerate(GATHER):
            self.unsent[gi] = [
                _cast_into(self.shards[k], me[k in INTERLEAVED], self.axis[k], "cast_" + k, after, F32 if k == "conv_w" else BF16) for k in names
            ]
            if issued is None:
                self._issue(None)
                after = self.tokens[-1]

    def _issue(self, stage):
        for gi, (names, issued, _, forwarded) in enumerate(GATHER):
            if issued == stage and gi in self.unsent:
                plan = functools.partial(self._plan_gather, names, forwarded is None)
                copies = (7 if forwarded is None else 4) * len(names)
                send, recv, fulls, token = _split_start(f"gather_start_{gi}", self.unsent.pop(gi), plan, copies)
                self.tokens.append(token)
                self.late[gi] = (send, recv, fulls, plan)

    def weight(self, k):
        return self.wfull[k]

    def dep(self):
        tokens, self.tokens = self.tokens, []
        return tokens

    def reduce(self, vec, name):
        return _all_reduce_small(vec, name)

    def grad(self, k, g):
        self.grads[k] = g
        for gi, names in enumerate(GROUPS):
            if k == names[-1]:
                plan = functools.partial(self._plan_sibling, names)
                lands = [lax.empty((4, *self._shard_shape(n)), F32) for n in names]
                send, recv, arrays, token = _split_start(f"sibling_start_{gi}", [self.grads[n] for n in names] + lands, plan, 4 * len(names))
                self.tokens.append(token)
                self.state[gi] = (send, recv, arrays, plan)

    def done(self, stage, after):
        for gi, (names, _, _, forwarded) in enumerate(GATHER):
            if forwarded == stage:
                send, recv, fulls, plan = self.late[gi]
                self.wfull.update(zip(names, _split_wait(f"forward_wait_{gi}", send, recv, fulls, plan, after)))
        for gi, (names, _, arrived, forwarded) in enumerate(GATHER):
            if arrived == stage:
                send, recv, fulls, plan = self.late[gi]
                fulls = _split_wait(f"gather_wait_{gi}", send, recv, fulls, plan, after)
                if forwarded is None:
                    self.wfull.update(zip(names, fulls))
                else:
                    plan = functools.partial(self._plan_forward, names)
                    send, recv, fulls, token = _split_start(f"forward_start_{gi}", fulls, plan, 3 * len(names))
                    self.tokens.append(token)
                    self.late[gi] = (send, recv, fulls, plan)
        self._issue(stage)
        if stage in STAGES:
            self._to_chips(STAGES.index(stage), after)

    def _shard_shape(self, k):
        shape = list(self.grads[k].shape)
        shape[self.axis[k]] = self.size[k]
        return tuple(shape)

    def _to_chips(self, gi, after):
        names = GROUPS[gi]
        n = len(names)
        send, recv, arrays, plan = self.state[gi]
        arrays = _split_wait(f"sibling_wait_{gi}", send, recv, arrays, plan, after)
        own, parts = [], []
        for k, g, land in zip(names, arrays[:n], arrays[n:]):
            ids = self.ids[k in INTERLEAVED]
            own.append(_add_blocks(ids, g, land, self.axis[k], self.size[k], [0], F32, "add_own_" + k)[0])
            parts.append(_add_blocks(ids, g, land, self.axis[k], self.size[k], [1, 2, 3], BF16, "add_send_" + k))
        plan = functools.partial(self._plan_chips, names)
        lands = [lax.empty(part.shape, BF16) for part in parts]
        send, recv, arrays, token = _split_start(f"chips_start_{gi}", parts + lands, plan, 3 * n)
        self.tokens.append(token)
        self.state[gi] = (send, recv, arrays, plan, own)

    def finish(self, gi, after):
        names = GROUPS[gi]
        send, recv, arrays, plan, own = self.state[gi]
        arrays = _split_wait(f"chips_wait_{gi}", send, recv, arrays, plan, after)
        return {k: (o, r) for k, o, r in zip(names, own, arrays[len(names) :])}


class _Resident:
    def __init__(self, wfull):
        self.wfull, self.grads = wfull, {}

    def weight(self, k):
        return self.wfull[k]

    def grad(self, k, g):
        self.grads[k] = g

    def dep(self):
        return None

    def reduce(self, vec, name):
        return vec

    def done(self, stage, after):
        pass


def _local_step(x, p, target, small, ex):
    a1, r1 = _rms_fwd(x, small["norm_mix"], "norm_mix_fwd", dep=ex.dep())
    ex.done("norm_mix_fwd", a1)
    bias = _bias_table(jnp.pad(small["rel_bias"], ((0, 0), (0, N_REL_PAD - N_REL))), dep=ex.dep())
    ex.done("bias_table", bias)
    proj = _matmul(a1, ex.weight("w_in"), "nn", F32, "in_proj", dep=ex.dep())
    y_hg, o_hg, states = _hgrn_fwd(proj, small["lb_logits"], small["hg_norm"])
    y_att = _att_fwd(proj, bias, dep=ex.dep())
    ex.done("att_fwd", y_att)
    ycat = lax.dynamic_update_slice(y_hg, y_att, (0, HG_WIDTH))
    h1 = _matmul(ycat, ex.weight("w_out"), "nn", F32, "out_proj", resid=x, dep=ex.dep())
    a2, r2 = _rms_fwd(h1, small["norm_ffn"], "norm_ffn_fwd")
    ex.done("norm_ffn_fwd", a2)
    conv_w = ex.weight("conv_w")
    u = _matmul(a2, ex.weight("w_up"), "nn", BF16, "up_proj")
    conv_b = _interleave_cols(small["conv_b"])
    ex.done("up_proj", u)
    z = _ffn_act_fwd(u, conv_w, conv_b)
    ex.done("ffn_act_fwd", z)
    h2 = _matmul(z, ex.weight("w_down"), "nn", F32, "down_proj", tk=2816, resid=h1)
    a3, r3 = _rms_fwd(h2, small["norm_ple"], "norm_ple_fwd")
    gpre = _matmul(a3, ex.weight("w_ple_gate"), "nn", F32, "ple_gate")
    pp = _matmul(p, ex.weight("w_ple_proj"), "nn", F32, "ple_proj")
    dh3, dgpre, dpp, d_final, loss = _ple_loss(gpre, pp, h2, small["final_norm"], target)

    ex.grad("w_ple_proj", _matmul(p, dpp, "tn", F32, "d_w_ple_proj", tm=512))
    ex.grad("w_ple_gate", _matmul(a3, dgpre, "tn", F32, "d_w_ple_gate", tm=512))
    da3 = _matmul(dgpre, ex.weight("w_ple_gate"), "nt", F32, "d_norm_ple_out")
    dh2, d_ple = _rms_bwd(da3, h2, r3, small["norm_ple"], dh3, "norm_ple_bwd")
    dz = _matmul(dh2, ex.weight("w_down"), "nt", BF16, "d_ffn_act")
    ex.grad("w_down", _matmul(z, dh2, "tn", F32, "d_w_down", tm=512))
    du, dcw, dcb = _ffn_act_bwd(u, dz, conv_w, conv_b, dep=ex.dep())
    ex.done("ffn_act_bwd", du)
    d_conv_w, d_conv_b = _deinterleave_cols(dcw), _deinterleave_cols(dcb)
    ex.grad("w_up", _matmul(a2, du, "tn", F32, "d_w_up", tm=512, dep=ex.dep()))
    da2 = _matmul(du, ex.weight("w_up"), "nt", F32, "d_norm_ffn_out", tk=2816, dep=ex.dep())
    dh1, d_ffn = _rms_bwd(da2, h1, r2, small["norm_ffn"], dh2, "norm_ffn_bwd")
    dycat = _matmul(dh1, ex.weight("w_out"), "nt", F32, "d_mix_out")
    ex.done("d_mix_out", dycat)
    ex.grad("w_out", _matmul(ycat, dh1, "tn", F32, "d_w_out", tm=512, dep=ex.dep()))
    dp_hg, d_lb, d_hgn = _hgrn_bwd(proj, small["lb_logits"], small["hg_norm"], o_hg, dycat, states, dep=ex.dep())
    ex.done("hgrn_bwd", d_lb)
    dq_att, dk_att, dv_att, gsum = _att_bwd(proj, bias, dycat, dep=ex.dep())
    d_rel = _rel_bias_grad(gsum)
    d_small = {
        "norm_mix": jnp.zeros_like(small["norm_mix"]), "lb_logits": d_lb, "hg_norm": d_hgn, "rel_bias": d_rel, "norm_ffn": d_ffn,
        "conv_b": d_conv_b, "norm_ple": d_ple, "final_norm": d_final,
    }
    packed = jnp.concatenate([_pack_small(d_small), _rows128(d_conv_w, CONV_W_FULL_ROWS), _rows128(loss[0:1, 0:1], 8)], axis=0)
    early = ex.reduce(packed, "all_reduce_small")
    dproj = dp_hg
    for k, part in enumerate((dq_att, dk_att, dv_att)):
        dproj = lax.dynamic_update_slice(dproj, part, (0, 4 * HG_WIDTH + k * ATT_WIDTH))
    ex.grad("w_in", _matmul(a1, dproj, "tn", F32, "d_w_in", tm=512, dep=[early]))
    da1 = _matmul(dproj, ex.weight("w_in"), "nt", F32, "d_norm_mix_out", tk=1792, dep=ex.dep())
    dx, d_mix = _rms_bwd(da1, x, r1, small["norm_mix"], dh1, "norm_mix_bwd")
    rows = dict(SMALL)["norm_mix"]
    late = ex.reduce(_rows128(d_mix, rows), "all_reduce_norm_mix")
    ex.done("d_norm_mix_out", late)
    return dx, jnp.concatenate([late, early[rows:]], axis=0)


def kernel(x, p, norm_mix, w_in, lb_logits, hg_norm, rel_bias, w_out, norm_ffn, w_up, conv_w, conv_b, w_down, norm_ple, w_ple_gate, w_ple_proj, final_norm, loss_target, m_norm_mix, m_w_in, m_lb_logits, m_hg_norm, m_rel_bias, m_w_out, m_norm_ffn, m_w_up, m_conv_w, m_conv_b, m_w_down, m_norm_ple, m_w_ple_gate, m_w_ple_proj, m_final_norm, v_norm_mix, v_w_in, v_lb_logits, v_hg_norm, v_rel_bias, v_w_out, v_norm_ffn, v_w_up, v_conv_w, v_conv_b, v_w_down, v_norm_ple, v_w_ple_gate, v_w_ple_proj, v_final_norm):
    given = dict(locals())
    mx, my, mc = _position()
    me = 4 * mx + 2 * my + mc
    big = {k: given[k][0] for k, _ in BIG}
    ex = _Exchange(big, conv_w[0], (mx, my, mc))
    ex.gather()

    small = {
        "norm_mix": norm_mix, "lb_logits": lb_logits, "hg_norm": hg_norm, "rel_bias": rel_bias[0], "norm_ffn": norm_ffn,
        "conv_b": conv_b, "norm_ple": norm_ple, "final_norm": final_norm.reshape(1, -1),
    }
    dx, reduced = _local_step(x[0], p[0, 0], loss_target[0], small, ex)

    out = {}
    shapes = {k: given[k].shape for k, _ in SMALL}
    g_small, at = _unpack_small(reduced, shapes)
    g_conv_full = reduced[at : at + CONV_W_FULL_ROWS].reshape(3, 2 * D_FF)
    total_loss = reduced[at + CONV_W_FULL_ROWS, 0]
    cw = conv_w.shape[2]
    g_conv = lax.dynamic_slice_in_dim(g_conv_full, me * cw, cw, axis=1)

    def pack_with_conv(parts, conv_part):
        return jnp.concatenate([_pack_small(parts), _rows128(conv_part, CONV_W_SHARD_ROWS)], axis=0)

    d_pk, m_pk, v_pk = _adam_small(
        pack_with_conv({k: given[k] for k, _ in SMALL}, conv_w),
        pack_with_conv(g_small, g_conv),
        pack_with_conv({k: given["m_" + k] for k, _ in SMALL}, m_conv_w),
        pack_with_conv({k: given["v_" + k] for k, _ in SMALL}, v_conv_w),
    )
    for name, pk in (("d", d_pk), ("m", m_pk), ("v", v_pk)):
        parts, at = _unpack_small(pk, shapes)
        parts["conv_w"] = pk[at : at + CONV_W_SHARD_ROWS].reshape(-1)[: 3 * cw].reshape(conv_w.shape)
        for k, a in parts.items():
            out.setdefault(k, {})
            out[k][name] = a
    for k, _ in SMALL:
        out[k]["g"] = g_small[k]
    out["conv_w"]["g"] = g_conv.reshape(conv_w.shape)

    after, started = v_pk, ex.dep()
    for gi in range(len(GROUPS)):
        for k, (o, r) in ex.finish(gi, after).items():
            g, d, nm, nv = _adam_big(big[k], given["m_" + k][0], given["v_" + k][0], o, r, "adam_" + k, dep=started)
            out[k] = tuple(a[None] for a in (g, d, nm, nv))
            after = nv

    order = ["norm_mix", "w_in", "lb_logits", "hg_norm", "rel_bias", "w_out", "norm_ffn", "w_up", "conv_w", "conv_b", "w_down", "norm_ple", "w_ple_gate", "w_ple_proj", "final_norm"]

    def pick(k, what):
        return out[k][what] if isinstance(out[k], dict) else out[k][{"g": 0, "d": 1, "m": 2, "v": 3}[what]]

    return (total_loss, dx[None], *[pick(k, "g") for k in order], *[pick(k, "d") for k in order], *[pick(k, "m") for k in order], *[pick(k, "v") for k in order])
```

```python
import functools

import jax
import jax.numpy as jnp
from jax import lax
from jax.experimental import pallas as pl
from jax.experimental.pallas import tpu as pltpu

F32 = jnp.float32
BF16 = jnp.bfloat16

D_MODEL = 2048
CHUNK = 64
HG_HEADS = 8
HEAD_DIM = 128
HG_WIDTH = HG_HEADS * HEAD_DIM
ATT_HEADS = 8
ATT_WIDTH = ATT_HEADS * HEAD_DIM
LEFT_CHUNKS = 8
PAD = LEFT_CHUNKS * CHUNK
BAND = PAD + CHUNK
REL_CLIP = 128
N_REL = 2 * REL_CLIP + 1
N_REL_PAD = 384
D_FF = 5632
EPS = 1e-6
ATT_SCALE = HEAD_DIM ** -0.5
SUB = 32
HG_BLOCK = 8
Q_BLOCK = 4 * CHUNK
K_BLOCK = Q_BLOCK + PAD
DIAG = 1024
MASKED = -1e30

ADAM_LR = 0.001
ADAM_B1 = 0.9
ADAM_B2 = 0.999
ADAM_EPS = 1e-08
ADAM_WD = 0.01
ADAM_STEP = 10

N_DEV = 8
VMEM_LIMIT = 48 * 1024 * 1024
MESH = pl.DeviceIdType.MESH
ANY = pl.BlockSpec(memory_space=pl.ANY)
HIGHEST = lax.Precision.HIGHEST

NN = (((1,), (0,)), ((), ()))
NT = (((1,), (1,)), ((), ()))
TN = (((0,), (0,)), ((), ()))


def _params(*sem):
    return pltpu.CompilerParams(dimension_semantics=sem if sem else None, vmem_limit_bytes=VMEM_LIMIT)


def _pallas(body, n_in, dep, **kw):
    deps = [] if dep is None else list(dep)
    if not deps:
        return pl.pallas_call(body, **kw)

    def body_after(*refs):
        body(*refs[:n_in], *refs[n_in + len(deps) :])

    call = pl.pallas_call(body_after, **dict(kw, in_specs=list(kw["in_specs"]) + [ANY] * len(deps)))
    return lambda *ops: call(*ops, *deps)


def _dot(a, b, dims=NN):
    return lax.dot_general(a, b, dims, preferred_element_type=F32)


def _dot3(a, b, dims=NN):
    a_hi, b_hi = a.astype(BF16), b.astype(BF16)
    a_lo, b_lo = (a - a_hi.astype(F32)).astype(BF16), (b - b_hi.astype(F32)).astype(BF16)
    return _dot(a_hi, b_hi, dims) + (_dot(a_hi, b_lo, dims) + _dot(a_lo, b_hi, dims))


def _sigmoid(x):
    return 1.0 / (1.0 + jnp.exp(-x))


def _tile(n, prefs):
    for t in prefs:
        if n % t == 0:
            return t
    return n


def _matmul(a, b, mode, out_dtype, name, tm=1024, tn=1024, tk=None, resid=None, dep=None):
    if mode == "nn":
        (m, k), n = a.shape, b.shape[1]
    elif mode == "nt":
        (m, k), n = a.shape, b.shape[0]
    else:
        (k, m), n = a.shape, b.shape[1]
    tm = _tile(m, (tm, 512, 256, 128))
    tn = _tile(n, (tn, 1408, 512, 256, 128))
    tk = k if tk is None else _tile(k, (tk,))
    nk = k // tk
    dims = {"nn": NN, "nt": NT, "tn": TN}[mode]
    a_spec = pl.BlockSpec((tk, tm), lambda i, j, s: (s, i)) if mode == "tn" else pl.BlockSpec((tm, tk), lambda i, j, s: (i, s))
    b_spec = pl.BlockSpec((tn, tk), lambda i, j, s: (j, s)) if mode == "nt" else pl.BlockSpec((tk, tn), lambda i, j, s: (s, j))
    o_spec = pl.BlockSpec((tm, tn), lambda i, j, s: (i, j))
    has_res = resid is not None

    def body(*refs):
        a_ref, b_ref = refs[0], refs[1]
        o_ref = refs[2 + has_res]
        part = _dot(a_ref[...].astype(BF16), b_ref[...].astype(BF16), dims)

        def finish(acc):
            if has_res:
                acc = acc + refs[2][...]
            o_ref[...] = acc.astype(out_dtype)

        if nk == 1:
            finish(part)
        else:
            acc_ref = refs[-1]
            s = pl.program_id(2)

            @pl.when(s == 0)
            def _():
                acc_ref[...] = part

            @pl.when(s > 0)
            def _():
                acc_ref[...] += part

            @pl.when(s == nk - 1)
            def _():
                finish(acc_ref[...])

    return _pallas(
        body,
        2 + has_res,
        dep,
        name=name,
        grid=(m // tm, n // tn, nk),
        in_specs=[a_spec, b_spec] + ([o_spec] if has_res else []),
        out_specs=o_spec,
        out_shape=jax.ShapeDtypeStruct((m, n), out_dtype),
        scratch_shapes=[pltpu.VMEM((tm, tn), F32)] if nk > 1 else [],
        compiler_params=_params("parallel", "parallel", "arbitrary"),
    )(*([a, b] + ([resid] if has_res else [])))


def _rms_fwd(x, g, name, dep=None):
    t, d = x.shape
    tm = _tile(t, (256,))

    def body(x_ref, g_ref, a_ref, r_ref):
        xv = x_ref[...]
        r = lax.rsqrt(jnp.mean(xv * xv, axis=-1, keepdims=True) + EPS)
        a_ref[...] = (xv * r * g_ref[...]).astype(BF16)
        r_ref[...] = r

    row = pl.BlockSpec((tm, d), lambda i: (i, 0))
    return _pallas(
        body,
        2,
        dep,
        name=name,
        grid=(t // tm,),
        in_specs=[row, pl.BlockSpec((1, d), lambda i: (0, 0))],
        out_specs=[row, pl.BlockSpec((tm, 1), lambda i: (i, 0))],
        out_shape=[jax.ShapeDtypeStruct((t, d), BF16), jax.ShapeDtypeStruct((t, 1), F32)],
        compiler_params=_params("parallel"),
    )(x, g)


def _rms_bwd(da, x, r, g, resid, name, dep=None):
    t, d = x.shape
    tm = _tile(t, (256,))

    def body(da_ref, x_ref, r_ref, g_ref, res_ref, dx_ref, dg_ref):
        i = pl.program_id(0)
        rv = r_ref[...]
        n = x_ref[...] * rv
        dav = da_ref[...]
        dn = dav * g_ref[...]
        dx_ref[...] = rv * (dn - n * jnp.mean(dn * n, axis=-1, keepdims=True)) + res_ref[...]
        part = jnp.sum(dav * n, axis=0, keepdims=True)

        @pl.when(i == 0)
        def _():
            dg_ref[...] = part

        @pl.when(i > 0)
        def _():
            dg_ref[...] += part

    row = pl.BlockSpec((tm, d), lambda i: (i, 0))
    vec = pl.BlockSpec((1, d), lambda i: (0, 0))
    return _pallas(
        body,
        5,
        dep,
        name=name,
        grid=(t // tm,),
        in_specs=[row, row, pl.BlockSpec((tm, 1), lambda i: (i, 0)), vec, row],
        out_specs=[row, vec],
        out_shape=[jax.ShapeDtypeStruct((t, d), F32), jax.ShapeDtypeStruct((1, d), F32)],
        compiler_params=_params("arbitrary"),
    )(da, x, r, g, resid)


def _tri(n, upper):
    r = lax.broadcasted_iota(jnp.int32, (n, n), 0)
    c = lax.broadcasted_iota(jnp.int32, (n, n), 1)
    return jnp.where((c >= r) if upper else (c <= r), 1.0, 0.0).astype(F32)


def _hgrn_gates(q, fp, lbl):
    l0, l1 = lbl[0:1, :], lbl[1:2, :]
    mx = jnp.maximum(l0, l1)
    e0, e1 = jnp.exp(l0 - mx), jnp.exp(l1 - mx)
    lb = e0 / (e0 + e1)
    sig = _sigmoid(fp)
    f = lb + (1.0 - lb) * sig
    kk = (1.0 - lb) * _sigmoid(-fp)
    sq = _sigmoid(q)
    b = jnp.dot(_tri(CHUNK, False), jnp.log(f), precision=HIGHEST, preferred_element_type=F32)
    return lb, sig, f, kk, sq, q * sq, b


def _heads(x):
    return [x[:, j * HEAD_DIM : (j + 1) * HEAD_DIM] for j in range(x.shape[1] // HEAD_DIM)]


def _wide(parts):
    return jnp.concatenate(parts, axis=1)


def _intra_blocks(b):
    out = []
    for lo in range(0, CHUNK, SUB):
        hi = lo + SUB
        br = b[lo + SUB // 2 : lo + SUB // 2 + 1, :]
        row = lax.broadcasted_iota(jnp.int32, (SUB, hi), 0) + lo
        col = lax.broadcasted_iota(jnp.int32, (SUB, hi), 1)
        out.append((lo, hi, jnp.exp(b[lo:hi] - br), jnp.exp(br - b[:hi]), col <= row))
    return out


def _hgrn_fwd(proj, lb_logits, hg_norm):
    t = proj.shape[0]
    nc = t // CHUNK

    def body(q_ref, f_ref, i_ref, g_ref, lbl_ref, hgn_ref, y_ref, o_ref, st_ref, s_scr):
        c = pl.program_id(1)

        @pl.when(c == 0)
        def _():
            s_scr[...] = jnp.zeros_like(s_scr)

        hs = range(HG_BLOCK)
        sts = [s_scr[j] for j in hs]
        _, _, _, kk, _, qf, b = _hgrn_gates(q_ref[...], f_ref[...], lbl_ref[...])
        vb = _heads(i_ref[...].astype(BF16))
        bl = b[CHUNK - 1 : CHUNK, :]
        qe = _heads((qf * jnp.exp(b)).astype(BF16))
        kd = _heads((kk * jnp.exp(bl - b)).astype(BF16))
        decay = _heads(jnp.exp(bl))
        o = [_dot(qe[j], sts[j].astype(BF16), NT) for j in hs]
        parts = [[] for _ in hs]
        for lo, hi, ea, eb, mask in _intra_blocks(b):
            a, bk = _heads((qf[lo:hi] * ea).astype(BF16)), _heads((kk[:hi] * eb).astype(BF16))
            p = [jnp.where(mask, _dot(a[j], bk[j], NT), 0.0).astype(BF16) for j in hs]
            for j in hs:
                parts[j].append(_dot(p[j], vb[j][:hi]))
        o = [o[j] + jnp.concatenate(parts[j], axis=0) for j in hs]
        new = [sts[j] * decay[j] + _dot(vb[j], kd[j], TN) for j in hs]
        hgn = hgn_ref[...]
        on = [o[j] * lax.rsqrt(jnp.mean(o[j] * o[j], axis=-1, keepdims=True) + EPS) * hgn for j in hs]
        gg = g_ref[...]
        for j in hs:
            st_ref[j] = sts[j]
            s_scr[j] = new[j]
        o_ref[...] = _wide(o)
        y_ref[...] = (_wide(on) * (gg * _sigmoid(gg))).astype(BF16)

    wide = HG_BLOCK * HEAD_DIM
    groups = HG_HEADS // HG_BLOCK

    def col(k):
        return pl.BlockSpec((CHUNK, wide), lambda g, c: (c, k * groups + g))

    out = pl.BlockSpec((CHUNK, wide), lambda g, c: (c, g))
    return pl.pallas_call(
        body,
        name="hgrn_fwd",
        grid=(groups, nc),
        in_specs=[col(0), col(1), col(2), col(3), pl.BlockSpec((2, wide), lambda g, c: (0, g)), pl.BlockSpec((1, HEAD_DIM), lambda g, c: (0, 0))],
        out_specs=[out, out, pl.BlockSpec((HG_BLOCK, None, HEAD_DIM, HEAD_DIM), lambda g, c: (g, c, 0, 0))],
        out_shape=[
            jax.ShapeDtypeStruct((t, HG_WIDTH + ATT_WIDTH), BF16),
            jax.ShapeDtypeStruct((t, HG_WIDTH), F32),
            jax.ShapeDtypeStruct((HG_HEADS, nc, HEAD_DIM, HEAD_DIM), F32),
        ],
        scratch_shapes=[pltpu.VMEM((HG_BLOCK, HEAD_DIM, HEAD_DIM), F32)],
        compiler_params=_params("arbitrary", "arbitrary"),
    )(proj, proj, proj, proj, lb_logits, hg_norm)


def _hgrn_bwd(proj, lb_logits, hg_norm, o_hg, dycat, states, dep=None):
    t = proj.shape[0]
    nc = t // CHUNK

    def body(q_ref, f_ref, i_ref, g_ref, lbl_ref, hgn_ref, o_ref, dy_ref, st_ref, dp_ref, dlbl_ref, dhgn_ref, dst_scr, dlb_scr):
        h = pl.program_id(0)
        c = pl.program_id(1)

        @pl.when(c == 0)
        def _():
            dst_scr[...] = jnp.zeros_like(dst_scr)
            dlb_scr[...] = jnp.zeros_like(dlb_scr)

        @pl.when((c == 0) & (h == 0))
        def _():
            dhgn_ref[...] = jnp.zeros_like(dhgn_ref)

        hs = range(HG_BLOCK)
        hgn = _wide([hgn_ref[...]] * HG_BLOCK)
        q, fp, gg, vi = q_ref[...], f_ref[...], g_ref[...], i_ref[...]
        lb, sig, f, kk, sq, qf, b = _hgrn_gates(q, fp, lbl_ref[...])
        o, dy = o_ref[...], dy_ref[...]
        sg = _sigmoid(gg)
        n = _wide([oh * lax.rsqrt(jnp.mean(oh * oh, axis=-1, keepdims=True) + EPS) for oh in _heads(o)])
        don = dy * (gg * sg)
        dgg = dy * (n * hgn) * (sg * (1.0 + gg * (1.0 - sg)))
        d_hgn = sum(_heads(jnp.sum(don * n, axis=0, keepdims=True)))
        dn = don * hgn
        do = _wide(
            [
                lax.rsqrt(jnp.mean(oh * oh, axis=-1, keepdims=True) + EPS) * (dnh - nh * jnp.mean(dnh * nh, axis=-1, keepdims=True))
                for oh, dnh, nh in zip(_heads(o), _heads(dn), _heads(n))
            ]
        )
        sts = [st_ref[j] for j in hs]
        dstn = [dst_scr[j] for j in hs]
        bl = b[CHUNK - 1 : CHUNK, :]
        e_b, e_bl, e_l = jnp.exp(b), jnp.exp(bl - b), jnp.exp(bl)
        doh, vih = _heads(do), _heads(vi)
        dobh = _heads(do.astype(BF16))
        dq_acc = _wide([_dot3(doh[j], sts[j]) for j in hs]) * e_b
        dk_inter = _wide([_dot3(vih[j], dstn[j]) for j in hs]) * e_bl
        dk_acc = dk_inter
        kd = _heads((kk * e_bl).astype(BF16))
        dv_acc = _wide([_dot(kd[j], dstn[j].astype(BF16), NT) for j in hs])
        qe, decay = _heads((qf * e_b).astype(BF16)), _heads(e_l)
        dst_new = [dstn[j] * decay[j] + _dot(dobh[j], qe[j], TN) for j in hs]
        db_last = e_l * _wide([jnp.sum(sts[j] * dstn[j], axis=0, keepdims=True) for j in hs]) + jnp.sum(kk * dk_inter, axis=0, keepdims=True)
        dq_parts = []
        for lo, hi, ea, eb, mask in _intra_blocks(b):
            a, bk = qf[lo:hi] * ea, kk[:hi] * eb
            ah, bkh = _heads(a), _heads(bk)
            abh, bkbh = _heads(a.astype(BF16)), _heads(bk.astype(BF16))
            p = [jnp.where(mask, _dot(abh[j], bkbh[j], NT), 0.0).astype(BF16) for j in hs]
            dp = [jnp.where(mask, _dot3(doh[j][lo:hi], vih[j][:hi], NT), 0.0) for j in hs]
            dq_parts.append(_wide([_dot3(dp[j], bkh[j]) for j in hs]) * ea)
            dki = _wide([_dot3(dp[j], ah[j], TN) for j in hs]) * eb
            dvi = _wide([_dot(p[j], dobh[j][lo:hi], TN) for j in hs])
            if hi < CHUNK:
                zeros = jnp.zeros((CHUNK - hi, HG_BLOCK * HEAD_DIM), F32)
                dki = jnp.concatenate([dki, zeros], axis=0)
                dvi = jnp.concatenate([dvi, zeros], axis=0)
            dk_acc = dk_acc + dki
            dv_acc = dv_acc + dvi
        dq_acc = dq_acc + jnp.concatenate(dq_parts, axis=0)
        rows = lax.broadcasted_iota(jnp.int32, dq_acc.shape, 0)
        db = qf * dq_acc - kk * dk_acc + jnp.where(rows == CHUNK - 1, db_last, 0.0)
        dlf = jnp.dot(_tri(CHUNK, True), db, precision=HIGHEST, preferred_element_type=F32)
        dfk = dlf / f - dk_acc
        for k, part in enumerate((dq_acc * (sq * (1.0 + q * (1.0 - sq))), (1.0 - lb) * dfk * sig * (1.0 - sig), dv_acc, dgg)):
            dp_ref[:, k * HG_WIDTH : (k + 1) * HG_WIDTH] = part.astype(BF16)
        dlb_scr[...] += jnp.sum(dfk * (1.0 - sig), axis=0, keepdims=True)
        dhgn_ref[...] += d_hgn
        for j in hs:
            dst_scr[j] = dst_new[j]

        @pl.when(c == nc - 1)
        def _():
            dl0 = dlb_scr[...] * lb * (1.0 - lb)
            dlbl_ref[0:1, :] = dl0
            dlbl_ref[1:2, :] = -dl0

    wide = HG_BLOCK * HEAD_DIM
    groups = HG_HEADS // HG_BLOCK

    def col(k):
        return pl.BlockSpec((CHUNK, wide), lambda g, c: (nc - 1 - c, k * groups + g))

    blk = pl.BlockSpec((CHUNK, wide), lambda g, c: (nc - 1 - c, g))
    assert groups == 1, "d(q, f, i, g) are written as one contiguous column range of the in_proj gradient"
    return _pallas(
        body,
        9,
        dep,
        name="hgrn_bwd",
        grid=(groups, nc),
        in_specs=[
            col(0), col(1), col(2), col(3),
            pl.BlockSpec((2, wide), lambda g, c: (0, g)),
            pl.BlockSpec((1, HEAD_DIM), lambda g, c: (0, 0)),
            blk, blk,
            pl.BlockSpec((HG_BLOCK, None, HEAD_DIM, HEAD_DIM), lambda g, c: (g, nc - 1 - c, 0, 0)),
        ],
        out_specs=[
            pl.BlockSpec((CHUNK, 4 * HG_WIDTH), lambda g, c: (nc - 1 - c, 0)),
            pl.BlockSpec((2, wide), lambda g, c: (0, g)),
            pl.BlockSpec((1, HEAD_DIM), lambda g, c: (0, 0)),
        ],
        out_shape=[
            jax.ShapeDtypeStruct((t, 4 * HG_WIDTH + 3 * ATT_WIDTH), BF16),
            jax.ShapeDtypeStruct((2, HG_WIDTH), F32),
            jax.ShapeDtypeStruct((1, HEAD_DIM), F32),
        ],
        scratch_shapes=[pltpu.VMEM((HG_BLOCK, HEAD_DIM, HEAD_DIM), F32), pltpu.VMEM((1, wide), F32)],
        compiler_params=_params("arbitrary", "arbitrary"),
    )(proj, proj, proj, proj, lb_logits, hg_norm, o_hg, dycat, states)


def _diagonal_slots(shift):
    i = lax.broadcasted_iota(jnp.int32, (N_REL_PAD, DIAG), 0)
    u = lax.broadcasted_iota(jnp.int32, (N_REL_PAD, DIAG), 1)
    offset = u - shift if shift else jnp.where(u < K_BLOCK, u, u - DIAG)
    return jnp.where(jnp.clip(PAD - offset, -REL_CLIP, REL_CLIP) + REL_CLIP == i, 1.0, 0.0).astype(BF16)


def _split3(x):
    hi = x.astype(BF16)
    mid = (x - hi.astype(F32)).astype(BF16)
    return hi, mid, (x - hi.astype(F32) - mid.astype(F32)).astype(BF16)


def _bias_table(rel_bias, dep=None):
    def body(rb_ref, o_ref, diag):
        h = pl.program_id(0)

        @pl.when(h == 0)
        def _():
            hi, mid, lo = _split3(rb_ref[...])
            slots = _diagonal_slots(0)
            diag[...] = _dot(hi, slots) + (_dot(mid, slots) + _dot(lo, slots))

        rows = jnp.broadcast_to(diag[pl.ds(h, 1), :], (Q_BLOCK, DIAG))
        row = lax.broadcasted_iota(jnp.int32, (Q_BLOCK, K_BLOCK), 0)
        col = lax.broadcasted_iota(jnp.int32, (Q_BLOCK, K_BLOCK), 1)
        first = row - (row & (CHUNK - 1))
        seen = (col >= first) & (col < first + BAND)
        o_ref[...] = jnp.where(seen, pltpu.roll(rows, 0, 1, stride=1, stride_axis=0)[:, :K_BLOCK], MASKED)

    return _pallas(
        body,
        1,
        dep,
        name="bias_table",
        grid=(ATT_HEADS,),
        in_specs=[pl.BlockSpec((ATT_HEADS, N_REL_PAD), lambda h: (0, 0))],
        out_specs=pl.BlockSpec((None, Q_BLOCK, K_BLOCK), lambda h: (h, 0, 0)),
        out_shape=jax.ShapeDtypeStruct((ATT_HEADS, Q_BLOCK, K_BLOCK), F32),
        scratch_shapes=[pltpu.VMEM((ATT_HEADS, DIAG), F32)],
        compiler_params=_params("arbitrary"),
    )(rel_bias)


def _att_probs(q_ref, kpad, bias_ref, blk):
    qs = (q_ref[...] * ATT_SCALE).astype(BF16)
    start = pl.multiple_of(blk * Q_BLOCK, Q_BLOCK)
    kb = kpad[pl.ds(start, K_BLOCK), :]
    s = _dot(qs, kb, NT) + bias_ref[...]
    col = lax.broadcasted_iota(jnp.int32, (Q_BLOCK, K_BLOCK), 1)
    s = jnp.where(col >= PAD - blk * Q_BLOCK, s, MASKED)
    e = jnp.exp(s - jnp.max(s, axis=-1, keepdims=True))
    return qs, kb, start, e * (1.0 / jnp.sum(e, axis=-1, keepdims=True))


def _fill_padded(dst, src):
    dst[0:PAD, :] = jnp.zeros((PAD, HEAD_DIM), BF16)
    dst[PAD:, :] = src[...].astype(BF16)


def _att_fwd(proj, bias, dep=None):
    t = proj.shape[0]
    nb = t // Q_BLOCK

    def body(q_ref, k_ref, v_ref, bias_ref, y_ref, kpad, vpad):
        c = pl.program_id(1)

        @pl.when(c == 0)
        def _():
            _fill_padded(kpad, k_ref)
            _fill_padded(vpad, v_ref)

        _, _, start, p = _att_probs(q_ref, kpad, bias_ref, c)
        y_ref[...] = _dot(p.astype(BF16), vpad[pl.ds(start, K_BLOCK), :]).astype(BF16)

    base = 4 * HG_HEADS
    return _pallas(
        body,
        4,
        dep,
        name="att_fwd",
        grid=(ATT_HEADS, nb),
        in_specs=[
            pl.BlockSpec((Q_BLOCK, HEAD_DIM), lambda h, c: (c, base + h)),
            pl.BlockSpec((t, HEAD_DIM), lambda h, c: (0, base + ATT_HEADS + h)),
            pl.BlockSpec((t, HEAD_DIM), lambda h, c: (0, base + 2 * ATT_HEADS + h)),
            pl.BlockSpec((None, Q_BLOCK, K_BLOCK), lambda h, c: (h, 0, 0)),
        ],
        out_specs=pl.BlockSpec((Q_BLOCK, HEAD_DIM), lambda h, c: (c, h)),
        out_shape=jax.ShapeDtypeStruct((t, ATT_WIDTH), BF16),
        scratch_shapes=[pltpu.VMEM((t + PAD, HEAD_DIM), BF16), pltpu.VMEM((t + PAD, HEAD_DIM), BF16)],
        compiler_params=_params("arbitrary", "arbitrary"),
    )(proj, proj, proj, bias)


def _att_bwd(proj, bias, dycat, dep=None):
    t = proj.shape[0]
    nb = t // Q_BLOCK

    def body(q_ref, k_ref, v_ref, bias_ref, dy_ref, dq_ref, dk_ref, dv_ref, g_ref, kpad, vpad, dkacc, dvacc):
        c = pl.program_id(1)

        @pl.when(c == 0)
        def _():
            _fill_padded(kpad, k_ref)
            _fill_padded(vpad, v_ref)
            dkacc[...] = jnp.zeros_like(dkacc)
            dvacc[...] = jnp.zeros_like(dvacc)
            g_ref[...] = jnp.zeros_like(g_ref)

        qs, kb, start, p = _att_probs(q_ref, kpad, bias_ref, c)
        band = pl.ds(start, K_BLOCK)
        dyb = dy_ref[...].astype(BF16)
        dvacc[band, :] += _dot(p.astype(BF16), dyb, TN)
        dp = _dot(dyb, vpad[band, :], NT)
        ds = p * (dp - jnp.sum(dp * p, axis=-1, keepdims=True))
        g_ref[...] += ds
        dsb = ds.astype(BF16)
        dq_ref[...] = (_dot(dsb, kb) * ATT_SCALE).astype(BF16)
        dkacc[band, :] += _dot(dsb, qs, TN)

        @pl.when(c == nb - 1)
        def _():
            dk_ref[...] = dkacc[PAD:, :].astype(BF16)
            dv_ref[...] = dvacc[PAD:, :].astype(BF16)

    base = 4 * HG_HEADS
    whole = pl.BlockSpec((t, HEAD_DIM), lambda h, c: (0, h))
    return _pallas(
        body,
        5,
        dep,
        name="att_bwd",
        grid=(ATT_HEADS, nb),
        in_specs=[
            pl.BlockSpec((Q_BLOCK, HEAD_DIM), lambda h, c: (c, base + h)),
            pl.BlockSpec((t, HEAD_DIM), lambda h, c: (0, base + ATT_HEADS + h)),
            pl.BlockSpec((t, HEAD_DIM), lambda h, c: (0, base + 2 * ATT_HEADS + h)),
            pl.BlockSpec((None, Q_BLOCK, K_BLOCK), lambda h, c: (h, 0, 0)),
            pl.BlockSpec((Q_BLOCK, HEAD_DIM), lambda h, c: (c, HG_HEADS + h)),
        ],
        out_specs=[pl.BlockSpec((Q_BLOCK, HEAD_DIM), lambda h, c: (c, h)), whole, whole, pl.BlockSpec((None, Q_BLOCK, K_BLOCK), lambda h, c: (h, 0, 0))],
        out_shape=[
            jax.ShapeDtypeStruct((t, ATT_WIDTH), BF16),
            jax.ShapeDtypeStruct((t, ATT_WIDTH), BF16),
            jax.ShapeDtypeStruct((t, ATT_WIDTH), BF16),
            jax.ShapeDtypeStruct((ATT_HEADS, Q_BLOCK, K_BLOCK), F32),
        ],
        scratch_shapes=[
            pltpu.VMEM((t + PAD, HEAD_DIM), BF16),
            pltpu.VMEM((t + PAD, HEAD_DIM), BF16),
            pltpu.VMEM((t + PAD, HEAD_DIM), F32),
            pltpu.VMEM((t + PAD, HEAD_DIM), F32),
        ],
        compiler_params=_params("arbitrary", "arbitrary"),
    )(proj, proj, proj, bias, dycat)


def _rel_bias_grad(gsum):
    def body(g_ref, o_ref):
        r = lax.broadcasted_iota(jnp.int32, (Q_BLOCK, Q_BLOCK), 0)
        c = lax.broadcasted_iota(jnp.int32, (Q_BLOCK, Q_BLOCK), 1)
        flip = jnp.where(r + c == Q_BLOCK - 1, 1.0, 0.0).astype(BF16)
        sums = []
        for h in range(ATT_HEADS):
            hi, mid, lo = _split3(g_ref[h])
            rev = _dot(flip, hi) + (_dot(flip, mid) + _dot(flip, lo))
            wide = jnp.concatenate([rev, jnp.zeros((Q_BLOCK, DIAG - K_BLOCK), F32)], axis=1)
            sums.append(jnp.sum(pltpu.roll(wide, 0, 1, stride=1, stride_axis=0), axis=0, keepdims=True))
        hi, mid, lo = _split3(jnp.concatenate(sums, axis=0))
        slots = _diagonal_slots(Q_BLOCK - 1)
        o_ref[...] = _dot(hi, slots, NT) + (_dot(mid, slots, NT) + _dot(lo, slots, NT))

    return pl.pallas_call(
        body,
        name="rel_bias_grad",
        out_shape=jax.ShapeDtypeStruct((ATT_HEADS, N_REL_PAD), F32),
        compiler_params=_params(),
    )(gsum)


HALO = 16


FF_TILE = 1408
FF_TILES = D_FF // FF_TILE


def _interleave_cols(a):
    lead = a.shape[:-1]
    return jnp.swapaxes(a.reshape(*lead, 2, FF_TILES, FF_TILE), -3, -2).reshape(*lead, 2 * D_FF)


def _deinterleave_cols(a):
    lead = a.shape[:-1]
    return jnp.swapaxes(a.reshape(*lead, FF_TILES, 2, FF_TILE), -3, -2).reshape(*lead, 2 * D_FF)


def _ffn_specs(t, tm):
    wide = 2 * FF_TILE
    tile = pl.BlockSpec((tm, wide), lambda j, i: (i, j))
    before = pl.BlockSpec((HALO, wide), lambda j, i: (jnp.maximum(i * (tm // HALO) - 1, 0), j))
    after = pl.BlockSpec((HALO, wide), lambda j, i: (jnp.minimum((i + 1) * (tm // HALO), t // HALO - 1), j))
    vec = lambda rows: pl.BlockSpec((rows, wide), lambda j, i: (0, j))
    return tile, before, after, vec


def _shifted(x, rows, offsets):
    r = lax.broadcasted_iota(jnp.int32, (rows, x.shape[0]), 0)
    c = lax.broadcasted_iota(jnp.int32, (rows, x.shape[0]), 1)
    pick = jnp.concatenate([jnp.where(c == r + o, 1.0, 0.0).astype(BF16) for o in offsets], axis=0)
    out = _dot(pick, x)
    return [out[k * rows : (k + 1) * rows] for k in range(len(offsets))]


def _conv(x, w, b, rows):
    taps = _shifted(x, rows, [HALO - 2, HALO - 1]) + [x[HALO : HALO + rows].astype(F32)]
    return b + w[0:1] * taps[0] + w[1:2] * taps[1] + w[2:3] * taps[2], taps


def _ffn_act_fwd(u, conv_w, conv_b):
    t = u.shape[0]
    tm = _tile(t, (128,))
    tile, before, _, vec = _ffn_specs(t, tm)

    def body(u_ref, h_ref, w_ref, b_ref, z_ref):
        first = pl.program_id(1) == 0
        halo = h_ref[...]
        x = jnp.concatenate([jnp.where(first, jnp.zeros_like(halo), halo), u_ref[...]], axis=0)
        c, _ = _conv(x, w_ref[...], b_ref[...], tm)
        gate, val = c[:, :FF_TILE], c[:, FF_TILE:]
        z_ref[...] = (gate * _sigmoid(gate) * val).astype(BF16)

    return pl.pallas_call(
        body,
        name="ffn_act_fwd",
        grid=(FF_TILES, t // tm),
        in_specs=[tile, before, vec(3), vec(1)],
        out_specs=pl.BlockSpec((tm, FF_TILE), lambda j, i: (i, j)),
        out_shape=jax.ShapeDtypeStruct((t, D_FF), BF16),
        compiler_params=_params("parallel", "parallel"),
    )(u, u, conv_w, conv_b)


def _ffn_act_bwd(u, dz, conv_w, conv_b, dep=None):
    t = u.shape[0]
    tm = _tile(t, (128,))
    nt = t // tm
    ext = tm + HALO
    tile, before, after, vec = _ffn_specs(t, tm)

    def body(u_ref, ub_ref, ua_ref, w_ref, b_ref, dz_ref, dza_ref, du_ref, dw_ref, db_ref):
        i = pl.program_id(1)
        first, last = i == 0, i == nt - 1
        ub, ua = ub_ref[...], ua_ref[...]
        parts = [jnp.where(first, jnp.zeros_like(ub), ub), u_ref[...], jnp.where(last, jnp.zeros_like(ua), ua)]
        w = w_ref[...]
        c, taps = _conv(jnp.concatenate(parts, axis=0), w, b_ref[...], ext)
        gate, val = c[:, :FF_TILE], c[:, FF_TILE:]
        dz = jnp.concatenate([dz_ref[...].astype(F32), jnp.where(last, 0.0, dza_ref[...].astype(F32))], axis=0)
        sg = _sigmoid(gate)
        d = jnp.concatenate([dz * val * (sg * (1.0 + gate * (1.0 - sg))), dz * (gate * sg)], axis=1)
        d1, d2 = _shifted(d.astype(BF16), tm, [1, 2])
        du_ref[...] = (w[2:3] * d[:tm] + w[1:2] * d1 + w[0:1] * d2).astype(BF16)

        @pl.when(first)
        def _():
            dw_ref[...] = jnp.zeros_like(dw_ref)
            db_ref[...] = jnp.zeros_like(db_ref)

        for k, tap in enumerate(taps):
            dw_ref[k : k + 1, :] += jnp.sum(d[:tm] * tap[:tm], axis=0, keepdims=True)
        db_ref[...] += jnp.sum(d[:tm], axis=0, keepdims=True)

    narrow = lambda rows, index: pl.BlockSpec((rows, FF_TILE), index)
    return _pallas(
        body,
        7,
        dep,
        name="ffn_act_bwd",
        grid=(FF_TILES, nt),
        in_specs=[
            tile, before, after, vec(3), vec(1),
            narrow(tm, lambda j, i: (i, j)),
            narrow(HALO, lambda j, i: (jnp.minimum((i + 1) * (tm // HALO), t // HALO - 1), j)),
        ],
        out_specs=[tile, vec(3), vec(1)],
        out_shape=[
            jax.ShapeDtypeStruct((t, 2 * D_FF), BF16),
            jax.ShapeDtypeStruct((3, 2 * D_FF), F32),
            jax.ShapeDtypeStruct((1, 2 * D_FF), F32),
        ],
        compiler_params=_params("parallel", "arbitrary"),
    )(u, u, u, conv_w, conv_b, dz, dz)


def _ple_loss(gpre, pp, h2, final_norm, target):
    t, d = h2.shape
    tm = _tile(t, (256,))

    def body(gp_ref, pp_ref, h_ref, g_ref, tg_ref, dh_ref, dgp_ref, dpp_ref, dg_ref, loss_ref):
        i = pl.program_id(0)
        gate = _sigmoid(gp_ref[...])
        ppv = pp_ref[...]
        h3 = h_ref[...] + gate * ppv
        r = lax.rsqrt(jnp.mean(h3 * h3, axis=-1, keepdims=True) + EPS)
        n = h3 * r
        g = g_ref[...]
        err = n * g - tg_ref[...]
        loss = 0.5 * jnp.sum(jnp.mean(err * err, axis=-1, keepdims=True))
        dy = err * (1.0 / d)
        dn = dy * g
        dh = r * (dn - n * jnp.mean(dn * n, axis=-1, keepdims=True))
        dh_ref[...] = dh
        dgp_ref[...] = (dh * ppv * gate * (1.0 - gate)).astype(BF16)
        dpp_ref[...] = (dh * gate).astype(BF16)
        dg = jnp.sum(dy * n, axis=0, keepdims=True)

        @pl.when(i == 0)
        def _():
            dg_ref[...] = dg
            loss_ref[...] = jnp.full(loss_ref.shape, loss, F32)

        @pl.when(i > 0)
        def _():
            dg_ref[...] += dg
            loss_ref[...] += loss

    row = pl.BlockSpec((tm, d), lambda i: (i, 0))
    vec = pl.BlockSpec((1, d), lambda i: (0, 0))
    return pl.pallas_call(
        body,
        name="ple_loss",
        grid=(t // tm,),
        in_specs=[row, row, row, vec, row],
        out_specs=[row, row, row, vec, pl.BlockSpec((8, 128), lambda i: (0, 0))],
        out_shape=[
            jax.ShapeDtypeStruct((t, d), F32),
            jax.ShapeDtypeStruct((t, d), BF16),
            jax.ShapeDtypeStruct((t, d), BF16),
            jax.ShapeDtypeStruct((1, d), F32),
            jax.ShapeDtypeStruct((8, 128), F32),
        ],
        compiler_params=_params("arbitrary"),
    )(gpre, pp, h2, final_norm, target)


def _adamw(w, g, m, v):
    m = ADAM_B1 * m + (1.0 - ADAM_B1) * g
    v = ADAM_B2 * v + (1.0 - ADAM_B2) * (g * g)
    m_hat = m / (1.0 - ADAM_B1 ** ADAM_STEP)
    v_hat = v / (1.0 - ADAM_B2 ** ADAM_STEP)
    return -ADAM_LR * (m_hat / (jnp.sqrt(v_hat) + ADAM_EPS) + ADAM_WD * w), m, v


def _adam_big(w, m, v, own, recv, name, dep=None):
    r, c = w.shape
    tr = _tile(r, (256, 176))

    def body(w_ref, m_ref, v_ref, own_ref, recv_ref, g_ref, d_ref, nm_ref, nv_ref):
        g = own_ref[...]
        for k in range(3):
            g = g + recv_ref[k].astype(F32)
        g_ref[...] = g
        d_ref[...], nm_ref[...], nv_ref[...] = _adamw(w_ref[...], g, m_ref[...], v_ref[...])

    blk = pl.BlockSpec((tr, c), lambda i: (i, 0))
    return _pallas(
        body,
        5,
        dep,
        name=name,
        grid=(r // tr,),
        in_specs=[blk, blk, blk, blk, pl.BlockSpec((3, tr, c), lambda i: (0, i, 0))],
        out_specs=[blk] * 4,
        out_shape=[jax.ShapeDtypeStruct((r, c), F32)] * 4,
        compiler_params=_params("parallel"),
    )(w, m, v, own, recv)


def _adam_small(w, g, m, v):
    def body(w_ref, g_ref, m_ref, v_ref, d_ref, nm_ref, nv_ref):
        d_ref[...], nm_ref[...], nv_ref[...] = _adamw(w_ref[...], g_ref[...], m_ref[...], v_ref[...])

    return pl.pallas_call(body, name="adam_small", out_shape=[jax.ShapeDtypeStruct(w.shape, F32)] * 3, compiler_params=_params())(w, g, m, v)


def _position():
    return lax.axis_index("x"), lax.axis_index("y"), lax.axis_index("c")


def _other_chips(x, y):
    return [(1 - x, y), (x, 1 - y), (1 - x, 1 - y)]


def _block_index(dev, interleaved):
    x, y, c = dev
    return 4 * y + 2 * c + x if interleaved else 4 * x + 2 * y + c


def _shard_of(ref, axis, size, dev, interleaved=False):
    start = pl.multiple_of(_block_index(dev, interleaved) * size, 128 if axis == 1 else 16)
    return ref.at[:, pl.ds(start, size)] if axis == 1 else ref.at[pl.ds(start, size), :]


def _add_blocks(ids, grad, landed, axis, size, targets, out_dtype, name):
    rows = size if axis == 0 else grad.shape[0]
    cols = size if axis == 1 else grad.shape[1]
    tr = _tile(rows, (256, 176))
    nr = rows // tr
    nt = len(targets)

    def body(ids_ref, g_ref, l_ref, o_ref):
        o_ref[...] = (g_ref[...] + l_ref[...]).astype(out_dtype)

    if axis == 1:
        g_spec = pl.BlockSpec((tr, cols), lambda k, i, ids: (i, ids[targets[0] + k]))
    else:
        g_spec = pl.BlockSpec((tr, cols), lambda k, i, ids: (ids[targets[0] + k] * nr + i, 0))
    return pl.pallas_call(
        body,
        name=name,
        grid_spec=pltpu.PrefetchScalarGridSpec(
            num_scalar_prefetch=1,
            grid=(nt, nr),
            in_specs=[g_spec, pl.BlockSpec((None, tr, cols), lambda k, i, ids: (ids[4 + targets[0] + k], i, 0))],
            out_specs=pl.BlockSpec((None, tr, cols), lambda k, i, ids: (k, i, 0)),
        ),
        out_shape=jax.ShapeDtypeStruct((nt, rows, cols), out_dtype),
        compiler_params=_params("parallel", "parallel"),
    )(ids, grad, landed)


def _all_reduce_small(vec, name):
    rows = vec.shape[0]

    def body(v_ref, o_ref, land, send_sems, recv_sems):
        x, y, c = _position()
        mine = 4 * x + 2 * y + c
        copies = []
        for mask in range(1, N_DEV):
            peer = (1 - x if mask & 4 else x, 1 - y if mask & 2 else y, 1 - c if mask & 1 else c)
            copies.append(
                pltpu.make_async_remote_copy(
                    src_ref=v_ref, dst_ref=land.at[mine], send_sem=send_sems.at[mask - 1], recv_sem=recv_sems.at[mask - 1], device_id=peer, device_id_type=MESH
                )
            )
        for cp in copies:
            cp.start()
        land[mine] = v_ref[...]
        for cp in copies:
            cp.wait()
        acc = land[0]
        for k in range(1, N_DEV):
            acc = acc + land[k]
        o_ref[...] = acc

    return pl.pallas_call(
        body,
        name=name,
        out_shape=jax.ShapeDtypeStruct(vec.shape, F32),
        in_specs=[pl.BlockSpec(memory_space=pltpu.VMEM)],
        out_specs=pl.BlockSpec(memory_space=pltpu.VMEM),
        scratch_shapes=[pltpu.VMEM((N_DEV, rows, 128), F32), pltpu.SemaphoreType.DMA((N_DEV - 1,)), pltpu.SemaphoreType.DMA((N_DEV - 1,))],
    )(vec)


def _rows128(a, rows):
    flat = a.reshape(-1)
    return jnp.pad(flat, (0, rows * 128 - flat.shape[0])).reshape(rows, 128)


def _pad_rel(a):
    return jnp.pad(a.reshape(ATT_HEADS, -1)[:, :N_REL], ((0, 0), (0, N_REL_PAD - N_REL)))


SMALL = [("norm_mix", 16), ("lb_logits", 16), ("hg_norm", 8), ("rel_bias", 24), ("norm_ffn", 16), ("conv_b", 88), ("norm_ple", 16), ("final_norm", 16)]
CONV_W_FULL_ROWS = 3 * 2 * D_FF // 128
CONV_W_SHARD_ROWS = 40


def _pack_small(parts):
    return jnp.concatenate([_rows128(_pad_rel(parts[k]) if k == "rel_bias" else parts[k], rows) for k, rows in SMALL], axis=0)


def _unpack_small(packed, shapes):
    out, at = {}, 0
    for k, rows in SMALL:
        blk = packed[at : at + rows]
        at += rows
        if k == "rel_bias":
            out[k] = blk.reshape(ATT_HEADS, N_REL_PAD)[:, :N_REL].reshape(shapes[k])
        else:
            n = 1
            for s in shapes[k]:
                n *= s
            out[k] = blk.reshape(-1)[:n].reshape(shapes[k])
    return out, at


BIG = [("w_in", 1), ("w_out", 0), ("w_up", 1), ("w_down", 0), ("w_ple_gate", 0), ("w_ple_proj", 1)]


HBM = pl.BlockSpec(memory_space=pltpu.HBM)
SEM = pl.BlockSpec(memory_space=pltpu.SEMAPHORE)
EFFECT = pltpu.SideEffectType.DATAFLOW_SIDE_EFFECTING


def _copies(plan, refs, send_sems, recv_sems):
    return [
        pltpu.make_async_remote_copy(src_ref=src, dst_ref=dst, send_sem=send_sems.at[i], recv_sem=recv_sems.at[i], device_id=dev, device_id_type=MESH)
        for i, (src, dst, dev) in enumerate(plan(refs))
    ]


def _split_start(name, arrays, plan, n):
    k = len(arrays)

    def body(*refs):
        for cp in _copies(plan, refs[:k], refs[k], refs[k + 1]):
            cp.start()
        refs[-1][...] = jnp.zeros_like(refs[-1])

    out = pl.pallas_call(
        body,
        name=name,
        out_shape=(pltpu.SemaphoreType.DMA((n,)), pltpu.SemaphoreType.DMA((n,)), *[pltpu.HBM(a.shape, a.dtype) for a in arrays], jax.ShapeDtypeStruct((8, 128), F32)),
        in_specs=[HBM] * k,
        out_specs=(SEM, SEM, *[HBM] * k, pl.BlockSpec(memory_space=pltpu.VMEM)),
        input_output_aliases={i: 2 + i for i in range(k)},
        compiler_params=pltpu.CompilerParams(has_side_effects=EFFECT),
    )(*[pltpu.with_memory_space_constraint(a, pltpu.HBM) for a in arrays])
    return out[0], out[1], list(out[2 : 2 + k]), out[-1]


def _split_wait(name, send, recv, arrays, plan, after):
    k = len(arrays)

    def body(*refs):
        for cp in _copies(plan, refs[:k], refs[k], refs[k + 1]):
            cp.wait_send()
            cp.wait_recv()

    out = pl.pallas_call(
        body,
        name=name,
        out_shape=tuple(pltpu.HBM(a.shape, a.dtype) for a in arrays),
        in_specs=[HBM] * k + [SEM, SEM, ANY],
        out_specs=tuple([HBM] * k),
        input_output_aliases={i: i for i in range(k)},
        compiler_params=pltpu.CompilerParams(has_side_effects=EFFECT),
    )(*arrays, send, recv, after)
    return list(out)


def _cast_into(w, me, axis, name, dep, dtype):
    r, c = w.shape
    tr = _tile(r, (256, 176))
    nr = r // tr
    deps = [] if dep is None else [dep]

    def body(me_ref, w_ref, *rest):
        rest[-1][...] = w_ref[...].astype(dtype)

    if axis == 1:
        shape, o_spec = (r, N_DEV * c), pl.BlockSpec((tr, c), lambda i, me: (i, me[0]))
    else:
        shape, o_spec = (N_DEV * r, c), pl.BlockSpec((tr, c), lambda i, me: (me[0] * nr + i, 0))
    return pl.pallas_call(
        body,
        name=name,
        grid_spec=pltpu.PrefetchScalarGridSpec(
            num_scalar_prefetch=1, grid=(nr,), in_specs=[pl.BlockSpec((tr, c), lambda i, me: (i, 0))] + [ANY] * len(deps), out_specs=o_spec
        ),
        out_shape=jax.ShapeDtypeStruct(shape, dtype),
        compiler_params=_params("parallel"),
    )(me, w, *deps)


GATHER = [
    (["w_in"], None, "norm_mix_fwd", "bias_table"),
    (["w_out"], "norm_mix_fwd", "att_fwd", None),
    (["w_up", "conv_w"], "norm_mix_fwd", "att_fwd", "norm_ffn_fwd"),
    (["w_down", "w_ple_gate", "w_ple_proj"], "att_fwd", "up_proj", "ffn_act_fwd"),
]
GROUPS = [["w_ple_proj", "w_ple_gate", "w_down"], ["w_up"], ["w_out"], ["w_in"]]
STAGES = ["ffn_act_bwd", "d_mix_out", "hgrn_bwd", "d_norm_mix_out"]
INTERLEAVED = {"w_up", "conv_w"}


class _Exchange:
    def __init__(self, big, conv_w, position):
        self.big, self.axis = big, dict(BIG, conv_w=1)
        self.shards = dict(big, conv_w=conv_w)
        self.size = {k: w.shape[self.axis[k]] for k, w in self.shards.items()}
        self.x, self.y, self.c = position
        chips = [(self.x, self.y)] + _other_chips(self.x, self.y)
        landed = [2 * cx + cy for cx, cy in chips]
        self.ids = {
            flag: jnp.stack([_block_index((cx, cy, self.c), flag) for cx, cy in chips] + landed).astype(jnp.int32) for flag in (False, True)
        }
        self.tokens, self.grads, self.state, self.wfull = [], {}, {}, {}


    def _slot(self, ref, k, dev):
        return _shard_of(ref, self.axis[k], self.size[k], dev, interleaved=k in INTERLEAVED)

    def _plan_gather(self, names, direct, refs):
        x, y, c = _position()
        me, out = (x, y, c), []
        for k, ref in zip(names, refs):
            mine = self._slot(ref, k, me)
            out.append((mine, mine, (x, y, 1 - c)))
            out += [(mine, mine, (*chip, c)) for chip in _other_chips(x, y)]
            if direct:
                out += [(mine, mine, (*chip, 1 - c)) for chip in _other_chips(x, y)]
        return out

    def _plan_forward(self, names, refs):
        x, y, c = _position()
        out = []
        for k, ref in zip(names, refs):
            for chip in _other_chips(x, y):
                block = self._slot(ref, k, (*chip, c))
                out.append((block, block, (x, y, 1 - c)))
        return out

    def _plan_sibling(self, names, refs):
        x, y, c = _position()
        n = len(names)
        return [(self._slot(refs[i], k, (p // 2, p % 2, 1 - c)), refs[n + i].at[p], (x, y, 1 - c)) for i, k in enumerate(names) for p in range(4)]

    def _plan_chips(self, names, refs):
        x, y, c = _position()
        n = len(names)
        return [(refs[i].at[j], refs[n + i].at[j], (*chip, c)) for i in range(n) for j, chip in enumerate(_other_chips(x, y))]


    def gather(self):
        me = {flag: _block_index((self.x, self.y, self.c), flag).astype(jnp.int32).reshape(1) for flag in (False, True)}
        self.late, self.unsent = {}, {}
        after = None
        for gi, (names, issued, *_) in enumerate(GATHER):
            self.unsent[gi] = [
                _cast_into(self.shards[k], me[k in INTERLEAVED], self.axis[k], "cast_" + k, after, F32 if k == "conv_w" else BF16) for k in names
            ]
            if issued is None:
                self._issue(None)
                after = self.tokens[-1]
        self.tokens += [a for arrays in self.unsent.values() for a in arrays]

    def _issue(self, stage):
        for gi, (names, issued, _, forwarded) in enumerate(GATHER):
            if issued == stage and gi in self.unsent:
                plan = functools.partial(self._plan_gather, names, forwarded is None)
                copies = (7 if forwarded is None else 4) * len(names)
                send, recv, fulls, token = _split_start(f"gather_start_{gi}", self.unsent.pop(gi), plan, copies)
                self.tokens.append(token)
                self.late[gi] = (send, recv, fulls, plan)

    def weight(self, k):
        return self.wfull[k]

    def dep(self):
        tokens, self.tokens = self.tokens, []
        return tokens

    def reduce(self, vec, name):
        return _all_reduce_small(vec, name)

    def grad(self, k, g):
        self.grads[k] = g
        for gi, names in enumerate(GROUPS):
            if k == names[-1]:
                plan = functools.partial(self._plan_sibling, names)
                lands = [lax.empty((4, *self._shard_shape(n)), F32) for n in names]
                send, recv, arrays, token = _split_start(f"sibling_start_{gi}", [self.grads[n] for n in names] + lands, plan, 4 * len(names))
                self.tokens.append(token)
                self.state[gi] = (send, recv, arrays, plan)

    def done(self, stage, after):
        for gi, (names, _, _, forwarded) in enumerate(GATHER):
            if forwarded == stage:
                send, recv, fulls, plan = self.late[gi]
                self.wfull.update(zip(names, _split_wait(f"forward_wait_{gi}", send, recv, fulls, plan, after)))
        for gi, (names, _, arrived, forwarded) in enumerate(GATHER):
            if arrived == stage:
                send, recv, fulls, plan = self.late[gi]
                fulls = _split_wait(f"gather_wait_{gi}", send, recv, fulls, plan, after)
                if forwarded is None:
                    self.wfull.update(zip(names, fulls))
                else:
                    plan = functools.partial(self._plan_forward, names)
                    send, recv, fulls, token = _split_start(f"forward_start_{gi}", fulls, plan, 3 * len(names))
                    self.tokens.append(token)
                    self.late[gi] = (send, recv, fulls, plan)
        self._issue(stage)
        if stage in STAGES:
            self._to_chips(STAGES.index(stage), after)

    def _shard_shape(self, k):
        shape = list(self.grads[k].shape)
        shape[self.axis[k]] = self.size[k]
        return tuple(shape)

    def _to_chips(self, gi, after):
        names = GROUPS[gi]
        n = len(names)
        send, recv, arrays, plan = self.state[gi]
        arrays = _split_wait(f"sibling_wait_{gi}", send, recv, arrays, plan, after)
        own, parts = [], []
        for k, g, land in zip(names, arrays[:n], arrays[n:]):
            ids = self.ids[k in INTERLEAVED]
            own.append(_add_blocks(ids, g, land, self.axis[k], self.size[k], [0], F32, "add_own_" + k)[0])
            parts.append(_add_blocks(ids, g, land, self.axis[k], self.size[k], [1, 2, 3], BF16, "add_send_" + k))
        plan = functools.partial(self._plan_chips, names)
        lands = [lax.empty(part.shape, BF16) for part in parts]
        send, recv, arrays, token = _split_start(f"chips_start_{gi}", parts + lands, plan, 3 * n)
        self.tokens.append(token)
        self.state[gi] = (send, recv, arrays, plan, own)

    def finish(self, gi, after):
        names = GROUPS[gi]
        send, recv, arrays, plan, own = self.state[gi]
        arrays = _split_wait(f"chips_wait_{gi}", send, recv, arrays, plan, after)
        return {k: (o, r) for k, o, r in zip(names, own, arrays[len(names) :])}


class _Resident:
    def __init__(self, wfull):
        self.wfull, self.grads = wfull, {}

    def weight(self, k):
        return self.wfull[k]

    def grad(self, k, g):
        self.grads[k] = g

    def dep(self):
        return None

    def reduce(self, vec, name):
        return vec

    def done(self, stage, after):
        pass


def _local_step(x, p, target, small, ex):
    a1, r1 = _rms_fwd(x, small["norm_mix"], "norm_mix_fwd", dep=ex.dep())
    ex.done("norm_mix_fwd", a1)
    bias = _bias_table(jnp.pad(small["rel_bias"], ((0, 0), (0, N_REL_PAD - N_REL))), dep=ex.dep())
    ex.done("bias_table", bias)
    proj = _matmul(a1, ex.weight("w_in"), "nn", F32, "in_proj", dep=ex.dep())
    y_hg, o_hg, states = _hgrn_fwd(proj, small["lb_logits"], small["hg_norm"])
    y_att = _att_fwd(proj, bias, dep=ex.dep())
    ycat = lax.dynamic_update_slice(y_hg, y_att, (0, HG_WIDTH))
    ex.done("att_fwd", ycat)
    h1 = _matmul(ycat, ex.weight("w_out"), "nn", F32, "out_proj", resid=x, dep=ex.dep())
    a2, r2 = _rms_fwd(h1, small["norm_ffn"], "norm_ffn_fwd")
    ex.done("norm_ffn_fwd", a2)
    conv_w = ex.weight("conv_w")
    u = _matmul(a2, ex.weight("w_up"), "nn", BF16, "up_proj")
    conv_b = _interleave_cols(small["conv_b"])
    ex.done("up_proj", u)
    z = _ffn_act_fwd(u, conv_w, conv_b)
    ex.done("ffn_act_fwd", z)
    h2 = _matmul(z, ex.weight("w_down"), "nn", F32, "down_proj", tk=2816, resid=h1)
    a3, r3 = _rms_fwd(h2, small["norm_ple"], "norm_ple_fwd")
    gpre = _matmul(a3, ex.weight("w_ple_gate"), "nn", F32, "ple_gate")
    pp = _matmul(p, ex.weight("w_ple_proj"), "nn", F32, "ple_proj")
    dh3, dgpre, dpp, d_final, loss = _ple_loss(gpre, pp, h2, small["final_norm"], target)

    ex.grad("w_ple_proj", _matmul(p, dpp, "tn", F32, "d_w_ple_proj", tm=512))
    ex.grad("w_ple_gate", _matmul(a3, dgpre, "tn", F32, "d_w_ple_gate", tm=512))
    da3 = _matmul(dgpre, ex.weight("w_ple_gate"), "nt", F32, "d_norm_ple_out")
    dh2, d_ple = _rms_bwd(da3, h2, r3, small["norm_ple"], dh3, "norm_ple_bwd")
    dz = _matmul(dh2, ex.weight("w_down"), "nt", BF16, "d_ffn_act")
    ex.grad("w_down", _matmul(z, dh2, "tn", F32, "d_w_down", tm=512))
    du, dcw, dcb = _ffn_act_bwd(u, dz, conv_w, conv_b, dep=ex.dep())
    ex.done("ffn_act_bwd", du)
    d_conv_w, d_conv_b = _deinterleave_cols(dcw), _deinterleave_cols(dcb)
    ex.grad("w_up", _matmul(a2, du, "tn", F32, "d_w_up", tm=512, dep=ex.dep()))
    da2 = _matmul(du, ex.weight("w_up"), "nt", F32, "d_norm_ffn_out", tk=2816, dep=ex.dep())
    dh1, d_ffn = _rms_bwd(da2, h1, r2, small["norm_ffn"], dh2, "norm_ffn_bwd")
    dycat = _matmul(dh1, ex.weight("w_out"), "nt", F32, "d_mix_out")
    ex.done("d_mix_out", dycat)
    ex.grad("w_out", _matmul(ycat, dh1, "tn", F32, "d_w_out", tk=2048, dep=ex.dep()))
    dp_hg, d_lb, d_hgn = _hgrn_bwd(proj, small["lb_logits"], small["hg_norm"], o_hg, dycat, states, dep=ex.dep())
    ex.done("hgrn_bwd", d_lb)
    dq_att, dk_att, dv_att, gsum = _att_bwd(proj, bias, dycat, dep=ex.dep())
    d_rel = _rel_bias_grad(gsum)
    d_small = {
        "norm_mix": jnp.zeros_like(small["norm_mix"]), "lb_logits": d_lb, "hg_norm": d_hgn, "rel_bias": d_rel, "norm_ffn": d_ffn,
        "conv_b": d_conv_b, "norm_ple": d_ple, "final_norm": d_final,
    }
    packed = jnp.concatenate([_pack_small(d_small), _rows128(d_conv_w, CONV_W_FULL_ROWS), _rows128(loss[0:1, 0:1], 8)], axis=0)
    early = ex.reduce(packed, "all_reduce_small")
    dproj = dp_hg
    for k, part in enumerate((dq_att, dk_att, dv_att)):
        dproj = lax.dynamic_update_slice(dproj, part, (0, 4 * HG_WIDTH + k * ATT_WIDTH))
    ex.grad("w_in", _matmul(a1, dproj, "tn", F32, "d_w_in", tm=512, dep=[early]))
    da1 = _matmul(dproj, ex.weight("w_in"), "nt", F32, "d_norm_mix_out", tk=1792, dep=ex.dep())
    dx, d_mix = _rms_bwd(da1, x, r1, small["norm_mix"], dh1, "norm_mix_bwd")
    rows = dict(SMALL)["norm_mix"]
    late = ex.reduce(_rows128(d_mix, rows), "all_reduce_norm_mix")
    ex.done("d_norm_mix_out", late)
    return dx, jnp.concatenate([late, early[rows:]], axis=0)


def kernel(x, p, norm_mix, w_in, lb_logits, hg_norm, rel_bias, w_out, norm_ffn, w_up, conv_w, conv_b, w_down, norm_ple, w_ple_gate, w_ple_proj, final_norm, loss_target, m_norm_mix, m_w_in, m_lb_logits, m_hg_norm, m_rel_bias, m_w_out, m_norm_ffn, m_w_up, m_conv_w, m_conv_b, m_w_down, m_norm_ple, m_w_ple_gate, m_w_ple_proj, m_final_norm, v_norm_mix, v_w_in, v_lb_logits, v_hg_norm, v_rel_bias, v_w_out, v_norm_ffn, v_w_up, v_conv_w, v_conv_b, v_w_down, v_norm_ple, v_w_ple_gate, v_w_ple_proj, v_final_norm):
    given = dict(locals())
    mx, my, mc = _position()
    me = 4 * mx + 2 * my + mc
    big = {k: given[k][0] for k, _ in BIG}
    ex = _Exchange(big, conv_w[0], (mx, my, mc))
    ex.gather()

    small = {
        "norm_mix": norm_mix, "lb_logits": lb_logits, "hg_norm": hg_norm, "rel_bias": rel_bias[0], "norm_ffn": norm_ffn,
        "conv_b": conv_b, "norm_ple": norm_ple, "final_norm": final_norm.reshape(1, -1),
    }
    dx, reduced = _local_step(x[0], p[0, 0], loss_target[0], small, ex)

    out = {}
    shapes = {k: given[k].shape for k, _ in SMALL}
    g_small, at = _unpack_small(reduced, shapes)
    g_conv_full = reduced[at : at + CONV_W_FULL_ROWS].reshape(3, 2 * D_FF)
    total_loss = reduced[at + CONV_W_FULL_ROWS, 0]
    cw = conv_w.shape[2]
    g_conv = lax.dynamic_slice_in_dim(g_conv_full, me * cw, cw, axis=1)

    def pack_with_conv(parts, conv_part):
        return jnp.concatenate([_pack_small(parts), _rows128(conv_part, CONV_W_SHARD_ROWS)], axis=0)

    d_pk, m_pk, v_pk = _adam_small(
        pack_with_conv({k: given[k] for k, _ in SMALL}, conv_w),
        pack_with_conv(g_small, g_conv),
        pack_with_conv({k: given["m_" + k] for k, _ in SMALL}, m_conv_w),
        pack_with_conv({k: given["v_" + k] for k, _ in SMALL}, v_conv_w),
    )
    for name, pk in (("d", d_pk), ("m", m_pk), ("v", v_pk)):
        parts, at = _unpack_small(pk, shapes)
        parts["conv_w"] = pk[at : at + CONV_W_SHARD_ROWS].reshape(-1)[: 3 * cw].reshape(conv_w.shape)
        for k, a in parts.items():
            out.setdefault(k, {})
            out[k][name] = a
    for k, _ in SMALL:
        out[k]["g"] = g_small[k]
    out["conv_w"]["g"] = g_conv.reshape(conv_w.shape)

    after, started = v_pk, ex.dep()
    for gi in range(len(GROUPS)):
        for k, (o, r) in ex.finish(gi, after).items():
            g, d, nm, nv = _adam_big(big[k], given["m_" + k][0], given["v_" + k][0], o, r, "adam_" + k, dep=started)
            out[k] = tuple(a[None] for a in (g, d, nm, nv))
            after = nv

    order = ["norm_mix", "w_in", "lb_logits", "hg_norm", "rel_bias", "w_out", "norm_ffn", "w_up", "conv_w", "conv_b", "w_down", "norm_ple", "w_ple_gate", "w_ple_proj", "final_norm"]

    def pick(k, what):
        return out[k][what] if isinstance(out[k], dict) else out[k][{"g": 0, "d": 1, "m": 2, "v": 3}[what]]

    return (total_loss, dx[None], *[pick(k, "g") for k in order], *[pick(k, "d") for k in order], *[pick(k, "m") for k in order], *[pick(k, "v") for k in order])
```

```python
import functools

import jax
import jax.numpy as jnp
from jax import lax
from jax.experimental import pallas as pl
from jax.experimental.pallas import tpu as pltpu

F32 = jnp.float32
BF16 = jnp.bfloat16

D_MODEL = 2048
CHUNK = 64
HG_HEADS = 8
HEAD_DIM = 128
HG_WIDTH = HG_HEADS * HEAD_DIM
ATT_HEADS = 8
ATT_WIDTH = ATT_HEADS * HEAD_DIM
LEFT_CHUNKS = 8
PAD = LEFT_CHUNKS * CHUNK
BAND = PAD + CHUNK
REL_CLIP = 128
N_REL = 2 * REL_CLIP + 1
N_REL_PAD = 384
D_FF = 5632
EPS = 1e-6
ATT_SCALE = HEAD_DIM ** -0.5
SUB = 32
HG_BLOCK = 8
Q_BLOCK = 4 * CHUNK
K_BLOCK = Q_BLOCK + PAD
DIAG = 1024
MASKED = -1e30

ADAM_LR = 0.001
ADAM_B1 = 0.9
ADAM_B2 = 0.999
ADAM_EPS = 1e-08
ADAM_WD = 0.01
ADAM_STEP = 10

N_DEV = 8
VMEM_LIMIT = 48 * 1024 * 1024
MESH = pl.DeviceIdType.MESH
ANY = pl.BlockSpec(memory_space=pl.ANY)
HIGHEST = lax.Precision.HIGHEST

NN = (((1,), (0,)), ((), ()))
NT = (((1,), (1,)), ((), ()))
TN = (((0,), (0,)), ((), ()))


def _params(*sem):
    return pltpu.CompilerParams(dimension_semantics=sem if sem else None, vmem_limit_bytes=VMEM_LIMIT)


def _pallas(body, n_in, dep, **kw):
    deps = [] if dep is None else list(dep)
    if not deps:
        return pl.pallas_call(body, **kw)

    def body_after(*refs):
        body(*refs[:n_in], *refs[n_in + len(deps) :])

    call = pl.pallas_call(body_after, **dict(kw, in_specs=list(kw["in_specs"]) + [ANY] * len(deps)))
    return lambda *ops: call(*ops, *deps)


def _dot(a, b, dims=NN):
    return lax.dot_general(a, b, dims, preferred_element_type=F32)


def _dot3(a, b, dims=NN):
    a_hi, b_hi = a.astype(BF16), b.astype(BF16)
    a_lo, b_lo = (a - a_hi.astype(F32)).astype(BF16), (b - b_hi.astype(F32)).astype(BF16)
    return _dot(a_hi, b_hi, dims) + (_dot(a_hi, b_lo, dims) + _dot(a_lo, b_hi, dims))


def _sigmoid(x):
    return 1.0 / (1.0 + jnp.exp(-x))


def _tile(n, prefs):
    for t in prefs:
        if n % t == 0:
            return t
    return n


def _matmul(a, b, mode, out_dtype, name, tm=1024, tn=1024, tk=None, resid=None, dep=None):
    if mode == "nn":
        (m, k), n = a.shape, b.shape[1]
    elif mode == "nt":
        (m, k), n = a.shape, b.shape[0]
    else:
        (k, m), n = a.shape, b.shape[1]
    tm = _tile(m, (tm, 512, 256, 128))
    tn = _tile(n, (tn, 1408, 512, 256, 128))
    tk = k if tk is None else _tile(k, (tk,))
    nk = k // tk
    dims = {"nn": NN, "nt": NT, "tn": TN}[mode]
    a_spec = pl.BlockSpec((tk, tm), lambda i, j, s: (s, i)) if mode == "tn" else pl.BlockSpec((tm, tk), lambda i, j, s: (i, s))
    b_spec = pl.BlockSpec((tn, tk), lambda i, j, s: (j, s)) if mode == "nt" else pl.BlockSpec((tk, tn), lambda i, j, s: (s, j))
    o_spec = pl.BlockSpec((tm, tn), lambda i, j, s: (i, j))
    has_res = resid is not None

    def body(*refs):
        a_ref, b_ref = refs[0], refs[1]
        o_ref = refs[2 + has_res]
        part = _dot(a_ref[...].astype(BF16), b_ref[...].astype(BF16), dims)

        def finish(acc):
            if has_res:
                acc = acc + refs[2][...]
            o_ref[...] = acc.astype(out_dtype)

        if nk == 1:
            finish(part)
        else:
            acc_ref = refs[-1]
            s = pl.program_id(2)

            @pl.when(s == 0)
            def _():
                acc_ref[...] = part

            @pl.when(s > 0)
            def _():
                acc_ref[...] += part

            @pl.when(s == nk - 1)
            def _():
                finish(acc_ref[...])

    return _pallas(
        body,
        2 + has_res,
        dep,
        name=name,
        grid=(m // tm, n // tn, nk),
        in_specs=[a_spec, b_spec] + ([o_spec] if has_res else []),
        out_specs=o_spec,
        out_shape=jax.ShapeDtypeStruct((m, n), out_dtype),
        scratch_shapes=[pltpu.VMEM((tm, tn), F32)] if nk > 1 else [],
        compiler_params=_params("parallel", "parallel", "arbitrary"),
    )(*([a, b] + ([resid] if has_res else [])))


def _rms_fwd(x, g, name, dep=None):
    t, d = x.shape
    tm = _tile(t, (256,))

    def body(x_ref, g_ref, a_ref, r_ref):
        xv = x_ref[...]
        r = lax.rsqrt(jnp.mean(xv * xv, axis=-1, keepdims=True) + EPS)
        a_ref[...] = (xv * r * g_ref[...]).astype(BF16)
        r_ref[...] = r

    row = pl.BlockSpec((tm, d), lambda i: (i, 0))
    return _pallas(
        body,
        2,
        dep,
        name=name,
        grid=(t // tm,),
        in_specs=[row, pl.BlockSpec((1, d), lambda i: (0, 0))],
        out_specs=[row, pl.BlockSpec((tm, 1), lambda i: (i, 0))],
        out_shape=[jax.ShapeDtypeStruct((t, d), BF16), jax.ShapeDtypeStruct((t, 1), F32)],
        compiler_params=_params("parallel"),
    )(x, g)


def _rms_bwd(da, x, r, g, resid, name, dep=None):
    t, d = x.shape
    tm = _tile(t, (256,))

    def body(da_ref, x_ref, r_ref, g_ref, res_ref, dx_ref, dg_ref):
        i = pl.program_id(0)
        rv = r_ref[...]
        n = x_ref[...] * rv
        dav = da_ref[...]
        dn = dav * g_ref[...]
        dx_ref[...] = rv * (dn - n * jnp.mean(dn * n, axis=-1, keepdims=True)) + res_ref[...]
        part = jnp.sum(dav * n, axis=0, keepdims=True)

        @pl.when(i == 0)
        def _():
            dg_ref[...] = part

        @pl.when(i > 0)
        def _():
            dg_ref[...] += part

    row = pl.BlockSpec((tm, d), lambda i: (i, 0))
    vec = pl.BlockSpec((1, d), lambda i: (0, 0))
    return _pallas(
        body,
        5,
        dep,
        name=name,
        grid=(t // tm,),
        in_specs=[row, row, pl.BlockSpec((tm, 1), lambda i: (i, 0)), vec, row],
        out_specs=[row, vec],
        out_shape=[jax.ShapeDtypeStruct((t, d), F32), jax.ShapeDtypeStruct((1, d), F32)],
        compiler_params=_params("arbitrary"),
    )(da, x, r, g, resid)


def _tri(n, upper):
    r = lax.broadcasted_iota(jnp.int32, (n, n), 0)
    c = lax.broadcasted_iota(jnp.int32, (n, n), 1)
    return jnp.where((c >= r) if upper else (c <= r), 1.0, 0.0).astype(F32)


def _hgrn_gates(q, fp, lbl):
    l0, l1 = lbl[0:1, :], lbl[1:2, :]
    mx = jnp.maximum(l0, l1)
    e0, e1 = jnp.exp(l0 - mx), jnp.exp(l1 - mx)
    lb = e0 / (e0 + e1)
    sig = _sigmoid(fp)
    f = lb + (1.0 - lb) * sig
    kk = (1.0 - lb) * _sigmoid(-fp)
    sq = _sigmoid(q)
    b = jnp.dot(_tri(CHUNK, False), jnp.log(f), precision=HIGHEST, preferred_element_type=F32)
    return lb, sig, f, kk, sq, q * sq, b


def _heads(x):
    return [x[:, j * HEAD_DIM : (j + 1) * HEAD_DIM] for j in range(x.shape[1] // HEAD_DIM)]


def _wide(parts):
    return jnp.concatenate(parts, axis=1)


def _intra_blocks(b):
    out = []
    for lo in range(0, CHUNK, SUB):
        hi = lo + SUB
        br = b[lo + SUB // 2 : lo + SUB // 2 + 1, :]
        row = lax.broadcasted_iota(jnp.int32, (SUB, hi), 0) + lo
        col = lax.broadcasted_iota(jnp.int32, (SUB, hi), 1)
        out.append((lo, hi, jnp.exp(b[lo:hi] - br), jnp.exp(br - b[:hi]), col <= row))
    return out


def _hgrn_fwd(proj, lb_logits, hg_norm):
    t = proj.shape[0]
    nc = t // CHUNK

    def body(q_ref, f_ref, i_ref, g_ref, lbl_ref, hgn_ref, y_ref, o_ref, st_ref, s_scr):
        c = pl.program_id(1)

        @pl.when(c == 0)
        def _():
            s_scr[...] = jnp.zeros_like(s_scr)

        hs = range(HG_BLOCK)
        sts = [s_scr[j] for j in hs]
        _, _, _, kk, _, qf, b = _hgrn_gates(q_ref[...], f_ref[...], lbl_ref[...])
        vb = _heads(i_ref[...].astype(BF16))
        bl = b[CHUNK - 1 : CHUNK, :]
        qe = _heads((qf * jnp.exp(b)).astype(BF16))
        kd = _heads((kk * jnp.exp(bl - b)).astype(BF16))
        decay = _heads(jnp.exp(bl))
        o = [_dot(qe[j], sts[j].astype(BF16), NT) for j in hs]
        parts = [[] for _ in hs]
        for lo, hi, ea, eb, mask in _intra_blocks(b):
            a, bk = _heads((qf[lo:hi] * ea).astype(BF16)), _heads((kk[:hi] * eb).astype(BF16))
            p = [jnp.where(mask, _dot(a[j], bk[j], NT), 0.0).astype(BF16) for j in hs]
            for j in hs:
                parts[j].append(_dot(p[j], vb[j][:hi]))
        o = [o[j] + jnp.concatenate(parts[j], axis=0) for j in hs]
        new = [sts[j] * decay[j] + _dot(vb[j], kd[j], TN) for j in hs]
        hgn = hgn_ref[...]
        on = [o[j] * lax.rsqrt(jnp.mean(o[j] * o[j], axis=-1, keepdims=True) + EPS) * hgn for j in hs]
        gg = g_ref[...]
        for j in hs:
            st_ref[j] = sts[j]
            s_scr[j] = new[j]
        o_ref[...] = _wide(o)
        y_ref[...] = (_wide(on) * (gg * _sigmoid(gg))).astype(BF16)

    wide = HG_BLOCK * HEAD_DIM
    groups = HG_HEADS // HG_BLOCK

    def col(k):
        return pl.BlockSpec((CHUNK, wide), lambda g, c: (c, k * groups + g))

    out = pl.BlockSpec((CHUNK, wide), lambda g, c: (c, g))
    return pl.pallas_call(
        body,
        name="hgrn_fwd",
        grid=(groups, nc),
        in_specs=[col(0), col(1), col(2), col(3), pl.BlockSpec((2, wide), lambda g, c: (0, g)), pl.BlockSpec((1, HEAD_DIM), lambda g, c: (0, 0))],
        out_specs=[out, out, pl.BlockSpec((HG_BLOCK, None, HEAD_DIM, HEAD_DIM), lambda g, c: (g, c, 0, 0))],
        out_shape=[
            jax.ShapeDtypeStruct((t, HG_WIDTH + ATT_WIDTH), BF16),
            jax.ShapeDtypeStruct((t, HG_WIDTH), F32),
            jax.ShapeDtypeStruct((HG_HEADS, nc, HEAD_DIM, HEAD_DIM), F32),
        ],
        scratch_shapes=[pltpu.VMEM((HG_BLOCK, HEAD_DIM, HEAD_DIM), F32)],
        compiler_params=_params("arbitrary", "arbitrary"),
    )(proj, proj, proj, proj, lb_logits, hg_norm)


def _hgrn_bwd(proj, lb_logits, hg_norm, o_hg, dycat, states, dep=None):
    t = proj.shape[0]
    nc = t // CHUNK

    def body(q_ref, f_ref, i_ref, g_ref, lbl_ref, hgn_ref, o_ref, dy_ref, st_ref, dp_ref, dlbl_ref, dhgn_ref, dst_scr, dlb_scr):
        h = pl.program_id(0)
        c = pl.program_id(1)

        @pl.when(c == 0)
        def _():
            dst_scr[...] = jnp.zeros_like(dst_scr)
            dlb_scr[...] = jnp.zeros_like(dlb_scr)

        @pl.when((c == 0) & (h == 0))
        def _():
            dhgn_ref[...] = jnp.zeros_like(dhgn_ref)

        hs = range(HG_BLOCK)
        hgn = _wide([hgn_ref[...]] * HG_BLOCK)
        q, fp, gg, vi = q_ref[...], f_ref[...], g_ref[...], i_ref[...]
        lb, sig, f, kk, sq, qf, b = _hgrn_gates(q, fp, lbl_ref[...])
        o, dy = o_ref[...], dy_ref[...]
        sg = _sigmoid(gg)
        n = _wide([oh * lax.rsqrt(jnp.mean(oh * oh, axis=-1, keepdims=True) + EPS) for oh in _heads(o)])
        don = dy * (gg * sg)
        dgg = dy * (n * hgn) * (sg * (1.0 + gg * (1.0 - sg)))
        d_hgn = sum(_heads(jnp.sum(don * n, axis=0, keepdims=True)))
        dn = don * hgn
        do = _wide(
            [
                lax.rsqrt(jnp.mean(oh * oh, axis=-1, keepdims=True) + EPS) * (dnh - nh * jnp.mean(dnh * nh, axis=-1, keepdims=True))
                for oh, dnh, nh in zip(_heads(o), _heads(dn), _heads(n))
            ]
        )
        sts = [st_ref[j] for j in hs]
        dstn = [dst_scr[j] for j in hs]
        bl = b[CHUNK - 1 : CHUNK, :]
        e_b, e_bl, e_l = jnp.exp(b), jnp.exp(bl - b), jnp.exp(bl)
        doh, vih = _heads(do), _heads(vi)
        dobh = _heads(do.astype(BF16))
        dq_acc = _wide([_dot3(doh[j], sts[j]) for j in hs]) * e_b
        dk_inter = _wide([_dot3(vih[j], dstn[j]) for j in hs]) * e_bl
        dk_acc = dk_inter
        kd = _heads((kk * e_bl).astype(BF16))
        dv_acc = _wide([_dot(kd[j], dstn[j].astype(BF16), NT) for j in hs])
        qe, decay = _heads((qf * e_b).astype(BF16)), _heads(e_l)
        dst_new = [dstn[j] * decay[j] + _dot(dobh[j], qe[j], TN) for j in hs]
        db_last = e_l * _wide([jnp.sum(sts[j] * dstn[j], axis=0, keepdims=True) for j in hs]) + jnp.sum(kk * dk_inter, axis=0, keepdims=True)
        dq_parts = []
        for lo, hi, ea, eb, mask in _intra_blocks(b):
            a, bk = qf[lo:hi] * ea, kk[:hi] * eb
            ah, bkh = _heads(a), _heads(bk)
            abh, bkbh = _heads(a.astype(BF16)), _heads(bk.astype(BF16))
            p = [jnp.where(mask, _dot(abh[j], bkbh[j], NT), 0.0).astype(BF16) for j in hs]
            dp = [jnp.where(mask, _dot3(doh[j][lo:hi], vih[j][:hi], NT), 0.0) for j in hs]
            dq_parts.append(_wide([_dot3(dp[j], bkh[j]) for j in hs]) * ea)
            dki = _wide([_dot3(dp[j], ah[j], TN) for j in hs]) * eb
            dvi = _wide([_dot(p[j], dobh[j][lo:hi], TN) for j in hs])
            if hi < CHUNK:
                zeros = jnp.zeros((CHUNK - hi, HG_BLOCK * HEAD_DIM), F32)
                dki = jnp.concatenate([dki, zeros], axis=0)
                dvi = jnp.concatenate([dvi, zeros], axis=0)
            dk_acc = dk_acc + dki
            dv_acc = dv_acc + dvi
        dq_acc = dq_acc + jnp.concatenate(dq_parts, axis=0)
        rows = lax.broadcasted_iota(jnp.int32, dq_acc.shape, 0)
        db = qf * dq_acc - kk * dk_acc + jnp.where(rows == CHUNK - 1, db_last, 0.0)
        dlf = jnp.dot(_tri(CHUNK, True), db, precision=HIGHEST, preferred_element_type=F32)
        dfk = dlf / f - dk_acc
        for k, part in enumerate((dq_acc * (sq * (1.0 + q * (1.0 - sq))), (1.0 - lb) * dfk * sig * (1.0 - sig), dv_acc, dgg)):
            dp_ref[:, k * HG_WIDTH : (k + 1) * HG_WIDTH] = part.astype(BF16)
        dlb_scr[...] += jnp.sum(dfk * (1.0 - sig), axis=0, keepdims=True)
        dhgn_ref[...] += d_hgn
        for j in hs:
            dst_scr[j] = dst_new[j]

        @pl.when(c == nc - 1)
        def _():
            dl0 = dlb_scr[...] * lb * (1.0 - lb)
            dlbl_ref[0:1, :] = dl0
            dlbl_ref[1:2, :] = -dl0

    wide = HG_BLOCK * HEAD_DIM
    groups = HG_HEADS // HG_BLOCK

    def col(k):
        return pl.BlockSpec((CHUNK, wide), lambda g, c: (nc - 1 - c, k * groups + g))

    blk = pl.BlockSpec((CHUNK, wide), lambda g, c: (nc - 1 - c, g))
    assert groups == 1, "d(q, f, i, g) are written as one contiguous column range of the in_proj gradient"
    return _pallas(
        body,
        9,
        dep,
        name="hgrn_bwd",
        grid=(groups, nc),
        in_specs=[
            col(0), col(1), col(2), col(3),
            pl.BlockSpec((2, wide), lambda g, c: (0, g)),
            pl.BlockSpec((1, HEAD_DIM), lambda g, c: (0, 0)),
            blk, blk,
            pl.BlockSpec((HG_BLOCK, None, HEAD_DIM, HEAD_DIM), lambda g, c: (g, nc - 1 - c, 0, 0)),
        ],
        out_specs=[
            pl.BlockSpec((CHUNK, 4 * HG_WIDTH), lambda g, c: (nc - 1 - c, 0)),
            pl.BlockSpec((2, wide), lambda g, c: (0, g)),
            pl.BlockSpec((1, HEAD_DIM), lambda g, c: (0, 0)),
        ],
        out_shape=[
            jax.ShapeDtypeStruct((t, 4 * HG_WIDTH + 3 * ATT_WIDTH), BF16),
            jax.ShapeDtypeStruct((2, HG_WIDTH), F32),
            jax.ShapeDtypeStruct((1, HEAD_DIM), F32),
        ],
        scratch_shapes=[pltpu.VMEM((HG_BLOCK, HEAD_DIM, HEAD_DIM), F32), pltpu.VMEM((1, wide), F32)],
        compiler_params=_params("arbitrary", "arbitrary"),
    )(proj, proj, proj, proj, lb_logits, hg_norm, o_hg, dycat, states)


def _diagonal_slots(shift):
    i = lax.broadcasted_iota(jnp.int32, (N_REL_PAD, DIAG), 0)
    u = lax.broadcasted_iota(jnp.int32, (N_REL_PAD, DIAG), 1)
    offset = u - shift if shift else jnp.where(u < K_BLOCK, u, u - DIAG)
    return jnp.where(jnp.clip(PAD - offset, -REL_CLIP, REL_CLIP) + REL_CLIP == i, 1.0, 0.0).astype(BF16)


def _split3(x):
    hi = x.astype(BF16)
    mid = (x - hi.astype(F32)).astype(BF16)
    return hi, mid, (x - hi.astype(F32) - mid.astype(F32)).astype(BF16)


def _bias_table(rel_bias, dep=None):
    def body(rb_ref, o_ref, diag):
        h = pl.program_id(0)

        @pl.when(h == 0)
        def _():
            hi, mid, lo = _split3(rb_ref[...])
            slots = _diagonal_slots(0)
            diag[...] = _dot(hi, slots) + (_dot(mid, slots) + _dot(lo, slots))

        rows = jnp.broadcast_to(diag[pl.ds(h, 1), :], (Q_BLOCK, DIAG))
        row = lax.broadcasted_iota(jnp.int32, (Q_BLOCK, K_BLOCK), 0)
        col = lax.broadcasted_iota(jnp.int32, (Q_BLOCK, K_BLOCK), 1)
        first = row - (row & (CHUNK - 1))
        seen = (col >= first) & (col < first + BAND)
        o_ref[...] = jnp.where(seen, pltpu.roll(rows, 0, 1, stride=1, stride_axis=0)[:, :K_BLOCK], MASKED)

    return _pallas(
        body,
        1,
        dep,
        name="bias_table",
        grid=(ATT_HEADS,),
        in_specs=[pl.BlockSpec((ATT_HEADS, N_REL_PAD), lambda h: (0, 0))],
        out_specs=pl.BlockSpec((None, Q_BLOCK, K_BLOCK), lambda h: (h, 0, 0)),
        out_shape=jax.ShapeDtypeStruct((ATT_HEADS, Q_BLOCK, K_BLOCK), F32),
        scratch_shapes=[pltpu.VMEM((ATT_HEADS, DIAG), F32)],
        compiler_params=_params("arbitrary"),
    )(rel_bias)


def _att_probs(q_ref, kpad, bias_ref, blk):
    qs = (q_ref[...] * ATT_SCALE).astype(BF16)
    start = pl.multiple_of(blk * Q_BLOCK, Q_BLOCK)
    kb = kpad[pl.ds(start, K_BLOCK), :]
    s = _dot(qs, kb, NT) + bias_ref[...]
    col = lax.broadcasted_iota(jnp.int32, (Q_BLOCK, K_BLOCK), 1)
    s = jnp.where(col >= PAD - blk * Q_BLOCK, s, MASKED)
    e = jnp.exp(s - jnp.max(s, axis=-1, keepdims=True))
    return qs, kb, start, e * (1.0 / jnp.sum(e, axis=-1, keepdims=True))


def _fill_padded(dst, src):
    dst[0:PAD, :] = jnp.zeros((PAD, HEAD_DIM), BF16)
    dst[PAD:, :] = src[...].astype(BF16)


def _att_fwd(proj, bias, dep=None):
    t = proj.shape[0]
    nb = t // Q_BLOCK

    def body(q_ref, k_ref, v_ref, bias_ref, y_ref, kpad, vpad):
        c = pl.program_id(1)

        @pl.when(c == 0)
        def _():
            _fill_padded(kpad, k_ref)
            _fill_padded(vpad, v_ref)

        _, _, start, p = _att_probs(q_ref, kpad, bias_ref, c)
        y_ref[...] = _dot(p.astype(BF16), vpad[pl.ds(start, K_BLOCK), :]).astype(BF16)

    base = 4 * HG_HEADS
    return _pallas(
        body,
        4,
        dep,
        name="att_fwd",
        grid=(ATT_HEADS, nb),
        in_specs=[
            pl.BlockSpec((Q_BLOCK, HEAD_DIM), lambda h, c: (c, base + h)),
            pl.BlockSpec((t, HEAD_DIM), lambda h, c: (0, base + ATT_HEADS + h)),
            pl.BlockSpec((t, HEAD_DIM), lambda h, c: (0, base + 2 * ATT_HEADS + h)),
            pl.BlockSpec((None, Q_BLOCK, K_BLOCK), lambda h, c: (h, 0, 0)),
        ],
        out_specs=pl.BlockSpec((Q_BLOCK, HEAD_DIM), lambda h, c: (c, h)),
        out_shape=jax.ShapeDtypeStruct((t, ATT_WIDTH), BF16),
        scratch_shapes=[pltpu.VMEM((t + PAD, HEAD_DIM), BF16), pltpu.VMEM((t + PAD, HEAD_DIM), BF16)],
        compiler_params=_params("arbitrary", "arbitrary"),
    )(proj, proj, proj, bias)


def _att_bwd(proj, bias, dycat, dep=None):
    t = proj.shape[0]
    nb = t // Q_BLOCK

    def body(q_ref, k_ref, v_ref, bias_ref, dy_ref, dq_ref, dk_ref, dv_ref, g_ref, kpad, vpad, dkacc, dvacc):
        c = pl.program_id(1)

        @pl.when(c == 0)
        def _():
            _fill_padded(kpad, k_ref)
            _fill_padded(vpad, v_ref)
            dkacc[...] = jnp.zeros_like(dkacc)
            dvacc[...] = jnp.zeros_like(dvacc)
            g_ref[...] = jnp.zeros_like(g_ref)

        qs, kb, start, p = _att_probs(q_ref, kpad, bias_ref, c)
        band = pl.ds(start, K_BLOCK)
        dyb = dy_ref[...].astype(BF16)
        dvacc[band, :] += _dot(p.astype(BF16), dyb, TN)
        dp = _dot(dyb, vpad[band, :], NT)
        ds = p * (dp - jnp.sum(dp * p, axis=-1, keepdims=True))
        g_ref[...] += ds
        dsb = ds.astype(BF16)
        dq_ref[...] = (_dot(dsb, kb) * ATT_SCALE).astype(BF16)
        dkacc[band, :] += _dot(dsb, qs, TN)

        @pl.when(c == nb - 1)
        def _():
            dk_ref[...] = dkacc[PAD:, :].astype(BF16)
            dv_ref[...] = dvacc[PAD:, :].astype(BF16)

    base = 4 * HG_HEADS
    whole = pl.BlockSpec((t, HEAD_DIM), lambda h, c: (0, h))
    return _pallas(
        body,
        5,
        dep,
        name="att_bwd",
        grid=(ATT_HEADS, nb),
        in_specs=[
            pl.BlockSpec((Q_BLOCK, HEAD_DIM), lambda h, c: (c, base + h)),
            pl.BlockSpec((t, HEAD_DIM), lambda h, c: (0, base + ATT_HEADS + h)),
            pl.BlockSpec((t, HEAD_DIM), lambda h, c: (0, base + 2 * ATT_HEADS + h)),
            pl.BlockSpec((None, Q_BLOCK, K_BLOCK), lambda h, c: (h, 0, 0)),
            pl.BlockSpec((Q_BLOCK, HEAD_DIM), lambda h, c: (c, HG_HEADS + h)),
        ],
        out_specs=[pl.BlockSpec((Q_BLOCK, HEAD_DIM), lambda h, c: (c, h)), whole, whole, pl.BlockSpec((None, Q_BLOCK, K_BLOCK), lambda h, c: (h, 0, 0))],
        out_shape=[
            jax.ShapeDtypeStruct((t, ATT_WIDTH), BF16),
            jax.ShapeDtypeStruct((t, ATT_WIDTH), BF16),
            jax.ShapeDtypeStruct((t, ATT_WIDTH), BF16),
            jax.ShapeDtypeStruct((ATT_HEADS, Q_BLOCK, K_BLOCK), F32),
        ],
        scratch_shapes=[
            pltpu.VMEM((t + PAD, HEAD_DIM), BF16),
            pltpu.VMEM((t + PAD, HEAD_DIM), BF16),
            pltpu.VMEM((t + PAD, HEAD_DIM), F32),
            pltpu.VMEM((t + PAD, HEAD_DIM), F32),
        ],
        compiler_params=_params("arbitrary", "arbitrary"),
    )(proj, proj, proj, bias, dycat)


def _rel_bias_grad(gsum):
    def body(g_ref, o_ref):
        r = lax.broadcasted_iota(jnp.int32, (Q_BLOCK, Q_BLOCK), 0)
        c = lax.broadcasted_iota(jnp.int32, (Q_BLOCK, Q_BLOCK), 1)
        flip = jnp.where(r + c == Q_BLOCK - 1, 1.0, 0.0).astype(BF16)
        sums = []
        for h in range(ATT_HEADS):
            hi, mid, lo = _split3(g_ref[h])
            rev = _dot(flip, hi) + (_dot(flip, mid) + _dot(flip, lo))
            wide = jnp.concatenate([rev, jnp.zeros((Q_BLOCK, DIAG - K_BLOCK), F32)], axis=1)
            sums.append(jnp.sum(pltpu.roll(wide, 0, 1, stride=1, stride_axis=0), axis=0, keepdims=True))
        hi, mid, lo = _split3(jnp.concatenate(sums, axis=0))
        slots = _diagonal_slots(Q_BLOCK - 1)
        o_ref[...] = _dot(hi, slots, NT) + (_dot(mid, slots, NT) + _dot(lo, slots, NT))

    return pl.pallas_call(
        body,
        name="rel_bias_grad",
        out_shape=jax.ShapeDtypeStruct((ATT_HEADS, N_REL_PAD), F32),
        compiler_params=_params(),
    )(gsum)


HALO = 16


FF_TILE = 1408
FF_TILES = D_FF // FF_TILE


def _interleave_cols(a):
    lead = a.shape[:-1]
    return jnp.swapaxes(a.reshape(*lead, 2, FF_TILES, FF_TILE), -3, -2).reshape(*lead, 2 * D_FF)


def _deinterleave_cols(a):
    lead = a.shape[:-1]
    return jnp.swapaxes(a.reshape(*lead, FF_TILES, 2, FF_TILE), -3, -2).reshape(*lead, 2 * D_FF)


def _ffn_specs(t, tm):
    wide = 2 * FF_TILE
    tile = pl.BlockSpec((tm, wide), lambda j, i: (i, j))
    before = pl.BlockSpec((HALO, wide), lambda j, i: (jnp.maximum(i * (tm // HALO) - 1, 0), j))
    after = pl.BlockSpec((HALO, wide), lambda j, i: (jnp.minimum((i + 1) * (tm // HALO), t // HALO - 1), j))
    vec = lambda rows: pl.BlockSpec((rows, wide), lambda j, i: (0, j))
    return tile, before, after, vec


def _shifted(x, rows, offsets):
    r = lax.broadcasted_iota(jnp.int32, (rows, x.shape[0]), 0)
    c = lax.broadcasted_iota(jnp.int32, (rows, x.shape[0]), 1)
    pick = jnp.concatenate([jnp.where(c == r + o, 1.0, 0.0).astype(BF16) for o in offsets], axis=0)
    out = _dot(pick, x)
    return [out[k * rows : (k + 1) * rows] for k in range(len(offsets))]


def _conv(x, w, b, rows):
    taps = _shifted(x, rows, [HALO - 2, HALO - 1]) + [x[HALO : HALO + rows].astype(F32)]
    return b + w[0:1] * taps[0] + w[1:2] * taps[1] + w[2:3] * taps[2], taps


def _ffn_act_fwd(u, conv_w, conv_b):
    t = u.shape[0]
    tm = _tile(t, (128,))
    tile, before, _, vec = _ffn_specs(t, tm)

    def body(u_ref, h_ref, w_ref, b_ref, z_ref):
        first = pl.program_id(1) == 0
        halo = h_ref[...]
        x = jnp.concatenate([jnp.where(first, jnp.zeros_like(halo), halo), u_ref[...]], axis=0)
        c, _ = _conv(x, w_ref[...], b_ref[...], tm)
        gate, val = c[:, :FF_TILE], c[:, FF_TILE:]
        z_ref[...] = (gate * _sigmoid(gate) * val).astype(BF16)

    return pl.pallas_call(
        body,
        name="ffn_act_fwd",
        grid=(FF_TILES, t // tm),
        in_specs=[tile, before, vec(3), vec(1)],
        out_specs=pl.BlockSpec((tm, FF_TILE), lambda j, i: (i, j)),
        out_shape=jax.ShapeDtypeStruct((t, D_FF), BF16),
        compiler_params=_params("parallel", "parallel"),
    )(u, u, conv_w, conv_b)


def _ffn_act_bwd(u, dz, conv_w, conv_b, dep=None):
    t = u.shape[0]
    tm = _tile(t, (128,))
    nt = t // tm
    ext = tm + HALO
    tile, before, after, vec = _ffn_specs(t, tm)

    def body(u_ref, ub_ref, ua_ref, w_ref, b_ref, dz_ref, dza_ref, du_ref, dw_ref, db_ref):
        i = pl.program_id(1)
        first, last = i == 0, i == nt - 1
        ub, ua = ub_ref[...], ua_ref[...]
        parts = [jnp.where(first, jnp.zeros_like(ub), ub), u_ref[...], jnp.where(last, jnp.zeros_like(ua), ua)]
        w = w_ref[...]
        c, taps = _conv(jnp.concatenate(parts, axis=0), w, b_ref[...], ext)
        gate, val = c[:, :FF_TILE], c[:, FF_TILE:]
        dz = jnp.concatenate([dz_ref[...].astype(F32), jnp.where(last, 0.0, dza_ref[...].astype(F32))], axis=0)
        sg = _sigmoid(gate)
        d = jnp.concatenate([dz * val * (sg * (1.0 + gate * (1.0 - sg))), dz * (gate * sg)], axis=1)
        d1, d2 = _shifted(d.astype(BF16), tm, [1, 2])
        du_ref[...] = (w[2:3] * d[:tm] + w[1:2] * d1 + w[0:1] * d2).astype(BF16)

        @pl.when(first)
        def _():
            dw_ref[...] = jnp.zeros_like(dw_ref)
            db_ref[...] = jnp.zeros_like(db_ref)

        for k, tap in enumerate(taps):
            dw_ref[k : k + 1, :] += jnp.sum(d[:tm] * tap[:tm], axis=0, keepdims=True)
        db_ref[...] += jnp.sum(d[:tm], axis=0, keepdims=True)

    narrow = lambda rows, index: pl.BlockSpec((rows, FF_TILE), index)
    return _pallas(
        body,
        7,
        dep,
        name="ffn_act_bwd",
        grid=(FF_TILES, nt),
        in_specs=[
            tile, before, after, vec(3), vec(1),
            narrow(tm, lambda j, i: (i, j)),
            narrow(HALO, lambda j, i: (jnp.minimum((i + 1) * (tm // HALO), t // HALO - 1), j)),
        ],
        out_specs=[tile, vec(3), vec(1)],
        out_shape=[
            jax.ShapeDtypeStruct((t, 2 * D_FF), BF16),
            jax.ShapeDtypeStruct((3, 2 * D_FF), F32),
            jax.ShapeDtypeStruct((1, 2 * D_FF), F32),
        ],
        compiler_params=_params("parallel", "arbitrary"),
    )(u, u, u, conv_w, conv_b, dz, dz)


def _ple_loss(gpre, pp, h2, final_norm, target):
    t, d = h2.shape
    tm = _tile(t, (256,))

    def body(gp_ref, pp_ref, h_ref, g_ref, tg_ref, dh_ref, dgp_ref, dpp_ref, dg_ref, loss_ref):
        i = pl.program_id(0)
        gate = _sigmoid(gp_ref[...])
        ppv = pp_ref[...]
        h3 = h_ref[...] + gate * ppv
        r = lax.rsqrt(jnp.mean(h3 * h3, axis=-1, keepdims=True) + EPS)
        n = h3 * r
        g = g_ref[...]
        err = n * g - tg_ref[...]
        loss = 0.5 * jnp.sum(jnp.mean(err * err, axis=-1, keepdims=True))
        dy = err * (1.0 / d)
        dn = dy * g
        dh = r * (dn - n * jnp.mean(dn * n, axis=-1, keepdims=True))
        dh_ref[...] = dh
        dgp_ref[...] = (dh * ppv * gate * (1.0 - gate)).astype(BF16)
        dpp_ref[...] = (dh * gate).astype(BF16)
        dg = jnp.sum(dy * n, axis=0, keepdims=True)

        @pl.when(i == 0)
        def _():
            dg_ref[...] = dg
            loss_ref[...] = jnp.full(loss_ref.shape, loss, F32)

        @pl.when(i > 0)
        def _():
            dg_ref[...] += dg
            loss_ref[...] += loss

    row = pl.BlockSpec((tm, d), lambda i: (i, 0))
    vec = pl.BlockSpec((1, d), lambda i: (0, 0))
    return pl.pallas_call(
        body,
        name="ple_loss",
        grid=(t // tm,),
        in_specs=[row, row, row, vec, row],
        out_specs=[row, row, row, vec, pl.BlockSpec((8, 128), lambda i: (0, 0))],
        out_shape=[
            jax.ShapeDtypeStruct((t, d), F32),
            jax.ShapeDtypeStruct((t, d), BF16),
            jax.ShapeDtypeStruct((t, d), BF16),
            jax.ShapeDtypeStruct((1, d), F32),
            jax.ShapeDtypeStruct((8, 128), F32),
        ],
        compiler_params=_params("arbitrary"),
    )(gpre, pp, h2, final_norm, target)


def _adamw(w, g, m, v):
    m = ADAM_B1 * m + (1.0 - ADAM_B1) * g
    v = ADAM_B2 * v + (1.0 - ADAM_B2) * (g * g)
    m_hat = m / (1.0 - ADAM_B1 ** ADAM_STEP)
    v_hat = v / (1.0 - ADAM_B2 ** ADAM_STEP)
    return -ADAM_LR * (m_hat / (jnp.sqrt(v_hat) + ADAM_EPS) + ADAM_WD * w), m, v


def _adam_big(w, m, v, own, recv, name, dep=None):
    r, c = w.shape
    tr = _tile(r, (256, 176))

    def body(w_ref, m_ref, v_ref, own_ref, recv_ref, g_ref, d_ref, nm_ref, nv_ref):
        g = own_ref[...]
        for k in range(3):
            g = g + recv_ref[k].astype(F32)
        g_ref[...] = g
        d_ref[...], nm_ref[...], nv_ref[...] = _adamw(w_ref[...], g, m_ref[...], v_ref[...])

    blk = pl.BlockSpec((tr, c), lambda i: (i, 0))
    return _pallas(
        body,
        5,
        dep,
        name=name,
        grid=(r // tr,),
        in_specs=[blk, blk, blk, blk, pl.BlockSpec((3, tr, c), lambda i: (0, i, 0))],
        out_specs=[blk] * 4,
        out_shape=[jax.ShapeDtypeStruct((r, c), F32)] * 4,
        compiler_params=_params("parallel"),
    )(w, m, v, own, recv)


def _adam_small(w, g, m, v):
    def body(w_ref, g_ref, m_ref, v_ref, d_ref, nm_ref, nv_ref):
        d_ref[...], nm_ref[...], nv_ref[...] = _adamw(w_ref[...], g_ref[...], m_ref[...], v_ref[...])

    return pl.pallas_call(body, name="adam_small", out_shape=[jax.ShapeDtypeStruct(w.shape, F32)] * 3, compiler_params=_params())(w, g, m, v)


def _position():
    return lax.axis_index("x"), lax.axis_index("y"), lax.axis_index("c")


def _other_chips(x, y):
    return [(1 - x, y), (x, 1 - y), (1 - x, 1 - y)]


def _block_index(dev, interleaved):
    x, y, c = dev
    return 4 * y + 2 * c + x if interleaved else 4 * x + 2 * y + c


def _shard_of(ref, axis, size, dev, interleaved=False):
    start = pl.multiple_of(_block_index(dev, interleaved) * size, 128 if axis == 1 else 16)
    return ref.at[:, pl.ds(start, size)] if axis == 1 else ref.at[pl.ds(start, size), :]


def _add_blocks(ids, grad, landed, axis, size, targets, out_dtype, name):
    rows = size if axis == 0 else grad.shape[0]
    cols = size if axis == 1 else grad.shape[1]
    tr = _tile(rows, (256, 176))
    nr = rows // tr
    nt = len(targets)

    def body(ids_ref, g_ref, l_ref, o_ref):
        o_ref[...] = (g_ref[...] + l_ref[...]).astype(out_dtype)

    if axis == 1:
        g_spec = pl.BlockSpec((tr, cols), lambda k, i, ids: (i, ids[targets[0] + k]))
    else:
        g_spec = pl.BlockSpec((tr, cols), lambda k, i, ids: (ids[targets[0] + k] * nr + i, 0))
    return pl.pallas_call(
        body,
        name=name,
        grid_spec=pltpu.PrefetchScalarGridSpec(
            num_scalar_prefetch=1,
            grid=(nt, nr),
            in_specs=[g_spec, pl.BlockSpec((None, tr, cols), lambda k, i, ids: (ids[4 + targets[0] + k], i, 0))],
            out_specs=pl.BlockSpec((None, tr, cols), lambda k, i, ids: (k, i, 0)),
        ),
        out_shape=jax.ShapeDtypeStruct((nt, rows, cols), out_dtype),
        compiler_params=_params("parallel", "parallel"),
    )(ids, grad, landed)


def _all_reduce_small(vec, name):
    rows = vec.shape[0]

    def body(v_ref, o_ref, land, send_sems, recv_sems):
        x, y, c = _position()
        mine = 4 * x + 2 * y + c
        copies = []
        for mask in range(1, N_DEV):
            peer = (1 - x if mask & 4 else x, 1 - y if mask & 2 else y, 1 - c if mask & 1 else c)
            copies.append(
                pltpu.make_async_remote_copy(
                    src_ref=v_ref, dst_ref=land.at[mine], send_sem=send_sems.at[mask - 1], recv_sem=recv_sems.at[mask - 1], device_id=peer, device_id_type=MESH
                )
            )
        for cp in copies:
            cp.start()
        land[mine] = v_ref[...]
        for cp in copies:
            cp.wait()
        acc = land[0]
        for k in range(1, N_DEV):
            acc = acc + land[k]
        o_ref[...] = acc

    return pl.pallas_call(
        body,
        name=name,
        out_shape=jax.ShapeDtypeStruct(vec.shape, F32),
        in_specs=[pl.BlockSpec(memory_space=pltpu.VMEM)],
        out_specs=pl.BlockSpec(memory_space=pltpu.VMEM),
        scratch_shapes=[pltpu.VMEM((N_DEV, rows, 128), F32), pltpu.SemaphoreType.DMA((N_DEV - 1,)), pltpu.SemaphoreType.DMA((N_DEV - 1,))],
    )(vec)


def _rows128(a, rows):
    flat = a.reshape(-1)
    return jnp.pad(flat, (0, rows * 128 - flat.shape[0])).reshape(rows, 128)


def _pad_rel(a):
    return jnp.pad(a.reshape(ATT_HEADS, -1)[:, :N_REL], ((0, 0), (0, N_REL_PAD - N_REL)))


SMALL = [("norm_mix", 16), ("lb_logits", 16), ("hg_norm", 8), ("rel_bias", 24), ("norm_ffn", 16), ("conv_b", 88), ("norm_ple", 16), ("final_norm", 16)]
CONV_W_FULL_ROWS = 3 * 2 * D_FF // 128
CONV_W_SHARD_ROWS = 40


def _pack_small(parts):
    return jnp.concatenate([_rows128(_pad_rel(parts[k]) if k == "rel_bias" else parts[k], rows) for k, rows in SMALL], axis=0)


def _unpack_small(packed, shapes):
    out, at = {}, 0
    for k, rows in SMALL:
        blk = packed[at : at + rows]
        at += rows
        if k == "rel_bias":
            out[k] = blk.reshape(ATT_HEADS, N_REL_PAD)[:, :N_REL].reshape(shapes[k])
        else:
            n = 1
            for s in shapes[k]:
                n *= s
            out[k] = blk.reshape(-1)[:n].reshape(shapes[k])
    return out, at


BIG = [("w_in", 1), ("w_out", 0), ("w_up", 1), ("w_down", 0), ("w_ple_gate", 0), ("w_ple_proj", 1)]


HBM = pl.BlockSpec(memory_space=pltpu.HBM)
SEM = pl.BlockSpec(memory_space=pltpu.SEMAPHORE)
EFFECT = pltpu.SideEffectType.DATAFLOW_SIDE_EFFECTING


def _copies(plan, refs, send_sems, recv_sems):
    return [
        pltpu.make_async_remote_copy(src_ref=src, dst_ref=dst, send_sem=send_sems.at[i], recv_sem=recv_sems.at[i], device_id=dev, device_id_type=MESH)
        for i, (src, dst, dev) in enumerate(plan(refs))
    ]


def _split_start(name, arrays, plan, n):
    k = len(arrays)

    def body(*refs):
        for cp in _copies(plan, refs[:k], refs[k], refs[k + 1]):
            cp.start()
        refs[-1][...] = jnp.zeros_like(refs[-1])

    out = pl.pallas_call(
        body,
        name=name,
        out_shape=(pltpu.SemaphoreType.DMA((n,)), pltpu.SemaphoreType.DMA((n,)), *[pltpu.HBM(a.shape, a.dtype) for a in arrays], jax.ShapeDtypeStruct((8, 128), F32)),
        in_specs=[HBM] * k,
        out_specs=(SEM, SEM, *[HBM] * k, pl.BlockSpec(memory_space=pltpu.VMEM)),
        input_output_aliases={i: 2 + i for i in range(k)},
        compiler_params=pltpu.CompilerParams(has_side_effects=EFFECT),
    )(*[pltpu.with_memory_space_constraint(a, pltpu.HBM) for a in arrays])
    return out[0], out[1], list(out[2 : 2 + k]), out[-1]


def _split_wait(name, send, recv, arrays, plan, after):
    k = len(arrays)

    def body(*refs):
        for cp in _copies(plan, refs[:k], refs[k], refs[k + 1]):
            cp.wait_send()
            cp.wait_recv()

    out = pl.pallas_call(
        body,
        name=name,
        out_shape=tuple(pltpu.HBM(a.shape, a.dtype) for a in arrays),
        in_specs=[HBM] * k + [SEM, SEM, ANY],
        out_specs=tuple([HBM] * k),
        input_output_aliases={i: i for i in range(k)},
        compiler_params=pltpu.CompilerParams(has_side_effects=EFFECT),
    )(*arrays, send, recv, after)
    return list(out)


def _cast_into(w, me, axis, name, dep, dtype):
    r, c = w.shape
    tr = _tile(r, (256, 176))
    nr = r // tr
    deps = [] if dep is None else [dep]

    def body(me_ref, w_ref, *rest):
        rest[-1][...] = w_ref[...].astype(dtype)

    if axis == 1:
        shape, o_spec = (r, N_DEV * c), pl.BlockSpec((tr, c), lambda i, me: (i, me[0]))
    else:
        shape, o_spec = (N_DEV * r, c), pl.BlockSpec((tr, c), lambda i, me: (me[0] * nr + i, 0))
    return pl.pallas_call(
        body,
        name=name,
        grid_spec=pltpu.PrefetchScalarGridSpec(
            num_scalar_prefetch=1, grid=(nr,), in_specs=[pl.BlockSpec((tr, c), lambda i, me: (i, 0))] + [ANY] * len(deps), out_specs=o_spec
        ),
        out_shape=jax.ShapeDtypeStruct(shape, dtype),
        compiler_params=_params("parallel"),
    )(me, w, *deps)


GATHER = [
    (["w_in"], None, "norm_mix_fwd", "bias_table"),
    (["w_out"], "norm_mix_fwd", "att_fwd", None),
    (["w_up", "conv_w"], "norm_mix_fwd", "att_fwd", "norm_ffn_fwd"),
    (["w_down", "w_ple_gate", "w_ple_proj"], "att_fwd", "up_proj", "ffn_act_fwd"),
]
GROUPS = [["w_ple_proj", "w_ple_gate", "w_down"], ["w_up"], ["w_out"], ["w_in"]]
STAGES = ["ffn_act_bwd", "d_mix_out", "hgrn_bwd", "d_norm_mix_out"]
INTERLEAVED = {"w_up", "conv_w"}


class _Exchange:
    def __init__(self, big, conv_w, position):
        self.big, self.axis = big, dict(BIG, conv_w=1)
        self.shards = dict(big, conv_w=conv_w)
        self.size = {k: w.shape[self.axis[k]] for k, w in self.shards.items()}
        self.x, self.y, self.c = position
        chips = [(self.x, self.y)] + _other_chips(self.x, self.y)
        landed = [2 * cx + cy for cx, cy in chips]
        self.ids = {
            flag: jnp.stack([_block_index((cx, cy, self.c), flag) for cx, cy in chips] + landed).astype(jnp.int32) for flag in (False, True)
        }
        self.tokens, self.grads, self.state, self.wfull = [], {}, {}, {}


    def _slot(self, ref, k, dev):
        return _shard_of(ref, self.axis[k], self.size[k], dev, interleaved=k in INTERLEAVED)

    def _plan_gather(self, names, direct, refs):
        x, y, c = _position()
        me, out = (x, y, c), []
        for k, ref in zip(names, refs):
            mine = self._slot(ref, k, me)
            out.append((mine, mine, (x, y, 1 - c)))
            out += [(mine, mine, (*chip, c)) for chip in _other_chips(x, y)]
            if direct:
                out += [(mine, mine, (*chip, 1 - c)) for chip in _other_chips(x, y)]
        return out

    def _plan_forward(self, names, refs):
        x, y, c = _position()
        out = []
        for k, ref in zip(names, refs):
            for chip in _other_chips(x, y):
                block = self._slot(ref, k, (*chip, c))
                out.append((block, block, (x, y, 1 - c)))
        return out

    def _plan_sibling(self, names, refs):
        x, y, c = _position()
        n = len(names)
        return [(self._slot(refs[i], k, (p // 2, p % 2, 1 - c)), refs[n + i].at[p], (x, y, 1 - c)) for i, k in enumerate(names) for p in range(4)]

    def _plan_chips(self, names, refs):
        x, y, c = _position()
        n = len(names)
        return [(refs[i].at[j], refs[n + i].at[j], (*chip, c)) for i in range(n) for j, chip in enumerate(_other_chips(x, y))]


    def gather(self):
        me = {flag: _block_index((self.x, self.y, self.c), flag).astype(jnp.int32).reshape(1) for flag in (False, True)}
        self.late, self.unsent = {}, {}
        after = None
        for gi, (names, issued, *_) in enumerate(GATHER):
            self.unsent[gi] = [
                _cast_into(self.shards[k], me[k in INTERLEAVED], self.axis[k], "cast_" + k, after, F32 if k == "conv_w" else BF16) for k in names
            ]
            if issued is None:
                self._issue(None)
                after = self.tokens[-1]
        self.tokens += [a for arrays in self.unsent.values() for a in arrays]

    def _issue(self, stage):
        for gi, (names, issued, _, forwarded) in enumerate(GATHER):
            if issued == stage and gi in self.unsent:
                plan = functools.partial(self._plan_gather, names, forwarded is None)
                copies = (7 if forwarded is None else 4) * len(names)
                send, recv, fulls, token = _split_start(f"gather_start_{gi}", self.unsent.pop(gi), plan, copies)
                self.tokens.append(token)
                self.late[gi] = (send, recv, fulls, plan)

    def weight(self, k):
        return self.wfull[k]

    def dep(self):
        tokens, self.tokens = self.tokens, []
        return tokens

    def reduce(self, vec, name):
        return _all_reduce_small(vec, name)

    def grad(self, k, g):
        self.grads[k] = g
        for gi, names in enumerate(GROUPS):
            if k == names[-1]:
                plan = functools.partial(self._plan_sibling, names)
                lands = [lax.empty((4, *self._shard_shape(n)), F32) for n in names]
                send, recv, arrays, token = _split_start(f"sibling_start_{gi}", [self.grads[n] for n in names] + lands, plan, 4 * len(names))
                self.tokens.append(token)
                self.state[gi] = (send, recv, arrays, plan)

    def done(self, stage, after):
        for gi, (names, _, _, forwarded) in enumerate(GATHER):
            if forwarded == stage:
                send, recv, fulls, plan = self.late[gi]
                self.wfull.update(zip(names, _split_wait(f"forward_wait_{gi}", send, recv, fulls, plan, after)))
        for gi, (names, _, arrived, forwarded) in enumerate(GATHER):
            if arrived == stage:
                send, recv, fulls, plan = self.late[gi]
                fulls = _split_wait(f"gather_wait_{gi}", send, recv, fulls, plan, after)
                if forwarded is None:
                    self.wfull.update(zip(names, fulls))
                else:
                    plan = functools.partial(self._plan_forward, names)
                    send, recv, fulls, token = _split_start(f"forward_start_{gi}", fulls, plan, 3 * len(names))
                    self.tokens.append(token)
                    self.late[gi] = (send, recv, fulls, plan)
        self._issue(stage)
        if stage in STAGES:
            self._to_chips(STAGES.index(stage), after)

    def _shard_shape(self, k):
        shape = list(self.grads[k].shape)
        shape[self.axis[k]] = self.size[k]
        return tuple(shape)

    def _to_chips(self, gi, after):
        names = GROUPS[gi]
        n = len(names)
        send, recv, arrays, plan = self.state[gi]
        arrays = _split_wait(f"sibling_wait_{gi}", send, recv, arrays, plan, after)
        own, parts = [], []
        for k, g, land in zip(names, arrays[:n], arrays[n:]):
            ids = self.ids[k in INTERLEAVED]
            own.append(_add_blocks(ids, g, land, self.axis[k], self.size[k], [0], F32, "add_own_" + k)[0])
            parts.append(_add_blocks(ids, g, land, self.axis[k], self.size[k], [1, 2, 3], BF16, "add_send_" + k))
        plan = functools.partial(self._plan_chips, names)
        lands = [lax.empty(part.shape, BF16) for part in parts]
        send, recv, arrays, token = _split_start(f"chips_start_{gi}", parts + lands, plan, 3 * n)
        self.tokens.append(token)
        self.state[gi] = (send, recv, arrays, plan, own)

    def finish(self, gi, after):
        names = GROUPS[gi]
        send, recv, arrays, plan, own = self.state[gi]
        arrays = _split_wait(f"chips_wait_{gi}", send, recv, arrays, plan, after)
        return {k: (o, r) for k, o, r in zip(names, own, arrays[len(names) :])}


class _Resident:
    def __init__(self, wfull):
        self.wfull, self.grads = wfull, {}

    def weight(self, k):
        return self.wfull[k]

    def grad(self, k, g):
        self.grads[k] = g

    def dep(self):
        return None

    def reduce(self, vec, name):
        return vec

    def done(self, stage, after):
        pass


def _local_step(x, p, target, small, ex):
    bias = _bias_table(jnp.pad(small["rel_bias"], ((0, 0), (0, N_REL_PAD - N_REL))), dep=ex.dep())
    a1, r1 = _rms_fwd(x, small["norm_mix"], "norm_mix_fwd", dep=[bias])
    ex.done("norm_mix_fwd", a1)
    ex.done("bias_table", a1)
    proj = _matmul(a1, ex.weight("w_in"), "nn", F32, "in_proj", dep=ex.dep())
    y_hg, o_hg, states = _hgrn_fwd(proj, small["lb_logits"], small["hg_norm"])
    y_att = _att_fwd(proj, bias, dep=ex.dep())
    ycat = lax.dynamic_update_slice(y_hg, y_att, (0, HG_WIDTH))
    ex.done("att_fwd", ycat)
    h1 = _matmul(ycat, ex.weight("w_out"), "nn", F32, "out_proj", resid=x, dep=ex.dep())
    a2, r2 = _rms_fwd(h1, small["norm_ffn"], "norm_ffn_fwd")
    ex.done("norm_ffn_fwd", a2)
    conv_w = ex.weight("conv_w")
    u = _matmul(a2, ex.weight("w_up"), "nn", BF16, "up_proj")
    conv_b = _interleave_cols(small["conv_b"])
    ex.done("up_proj", u)
    z = _ffn_act_fwd(u, conv_w, conv_b)
    ex.done("ffn_act_fwd", z)
    h2 = _matmul(z, ex.weight("w_down"), "nn", F32, "down_proj", tm=512, resid=h1)
    a3, r3 = _rms_fwd(h2, small["norm_ple"], "norm_ple_fwd")
    gpre = _matmul(a3, ex.weight("w_ple_gate"), "nn", F32, "ple_gate")
    pp = _matmul(p, ex.weight("w_ple_proj"), "nn", F32, "ple_proj")
    dh3, dgpre, dpp, d_final, loss = _ple_loss(gpre, pp, h2, small["final_norm"], target)

    ex.grad("w_ple_proj", _matmul(p, dpp, "tn", F32, "d_w_ple_proj", tm=512))
    ex.grad("w_ple_gate", _matmul(a3, dgpre, "tn", F32, "d_w_ple_gate", tm=512))
    da3 = _matmul(dgpre, ex.weight("w_ple_gate"), "nt", F32, "d_norm_ple_out")
    dh2, d_ple = _rms_bwd(da3, h2, r3, small["norm_ple"], dh3, "norm_ple_bwd")
    dz = _matmul(dh2, ex.weight("w_down"), "nt", BF16, "d_ffn_act")
    ex.grad("w_down", _matmul(z, dh2, "tn", F32, "d_w_down", tm=512))
    du, dcw, dcb = _ffn_act_bwd(u, dz, conv_w, conv_b, dep=ex.dep())
    ex.done("ffn_act_bwd", du)
    d_conv_w, d_conv_b = _deinterleave_cols(dcw), _deinterleave_cols(dcb)
    ex.grad("w_up", _matmul(a2, du, "tn", F32, "d_w_up", tm=512, dep=ex.dep()))
    da2 = _matmul(du, ex.weight("w_up"), "nt", F32, "d_norm_ffn_out", tm=512, tk=5632, dep=ex.dep())
    dh1, d_ffn = _rms_bwd(da2, h1, r2, small["norm_ffn"], dh2, "norm_ffn_bwd")
    dycat = _matmul(dh1, ex.weight("w_out"), "nt", F32, "d_mix_out")
    ex.done("d_mix_out", dycat)
    ex.grad("w_out", _matmul(ycat, dh1, "tn", F32, "d_w_out", tk=2048, dep=ex.dep()))
    dp_hg, d_lb, d_hgn = _hgrn_bwd(proj, small["lb_logits"], small["hg_norm"], o_hg, dycat, states, dep=ex.dep())
    ex.done("hgrn_bwd", d_lb)
    dq_att, dk_att, dv_att, gsum = _att_bwd(proj, bias, dycat, dep=ex.dep())
    d_rel = _rel_bias_grad(gsum)
    d_small = {
        "norm_mix": jnp.zeros_like(small["norm_mix"]), "lb_logits": d_lb, "hg_norm": d_hgn, "rel_bias": d_rel, "norm_ffn": d_ffn,
        "conv_b": d_conv_b, "norm_ple": d_ple, "final_norm": d_final,
    }
    packed = jnp.concatenate([_pack_small(d_small), _rows128(d_conv_w, CONV_W_FULL_ROWS), _rows128(loss[0:1, 0:1], 8)], axis=0)
    early = ex.reduce(packed, "all_reduce_small")
    dproj = dp_hg
    for k, part in enumerate((dq_att, dk_att, dv_att)):
        dproj = lax.dynamic_update_slice(dproj, part, (0, 4 * HG_WIDTH + k * ATT_WIDTH))
    ex.grad("w_in", _matmul(a1, dproj, "tn", F32, "d_w_in", tm=512, dep=[early]))
    da1 = _matmul(dproj, ex.weight("w_in"), "nt", F32, "d_norm_mix_out", tm=512, tn=512, dep=ex.dep())
    dx, d_mix = _rms_bwd(da1, x, r1, small["norm_mix"], dh1, "norm_mix_bwd")
    rows = dict(SMALL)["norm_mix"]
    late = ex.reduce(_rows128(d_mix, rows), "all_reduce_norm_mix")
    ex.done("d_norm_mix_out", late)
    return dx, jnp.concatenate([late, early[rows:]], axis=0)


def kernel(x, p, norm_mix, w_in, lb_logits, hg_norm, rel_bias, w_out, norm_ffn, w_up, conv_w, conv_b, w_down, norm_ple, w_ple_gate, w_ple_proj, final_norm, loss_target, m_norm_mix, m_w_in, m_lb_logits, m_hg_norm, m_rel_bias, m_w_out, m_norm_ffn, m_w_up, m_conv_w, m_conv_b, m_w_down, m_norm_ple, m_w_ple_gate, m_w_ple_proj, m_final_norm, v_norm_mix, v_w_in, v_lb_logits, v_hg_norm, v_rel_bias, v_w_out, v_norm_ffn, v_w_up, v_conv_w, v_conv_b, v_w_down, v_norm_ple, v_w_ple_gate, v_w_ple_proj, v_final_norm):
    given = dict(locals())
    mx, my, mc = _position()
    me = 4 * mx + 2 * my + mc
    big = {k: given[k][0] for k, _ in BIG}
    ex = _Exchange(big, conv_w[0], (mx, my, mc))
    ex.gather()

    small = {
        "norm_mix": norm_mix, "lb_logits": lb_logits, "hg_norm": hg_norm, "rel_bias": rel_bias[0], "norm_ffn": norm_ffn,
        "conv_b": conv_b, "norm_ple": norm_ple, "final_norm": final_norm.reshape(1, -1),
    }
    dx, reduced = _local_step(x[0], p[0, 0], loss_target[0], small, ex)

    out = {}
    shapes = {k: given[k].shape for k, _ in SMALL}
    g_small, at = _unpack_small(reduced, shapes)
    g_conv_full = reduced[at : at + CONV_W_FULL_ROWS].reshape(3, 2 * D_FF)
    total_loss = reduced[at + CONV_W_FULL_ROWS, 0]
    cw = conv_w.shape[2]
    g_conv = lax.dynamic_slice_in_dim(g_conv_full, me * cw, cw, axis=1)

    def pack_with_conv(parts, conv_part):
        return jnp.concatenate([_pack_small(parts), _rows128(conv_part, CONV_W_SHARD_ROWS)], axis=0)

    d_pk, m_pk, v_pk = _adam_small(
        pack_with_conv({k: given[k] for k, _ in SMALL}, conv_w),
        pack_with_conv(g_small, g_conv),
        pack_with_conv({k: given["m_" + k] for k, _ in SMALL}, m_conv_w),
        pack_with_conv({k: given["v_" + k] for k, _ in SMALL}, v_conv_w),
    )
    for name, pk in (("d", d_pk), ("m", m_pk), ("v", v_pk)):
        parts, at = _unpack_small(pk, shapes)
        parts["conv_w"] = pk[at : at + CONV_W_SHARD_ROWS].reshape(-1)[: 3 * cw].reshape(conv_w.shape)
        for k, a in parts.items():
            out.setdefault(k, {})
            out[k][name] = a
    for k, _ in SMALL:
        out[k]["g"] = g_small[k]
    out["conv_w"]["g"] = g_conv.reshape(conv_w.shape)

    after, started = v_pk, ex.dep()
    for gi in range(len(GROUPS)):
        for k, (o, r) in ex.finish(gi, after).items():
            g, d, nm, nv = _adam_big(big[k], given["m_" + k][0], given["v_" + k][0], o, r, "adam_" + k, dep=started)
            out[k] = tuple(a[None] for a in (g, d, nm, nv))
            after = nv

    order = ["norm_mix", "w_in", "lb_logits", "hg_norm", "rel_bias", "w_out", "norm_ffn", "w_up", "conv_w", "conv_b", "w_down", "norm_ple", "w_ple_gate", "w_ple_proj", "final_norm"]

    def pick(k, what):
        return out[k][what] if isinstance(out[k], dict) else out[k][{"g": 0, "d": 1, "m": 2, "v": 3}[what]]

    return (total_loss, dx[None], *[pick(k, "g") for k in order], *[pick(k, "d") for k in order], *[pick(k, "m") for k in order], *[pick(k, "v") for k in order])
```

```python
import functools

import jax
import jax.numpy as jnp
from jax import lax
from jax.experimental import pallas as pl
from jax.experimental.pallas import tpu as pltpu

F32 = jnp.float32
BF16 = jnp.bfloat16

D_MODEL = 2048
CHUNK = 64
HG_HEADS = 8
HEAD_DIM = 128
HG_WIDTH = HG_HEADS * HEAD_DIM
ATT_HEADS = 8
ATT_WIDTH = ATT_HEADS * HEAD_DIM
LEFT_CHUNKS = 8
PAD = LEFT_CHUNKS * CHUNK
BAND = PAD + CHUNK
REL_CLIP = 128
N_REL = 2 * REL_CLIP + 1
N_REL_PAD = 384
D_FF = 5632
EPS = 1e-6
ATT_SCALE = HEAD_DIM ** -0.5
SUB = 32
HG_BLOCK = 8
Q_BLOCK = 4 * CHUNK
K_BLOCK = Q_BLOCK + PAD
DIAG = 1024
MASKED = -1e30

ADAM_LR = 0.001
ADAM_B1 = 0.9
ADAM_B2 = 0.999
ADAM_EPS = 1e-08
ADAM_WD = 0.01
ADAM_STEP = 10

N_DEV = 8
VMEM_LIMIT = 48 * 1024 * 1024
MESH = pl.DeviceIdType.MESH
ANY = pl.BlockSpec(memory_space=pl.ANY)
HIGHEST = lax.Precision.HIGHEST

NN = (((1,), (0,)), ((), ()))
NT = (((1,), (1,)), ((), ()))
TN = (((0,), (0,)), ((), ()))


def _params(*sem):
    return pltpu.CompilerParams(dimension_semantics=sem if sem else None, vmem_limit_bytes=VMEM_LIMIT)


def _pallas(body, n_in, dep, **kw):
    deps = [] if dep is None else list(dep)
    if not deps:
        return pl.pallas_call(body, **kw)

    def body_after(*refs):
        body(*refs[:n_in], *refs[n_in + len(deps) :])

    call = pl.pallas_call(body_after, **dict(kw, in_specs=list(kw["in_specs"]) + [ANY] * len(deps)))
    return lambda *ops: call(*ops, *deps)


def _dot(a, b, dims=NN):
    return lax.dot_general(a, b, dims, preferred_element_type=F32)


def _dot3(a, b, dims=NN):
    a_hi, b_hi = a.astype(BF16), b.astype(BF16)
    a_lo, b_lo = (a - a_hi.astype(F32)).astype(BF16), (b - b_hi.astype(F32)).astype(BF16)
    return _dot(a_hi, b_hi, dims) + (_dot(a_hi, b_lo, dims) + _dot(a_lo, b_hi, dims))


def _sigmoid(x):
    return 1.0 / (1.0 + jnp.exp(-x))


def _tile(n, prefs):
    for t in prefs:
        if n % t == 0:
            return t
    return n


def _matmul(a, b, mode, out_dtype, name, tm=1024, tn=1024, tk=None, resid=None, dep=None):
    if mode == "nn":
        (m, k), n = a.shape, b.shape[1]
    elif mode == "nt":
        (m, k), n = a.shape, b.shape[0]
    else:
        (k, m), n = a.shape, b.shape[1]
    tm = _tile(m, (tm, 512, 256, 128))
    tn = _tile(n, (tn, 1408, 512, 256, 128))
    tk = k if tk is None else _tile(k, (tk,))
    nk = k // tk
    dims = {"nn": NN, "nt": NT, "tn": TN}[mode]
    a_spec = pl.BlockSpec((tk, tm), lambda i, j, s: (s, i)) if mode == "tn" else pl.BlockSpec((tm, tk), lambda i, j, s: (i, s))
    b_spec = pl.BlockSpec((tn, tk), lambda i, j, s: (j, s)) if mode == "nt" else pl.BlockSpec((tk, tn), lambda i, j, s: (s, j))
    o_spec = pl.BlockSpec((tm, tn), lambda i, j, s: (i, j))
    has_res = resid is not None

    def body(*refs):
        a_ref, b_ref = refs[0], refs[1]
        o_ref = refs[2 + has_res]
        part = _dot(a_ref[...].astype(BF16), b_ref[...].astype(BF16), dims)

        def finish(acc):
            if has_res:
                acc = acc + refs[2][...]
            o_ref[...] = acc.astype(out_dtype)

        if nk == 1:
            finish(part)
        else:
            acc_ref = refs[-1]
            s = pl.program_id(2)

            @pl.when(s == 0)
            def _():
                acc_ref[...] = part

            @pl.when(s > 0)
            def _():
                acc_ref[...] += part

            @pl.when(s == nk - 1)
            def _():
                finish(acc_ref[...])

    return _pallas(
        body,
        2 + has_res,
        dep,
        name=name,
        grid=(m // tm, n // tn, nk),
        in_specs=[a_spec, b_spec] + ([o_spec] if has_res else []),
        out_specs=o_spec,
        out_shape=jax.ShapeDtypeStruct((m, n), out_dtype),
        scratch_shapes=[pltpu.VMEM((tm, tn), F32)] if nk > 1 else [],
        compiler_params=_params("parallel", "parallel", "arbitrary"),
    )(*([a, b] + ([resid] if has_res else [])))


def _rms_fwd(x, g, name, dep=None):
    t, d = x.shape
    tm = _tile(t, (256,))

    def body(x_ref, g_ref, a_ref, r_ref):
        xv = x_ref[...]
        r = lax.rsqrt(jnp.mean(xv * xv, axis=-1, keepdims=True) + EPS)
        a_ref[...] = (xv * r * g_ref[...]).astype(BF16)
        r_ref[...] = r

    row = pl.BlockSpec((tm, d), lambda i: (i, 0))
    return _pallas(
        body,
        2,
        dep,
        name=name,
        grid=(t // tm,),
        in_specs=[row, pl.BlockSpec((1, d), lambda i: (0, 0))],
        out_specs=[row, pl.BlockSpec((tm, 1), lambda i: (i, 0))],
        out_shape=[jax.ShapeDtypeStruct((t, d), BF16), jax.ShapeDtypeStruct((t, 1), F32)],
        compiler_params=_params("parallel"),
    )(x, g)


def _rms_bwd(da, x, r, g, resid, name, dep=None):
    t, d = x.shape
    tm = _tile(t, (256,))

    def body(da_ref, x_ref, r_ref, g_ref, res_ref, dx_ref, dg_ref):
        i = pl.program_id(0)
        rv = r_ref[...]
        n = x_ref[...] * rv
        dav = da_ref[...]
        dn = dav * g_ref[...]
        dx_ref[...] = rv * (dn - n * jnp.mean(dn * n, axis=-1, keepdims=True)) + res_ref[...]
        part = jnp.sum(dav * n, axis=0, keepdims=True)

        @pl.when(i == 0)
        def _():
            dg_ref[...] = part

        @pl.when(i > 0)
        def _():
            dg_ref[...] += part

    row = pl.BlockSpec((tm, d), lambda i: (i, 0))
    vec = pl.BlockSpec((1, d), lambda i: (0, 0))
    return _pallas(
        body,
        5,
        dep,
        name=name,
        grid=(t // tm,),
        in_specs=[row, row, pl.BlockSpec((tm, 1), lambda i: (i, 0)), vec, row],
        out_specs=[row, vec],
        out_shape=[jax.ShapeDtypeStruct((t, d), F32), jax.ShapeDtypeStruct((1, d), F32)],
        compiler_params=_params("arbitrary"),
    )(da, x, r, g, resid)


def _tri(n, upper):
    r = lax.broadcasted_iota(jnp.int32, (n, n), 0)
    c = lax.broadcasted_iota(jnp.int32, (n, n), 1)
    return jnp.where((c >= r) if upper else (c <= r), 1.0, 0.0).astype(F32)


def _hgrn_gates(q, fp, lbl):
    l0, l1 = lbl[0:1, :], lbl[1:2, :]
    mx = jnp.maximum(l0, l1)
    e0, e1 = jnp.exp(l0 - mx), jnp.exp(l1 - mx)
    lb = e0 / (e0 + e1)
    sig = _sigmoid(fp)
    f = lb + (1.0 - lb) * sig
    kk = (1.0 - lb) * _sigmoid(-fp)
    sq = _sigmoid(q)
    b = jnp.dot(_tri(CHUNK, False), jnp.log(f), precision=HIGHEST, preferred_element_type=F32)
    return lb, sig, f, kk, sq, q * sq, b


def _heads(x):
    return [x[:, j * HEAD_DIM : (j + 1) * HEAD_DIM] for j in range(x.shape[1] // HEAD_DIM)]


def _wide(parts):
    return jnp.concatenate(parts, axis=1)


def _intra_blocks(b):
    out = []
    for lo in range(0, CHUNK, SUB):
        hi = lo + SUB
        br = b[lo + SUB // 2 : lo + SUB // 2 + 1, :]
        row = lax.broadcasted_iota(jnp.int32, (SUB, hi), 0) + lo
        col = lax.broadcasted_iota(jnp.int32, (SUB, hi), 1)
        out.append((lo, hi, jnp.exp(b[lo:hi] - br), jnp.exp(br - b[:hi]), col <= row))
    return out


def _hgrn_fwd(proj, lb_logits, hg_norm):
    t = proj.shape[0]
    nc = t // CHUNK

    def body(q_ref, f_ref, i_ref, g_ref, lbl_ref, hgn_ref, y_ref, o_ref, st_ref, s_scr):
        c = pl.program_id(1)

        @pl.when(c == 0)
        def _():
            s_scr[...] = jnp.zeros_like(s_scr)

        hs = range(HG_BLOCK)
        sts = [s_scr[j] for j in hs]
        _, _, _, kk, _, qf, b = _hgrn_gates(q_ref[...], f_ref[...], lbl_ref[...])
        vb = _heads(i_ref[...].astype(BF16))
        bl = b[CHUNK - 1 : CHUNK, :]
        qe = _heads((qf * jnp.exp(b)).astype(BF16))
        kd = _heads((kk * jnp.exp(bl - b)).astype(BF16))
        decay = _heads(jnp.exp(bl))
        o = [_dot(qe[j], sts[j].astype(BF16), NT) for j in hs]
        parts = [[] for _ in hs]
        for lo, hi, ea, eb, mask in _intra_blocks(b):
            a, bk = _heads((qf[lo:hi] * ea).astype(BF16)), _heads((kk[:hi] * eb).astype(BF16))
            p = [jnp.where(mask, _dot(a[j], bk[j], NT), 0.0).astype(BF16) for j in hs]
            for j in hs:
                parts[j].append(_dot(p[j], vb[j][:hi]))
        o = [o[j] + jnp.concatenate(parts[j], axis=0) for j in hs]
        new = [sts[j] * decay[j] + _dot(vb[j], kd[j], TN) for j in hs]
        hgn = hgn_ref[...]
        on = [o[j] * lax.rsqrt(jnp.mean(o[j] * o[j], axis=-1, keepdims=True) + EPS) * hgn for j in hs]
        gg = g_ref[...]
        for j in hs:
            st_ref[j] = sts[j]
            s_scr[j] = new[j]
        o_ref[...] = _wide(o)
        y_ref[...] = (_wide(on) * (gg * _sigmoid(gg))).astype(BF16)

    wide = HG_BLOCK * HEAD_DIM
    groups = HG_HEADS // HG_BLOCK

    def col(k):
        return pl.BlockSpec((CHUNK, wide), lambda g, c: (c, k * groups + g))

    out = pl.BlockSpec((CHUNK, wide), lambda g, c: (c, g))
    return pl.pallas_call(
        body,
        name="hgrn_fwd",
        grid=(groups, nc),
        in_specs=[col(0), col(1), col(2), col(3), pl.BlockSpec((2, wide), lambda g, c: (0, g)), pl.BlockSpec((1, HEAD_DIM), lambda g, c: (0, 0))],
        out_specs=[out, out, pl.BlockSpec((HG_BLOCK, None, HEAD_DIM, HEAD_DIM), lambda g, c: (g, c, 0, 0))],
        out_shape=[
            jax.ShapeDtypeStruct((t, HG_WIDTH + ATT_WIDTH), BF16),
            jax.ShapeDtypeStruct((t, HG_WIDTH), F32),
            jax.ShapeDtypeStruct((HG_HEADS, nc, HEAD_DIM, HEAD_DIM), F32),
        ],
        scratch_shapes=[pltpu.VMEM((HG_BLOCK, HEAD_DIM, HEAD_DIM), F32)],
        compiler_params=_params("arbitrary", "arbitrary"),
    )(proj, proj, proj, proj, lb_logits, hg_norm)


def _hgrn_bwd(proj, lb_logits, hg_norm, o_hg, dycat, states, dep=None):
    t = proj.shape[0]
    nc = t // CHUNK

    def body(q_ref, f_ref, i_ref, g_ref, lbl_ref, hgn_ref, o_ref, dy_ref, st_ref, dp_ref, dlbl_ref, dhgn_ref, dst_scr, dlb_scr):
        h = pl.program_id(0)
        c = pl.program_id(1)

        @pl.when(c == 0)
        def _():
            dst_scr[...] = jnp.zeros_like(dst_scr)
            dlb_scr[...] = jnp.zeros_like(dlb_scr)

        @pl.when((c == 0) & (h == 0))
        def _():
            dhgn_ref[...] = jnp.zeros_like(dhgn_ref)

        hs = range(HG_BLOCK)
        hgn = _wide([hgn_ref[...]] * HG_BLOCK)
        q, fp, gg, vi = q_ref[...], f_ref[...], g_ref[...], i_ref[...]
        lb, sig, f, kk, sq, qf, b = _hgrn_gates(q, fp, lbl_ref[...])
        o, dy = o_ref[...], dy_ref[...]
        sg = _sigmoid(gg)
        n = _wide([oh * lax.rsqrt(jnp.mean(oh * oh, axis=-1, keepdims=True) + EPS) for oh in _heads(o)])
        don = dy * (gg * sg)
        dgg = dy * (n * hgn) * (sg * (1.0 + gg * (1.0 - sg)))
        d_hgn = sum(_heads(jnp.sum(don * n, axis=0, keepdims=True)))
        dn = don * hgn
        do = _wide(
            [
                lax.rsqrt(jnp.mean(oh * oh, axis=-1, keepdims=True) + EPS) * (dnh - nh * jnp.mean(dnh * nh, axis=-1, keepdims=True))
                for oh, dnh, nh in zip(_heads(o), _heads(dn), _heads(n))
            ]
        )
        sts = [st_ref[j] for j in hs]
        dstn = [dst_scr[j] for j in hs]
        bl = b[CHUNK - 1 : CHUNK, :]
        e_b, e_bl, e_l = jnp.exp(b), jnp.exp(bl - b), jnp.exp(bl)
        doh, vih = _heads(do), _heads(vi)
        dobh = _heads(do.astype(BF16))
        dq_acc = _wide([_dot3(doh[j], sts[j]) for j in hs]) * e_b
        dk_inter = _wide([_dot3(vih[j], dstn[j]) for j in hs]) * e_bl
        dk_acc = dk_inter
        kd = _heads((kk * e_bl).astype(BF16))
        dv_acc = _wide([_dot(kd[j], dstn[j].astype(BF16), NT) for j in hs])
        qe, decay = _heads((qf * e_b).astype(BF16)), _heads(e_l)
        dst_new = [dstn[j] * decay[j] + _dot(dobh[j], qe[j], TN) for j in hs]
        db_last = e_l * _wide([jnp.sum(sts[j] * dstn[j], axis=0, keepdims=True) for j in hs]) + jnp.sum(kk * dk_inter, axis=0, keepdims=True)
        dq_parts = []
        for lo, hi, ea, eb, mask in _intra_blocks(b):
            a, bk = qf[lo:hi] * ea, kk[:hi] * eb
            ah, bkh = _heads(a), _heads(bk)
            abh, bkbh = _heads(a.astype(BF16)), _heads(bk.astype(BF16))
            p = [jnp.where(mask, _dot(abh[j], bkbh[j], NT), 0.0).astype(BF16) for j in hs]
            dp = [jnp.where(mask, _dot3(doh[j][lo:hi], vih[j][:hi], NT), 0.0) for j in hs]
            dq_parts.append(_wide([_dot3(dp[j], bkh[j]) for j in hs]) * ea)
            dki = _wide([_dot3(dp[j], ah[j], TN) for j in hs]) * eb
            dvi = _wide([_dot(p[j], dobh[j][lo:hi], TN) for j in hs])
            if hi < CHUNK:
                zeros = jnp.zeros((CHUNK - hi, HG_BLOCK * HEAD_DIM), F32)
                dki = jnp.concatenate([dki, zeros], axis=0)
                dvi = jnp.concatenate([dvi, zeros], axis=0)
            dk_acc = dk_acc + dki
            dv_acc = dv_acc + dvi
        dq_acc = dq_acc + jnp.concatenate(dq_parts, axis=0)
        rows = lax.broadcasted_iota(jnp.int32, dq_acc.shape, 0)
        db = qf * dq_acc - kk * dk_acc + jnp.where(rows == CHUNK - 1, db_last, 0.0)
        dlf = jnp.dot(_tri(CHUNK, True), db, precision=HIGHEST, preferred_element_type=F32)
        dfk = dlf / f - dk_acc
        for k, part in enumerate((dq_acc * (sq * (1.0 + q * (1.0 - sq))), (1.0 - lb) * dfk * sig * (1.0 - sig), dv_acc, dgg)):
            dp_ref[:, k * HG_WIDTH : (k + 1) * HG_WIDTH] = part.astype(BF16)
        dlb_scr[...] += jnp.sum(dfk * (1.0 - sig), axis=0, keepdims=True)
        dhgn_ref[...] += d_hgn
        for j in hs:
            dst_scr[j] = dst_new[j]

        @pl.when(c == nc - 1)
        def _():
            dl0 = dlb_scr[...] * lb * (1.0 - lb)
            dlbl_ref[0:1, :] = dl0
            dlbl_ref[1:2, :] = -dl0

    wide = HG_BLOCK * HEAD_DIM
    groups = HG_HEADS // HG_BLOCK

    def col(k):
        return pl.BlockSpec((CHUNK, wide), lambda g, c: (nc - 1 - c, k * groups + g))

    blk = pl.BlockSpec((CHUNK, wide), lambda g, c: (nc - 1 - c, g))
    assert groups == 1, "d(q, f, i, g) are written as one contiguous column range of the in_proj gradient"
    return _pallas(
        body,
        9,
        dep,
        name="hgrn_bwd",
        grid=(groups, nc),
        in_specs=[
            col(0), col(1), col(2), col(3),
            pl.BlockSpec((2, wide), lambda g, c: (0, g)),
            pl.BlockSpec((1, HEAD_DIM), lambda g, c: (0, 0)),
            blk, blk,
            pl.BlockSpec((HG_BLOCK, None, HEAD_DIM, HEAD_DIM), lambda g, c: (g, nc - 1 - c, 0, 0)),
        ],
        out_specs=[
            pl.BlockSpec((CHUNK, 4 * HG_WIDTH), lambda g, c: (nc - 1 - c, 0)),
            pl.BlockSpec((2, wide), lambda g, c: (0, g)),
            pl.BlockSpec((1, HEAD_DIM), lambda g, c: (0, 0)),
        ],
        out_shape=[
            jax.ShapeDtypeStruct((t, 4 * HG_WIDTH + 3 * ATT_WIDTH), BF16),
            jax.ShapeDtypeStruct((2, HG_WIDTH), F32),
            jax.ShapeDtypeStruct((1, HEAD_DIM), F32),
        ],
        scratch_shapes=[pltpu.VMEM((HG_BLOCK, HEAD_DIM, HEAD_DIM), F32), pltpu.VMEM((1, wide), F32)],
        compiler_params=_params("arbitrary", "arbitrary"),
    )(proj, proj, proj, proj, lb_logits, hg_norm, o_hg, dycat, states)


def _diagonal_slots(shift):
    i = lax.broadcasted_iota(jnp.int32, (N_REL_PAD, DIAG), 0)
    u = lax.broadcasted_iota(jnp.int32, (N_REL_PAD, DIAG), 1)
    offset = u - shift if shift else jnp.where(u < K_BLOCK, u, u - DIAG)
    return jnp.where(jnp.clip(PAD - offset, -REL_CLIP, REL_CLIP) + REL_CLIP == i, 1.0, 0.0).astype(BF16)


def _split3(x):
    hi = x.astype(BF16)
    mid = (x - hi.astype(F32)).astype(BF16)
    return hi, mid, (x - hi.astype(F32) - mid.astype(F32)).astype(BF16)


def _bias_table(rel_bias, dep=None):
    def body(rb_ref, o_ref, diag):
        h = pl.program_id(0)

        @pl.when(h == 0)
        def _():
            hi, mid, lo = _split3(rb_ref[...])
            slots = _diagonal_slots(0)
            diag[...] = _dot(hi, slots) + (_dot(mid, slots) + _dot(lo, slots))

        rows = jnp.broadcast_to(diag[pl.ds(h, 1), :], (Q_BLOCK, DIAG))
        row = lax.broadcasted_iota(jnp.int32, (Q_BLOCK, K_BLOCK), 0)
        col = lax.broadcasted_iota(jnp.int32, (Q_BLOCK, K_BLOCK), 1)
        first = row - (row & (CHUNK - 1))
        seen = (col >= first) & (col < first + BAND)
        o_ref[...] = jnp.where(seen, pltpu.roll(rows, 0, 1, stride=1, stride_axis=0)[:, :K_BLOCK], MASKED)

    return _pallas(
        body,
        1,
        dep,
        name="bias_table",
        grid=(ATT_HEADS,),
        in_specs=[pl.BlockSpec((ATT_HEADS, N_REL_PAD), lambda h: (0, 0))],
        out_specs=pl.BlockSpec((None, Q_BLOCK, K_BLOCK), lambda h: (h, 0, 0)),
        out_shape=jax.ShapeDtypeStruct((ATT_HEADS, Q_BLOCK, K_BLOCK), F32),
        scratch_shapes=[pltpu.VMEM((ATT_HEADS, DIAG), F32)],
        compiler_params=_params("arbitrary"),
    )(rel_bias)


def _att_probs(q_ref, kpad, bias_ref, blk):
    qs = (q_ref[...] * ATT_SCALE).astype(BF16)
    start = pl.multiple_of(blk * Q_BLOCK, Q_BLOCK)
    kb = kpad[pl.ds(start, K_BLOCK), :]
    s = _dot(qs, kb, NT) + bias_ref[...]
    col = lax.broadcasted_iota(jnp.int32, (Q_BLOCK, K_BLOCK), 1)
    s = jnp.where(col >= PAD - blk * Q_BLOCK, s, MASKED)
    e = jnp.exp(s - jnp.max(s, axis=-1, keepdims=True))
    return qs, kb, start, e * (1.0 / jnp.sum(e, axis=-1, keepdims=True))


def _fill_padded(dst, src):
    dst[0:PAD, :] = jnp.zeros((PAD, HEAD_DIM), BF16)
    dst[PAD:, :] = src[...].astype(BF16)


def _att_fwd(proj, bias, y_mix, dep=None):
    t = proj.shape[0]
    nb = t // Q_BLOCK

    def body(q_ref, k_ref, v_ref, bias_ref, y_in_ref, y_ref, kpad, vpad):
        c = pl.program_id(1)

        @pl.when(c == 0)
        def _():
            _fill_padded(kpad, k_ref)
            _fill_padded(vpad, v_ref)

        _, _, start, p = _att_probs(q_ref, kpad, bias_ref, c)
        y_ref[...] = _dot(p.astype(BF16), vpad[pl.ds(start, K_BLOCK), :]).astype(BF16)

    base = 4 * HG_HEADS
    return _pallas(
        body,
        5,
        dep,
        name="att_fwd",
        grid=(ATT_HEADS, nb),
        in_specs=[
            pl.BlockSpec((Q_BLOCK, HEAD_DIM), lambda h, c: (c, base + h)),
            pl.BlockSpec((t, HEAD_DIM), lambda h, c: (0, base + ATT_HEADS + h)),
            pl.BlockSpec((t, HEAD_DIM), lambda h, c: (0, base + 2 * ATT_HEADS + h)),
            pl.BlockSpec((None, Q_BLOCK, K_BLOCK), lambda h, c: (h, 0, 0)),
            ANY,
        ],
        out_specs=pl.BlockSpec((Q_BLOCK, HEAD_DIM), lambda h, c: (c, HG_HEADS + h)),
        out_shape=jax.ShapeDtypeStruct(y_mix.shape, BF16),
        input_output_aliases={4: 0},
        scratch_shapes=[pltpu.VMEM((t + PAD, HEAD_DIM), BF16), pltpu.VMEM((t + PAD, HEAD_DIM), BF16)],
        compiler_params=_params("arbitrary", "arbitrary"),
    )(proj, proj, proj, bias, y_mix)


def _att_bwd(proj, bias, dycat, dproj, dep=None):
    t = proj.shape[0]
    nb = t // Q_BLOCK

    def body(q_ref, k_ref, v_ref, bias_ref, dy_ref, dp_in_ref, dq_ref, dk_ref, dv_ref, g_ref, kpad, vpad, dkacc, dvacc):
        c = pl.program_id(1)

        @pl.when(c == 0)
        def _():
            _fill_padded(kpad, k_ref)
            _fill_padded(vpad, v_ref)
            dkacc[...] = jnp.zeros_like(dkacc)
            dvacc[...] = jnp.zeros_like(dvacc)
            g_ref[...] = jnp.zeros_like(g_ref)

        qs, kb, start, p = _att_probs(q_ref, kpad, bias_ref, c)
        band = pl.ds(start, K_BLOCK)
        dyb = dy_ref[...].astype(BF16)
        dvacc[band, :] += _dot(p.astype(BF16), dyb, TN)
        dp = _dot(dyb, vpad[band, :], NT)
        ds = p * (dp - jnp.sum(dp * p, axis=-1, keepdims=True))
        g_ref[...] += ds
        dsb = ds.astype(BF16)
        dq_ref[...] = (_dot(dsb, kb) * ATT_SCALE).astype(BF16)
        dkacc[band, :] += _dot(dsb, qs, TN)

        @pl.when(c == nb - 1)
        def _():
            dk_ref[...] = dkacc[PAD:, :].astype(BF16)
            dv_ref[...] = dvacc[PAD:, :].astype(BF16)

    base = 4 * HG_HEADS
    whole = pl.BlockSpec((t, HEAD_DIM), lambda h, c: (0, h))
    return _pallas(
        body,
        6,
        dep,
        name="att_bwd",
        grid=(ATT_HEADS, nb),
        in_specs=[
            pl.BlockSpec((Q_BLOCK, HEAD_DIM), lambda h, c: (c, base + h)),
            pl.BlockSpec((t, HEAD_DIM), lambda h, c: (0, base + ATT_HEADS + h)),
            pl.BlockSpec((t, HEAD_DIM), lambda h, c: (0, base + 2 * ATT_HEADS + h)),
            pl.BlockSpec((None, Q_BLOCK, K_BLOCK), lambda h, c: (h, 0, 0)),
            pl.BlockSpec((Q_BLOCK, HEAD_DIM), lambda h, c: (c, HG_HEADS + h)),
            ANY,
        ],
        out_specs=[pl.BlockSpec((Q_BLOCK, HEAD_DIM), lambda h, c: (c, base + h)), whole, whole, pl.BlockSpec((None, Q_BLOCK, K_BLOCK), lambda h, c: (h, 0, 0))],
        input_output_aliases={5: 0},
        out_shape=[
            jax.ShapeDtypeStruct(dproj.shape, BF16),
            jax.ShapeDtypeStruct((t, ATT_WIDTH), BF16),
            jax.ShapeDtypeStruct((t, ATT_WIDTH), BF16),
            jax.ShapeDtypeStruct((ATT_HEADS, Q_BLOCK, K_BLOCK), F32),
        ],
        scratch_shapes=[
            pltpu.VMEM((t + PAD, HEAD_DIM), BF16),
            pltpu.VMEM((t + PAD, HEAD_DIM), BF16),
            pltpu.VMEM((t + PAD, HEAD_DIM), F32),
            pltpu.VMEM((t + PAD, HEAD_DIM), F32),
        ],
        compiler_params=_params("arbitrary", "arbitrary"),
    )(proj, proj, proj, bias, dycat, dproj)


def _rel_bias_grad(gsum):
    def body(g_ref, o_ref):
        r = lax.broadcasted_iota(jnp.int32, (Q_BLOCK, Q_BLOCK), 0)
        c = lax.broadcasted_iota(jnp.int32, (Q_BLOCK, Q_BLOCK), 1)
        flip = jnp.where(r + c == Q_BLOCK - 1, 1.0, 0.0).astype(BF16)
        sums = []
        for h in range(ATT_HEADS):
            hi, mid, lo = _split3(g_ref[h])
            rev = _dot(flip, hi) + (_dot(flip, mid) + _dot(flip, lo))
            wide = jnp.concatenate([rev, jnp.zeros((Q_BLOCK, DIAG - K_BLOCK), F32)], axis=1)
            sums.append(jnp.sum(pltpu.roll(wide, 0, 1, stride=1, stride_axis=0), axis=0, keepdims=True))
        hi, mid, lo = _split3(jnp.concatenate(sums, axis=0))
        slots = _diagonal_slots(Q_BLOCK - 1)
        o_ref[...] = _dot(hi, slots, NT) + (_dot(mid, slots, NT) + _dot(lo, slots, NT))

    return pl.pallas_call(
        body,
        name="rel_bias_grad",
        out_shape=jax.ShapeDtypeStruct((ATT_HEADS, N_REL_PAD), F32),
        compiler_params=_params(),
    )(gsum)


HALO = 16


FF_TILE = 1408
FF_TILES = D_FF // FF_TILE


def _interleave_cols(a):
    lead = a.shape[:-1]
    return jnp.swapaxes(a.reshape(*lead, 2, FF_TILES, FF_TILE), -3, -2).reshape(*lead, 2 * D_FF)


def _deinterleave_cols(a):
    lead = a.shape[:-1]
    return jnp.swapaxes(a.reshape(*lead, FF_TILES, 2, FF_TILE), -3, -2).reshape(*lead, 2 * D_FF)


def _ffn_specs(t, tm):
    wide = 2 * FF_TILE
    tile = pl.BlockSpec((tm, wide), lambda j, i: (i, j))
    before = pl.BlockSpec((HALO, wide), lambda j, i: (jnp.maximum(i * (tm // HALO) - 1, 0), j))
    after = pl.BlockSpec((HALO, wide), lambda j, i: (jnp.minimum((i + 1) * (tm // HALO), t // HALO - 1), j))
    vec = lambda rows: pl.BlockSpec((rows, wide), lambda j, i: (0, j))
    return tile, before, after, vec


def _shifted(x, rows, offsets):
    r = lax.broadcasted_iota(jnp.int32, (rows, x.shape[0]), 0)
    c = lax.broadcasted_iota(jnp.int32, (rows, x.shape[0]), 1)
    pick = jnp.concatenate([jnp.where(c == r + o, 1.0, 0.0).astype(BF16) for o in offsets], axis=0)
    out = _dot(pick, x)
    return [out[k * rows : (k + 1) * rows] for k in range(len(offsets))]


def _conv(x, w, b, rows):
    taps = _shifted(x, rows, [HALO - 2, HALO - 1]) + [x[HALO : HALO + rows].astype(F32)]
    return b + w[0:1] * taps[0] + w[1:2] * taps[1] + w[2:3] * taps[2], taps


def _ffn_act_fwd(u, conv_w, conv_b):
    t = u.shape[0]
    tm = _tile(t, (128,))
    tile, before, _, vec = _ffn_specs(t, tm)

    def body(u_ref, h_ref, w_ref, b_ref, z_ref):
        first = pl.program_id(1) == 0
        halo = h_ref[...]
        x = jnp.concatenate([jnp.where(first, jnp.zeros_like(halo), halo), u_ref[...]], axis=0)
        c, _ = _conv(x, w_ref[...], b_ref[...], tm)
        gate, val = c[:, :FF_TILE], c[:, FF_TILE:]
        z_ref[...] = (gate * _sigmoid(gate) * val).astype(BF16)

    return pl.pallas_call(
        body,
        name="ffn_act_fwd",
        grid=(FF_TILES, t // tm),
        in_specs=[tile, before, vec(3), vec(1)],
        out_specs=pl.BlockSpec((tm, FF_TILE), lambda j, i: (i, j)),
        out_shape=jax.ShapeDtypeStruct((t, D_FF), BF16),
        compiler_params=_params("parallel", "parallel"),
    )(u, u, conv_w, conv_b)


def _ffn_act_bwd(u, dz, conv_w, conv_b, dep=None):
    t = u.shape[0]
    tm = _tile(t, (128,))
    nt = t // tm
    ext = tm + HALO
    tile, before, after, vec = _ffn_specs(t, tm)

    def body(u_ref, ub_ref, ua_ref, w_ref, b_ref, dz_ref, dza_ref, du_ref, dw_ref, db_ref):
        i = pl.program_id(1)
        first, last = i == 0, i == nt - 1
        ub, ua = ub_ref[...], ua_ref[...]
        parts = [jnp.where(first, jnp.zeros_like(ub), ub), u_ref[...], jnp.where(last, jnp.zeros_like(ua), ua)]
        w = w_ref[...]
        c, taps = _conv(jnp.concatenate(parts, axis=0), w, b_ref[...], ext)
        gate, val = c[:, :FF_TILE], c[:, FF_TILE:]
        dz = jnp.concatenate([dz_ref[...].astype(F32), jnp.where(last, 0.0, dza_ref[...].astype(F32))], axis=0)
        sg = _sigmoid(gate)
        d = jnp.concatenate([dz * val * (sg * (1.0 + gate * (1.0 - sg))), dz * (gate * sg)], axis=1)
        d1, d2 = _shifted(d.astype(BF16), tm, [1, 2])
        du_ref[...] = (w[2:3] * d[:tm] + w[1:2] * d1 + w[0:1] * d2).astype(BF16)

        @pl.when(first)
        def _():
            dw_ref[...] = jnp.zeros_like(dw_ref)
            db_ref[...] = jnp.zeros_like(db_ref)

        for k, tap in enumerate(taps):
            dw_ref[k : k + 1, :] += jnp.sum(d[:tm] * tap[:tm], axis=0, keepdims=True)
        db_ref[...] += jnp.sum(d[:tm], axis=0, keepdims=True)

    narrow = lambda rows, index: pl.BlockSpec((rows, FF_TILE), index)
    return _pallas(
        body,
        7,
        dep,
        name="ffn_act_bwd",
        grid=(FF_TILES, nt),
        in_specs=[
            tile, before, after, vec(3), vec(1),
            narrow(tm, lambda j, i: (i, j)),
            narrow(HALO, lambda j, i: (jnp.minimum((i + 1) * (tm // HALO), t // HALO - 1), j)),
        ],
        out_specs=[tile, vec(3), vec(1)],
        out_shape=[
            jax.ShapeDtypeStruct((t, 2 * D_FF), BF16),
            jax.ShapeDtypeStruct((3, 2 * D_FF), F32),
            jax.ShapeDtypeStruct((1, 2 * D_FF), F32),
        ],
        compiler_params=_params("parallel", "arbitrary"),
    )(u, u, u, conv_w, conv_b, dz, dz)


def _ple_loss(gpre, pp, h2, final_norm, target):
    t, d = h2.shape
    tm = _tile(t, (256,))

    def body(gp_ref, pp_ref, h_ref, g_ref, tg_ref, dh_ref, dgp_ref, dpp_ref, dg_ref, loss_ref):
        i = pl.program_id(0)
        gate = _sigmoid(gp_ref[...])
        ppv = pp_ref[...]
        h3 = h_ref[...] + gate * ppv
        r = lax.rsqrt(jnp.mean(h3 * h3, axis=-1, keepdims=True) + EPS)
        n = h3 * r
        g = g_ref[...]
        err = n * g - tg_ref[...]
        loss = 0.5 * jnp.sum(jnp.mean(err * err, axis=-1, keepdims=True))
        dy = err * (1.0 / d)
        dn = dy * g
        dh = r * (dn - n * jnp.mean(dn * n, axis=-1, keepdims=True))
        dh_ref[...] = dh
        dgp_ref[...] = (dh * ppv * gate * (1.0 - gate)).astype(BF16)
        dpp_ref[...] = (dh * gate).astype(BF16)
        dg = jnp.sum(dy * n, axis=0, keepdims=True)

        @pl.when(i == 0)
        def _():
            dg_ref[...] = dg
            loss_ref[...] = jnp.full(loss_ref.shape, loss, F32)

        @pl.when(i > 0)
        def _():
            dg_ref[...] += dg
            loss_ref[...] += loss

    row = pl.BlockSpec((tm, d), lambda i: (i, 0))
    vec = pl.BlockSpec((1, d), lambda i: (0, 0))
    return pl.pallas_call(
        body,
        name="ple_loss",
        grid=(t // tm,),
        in_specs=[row, row, row, vec, row],
        out_specs=[row, row, row, vec, pl.BlockSpec((8, 128), lambda i: (0, 0))],
        out_shape=[
            jax.ShapeDtypeStruct((t, d), F32),
            jax.ShapeDtypeStruct((t, d), BF16),
            jax.ShapeDtypeStruct((t, d), BF16),
            jax.ShapeDtypeStruct((1, d), F32),
            jax.ShapeDtypeStruct((8, 128), F32),
        ],
        compiler_params=_params("arbitrary"),
    )(gpre, pp, h2, final_norm, target)


def _adamw(w, g, m, v):
    m = ADAM_B1 * m + (1.0 - ADAM_B1) * g
    v = ADAM_B2 * v + (1.0 - ADAM_B2) * (g * g)
    m_hat = m / (1.0 - ADAM_B1 ** ADAM_STEP)
    v_hat = v / (1.0 - ADAM_B2 ** ADAM_STEP)
    return -ADAM_LR * (m_hat / (jnp.sqrt(v_hat) + ADAM_EPS) + ADAM_WD * w), m, v


def _adam_big(w, m, v, own, recv, name, dep=None):
    r, c = w.shape
    tr = _tile(r, (256, 176))

    def body(w_ref, m_ref, v_ref, own_ref, recv_ref, g_ref, d_ref, nm_ref, nv_ref):
        g = own_ref[...]
        for k in range(3):
            g = g + recv_ref[k].astype(F32)
        g_ref[...] = g
        d_ref[...], nm_ref[...], nv_ref[...] = _adamw(w_ref[...], g, m_ref[...], v_ref[...])

    blk = pl.BlockSpec((tr, c), lambda i: (i, 0))
    return _pallas(
        body,
        5,
        dep,
        name=name,
        grid=(r // tr,),
        in_specs=[blk, blk, blk, blk, pl.BlockSpec((3, tr, c), lambda i: (0, i, 0))],
        out_specs=[blk] * 4,
        out_shape=[jax.ShapeDtypeStruct((r, c), F32)] * 4,
        compiler_params=_params("parallel"),
    )(w, m, v, own, recv)


def _adam_small(w, g, m, v):
    def body(w_ref, g_ref, m_ref, v_ref, d_ref, nm_ref, nv_ref):
        d_ref[...], nm_ref[...], nv_ref[...] = _adamw(w_ref[...], g_ref[...], m_ref[...], v_ref[...])

    return pl.pallas_call(body, name="adam_small", out_shape=[jax.ShapeDtypeStruct(w.shape, F32)] * 3, compiler_params=_params())(w, g, m, v)


def _position():
    return lax.axis_index("x"), lax.axis_index("y"), lax.axis_index("c")


def _other_chips(x, y):
    return [(1 - x, y), (x, 1 - y), (1 - x, 1 - y)]


def _block_index(dev, interleaved):
    x, y, c = dev
    return 4 * y + 2 * c + x if interleaved else 4 * x + 2 * y + c


def _shard_of(ref, axis, size, dev, interleaved=False):
    start = pl.multiple_of(_block_index(dev, interleaved) * size, 128 if axis == 1 else 16)
    return ref.at[:, pl.ds(start, size)] if axis == 1 else ref.at[pl.ds(start, size), :]


def _add_blocks(ids, grad, landed, axis, size, targets, out_dtype, name):
    rows = size if axis == 0 else grad.shape[0]
    cols = size if axis == 1 else grad.shape[1]
    tr = _tile(rows, (256, 176))
    nr = rows // tr
    nt = len(targets)

    def body(ids_ref, g_ref, l_ref, o_ref):
        o_ref[...] = (g_ref[...] + l_ref[...]).astype(out_dtype)

    if axis == 1:
        g_spec = pl.BlockSpec((tr, cols), lambda k, i, ids: (i, ids[targets[0] + k]))
    else:
        g_spec = pl.BlockSpec((tr, cols), lambda k, i, ids: (ids[targets[0] + k] * nr + i, 0))
    return pl.pallas_call(
        body,
        name=name,
        grid_spec=pltpu.PrefetchScalarGridSpec(
            num_scalar_prefetch=1,
            grid=(nt, nr),
            in_specs=[g_spec, pl.BlockSpec((None, tr, cols), lambda k, i, ids: (ids[4 + targets[0] + k], i, 0))],
            out_specs=pl.BlockSpec((None, tr, cols), lambda k, i, ids: (k, i, 0)),
        ),
        out_shape=jax.ShapeDtypeStruct((nt, rows, cols), out_dtype),
        compiler_params=_params("parallel", "parallel"),
    )(ids, grad, landed)


def _all_reduce_small(vec, name):
    rows = vec.shape[0]

    def body(v_ref, o_ref, land, send_sems, recv_sems):
        x, y, c = _position()
        mine = 4 * x + 2 * y + c
        copies = []
        for mask in range(1, N_DEV):
            peer = (1 - x if mask & 4 else x, 1 - y if mask & 2 else y, 1 - c if mask & 1 else c)
            copies.append(
                pltpu.make_async_remote_copy(
                    src_ref=v_ref, dst_ref=land.at[mine], send_sem=send_sems.at[mask - 1], recv_sem=recv_sems.at[mask - 1], device_id=peer, device_id_type=MESH
                )
            )
        for cp in copies:
            cp.start()
        land[mine] = v_ref[...]
        for cp in copies:
            cp.wait()
        acc = land[0]
        for k in range(1, N_DEV):
            acc = acc + land[k]
        o_ref[...] = acc

    return pl.pallas_call(
        body,
        name=name,
        out_shape=jax.ShapeDtypeStruct(vec.shape, F32),
        in_specs=[pl.BlockSpec(memory_space=pltpu.VMEM)],
        out_specs=pl.BlockSpec(memory_space=pltpu.VMEM),
        scratch_shapes=[pltpu.VMEM((N_DEV, rows, 128), F32), pltpu.SemaphoreType.DMA((N_DEV - 1,)), pltpu.SemaphoreType.DMA((N_DEV - 1,))],
    )(vec)


def _rows128(a, rows):
    flat = a.reshape(-1)
    return jnp.pad(flat, (0, rows * 128 - flat.shape[0])).reshape(rows, 128)


def _pad_rel(a):
    return jnp.pad(a.reshape(ATT_HEADS, -1)[:, :N_REL], ((0, 0), (0, N_REL_PAD - N_REL)))


SMALL = [("norm_mix", 16), ("lb_logits", 16), ("hg_norm", 8), ("rel_bias", 24), ("norm_ffn", 16), ("conv_b", 88), ("norm_ple", 16), ("final_norm", 16)]
CONV_W_FULL_ROWS = 3 * 2 * D_FF // 128
CONV_W_SHARD_ROWS = 40


def _pack_small(parts):
    return jnp.concatenate([_rows128(_pad_rel(parts[k]) if k == "rel_bias" else parts[k], rows) for k, rows in SMALL], axis=0)


def _unpack_small(packed, shapes):
    out, at = {}, 0
    for k, rows in SMALL:
        blk = packed[at : at + rows]
        at += rows
        if k == "rel_bias":
            out[k] = blk.reshape(ATT_HEADS, N_REL_PAD)[:, :N_REL].reshape(shapes[k])
        else:
            n = 1
            for s in shapes[k]:
                n *= s
            out[k] = blk.reshape(-1)[:n].reshape(shapes[k])
    return out, at


BIG = [("w_in", 1), ("w_out", 0), ("w_up", 1), ("w_down", 0), ("w_ple_gate", 0), ("w_ple_proj", 1)]


HBM = pl.BlockSpec(memory_space=pltpu.HBM)
SEM = pl.BlockSpec(memory_space=pltpu.SEMAPHORE)
EFFECT = pltpu.SideEffectType.DATAFLOW_SIDE_EFFECTING


def _copies(plan, refs, send_sems, recv_sems):
    return [
        pltpu.make_async_remote_copy(src_ref=src, dst_ref=dst, send_sem=send_sems.at[i], recv_sem=recv_sems.at[i], device_id=dev, device_id_type=MESH)
        for i, (src, dst, dev) in enumerate(plan(refs))
    ]


def _split_start(name, arrays, plan, n):
    k = len(arrays)

    def body(*refs):
        for cp in _copies(plan, refs[:k], refs[k], refs[k + 1]):
            cp.start()
        refs[-1][...] = jnp.zeros_like(refs[-1])

    out = pl.pallas_call(
        body,
        name=name,
        out_shape=(pltpu.SemaphoreType.DMA((n,)), pltpu.SemaphoreType.DMA((n,)), *[pltpu.HBM(a.shape, a.dtype) for a in arrays], jax.ShapeDtypeStruct((8, 128), F32)),
        in_specs=[HBM] * k,
        out_specs=(SEM, SEM, *[HBM] * k, pl.BlockSpec(memory_space=pltpu.VMEM)),
        input_output_aliases={i: 2 + i for i in range(k)},
        compiler_params=pltpu.CompilerParams(has_side_effects=EFFECT),
    )(*[pltpu.with_memory_space_constraint(a, pltpu.HBM) for a in arrays])
    return out[0], out[1], list(out[2 : 2 + k]), out[-1]


def _split_wait(name, send, recv, arrays, plan, after):
    k = len(arrays)

    def body(*refs):
        for cp in _copies(plan, refs[:k], refs[k], refs[k + 1]):
            cp.wait_send()
            cp.wait_recv()

    out = pl.pallas_call(
        body,
        name=name,
        out_shape=tuple(pltpu.HBM(a.shape, a.dtype) for a in arrays),
        in_specs=[HBM] * k + [SEM, SEM, ANY],
        out_specs=tuple([HBM] * k),
        input_output_aliases={i: i for i in range(k)},
        compiler_params=pltpu.CompilerParams(has_side_effects=EFFECT),
    )(*arrays, send, recv, after)
    return list(out)


def _cast_into(w, me, axis, name, dep, dtype):
    r, c = w.shape
    tr = _tile(r, (256, 176))
    nr = r // tr
    deps = [] if dep is None else [dep]

    def body(me_ref, w_ref, *rest):
        rest[-1][...] = w_ref[...].astype(dtype)

    if axis == 1:
        shape, o_spec = (r, N_DEV * c), pl.BlockSpec((tr, c), lambda i, me: (i, me[0]))
    else:
        shape, o_spec = (N_DEV * r, c), pl.BlockSpec((tr, c), lambda i, me: (me[0] * nr + i, 0))
    return pl.pallas_call(
        body,
        name=name,
        grid_spec=pltpu.PrefetchScalarGridSpec(
            num_scalar_prefetch=1, grid=(nr,), in_specs=[pl.BlockSpec((tr, c), lambda i, me: (i, 0))] + [ANY] * len(deps), out_specs=o_spec
        ),
        out_shape=jax.ShapeDtypeStruct(shape, dtype),
        compiler_params=_params("parallel"),
    )(me, w, *deps)


GATHER = [
    (["w_in"], None, "norm_mix_fwd", "bias_table"),
    (["w_out"], "norm_mix_fwd", "att_fwd", None),
    (["w_up", "conv_w"], "norm_mix_fwd", "att_fwd", "norm_ffn_fwd"),
    (["w_down", "w_ple_gate", "w_ple_proj"], "att_fwd", "up_proj", "ffn_act_fwd"),
]
GROUPS = [["w_ple_proj", "w_ple_gate", "w_down"], ["w_up"], ["w_out"], ["w_in"]]
STAGES = ["ffn_act_bwd", "d_mix_out", "hgrn_bwd", "d_norm_mix_out"]
INTERLEAVED = {"w_up", "conv_w"}


class _Exchange:
    def __init__(self, big, conv_w, position):
        self.big, self.axis = big, dict(BIG, conv_w=1)
        self.shards = dict(big, conv_w=conv_w)
        self.size = {k: w.shape[self.axis[k]] for k, w in self.shards.items()}
        self.x, self.y, self.c = position
        chips = [(self.x, self.y)] + _other_chips(self.x, self.y)
        landed = [2 * cx + cy for cx, cy in chips]
        self.ids = {
            flag: jnp.stack([_block_index((cx, cy, self.c), flag) for cx, cy in chips] + landed).astype(jnp.int32) for flag in (False, True)
        }
        self.tokens, self.grads, self.state, self.wfull = [], {}, {}, {}


    def _slot(self, ref, k, dev):
        return _shard_of(ref, self.axis[k], self.size[k], dev, interleaved=k in INTERLEAVED)

    def _plan_gather(self, names, direct, refs):
        x, y, c = _position()
        me, out = (x, y, c), []
        for k, ref in zip(names, refs):
            mine = self._slot(ref, k, me)
            out.append((mine, mine, (x, y, 1 - c)))
            out += [(mine, mine, (*chip, c)) for chip in _other_chips(x, y)]
            if direct:
                out += [(mine, mine, (*chip, 1 - c)) for chip in _other_chips(x, y)]
        return out

    def _plan_forward(self, names, refs):
        x, y, c = _position()
        out = []
        for k, ref in zip(names, refs):
            for chip in _other_chips(x, y):
                block = self._slot(ref, k, (*chip, c))
                out.append((block, block, (x, y, 1 - c)))
        return out

    def _plan_sibling(self, names, refs):
        x, y, c = _position()
        n = len(names)
        return [(self._slot(refs[i], k, (p // 2, p % 2, 1 - c)), refs[n + i].at[p], (x, y, 1 - c)) for i, k in enumerate(names) for p in range(4)]

    def _plan_chips(self, names, refs):
        x, y, c = _position()
        n = len(names)
        return [(refs[i].at[j], refs[n + i].at[j], (*chip, c)) for i in range(n) for j, chip in enumerate(_other_chips(x, y))]


    def gather(self):
        me = {flag: _block_index((self.x, self.y, self.c), flag).astype(jnp.int32).reshape(1) for flag in (False, True)}
        self.late, self.unsent = {}, {}
        after = None
        for gi, (names, issued, *_) in enumerate(GATHER):
            self.unsent[gi] = [
                _cast_into(self.shards[k], me[k in INTERLEAVED], self.axis[k], "cast_" + k, after, F32 if k == "conv_w" else BF16) for k in names
            ]
            if issued is None:
                self._issue(None)
                after = self.tokens[-1]
        self.tokens += [a for arrays in self.unsent.values() for a in arrays]

    def _issue(self, stage):
        for gi, (names, issued, _, forwarded) in enumerate(GATHER):
            if issued == stage and gi in self.unsent:
                plan = functools.partial(self._plan_gather, names, forwarded is None)
                copies = (7 if forwarded is None else 4) * len(names)
                send, recv, fulls, token = _split_start(f"gather_start_{gi}", self.unsent.pop(gi), plan, copies)
                self.tokens.append(token)
                self.late[gi] = (send, recv, fulls, plan)

    def weight(self, k):
        return self.wfull[k]

    def dep(self):
        tokens, self.tokens = self.tokens, []
        return tokens

    def reduce(self, vec, name):
        return _all_reduce_small(vec, name)

    def grad(self, k, g):
        self.grads[k] = g
        for gi, names in enumerate(GROUPS):
            if k == names[-1]:
                plan = functools.partial(self._plan_sibling, names)
                lands = [lax.empty((4, *self._shard_shape(n)), F32) for n in names]
                send, recv, arrays, token = _split_start(f"sibling_start_{gi}", [self.grads[n] for n in names] + lands, plan, 4 * len(names))
                self.tokens.append(token)
                self.state[gi] = (send, recv, arrays, plan)

    def done(self, stage, after):
        for gi, (names, _, _, forwarded) in enumerate(GATHER):
            if forwarded == stage:
                send, recv, fulls, plan = self.late[gi]
                self.wfull.update(zip(names, _split_wait(f"forward_wait_{gi}", send, recv, fulls, plan, after)))
        for gi, (names, _, arrived, forwarded) in enumerate(GATHER):
            if arrived == stage:
                send, recv, fulls, plan = self.late[gi]
                fulls = _split_wait(f"gather_wait_{gi}", send, recv, fulls, plan, after)
                if forwarded is None:
                    self.wfull.update(zip(names, fulls))
                else:
                    plan = functools.partial(self._plan_forward, names)
                    send, recv, fulls, token = _split_start(f"forward_start_{gi}", fulls, plan, 3 * len(names))
                    self.tokens.append(token)
                    self.late[gi] = (send, recv, fulls, plan)
        self._issue(stage)
        if stage in STAGES:
            self._to_chips(STAGES.index(stage), after)

    def _shard_shape(self, k):
        shape = list(self.grads[k].shape)
        shape[self.axis[k]] = self.size[k]
        return tuple(shape)

    def _to_chips(self, gi, after):
        names = GROUPS[gi]
        n = len(names)
        send, recv, arrays, plan = self.state[gi]
        arrays = _split_wait(f"sibling_wait_{gi}", send, recv, arrays, plan, after)
        own, parts = [], []
        for k, g, land in zip(names, arrays[:n], arrays[n:]):
            ids = self.ids[k in INTERLEAVED]
            own.append(_add_blocks(ids, g, land, self.axis[k], self.size[k], [0], F32, "add_own_" + k)[0])
            parts.append(_add_blocks(ids, g, land, self.axis[k], self.size[k], [1, 2, 3], BF16, "add_send_" + k))
        plan = functools.partial(self._plan_chips, names)
        lands = [lax.empty(part.shape, BF16) for part in parts]
        send, recv, arrays, token = _split_start(f"chips_start_{gi}", parts + lands, plan, 3 * n)
        self.tokens.append(token)
        self.state[gi] = (send, recv, arrays, plan, own)

    def finish(self, gi, after):
        names = GROUPS[gi]
        send, recv, arrays, plan, own = self.state[gi]
        arrays = _split_wait(f"chips_wait_{gi}", send, recv, arrays, plan, after)
        return {k: (o, r) for k, o, r in zip(names, own, arrays[len(names) :])}


class _Resident:
    def __init__(self, wfull):
        self.wfull, self.grads = wfull, {}

    def weight(self, k):
        return self.wfull[k]

    def grad(self, k, g):
        self.grads[k] = g

    def dep(self):
        return None

    def reduce(self, vec, name):
        return vec

    def done(self, stage, after):
        pass


def _local_step(x, p, target, small, ex):
    bias = _bias_table(jnp.pad(small["rel_bias"], ((0, 0), (0, N_REL_PAD - N_REL))), dep=ex.dep())
    a1, r1 = _rms_fwd(x, small["norm_mix"], "norm_mix_fwd", dep=[bias])
    ex.done("norm_mix_fwd", a1)
    ex.done("bias_table", a1)
    proj = _matmul(a1, ex.weight("w_in"), "nn", F32, "in_proj", dep=ex.dep())
    y_hg, o_hg, states = _hgrn_fwd(proj, small["lb_logits"], small["hg_norm"])
    ycat = _att_fwd(proj, bias, y_hg, dep=ex.dep())
    ex.done("att_fwd", ycat)
    h1 = _matmul(ycat, ex.weight("w_out"), "nn", F32, "out_proj", resid=x, dep=ex.dep())
    a2, r2 = _rms_fwd(h1, small["norm_ffn"], "norm_ffn_fwd")
    ex.done("norm_ffn_fwd", a2)
    conv_w = ex.weight("conv_w")
    u = _matmul(a2, ex.weight("w_up"), "nn", BF16, "up_proj")
    conv_b = _interleave_cols(small["conv_b"])
    ex.done("up_proj", u)
    z = _ffn_act_fwd(u, conv_w, conv_b)
    ex.done("ffn_act_fwd", z)
    h2 = _matmul(z, ex.weight("w_down"), "nn", F32, "down_proj", tk=2816, resid=h1)
    a3, r3 = _rms_fwd(h2, small["norm_ple"], "norm_ple_fwd")
    gpre = _matmul(a3, ex.weight("w_ple_gate"), "nn", F32, "ple_gate")
    pp = _matmul(p, ex.weight("w_ple_proj"), "nn", F32, "ple_proj")
    dh3, dgpre, dpp, d_final, loss = _ple_loss(gpre, pp, h2, small["final_norm"], target)

    ex.grad("w_ple_proj", _matmul(p, dpp, "tn", F32, "d_w_ple_proj", tm=512))
    ex.grad("w_ple_gate", _matmul(a3, dgpre, "tn", F32, "d_w_ple_gate", tm=512))
    da3 = _matmul(dgpre, ex.weight("w_ple_gate"), "nt", F32, "d_norm_ple_out")
    dh2, d_ple = _rms_bwd(da3, h2, r3, small["norm_ple"], dh3, "norm_ple_bwd")
    dz = _matmul(dh2, ex.weight("w_down"), "nt", BF16, "d_ffn_act")
    ex.grad("w_down", _matmul(z, dh2, "tn", F32, "d_w_down", tm=512))
    du, dcw, dcb = _ffn_act_bwd(u, dz, conv_w, conv_b, dep=ex.dep())
    ex.done("ffn_act_bwd", du)
    d_conv_w, d_conv_b = _deinterleave_cols(dcw), _deinterleave_cols(dcb)
    ex.grad("w_up", _matmul(a2, du, "tn", F32, "d_w_up", tm=512, dep=ex.dep()))
    da2 = _matmul(du, ex.weight("w_up"), "nt", F32, "d_norm_ffn_out", tk=2816, dep=ex.dep())
    dh1, d_ffn = _rms_bwd(da2, h1, r2, small["norm_ffn"], dh2, "norm_ffn_bwd")
    dycat = _matmul(dh1, ex.weight("w_out"), "nt", F32, "d_mix_out")
    ex.done("d_mix_out", dycat)
    ex.grad("w_out", _matmul(ycat, dh1, "tn", F32, "d_w_out", tk=2048, dep=ex.dep()))
    dp_hg, d_lb, d_hgn = _hgrn_bwd(proj, small["lb_logits"], small["hg_norm"], o_hg, dycat, states, dep=ex.dep())
    ex.done("hgrn_bwd", d_lb)
    dproj, dk_att, dv_att, gsum = _att_bwd(proj, bias, dycat, dp_hg, dep=ex.dep())
    d_rel = _rel_bias_grad(gsum)
    d_small = {
        "norm_mix": jnp.zeros_like(small["norm_mix"]), "lb_logits": d_lb, "hg_norm": d_hgn, "rel_bias": d_rel, "norm_ffn": d_ffn,
        "conv_b": d_conv_b, "norm_ple": d_ple, "final_norm": d_final,
    }
    packed = jnp.concatenate([_pack_small(d_small), _rows128(d_conv_w, CONV_W_FULL_ROWS), _rows128(loss[0:1, 0:1], 8)], axis=0)
    early = ex.reduce(packed, "all_reduce_small")
    for k, part in enumerate((dk_att, dv_att)):
        dproj = lax.dynamic_update_slice(dproj, part, (0, 4 * HG_WIDTH + (k + 1) * ATT_WIDTH))
    ex.grad("w_in", _matmul(a1, dproj, "tn", F32, "d_w_in", tm=512, dep=[early]))
    da1 = _matmul(dproj, ex.weight("w_in"), "nt", F32, "d_norm_mix_out", tk=1792, dep=ex.dep())
    dx, d_mix = _rms_bwd(da1, x, r1, small["norm_mix"], dh1, "norm_mix_bwd")
    rows = dict(SMALL)["norm_mix"]
    late = ex.reduce(_rows128(d_mix, rows), "all_reduce_norm_mix")
    ex.done("d_norm_mix_out", late)
    return dx, jnp.concatenate([late, early[rows:]], axis=0)


def kernel(x, p, norm_mix, w_in, lb_logits, hg_norm, rel_bias, w_out, norm_ffn, w_up, conv_w, conv_b, w_down, norm_ple, w_ple_gate, w_ple_proj, final_norm, loss_target, m_norm_mix, m_w_in, m_lb_logits, m_hg_norm, m_rel_bias, m_w_out, m_norm_ffn, m_w_up, m_conv_w, m_conv_b, m_w_down, m_norm_ple, m_w_ple_gate, m_w_ple_proj, m_final_norm, v_norm_mix, v_w_in, v_lb_logits, v_hg_norm, v_rel_bias, v_w_out, v_norm_ffn, v_w_up, v_conv_w, v_conv_b, v_w_down, v_norm_ple, v_w_ple_gate, v_w_ple_proj, v_final_norm):
    given = dict(locals())
    mx, my, mc = _position()
    me = 4 * mx + 2 * my + mc
    big = {k: given[k][0] for k, _ in BIG}
    ex = _Exchange(big, conv_w[0], (mx, my, mc))
    ex.gather()

    small = {
        "norm_mix": norm_mix, "lb_logits": lb_logits, "hg_norm": hg_norm, "rel_bias": rel_bias[0], "norm_ffn": norm_ffn,
        "conv_b": conv_b, "norm_ple": norm_ple, "final_norm": final_norm.reshape(1, -1),
    }
    dx, reduced = _local_step(x[0], p[0, 0], loss_target[0], small, ex)

    out = {}
    shapes = {k: given[k].shape for k, _ in SMALL}
    g_small, at = _unpack_small(reduced, shapes)
    g_conv_full = reduced[at : at + CONV_W_FULL_ROWS].reshape(3, 2 * D_FF)
    total_loss = reduced[at + CONV_W_FULL_ROWS, 0]
    cw = conv_w.shape[2]
    g_conv = lax.dynamic_slice_in_dim(g_conv_full, me * cw, cw, axis=1)

    def pack_with_conv(parts, conv_part):
        return jnp.concatenate([_pack_small(parts), _rows128(conv_part, CONV_W_SHARD_ROWS)], axis=0)

    d_pk, m_pk, v_pk = _adam_small(
        pack_with_conv({k: given[k] for k, _ in SMALL}, conv_w),
        pack_with_conv(g_small, g_conv),
        pack_with_conv({k: given["m_" + k] for k, _ in SMALL}, m_conv_w),
        pack_with_conv({k: given["v_" + k] for k, _ in SMALL}, v_conv_w),
    )
    for name, pk in (("d", d_pk), ("m", m_pk), ("v", v_pk)):
        parts, at = _unpack_small(pk, shapes)
        parts["conv_w"] = pk[at : at + CONV_W_SHARD_ROWS].reshape(-1)[: 3 * cw].reshape(conv_w.shape)
        for k, a in parts.items():
            out.setdefault(k, {})
            out[k][name] = a
    for k, _ in SMALL:
        out[k]["g"] = g_small[k]
    out["conv_w"]["g"] = g_conv.reshape(conv_w.shape)

    after, started = v_pk, ex.dep()
    for gi in range(len(GROUPS)):
        for k, (o, r) in ex.finish(gi, after).items():
            g, d, nm, nv = _adam_big(big[k], given["m_" + k][0], given["v_" + k][0], o, r, "adam_" + k, dep=started)
            out[k] = tuple(a[None] for a in (g, d, nm, nv))
            after = nv

    order = ["norm_mix", "w_in", "lb_logits", "hg_norm", "rel_bias", "w_out", "norm_ffn", "w_up", "conv_w", "conv_b", "w_down", "norm_ple", "w_ple_gate", "w_ple_proj", "final_norm"]

    def pick(k, what):
        return out[k][what] if isinstance(out[k], dict) else out[k][{"g": 0, "d": 1, "m": 2, "v": 3}[what]]

    return (total_loss, dx[None], *[pick(k, "g") for k in order], *[pick(k, "d") for k in order], *[pick(k, "m") for k in order], *[pick(k, "v") for k in order])
```

```python
import functools

import jax
import jax.numpy as jnp
from jax import lax
from jax.experimental import pallas as pl
from jax.experimental.pallas import tpu as pltpu

F32 = jnp.float32
BF16 = jnp.bfloat16

D_MODEL = 2048
CHUNK = 64
HG_HEADS = 8
HEAD_DIM = 128
HG_WIDTH = HG_HEADS * HEAD_DIM
ATT_HEADS = 8
ATT_WIDTH = ATT_HEADS * HEAD_DIM
LEFT_CHUNKS = 8
PAD = LEFT_CHUNKS * CHUNK
BAND = PAD + CHUNK
REL_CLIP = 128
N_REL = 2 * REL_CLIP + 1
N_REL_PAD = 384
D_FF = 5632
EPS = 1e-6
ATT_SCALE = HEAD_DIM ** -0.5
SUB = 32
HG_BLOCK = 8
Q_BLOCK = 4 * CHUNK
K_BLOCK = Q_BLOCK + PAD
DIAG = 1024
ATT_BLOCK = 2
MASKED = -1e30

ADAM_LR = 0.001
ADAM_B1 = 0.9
ADAM_B2 = 0.999
ADAM_EPS = 1e-08
ADAM_WD = 0.01
ADAM_STEP = 10

N_DEV = 8
VMEM_LIMIT = 48 * 1024 * 1024
MESH = pl.DeviceIdType.MESH
ANY = pl.BlockSpec(memory_space=pl.ANY)
HIGHEST = lax.Precision.HIGHEST

NN = (((1,), (0,)), ((), ()))
NT = (((1,), (1,)), ((), ()))
TN = (((0,), (0,)), ((), ()))


def _params(*sem):
    return pltpu.CompilerParams(dimension_semantics=sem if sem else None, vmem_limit_bytes=VMEM_LIMIT)


def _pallas(body, n_in, dep, **kw):
    deps = [] if dep is None else list(dep)
    if not deps:
        return pl.pallas_call(body, **kw)

    def body_after(*refs):
        body(*refs[:n_in], *refs[n_in + len(deps) :])

    call = pl.pallas_call(body_after, **dict(kw, in_specs=list(kw["in_specs"]) + [ANY] * len(deps)))
    return lambda *ops: call(*ops, *deps)


def _dot(a, b, dims=NN):
    return lax.dot_general(a, b, dims, preferred_element_type=F32)


def _dot3(a, b, dims=NN):
    a_hi, b_hi = a.astype(BF16), b.astype(BF16)
    a_lo, b_lo = (a - a_hi.astype(F32)).astype(BF16), (b - b_hi.astype(F32)).astype(BF16)
    return _dot(a_hi, b_hi, dims) + (_dot(a_hi, b_lo, dims) + _dot(a_lo, b_hi, dims))


def _sigmoid(x):
    return 1.0 / (1.0 + jnp.exp(-x))


def _tile(n, prefs):
    for t in prefs:
        if n % t == 0:
            return t
    return n


def _matmul(a, b, mode, out_dtype, name, tm=1024, tn=1024, tk=None, resid=None, dep=None):
    if mode == "nn":
        (m, k), n = a.shape, b.shape[1]
    elif mode == "nt":
        (m, k), n = a.shape, b.shape[0]
    else:
        (k, m), n = a.shape, b.shape[1]
    tm = _tile(m, (tm, 512, 256, 128))
    tn = _tile(n, (tn, 1408, 512, 256, 128))
    tk = k if tk is None else _tile(k, (tk,))
    nk = k // tk
    dims = {"nn": NN, "nt": NT, "tn": TN}[mode]
    a_spec = pl.BlockSpec((tk, tm), lambda i, j, s: (s, i)) if mode == "tn" else pl.BlockSpec((tm, tk), lambda i, j, s: (i, s))
    b_spec = pl.BlockSpec((tn, tk), lambda i, j, s: (j, s)) if mode == "nt" else pl.BlockSpec((tk, tn), lambda i, j, s: (s, j))
    o_spec = pl.BlockSpec((tm, tn), lambda i, j, s: (i, j))
    has_res = resid is not None

    def body(*refs):
        a_ref, b_ref = refs[0], refs[1]
        o_ref = refs[2 + has_res]
        part = _dot(a_ref[...].astype(BF16), b_ref[...].astype(BF16), dims)

        def finish(acc):
            if has_res:
                acc = acc + refs[2][...]
            o_ref[...] = acc.astype(out_dtype)

        if nk == 1:
            finish(part)
        else:
            acc_ref = refs[-1]
            s = pl.program_id(2)

            @pl.when(s == 0)
            def _():
                acc_ref[...] = part

            @pl.when(s > 0)
            def _():
                acc_ref[...] += part

            @pl.when(s == nk - 1)
            def _():
                finish(acc_ref[...])

    return _pallas(
        body,
        2 + has_res,
        dep,
        name=name,
        grid=(m // tm, n // tn, nk),
        in_specs=[a_spec, b_spec] + ([o_spec] if has_res else []),
        out_specs=o_spec,
        out_shape=jax.ShapeDtypeStruct((m, n), out_dtype),
        scratch_shapes=[pltpu.VMEM((tm, tn), F32)] if nk > 1 else [],
        compiler_params=_params("parallel", "parallel", "arbitrary"),
    )(*([a, b] + ([resid] if has_res else [])))


def _rms_fwd(x, g, name, dep=None):
    t, d = x.shape
    tm = _tile(t, (256,))

    def body(x_ref, g_ref, a_ref, r_ref):
        xv = x_ref[...]
        r = lax.rsqrt(jnp.mean(xv * xv, axis=-1, keepdims=True) + EPS)
        a_ref[...] = (xv * r * g_ref[...]).astype(BF16)
        r_ref[...] = r

    row = pl.BlockSpec((tm, d), lambda i: (i, 0))
    return _pallas(
        body,
        2,
        dep,
        name=name,
        grid=(t // tm,),
        in_specs=[row, pl.BlockSpec((1, d), lambda i: (0, 0))],
        out_specs=[row, pl.BlockSpec((tm, 1), lambda i: (i, 0))],
        out_shape=[jax.ShapeDtypeStruct((t, d), BF16), jax.ShapeDtypeStruct((t, 1), F32)],
        compiler_params=_params("parallel"),
    )(x, g)


def _rms_bwd(da, x, r, g, resid, name, dep=None):
    t, d = x.shape
    tm = _tile(t, (256,))

    def body(da_ref, x_ref, r_ref, g_ref, res_ref, dx_ref, dg_ref):
        i = pl.program_id(0)
        rv = r_ref[...]
        n = x_ref[...] * rv
        dav = da_ref[...]
        dn = dav * g_ref[...]
        dx_ref[...] = rv * (dn - n * jnp.mean(dn * n, axis=-1, keepdims=True)) + res_ref[...]
        part = jnp.sum(dav * n, axis=0, keepdims=True)

        @pl.when(i == 0)
        def _():
            dg_ref[...] = part

        @pl.when(i > 0)
        def _():
            dg_ref[...] += part

    row = pl.BlockSpec((tm, d), lambda i: (i, 0))
    vec = pl.BlockSpec((1, d), lambda i: (0, 0))
    return _pallas(
        body,
        5,
        dep,
        name=name,
        grid=(t // tm,),
        in_specs=[row, row, pl.BlockSpec((tm, 1), lambda i: (i, 0)), vec, row],
        out_specs=[row, vec],
        out_shape=[jax.ShapeDtypeStruct((t, d), F32), jax.ShapeDtypeStruct((1, d), F32)],
        compiler_params=_params("arbitrary"),
    )(da, x, r, g, resid)


def _tri(n, upper):
    r = lax.broadcasted_iota(jnp.int32, (n, n), 0)
    c = lax.broadcasted_iota(jnp.int32, (n, n), 1)
    return jnp.where((c >= r) if upper else (c <= r), 1.0, 0.0).astype(F32)


def _hgrn_gates(q, fp, lbl):
    l0, l1 = lbl[0:1, :], lbl[1:2, :]
    mx = jnp.maximum(l0, l1)
    e0, e1 = jnp.exp(l0 - mx), jnp.exp(l1 - mx)
    lb = e0 / (e0 + e1)
    sig = _sigmoid(fp)
    f = lb + (1.0 - lb) * sig
    kk = (1.0 - lb) * _sigmoid(-fp)
    sq = _sigmoid(q)
    b = jnp.dot(_tri(CHUNK, False), jnp.log(f), precision=HIGHEST, preferred_element_type=F32)
    return lb, sig, f, kk, sq, q * sq, b


def _heads(x):
    return [x[:, j * HEAD_DIM : (j + 1) * HEAD_DIM] for j in range(x.shape[1] // HEAD_DIM)]


def _wide(parts):
    return jnp.concatenate(parts, axis=1)


def _intra_blocks(b):
    out = []
    for lo in range(0, CHUNK, SUB):
        hi = lo + SUB
        br = b[lo + SUB // 2 : lo + SUB // 2 + 1, :]
        row = lax.broadcasted_iota(jnp.int32, (SUB, hi), 0) + lo
        col = lax.broadcasted_iota(jnp.int32, (SUB, hi), 1)
        out.append((lo, hi, jnp.exp(b[lo:hi] - br), jnp.exp(br - b[:hi]), col <= row))
    return out


def _hgrn_fwd(proj, lb_logits, hg_norm):
    t = proj.shape[0]
    nc = t // CHUNK

    def body(q_ref, f_ref, i_ref, g_ref, lbl_ref, hgn_ref, y_ref, o_ref, st_ref, s_scr):
        c = pl.program_id(1)

        @pl.when(c == 0)
        def _():
            s_scr[...] = jnp.zeros_like(s_scr)

        hs = range(HG_BLOCK)
        sts = [s_scr[j] for j in hs]
        _, _, _, kk, _, qf, b = _hgrn_gates(q_ref[...], f_ref[...], lbl_ref[...])
        vb = _heads(i_ref[...].astype(BF16))
        bl = b[CHUNK - 1 : CHUNK, :]
        qe = _heads((qf * jnp.exp(b)).astype(BF16))
        kd = _heads((kk * jnp.exp(bl - b)).astype(BF16))
        decay = _heads(jnp.exp(bl))
        o = [_dot(qe[j], sts[j].astype(BF16), NT) for j in hs]
        parts = [[] for _ in hs]
        for lo, hi, ea, eb, mask in _intra_blocks(b):
            a, bk = _heads((qf[lo:hi] * ea).astype(BF16)), _heads((kk[:hi] * eb).astype(BF16))
            p = [jnp.where(mask, _dot(a[j], bk[j], NT), 0.0).astype(BF16) for j in hs]
            for j in hs:
                parts[j].append(_dot(p[j], vb[j][:hi]))
        o = [o[j] + jnp.concatenate(parts[j], axis=0) for j in hs]
        new = [sts[j] * decay[j] + _dot(vb[j], kd[j], TN) for j in hs]
        hgn = hgn_ref[...]
        on = [o[j] * lax.rsqrt(jnp.mean(o[j] * o[j], axis=-1, keepdims=True) + EPS) * hgn for j in hs]
        gg = g_ref[...]
        for j in hs:
            st_ref[j] = sts[j]
            s_scr[j] = new[j]
        o_ref[...] = _wide(o)
        y_ref[...] = (_wide(on) * (gg * _sigmoid(gg))).astype(BF16)

    wide = HG_BLOCK * HEAD_DIM
    groups = HG_HEADS // HG_BLOCK

    def col(k):
        return pl.BlockSpec((CHUNK, wide), lambda g, c: (c, k * groups + g))

    out = pl.BlockSpec((CHUNK, wide), lambda g, c: (c, g))
    return pl.pallas_call(
        body,
        name="hgrn_fwd",
        grid=(groups, nc),
        in_specs=[col(0), col(1), col(2), col(3), pl.BlockSpec((2, wide), lambda g, c: (0, g)), pl.BlockSpec((1, HEAD_DIM), lambda g, c: (0, 0))],
        out_specs=[out, out, pl.BlockSpec((HG_BLOCK, None, HEAD_DIM, HEAD_DIM), lambda g, c: (g, c, 0, 0))],
        out_shape=[
            jax.ShapeDtypeStruct((t, HG_WIDTH + ATT_WIDTH), BF16),
            jax.ShapeDtypeStruct((t, HG_WIDTH), F32),
            jax.ShapeDtypeStruct((HG_HEADS, nc, HEAD_DIM, HEAD_DIM), F32),
        ],
        scratch_shapes=[pltpu.VMEM((HG_BLOCK, HEAD_DIM, HEAD_DIM), F32)],
        compiler_params=_params("arbitrary", "arbitrary"),
    )(proj, proj, proj, proj, lb_logits, hg_norm)


def _hgrn_bwd(proj, lb_logits, hg_norm, o_hg, dycat, states, dep=None):
    t = proj.shape[0]
    nc = t // CHUNK

    def body(q_ref, f_ref, i_ref, g_ref, lbl_ref, hgn_ref, o_ref, dy_ref, st_ref, dp_ref, dlbl_ref, dhgn_ref, dst_scr, dlb_scr):
        h = pl.program_id(0)
        c = pl.program_id(1)

        @pl.when(c == 0)
        def _():
            dst_scr[...] = jnp.zeros_like(dst_scr)
            dlb_scr[...] = jnp.zeros_like(dlb_scr)

        @pl.when((c == 0) & (h == 0))
        def _():
            dhgn_ref[...] = jnp.zeros_like(dhgn_ref)

        hs = range(HG_BLOCK)
        hgn = _wide([hgn_ref[...]] * HG_BLOCK)
        q, fp, gg, vi = q_ref[...], f_ref[...], g_ref[...], i_ref[...]
        lb, sig, f, kk, sq, qf, b = _hgrn_gates(q, fp, lbl_ref[...])
        o, dy = o_ref[...], dy_ref[...]
        sg = _sigmoid(gg)
        n = _wide([oh * lax.rsqrt(jnp.mean(oh * oh, axis=-1, keepdims=True) + EPS) for oh in _heads(o)])
        don = dy * (gg * sg)
        dgg = dy * (n * hgn) * (sg * (1.0 + gg * (1.0 - sg)))
        d_hgn = sum(_heads(jnp.sum(don * n, axis=0, keepdims=True)))
        dn = don * hgn
        do = _wide(
            [
                lax.rsqrt(jnp.mean(oh * oh, axis=-1, keepdims=True) + EPS) * (dnh - nh * jnp.mean(dnh * nh, axis=-1, keepdims=True))
                for oh, dnh, nh in zip(_heads(o), _heads(dn), _heads(n))
            ]
        )
        sts = [st_ref[j] for j in hs]
        dstn = [dst_scr[j] for j in hs]
        bl = b[CHUNK - 1 : CHUNK, :]
        e_b, e_bl, e_l = jnp.exp(b), jnp.exp(bl - b), jnp.exp(bl)
        doh, vih = _heads(do), _heads(vi)
        dobh = _heads(do.astype(BF16))
        dq_acc = _wide([_dot3(doh[j], sts[j]) for j in hs]) * e_b
        dk_inter = _wide([_dot3(vih[j], dstn[j]) for j in hs]) * e_bl
        dk_acc = dk_inter
        kd = _heads((kk * e_bl).astype(BF16))
        dv_acc = _wide([_dot(kd[j], dstn[j].astype(BF16), NT) for j in hs])
        qe, decay = _heads((qf * e_b).astype(BF16)), _heads(e_l)
        dst_new = [dstn[j] * decay[j] + _dot(dobh[j], qe[j], TN) for j in hs]
        db_last = e_l * _wide([jnp.sum(sts[j] * dstn[j], axis=0, keepdims=True) for j in hs]) + jnp.sum(kk * dk_inter, axis=0, keepdims=True)
        dq_parts = []
        for lo, hi, ea, eb, mask in _intra_blocks(b):
            a, bk = qf[lo:hi] * ea, kk[:hi] * eb
            ah, bkh = _heads(a), _heads(bk)
            abh, bkbh = _heads(a.astype(BF16)), _heads(bk.astype(BF16))
            p = [jnp.where(mask, _dot(abh[j], bkbh[j], NT), 0.0).astype(BF16) for j in hs]
            dp = [jnp.where(mask, _dot3(doh[j][lo:hi], vih[j][:hi], NT), 0.0) for j in hs]
            dq_parts.append(_wide([_dot3(dp[j], bkh[j]) for j in hs]) * ea)
            dki = _wide([_dot3(dp[j], ah[j], TN) for j in hs]) * eb
            dvi = _wide([_dot(p[j], dobh[j][lo:hi], TN) for j in hs])
            if hi < CHUNK:
                zeros = jnp.zeros((CHUNK - hi, HG_BLOCK * HEAD_DIM), F32)
                dki = jnp.concatenate([dki, zeros], axis=0)
                dvi = jnp.concatenate([dvi, zeros], axis=0)
            dk_acc = dk_acc + dki
            dv_acc = dv_acc + dvi
        dq_acc = dq_acc + jnp.concatenate(dq_parts, axis=0)
        rows = lax.broadcasted_iota(jnp.int32, dq_acc.shape, 0)
        db = qf * dq_acc - kk * dk_acc + jnp.where(rows == CHUNK - 1, db_last, 0.0)
        dlf = jnp.dot(_tri(CHUNK, True), db, precision=HIGHEST, preferred_element_type=F32)
        dfk = dlf / f - dk_acc
        for k, part in enumerate((dq_acc * (sq * (1.0 + q * (1.0 - sq))), (1.0 - lb) * dfk * sig * (1.0 - sig), dv_acc, dgg)):
            dp_ref[:, k * HG_WIDTH : (k + 1) * HG_WIDTH] = part.astype(BF16)
        dlb_scr[...] += jnp.sum(dfk * (1.0 - sig), axis=0, keepdims=True)
        dhgn_ref[...] += d_hgn
        for j in hs:
            dst_scr[j] = dst_new[j]

        @pl.when(c == nc - 1)
        def _():
            dl0 = dlb_scr[...] * lb * (1.0 - lb)
            dlbl_ref[0:1, :] = dl0
            dlbl_ref[1:2, :] = -dl0

    wide = HG_BLOCK * HEAD_DIM
    groups = HG_HEADS // HG_BLOCK

    def col(k):
        return pl.BlockSpec((CHUNK, wide), lambda g, c: (nc - 1 - c, k * groups + g))

    blk = pl.BlockSpec((CHUNK, wide), lambda g, c: (nc - 1 - c, g))
    assert groups == 1, "d(q, f, i, g) are written as one contiguous column range of the in_proj gradient"
    return _pallas(
        body,
        9,
        dep,
        name="hgrn_bwd",
        grid=(groups, nc),
        in_specs=[
            col(0), col(1), col(2), col(3),
            pl.BlockSpec((2, wide), lambda g, c: (0, g)),
            pl.BlockSpec((1, HEAD_DIM), lambda g, c: (0, 0)),
            blk, blk,
            pl.BlockSpec((HG_BLOCK, None, HEAD_DIM, HEAD_DIM), lambda g, c: (g, nc - 1 - c, 0, 0)),
        ],
        out_specs=[
            pl.BlockSpec((CHUNK, 4 * HG_WIDTH), lambda g, c: (nc - 1 - c, 0)),
            pl.BlockSpec((2, wide), lambda g, c: (0, g)),
            pl.BlockSpec((1, HEAD_DIM), lambda g, c: (0, 0)),
        ],
        out_shape=[
            jax.ShapeDtypeStruct((t, 4 * HG_WIDTH + 3 * ATT_WIDTH), BF16),
            jax.ShapeDtypeStruct((2, HG_WIDTH), F32),
            jax.ShapeDtypeStruct((1, HEAD_DIM), F32),
        ],
        scratch_shapes=[pltpu.VMEM((HG_BLOCK, HEAD_DIM, HEAD_DIM), F32), pltpu.VMEM((1, wide), F32)],
        compiler_params=_params("arbitrary", "arbitrary"),
    )(proj, proj, proj, proj, lb_logits, hg_norm, o_hg, dycat, states)


def _diagonal_slots(shift):
    i = lax.broadcasted_iota(jnp.int32, (N_REL_PAD, DIAG), 0)
    u = lax.broadcasted_iota(jnp.int32, (N_REL_PAD, DIAG), 1)
    offset = u - shift if shift else jnp.where(u < K_BLOCK, u, u - DIAG)
    return jnp.where(jnp.clip(PAD - offset, -REL_CLIP, REL_CLIP) + REL_CLIP == i, 1.0, 0.0).astype(BF16)


def _split3(x):
    hi = x.astype(BF16)
    mid = (x - hi.astype(F32)).astype(BF16)
    return hi, mid, (x - hi.astype(F32) - mid.astype(F32)).astype(BF16)


def _bias_table(rel_bias, dep=None):
    def body(rb_ref, o_ref, diag):
        h = pl.program_id(0)

        @pl.when(h == 0)
        def _():
            hi, mid, lo = _split3(rb_ref[...])
            slots = _diagonal_slots(0)
            diag[...] = _dot(hi, slots) + (_dot(mid, slots) + _dot(lo, slots))

        rows = jnp.broadcast_to(diag[pl.ds(h, 1), :], (Q_BLOCK, DIAG))
        row = lax.broadcasted_iota(jnp.int32, (Q_BLOCK, K_BLOCK), 0)
        col = lax.broadcasted_iota(jnp.int32, (Q_BLOCK, K_BLOCK), 1)
        first = row - (row & (CHUNK - 1))
        seen = (col >= first) & (col < first + BAND)
        o_ref[...] = jnp.where(seen, pltpu.roll(rows, 0, 1, stride=1, stride_axis=0)[:, :K_BLOCK], MASKED)

    return _pallas(
        body,
        1,
        dep,
        name="bias_table",
        grid=(ATT_HEADS,),
        in_specs=[pl.BlockSpec((ATT_HEADS, N_REL_PAD), lambda h: (0, 0))],
        out_specs=pl.BlockSpec((None, Q_BLOCK, K_BLOCK), lambda h: (h, 0, 0)),
        out_shape=jax.ShapeDtypeStruct((ATT_HEADS, Q_BLOCK, K_BLOCK), F32),
        scratch_shapes=[pltpu.VMEM((ATT_HEADS, DIAG), F32)],
        compiler_params=_params("arbitrary"),
    )(rel_bias)


def _att_probs(q_ref, kpad, bias_ref, blk, cols=slice(None), head=None):
    qs = (q_ref[:, cols] * ATT_SCALE).astype(BF16)
    start = pl.multiple_of(blk * Q_BLOCK, Q_BLOCK)
    kb = kpad[pl.ds(start, K_BLOCK), cols]
    s = _dot(qs, kb, NT) + (bias_ref[...] if head is None else bias_ref[head])
    col = lax.broadcasted_iota(jnp.int32, (Q_BLOCK, K_BLOCK), 1)
    s = jnp.where(col >= PAD - blk * Q_BLOCK, s, MASKED)
    e = jnp.exp(s - jnp.max(s, axis=-1, keepdims=True))
    return qs, kb, start, e * (1.0 / jnp.sum(e, axis=-1, keepdims=True))


def _fill_padded(dst, src):
    dst[0:PAD, :] = jnp.zeros((PAD, dst.shape[1]), BF16)
    dst[PAD:, :] = src[...].astype(BF16)


def _att_fwd(proj, bias, y_mix, dep=None):
    t = proj.shape[0]
    nb = t // Q_BLOCK

    def body(q_ref, k_ref, v_ref, bias_ref, y_in_ref, y_ref, kpad, vpad):
        c = pl.program_id(1)

        @pl.when(c == 0)
        def _():
            _fill_padded(kpad, k_ref)
            _fill_padded(vpad, v_ref)

        for j in range(ATT_BLOCK):
            cols = slice(j * HEAD_DIM, (j + 1) * HEAD_DIM)
            _, _, start, p = _att_probs(q_ref, kpad, bias_ref, c, cols, j)
            y_ref[:, cols] = _dot(p.astype(BF16), vpad[pl.ds(start, K_BLOCK), cols]).astype(BF16)

    wide = ATT_BLOCK * HEAD_DIM
    base = 4 * HG_HEADS // ATT_BLOCK
    groups = ATT_HEADS // ATT_BLOCK
    return _pallas(
        body,
        5,
        dep,
        name="att_fwd",
        grid=(groups, nb),
        in_specs=[
            pl.BlockSpec((Q_BLOCK, wide), lambda g, c: (c, base + g)),
            pl.BlockSpec((t, wide), lambda g, c: (0, base + groups + g)),
            pl.BlockSpec((t, wide), lambda g, c: (0, base + 2 * groups + g)),
            pl.BlockSpec((ATT_BLOCK, Q_BLOCK, K_BLOCK), lambda g, c: (g, 0, 0)),
            ANY,
        ],
        out_specs=pl.BlockSpec((Q_BLOCK, wide), lambda g, c: (c, HG_HEADS // ATT_BLOCK + g)),
        out_shape=jax.ShapeDtypeStruct(y_mix.shape, BF16),
        input_output_aliases={4: 0},
        scratch_shapes=[pltpu.VMEM((t + PAD, wide), BF16), pltpu.VMEM((t + PAD, wide), BF16)],
        compiler_params=_params("arbitrary", "arbitrary"),
    )(proj, proj, proj, bias, y_mix)


def _att_bwd(proj, bias, dycat, dproj, dep=None):
    t = proj.shape[0]
    nb = t // Q_BLOCK

    def body(q_ref, k_ref, v_ref, bias_ref, dy_ref, dp_in_ref, dq_ref, dk_ref, dv_ref, g_ref, kpad, vpad, dkacc, dvacc):
        c = pl.program_id(1)

        @pl.when(c == 0)
        def _():
            _fill_padded(kpad, k_ref)
            _fill_padded(vpad, v_ref)
            dkacc[...] = jnp.zeros_like(dkacc)
            dvacc[...] = jnp.zeros_like(dvacc)
            g_ref[...] = jnp.zeros_like(g_ref)

        qs, kb, start, p = _att_probs(q_ref, kpad, bias_ref, c)
        band = pl.ds(start, K_BLOCK)
        dyb = dy_ref[...].astype(BF16)
        dvacc[band, :] += _dot(p.astype(BF16), dyb, TN)
        dp = _dot(dyb, vpad[band, :], NT)
        ds = p * (dp - jnp.sum(dp * p, axis=-1, keepdims=True))
        g_ref[...] += ds
        dsb = ds.astype(BF16)
        dq_ref[...] = (_dot(dsb, kb) * ATT_SCALE).astype(BF16)
        dkacc[band, :] += _dot(dsb, qs, TN)

        @pl.when(c == nb - 1)
        def _():
            dk_ref[...] = dkacc[PAD:, :].astype(BF16)
            dv_ref[...] = dvacc[PAD:, :].astype(BF16)

    base = 4 * HG_HEADS
    whole = pl.BlockSpec((t, HEAD_DIM), lambda h, c: (0, h))
    return _pallas(
        body,
        6,
        dep,
        name="att_bwd",
        grid=(ATT_HEADS, nb),
        in_specs=[
            pl.BlockSpec((Q_BLOCK, HEAD_DIM), lambda h, c: (c, base + h)),
            pl.BlockSpec((t, HEAD_DIM), lambda h, c: (0, base + ATT_HEADS + h)),
            pl.BlockSpec((t, HEAD_DIM), lambda h, c: (0, base + 2 * ATT_HEADS + h)),
            pl.BlockSpec((None, Q_BLOCK, K_BLOCK), lambda h, c: (h, 0, 0)),
            pl.BlockSpec((Q_BLOCK, HEAD_DIM), lambda h, c: (c, HG_HEADS + h)),
            ANY,
        ],
        out_specs=[pl.BlockSpec((Q_BLOCK, HEAD_DIM), lambda h, c: (c, base + h)), whole, whole, pl.BlockSpec((None, Q_BLOCK, K_BLOCK), lambda h, c: (h, 0, 0))],
        input_output_aliases={5: 0},
        out_shape=[
            jax.ShapeDtypeStruct(dproj.shape, BF16),
            jax.ShapeDtypeStruct((t, ATT_WIDTH), BF16),
            jax.ShapeDtypeStruct((t, ATT_WIDTH), BF16),
            jax.ShapeDtypeStruct((ATT_HEADS, Q_BLOCK, K_BLOCK), F32),
        ],
        scratch_shapes=[
            pltpu.VMEM((t + PAD, HEAD_DIM), BF16),
            pltpu.VMEM((t + PAD, HEAD_DIM), BF16),
            pltpu.VMEM((t + PAD, HEAD_DIM), F32),
            pltpu.VMEM((t + PAD, HEAD_DIM), F32),
        ],
        compiler_params=_params("arbitrary", "arbitrary"),
    )(proj, proj, proj, bias, dycat, dproj)


def _rel_bias_grad(gsum):
    def body(g_ref, o_ref):
        r = lax.broadcasted_iota(jnp.int32, (Q_BLOCK, Q_BLOCK), 0)
        c = lax.broadcasted_iota(jnp.int32, (Q_BLOCK, Q_BLOCK), 1)
        flip = jnp.where(r + c == Q_BLOCK - 1, 1.0, 0.0).astype(BF16)
        sums = []
        for h in range(ATT_HEADS):
            hi, mid, lo = _split3(g_ref[h])
            rev = _dot(flip, hi) + (_dot(flip, mid) + _dot(flip, lo))
            wide = jnp.concatenate([rev, jnp.zeros((Q_BLOCK, DIAG - K_BLOCK), F32)], axis=1)
            sums.append(jnp.sum(pltpu.roll(wide, 0, 1, stride=1, stride_axis=0), axis=0, keepdims=True))
        hi, mid, lo = _split3(jnp.concatenate(sums, axis=0))
        slots = _diagonal_slots(Q_BLOCK - 1)
        o_ref[...] = _dot(hi, slots, NT) + (_dot(mid, slots, NT) + _dot(lo, slots, NT))

    return pl.pallas_call(
        body,
        name="rel_bias_grad",
        out_shape=jax.ShapeDtypeStruct((ATT_HEADS, N_REL_PAD), F32),
        compiler_params=_params(),
    )(gsum)


HALO = 16


FF_TILE = 1408
FF_TILES = D_FF // FF_TILE


def _interleave_cols(a):
    lead = a.shape[:-1]
    return jnp.swapaxes(a.reshape(*lead, 2, FF_TILES, FF_TILE), -3, -2).reshape(*lead, 2 * D_FF)


def _deinterleave_cols(a):
    lead = a.shape[:-1]
    return jnp.swapaxes(a.reshape(*lead, FF_TILES, 2, FF_TILE), -3, -2).reshape(*lead, 2 * D_FF)


def _ffn_specs(t, tm):
    wide = 2 * FF_TILE
    tile = pl.BlockSpec((tm, wide), lambda j, i: (i, j))
    before = pl.BlockSpec((HALO, wide), lambda j, i: (jnp.maximum(i * (tm // HALO) - 1, 0), j))
    after = pl.BlockSpec((HALO, wide), lambda j, i: (jnp.minimum((i + 1) * (tm // HALO), t // HALO - 1), j))
    vec = lambda rows: pl.BlockSpec((rows, wide), lambda j, i: (0, j))
    return tile, before, after, vec


def _shifted(x, rows, offsets):
    r = lax.broadcasted_iota(jnp.int32, (rows, x.shape[0]), 0)
    c = lax.broadcasted_iota(jnp.int32, (rows, x.shape[0]), 1)
    pick = jnp.concatenate([jnp.where(c == r + o, 1.0, 0.0).astype(BF16) for o in offsets], axis=0)
    out = _dot(pick, x)
    return [out[k * rows : (k + 1) * rows] for k in range(len(offsets))]


def _conv(x, w, b, rows):
    taps = _shifted(x, rows, [HALO - 2, HALO - 1]) + [x[HALO : HALO + rows].astype(F32)]
    return b + w[0:1] * taps[0] + w[1:2] * taps[1] + w[2:3] * taps[2], taps


def _ffn_act_fwd(u, conv_w, conv_b):
    t = u.shape[0]
    tm = _tile(t, (128,))
    tile, before, _, vec = _ffn_specs(t, tm)

    def body(u_ref, h_ref, w_ref, b_ref, z_ref):
        first = pl.program_id(1) == 0
        halo = h_ref[...]
        x = jnp.concatenate([jnp.where(first, jnp.zeros_like(halo), halo), u_ref[...]], axis=0)
        c, _ = _conv(x, w_ref[...], b_ref[...], tm)
        gate, val = c[:, :FF_TILE], c[:, FF_TILE:]
        z_ref[...] = (gate * _sigmoid(gate) * val).astype(BF16)

    return pl.pallas_call(
        body,
        name="ffn_act_fwd",
        grid=(FF_TILES, t // tm),
        in_specs=[tile, before, vec(3), vec(1)],
        out_specs=pl.BlockSpec((tm, FF_TILE), lambda j, i: (i, j)),
        out_shape=jax.ShapeDtypeStruct((t, D_FF), BF16),
        compiler_params=_params("parallel", "parallel"),
    )(u, u, conv_w, conv_b)


def _ffn_act_bwd(u, dz, conv_w, conv_b, dep=None):
    t = u.shape[0]
    tm = _tile(t, (128,))
    nt = t // tm
    ext = tm + HALO
    tile, before, after, vec = _ffn_specs(t, tm)

    def body(u_ref, ub_ref, ua_ref, w_ref, b_ref, dz_ref, dza_ref, du_ref, dw_ref, db_ref):
        i = pl.program_id(1)
        first, last = i == 0, i == nt - 1
        ub, ua = ub_ref[...], ua_ref[...]
        parts = [jnp.where(first, jnp.zeros_like(ub), ub), u_ref[...], jnp.where(last, jnp.zeros_like(ua), ua)]
        w = w_ref[...]
        c, taps = _conv(jnp.concatenate(parts, axis=0), w, b_ref[...], ext)
        gate, val = c[:, :FF_TILE], c[:, FF_TILE:]
        dz = jnp.concatenate([dz_ref[...].astype(F32), jnp.where(last, 0.0, dza_ref[...].astype(F32))], axis=0)
        sg = _sigmoid(gate)
        d = jnp.concatenate([dz * val * (sg * (1.0 + gate * (1.0 - sg))), dz * (gate * sg)], axis=1)
        d1, d2 = _shifted(d.astype(BF16), tm, [1, 2])
        du_ref[...] = (w[2:3] * d[:tm] + w[1:2] * d1 + w[0:1] * d2).astype(BF16)

        @pl.when(first)
        def _():
            dw_ref[...] = jnp.zeros_like(dw_ref)
            db_ref[...] = jnp.zeros_like(db_ref)

        for k, tap in enumerate(taps):
            dw_ref[k : k + 1, :] += jnp.sum(d[:tm] * tap[:tm], axis=0, keepdims=True)
        db_ref[...] += jnp.sum(d[:tm], axis=0, keepdims=True)

    narrow = lambda rows, index: pl.BlockSpec((rows, FF_TILE), index)
    return _pallas(
        body,
        7,
        dep,
        name="ffn_act_bwd",
        grid=(FF_TILES, nt),
        in_specs=[
            tile, before, after, vec(3), vec(1),
            narrow(tm, lambda j, i: (i, j)),
            narrow(HALO, lambda j, i: (jnp.minimum((i + 1) * (tm // HALO), t // HALO - 1), j)),
        ],
        out_specs=[tile, vec(3), vec(1)],
        out_shape=[
            jax.ShapeDtypeStruct((t, 2 * D_FF), BF16),
            jax.ShapeDtypeStruct((3, 2 * D_FF), F32),
            jax.ShapeDtypeStruct((1, 2 * D_FF), F32),
        ],
        compiler_params=_params("parallel", "arbitrary"),
    )(u, u, u, conv_w, conv_b, dz, dz)


def _ple_loss(gpre, pp, h2, final_norm, target):
    t, d = h2.shape
    tm = _tile(t, (256,))

    def body(gp_ref, pp_ref, h_ref, g_ref, tg_ref, dh_ref, dgp_ref, dpp_ref, dg_ref, loss_ref):
        i = pl.program_id(0)
        gate = _sigmoid(gp_ref[...])
        ppv = pp_ref[...]
        h3 = h_ref[...] + gate * ppv
        r = lax.rsqrt(jnp.mean(h3 * h3, axis=-1, keepdims=True) + EPS)
        n = h3 * r
        g = g_ref[...]
        err = n * g - tg_ref[...]
        loss = 0.5 * jnp.sum(jnp.mean(err * err, axis=-1, keepdims=True))
        dy = err * (1.0 / d)
        dn = dy * g
        dh = r * (dn - n * jnp.mean(dn * n, axis=-1, keepdims=True))
        dh_ref[...] = dh
        dgp_ref[...] = (dh * ppv * gate * (1.0 - gate)).astype(BF16)
        dpp_ref[...] = (dh * gate).astype(BF16)
        dg = jnp.sum(dy * n, axis=0, keepdims=True)

        @pl.when(i == 0)
        def _():
            dg_ref[...] = dg
            loss_ref[...] = jnp.full(loss_ref.shape, loss, F32)

        @pl.when(i > 0)
        def _():
            dg_ref[...] += dg
            loss_ref[...] += loss

    row = pl.BlockSpec((tm, d), lambda i: (i, 0))
    vec = pl.BlockSpec((1, d), lambda i: (0, 0))
    return pl.pallas_call(
        body,
        name="ple_loss",
        grid=(t // tm,),
        in_specs=[row, row, row, vec, row],
        out_specs=[row, row, row, vec, pl.BlockSpec((8, 128), lambda i: (0, 0))],
        out_shape=[
            jax.ShapeDtypeStruct((t, d), F32),
            jax.ShapeDtypeStruct((t, d), BF16),
            jax.ShapeDtypeStruct((t, d), BF16),
            jax.ShapeDtypeStruct((1, d), F32),
            jax.ShapeDtypeStruct((8, 128), F32),
        ],
        compiler_params=_params("arbitrary"),
    )(gpre, pp, h2, final_norm, target)


def _adamw(w, g, m, v):
    m = ADAM_B1 * m + (1.0 - ADAM_B1) * g
    v = ADAM_B2 * v + (1.0 - ADAM_B2) * (g * g)
    m_hat = m / (1.0 - ADAM_B1 ** ADAM_STEP)
    v_hat = v / (1.0 - ADAM_B2 ** ADAM_STEP)
    return -ADAM_LR * (m_hat / (jnp.sqrt(v_hat) + ADAM_EPS) + ADAM_WD * w), m, v


def _adam_big(w, m, v, own, recv, name, dep=None):
    r, c = w.shape
    tr = _tile(r, (256, 176))

    def body(w_ref, m_ref, v_ref, own_ref, recv_ref, g_ref, d_ref, nm_ref, nv_ref):
        g = own_ref[...]
        for k in range(3):
            g = g + recv_ref[k].astype(F32)
        g_ref[...] = g
        d_ref[...], nm_ref[...], nv_ref[...] = _adamw(w_ref[...], g, m_ref[...], v_ref[...])

    blk = pl.BlockSpec((tr, c), lambda i: (i, 0))
    return _pallas(
        body,
        5,
        dep,
        name=name,
        grid=(r // tr,),
        in_specs=[blk, blk, blk, blk, pl.BlockSpec((3, tr, c), lambda i: (0, i, 0))],
        out_specs=[blk] * 4,
        out_shape=[jax.ShapeDtypeStruct((r, c), F32)] * 4,
        compiler_params=_params("parallel"),
    )(w, m, v, own, recv)


def _adam_small(w, g, m, v):
    def body(w_ref, g_ref, m_ref, v_ref, d_ref, nm_ref, nv_ref):
        d_ref[...], nm_ref[...], nv_ref[...] = _adamw(w_ref[...], g_ref[...], m_ref[...], v_ref[...])

    return pl.pallas_call(body, name="adam_small", out_shape=[jax.ShapeDtypeStruct(w.shape, F32)] * 3, compiler_params=_params())(w, g, m, v)


def _position():
    return lax.axis_index("x"), lax.axis_index("y"), lax.axis_index("c")


def _other_chips(x, y):
    return [(1 - x, y), (x, 1 - y), (1 - x, 1 - y)]


def _block_index(dev, interleaved):
    x, y, c = dev
    return 4 * y + 2 * c + x if interleaved else 4 * x + 2 * y + c


def _shard_of(ref, axis, size, dev, interleaved=False):
    start = pl.multiple_of(_block_index(dev, interleaved) * size, 128 if axis == 1 else 16)
    return ref.at[:, pl.ds(start, size)] if axis == 1 else ref.at[pl.ds(start, size), :]


def _add_blocks(ids, grad, landed, axis, size, targets, out_dtype, name):
    rows = size if axis == 0 else grad.shape[0]
    cols = size if axis == 1 else grad.shape[1]
    tr = _tile(rows, (256, 176))
    nr = rows // tr
    nt = len(targets)

    def body(ids_ref, g_ref, l_ref, o_ref):
        o_ref[...] = (g_ref[...] + l_ref[...]).astype(out_dtype)

    if axis == 1:
        g_spec = pl.BlockSpec((tr, cols), lambda k, i, ids: (i, ids[targets[0] + k]))
    else:
        g_spec = pl.BlockSpec((tr, cols), lambda k, i, ids: (ids[targets[0] + k] * nr + i, 0))
    return pl.pallas_call(
        body,
        name=name,
        grid_spec=pltpu.PrefetchScalarGridSpec(
            num_scalar_prefetch=1,
            grid=(nt, nr),
            in_specs=[g_spec, pl.BlockSpec((None, tr, cols), lambda k, i, ids: (ids[4 + targets[0] + k], i, 0))],
            out_specs=pl.BlockSpec((None, tr, cols), lambda k, i, ids: (k, i, 0)),
        ),
        out_shape=jax.ShapeDtypeStruct((nt, rows, cols), out_dtype),
        compiler_params=_params("parallel", "parallel"),
    )(ids, grad, landed)


def _all_reduce_small(vec, name):
    rows = vec.shape[0]

    def body(v_ref, o_ref, land, send_sems, recv_sems):
        x, y, c = _position()
        mine = 4 * x + 2 * y + c
        copies = []
        for mask in range(1, N_DEV):
            peer = (1 - x if mask & 4 else x, 1 - y if mask & 2 else y, 1 - c if mask & 1 else c)
            copies.append(
                pltpu.make_async_remote_copy(
                    src_ref=v_ref, dst_ref=land.at[mine], send_sem=send_sems.at[mask - 1], recv_sem=recv_sems.at[mask - 1], device_id=peer, device_id_type=MESH
                )
            )
        for cp in copies:
            cp.start()
        land[mine] = v_ref[...]
        for cp in copies:
            cp.wait()
        acc = land[0]
        for k in range(1, N_DEV):
            acc = acc + land[k]
        o_ref[...] = acc

    return pl.pallas_call(
        body,
        name=name,
        out_shape=jax.ShapeDtypeStruct(vec.shape, F32),
        in_specs=[pl.BlockSpec(memory_space=pltpu.VMEM)],
        out_specs=pl.BlockSpec(memory_space=pltpu.VMEM),
        scratch_shapes=[pltpu.VMEM((N_DEV, rows, 128), F32), pltpu.SemaphoreType.DMA((N_DEV - 1,)), pltpu.SemaphoreType.DMA((N_DEV - 1,))],
    )(vec)


def _rows128(a, rows):
    flat = a.reshape(-1)
    return jnp.pad(flat, (0, rows * 128 - flat.shape[0])).reshape(rows, 128)


def _pad_rel(a):
    return jnp.pad(a.reshape(ATT_HEADS, -1)[:, :N_REL], ((0, 0), (0, N_REL_PAD - N_REL)))


SMALL = [("norm_mix", 16), ("lb_logits", 16), ("hg_norm", 8), ("rel_bias", 24), ("norm_ffn", 16), ("conv_b", 88), ("norm_ple", 16), ("final_norm", 16)]
CONV_W_FULL_ROWS = 3 * 2 * D_FF // 128
CONV_W_SHARD_ROWS = 40


def _pack_small(parts):
    return jnp.concatenate([_rows128(_pad_rel(parts[k]) if k == "rel_bias" else parts[k], rows) for k, rows in SMALL], axis=0)


def _unpack_small(packed, shapes):
    out, at = {}, 0
    for k, rows in SMALL:
        blk = packed[at : at + rows]
        at += rows
        if k == "rel_bias":
            out[k] = blk.reshape(ATT_HEADS, N_REL_PAD)[:, :N_REL].reshape(shapes[k])
        else:
            n = 1
            for s in shapes[k]:
                n *= s
            out[k] = blk.reshape(-1)[:n].reshape(shapes[k])
    return out, at


BIG = [("w_in", 1), ("w_out", 0), ("w_up", 1), ("w_down", 0), ("w_ple_gate", 0), ("w_ple_proj", 1)]


HBM = pl.BlockSpec(memory_space=pltpu.HBM)
SEM = pl.BlockSpec(memory_space=pltpu.SEMAPHORE)
EFFECT = pltpu.SideEffectType.DATAFLOW_SIDE_EFFECTING


def _copies(plan, refs, send_sems, recv_sems):
    return [
        pltpu.make_async_remote_copy(src_ref=src, dst_ref=dst, send_sem=send_sems.at[i], recv_sem=recv_sems.at[i], device_id=dev, device_id_type=MESH)
        for i, (src, dst, dev) in enumerate(plan(refs))
    ]


def _split_start(name, arrays, plan, n):
    k = len(arrays)

    def body(*refs):
        for cp in _copies(plan, refs[:k], refs[k], refs[k + 1]):
            cp.start()
        refs[-1][...] = jnp.zeros_like(refs[-1])

    out = pl.pallas_call(
        body,
        name=name,
        out_shape=(pltpu.SemaphoreType.DMA((n,)), pltpu.SemaphoreType.DMA((n,)), *[pltpu.HBM(a.shape, a.dtype) for a in arrays], jax.ShapeDtypeStruct((8, 128), F32)),
        in_specs=[HBM] * k,
        out_specs=(SEM, SEM, *[HBM] * k, pl.BlockSpec(memory_space=pltpu.VMEM)),
        input_output_aliases={i: 2 + i for i in range(k)},
        compiler_params=pltpu.CompilerParams(has_side_effects=EFFECT),
    )(*[pltpu.with_memory_space_constraint(a, pltpu.HBM) for a in arrays])
    return out[0], out[1], list(out[2 : 2 + k]), out[-1]


def _split_wait(name, send, recv, arrays, plan, after):
    k = len(arrays)

    def body(*refs):
        for cp in _copies(plan, refs[:k], refs[k], refs[k + 1]):
            cp.wait_send()
            cp.wait_recv()

    out = pl.pallas_call(
        body,
        name=name,
        out_shape=tuple(pltpu.HBM(a.shape, a.dtype) for a in arrays),
        in_specs=[HBM] * k + [SEM, SEM, ANY],
        out_specs=tuple([HBM] * k),
        input_output_aliases={i: i for i in range(k)},
        compiler_params=pltpu.CompilerParams(has_side_effects=EFFECT),
    )(*arrays, send, recv, after)
    return list(out)


def _cast_into(w, me, axis, name, dep, dtype):
    r, c = w.shape
    tr = _tile(r, (256, 176))
    nr = r // tr
    deps = [] if dep is None else [dep]

    def body(me_ref, w_ref, *rest):
        rest[-1][...] = w_ref[...].astype(dtype)

    if axis == 1:
        shape, o_spec = (r, N_DEV * c), pl.BlockSpec((tr, c), lambda i, me: (i, me[0]))
    else:
        shape, o_spec = (N_DEV * r, c), pl.BlockSpec((tr, c), lambda i, me: (me[0] * nr + i, 0))
    return pl.pallas_call(
        body,
        name=name,
        grid_spec=pltpu.PrefetchScalarGridSpec(
            num_scalar_prefetch=1, grid=(nr,), in_specs=[pl.BlockSpec((tr, c), lambda i, me: (i, 0))] + [ANY] * len(deps), out_specs=o_spec
        ),
        out_shape=jax.ShapeDtypeStruct(shape, dtype),
        compiler_params=_params("parallel"),
    )(me, w, *deps)


GATHER = [
    (["w_in"], None, "norm_mix_fwd", "bias_table"),
    (["w_out"], "norm_mix_fwd", "att_fwd", None),
    (["w_up", "conv_w"], "norm_mix_fwd", "att_fwd", "norm_ffn_fwd"),
    (["w_down", "w_ple_gate", "w_ple_proj"], "att_fwd", "up_proj", "ffn_act_fwd"),
]
GROUPS = [["w_ple_proj", "w_ple_gate", "w_down"], ["w_up"], ["w_out"], ["w_in"]]
STAGES = ["ffn_act_bwd", "d_mix_out", "hgrn_bwd", "d_norm_mix_out"]
INTERLEAVED = {"w_up", "conv_w"}


class _Exchange:
    def __init__(self, big, conv_w, position):
        self.big, self.axis = big, dict(BIG, conv_w=1)
        self.shards = dict(big, conv_w=conv_w)
        self.size = {k: w.shape[self.axis[k]] for k, w in self.shards.items()}
        self.x, self.y, self.c = position
        chips = [(self.x, self.y)] + _other_chips(self.x, self.y)
        landed = [2 * cx + cy for cx, cy in chips]
        self.ids = {
            flag: jnp.stack([_block_index((cx, cy, self.c), flag) for cx, cy in chips] + landed).astype(jnp.int32) for flag in (False, True)
        }
        self.tokens, self.grads, self.state, self.wfull = [], {}, {}, {}


    def _slot(self, ref, k, dev):
        return _shard_of(ref, self.axis[k], self.size[k], dev, interleaved=k in INTERLEAVED)

    def _plan_gather(self, names, direct, refs):
        x, y, c = _position()
        me, out = (x, y, c), []
        for k, ref in zip(names, refs):
            mine = self._slot(ref, k, me)
            out.append((mine, mine, (x, y, 1 - c)))
            out += [(mine, mine, (*chip, c)) for chip in _other_chips(x, y)]
            if direct:
                out += [(mine, mine, (*chip, 1 - c)) for chip in _other_chips(x, y)]
        return out

    def _plan_forward(self, names, refs):
        x, y, c = _position()
        out = []
        for k, ref in zip(names, refs):
            for chip in _other_chips(x, y):
                block = self._slot(ref, k, (*chip, c))
                out.append((block, block, (x, y, 1 - c)))
        return out

    def _plan_sibling(self, names, refs):
        x, y, c = _position()
        n = len(names)
        return [(self._slot(refs[i], k, (p // 2, p % 2, 1 - c)), refs[n + i].at[p], (x, y, 1 - c)) for i, k in enumerate(names) for p in range(4)]

    def _plan_chips(self, names, refs):
        x, y, c = _position()
        n = len(names)
        return [(refs[i].at[j], refs[n + i].at[j], (*chip, c)) for i in range(n) for j, chip in enumerate(_other_chips(x, y))]


    def gather(self):
        me = {flag: _block_index((self.x, self.y, self.c), flag).astype(jnp.int32).reshape(1) for flag in (False, True)}
        self.late, self.unsent = {}, {}
        after = None
        for gi, (names, issued, *_) in enumerate(GATHER):
            self.unsent[gi] = [
                _cast_into(self.shards[k], me[k in INTERLEAVED], self.axis[k], "cast_" + k, after, F32 if k == "conv_w" else BF16) for k in names
            ]
            if issued is None:
                self._issue(None)
                after = self.tokens[-1]
        self.tokens += [a for arrays in self.unsent.values() for a in arrays]

    def _issue(self, stage):
        for gi, (names, issued, _, forwarded) in enumerate(GATHER):
            if issued == stage and gi in self.unsent:
                plan = functools.partial(self._plan_gather, names, forwarded is None)
                copies = (7 if forwarded is None else 4) * len(names)
                send, recv, fulls, token = _split_start(f"gather_start_{gi}", self.unsent.pop(gi), plan, copies)
                self.tokens.append(token)
                self.late[gi] = (send, recv, fulls, plan)

    def weight(self, k):
        return self.wfull[k]

    def dep(self):
        tokens, self.tokens = self.tokens, []
        return tokens

    def reduce(self, vec, name):
        return _all_reduce_small(vec, name)

    def grad(self, k, g):
        self.grads[k] = g
        for gi, names in enumerate(GROUPS):
            if k == names[-1]:
                plan = functools.partial(self._plan_sibling, names)
                lands = [lax.empty((4, *self._shard_shape(n)), F32) for n in names]
                send, recv, arrays, token = _split_start(f"sibling_start_{gi}", [self.grads[n] for n in names] + lands, plan, 4 * len(names))
                self.tokens.append(token)
                self.state[gi] = (send, recv, arrays, plan)

    def done(self, stage, after):
        for gi, (names, _, _, forwarded) in enumerate(GATHER):
            if forwarded == stage:
                send, recv, fulls, plan = self.late[gi]
                self.wfull.update(zip(names, _split_wait(f"forward_wait_{gi}", send, recv, fulls, plan, after)))
        for gi, (names, _, arrived, forwarded) in enumerate(GATHER):
            if arrived == stage:
                send, recv, fulls, plan = self.late[gi]
                fulls = _split_wait(f"gather_wait_{gi}", send, recv, fulls, plan, after)
                if forwarded is None:
                    self.wfull.update(zip(names, fulls))
                else:
                    plan = functools.partial(self._plan_forward, names)
                    send, recv, fulls, token = _split_start(f"forward_start_{gi}", fulls, plan, 3 * len(names))
                    self.tokens.append(token)
                    self.late[gi] = (send, recv, fulls, plan)
        self._issue(stage)
        if stage in STAGES:
            self._to_chips(STAGES.index(stage), after)

    def _shard_shape(self, k):
        shape = list(self.grads[k].shape)
        shape[self.axis[k]] = self.size[k]
        return tuple(shape)

    def _to_chips(self, gi, after):
        names = GROUPS[gi]
        n = len(names)
        send, recv, arrays, plan = self.state[gi]
        arrays = _split_wait(f"sibling_wait_{gi}", send, recv, arrays, plan, after)
        own, parts = [], []
        for k, g, land in zip(names, arrays[:n], arrays[n:]):
            ids = self.ids[k in INTERLEAVED]
            own.append(_add_blocks(ids, g, land, self.axis[k], self.size[k], [0], F32, "add_own_" + k)[0])
            parts.append(_add_blocks(ids, g, land, self.axis[k], self.size[k], [1, 2, 3], BF16, "add_send_" + k))
        plan = functools.partial(self._plan_chips, names)
        lands = [lax.empty(part.shape, BF16) for part in parts]
        send, recv, arrays, token = _split_start(f"chips_start_{gi}", parts + lands, plan, 3 * n)
        self.tokens.append(token)
        self.state[gi] = (send, recv, arrays, plan, own)

    def finish(self, gi, after):
        names = GROUPS[gi]
        send, recv, arrays, plan, own = self.state[gi]
        arrays = _split_wait(f"chips_wait_{gi}", send, recv, arrays, plan, after)
        return {k: (o, r) for k, o, r in zip(names, own, arrays[len(names) :])}


class _Resident:
    def __init__(self, wfull):
        self.wfull, self.grads = wfull, {}

    def weight(self, k):
        return self.wfull[k]

    def grad(self, k, g):
        self.grads[k] = g

    def dep(self):
        return None

    def reduce(self, vec, name):
        return vec

    def done(self, stage, after):
        pass


def _local_step(x, p, target, small, ex):
    bias = _bias_table(jnp.pad(small["rel_bias"], ((0, 0), (0, N_REL_PAD - N_REL))), dep=ex.dep())
    a1, r1 = _rms_fwd(x, small["norm_mix"], "norm_mix_fwd", dep=[bias])
    ex.done("norm_mix_fwd", a1)
    ex.done("bias_table", a1)
    proj = _matmul(a1, ex.weight("w_in"), "nn", F32, "in_proj", dep=ex.dep())
    y_hg, o_hg, states = _hgrn_fwd(proj, small["lb_logits"], small["hg_norm"])
    ycat = _att_fwd(proj, bias, y_hg, dep=ex.dep())
    ex.done("att_fwd", ycat)
    h1 = _matmul(ycat, ex.weight("w_out"), "nn", F32, "out_proj", resid=x, dep=ex.dep())
    a2, r2 = _rms_fwd(h1, small["norm_ffn"], "norm_ffn_fwd")
    ex.done("norm_ffn_fwd", a2)
    conv_w = ex.weight("conv_w")
    u = _matmul(a2, ex.weight("w_up"), "nn", BF16, "up_proj")
    conv_b = _interleave_cols(small["conv_b"])
    ex.done("up_proj", u)
    z = _ffn_act_fwd(u, conv_w, conv_b)
    ex.done("ffn_act_fwd", z)
    h2 = _matmul(z, ex.weight("w_down"), "nn", F32, "down_proj", tk=2816, resid=h1)
    a3, r3 = _rms_fwd(h2, small["norm_ple"], "norm_ple_fwd")
    gpre = _matmul(a3, ex.weight("w_ple_gate"), "nn", F32, "ple_gate")
    pp = _matmul(p, ex.weight("w_ple_proj"), "nn", F32, "ple_proj")
    dh3, dgpre, dpp, d_final, loss = _ple_loss(gpre, pp, h2, small["final_norm"], target)

    ex.grad("w_ple_proj", _matmul(p, dpp, "tn", F32, "d_w_ple_proj", tm=512))
    ex.grad("w_ple_gate", _matmul(a3, dgpre, "tn", F32, "d_w_ple_gate", tm=512))
    da3 = _matmul(dgpre, ex.weight("w_ple_gate"), "nt", F32, "d_norm_ple_out")
    dh2, d_ple = _rms_bwd(da3, h2, r3, small["norm_ple"], dh3, "norm_ple_bwd")
    dz = _matmul(dh2, ex.weight("w_down"), "nt", BF16, "d_ffn_act")
    ex.grad("w_down", _matmul(z, dh2, "tn", F32, "d_w_down", tm=512))
    du, dcw, dcb = _ffn_act_bwd(u, dz, conv_w, conv_b, dep=ex.dep())
    ex.done("ffn_act_bwd", du)
    d_conv_w, d_conv_b = _deinterleave_cols(dcw), _deinterleave_cols(dcb)
    ex.grad("w_up", _matmul(a2, du, "tn", F32, "d_w_up", tm=512, dep=ex.dep()))
    da2 = _matmul(du, ex.weight("w_up"), "nt", F32, "d_norm_ffn_out", tk=2816, dep=ex.dep())
    dh1, d_ffn = _rms_bwd(da2, h1, r2, small["norm_ffn"], dh2, "norm_ffn_bwd")
    dycat = _matmul(dh1, ex.weight("w_out"), "nt", F32, "d_mix_out")
    ex.done("d_mix_out", dycat)
    ex.grad("w_out", _matmul(ycat, dh1, "tn", F32, "d_w_out", tk=2048, dep=ex.dep()))
    dp_hg, d_lb, d_hgn = _hgrn_bwd(proj, small["lb_logits"], small["hg_norm"], o_hg, dycat, states, dep=ex.dep())
    ex.done("hgrn_bwd", d_lb)
    dproj, dk_att, dv_att, gsum = _att_bwd(proj, bias, dycat, dp_hg, dep=ex.dep())
    d_rel = _rel_bias_grad(gsum)
    d_small = {
        "norm_mix": jnp.zeros_like(small["norm_mix"]), "lb_logits": d_lb, "hg_norm": d_hgn, "rel_bias": d_rel, "norm_ffn": d_ffn,
        "conv_b": d_conv_b, "norm_ple": d_ple, "final_norm": d_final,
    }
    packed = jnp.concatenate([_pack_small(d_small), _rows128(d_conv_w, CONV_W_FULL_ROWS), _rows128(loss[0:1, 0:1], 8)], axis=0)
    early = ex.reduce(packed, "all_reduce_small")
    for k, part in enumerate((dk_att, dv_att)):
        dproj = lax.dynamic_update_slice(dproj, part, (0, 4 * HG_WIDTH + (k + 1) * ATT_WIDTH))
    ex.grad("w_in", _matmul(a1, dproj, "tn", F32, "d_w_in", tm=512, dep=[early]))
    da1 = _matmul(dproj, ex.weight("w_in"), "nt", F32, "d_norm_mix_out", tk=1792, dep=ex.dep())
    dx, d_mix = _rms_bwd(da1, x, r1, small["norm_mix"], dh1, "norm_mix_bwd")
    rows = dict(SMALL)["norm_mix"]
    late = ex.reduce(_rows128(d_mix, rows), "all_reduce_norm_mix")
    ex.done("d_norm_mix_out", late)
    return dx, jnp.concatenate([late, early[rows:]], axis=0)


def kernel(x, p, norm_mix, w_in, lb_logits, hg_norm, rel_bias, w_out, norm_ffn, w_up, conv_w, conv_b, w_down, norm_ple, w_ple_gate, w_ple_proj, final_norm, loss_target, m_norm_mix, m_w_in, m_lb_logits, m_hg_norm, m_rel_bias, m_w_out, m_norm_ffn, m_w_up, m_conv_w, m_conv_b, m_w_down, m_norm_ple, m_w_ple_gate, m_w_ple_proj, m_final_norm, v_norm_mix, v_w_in, v_lb_logits, v_hg_norm, v_rel_bias, v_w_out, v_norm_ffn, v_w_up, v_conv_w, v_conv_b, v_w_down, v_norm_ple, v_w_ple_gate, v_w_ple_proj, v_final_norm):
    given = dict(locals())
    mx, my, mc = _position()
    me = 4 * mx + 2 * my + mc
    big = {k: given[k][0] for k, _ in BIG}
    ex = _Exchange(big, conv_w[0], (mx, my, mc))
    ex.gather()

    small = {
        "norm_mix": norm_mix, "lb_logits": lb_logits, "hg_norm": hg_norm, "rel_bias": rel_bias[0], "norm_ffn": norm_ffn,
        "conv_b": conv_b, "norm_ple": norm_ple, "final_norm": final_norm.reshape(1, -1),
    }
    dx, reduced = _local_step(x[0], p[0, 0], loss_target[0], small, ex)

    out = {}
    shapes = {k: given[k].shape for k, _ in SMALL}
    g_small, at = _unpack_small(reduced, shapes)
    g_conv_full = reduced[at : at + CONV_W_FULL_ROWS].reshape(3, 2 * D_FF)
    total_loss = reduced[at + CONV_W_FULL_ROWS, 0]
    cw = conv_w.shape[2]
    g_conv = lax.dynamic_slice_in_dim(g_conv_full, me * cw, cw, axis=1)

    def pack_with_conv(parts, conv_part):
        return jnp.concatenate([_pack_small(parts), _rows128(conv_part, CONV_W_SHARD_ROWS)], axis=0)

    d_pk, m_pk, v_pk = _adam_small(
        pack_with_conv({k: given[k] for k, _ in SMALL}, conv_w),
        pack_with_conv(g_small, g_conv),
        pack_with_conv({k: given["m_" + k] for k, _ in SMALL}, m_conv_w),
        pack_with_conv({k: given["v_" + k] for k, _ in SMALL}, v_conv_w),
    )
    for name, pk in (("d", d_pk), ("m", m_pk), ("v", v_pk)):
        parts, at = _unpack_small(pk, shapes)
        parts["conv_w"] = pk[at : at + CONV_W_SHARD_ROWS].reshape(-1)[: 3 * cw].reshape(conv_w.shape)
        for k, a in parts.items():
            out.setdefault(k, {})
            out[k][name] = a
    for k, _ in SMALL:
        out[k]["g"] = g_small[k]
    out["conv_w"]["g"] = g_conv.reshape(conv_w.shape)

    after, started = v_pk, ex.dep()
    for gi in range(len(GROUPS)):
        for k, (o, r) in ex.finish(gi, after).items():
            g, d, nm, nv = _adam_big(big[k], given["m_" + k][0], given["v_" + k][0], o, r, "adam_" + k, dep=started)
            out[k] = tuple(a[None] for a in (g, d, nm, nv))
            after = nv

    order = ["norm_mix", "w_in", "lb_logits", "hg_norm", "rel_bias", "w_out", "norm_ffn", "w_up", "conv_w", "conv_b", "w_down", "norm_ple", "w_ple_gate", "w_ple_proj", "final_norm"]

    def pick(k, what):
        return out[k][what] if isinstance(out[k], dict) else out[k][{"g": 0, "d": 1, "m": 2, "v": 3}[what]]

    return (total_loss, dx[None], *[pick(k, "g") for k in order], *[pick(k, "d") for k in order], *[pick(k, "m") for k in order], *[pick(k, "v") for k in order])
```

```python
import functools

import jax
import jax.numpy as jnp
from jax import lax
from jax.experimental import pallas as pl
from jax.experimental.pallas import tpu as pltpu

F32 = jnp.float32
BF16 = jnp.bfloat16

D_MODEL = 2048
CHUNK = 64
HG_HEADS = 8
HEAD_DIM = 128
HG_WIDTH = HG_HEADS * HEAD_DIM
ATT_HEADS = 8
ATT_WIDTH = ATT_HEADS * HEAD_DIM
LEFT_CHUNKS = 8
PAD = LEFT_CHUNKS * CHUNK
BAND = PAD + CHUNK
REL_CLIP = 128
N_REL = 2 * REL_CLIP + 1
N_REL_PAD = 384
D_FF = 5632
EPS = 1e-6
ATT_SCALE = HEAD_DIM ** -0.5
SUB = 32
HG_BLOCK = 8
Q_BLOCK = 4 * CHUNK
K_BLOCK = Q_BLOCK + PAD
DIAG = 1024
ATT_BLOCK = 2
MASKED = -1e30

ADAM_LR = 0.001
ADAM_B1 = 0.9
ADAM_B2 = 0.999
ADAM_EPS = 1e-08
ADAM_WD = 0.01
ADAM_STEP = 10

N_DEV = 8
VMEM_LIMIT = 48 * 1024 * 1024
ATT_BWD_VMEM = 58 * 1024 * 1024
MESH = pl.DeviceIdType.MESH
ANY = pl.BlockSpec(memory_space=pl.ANY)
HIGHEST = lax.Precision.HIGHEST

NN = (((1,), (0,)), ((), ()))
NT = (((1,), (1,)), ((), ()))
TN = (((0,), (0,)), ((), ()))


def _params(*sem):
    return pltpu.CompilerParams(dimension_semantics=sem if sem else None, vmem_limit_bytes=VMEM_LIMIT)


def _pallas(body, n_in, dep, **kw):
    deps = [] if dep is None else list(dep)
    if not deps:
        return pl.pallas_call(body, **kw)

    def body_after(*refs):
        body(*refs[:n_in], *refs[n_in + len(deps) :])

    call = pl.pallas_call(body_after, **dict(kw, in_specs=list(kw["in_specs"]) + [ANY] * len(deps)))
    return lambda *ops: call(*ops, *deps)


def _dot(a, b, dims=NN):
    return lax.dot_general(a, b, dims, preferred_element_type=F32)


def _dot3(a, b, dims=NN):
    a_hi, b_hi = a.astype(BF16), b.astype(BF16)
    a_lo, b_lo = (a - a_hi.astype(F32)).astype(BF16), (b - b_hi.astype(F32)).astype(BF16)
    return _dot(a_hi, b_hi, dims) + (_dot(a_hi, b_lo, dims) + _dot(a_lo, b_hi, dims))


def _sigmoid(x):
    return 1.0 / (1.0 + jnp.exp(-x))


def _tile(n, prefs):
    for t in prefs:
        if n % t == 0:
            return t
    return n


def _matmul(a, b, mode, out_dtype, name, tm=1024, tn=1024, tk=None, resid=None, dep=None):
    if mode == "nn":
        (m, k), n = a.shape, b.shape[1]
    elif mode == "nt":
        (m, k), n = a.shape, b.shape[0]
    else:
        (k, m), n = a.shape, b.shape[1]
    tm = _tile(m, (tm, 512, 256, 128))
    tn = _tile(n, (tn, 1408, 512, 256, 128))
    tk = k if tk is None else _tile(k, (tk,))
    nk = k // tk
    dims = {"nn": NN, "nt": NT, "tn": TN}[mode]
    a_spec = pl.BlockSpec((tk, tm), lambda i, j, s: (s, i)) if mode == "tn" else pl.BlockSpec((tm, tk), lambda i, j, s: (i, s))
    b_spec = pl.BlockSpec((tn, tk), lambda i, j, s: (j, s)) if mode == "nt" else pl.BlockSpec((tk, tn), lambda i, j, s: (s, j))
    o_spec = pl.BlockSpec((tm, tn), lambda i, j, s: (i, j))
    has_res = resid is not None

    def body(*refs):
        a_ref, b_ref = refs[0], refs[1]
        o_ref = refs[2 + has_res]
        part = _dot(a_ref[...].astype(BF16), b_ref[...].astype(BF16), dims)

        def finish(acc):
            if has_res:
                acc = acc + refs[2][...]
            o_ref[...] = acc.astype(out_dtype)

        if nk == 1:
            finish(part)
        else:
            acc_ref = refs[-1]
            s = pl.program_id(2)

            @pl.when(s == 0)
            def _():
                acc_ref[...] = part

            @pl.when(s > 0)
            def _():
                acc_ref[...] += part

            @pl.when(s == nk - 1)
            def _():
                finish(acc_ref[...])

    return _pallas(
        body,
        2 + has_res,
        dep,
        name=name,
        grid=(m // tm, n // tn, nk),
        in_specs=[a_spec, b_spec] + ([o_spec] if has_res else []),
        out_specs=o_spec,
        out_shape=jax.ShapeDtypeStruct((m, n), out_dtype),
        scratch_shapes=[pltpu.VMEM((tm, tn), F32)] if nk > 1 else [],
        compiler_params=_params("parallel", "parallel", "arbitrary"),
    )(*([a, b] + ([resid] if has_res else [])))


def _rms_fwd(x, g, name, dep=None):
    t, d = x.shape
    tm = _tile(t, (256,))

    def body(x_ref, g_ref, a_ref, r_ref):
        xv = x_ref[...]
        r = lax.rsqrt(jnp.mean(xv * xv, axis=-1, keepdims=True) + EPS)
        a_ref[...] = (xv * r * g_ref[...]).astype(BF16)
        r_ref[...] = r

    row = pl.BlockSpec((tm, d), lambda i: (i, 0))
    return _pallas(
        body,
        2,
        dep,
        name=name,
        grid=(t // tm,),
        in_specs=[row, pl.BlockSpec((1, d), lambda i: (0, 0))],
        out_specs=[row, pl.BlockSpec((tm, 1), lambda i: (i, 0))],
        out_shape=[jax.ShapeDtypeStruct((t, d), BF16), jax.ShapeDtypeStruct((t, 1), F32)],
        compiler_params=_params("parallel"),
    )(x, g)


def _rms_bwd(da, x, r, g, resid, name, dep=None):
    t, d = x.shape
    tm = _tile(t, (256,))

    def body(da_ref, x_ref, r_ref, g_ref, res_ref, dx_ref, dg_ref):
        i = pl.program_id(0)
        rv = r_ref[...]
        n = x_ref[...] * rv
        dav = da_ref[...]
        dn = dav * g_ref[...]
        dx_ref[...] = rv * (dn - n * jnp.mean(dn * n, axis=-1, keepdims=True)) + res_ref[...]
        part = jnp.sum(dav * n, axis=0, keepdims=True)

        @pl.when(i == 0)
        def _():
            dg_ref[...] = part

        @pl.when(i > 0)
        def _():
            dg_ref[...] += part

    row = pl.BlockSpec((tm, d), lambda i: (i, 0))
    vec = pl.BlockSpec((1, d), lambda i: (0, 0))
    return _pallas(
        body,
        5,
        dep,
        name=name,
        grid=(t // tm,),
        in_specs=[row, row, pl.BlockSpec((tm, 1), lambda i: (i, 0)), vec, row],
        out_specs=[row, vec],
        out_shape=[jax.ShapeDtypeStruct((t, d), F32), jax.ShapeDtypeStruct((1, d), F32)],
        compiler_params=_params("arbitrary"),
    )(da, x, r, g, resid)


def _tri(n, upper):
    r = lax.broadcasted_iota(jnp.int32, (n, n), 0)
    c = lax.broadcasted_iota(jnp.int32, (n, n), 1)
    return jnp.where((c >= r) if upper else (c <= r), 1.0, 0.0).astype(F32)


def _hgrn_gates(q, fp, lbl):
    l0, l1 = lbl[0:1, :], lbl[1:2, :]
    mx = jnp.maximum(l0, l1)
    e0, e1 = jnp.exp(l0 - mx), jnp.exp(l1 - mx)
    lb = e0 / (e0 + e1)
    sig = _sigmoid(fp)
    f = lb + (1.0 - lb) * sig
    kk = (1.0 - lb) * _sigmoid(-fp)
    sq = _sigmoid(q)
    b = jnp.dot(_tri(CHUNK, False), jnp.log(f), precision=HIGHEST, preferred_element_type=F32)
    return lb, sig, f, kk, sq, q * sq, b


def _heads(x):
    return [x[:, j * HEAD_DIM : (j + 1) * HEAD_DIM] for j in range(x.shape[1] // HEAD_DIM)]


def _wide(parts):
    return jnp.concatenate(parts, axis=1)


def _intra_blocks(b):
    out = []
    for lo in range(0, CHUNK, SUB):
        hi = lo + SUB
        br = b[lo + SUB // 2 : lo + SUB // 2 + 1, :]
        row = lax.broadcasted_iota(jnp.int32, (SUB, hi), 0) + lo
        col = lax.broadcasted_iota(jnp.int32, (SUB, hi), 1)
        out.append((lo, hi, jnp.exp(b[lo:hi] - br), jnp.exp(br - b[:hi]), col <= row))
    return out


def _hgrn_fwd(proj, lb_logits, hg_norm):
    t = proj.shape[0]
    nc = t // CHUNK

    def body(q_ref, f_ref, i_ref, g_ref, lbl_ref, hgn_ref, y_ref, o_ref, st_ref, s_scr):
        c = pl.program_id(1)

        @pl.when(c == 0)
        def _():
            s_scr[...] = jnp.zeros_like(s_scr)

        hs = range(HG_BLOCK)
        sts = [s_scr[j] for j in hs]
        _, _, _, kk, _, qf, b = _hgrn_gates(q_ref[...], f_ref[...], lbl_ref[...])
        vb = _heads(i_ref[...].astype(BF16))
        bl = b[CHUNK - 1 : CHUNK, :]
        qe = _heads((qf * jnp.exp(b)).astype(BF16))
        kd = _heads((kk * jnp.exp(bl - b)).astype(BF16))
        decay = _heads(jnp.exp(bl))
        o = [_dot(qe[j], sts[j].astype(BF16), NT) for j in hs]
        parts = [[] for _ in hs]
        for lo, hi, ea, eb, mask in _intra_blocks(b):
            a, bk = _heads((qf[lo:hi] * ea).astype(BF16)), _heads((kk[:hi] * eb).astype(BF16))
            p = [jnp.where(mask, _dot(a[j], bk[j], NT), 0.0).astype(BF16) for j in hs]
            for j in hs:
                parts[j].append(_dot(p[j], vb[j][:hi]))
        o = [o[j] + jnp.concatenate(parts[j], axis=0) for j in hs]
        new = [sts[j] * decay[j] + _dot(vb[j], kd[j], TN) for j in hs]
        hgn = hgn_ref[...]
        on = [o[j] * lax.rsqrt(jnp.mean(o[j] * o[j], axis=-1, keepdims=True) + EPS) * hgn for j in hs]
        gg = g_ref[...]
        for j in hs:
            st_ref[j] = sts[j]
            s_scr[j] = new[j]
        o_ref[...] = _wide(o)
        y_ref[...] = (_wide(on) * (gg * _sigmoid(gg))).astype(BF16)

    wide = HG_BLOCK * HEAD_DIM
    groups = HG_HEADS // HG_BLOCK

    def col(k):
        return pl.BlockSpec((CHUNK, wide), lambda g, c: (c, k * groups + g))

    out = pl.BlockSpec((CHUNK, wide), lambda g, c: (c, g))
    return pl.pallas_call(
        body,
        name="hgrn_fwd",
        grid=(groups, nc),
        in_specs=[col(0), col(1), col(2), col(3), pl.BlockSpec((2, wide), lambda g, c: (0, g)), pl.BlockSpec((1, HEAD_DIM), lambda g, c: (0, 0))],
        out_specs=[out, out, pl.BlockSpec((HG_BLOCK, None, HEAD_DIM, HEAD_DIM), lambda g, c: (g, c, 0, 0))],
        out_shape=[
            jax.ShapeDtypeStruct((t, HG_WIDTH + ATT_WIDTH), BF16),
            jax.ShapeDtypeStruct((t, HG_WIDTH), F32),
            jax.ShapeDtypeStruct((HG_HEADS, nc, HEAD_DIM, HEAD_DIM), F32),
        ],
        scratch_shapes=[pltpu.VMEM((HG_BLOCK, HEAD_DIM, HEAD_DIM), F32)],
        compiler_params=_params("arbitrary", "arbitrary"),
    )(proj, proj, proj, proj, lb_logits, hg_norm)


def _hgrn_bwd(proj, lb_logits, hg_norm, o_hg, dycat, states, dep=None):
    t = proj.shape[0]
    nc = t // CHUNK

    def body(q_ref, f_ref, i_ref, g_ref, lbl_ref, hgn_ref, o_ref, dy_ref, st_ref, dp_ref, dlbl_ref, dhgn_ref, dst_scr, dlb_scr):
        h = pl.program_id(0)
        c = pl.program_id(1)

        @pl.when(c == 0)
        def _():
            dst_scr[...] = jnp.zeros_like(dst_scr)
            dlb_scr[...] = jnp.zeros_like(dlb_scr)

        @pl.when((c == 0) & (h == 0))
        def _():
            dhgn_ref[...] = jnp.zeros_like(dhgn_ref)

        hs = range(HG_BLOCK)
        hgn = _wide([hgn_ref[...]] * HG_BLOCK)
        q, fp, gg, vi = q_ref[...], f_ref[...], g_ref[...], i_ref[...]
        lb, sig, f, kk, sq, qf, b = _hgrn_gates(q, fp, lbl_ref[...])
        o, dy = o_ref[...], dy_ref[...]
        sg = _sigmoid(gg)
        n = _wide([oh * lax.rsqrt(jnp.mean(oh * oh, axis=-1, keepdims=True) + EPS) for oh in _heads(o)])
        don = dy * (gg * sg)
        dgg = dy * (n * hgn) * (sg * (1.0 + gg * (1.0 - sg)))
        d_hgn = sum(_heads(jnp.sum(don * n, axis=0, keepdims=True)))
        dn = don * hgn
        do = _wide(
            [
                lax.rsqrt(jnp.mean(oh * oh, axis=-1, keepdims=True) + EPS) * (dnh - nh * jnp.mean(dnh * nh, axis=-1, keepdims=True))
                for oh, dnh, nh in zip(_heads(o), _heads(dn), _heads(n))
            ]
        )
        sts = [st_ref[j] for j in hs]
        dstn = [dst_scr[j] for j in hs]
        bl = b[CHUNK - 1 : CHUNK, :]
        e_b, e_bl, e_l = jnp.exp(b), jnp.exp(bl - b), jnp.exp(bl)
        doh, vih = _heads(do), _heads(vi)
        dobh = _heads(do.astype(BF16))
        dq_acc = _wide([_dot3(doh[j], sts[j]) for j in hs]) * e_b
        dk_inter = _wide([_dot3(vih[j], dstn[j]) for j in hs]) * e_bl
        dk_acc = dk_inter
        kd = _heads((kk * e_bl).astype(BF16))
        dv_acc = _wide([_dot(kd[j], dstn[j].astype(BF16), NT) for j in hs])
        qe, decay = _heads((qf * e_b).astype(BF16)), _heads(e_l)
        dst_new = [dstn[j] * decay[j] + _dot(dobh[j], qe[j], TN) for j in hs]
        db_last = e_l * _wide([jnp.sum(sts[j] * dstn[j], axis=0, keepdims=True) for j in hs]) + jnp.sum(kk * dk_inter, axis=0, keepdims=True)
        dq_parts = []
        for lo, hi, ea, eb, mask in _intra_blocks(b):
            a, bk = qf[lo:hi] * ea, kk[:hi] * eb
            ah, bkh = _heads(a), _heads(bk)
            abh, bkbh = _heads(a.astype(BF16)), _heads(bk.astype(BF16))
            p = [jnp.where(mask, _dot(abh[j], bkbh[j], NT), 0.0).astype(BF16) for j in hs]
            dp = [jnp.where(mask, _dot3(doh[j][lo:hi], vih[j][:hi], NT), 0.0) for j in hs]
            dq_parts.append(_wide([_dot3(dp[j], bkh[j]) for j in hs]) * ea)
            dki = _wide([_dot3(dp[j], ah[j], TN) for j in hs]) * eb
            dvi = _wide([_dot(p[j], dobh[j][lo:hi], TN) for j in hs])
            if hi < CHUNK:
                zeros = jnp.zeros((CHUNK - hi, HG_BLOCK * HEAD_DIM), F32)
                dki = jnp.concatenate([dki, zeros], axis=0)
                dvi = jnp.concatenate([dvi, zeros], axis=0)
            dk_acc = dk_acc + dki
            dv_acc = dv_acc + dvi
        dq_acc = dq_acc + jnp.concatenate(dq_parts, axis=0)
        rows = lax.broadcasted_iota(jnp.int32, dq_acc.shape, 0)
        db = qf * dq_acc - kk * dk_acc + jnp.where(rows == CHUNK - 1, db_last, 0.0)
        dlf = jnp.dot(_tri(CHUNK, True), db, precision=HIGHEST, preferred_element_type=F32)
        dfk = dlf / f - dk_acc
        for k, part in enumerate((dq_acc * (sq * (1.0 + q * (1.0 - sq))), (1.0 - lb) * dfk * sig * (1.0 - sig), dv_acc, dgg)):
            dp_ref[:, k * HG_WIDTH : (k + 1) * HG_WIDTH] = part.astype(BF16)
        dlb_scr[...] += jnp.sum(dfk * (1.0 - sig), axis=0, keepdims=True)
        dhgn_ref[...] += d_hgn
        for j in hs:
            dst_scr[j] = dst_new[j]

        @pl.when(c == nc - 1)
        def _():
            dl0 = dlb_scr[...] * lb * (1.0 - lb)
            dlbl_ref[0:1, :] = dl0
            dlbl_ref[1:2, :] = -dl0

    wide = HG_BLOCK * HEAD_DIM
    groups = HG_HEADS // HG_BLOCK

    def col(k):
        return pl.BlockSpec((CHUNK, wide), lambda g, c: (nc - 1 - c, k * groups + g))

    blk = pl.BlockSpec((CHUNK, wide), lambda g, c: (nc - 1 - c, g))
    assert groups == 1, "d(q, f, i, g) are written as one contiguous column range of the in_proj gradient"
    return _pallas(
        body,
        9,
        dep,
        name="hgrn_bwd",
        grid=(groups, nc),
        in_specs=[
            col(0), col(1), col(2), col(3),
            pl.BlockSpec((2, wide), lambda g, c: (0, g)),
            pl.BlockSpec((1, HEAD_DIM), lambda g, c: (0, 0)),
            blk, blk,
            pl.BlockSpec((HG_BLOCK, None, HEAD_DIM, HEAD_DIM), lambda g, c: (g, nc - 1 - c, 0, 0)),
        ],
        out_specs=[
            pl.BlockSpec((CHUNK, 4 * HG_WIDTH), lambda g, c: (nc - 1 - c, 0)),
            pl.BlockSpec((2, wide), lambda g, c: (0, g)),
            pl.BlockSpec((1, HEAD_DIM), lambda g, c: (0, 0)),
        ],
        out_shape=[
            jax.ShapeDtypeStruct((t, 4 * HG_WIDTH + 3 * ATT_WIDTH), BF16),
            jax.ShapeDtypeStruct((2, HG_WIDTH), F32),
            jax.ShapeDtypeStruct((1, HEAD_DIM), F32),
        ],
        scratch_shapes=[pltpu.VMEM((HG_BLOCK, HEAD_DIM, HEAD_DIM), F32), pltpu.VMEM((1, wide), F32)],
        compiler_params=_params("arbitrary", "arbitrary"),
    )(proj, proj, proj, proj, lb_logits, hg_norm, o_hg, dycat, states)


def _diagonal_slots(shift):
    i = lax.broadcasted_iota(jnp.int32, (N_REL_PAD, DIAG), 0)
    u = lax.broadcasted_iota(jnp.int32, (N_REL_PAD, DIAG), 1)
    offset = u - shift if shift else jnp.where(u < K_BLOCK, u, u - DIAG)
    return jnp.where(jnp.clip(PAD - offset, -REL_CLIP, REL_CLIP) + REL_CLIP == i, 1.0, 0.0).astype(BF16)


def _split3(x):
    hi = x.astype(BF16)
    mid = (x - hi.astype(F32)).astype(BF16)
    return hi, mid, (x - hi.astype(F32) - mid.astype(F32)).astype(BF16)


def _bias_table(rel_bias, dep=None):
    def body(rb_ref, o_ref, diag):
        h = pl.program_id(0)

        @pl.when(h == 0)
        def _():
            hi, mid, lo = _split3(rb_ref[...])
            slots = _diagonal_slots(0)
            diag[...] = _dot(hi, slots) + (_dot(mid, slots) + _dot(lo, slots))

        rows = jnp.broadcast_to(diag[pl.ds(h, 1), :], (Q_BLOCK, DIAG))
        row = lax.broadcasted_iota(jnp.int32, (Q_BLOCK, K_BLOCK), 0)
        col = lax.broadcasted_iota(jnp.int32, (Q_BLOCK, K_BLOCK), 1)
        first = row - (row & (CHUNK - 1))
        seen = (col >= first) & (col < first + BAND)
        o_ref[...] = jnp.where(seen, pltpu.roll(rows, 0, 1, stride=1, stride_axis=0)[:, :K_BLOCK], MASKED)

    return _pallas(
        body,
        1,
        dep,
        name="bias_table",
        grid=(ATT_HEADS,),
        in_specs=[pl.BlockSpec((ATT_HEADS, N_REL_PAD), lambda h: (0, 0))],
        out_specs=pl.BlockSpec((None, Q_BLOCK, K_BLOCK), lambda h: (h, 0, 0)),
        out_shape=jax.ShapeDtypeStruct((ATT_HEADS, Q_BLOCK, K_BLOCK), F32),
        scratch_shapes=[pltpu.VMEM((ATT_HEADS, DIAG), F32)],
        compiler_params=_params("arbitrary"),
    )(rel_bias)


def _att_probs(q_ref, kpad, bias_ref, blk, cols=slice(None), head=None):
    qs = (q_ref[:, cols] * ATT_SCALE).astype(BF16)
    start = pl.multiple_of(blk * Q_BLOCK, Q_BLOCK)
    kb = kpad[pl.ds(start, K_BLOCK), cols]
    s = _dot(qs, kb, NT) + (bias_ref[...] if head is None else bias_ref[head])
    col = lax.broadcasted_iota(jnp.int32, (Q_BLOCK, K_BLOCK), 1)
    s = jnp.where(col >= PAD - blk * Q_BLOCK, s, MASKED)
    e = jnp.exp(s - jnp.max(s, axis=-1, keepdims=True))
    return qs, kb, start, e * (1.0 / jnp.sum(e, axis=-1, keepdims=True))


def _fill_padded(dst, src):
    dst[0:PAD, :] = jnp.zeros((PAD, dst.shape[1]), BF16)
    dst[PAD:, :] = src[...].astype(BF16)


def _att_fwd(proj, bias, y_mix, dep=None):
    t = proj.shape[0]
    nb = t // Q_BLOCK

    def body(q_ref, k_ref, v_ref, bias_ref, y_in_ref, y_ref, kpad, vpad):
        c = pl.program_id(1)

        @pl.when(c == 0)
        def _():
            _fill_padded(kpad, k_ref)
            _fill_padded(vpad, v_ref)

        for j in range(ATT_BLOCK):
            cols = slice(j * HEAD_DIM, (j + 1) * HEAD_DIM)
            _, _, start, p = _att_probs(q_ref, kpad, bias_ref, c, cols, j)
            y_ref[:, cols] = _dot(p.astype(BF16), vpad[pl.ds(start, K_BLOCK), cols]).astype(BF16)

    wide = ATT_BLOCK * HEAD_DIM
    base = 4 * HG_HEADS // ATT_BLOCK
    groups = ATT_HEADS // ATT_BLOCK
    return _pallas(
        body,
        5,
        dep,
        name="att_fwd",
        grid=(groups, nb),
        in_specs=[
            pl.BlockSpec((Q_BLOCK, wide), lambda g, c: (c, base + g)),
            pl.BlockSpec((t, wide), lambda g, c: (0, base + groups + g)),
            pl.BlockSpec((t, wide), lambda g, c: (0, base + 2 * groups + g)),
            pl.BlockSpec((ATT_BLOCK, Q_BLOCK, K_BLOCK), lambda g, c: (g, 0, 0)),
            ANY,
        ],
        out_specs=pl.BlockSpec((Q_BLOCK, wide), lambda g, c: (c, HG_HEADS // ATT_BLOCK + g)),
        out_shape=jax.ShapeDtypeStruct(y_mix.shape, BF16),
        input_output_aliases={4: 0},
        scratch_shapes=[pltpu.VMEM((t + PAD, wide), BF16), pltpu.VMEM((t + PAD, wide), BF16)],
        compiler_params=_params("arbitrary", "arbitrary"),
    )(proj, proj, proj, bias, y_mix)


def _att_bwd(proj, bias, dycat, dproj, dep=None):
    t = proj.shape[0]
    nb = t // Q_BLOCK

    def body(q_ref, k_ref, v_ref, bias_ref, dy_ref, dp_in_ref, dq_ref, dk_ref, dv_ref, g_ref, kpad, vpad, dkacc, dvacc):
        c = pl.program_id(1)

        @pl.when(c == 0)
        def _():
            _fill_padded(kpad, k_ref)
            _fill_padded(vpad, v_ref)
            dkacc[...] = jnp.zeros_like(dkacc)
            dvacc[...] = jnp.zeros_like(dvacc)
            g_ref[...] = jnp.zeros_like(g_ref)

        for j in range(ATT_BLOCK):
            cols = slice(j * HEAD_DIM, (j + 1) * HEAD_DIM)
            qs, kb, start, p = _att_probs(q_ref, kpad, bias_ref, c, cols, j)
            band = pl.ds(start, K_BLOCK)
            dyb = dy_ref[:, cols].astype(BF16)
            dvacc[band, cols] += _dot(p.astype(BF16), dyb, TN)
            dp = _dot(dyb, vpad[band, cols], NT)
            ds = p * (dp - jnp.sum(dp * p, axis=-1, keepdims=True))
            g_ref[j] += ds
            dsb = ds.astype(BF16)
            dq_ref[:, cols] = (_dot(dsb, kb) * ATT_SCALE).astype(BF16)
            dkacc[band, cols] += _dot(dsb, qs, TN)

        @pl.when(c == nb - 1)
        def _():
            dk_ref[...] = dkacc[PAD:, :].astype(BF16)
            dv_ref[...] = dvacc[PAD:, :].astype(BF16)

    wide = ATT_BLOCK * HEAD_DIM
    base = 4 * HG_HEADS // ATT_BLOCK
    groups = ATT_HEADS // ATT_BLOCK
    whole = pl.BlockSpec((t, wide), lambda g, c: (0, g))
    table = pl.BlockSpec((ATT_BLOCK, Q_BLOCK, K_BLOCK), lambda g, c: (g, 0, 0))
    return _pallas(
        body,
        6,
        dep,
        name="att_bwd",
        grid=(groups, nb),
        in_specs=[
            pl.BlockSpec((Q_BLOCK, wide), lambda g, c: (c, base + g)),
            pl.BlockSpec((t, wide), lambda g, c: (0, base + groups + g)),
            pl.BlockSpec((t, wide), lambda g, c: (0, base + 2 * groups + g)),
            table,
            pl.BlockSpec((Q_BLOCK, wide), lambda g, c: (c, HG_HEADS // ATT_BLOCK + g)),
            ANY,
        ],
        out_specs=[pl.BlockSpec((Q_BLOCK, wide), lambda g, c: (c, base + g)), whole, whole, table],
        input_output_aliases={5: 0},
        out_shape=[
            jax.ShapeDtypeStruct(dproj.shape, BF16),
            jax.ShapeDtypeStruct((t, ATT_WIDTH), BF16),
            jax.ShapeDtypeStruct((t, ATT_WIDTH), BF16),
            jax.ShapeDtypeStruct((ATT_HEADS, Q_BLOCK, K_BLOCK), F32),
        ],
        scratch_shapes=[
            pltpu.VMEM((t + PAD, wide), BF16),
            pltpu.VMEM((t + PAD, wide), BF16),
            pltpu.VMEM((t + PAD, wide), F32),
            pltpu.VMEM((t + PAD, wide), F32),
        ],
        compiler_params=pltpu.CompilerParams(dimension_semantics=("arbitrary", "arbitrary"), vmem_limit_bytes=ATT_BWD_VMEM),
    )(proj, proj, proj, bias, dycat, dproj)


def _rel_bias_grad(gsum):
    def body(g_ref, o_ref):
        r = lax.broadcasted_iota(jnp.int32, (Q_BLOCK, Q_BLOCK), 0)
        c = lax.broadcasted_iota(jnp.int32, (Q_BLOCK, Q_BLOCK), 1)
        flip = jnp.where(r + c == Q_BLOCK - 1, 1.0, 0.0).astype(BF16)
        sums = []
        for h in range(ATT_HEADS):
            hi, mid, lo = _split3(g_ref[h])
            rev = _dot(flip, hi) + (_dot(flip, mid) + _dot(flip, lo))
            wide = jnp.concatenate([rev, jnp.zeros((Q_BLOCK, DIAG - K_BLOCK), F32)], axis=1)
            sums.append(jnp.sum(pltpu.roll(wide, 0, 1, stride=1, stride_axis=0), axis=0, keepdims=True))
        hi, mid, lo = _split3(jnp.concatenate(sums, axis=0))
        slots = _diagonal_slots(Q_BLOCK - 1)
        o_ref[...] = _dot(hi, slots, NT) + (_dot(mid, slots, NT) + _dot(lo, slots, NT))

    return pl.pallas_call(
        body,
        name="rel_bias_grad",
        out_shape=jax.ShapeDtypeStruct((ATT_HEADS, N_REL_PAD), F32),
        compiler_params=_params(),
    )(gsum)


HALO = 16


FF_TILE = 1408
FF_TILES = D_FF // FF_TILE


def _interleave_cols(a):
    lead = a.shape[:-1]
    return jnp.swapaxes(a.reshape(*lead, 2, FF_TILES, FF_TILE), -3, -2).reshape(*lead, 2 * D_FF)


def _deinterleave_cols(a):
    lead = a.shape[:-1]
    return jnp.swapaxes(a.reshape(*lead, FF_TILES, 2, FF_TILE), -3, -2).reshape(*lead, 2 * D_FF)


def _ffn_specs(t, tm):
    wide = 2 * FF_TILE
    tile = pl.BlockSpec((tm, wide), lambda j, i: (i, j))
    before = pl.BlockSpec((HALO, wide), lambda j, i: (jnp.maximum(i * (tm // HALO) - 1, 0), j))
    after = pl.BlockSpec((HALO, wide), lambda j, i: (jnp.minimum((i + 1) * (tm // HALO), t // HALO - 1), j))
    vec = lambda rows: pl.BlockSpec((rows, wide), lambda j, i: (0, j))
    return tile, before, after, vec


def _shifted(x, rows, offsets):
    r = lax.broadcasted_iota(jnp.int32, (rows, x.shape[0]), 0)
    c = lax.broadcasted_iota(jnp.int32, (rows, x.shape[0]), 1)
    pick = jnp.concatenate([jnp.where(c == r + o, 1.0, 0.0).astype(BF16) for o in offsets], axis=0)
    out = _dot(pick, x)
    return [out[k * rows : (k + 1) * rows] for k in range(len(offsets))]


def _conv(x, w, b, rows):
    taps = _shifted(x, rows, [HALO - 2, HALO - 1]) + [x[HALO : HALO + rows].astype(F32)]
    return b + w[0:1] * taps[0] + w[1:2] * taps[1] + w[2:3] * taps[2], taps


def _ffn_act_fwd(u, conv_w, conv_b):
    t = u.shape[0]
    tm = _tile(t, (128,))
    tile, before, _, vec = _ffn_specs(t, tm)

    def body(u_ref, h_ref, w_ref, b_ref, z_ref):
        first = pl.program_id(1) == 0
        halo = h_ref[...]
        x = jnp.concatenate([jnp.where(first, jnp.zeros_like(halo), halo), u_ref[...]], axis=0)
        c, _ = _conv(x, w_ref[...], b_ref[...], tm)
        gate, val = c[:, :FF_TILE], c[:, FF_TILE:]
        z_ref[...] = (gate * _sigmoid(gate) * val).astype(BF16)

    return pl.pallas_call(
        body,
        name="ffn_act_fwd",
        grid=(FF_TILES, t // tm),
        in_specs=[tile, before, vec(3), vec(1)],
        out_specs=pl.BlockSpec((tm, FF_TILE), lambda j, i: (i, j)),
        out_shape=jax.ShapeDtypeStruct((t, D_FF), BF16),
        compiler_params=_params("parallel", "parallel"),
    )(u, u, conv_w, conv_b)


def _ffn_act_bwd(u, dz, conv_w, conv_b, dep=None):
    t = u.shape[0]
    tm = _tile(t, (128,))
    nt = t // tm
    ext = tm + HALO
    tile, before, after, vec = _ffn_specs(t, tm)

    def body(u_ref, ub_ref, ua_ref, w_ref, b_ref, dz_ref, dza_ref, du_ref, dw_ref, db_ref):
        i = pl.program_id(1)
        first, last = i == 0, i == nt - 1
        ub, ua = ub_ref[...], ua_ref[...]
        parts = [jnp.where(first, jnp.zeros_like(ub), ub), u_ref[...], jnp.where(last, jnp.zeros_like(ua), ua)]
        w = w_ref[...]
        c, taps = _conv(jnp.concatenate(parts, axis=0), w, b_ref[...], ext)
        gate, val = c[:, :FF_TILE], c[:, FF_TILE:]
        dz = jnp.concatenate([dz_ref[...].astype(F32), jnp.where(last, 0.0, dza_ref[...].astype(F32))], axis=0)
        sg = _sigmoid(gate)
        d = jnp.concatenate([dz * val * (sg * (1.0 + gate * (1.0 - sg))), dz * (gate * sg)], axis=1)
        d1, d2 = _shifted(d.astype(BF16), tm, [1, 2])
        du_ref[...] = (w[2:3] * d[:tm] + w[1:2] * d1 + w[0:1] * d2).astype(BF16)

        @pl.when(first)
        def _():
            dw_ref[...] = jnp.zeros_like(dw_ref)
            db_ref[...] = jnp.zeros_like(db_ref)

        for k, tap in enumerate(taps):
            dw_ref[k : k + 1, :] += jnp.sum(d[:tm] * tap[:tm], axis=0, keepdims=True)
        db_ref[...] += jnp.sum(d[:tm], axis=0, keepdims=True)

    narrow = lambda rows, index: pl.BlockSpec((rows, FF_TILE), index)
    return _pallas(
        body,
        7,
        dep,
        name="ffn_act_bwd",
        grid=(FF_TILES, nt),
        in_specs=[
            tile, before, after, vec(3), vec(1),
            narrow(tm, lambda j, i: (i, j)),
            narrow(HALO, lambda j, i: (jnp.minimum((i + 1) * (tm // HALO), t // HALO - 1), j)),
        ],
        out_specs=[tile, vec(3), vec(1)],
        out_shape=[
            jax.ShapeDtypeStruct((t, 2 * D_FF), BF16),
            jax.ShapeDtypeStruct((3, 2 * D_FF), F32),
            jax.ShapeDtypeStruct((1, 2 * D_FF), F32),
        ],
        compiler_params=_params("parallel", "arbitrary"),
    )(u, u, u, conv_w, conv_b, dz, dz)


def _ple_loss(gpre, pp, h2, final_norm, target):
    t, d = h2.shape
    tm = _tile(t, (256,))

    def body(gp_ref, pp_ref, h_ref, g_ref, tg_ref, dh_ref, dgp_ref, dpp_ref, dg_ref, loss_ref):
        i = pl.program_id(0)
        gate = _sigmoid(gp_ref[...])
        ppv = pp_ref[...]
        h3 = h_ref[...] + gate * ppv
        r = lax.rsqrt(jnp.mean(h3 * h3, axis=-1, keepdims=True) + EPS)
        n = h3 * r
        g = g_ref[...]
        err = n * g - tg_ref[...]
        loss = 0.5 * jnp.sum(jnp.mean(err * err, axis=-1, keepdims=True))
        dy = err * (1.0 / d)
        dn = dy * g
        dh = r * (dn - n * jnp.mean(dn * n, axis=-1, keepdims=True))
        dh_ref[...] = dh
        dgp_ref[...] = (dh * ppv * gate * (1.0 - gate)).astype(BF16)
        dpp_ref[...] = (dh * gate).astype(BF16)
        dg = jnp.sum(dy * n, axis=0, keepdims=True)

        @pl.when(i == 0)
        def _():
            dg_ref[...] = dg
            loss_ref[...] = jnp.full(loss_ref.shape, loss, F32)

        @pl.when(i > 0)
        def _():
            dg_ref[...] += dg
            loss_ref[...] += loss

    row = pl.BlockSpec((tm, d), lambda i: (i, 0))
    vec = pl.BlockSpec((1, d), lambda i: (0, 0))
    return pl.pallas_call(
        body,
        name="ple_loss",
        grid=(t // tm,),
        in_specs=[row, row, row, vec, row],
        out_specs=[row, row, row, vec, pl.BlockSpec((8, 128), lambda i: (0, 0))],
        out_shape=[
            jax.ShapeDtypeStruct((t, d), F32),
            jax.ShapeDtypeStruct((t, d), BF16),
            jax.ShapeDtypeStruct((t, d), BF16),
            jax.ShapeDtypeStruct((1, d), F32),
            jax.ShapeDtypeStruct((8, 128), F32),
        ],
        compiler_params=_params("arbitrary"),
    )(gpre, pp, h2, final_norm, target)


def _adamw(w, g, m, v):
    m = ADAM_B1 * m + (1.0 - ADAM_B1) * g
    v = ADAM_B2 * v + (1.0 - ADAM_B2) * (g * g)
    m_hat = m / (1.0 - ADAM_B1 ** ADAM_STEP)
    v_hat = v / (1.0 - ADAM_B2 ** ADAM_STEP)
    return -ADAM_LR * (m_hat / (jnp.sqrt(v_hat) + ADAM_EPS) + ADAM_WD * w), m, v


def _adam_big(w, m, v, own, recv, name, dep=None):
    r, c = w.shape
    tr = _tile(r, (256, 176))

    def body(w_ref, m_ref, v_ref, own_ref, recv_ref, g_ref, d_ref, nm_ref, nv_ref):
        g = own_ref[...]
        for k in range(3):
            g = g + recv_ref[k].astype(F32)
        g_ref[...] = g
        d_ref[...], nm_ref[...], nv_ref[...] = _adamw(w_ref[...], g, m_ref[...], v_ref[...])

    blk = pl.BlockSpec((tr, c), lambda i: (i, 0))
    return _pallas(
        body,
        5,
        dep,
        name=name,
        grid=(r // tr,),
        in_specs=[blk, blk, blk, blk, pl.BlockSpec((3, tr, c), lambda i: (0, i, 0))],
        out_specs=[blk] * 4,
        out_shape=[jax.ShapeDtypeStruct((r, c), F32)] * 4,
        compiler_params=_params("parallel"),
    )(w, m, v, own, recv)


def _adam_small(w, g, m, v):
    def body(w_ref, g_ref, m_ref, v_ref, d_ref, nm_ref, nv_ref):
        d_ref[...], nm_ref[...], nv_ref[...] = _adamw(w_ref[...], g_ref[...], m_ref[...], v_ref[...])

    return pl.pallas_call(body, name="adam_small", out_shape=[jax.ShapeDtypeStruct(w.shape, F32)] * 3, compiler_params=_params())(w, g, m, v)


def _position():
    return lax.axis_index("x"), lax.axis_index("y"), lax.axis_index("c")


def _other_chips(x, y):
    return [(1 - x, y), (x, 1 - y), (1 - x, 1 - y)]


def _block_index(dev, interleaved):
    x, y, c = dev
    return 4 * y + 2 * c + x if interleaved else 4 * x + 2 * y + c


def _shard_of(ref, axis, size, dev, interleaved=False):
    start = pl.multiple_of(_block_index(dev, interleaved) * size, 128 if axis == 1 else 16)
    return ref.at[:, pl.ds(start, size)] if axis == 1 else ref.at[pl.ds(start, size), :]


def _add_blocks(ids, grad, landed, axis, size, targets, out_dtype, name):
    rows = size if axis == 0 else grad.shape[0]
    cols = size if axis == 1 else grad.shape[1]
    tr = _tile(rows, (256, 176))
    nr = rows // tr
    nt = len(targets)

    def body(ids_ref, g_ref, l_ref, o_ref):
        o_ref[...] = (g_ref[...] + l_ref[...]).astype(out_dtype)

    if axis == 1:
        g_spec = pl.BlockSpec((tr, cols), lambda k, i, ids: (i, ids[targets[0] + k]))
    else:
        g_spec = pl.BlockSpec((tr, cols), lambda k, i, ids: (ids[targets[0] + k] * nr + i, 0))
    return pl.pallas_call(
        body,
        name=name,
        grid_spec=pltpu.PrefetchScalarGridSpec(
            num_scalar_prefetch=1,
            grid=(nt, nr),
            in_specs=[g_spec, pl.BlockSpec((None, tr, cols), lambda k, i, ids: (ids[4 + targets[0] + k], i, 0))],
            out_specs=pl.BlockSpec((None, tr, cols), lambda k, i, ids: (k, i, 0)),
        ),
        out_shape=jax.ShapeDtypeStruct((nt, rows, cols), out_dtype),
        compiler_params=_params("parallel", "parallel"),
    )(ids, grad, landed)


def _all_reduce_small(vec, name):
    rows = vec.shape[0]

    def body(v_ref, o_ref, land, send_sems, recv_sems):
        x, y, c = _position()
        mine = 4 * x + 2 * y + c
        copies = []
        for mask in range(1, N_DEV):
            peer = (1 - x if mask & 4 else x, 1 - y if mask & 2 else y, 1 - c if mask & 1 else c)
            copies.append(
                pltpu.make_async_remote_copy(
                    src_ref=v_ref, dst_ref=land.at[mine], send_sem=send_sems.at[mask - 1], recv_sem=recv_sems.at[mask - 1], device_id=peer, device_id_type=MESH
                )
            )
        for cp in copies:
            cp.start()
        land[mine] = v_ref[...]
        for cp in copies:
            cp.wait()
        acc = land[0]
        for k in range(1, N_DEV):
            acc = acc + land[k]
        o_ref[...] = acc

    return pl.pallas_call(
        body,
        name=name,
        out_shape=jax.ShapeDtypeStruct(vec.shape, F32),
        in_specs=[pl.BlockSpec(memory_space=pltpu.VMEM)],
        out_specs=pl.BlockSpec(memory_space=pltpu.VMEM),
        scratch_shapes=[pltpu.VMEM((N_DEV, rows, 128), F32), pltpu.SemaphoreType.DMA((N_DEV - 1,)), pltpu.SemaphoreType.DMA((N_DEV - 1,))],
    )(vec)


def _rows128(a, rows):
    flat = a.reshape(-1)
    return jnp.pad(flat, (0, rows * 128 - flat.shape[0])).reshape(rows, 128)


def _pad_rel(a):
    return jnp.pad(a.reshape(ATT_HEADS, -1)[:, :N_REL], ((0, 0), (0, N_REL_PAD - N_REL)))


SMALL = [("norm_mix", 16), ("lb_logits", 16), ("hg_norm", 8), ("rel_bias", 24), ("norm_ffn", 16), ("conv_b", 88), ("norm_ple", 16), ("final_norm", 16)]
CONV_W_FULL_ROWS = 3 * 2 * D_FF // 128
CONV_W_SHARD_ROWS = 40


def _pack_small(parts):
    return jnp.concatenate([_rows128(_pad_rel(parts[k]) if k == "rel_bias" else parts[k], rows) for k, rows in SMALL], axis=0)


def _unpack_small(packed, shapes):
    out, at = {}, 0
    for k, rows in SMALL:
        blk = packed[at : at + rows]
        at += rows
        if k == "rel_bias":
            out[k] = blk.reshape(ATT_HEADS, N_REL_PAD)[:, :N_REL].reshape(shapes[k])
        else:
            n = 1
            for s in shapes[k]:
                n *= s
            out[k] = blk.reshape(-1)[:n].reshape(shapes[k])
    return out, at


BIG = [("w_in", 1), ("w_out", 0), ("w_up", 1), ("w_down", 0), ("w_ple_gate", 0), ("w_ple_proj", 1)]


HBM = pl.BlockSpec(memory_space=pltpu.HBM)
SEM = pl.BlockSpec(memory_space=pltpu.SEMAPHORE)
EFFECT = pltpu.SideEffectType.DATAFLOW_SIDE_EFFECTING


def _copies(plan, refs, send_sems, recv_sems):
    return [
        pltpu.make_async_remote_copy(src_ref=src, dst_ref=dst, send_sem=send_sems.at[i], recv_sem=recv_sems.at[i], device_id=dev, device_id_type=MESH)
        for i, (src, dst, dev) in enumerate(plan(refs))
    ]


def _split_start(name, arrays, plan, n):
    k = len(arrays)

    def body(*refs):
        for cp in _copies(plan, refs[:k], refs[k], refs[k + 1]):
            cp.start()
        refs[-1][...] = jnp.zeros_like(refs[-1])

    out = pl.pallas_call(
        body,
        name=name,
        out_shape=(pltpu.SemaphoreType.DMA((n,)), pltpu.SemaphoreType.DMA((n,)), *[pltpu.HBM(a.shape, a.dtype) for a in arrays], jax.ShapeDtypeStruct((8, 128), F32)),
        in_specs=[HBM] * k,
        out_specs=(SEM, SEM, *[HBM] * k, pl.BlockSpec(memory_space=pltpu.VMEM)),
        input_output_aliases={i: 2 + i for i in range(k)},
        compiler_params=pltpu.CompilerParams(has_side_effects=EFFECT),
    )(*[pltpu.with_memory_space_constraint(a, pltpu.HBM) for a in arrays])
    return out[0], out[1], list(out[2 : 2 + k]), out[-1]


def _split_wait(name, send, recv, arrays, plan, after):
    k = len(arrays)

    def body(*refs):
        for cp in _copies(plan, refs[:k], refs[k], refs[k + 1]):
            cp.wait_send()
            cp.wait_recv()

    out = pl.pallas_call(
        body,
        name=name,
        out_shape=tuple(pltpu.HBM(a.shape, a.dtype) for a in arrays),
        in_specs=[HBM] * k + [SEM, SEM, ANY],
        out_specs=tuple([HBM] * k),
        input_output_aliases={i: i for i in range(k)},
        compiler_params=pltpu.CompilerParams(has_side_effects=EFFECT),
    )(*arrays, send, recv, after)
    return list(out)


def _cast_into(w, me, axis, name, dep, dtype):
    r, c = w.shape
    tr = _tile(r, (256, 176))
    nr = r // tr
    deps = [] if dep is None else [dep]

    def body(me_ref, w_ref, *rest):
        rest[-1][...] = w_ref[...].astype(dtype)

    if axis == 1:
        shape, o_spec = (r, N_DEV * c), pl.BlockSpec((tr, c), lambda i, me: (i, me[0]))
    else:
        shape, o_spec = (N_DEV * r, c), pl.BlockSpec((tr, c), lambda i, me: (me[0] * nr + i, 0))
    return pl.pallas_call(
        body,
        name=name,
        grid_spec=pltpu.PrefetchScalarGridSpec(
            num_scalar_prefetch=1, grid=(nr,), in_specs=[pl.BlockSpec((tr, c), lambda i, me: (i, 0))] + [ANY] * len(deps), out_specs=o_spec
        ),
        out_shape=jax.ShapeDtypeStruct(shape, dtype),
        compiler_params=_params("parallel"),
    )(me, w, *deps)


GATHER = [
    (["w_in"], None, "norm_mix_fwd", "bias_table"),
    (["w_out"], "norm_mix_fwd", "att_fwd", None),
    (["w_up", "conv_w"], "norm_mix_fwd", "att_fwd", "norm_ffn_fwd"),
    (["w_down", "w_ple_gate", "w_ple_proj"], "att_fwd", "up_proj", "ffn_act_fwd"),
]
GROUPS = [["w_ple_proj", "w_ple_gate", "w_down"], ["w_up"], ["w_out"], ["w_in"]]
STAGES = ["ffn_act_bwd", "d_mix_out", "hgrn_bwd", "d_norm_mix_out"]
INTERLEAVED = {"w_up", "conv_w"}


class _Exchange:
    def __init__(self, big, conv_w, position):
        self.big, self.axis = big, dict(BIG, conv_w=1)
        self.shards = dict(big, conv_w=conv_w)
        self.size = {k: w.shape[self.axis[k]] for k, w in self.shards.items()}
        self.x, self.y, self.c = position
        chips = [(self.x, self.y)] + _other_chips(self.x, self.y)
        landed = [2 * cx + cy for cx, cy in chips]
        self.ids = {
            flag: jnp.stack([_block_index((cx, cy, self.c), flag) for cx, cy in chips] + landed).astype(jnp.int32) for flag in (False, True)
        }
        self.tokens, self.grads, self.state, self.wfull = [], {}, {}, {}


    def _slot(self, ref, k, dev):
        return _shard_of(ref, self.axis[k], self.size[k], dev, interleaved=k in INTERLEAVED)

    def _plan_gather(self, names, direct, refs):
        x, y, c = _position()
        me, out = (x, y, c), []
        for k, ref in zip(names, refs):
            mine = self._slot(ref, k, me)
            out.append((mine, mine, (x, y, 1 - c)))
            out += [(mine, mine, (*chip, c)) for chip in _other_chips(x, y)]
            if direct:
                out += [(mine, mine, (*chip, 1 - c)) for chip in _other_chips(x, y)]
        return out

    def _plan_forward(self, names, refs):
        x, y, c = _position()
        out = []
        for k, ref in zip(names, refs):
            for chip in _other_chips(x, y):
                block = self._slot(ref, k, (*chip, c))
                out.append((block, block, (x, y, 1 - c)))
        return out

    def _plan_sibling(self, names, refs):
        x, y, c = _position()
        n = len(names)
        return [(self._slot(refs[i], k, (p // 2, p % 2, 1 - c)), refs[n + i].at[p], (x, y, 1 - c)) for i, k in enumerate(names) for p in range(4)]

    def _plan_chips(self, names, refs):
        x, y, c = _position()
        n = len(names)
        return [(refs[i].at[j], refs[n + i].at[j], (*chip, c)) for i in range(n) for j, chip in enumerate(_other_chips(x, y))]


    def gather(self):
        me = {flag: _block_index((self.x, self.y, self.c), flag).astype(jnp.int32).reshape(1) for flag in (False, True)}
        self.late, self.unsent = {}, {}
        after = None
        for gi, (names, issued, *_) in enumerate(GATHER):
            self.unsent[gi] = [
                _cast_into(self.shards[k], me[k in INTERLEAVED], self.axis[k], "cast_" + k, after, F32 if k == "conv_w" else BF16) for k in names
            ]
            if issued is None:
                self._issue(None)
                after = self.tokens[-1]
        self.tokens += [a for arrays in self.unsent.values() for a in arrays]

    def _issue(self, stage):
        for gi, (names, issued, _, forwarded) in enumerate(GATHER):
            if issued == stage and gi in self.unsent:
                plan = functools.partial(self._plan_gather, names, forwarded is None)
                copies = (7 if forwarded is None else 4) * len(names)
                send, recv, fulls, token = _split_start(f"gather_start_{gi}", self.unsent.pop(gi), plan, copies)
                self.tokens.append(token)
                self.late[gi] = (send, recv, fulls, plan)

    def weight(self, k):
        return self.wfull[k]

    def dep(self):
        tokens, self.tokens = self.tokens, []
        return tokens

    def reduce(self, vec, name):
        return _all_reduce_small(vec, name)

    def grad(self, k, g):
        self.grads[k] = g
        for gi, names in enumerate(GROUPS):
            if k == names[-1]:
                plan = functools.partial(self._plan_sibling, names)
                lands = [lax.empty((4, *self._shard_shape(n)), F32) for n in names]
                send, recv, arrays, token = _split_start(f"sibling_start_{gi}", [self.grads[n] for n in names] + lands, plan, 4 * len(names))
                self.tokens.append(token)
                self.state[gi] = (send, recv, arrays, plan)

    def done(self, stage, after):
        for gi, (names, _, _, forwarded) in enumerate(GATHER):
            if forwarded == stage:
                send, recv, fulls, plan = self.late[gi]
                self.wfull.update(zip(names, _split_wait(f"forward_wait_{gi}", send, recv, fulls, plan, after)))
        for gi, (names, _, arrived, forwarded) in enumerate(GATHER):
            if arrived == stage:
                send, recv, fulls, plan = self.late[gi]
                fulls = _split_wait(f"gather_wait_{gi}", send, recv, fulls, plan, after)
                if forwarded is None:
                    self.wfull.update(zip(names, fulls))
                else:
                    plan = functools.partial(self._plan_forward, names)
                    send, recv, fulls, token = _split_start(f"forward_start_{gi}", fulls, plan, 3 * len(names))
                    self.tokens.append(token)
                    self.late[gi] = (send, recv, fulls, plan)
        self._issue(stage)
        if stage in STAGES:
            self._to_chips(STAGES.index(stage), after)

    def _shard_shape(self, k):
        shape = list(self.grads[k].shape)
        shape[self.axis[k]] = self.size[k]
        return tuple(shape)

    def _to_chips(self, gi, after):
        names = GROUPS[gi]
        n = len(names)
        send, recv, arrays, plan = self.state[gi]
        arrays = _split_wait(f"sibling_wait_{gi}", send, recv, arrays, plan, after)
        own, parts = [], []
        for k, g, land in zip(names, arrays[:n], arrays[n:]):
            ids = self.ids[k in INTERLEAVED]
            own.append(_add_blocks(ids, g, land, self.axis[k], self.size[k], [0], F32, "add_own_" + k)[0])
            parts.append(_add_blocks(ids, g, land, self.axis[k], self.size[k], [1, 2, 3], BF16, "add_send_" + k))
        plan = functools.partial(self._plan_chips, names)
        lands = [lax.empty(part.shape, BF16) for part in parts]
        send, recv, arrays, token = _split_start(f"chips_start_{gi}", parts + lands, plan, 3 * n)
        self.tokens.append(token)
        self.state[gi] = (send, recv, arrays, plan, own)

    def finish(self, gi, after):
        names = GROUPS[gi]
        send, recv, arrays, plan, own = self.state[gi]
        arrays = _split_wait(f"chips_wait_{gi}", send, recv, arrays, plan, after)
        return {k: (o, r) for k, o, r in zip(names, own, arrays[len(names) :])}


class _Resident:
    def __init__(self, wfull):
        self.wfull, self.grads = wfull, {}

    def weight(self, k):
        return self.wfull[k]

    def grad(self, k, g):
        self.grads[k] = g

    def dep(self):
        return None

    def reduce(self, vec, name):
        return vec

    def done(self, stage, after):
        pass


def _local_step(x, p, target, small, ex):
    bias = _bias_table(jnp.pad(small["rel_bias"], ((0, 0), (0, N_REL_PAD - N_REL))), dep=ex.dep())
    a1, r1 = _rms_fwd(x, small["norm_mix"], "norm_mix_fwd", dep=[bias])
    ex.done("norm_mix_fwd", a1)
    ex.done("bias_table", a1)
    proj = _matmul(a1, ex.weight("w_in"), "nn", F32, "in_proj", dep=ex.dep())
    y_hg, o_hg, states = _hgrn_fwd(proj, small["lb_logits"], small["hg_norm"])
    ycat = _att_fwd(proj, bias, y_hg, dep=ex.dep())
    ex.done("att_fwd", ycat)
    h1 = _matmul(ycat, ex.weight("w_out"), "nn", F32, "out_proj", resid=x, dep=ex.dep())
    a2, r2 = _rms_fwd(h1, small["norm_ffn"], "norm_ffn_fwd")
    ex.done("norm_ffn_fwd", a2)
    conv_w = ex.weight("conv_w")
    u = _matmul(a2, ex.weight("w_up"), "nn", BF16, "up_proj")
    conv_b = _interleave_cols(small["conv_b"])
    ex.done("up_proj", u)
    z = _ffn_act_fwd(u, conv_w, conv_b)
    ex.done("ffn_act_fwd", z)
    h2 = _matmul(z, ex.weight("w_down"), "nn", F32, "down_proj", tk=2816, resid=h1)
    a3, r3 = _rms_fwd(h2, small["norm_ple"], "norm_ple_fwd")
    gpre = _matmul(a3, ex.weight("w_ple_gate"), "nn", F32, "ple_gate")
    pp = _matmul(p, ex.weight("w_ple_proj"), "nn", F32, "ple_proj")
    dh3, dgpre, dpp, d_final, loss = _ple_loss(gpre, pp, h2, small["final_norm"], target)

    ex.grad("w_ple_proj", _matmul(p, dpp, "tn", F32, "d_w_ple_proj", tm=512))
    ex.grad("w_ple_gate", _matmul(a3, dgpre, "tn", F32, "d_w_ple_gate", tm=512))
    da3 = _matmul(dgpre, ex.weight("w_ple_gate"), "nt", F32, "d_norm_ple_out")
    dh2, d_ple = _rms_bwd(da3, h2, r3, small["norm_ple"], dh3, "norm_ple_bwd")
    dz = _matmul(dh2, ex.weight("w_down"), "nt", BF16, "d_ffn_act")
    ex.grad("w_down", _matmul(z, dh2, "tn", F32, "d_w_down", tm=512))
    du, dcw, dcb = _ffn_act_bwd(u, dz, conv_w, conv_b, dep=ex.dep())
    ex.done("ffn_act_bwd", du)
    d_conv_w, d_conv_b = _deinterleave_cols(dcw), _deinterleave_cols(dcb)
    ex.grad("w_up", _matmul(a2, du, "tn", F32, "d_w_up", tm=512, dep=ex.dep()))
    da2 = _matmul(du, ex.weight("w_up"), "nt", F32, "d_norm_ffn_out", tk=2816, dep=ex.dep())
    dh1, d_ffn = _rms_bwd(da2, h1, r2, small["norm_ffn"], dh2, "norm_ffn_bwd")
    dycat = _matmul(dh1, ex.weight("w_out"), "nt", F32, "d_mix_out")
    ex.done("d_mix_out", dycat)
    ex.grad("w_out", _matmul(ycat, dh1, "tn", F32, "d_w_out", tk=2048, dep=ex.dep()))
    dp_hg, d_lb, d_hgn = _hgrn_bwd(proj, small["lb_logits"], small["hg_norm"], o_hg, dycat, states, dep=ex.dep())
    ex.done("hgrn_bwd", d_lb)
    dproj, dk_att, dv_att, gsum = _att_bwd(proj, bias, dycat, dp_hg, dep=ex.dep())
    d_rel = _rel_bias_grad(gsum)
    d_small = {
        "norm_mix": jnp.zeros_like(small["norm_mix"]), "lb_logits": d_lb, "hg_norm": d_hgn, "rel_bias": d_rel, "norm_ffn": d_ffn,
        "conv_b": d_conv_b, "norm_ple": d_ple, "final_norm": d_final,
    }
    packed = jnp.concatenate([_pack_small(d_small), _rows128(d_conv_w, CONV_W_FULL_ROWS), _rows128(loss[0:1, 0:1], 8)], axis=0)
    early = ex.reduce(packed, "all_reduce_small")
    for k, part in enumerate((dk_att, dv_att)):
        dproj = lax.dynamic_update_slice(dproj, part, (0, 4 * HG_WIDTH + (k + 1) * ATT_WIDTH))
    ex.grad("w_in", _matmul(a1, dproj, "tn", F32, "d_w_in", tm=512, dep=[early]))
    da1 = _matmul(dproj, ex.weight("w_in"), "nt", F32, "d_norm_mix_out", tk=1792, dep=ex.dep())
    dx, d_mix = _rms_bwd(da1, x, r1, small["norm_mix"], dh1, "norm_mix_bwd")
    rows = dict(SMALL)["norm_mix"]
    late = ex.reduce(_rows128(d_mix, rows), "all_reduce_norm_mix")
    ex.done("d_norm_mix_out", late)
    return dx, jnp.concatenate([late, early[rows:]], axis=0)


def kernel(x, p, norm_mix, w_in, lb_logits, hg_norm, rel_bias, w_out, norm_ffn, w_up, conv_w, conv_b, w_down, norm_ple, w_ple_gate, w_ple_proj, final_norm, loss_target, m_norm_mix, m_w_in, m_lb_logits, m_hg_norm, m_rel_bias, m_w_out, m_norm_ffn, m_w_up, m_conv_w, m_conv_b, m_w_down, m_norm_ple, m_w_ple_gate, m_w_ple_proj, m_final_norm, v_norm_mix, v_w_in, v_lb_logits, v_hg_norm, v_rel_bias, v_w_out, v_norm_ffn, v_w_up, v_conv_w, v_conv_b, v_w_down, v_norm_ple, v_w_ple_gate, v_w_ple_proj, v_final_norm):
    given = dict(locals())
    mx, my, mc = _position()
    me = 4 * mx + 2 * my + mc
    big = {k: given[k][0] for k, _ in BIG}
    ex = _Exchange(big, conv_w[0], (mx, my, mc))
    ex.gather()

    small = {
        "norm_mix": norm_mix, "lb_logits": lb_logits, "hg_norm": hg_norm, "rel_bias": rel_bias[0], "norm_ffn": norm_ffn,
        "conv_b": conv_b, "norm_ple": norm_ple, "final_norm": final_norm.reshape(1, -1),
    }
    dx, reduced = _local_step(x[0], p[0, 0], loss_target[0], small, ex)

    out = {}
    shapes = {k: given[k].shape for k, _ in SMALL}
    g_small, at = _unpack_small(reduced, shapes)
    g_conv_full = reduced[at : at + CONV_W_FULL_ROWS].reshape(3, 2 * D_FF)
    total_loss = reduced[at + CONV_W_FULL_ROWS, 0]
    cw = conv_w.shape[2]
    g_conv = lax.dynamic_slice_in_dim(g_conv_full, me * cw, cw, axis=1)

    def pack_with_conv(parts, conv_part):
        return jnp.concatenate([_pack_small(parts), _rows128(conv_part, CONV_W_SHARD_ROWS)], axis=0)

    d_pk, m_pk, v_pk = _adam_small(
        pack_with_conv({k: given[k] for k, _ in SMALL}, conv_w),
        pack_with_conv(g_small, g_conv),
        pack_with_conv({k: given["m_" + k] for k, _ in SMALL}, m_conv_w),
        pack_with_conv({k: given["v_" + k] for k, _ in SMALL}, v_conv_w),
    )
    for name, pk in (("d", d_pk), ("m", m_pk), ("v", v_pk)):
        parts, at = _unpack_small(pk, shapes)
        parts["conv_w"] = pk[at : at + CONV_W_SHARD_ROWS].reshape(-1)[: 3 * cw].reshape(conv_w.shape)
        for k, a in parts.items():
            out.setdefault(k, {})
            out[k][name] = a
    for k, _ in SMALL:
        out[k]["g"] = g_small[k]
    out["conv_w"]["g"] = g_conv.reshape(conv_w.shape)

    after, started = v_pk, ex.dep()
    for gi in range(len(GROUPS)):
        for k, (o, r) in ex.finish(gi, after).items():
            g, d, nm, nv = _adam_big(big[k], given["m_" + k][0], given["v_" + k][0], o, r, "adam_" + k, dep=started)
            out[k] = tuple(a[None] for a in (g, d, nm, nv))
            after = nv

    order = ["norm_mix", "w_in", "lb_logits", "hg_norm", "rel_bias", "w_out", "norm_ffn", "w_up", "conv_w", "conv_b", "w_down", "norm_ple", "w_ple_gate", "w_ple_proj", "final_norm"]

    def pick(k, what):
        return out[k][what] if isinstance(out[k], dict) else out[k][{"g": 0, "d": 1, "m": 2, "v": 3}[what]]

    return (total_loss, dx[None], *[pick(k, "g") for k in order], *[pick(k, "d") for k in order], *[pick(k, "m") for k in order], *[pick(k, "v") for k in order])
```

```python
import functools

import jax
import jax.numpy as jnp
from jax import lax
from jax.experimental import pallas as pl
from jax.experimental.pallas import tpu as pltpu

F32 = jnp.float32
BF16 = jnp.bfloat16

D_MODEL = 2048
CHUNK = 64
HG_HEADS = 8
HEAD_DIM = 128
HG_WIDTH = HG_HEADS * HEAD_DIM
ATT_HEADS = 8
ATT_WIDTH = ATT_HEADS * HEAD_DIM
LEFT_CHUNKS = 8
PAD = LEFT_CHUNKS * CHUNK
BAND = PAD + CHUNK
REL_CLIP = 128
N_REL = 2 * REL_CLIP + 1
N_REL_PAD = 384
D_FF = 5632
EPS = 1e-6
ATT_SCALE = HEAD_DIM ** -0.5
SUB = 32
HG_BLOCK = 8
Q_BLOCK = 4 * CHUNK
K_BLOCK = Q_BLOCK + PAD
DIAG = 1024
ATT_BLOCK = 2
MASKED = -1e30

ADAM_LR = 0.001
ADAM_B1 = 0.9
ADAM_B2 = 0.999
ADAM_EPS = 1e-08
ADAM_WD = 0.01
ADAM_STEP = 10

N_DEV = 8
VMEM_LIMIT = 48 * 1024 * 1024
ATT_BWD_VMEM = 58 * 1024 * 1024
MESH = pl.DeviceIdType.MESH
ANY = pl.BlockSpec(memory_space=pl.ANY)
HIGHEST = lax.Precision.HIGHEST

NN = (((1,), (0,)), ((), ()))
NT = (((1,), (1,)), ((), ()))
TN = (((0,), (0,)), ((), ()))


def _params(*sem):
    return pltpu.CompilerParams(dimension_semantics=sem if sem else None, vmem_limit_bytes=VMEM_LIMIT)


def _pallas(body, n_in, dep, **kw):
    deps = [] if dep is None else list(dep)
    if not deps:
        return pl.pallas_call(body, **kw)

    def body_after(*refs):
        body(*refs[:n_in], *refs[n_in + len(deps) :])

    call = pl.pallas_call(body_after, **dict(kw, in_specs=list(kw["in_specs"]) + [ANY] * len(deps)))
    return lambda *ops: call(*ops, *deps)


def _dot(a, b, dims=NN):
    return lax.dot_general(a, b, dims, preferred_element_type=F32)


def _dot3(a, b, dims=NN):
    a_hi, b_hi = a.astype(BF16), b.astype(BF16)
    a_lo, b_lo = (a - a_hi.astype(F32)).astype(BF16), (b - b_hi.astype(F32)).astype(BF16)
    return _dot(a_hi, b_hi, dims) + (_dot(a_hi, b_lo, dims) + _dot(a_lo, b_hi, dims))


def _sigmoid(x):
    return 1.0 / (1.0 + jnp.exp(-x))


def _tile(n, prefs):
    for t in prefs:
        if n % t == 0:
            return t
    return n


def _matmul(a, b, mode, out_dtype, name, tm=1024, tn=1024, tk=None, resid=None, dep=None):
    if mode == "nn":
        (m, k), n = a.shape, b.shape[1]
    elif mode == "nt":
        (m, k), n = a.shape, b.shape[0]
    else:
        (k, m), n = a.shape, b.shape[1]
    tm = _tile(m, (tm, 512, 256, 128))
    tn = _tile(n, (tn, 1408, 512, 256, 128))
    tk = k if tk is None else _tile(k, (tk,))
    nk = k // tk
    dims = {"nn": NN, "nt": NT, "tn": TN}[mode]
    a_spec = pl.BlockSpec((tk, tm), lambda i, j, s: (s, i)) if mode == "tn" else pl.BlockSpec((tm, tk), lambda i, j, s: (i, s))
    b_spec = pl.BlockSpec((tn, tk), lambda i, j, s: (j, s)) if mode == "nt" else pl.BlockSpec((tk, tn), lambda i, j, s: (s, j))
    o_spec = pl.BlockSpec((tm, tn), lambda i, j, s: (i, j))
    has_res = resid is not None

    def body(*refs):
        a_ref, b_ref = refs[0], refs[1]
        o_ref = refs[2 + has_res]
        part = _dot(a_ref[...].astype(BF16), b_ref[...].astype(BF16), dims)

        def finish(acc):
            if has_res:
                acc = acc + refs[2][...]
            o_ref[...] = acc.astype(out_dtype)

        if nk == 1:
            finish(part)
        else:
            acc_ref = refs[-1]
            s = pl.program_id(2)

            @pl.when(s == 0)
            def _():
                acc_ref[...] = part

            @pl.when(s > 0)
            def _():
                acc_ref[...] += part

            @pl.when(s == nk - 1)
            def _():
                finish(acc_ref[...])

    return _pallas(
        body,
        2 + has_res,
        dep,
        name=name,
        grid=(m // tm, n // tn, nk),
        in_specs=[a_spec, b_spec] + ([o_spec] if has_res else []),
        out_specs=o_spec,
        out_shape=jax.ShapeDtypeStruct((m, n), out_dtype),
        scratch_shapes=[pltpu.VMEM((tm, tn), F32)] if nk > 1 else [],
        compiler_params=_params("parallel", "parallel", "arbitrary"),
    )(*([a, b] + ([resid] if has_res else [])))


def _rms_fwd(x, g, name, dep=None):
    t, d = x.shape
    tm = _tile(t, (256,))

    def body(x_ref, g_ref, a_ref, r_ref):
        xv = x_ref[...]
        r = lax.rsqrt(jnp.mean(xv * xv, axis=-1, keepdims=True) + EPS)
        a_ref[...] = (xv * r * g_ref[...]).astype(BF16)
        r_ref[...] = r

    row = pl.BlockSpec((tm, d), lambda i: (i, 0))
    return _pallas(
        body,
        2,
        dep,
        name=name,
        grid=(t // tm,),
        in_specs=[row, pl.BlockSpec((1, d), lambda i: (0, 0))],
        out_specs=[row, pl.BlockSpec((tm, 1), lambda i: (i, 0))],
        out_shape=[jax.ShapeDtypeStruct((t, d), BF16), jax.ShapeDtypeStruct((t, 1), F32)],
        compiler_params=_params("parallel"),
    )(x, g)


def _rms_bwd(da, x, r, g, resid, name, dep=None):
    t, d = x.shape
    tm = _tile(t, (256,))

    def body(da_ref, x_ref, r_ref, g_ref, res_ref, dx_ref, dxb_ref, dg_ref):
        i = pl.program_id(0)
        rv = r_ref[...]
        n = x_ref[...] * rv
        dav = da_ref[...]
        dn = dav * g_ref[...]
        dx = rv * (dn - n * jnp.mean(dn * n, axis=-1, keepdims=True)) + res_ref[...]
        dx_ref[...] = dx
        dxb_ref[...] = dx.astype(BF16)
        part = jnp.sum(dav * n, axis=0, keepdims=True)

        @pl.when(i == 0)
        def _():
            dg_ref[...] = part

        @pl.when(i > 0)
        def _():
            dg_ref[...] += part

    row = pl.BlockSpec((tm, d), lambda i: (i, 0))
    vec = pl.BlockSpec((1, d), lambda i: (0, 0))
    return _pallas(
        body,
        5,
        dep,
        name=name,
        grid=(t // tm,),
        in_specs=[row, row, pl.BlockSpec((tm, 1), lambda i: (i, 0)), vec, row],
        out_specs=[row, row, vec],
        out_shape=[jax.ShapeDtypeStruct((t, d), F32), jax.ShapeDtypeStruct((t, d), BF16), jax.ShapeDtypeStruct((1, d), F32)],
        compiler_params=_params("arbitrary"),
    )(da, x, r, g, resid)


def _tri(n, upper):
    r = lax.broadcasted_iota(jnp.int32, (n, n), 0)
    c = lax.broadcasted_iota(jnp.int32, (n, n), 1)
    return jnp.where((c >= r) if upper else (c <= r), 1.0, 0.0).astype(F32)


def _hgrn_gates(q, fp, lbl):
    l0, l1 = lbl[0:1, :], lbl[1:2, :]
    mx = jnp.maximum(l0, l1)
    e0, e1 = jnp.exp(l0 - mx), jnp.exp(l1 - mx)
    lb = e0 / (e0 + e1)
    sig = _sigmoid(fp)
    f = lb + (1.0 - lb) * sig
    kk = (1.0 - lb) * _sigmoid(-fp)
    sq = _sigmoid(q)
    b = jnp.dot(_tri(CHUNK, False), jnp.log(f), precision=HIGHEST, preferred_element_type=F32)
    return lb, sig, f, kk, sq, q * sq, b


def _heads(x):
    return [x[:, j * HEAD_DIM : (j + 1) * HEAD_DIM] for j in range(x.shape[1] // HEAD_DIM)]


def _wide(parts):
    return jnp.concatenate(parts, axis=1)


def _intra_blocks(b):
    out = []
    for lo in range(0, CHUNK, SUB):
        hi = lo + SUB
        br = b[lo + SUB // 2 : lo + SUB // 2 + 1, :]
        row = lax.broadcasted_iota(jnp.int32, (SUB, hi), 0) + lo
        col = lax.broadcasted_iota(jnp.int32, (SUB, hi), 1)
        out.append((lo, hi, jnp.exp(b[lo:hi] - br), jnp.exp(br - b[:hi]), col <= row))
    return out


def _hgrn_fwd(proj, lb_logits, hg_norm):
    t = proj.shape[0]
    nc = t // CHUNK

    def body(q_ref, f_ref, i_ref, g_ref, lbl_ref, hgn_ref, y_ref, o_ref, st_ref, s_scr):
        c = pl.program_id(1)

        @pl.when(c == 0)
        def _():
            s_scr[...] = jnp.zeros_like(s_scr)

        hs = range(HG_BLOCK)
        sts = [s_scr[j] for j in hs]
        _, _, _, kk, _, qf, b = _hgrn_gates(q_ref[...], f_ref[...], lbl_ref[...])
        vb = _heads(i_ref[...].astype(BF16))
        bl = b[CHUNK - 1 : CHUNK, :]
        qe = _heads((qf * jnp.exp(b)).astype(BF16))
        kd = _heads((kk * jnp.exp(bl - b)).astype(BF16))
        decay = _heads(jnp.exp(bl))
        o = [_dot(qe[j], sts[j].astype(BF16), NT) for j in hs]
        parts = [[] for _ in hs]
        for lo, hi, ea, eb, mask in _intra_blocks(b):
            a, bk = _heads((qf[lo:hi] * ea).astype(BF16)), _heads((kk[:hi] * eb).astype(BF16))
            p = [jnp.where(mask, _dot(a[j], bk[j], NT), 0.0).astype(BF16) for j in hs]
            for j in hs:
                parts[j].append(_dot(p[j], vb[j][:hi]))
        o = [o[j] + jnp.concatenate(parts[j], axis=0) for j in hs]
        new = [sts[j] * decay[j] + _dot(vb[j], kd[j], TN) for j in hs]
        hgn = hgn_ref[...]
        on = [o[j] * lax.rsqrt(jnp.mean(o[j] * o[j], axis=-1, keepdims=True) + EPS) * hgn for j in hs]
        gg = g_ref[...]
        for j in hs:
            st_ref[j] = sts[j]
            s_scr[j] = new[j]
        o_ref[...] = _wide(o)
        y_ref[...] = (_wide(on) * (gg * _sigmoid(gg))).astype(BF16)

    wide = HG_BLOCK * HEAD_DIM
    groups = HG_HEADS // HG_BLOCK

    def col(k):
        return pl.BlockSpec((CHUNK, wide), lambda g, c: (c, k * groups + g))

    out = pl.BlockSpec((CHUNK, wide), lambda g, c: (c, g))
    return pl.pallas_call(
        body,
        name="hgrn_fwd",
        grid=(groups, nc),
        in_specs=[col(0), col(1), col(2), col(3), pl.BlockSpec((2, wide), lambda g, c: (0, g)), pl.BlockSpec((1, HEAD_DIM), lambda g, c: (0, 0))],
        out_specs=[out, out, pl.BlockSpec((HG_BLOCK, None, HEAD_DIM, HEAD_DIM), lambda g, c: (g, c, 0, 0))],
        out_shape=[
            jax.ShapeDtypeStruct((t, HG_WIDTH + ATT_WIDTH), BF16),
            jax.ShapeDtypeStruct((t, HG_WIDTH), F32),
            jax.ShapeDtypeStruct((HG_HEADS, nc, HEAD_DIM, HEAD_DIM), F32),
        ],
        scratch_shapes=[pltpu.VMEM((HG_BLOCK, HEAD_DIM, HEAD_DIM), F32)],
        compiler_params=_params("arbitrary", "arbitrary"),
    )(proj, proj, proj, proj, lb_logits, hg_norm)


def _hgrn_bwd(proj, lb_logits, hg_norm, o_hg, dycat, states, dep=None):
    t = proj.shape[0]
    nc = t // CHUNK

    def body(q_ref, f_ref, i_ref, g_ref, lbl_ref, hgn_ref, o_ref, dy_ref, st_ref, dp_ref, dlbl_ref, dhgn_ref, dst_scr, dlb_scr):
        h = pl.program_id(0)
        c = pl.program_id(1)

        @pl.when(c == 0)
        def _():
            dst_scr[...] = jnp.zeros_like(dst_scr)
            dlb_scr[...] = jnp.zeros_like(dlb_scr)

        @pl.when((c == 0) & (h == 0))
        def _():
            dhgn_ref[...] = jnp.zeros_like(dhgn_ref)

        hs = range(HG_BLOCK)
        hgn = _wide([hgn_ref[...]] * HG_BLOCK)
        q, fp, gg, vi = q_ref[...], f_ref[...], g_ref[...], i_ref[...]
        lb, sig, f, kk, sq, qf, b = _hgrn_gates(q, fp, lbl_ref[...])
        o, dy = o_ref[...], dy_ref[...]
        sg = _sigmoid(gg)
        n = _wide([oh * lax.rsqrt(jnp.mean(oh * oh, axis=-1, keepdims=True) + EPS) for oh in _heads(o)])
        don = dy * (gg * sg)
        dgg = dy * (n * hgn) * (sg * (1.0 + gg * (1.0 - sg)))
        d_hgn = sum(_heads(jnp.sum(don * n, axis=0, keepdims=True)))
        dn = don * hgn
        do = _wide(
            [
                lax.rsqrt(jnp.mean(oh * oh, axis=-1, keepdims=True) + EPS) * (dnh - nh * jnp.mean(dnh * nh, axis=-1, keepdims=True))
                for oh, dnh, nh in zip(_heads(o), _heads(dn), _heads(n))
            ]
        )
        sts = [st_ref[j] for j in hs]
        dstn = [dst_scr[j] for j in hs]
        bl = b[CHUNK - 1 : CHUNK, :]
        e_b, e_bl, e_l = jnp.exp(b), jnp.exp(bl - b), jnp.exp(bl)
        doh, vih = _heads(do), _heads(vi)
        dobh = _heads(do.astype(BF16))
        dq_acc = _wide([_dot3(doh[j], sts[j]) for j in hs]) * e_b
        dk_inter = _wide([_dot3(vih[j], dstn[j]) for j in hs]) * e_bl
        dk_acc = dk_inter
        kd = _heads((kk * e_bl).astype(BF16))
        dv_acc = _wide([_dot(kd[j], dstn[j].astype(BF16), NT) for j in hs])
        qe, decay = _heads((qf * e_b).astype(BF16)), _heads(e_l)
        dst_new = [dstn[j] * decay[j] + _dot(dobh[j], qe[j], TN) for j in hs]
        db_last = e_l * _wide([jnp.sum(sts[j] * dstn[j], axis=0, keepdims=True) for j in hs]) + jnp.sum(kk * dk_inter, axis=0, keepdims=True)
        dq_parts = []
        for lo, hi, ea, eb, mask in _intra_blocks(b):
            a, bk = qf[lo:hi] * ea, kk[:hi] * eb
            ah, bkh = _heads(a), _heads(bk)
            abh, bkbh = _heads(a.astype(BF16)), _heads(bk.astype(BF16))
            p = [jnp.where(mask, _dot(abh[j], bkbh[j], NT), 0.0).astype(BF16) for j in hs]
            dp = [jnp.where(mask, _dot3(doh[j][lo:hi], vih[j][:hi], NT), 0.0) for j in hs]
            dq_parts.append(_wide([_dot3(dp[j], bkh[j]) for j in hs]) * ea)
            dki = _wide([_dot3(dp[j], ah[j], TN) for j in hs]) * eb
            dvi = _wide([_dot(p[j], dobh[j][lo:hi], TN) for j in hs])
            if hi < CHUNK:
                zeros = jnp.zeros((CHUNK - hi, HG_BLOCK * HEAD_DIM), F32)
                dki = jnp.concatenate([dki, zeros], axis=0)
                dvi = jnp.concatenate([dvi, zeros], axis=0)
            dk_acc = dk_acc + dki
            dv_acc = dv_acc + dvi
        dq_acc = dq_acc + jnp.concatenate(dq_parts, axis=0)
        rows = lax.broadcasted_iota(jnp.int32, dq_acc.shape, 0)
        db = qf * dq_acc - kk * dk_acc + jnp.where(rows == CHUNK - 1, db_last, 0.0)
        dlf = jnp.dot(_tri(CHUNK, True), db, precision=HIGHEST, preferred_element_type=F32)
        dfk = dlf / f - dk_acc
        for k, part in enumerate((dq_acc * (sq * (1.0 + q * (1.0 - sq))), (1.0 - lb) * dfk * sig * (1.0 - sig), dv_acc, dgg)):
            dp_ref[:, k * HG_WIDTH : (k + 1) * HG_WIDTH] = part.astype(BF16)
        dlb_scr[...] += jnp.sum(dfk * (1.0 - sig), axis=0, keepdims=True)
        dhgn_ref[...] += d_hgn
        for j in hs:
            dst_scr[j] = dst_new[j]

        @pl.when(c == nc - 1)
        def _():
            dl0 = dlb_scr[...] * lb * (1.0 - lb)
            dlbl_ref[0:1, :] = dl0
            dlbl_ref[1:2, :] = -dl0

    wide = HG_BLOCK * HEAD_DIM
    groups = HG_HEADS // HG_BLOCK

    def col(k):
        return pl.BlockSpec((CHUNK, wide), lambda g, c: (nc - 1 - c, k * groups + g))

    blk = pl.BlockSpec((CHUNK, wide), lambda g, c: (nc - 1 - c, g))
    assert groups == 1, "d(q, f, i, g) are written as one contiguous column range of the in_proj gradient"
    return _pallas(
        body,
        9,
        dep,
        name="hgrn_bwd",
        grid=(groups, nc),
        in_specs=[
            col(0), col(1), col(2), col(3),
            pl.BlockSpec((2, wide), lambda g, c: (0, g)),
            pl.BlockSpec((1, HEAD_DIM), lambda g, c: (0, 0)),
            blk, blk,
            pl.BlockSpec((HG_BLOCK, None, HEAD_DIM, HEAD_DIM), lambda g, c: (g, nc - 1 - c, 0, 0)),
        ],
        out_specs=[
            pl.BlockSpec((CHUNK, 4 * HG_WIDTH), lambda g, c: (nc - 1 - c, 0)),
            pl.BlockSpec((2, wide), lambda g, c: (0, g)),
            pl.BlockSpec((1, HEAD_DIM), lambda g, c: (0, 0)),
        ],
        out_shape=[
            jax.ShapeDtypeStruct((t, 4 * HG_WIDTH + 3 * ATT_WIDTH), BF16),
            jax.ShapeDtypeStruct((2, HG_WIDTH), F32),
            jax.ShapeDtypeStruct((1, HEAD_DIM), F32),
        ],
        scratch_shapes=[pltpu.VMEM((HG_BLOCK, HEAD_DIM, HEAD_DIM), F32), pltpu.VMEM((1, wide), F32)],
        compiler_params=_params("arbitrary", "arbitrary"),
    )(proj, proj, proj, proj, lb_logits, hg_norm, o_hg, dycat, states)


def _diagonal_slots(shift):
    i = lax.broadcasted_iota(jnp.int32, (N_REL_PAD, DIAG), 0)
    u = lax.broadcasted_iota(jnp.int32, (N_REL_PAD, DIAG), 1)
    offset = u - shift if shift else jnp.where(u < K_BLOCK, u, u - DIAG)
    return jnp.where(jnp.clip(PAD - offset, -REL_CLIP, REL_CLIP) + REL_CLIP == i, 1.0, 0.0).astype(BF16)


def _split3(x):
    hi = x.astype(BF16)
    mid = (x - hi.astype(F32)).astype(BF16)
    return hi, mid, (x - hi.astype(F32) - mid.astype(F32)).astype(BF16)


def _bias_table(rel_bias, dep=None):
    def body(rb_ref, o_ref, diag):
        h = pl.program_id(0)

        @pl.when(h == 0)
        def _():
            hi, mid, lo = _split3(rb_ref[...])
            slots = _diagonal_slots(0)
            diag[...] = _dot(hi, slots) + (_dot(mid, slots) + _dot(lo, slots))

        rows = jnp.broadcast_to(diag[pl.ds(h, 1), :], (Q_BLOCK, DIAG))
        row = lax.broadcasted_iota(jnp.int32, (Q_BLOCK, K_BLOCK), 0)
        col = lax.broadcasted_iota(jnp.int32, (Q_BLOCK, K_BLOCK), 1)
        first = row - (row & (CHUNK - 1))
        seen = (col >= first) & (col < first + BAND)
        o_ref[...] = jnp.where(seen, pltpu.roll(rows, 0, 1, stride=1, stride_axis=0)[:, :K_BLOCK], MASKED)

    return _pallas(
        body,
        1,
        dep,
        name="bias_table",
        grid=(ATT_HEADS,),
        in_specs=[pl.BlockSpec((ATT_HEADS, N_REL_PAD), lambda h: (0, 0))],
        out_specs=pl.BlockSpec((None, Q_BLOCK, K_BLOCK), lambda h: (h, 0, 0)),
        out_shape=jax.ShapeDtypeStruct((ATT_HEADS, Q_BLOCK, K_BLOCK), F32),
        scratch_shapes=[pltpu.VMEM((ATT_HEADS, DIAG), F32)],
        compiler_params=_params("arbitrary"),
    )(rel_bias)


def _att_probs(q_ref, kpad, bias_ref, blk, cols=slice(None), head=None):
    qs = (q_ref[:, cols] * ATT_SCALE).astype(BF16)
    start = pl.multiple_of(blk * Q_BLOCK, Q_BLOCK)
    kb = kpad[pl.ds(start, K_BLOCK), cols]
    s = _dot(qs, kb, NT) + (bias_ref[...] if head is None else bias_ref[head])
    col = lax.broadcasted_iota(jnp.int32, (Q_BLOCK, K_BLOCK), 1)
    s = jnp.where(col >= PAD - blk * Q_BLOCK, s, MASKED)
    e = jnp.exp(s - jnp.max(s, axis=-1, keepdims=True))
    return qs, kb, start, e * (1.0 / jnp.sum(e, axis=-1, keepdims=True))


def _fill_padded(dst, src):
    dst[0:PAD, :] = jnp.zeros((PAD, dst.shape[1]), BF16)
    dst[PAD:, :] = src[...].astype(BF16)


def _att_fwd(proj, bias, y_mix, dep=None):
    t = proj.shape[0]
    nb = t // Q_BLOCK

    def body(q_ref, k_ref, v_ref, bias_ref, y_in_ref, y_ref, kpad, vpad):
        c = pl.program_id(1)

        @pl.when(c == 0)
        def _():
            _fill_padded(kpad, k_ref)
            _fill_padded(vpad, v_ref)

        for j in range(ATT_BLOCK):
            cols = slice(j * HEAD_DIM, (j + 1) * HEAD_DIM)
            _, _, start, p = _att_probs(q_ref, kpad, bias_ref, c, cols, j)
            y_ref[:, cols] = _dot(p.astype(BF16), vpad[pl.ds(start, K_BLOCK), cols]).astype(BF16)

    wide = ATT_BLOCK * HEAD_DIM
    base = 4 * HG_HEADS // ATT_BLOCK
    groups = ATT_HEADS // ATT_BLOCK
    return _pallas(
        body,
        5,
        dep,
        name="att_fwd",
        grid=(groups, nb),
        in_specs=[
            pl.BlockSpec((Q_BLOCK, wide), lambda g, c: (c, base + g)),
            pl.BlockSpec((t, wide), lambda g, c: (0, base + groups + g)),
            pl.BlockSpec((t, wide), lambda g, c: (0, base + 2 * groups + g)),
            pl.BlockSpec((ATT_BLOCK, Q_BLOCK, K_BLOCK), lambda g, c: (g, 0, 0)),
            ANY,
        ],
        out_specs=pl.BlockSpec((Q_BLOCK, wide), lambda g, c: (c, HG_HEADS // ATT_BLOCK + g)),
        out_shape=jax.ShapeDtypeStruct(y_mix.shape, BF16),
        input_output_aliases={4: 0},
        scratch_shapes=[pltpu.VMEM((t + PAD, wide), BF16), pltpu.VMEM((t + PAD, wide), BF16)],
        compiler_params=_params("arbitrary", "arbitrary"),
    )(proj, proj, proj, bias, y_mix)


def _att_bwd(proj, bias, dycat, dproj, dep=None):
    t = proj.shape[0]
    nb = t // Q_BLOCK

    def body(q_ref, k_ref, v_ref, bias_ref, dy_ref, dp_in_ref, dq_ref, dk_ref, dv_ref, g_ref, kpad, vpad, dkacc, dvacc):
        c = pl.program_id(1)

        @pl.when(c == 0)
        def _():
            _fill_padded(kpad, k_ref)
            _fill_padded(vpad, v_ref)
            dkacc[...] = jnp.zeros_like(dkacc)
            dvacc[...] = jnp.zeros_like(dvacc)
            g_ref[...] = jnp.zeros_like(g_ref)

        for j in range(ATT_BLOCK):
            cols = slice(j * HEAD_DIM, (j + 1) * HEAD_DIM)
            qs, kb, start, p = _att_probs(q_ref, kpad, bias_ref, c, cols, j)
            band = pl.ds(start, K_BLOCK)
            dyb = dy_ref[:, cols].astype(BF16)
            dvacc[band, cols] += _dot(p.astype(BF16), dyb, TN)
            dp = _dot(dyb, vpad[band, cols], NT)
            ds = p * (dp - jnp.sum(dp * p, axis=-1, keepdims=True))
            g_ref[j] += ds
            dsb = ds.astype(BF16)
            dq_ref[:, cols] = (_dot(dsb, kb) * ATT_SCALE).astype(BF16)
            dkacc[band, cols] += _dot(dsb, qs, TN)

        @pl.when(c == nb - 1)
        def _():
            dk_ref[...] = dkacc[PAD:, :].astype(BF16)
            dv_ref[...] = dvacc[PAD:, :].astype(BF16)

    wide = ATT_BLOCK * HEAD_DIM
    base = 4 * HG_HEADS // ATT_BLOCK
    groups = ATT_HEADS // ATT_BLOCK
    whole = pl.BlockSpec((t, wide), lambda g, c: (0, g))
    table = pl.BlockSpec((ATT_BLOCK, Q_BLOCK, K_BLOCK), lambda g, c: (g, 0, 0))
    return _pallas(
        body,
        6,
        dep,
        name="att_bwd",
        grid=(groups, nb),
        in_specs=[
            pl.BlockSpec((Q_BLOCK, wide), lambda g, c: (c, base + g)),
            pl.BlockSpec((t, wide), lambda g, c: (0, base + groups + g)),
            pl.BlockSpec((t, wide), lambda g, c: (0, base + 2 * groups + g)),
            table,
            pl.BlockSpec((Q_BLOCK, wide), lambda g, c: (c, HG_HEADS // ATT_BLOCK + g)),
            ANY,
        ],
        out_specs=[pl.BlockSpec((Q_BLOCK, wide), lambda g, c: (c, base + g)), whole, whole, table],
        input_output_aliases={5: 0},
        out_shape=[
            jax.ShapeDtypeStruct(dproj.shape, BF16),
            jax.ShapeDtypeStruct((t, ATT_WIDTH), BF16),
            jax.ShapeDtypeStruct((t, ATT_WIDTH), BF16),
            jax.ShapeDtypeStruct((ATT_HEADS, Q_BLOCK, K_BLOCK), F32),
        ],
        scratch_shapes=[
            pltpu.VMEM((t + PAD, wide), BF16),
            pltpu.VMEM((t + PAD, wide), BF16),
            pltpu.VMEM((t + PAD, wide), F32),
            pltpu.VMEM((t + PAD, wide), F32),
        ],
        compiler_params=pltpu.CompilerParams(dimension_semantics=("arbitrary", "arbitrary"), vmem_limit_bytes=ATT_BWD_VMEM),
    )(proj, proj, proj, bias, dycat, dproj)


def _rel_bias_grad(gsum):
    def body(g_ref, o_ref):
        r = lax.broadcasted_iota(jnp.int32, (Q_BLOCK, Q_BLOCK), 0)
        c = lax.broadcasted_iota(jnp.int32, (Q_BLOCK, Q_BLOCK), 1)
        flip = jnp.where(r + c == Q_BLOCK - 1, 1.0, 0.0).astype(BF16)
        sums = []
        for h in range(ATT_HEADS):
            hi, mid, lo = _split3(g_ref[h])
            rev = _dot(flip, hi) + (_dot(flip, mid) + _dot(flip, lo))
            wide = jnp.concatenate([rev, jnp.zeros((Q_BLOCK, DIAG - K_BLOCK), F32)], axis=1)
            sums.append(jnp.sum(pltpu.roll(wide, 0, 1, stride=1, stride_axis=0), axis=0, keepdims=True))
        hi, mid, lo = _split3(jnp.concatenate(sums, axis=0))
        slots = _diagonal_slots(Q_BLOCK - 1)
        o_ref[...] = _dot(hi, slots, NT) + (_dot(mid, slots, NT) + _dot(lo, slots, NT))

    return pl.pallas_call(
        body,
        name="rel_bias_grad",
        out_shape=jax.ShapeDtypeStruct((ATT_HEADS, N_REL_PAD), F32),
        compiler_params=_params(),
    )(gsum)


HALO = 16


FF_TILE = 1408
FF_TILES = D_FF // FF_TILE


def _interleave_cols(a):
    lead = a.shape[:-1]
    return jnp.swapaxes(a.reshape(*lead, 2, FF_TILES, FF_TILE), -3, -2).reshape(*lead, 2 * D_FF)


def _deinterleave_cols(a):
    lead = a.shape[:-1]
    return jnp.swapaxes(a.reshape(*lead, FF_TILES, 2, FF_TILE), -3, -2).reshape(*lead, 2 * D_FF)


def _ffn_specs(t, tm):
    wide = 2 * FF_TILE
    tile = pl.BlockSpec((tm, wide), lambda j, i: (i, j))
    before = pl.BlockSpec((HALO, wide), lambda j, i: (jnp.maximum(i * (tm // HALO) - 1, 0), j))
    after = pl.BlockSpec((HALO, wide), lambda j, i: (jnp.minimum((i + 1) * (tm // HALO), t // HALO - 1), j))
    vec = lambda rows: pl.BlockSpec((rows, wide), lambda j, i: (0, j))
    return tile, before, after, vec


def _shifted(x, rows, offsets):
    r = lax.broadcasted_iota(jnp.int32, (rows, x.shape[0]), 0)
    c = lax.broadcasted_iota(jnp.int32, (rows, x.shape[0]), 1)
    pick = jnp.concatenate([jnp.where(c == r + o, 1.0, 0.0).astype(BF16) for o in offsets], axis=0)
    out = _dot(pick, x)
    return [out[k * rows : (k + 1) * rows] for k in range(len(offsets))]


def _conv(x, w, b, rows):
    taps = _shifted(x, rows, [HALO - 2, HALO - 1]) + [x[HALO : HALO + rows].astype(F32)]
    return b + w[0:1] * taps[0] + w[1:2] * taps[1] + w[2:3] * taps[2], taps


def _ffn_act_fwd(u, conv_w, conv_b):
    t = u.shape[0]
    tm = _tile(t, (128,))
    tile, before, _, vec = _ffn_specs(t, tm)

    def body(u_ref, h_ref, w_ref, b_ref, z_ref):
        first = pl.program_id(1) == 0
        halo = h_ref[...]
        x = jnp.concatenate([jnp.where(first, jnp.zeros_like(halo), halo), u_ref[...]], axis=0)
        c, _ = _conv(x, w_ref[...], b_ref[...], tm)
        gate, val = c[:, :FF_TILE], c[:, FF_TILE:]
        z_ref[...] = (gate * _sigmoid(gate) * val).astype(BF16)

    return pl.pallas_call(
        body,
        name="ffn_act_fwd",
        grid=(FF_TILES, t // tm),
        in_specs=[tile, before, vec(3), vec(1)],
        out_specs=pl.BlockSpec((tm, FF_TILE), lambda j, i: (i, j)),
        out_shape=jax.ShapeDtypeStruct((t, D_FF), BF16),
        compiler_params=_params("parallel", "parallel"),
    )(u, u, conv_w, conv_b)


def _ffn_act_bwd(u, dz, conv_w, conv_b, dep=None):
    t = u.shape[0]
    tm = _tile(t, (128,))
    nt = t // tm
    ext = tm + HALO
    tile, before, after, vec = _ffn_specs(t, tm)

    def body(u_ref, ub_ref, ua_ref, w_ref, b_ref, dz_ref, dza_ref, du_ref, dw_ref, db_ref):
        i = pl.program_id(1)
        first, last = i == 0, i == nt - 1
        ub, ua = ub_ref[...], ua_ref[...]
        parts = [jnp.where(first, jnp.zeros_like(ub), ub), u_ref[...], jnp.where(last, jnp.zeros_like(ua), ua)]
        w = w_ref[...]
        c, taps = _conv(jnp.concatenate(parts, axis=0), w, b_ref[...], ext)
        gate, val = c[:, :FF_TILE], c[:, FF_TILE:]
        dz = jnp.concatenate([dz_ref[...].astype(F32), jnp.where(last, 0.0, dza_ref[...].astype(F32))], axis=0)
        sg = _sigmoid(gate)
        d = jnp.concatenate([dz * val * (sg * (1.0 + gate * (1.0 - sg))), dz * (gate * sg)], axis=1)
        d1, d2 = _shifted(d.astype(BF16), tm, [1, 2])
        du_ref[...] = (w[2:3] * d[:tm] + w[1:2] * d1 + w[0:1] * d2).astype(BF16)

        @pl.when(first)
        def _():
            dw_ref[...] = jnp.zeros_like(dw_ref)
            db_ref[...] = jnp.zeros_like(db_ref)

        for k, tap in enumerate(taps):
            dw_ref[k : k + 1, :] += jnp.sum(d[:tm] * tap[:tm], axis=0, keepdims=True)
        db_ref[...] += jnp.sum(d[:tm], axis=0, keepdims=True)

    narrow = lambda rows, index: pl.BlockSpec((rows, FF_TILE), index)
    return _pallas(
        body,
        7,
        dep,
        name="ffn_act_bwd",
        grid=(FF_TILES, nt),
        in_specs=[
            tile, before, after, vec(3), vec(1),
            narrow(tm, lambda j, i: (i, j)),
            narrow(HALO, lambda j, i: (jnp.minimum((i + 1) * (tm // HALO), t // HALO - 1), j)),
        ],
        out_specs=[tile, vec(3), vec(1)],
        out_shape=[
            jax.ShapeDtypeStruct((t, 2 * D_FF), BF16),
            jax.ShapeDtypeStruct((3, 2 * D_FF), F32),
            jax.ShapeDtypeStruct((1, 2 * D_FF), F32),
        ],
        compiler_params=_params("parallel", "arbitrary"),
    )(u, u, u, conv_w, conv_b, dz, dz)


def _ple_loss(gpre, pp, h2, final_norm, target):
    t, d = h2.shape
    tm = _tile(t, (256,))

    def body(gp_ref, pp_ref, h_ref, g_ref, tg_ref, dh_ref, dgp_ref, dpp_ref, dg_ref, loss_ref):
        i = pl.program_id(0)
        gate = _sigmoid(gp_ref[...])
        ppv = pp_ref[...]
        h3 = h_ref[...] + gate * ppv
        r = lax.rsqrt(jnp.mean(h3 * h3, axis=-1, keepdims=True) + EPS)
        n = h3 * r
        g = g_ref[...]
        err = n * g - tg_ref[...]
        loss = 0.5 * jnp.sum(jnp.mean(err * err, axis=-1, keepdims=True))
        dy = err * (1.0 / d)
        dn = dy * g
        dh = r * (dn - n * jnp.mean(dn * n, axis=-1, keepdims=True))
        dh_ref[...] = dh
        dgp_ref[...] = (dh * ppv * gate * (1.0 - gate)).astype(BF16)
        dpp_ref[...] = (dh * gate).astype(BF16)
        dg = jnp.sum(dy * n, axis=0, keepdims=True)

        @pl.when(i == 0)
        def _():
            dg_ref[...] = dg
            loss_ref[...] = jnp.full(loss_ref.shape, loss, F32)

        @pl.when(i > 0)
        def _():
            dg_ref[...] += dg
            loss_ref[...] += loss

    row = pl.BlockSpec((tm, d), lambda i: (i, 0))
    vec = pl.BlockSpec((1, d), lambda i: (0, 0))
    return pl.pallas_call(
        body,
        name="ple_loss",
        grid=(t // tm,),
        in_specs=[row, row, row, vec, row],
        out_specs=[row, row, row, vec, pl.BlockSpec((8, 128), lambda i: (0, 0))],
        out_shape=[
            jax.ShapeDtypeStruct((t, d), F32),
            jax.ShapeDtypeStruct((t, d), BF16),
            jax.ShapeDtypeStruct((t, d), BF16),
            jax.ShapeDtypeStruct((1, d), F32),
            jax.ShapeDtypeStruct((8, 128), F32),
        ],
        compiler_params=_params("arbitrary"),
    )(gpre, pp, h2, final_norm, target)


def _adamw(w, g, m, v):
    m = ADAM_B1 * m + (1.0 - ADAM_B1) * g
    v = ADAM_B2 * v + (1.0 - ADAM_B2) * (g * g)
    m_hat = m / (1.0 - ADAM_B1 ** ADAM_STEP)
    v_hat = v / (1.0 - ADAM_B2 ** ADAM_STEP)
    return -ADAM_LR * (m_hat / (jnp.sqrt(v_hat) + ADAM_EPS) + ADAM_WD * w), m, v


def _adam_big(w, m, v, own, recv, name, dep=None):
    r, c = w.shape
    tr = _tile(r, (256, 176))

    def body(w_ref, m_ref, v_ref, own_ref, recv_ref, g_ref, d_ref, nm_ref, nv_ref):
        g = own_ref[...]
        for k in range(3):
            g = g + recv_ref[k].astype(F32)
        g_ref[...] = g
        d_ref[...], nm_ref[...], nv_ref[...] = _adamw(w_ref[...], g, m_ref[...], v_ref[...])

    blk = pl.BlockSpec((tr, c), lambda i: (i, 0))
    return _pallas(
        body,
        5,
        dep,
        name=name,
        grid=(r // tr,),
        in_specs=[blk, blk, blk, blk, pl.BlockSpec((3, tr, c), lambda i: (0, i, 0))],
        out_specs=[blk] * 4,
        out_shape=[jax.ShapeDtypeStruct((r, c), F32)] * 4,
        compiler_params=_params("parallel"),
    )(w, m, v, own, recv)


def _adam_small(w, g, m, v):
    def body(w_ref, g_ref, m_ref, v_ref, d_ref, nm_ref, nv_ref):
        d_ref[...], nm_ref[...], nv_ref[...] = _adamw(w_ref[...], g_ref[...], m_ref[...], v_ref[...])

    return pl.pallas_call(body, name="adam_small", out_shape=[jax.ShapeDtypeStruct(w.shape, F32)] * 3, compiler_params=_params())(w, g, m, v)


def _position():
    return lax.axis_index("x"), lax.axis_index("y"), lax.axis_index("c")


def _other_chips(x, y):
    return [(1 - x, y), (x, 1 - y), (1 - x, 1 - y)]


def _block_index(dev, interleaved):
    x, y, c = dev
    return 4 * y + 2 * c + x if interleaved else 4 * x + 2 * y + c


def _shard_of(ref, axis, size, dev, interleaved=False):
    start = pl.multiple_of(_block_index(dev, interleaved) * size, 128 if axis == 1 else 16)
    return ref.at[:, pl.ds(start, size)] if axis == 1 else ref.at[pl.ds(start, size), :]


def _add_blocks(ids, grad, landed, axis, size, targets, out_dtype, name):
    rows = size if axis == 0 else grad.shape[0]
    cols = size if axis == 1 else grad.shape[1]
    tr = _tile(rows, (256, 176))
    nr = rows // tr
    nt = len(targets)

    def body(ids_ref, g_ref, l_ref, o_ref):
        o_ref[...] = (g_ref[...] + l_ref[...]).astype(out_dtype)

    if axis == 1:
        g_spec = pl.BlockSpec((tr, cols), lambda k, i, ids: (i, ids[targets[0] + k]))
    else:
        g_spec = pl.BlockSpec((tr, cols), lambda k, i, ids: (ids[targets[0] + k] * nr + i, 0))
    return pl.pallas_call(
        body,
        name=name,
        grid_spec=pltpu.PrefetchScalarGridSpec(
            num_scalar_prefetch=1,
            grid=(nt, nr),
            in_specs=[g_spec, pl.BlockSpec((None, tr, cols), lambda k, i, ids: (ids[4 + targets[0] + k], i, 0))],
            out_specs=pl.BlockSpec((None, tr, cols), lambda k, i, ids: (k, i, 0)),
        ),
        out_shape=jax.ShapeDtypeStruct((nt, rows, cols), out_dtype),
        compiler_params=_params("parallel", "parallel"),
    )(ids, grad, landed)


def _all_reduce_small(vec, name):
    rows = vec.shape[0]

    def body(v_ref, o_ref, land, send_sems, recv_sems):
        x, y, c = _position()
        mine = 4 * x + 2 * y + c
        copies = []
        for mask in range(1, N_DEV):
            peer = (1 - x if mask & 4 else x, 1 - y if mask & 2 else y, 1 - c if mask & 1 else c)
            copies.append(
                pltpu.make_async_remote_copy(
                    src_ref=v_ref, dst_ref=land.at[mine], send_sem=send_sems.at[mask - 1], recv_sem=recv_sems.at[mask - 1], device_id=peer, device_id_type=MESH
                )
            )
        for cp in copies:
            cp.start()
        land[mine] = v_ref[...]
        for cp in copies:
            cp.wait()
        acc = land[0]
        for k in range(1, N_DEV):
            acc = acc + land[k]
        o_ref[...] = acc

    return pl.pallas_call(
        body,
        name=name,
        out_shape=jax.ShapeDtypeStruct(vec.shape, F32),
        in_specs=[pl.BlockSpec(memory_space=pltpu.VMEM)],
        out_specs=pl.BlockSpec(memory_space=pltpu.VMEM),
        scratch_shapes=[pltpu.VMEM((N_DEV, rows, 128), F32), pltpu.SemaphoreType.DMA((N_DEV - 1,)), pltpu.SemaphoreType.DMA((N_DEV - 1,))],
    )(vec)


def _rows128(a, rows):
    flat = a.reshape(-1)
    return jnp.pad(flat, (0, rows * 128 - flat.shape[0])).reshape(rows, 128)


def _pad_rel(a):
    return jnp.pad(a.reshape(ATT_HEADS, -1)[:, :N_REL], ((0, 0), (0, N_REL_PAD - N_REL)))


SMALL = [("norm_mix", 16), ("lb_logits", 16), ("hg_norm", 8), ("rel_bias", 24), ("norm_ffn", 16), ("conv_b", 88), ("norm_ple", 16), ("final_norm", 16)]
CONV_W_FULL_ROWS = 3 * 2 * D_FF // 128
CONV_W_SHARD_ROWS = 40


def _pack_small(parts):
    return jnp.concatenate([_rows128(_pad_rel(parts[k]) if k == "rel_bias" else parts[k], rows) for k, rows in SMALL], axis=0)


def _unpack_small(packed, shapes):
    out, at = {}, 0
    for k, rows in SMALL:
        blk = packed[at : at + rows]
        at += rows
        if k == "rel_bias":
            out[k] = blk.reshape(ATT_HEADS, N_REL_PAD)[:, :N_REL].reshape(shapes[k])
        else:
            n = 1
            for s in shapes[k]:
                n *= s
            out[k] = blk.reshape(-1)[:n].reshape(shapes[k])
    return out, at


BIG = [("w_in", 1), ("w_out", 0), ("w_up", 1), ("w_down", 0), ("w_ple_gate", 0), ("w_ple_proj", 1)]


HBM = pl.BlockSpec(memory_space=pltpu.HBM)
SEM = pl.BlockSpec(memory_space=pltpu.SEMAPHORE)
EFFECT = pltpu.SideEffectType.DATAFLOW_SIDE_EFFECTING


def _copies(plan, refs, send_sems, recv_sems):
    return [
        pltpu.make_async_remote_copy(src_ref=src, dst_ref=dst, send_sem=send_sems.at[i], recv_sem=recv_sems.at[i], device_id=dev, device_id_type=MESH)
        for i, (src, dst, dev) in enumerate(plan(refs))
    ]


def _split_start(name, arrays, plan, n):
    k = len(arrays)

    def body(*refs):
        for cp in _copies(plan, refs[:k], refs[k], refs[k + 1]):
            cp.start()
        refs[-1][...] = jnp.zeros_like(refs[-1])

    out = pl.pallas_call(
        body,
        name=name,
        out_shape=(pltpu.SemaphoreType.DMA((n,)), pltpu.SemaphoreType.DMA((n,)), *[pltpu.HBM(a.shape, a.dtype) for a in arrays], jax.ShapeDtypeStruct((8, 128), F32)),
        in_specs=[HBM] * k,
        out_specs=(SEM, SEM, *[HBM] * k, pl.BlockSpec(memory_space=pltpu.VMEM)),
        input_output_aliases={i: 2 + i for i in range(k)},
        compiler_params=pltpu.CompilerParams(has_side_effects=EFFECT),
    )(*[pltpu.with_memory_space_constraint(a, pltpu.HBM) for a in arrays])
    return out[0], out[1], list(out[2 : 2 + k]), out[-1]


def _split_wait(name, send, recv, arrays, plan, after):
    k = len(arrays)

    def body(*refs):
        for cp in _copies(plan, refs[:k], refs[k], refs[k + 1]):
            cp.wait_send()
            cp.wait_recv()

    out = pl.pallas_call(
        body,
        name=name,
        out_shape=tuple(pltpu.HBM(a.shape, a.dtype) for a in arrays),
        in_specs=[HBM] * k + [SEM, SEM, ANY],
        out_specs=tuple([HBM] * k),
        input_output_aliases={i: i for i in range(k)},
        compiler_params=pltpu.CompilerParams(has_side_effects=EFFECT),
    )(*arrays, send, recv, after)
    return list(out)


def _cast_into(w, me, axis, name, dep, dtype):
    r, c = w.shape
    tr = _tile(r, (256, 176))
    nr = r // tr
    deps = [] if dep is None else [dep]

    def body(me_ref, w_ref, *rest):
        rest[-1][...] = w_ref[...].astype(dtype)

    if axis == 1:
        shape, o_spec = (r, N_DEV * c), pl.BlockSpec((tr, c), lambda i, me: (i, me[0]))
    else:
        shape, o_spec = (N_DEV * r, c), pl.BlockSpec((tr, c), lambda i, me: (me[0] * nr + i, 0))
    return pl.pallas_call(
        body,
        name=name,
        grid_spec=pltpu.PrefetchScalarGridSpec(
            num_scalar_prefetch=1, grid=(nr,), in_specs=[pl.BlockSpec((tr, c), lambda i, me: (i, 0))] + [ANY] * len(deps), out_specs=o_spec
        ),
        out_shape=jax.ShapeDtypeStruct(shape, dtype),
        compiler_params=_params("parallel"),
    )(me, w, *deps)


GATHER = [
    (["w_in"], None, "norm_mix_fwd", "bias_table"),
    (["w_out"], "norm_mix_fwd", "att_fwd", None),
    (["w_up", "conv_w"], "norm_mix_fwd", "att_fwd", "norm_ffn_fwd"),
    (["w_down", "w_ple_gate", "w_ple_proj"], "att_fwd", "up_proj", "ffn_act_fwd"),
]
GROUPS = [["w_ple_proj", "w_ple_gate", "w_down"], ["w_up"], ["w_out"], ["w_in"]]
STAGES = ["ffn_act_bwd", "d_mix_out", "hgrn_bwd", "d_norm_mix_out"]
INTERLEAVED = {"w_up", "conv_w"}


class _Exchange:
    def __init__(self, big, conv_w, position):
        self.big, self.axis = big, dict(BIG, conv_w=1)
        self.shards = dict(big, conv_w=conv_w)
        self.size = {k: w.shape[self.axis[k]] for k, w in self.shards.items()}
        self.x, self.y, self.c = position
        chips = [(self.x, self.y)] + _other_chips(self.x, self.y)
        landed = [2 * cx + cy for cx, cy in chips]
        self.ids = {
            flag: jnp.stack([_block_index((cx, cy, self.c), flag) for cx, cy in chips] + landed).astype(jnp.int32) for flag in (False, True)
        }
        self.tokens, self.grads, self.state, self.wfull = [], {}, {}, {}


    def _slot(self, ref, k, dev):
        return _shard_of(ref, self.axis[k], self.size[k], dev, interleaved=k in INTERLEAVED)

    def _plan_gather(self, names, direct, refs):
        x, y, c = _position()
        me, out = (x, y, c), []
        for k, ref in zip(names, refs):
            mine = self._slot(ref, k, me)
            out.append((mine, mine, (x, y, 1 - c)))
            out += [(mine, mine, (*chip, c)) for chip in _other_chips(x, y)]
            if direct:
                out += [(mine, mine, (*chip, 1 - c)) for chip in _other_chips(x, y)]
        return out

    def _plan_forward(self, names, refs):
        x, y, c = _position()
        out = []
        for k, ref in zip(names, refs):
            for chip in _other_chips(x, y):
                block = self._slot(ref, k, (*chip, c))
                out.append((block, block, (x, y, 1 - c)))
        return out

    def _plan_sibling(self, names, refs):
        x, y, c = _position()
        n = len(names)
        return [(self._slot(refs[i], k, (p // 2, p % 2, 1 - c)), refs[n + i].at[p], (x, y, 1 - c)) for i, k in enumerate(names) for p in range(4)]

    def _plan_chips(self, names, refs):
        x, y, c = _position()
        n = len(names)
        return [(refs[i].at[j], refs[n + i].at[j], (*chip, c)) for i in range(n) for j, chip in enumerate(_other_chips(x, y))]


    def gather(self):
        me = {flag: _block_index((self.x, self.y, self.c), flag).astype(jnp.int32).reshape(1) for flag in (False, True)}
        self.late, self.unsent = {}, {}
        after = None
        for gi, (names, issued, *_) in enumerate(GATHER):
            self.unsent[gi] = [
                _cast_into(self.shards[k], me[k in INTERLEAVED], self.axis[k], "cast_" + k, after, F32 if k == "conv_w" else BF16) for k in names
            ]
            if issued is None:
                self._issue(None)
                after = self.tokens[-1]
        self.tokens += [a for arrays in self.unsent.values() for a in arrays]

    def _issue(self, stage):
        for gi, (names, issued, _, forwarded) in enumerate(GATHER):
            if issued == stage and gi in self.unsent:
                plan = functools.partial(self._plan_gather, names, forwarded is None)
                copies = (7 if forwarded is None else 4) * len(names)
                send, recv, fulls, token = _split_start(f"gather_start_{gi}", self.unsent.pop(gi), plan, copies)
                self.tokens.append(token)
                self.late[gi] = (send, recv, fulls, plan)

    def weight(self, k):
        return self.wfull[k]

    def dep(self):
        tokens, self.tokens = self.tokens, []
        return tokens

    def reduce(self, vec, name):
        return _all_reduce_small(vec, name)

    def grad(self, k, g):
        self.grads[k] = g
        for gi, names in enumerate(GROUPS):
            if k == names[-1]:
                plan = functools.partial(self._plan_sibling, names)
                lands = [lax.empty((4, *self._shard_shape(n)), F32) for n in names]
                send, recv, arrays, token = _split_start(f"sibling_start_{gi}", [self.grads[n] for n in names] + lands, plan, 4 * len(names))
                self.tokens.append(token)
                self.state[gi] = (send, recv, arrays, plan)

    def done(self, stage, after):
        for gi, (names, _, _, forwarded) in enumerate(GATHER):
            if forwarded == stage:
                send, recv, fulls, plan = self.late[gi]
                self.wfull.update(zip(names, _split_wait(f"forward_wait_{gi}", send, recv, fulls, plan, after)))
        for gi, (names, _, arrived, forwarded) in enumerate(GATHER):
            if arrived == stage:
                send, recv, fulls, plan = self.late[gi]
                fulls = _split_wait(f"gather_wait_{gi}", send, recv, fulls, plan, after)
                if forwarded is None:
                    self.wfull.update(zip(names, fulls))
                else:
                    plan = functools.partial(self._plan_forward, names)
                    send, recv, fulls, token = _split_start(f"forward_start_{gi}", fulls, plan, 3 * len(names))
                    self.tokens.append(token)
                    self.late[gi] = (send, recv, fulls, plan)
        self._issue(stage)
        if stage in STAGES:
            self._to_chips(STAGES.index(stage), after)

    def _shard_shape(self, k):
        shape = list(self.grads[k].shape)
        shape[self.axis[k]] = self.size[k]
        return tuple(shape)

    def _to_chips(self, gi, after):
        names = GROUPS[gi]
        n = len(names)
        send, recv, arrays, plan = self.state[gi]
        arrays = _split_wait(f"sibling_wait_{gi}", send, recv, arrays, plan, after)
        own, parts = [], []
        for k, g, land in zip(names, arrays[:n], arrays[n:]):
            ids = self.ids[k in INTERLEAVED]
            own.append(_add_blocks(ids, g, land, self.axis[k], self.size[k], [0], F32, "add_own_" + k)[0])
            parts.append(_add_blocks(ids, g, land, self.axis[k], self.size[k], [1, 2, 3], BF16, "add_send_" + k))
        plan = functools.partial(self._plan_chips, names)
        lands = [lax.empty(part.shape, BF16) for part in parts]
        send, recv, arrays, token = _split_start(f"chips_start_{gi}", parts + lands, plan, 3 * n)
        self.tokens.append(token)
        self.state[gi] = (send, recv, arrays, plan, own)

    def finish(self, gi, after):
        names = GROUPS[gi]
        send, recv, arrays, plan, own = self.state[gi]
        arrays = _split_wait(f"chips_wait_{gi}", send, recv, arrays, plan, after)
        return {k: (o, r) for k, o, r in zip(names, own, arrays[len(names) :])}


class _Resident:
    def __init__(self, wfull):
        self.wfull, self.grads = wfull, {}

    def weight(self, k):
        return self.wfull[k]

    def grad(self, k, g):
        self.grads[k] = g

    def dep(self):
        return None

    def reduce(self, vec, name):
        return vec

    def done(self, stage, after):
        pass


def _local_step(x, p, target, small, ex):
    bias = _bias_table(jnp.pad(small["rel_bias"], ((0, 0), (0, N_REL_PAD - N_REL))), dep=ex.dep())
    a1, r1 = _rms_fwd(x, small["norm_mix"], "norm_mix_fwd", dep=[bias])
    ex.done("norm_mix_fwd", a1)
    ex.done("bias_table", a1)
    proj = _matmul(a1, ex.weight("w_in"), "nn", F32, "in_proj", dep=ex.dep())
    y_hg, o_hg, states = _hgrn_fwd(proj, small["lb_logits"], small["hg_norm"])
    ycat = _att_fwd(proj, bias, y_hg, dep=ex.dep())
    ex.done("att_fwd", ycat)
    h1 = _matmul(ycat, ex.weight("w_out"), "nn", F32, "out_proj", resid=x, dep=ex.dep())
    a2, r2 = _rms_fwd(h1, small["norm_ffn"], "norm_ffn_fwd")
    ex.done("norm_ffn_fwd", a2)
    conv_w = ex.weight("conv_w")
    u = _matmul(a2, ex.weight("w_up"), "nn", BF16, "up_proj")
    conv_b = _interleave_cols(small["conv_b"])
    ex.done("up_proj", u)
    z = _ffn_act_fwd(u, conv_w, conv_b)
    ex.done("ffn_act_fwd", z)
    h2 = _matmul(z, ex.weight("w_down"), "nn", F32, "down_proj", tk=2816, resid=h1)
    a3, r3 = _rms_fwd(h2, small["norm_ple"], "norm_ple_fwd")
    gpre = _matmul(a3, ex.weight("w_ple_gate"), "nn", F32, "ple_gate")
    pp = _matmul(p, ex.weight("w_ple_proj"), "nn", F32, "ple_proj")
    dh3, dgpre, dpp, d_final, loss = _ple_loss(gpre, pp, h2, small["final_norm"], target)

    ex.grad("w_ple_proj", _matmul(p, dpp, "tn", F32, "d_w_ple_proj", tm=512))
    ex.grad("w_ple_gate", _matmul(a3, dgpre, "tn", F32, "d_w_ple_gate", tm=512))
    da3 = _matmul(dgpre, ex.weight("w_ple_gate"), "nt", F32, "d_norm_ple_out")
    dh2, dh2b, d_ple = _rms_bwd(da3, h2, r3, small["norm_ple"], dh3, "norm_ple_bwd")
    dz = _matmul(dh2b, ex.weight("w_down"), "nt", BF16, "d_ffn_act")
    ex.grad("w_down", _matmul(z, dh2b, "tn", F32, "d_w_down", tm=512))
    du, dcw, dcb = _ffn_act_bwd(u, dz, conv_w, conv_b, dep=ex.dep())
    ex.done("ffn_act_bwd", du)
    d_conv_w, d_conv_b = _deinterleave_cols(dcw), _deinterleave_cols(dcb)
    ex.grad("w_up", _matmul(a2, du, "tn", F32, "d_w_up", tm=512, dep=ex.dep()))
    da2 = _matmul(du, ex.weight("w_up"), "nt", F32, "d_norm_ffn_out", tk=2816, dep=ex.dep())
    dh1, dh1b, d_ffn = _rms_bwd(da2, h1, r2, small["norm_ffn"], dh2, "norm_ffn_bwd")
    dycat = _matmul(dh1b, ex.weight("w_out"), "nt", F32, "d_mix_out")
    ex.done("d_mix_out", dycat)
    ex.grad("w_out", _matmul(ycat, dh1b, "tn", F32, "d_w_out", tm=512, dep=ex.dep()))
    dp_hg, d_lb, d_hgn = _hgrn_bwd(proj, small["lb_logits"], small["hg_norm"], o_hg, dycat, states, dep=ex.dep())
    ex.done("hgrn_bwd", d_lb)
    dproj, dk_att, dv_att, gsum = _att_bwd(proj, bias, dycat, dp_hg, dep=ex.dep())
    d_rel = _rel_bias_grad(gsum)
    d_small = {
        "norm_mix": jnp.zeros_like(small["norm_mix"]), "lb_logits": d_lb, "hg_norm": d_hgn, "rel_bias": d_rel, "norm_ffn": d_ffn,
        "conv_b": d_conv_b, "norm_ple": d_ple, "final_norm": d_final,
    }
    packed = jnp.concatenate([_pack_small(d_small), _rows128(d_conv_w, CONV_W_FULL_ROWS), _rows128(loss[0:1, 0:1], 8)], axis=0)
    early = ex.reduce(packed, "all_reduce_small")
    for k, part in enumerate((dk_att, dv_att)):
        dproj = lax.dynamic_update_slice(dproj, part, (0, 4 * HG_WIDTH + (k + 1) * ATT_WIDTH))
    ex.grad("w_in", _matmul(a1, dproj, "tn", F32, "d_w_in", tm=512, dep=[early]))
    da1 = _matmul(dproj, ex.weight("w_in"), "nt", F32, "d_norm_mix_out", tk=1792, dep=ex.dep())
    dx, _, d_mix = _rms_bwd(da1, x, r1, small["norm_mix"], dh1, "norm_mix_bwd")
    rows = dict(SMALL)["norm_mix"]
    late = ex.reduce(_rows128(d_mix, rows), "all_reduce_norm_mix")
    ex.done("d_norm_mix_out", late)
    return dx, jnp.concatenate([late, early[rows:]], axis=0)


def kernel(x, p, norm_mix, w_in, lb_logits, hg_norm, rel_bias, w_out, norm_ffn, w_up, conv_w, conv_b, w_down, norm_ple, w_ple_gate, w_ple_proj, final_norm, loss_target, m_norm_mix, m_w_in, m_lb_logits, m_hg_norm, m_rel_bias, m_w_out, m_norm_ffn, m_w_up, m_conv_w, m_conv_b, m_w_down, m_norm_ple, m_w_ple_gate, m_w_ple_proj, m_final_norm, v_norm_mix, v_w_in, v_lb_logits, v_hg_norm, v_rel_bias, v_w_out, v_norm_ffn, v_w_up, v_conv_w, v_conv_b, v_w_down, v_norm_ple, v_w_ple_gate, v_w_ple_proj, v_final_norm):
    given = dict(locals())
    mx, my, mc = _position()
    me = 4 * mx + 2 * my + mc
    big = {k: given[k][0] for k, _ in BIG}
    ex = _Exchange(big, conv_w[0], (mx, my, mc))
    ex.gather()

    small = {
        "norm_mix": norm_mix, "lb_logits": lb_logits, "hg_norm": hg_norm, "rel_bias": rel_bias[0], "norm_ffn": norm_ffn,
        "conv_b": conv_b, "norm_ple": norm_ple, "final_norm": final_norm.reshape(1, -1),
    }
    dx, reduced = _local_step(x[0], p[0, 0], loss_target[0], small, ex)

    out = {}
    shapes = {k: given[k].shape for k, _ in SMALL}
    g_small, at = _unpack_small(reduced, shapes)
    g_conv_full = reduced[at : at + CONV_W_FULL_ROWS].reshape(3, 2 * D_FF)
    total_loss = reduced[at + CONV_W_FULL_ROWS, 0]
    cw = conv_w.shape[2]
    g_conv = lax.dynamic_slice_in_dim(g_conv_full, me * cw, cw, axis=1)

    def pack_with_conv(parts, conv_part):
        return jnp.concatenate([_pack_small(parts), _rows128(conv_part, CONV_W_SHARD_ROWS)], axis=0)

    d_pk, m_pk, v_pk = _adam_small(
        pack_with_conv({k: given[k] for k, _ in SMALL}, conv_w),
        pack_with_conv(g_small, g_conv),
        pack_with_conv({k: given["m_" + k] for k, _ in SMALL}, m_conv_w),
        pack_with_conv({k: given["v_" + k] for k, _ in SMALL}, v_conv_w),
    )
    for name, pk in (("d", d_pk), ("m", m_pk), ("v", v_pk)):
        parts, at = _unpack_small(pk, shapes)
        parts["conv_w"] = pk[at : at + CONV_W_SHARD_ROWS].reshape(-1)[: 3 * cw].reshape(conv_w.shape)
        for k, a in parts.items():
            out.setdefault(k, {})
            out[k][name] = a
    for k, _ in SMALL:
        out[k]["g"] = g_small[k]
    out["conv_w"]["g"] = g_conv.reshape(conv_w.shape)

    after, started = v_pk, ex.dep()
    for gi in range(len(GROUPS)):
        for k, (o, r) in ex.finish(gi, after).items():
            g, d, nm, nv = _adam_big(big[k], given["m_" + k][0], given["v_" + k][0], o, r, "adam_" + k, dep=started)
            out[k] = tuple(a[None] for a in (g, d, nm, nv))
            after = nv

    order = ["norm_mix", "w_in", "lb_logits", "hg_norm", "rel_bias", "w_out", "norm_ffn", "w_up", "conv_w", "conv_b", "w_down", "norm_ple", "w_ple_gate", "w_ple_proj", "final_norm"]

    def pick(k, what):
        return out[k][what] if isinstance(out[k], dict) else out[k][{"g": 0, "d": 1, "m": 2, "v": 3}[what]]

    return (total_loss, dx[None], *[pick(k, "g") for k in order], *[pick(k, "d") for k in order], *[pick(k, "m") for k in order], *[pick(k, "v") for k in order])
```

```python
import functools

import jax
import jax.numpy as jnp
from jax import lax
from jax.experimental import pallas as pl
from jax.experimental.pallas import tpu as pltpu

F32 = jnp.float32
BF16 = jnp.bfloat16

D_MODEL = 2048
CHUNK = 64
HG_HEADS = 8
HEAD_DIM = 128
HG_WIDTH = HG_HEADS * HEAD_DIM
ATT_HEADS = 8
ATT_WIDTH = ATT_HEADS * HEAD_DIM
LEFT_CHUNKS = 8
PAD = LEFT_CHUNKS * CHUNK
BAND = PAD + CHUNK
REL_CLIP = 128
N_REL = 2 * REL_CLIP + 1
N_REL_PAD = 384
D_FF = 5632
EPS = 1e-6
ATT_SCALE = HEAD_DIM ** -0.5
SUB = 32
HG_BLOCK = 8
Q_BLOCK = 4 * CHUNK
K_BLOCK = Q_BLOCK + PAD
DIAG = 1024
ATT_BLOCK = 2
MASKED = -1e30

ADAM_LR = 0.001
ADAM_B1 = 0.9
ADAM_B2 = 0.999
ADAM_EPS = 1e-08
ADAM_WD = 0.01
ADAM_STEP = 10

N_DEV = 8
VMEM_LIMIT = 48 * 1024 * 1024
ATT_BWD_VMEM = 58 * 1024 * 1024
MESH = pl.DeviceIdType.MESH
ANY = pl.BlockSpec(memory_space=pl.ANY)
HIGHEST = lax.Precision.HIGHEST

NN = (((1,), (0,)), ((), ()))
NT = (((1,), (1,)), ((), ()))
TN = (((0,), (0,)), ((), ()))


def _params(*sem):
    return pltpu.CompilerParams(dimension_semantics=sem if sem else None, vmem_limit_bytes=VMEM_LIMIT)


def _pallas(body, n_in, dep, **kw):
    deps = [] if dep is None else list(dep)
    if not deps:
        return pl.pallas_call(body, **kw)

    def body_after(*refs):
        body(*refs[:n_in], *refs[n_in + len(deps) :])

    call = pl.pallas_call(body_after, **dict(kw, in_specs=list(kw["in_specs"]) + [ANY] * len(deps)))
    return lambda *ops: call(*ops, *deps)


def _dot(a, b, dims=NN):
    return lax.dot_general(a, b, dims, preferred_element_type=F32)


def _dot3(a, b, dims=NN):
    a_hi, b_hi = a.astype(BF16), b.astype(BF16)
    a_lo, b_lo = (a - a_hi.astype(F32)).astype(BF16), (b - b_hi.astype(F32)).astype(BF16)
    return _dot(a_hi, b_hi, dims) + (_dot(a_hi, b_lo, dims) + _dot(a_lo, b_hi, dims))


def _sigmoid(x):
    return 1.0 / (1.0 + jnp.exp(-x))


def _tile(n, prefs):
    for t in prefs:
        if n % t == 0:
            return t
    return n


def _matmul(a, b, mode, out_dtype, name, tm=1024, tn=1024, tk=None, resid=None, dep=None):
    if mode == "nn":
        (m, k), n = a.shape, b.shape[1]
    elif mode == "nt":
        (m, k), n = a.shape, b.shape[0]
    else:
        (k, m), n = a.shape, b.shape[1]
    tm = _tile(m, (tm, 512, 256, 128))
    tn = _tile(n, (tn, 1408, 512, 256, 128))
    tk = k if tk is None else _tile(k, (tk,))
    nk = k // tk
    dims = {"nn": NN, "nt": NT, "tn": TN}[mode]
    a_spec = pl.BlockSpec((tk, tm), lambda i, j, s: (s, i)) if mode == "tn" else pl.BlockSpec((tm, tk), lambda i, j, s: (i, s))
    b_spec = pl.BlockSpec((tn, tk), lambda i, j, s: (j, s)) if mode == "nt" else pl.BlockSpec((tk, tn), lambda i, j, s: (s, j))
    o_spec = pl.BlockSpec((tm, tn), lambda i, j, s: (i, j))
    has_res = resid is not None

    def body(*refs):
        a_ref, b_ref = refs[0], refs[1]
        o_ref = refs[2 + has_res]
        part = _dot(a_ref[...].astype(BF16), b_ref[...].astype(BF16), dims)

        def finish(acc):
            if has_res:
                acc = acc + refs[2][...]
            o_ref[...] = acc.astype(out_dtype)

        if nk == 1:
            finish(part)
        else:
            acc_ref = refs[-1]
            s = pl.program_id(2)

            @pl.when(s == 0)
            def _():
                acc_ref[...] = part

            @pl.when(s > 0)
            def _():
                acc_ref[...] += part

            @pl.when(s == nk - 1)
            def _():
                finish(acc_ref[...])

    return _pallas(
        body,
        2 + has_res,
        dep,
        name=name,
        grid=(m // tm, n // tn, nk),
        in_specs=[a_spec, b_spec] + ([o_spec] if has_res else []),
        out_specs=o_spec,
        out_shape=jax.ShapeDtypeStruct((m, n), out_dtype),
        scratch_shapes=[pltpu.VMEM((tm, tn), F32)] if nk > 1 else [],
        compiler_params=_params("parallel", "parallel", "arbitrary"),
    )(*([a, b] + ([resid] if has_res else [])))


def _rms_fwd(x, g, name, dep=None):
    t, d = x.shape
    tm = _tile(t, (256,))

    def body(x_ref, g_ref, a_ref, r_ref):
        xv = x_ref[...]
        r = lax.rsqrt(jnp.mean(xv * xv, axis=-1, keepdims=True) + EPS)
        a_ref[...] = (xv * r * g_ref[...]).astype(BF16)
        r_ref[...] = r

    row = pl.BlockSpec((tm, d), lambda i: (i, 0))
    return _pallas(
        body,
        2,
        dep,
        name=name,
        grid=(t // tm,),
        in_specs=[row, pl.BlockSpec((1, d), lambda i: (0, 0))],
        out_specs=[row, pl.BlockSpec((tm, 1), lambda i: (i, 0))],
        out_shape=[jax.ShapeDtypeStruct((t, d), BF16), jax.ShapeDtypeStruct((t, 1), F32)],
        compiler_params=_params("parallel"),
    )(x, g)


def _rms_bwd(da, x, r, g, resid, name, dep=None):
    t, d = x.shape
    tm = _tile(t, (256,))

    def body(da_ref, x_ref, r_ref, g_ref, res_ref, dx_ref, dxb_ref, dg_ref):
        i = pl.program_id(0)
        rv = r_ref[...]
        n = x_ref[...] * rv
        dav = da_ref[...].astype(F32)
        dn = dav * g_ref[...]
        dx = rv * (dn - n * jnp.mean(dn * n, axis=-1, keepdims=True)) + res_ref[...]
        dx_ref[...] = dx
        dxb_ref[...] = dx.astype(BF16)
        part = jnp.sum(dav * n, axis=0, keepdims=True)

        @pl.when(i == 0)
        def _():
            dg_ref[...] = part

        @pl.when(i > 0)
        def _():
            dg_ref[...] += part

    row = pl.BlockSpec((tm, d), lambda i: (i, 0))
    vec = pl.BlockSpec((1, d), lambda i: (0, 0))
    return _pallas(
        body,
        5,
        dep,
        name=name,
        grid=(t // tm,),
        in_specs=[row, row, pl.BlockSpec((tm, 1), lambda i: (i, 0)), vec, row],
        out_specs=[row, row, vec],
        out_shape=[jax.ShapeDtypeStruct((t, d), F32), jax.ShapeDtypeStruct((t, d), BF16), jax.ShapeDtypeStruct((1, d), F32)],
        compiler_params=_params("arbitrary"),
    )(da, x, r, g, resid)


def _tri(n, upper):
    r = lax.broadcasted_iota(jnp.int32, (n, n), 0)
    c = lax.broadcasted_iota(jnp.int32, (n, n), 1)
    return jnp.where((c >= r) if upper else (c <= r), 1.0, 0.0).astype(F32)


def _hgrn_gates(q, fp, lbl):
    l0, l1 = lbl[0:1, :], lbl[1:2, :]
    mx = jnp.maximum(l0, l1)
    e0, e1 = jnp.exp(l0 - mx), jnp.exp(l1 - mx)
    lb = e0 / (e0 + e1)
    sig = _sigmoid(fp)
    f = lb + (1.0 - lb) * sig
    kk = (1.0 - lb) * _sigmoid(-fp)
    sq = _sigmoid(q)
    b = jnp.dot(_tri(CHUNK, False), jnp.log(f), precision=HIGHEST, preferred_element_type=F32)
    return lb, sig, f, kk, sq, q * sq, b


def _heads(x):
    return [x[:, j * HEAD_DIM : (j + 1) * HEAD_DIM] for j in range(x.shape[1] // HEAD_DIM)]


def _wide(parts):
    return jnp.concatenate(parts, axis=1)


def _intra_blocks(b):
    out = []
    for lo in range(0, CHUNK, SUB):
        hi = lo + SUB
        br = b[lo + SUB // 2 : lo + SUB // 2 + 1, :]
        row = lax.broadcasted_iota(jnp.int32, (SUB, hi), 0) + lo
        col = lax.broadcasted_iota(jnp.int32, (SUB, hi), 1)
        out.append((lo, hi, jnp.exp(b[lo:hi] - br), jnp.exp(br - b[:hi]), col <= row))
    return out


def _hgrn_fwd(proj, lb_logits, hg_norm):
    t = proj.shape[0]
    nc = t // CHUNK

    def body(q_ref, f_ref, i_ref, g_ref, lbl_ref, hgn_ref, y_ref, o_ref, st_ref, s_scr):
        c = pl.program_id(1)

        @pl.when(c == 0)
        def _():
            s_scr[...] = jnp.zeros_like(s_scr)

        hs = range(HG_BLOCK)
        sts = [s_scr[j] for j in hs]
        _, _, _, kk, _, qf, b = _hgrn_gates(q_ref[...], f_ref[...], lbl_ref[...])
        vb = _heads(i_ref[...].astype(BF16))
        bl = b[CHUNK - 1 : CHUNK, :]
        qe = _heads((qf * jnp.exp(b)).astype(BF16))
        kd = _heads((kk * jnp.exp(bl - b)).astype(BF16))
        decay = _heads(jnp.exp(bl))
        o = [_dot(qe[j], sts[j].astype(BF16), NT) for j in hs]
        parts = [[] for _ in hs]
        for lo, hi, ea, eb, mask in _intra_blocks(b):
            a, bk = _heads((qf[lo:hi] * ea).astype(BF16)), _heads((kk[:hi] * eb).astype(BF16))
            p = [jnp.where(mask, _dot(a[j], bk[j], NT), 0.0).astype(BF16) for j in hs]
            for j in hs:
                parts[j].append(_dot(p[j], vb[j][:hi]))
        o = [o[j] + jnp.concatenate(parts[j], axis=0) for j in hs]
        new = [sts[j] * decay[j] + _dot(vb[j], kd[j], TN) for j in hs]
        hgn = hgn_ref[...]
        on = [o[j] * lax.rsqrt(jnp.mean(o[j] * o[j], axis=-1, keepdims=True) + EPS) * hgn for j in hs]
        gg = g_ref[...]
        for j in hs:
            st_ref[j] = sts[j]
            s_scr[j] = new[j]
        o_ref[...] = _wide(o)
        y_ref[...] = (_wide(on) * (gg * _sigmoid(gg))).astype(BF16)

    wide = HG_BLOCK * HEAD_DIM
    groups = HG_HEADS // HG_BLOCK

    def col(k):
        return pl.BlockSpec((CHUNK, wide), lambda g, c: (c, k * groups + g))

    out = pl.BlockSpec((CHUNK, wide), lambda g, c: (c, g))
    return pl.pallas_call(
        body,
        name="hgrn_fwd",
        grid=(groups, nc),
        in_specs=[col(0), col(1), col(2), col(3), pl.BlockSpec((2, wide), lambda g, c: (0, g)), pl.BlockSpec((1, HEAD_DIM), lambda g, c: (0, 0))],
        out_specs=[out, out, pl.BlockSpec((HG_BLOCK, None, HEAD_DIM, HEAD_DIM), lambda g, c: (g, c, 0, 0))],
        out_shape=[
            jax.ShapeDtypeStruct((t, HG_WIDTH + ATT_WIDTH), BF16),
            jax.ShapeDtypeStruct((t, HG_WIDTH), F32),
            jax.ShapeDtypeStruct((HG_HEADS, nc, HEAD_DIM, HEAD_DIM), F32),
        ],
        scratch_shapes=[pltpu.VMEM((HG_BLOCK, HEAD_DIM, HEAD_DIM), F32)],
        compiler_params=_params("arbitrary", "arbitrary"),
    )(proj, proj, proj, proj, lb_logits, hg_norm)


def _hgrn_bwd(proj, lb_logits, hg_norm, o_hg, dycat, states, dep=None):
    t = proj.shape[0]
    nc = t // CHUNK

    def body(q_ref, f_ref, i_ref, g_ref, lbl_ref, hgn_ref, o_ref, dy_ref, st_ref, dp_ref, dlbl_ref, dhgn_ref, dst_scr, dlb_scr):
        h = pl.program_id(0)
        c = pl.program_id(1)

        @pl.when(c == 0)
        def _():
            dst_scr[...] = jnp.zeros_like(dst_scr)
            dlb_scr[...] = jnp.zeros_like(dlb_scr)

        @pl.when((c == 0) & (h == 0))
        def _():
            dhgn_ref[...] = jnp.zeros_like(dhgn_ref)

        hs = range(HG_BLOCK)
        hgn = _wide([hgn_ref[...]] * HG_BLOCK)
        q, fp, gg, vi = q_ref[...], f_ref[...], g_ref[...], i_ref[...]
        lb, sig, f, kk, sq, qf, b = _hgrn_gates(q, fp, lbl_ref[...])
        o, dy = o_ref[...], dy_ref[...]
        sg = _sigmoid(gg)
        n = _wide([oh * lax.rsqrt(jnp.mean(oh * oh, axis=-1, keepdims=True) + EPS) for oh in _heads(o)])
        don = dy * (gg * sg)
        dgg = dy * (n * hgn) * (sg * (1.0 + gg * (1.0 - sg)))
        d_hgn = sum(_heads(jnp.sum(don * n, axis=0, keepdims=True)))
        dn = don * hgn
        do = _wide(
            [
                lax.rsqrt(jnp.mean(oh * oh, axis=-1, keepdims=True) + EPS) * (dnh - nh * jnp.mean(dnh * nh, axis=-1, keepdims=True))
                for oh, dnh, nh in zip(_heads(o), _heads(dn), _heads(n))
            ]
        )
        sts = [st_ref[j] for j in hs]
        dstn = [dst_scr[j] for j in hs]
        bl = b[CHUNK - 1 : CHUNK, :]
        e_b, e_bl, e_l = jnp.exp(b), jnp.exp(bl - b), jnp.exp(bl)
        doh, vih = _heads(do), _heads(vi)
        dobh = _heads(do.astype(BF16))
        dq_acc = _wide([_dot3(doh[j], sts[j]) for j in hs]) * e_b
        dk_inter = _wide([_dot3(vih[j], dstn[j]) for j in hs]) * e_bl
        dk_acc = dk_inter
        kd = _heads((kk * e_bl).astype(BF16))
        dv_acc = _wide([_dot(kd[j], dstn[j].astype(BF16), NT) for j in hs])
        qe, decay = _heads((qf * e_b).astype(BF16)), _heads(e_l)
        dst_new = [dstn[j] * decay[j] + _dot(dobh[j], qe[j], TN) for j in hs]
        db_last = e_l * _wide([jnp.sum(sts[j] * dstn[j], axis=0, keepdims=True) for j in hs]) + jnp.sum(kk * dk_inter, axis=0, keepdims=True)
        dq_parts = []
        for lo, hi, ea, eb, mask in _intra_blocks(b):
            a, bk = qf[lo:hi] * ea, kk[:hi] * eb
            ah, bkh = _heads(a), _heads(bk)
            abh, bkbh = _heads(a.astype(BF16)), _heads(bk.astype(BF16))
            p = [jnp.where(mask, _dot(abh[j], bkbh[j], NT), 0.0).astype(BF16) for j in hs]
            dp = [jnp.where(mask, _dot3(doh[j][lo:hi], vih[j][:hi], NT), 0.0) for j in hs]
            dq_parts.append(_wide([_dot3(dp[j], bkh[j]) for j in hs]) * ea)
            dki = _wide([_dot3(dp[j], ah[j], TN) for j in hs]) * eb
            dvi = _wide([_dot(p[j], dobh[j][lo:hi], TN) for j in hs])
            if hi < CHUNK:
                zeros = jnp.zeros((CHUNK - hi, HG_BLOCK * HEAD_DIM), F32)
                dki = jnp.concatenate([dki, zeros], axis=0)
                dvi = jnp.concatenate([dvi, zeros], axis=0)
            dk_acc = dk_acc + dki
            dv_acc = dv_acc + dvi
        dq_acc = dq_acc + jnp.concatenate(dq_parts, axis=0)
        rows = lax.broadcasted_iota(jnp.int32, dq_acc.shape, 0)
        db = qf * dq_acc - kk * dk_acc + jnp.where(rows == CHUNK - 1, db_last, 0.0)
        dlf = jnp.dot(_tri(CHUNK, True), db, precision=HIGHEST, preferred_element_type=F32)
        dfk = dlf / f - dk_acc
        for k, part in enumerate((dq_acc * (sq * (1.0 + q * (1.0 - sq))), (1.0 - lb) * dfk * sig * (1.0 - sig), dv_acc, dgg)):
            dp_ref[:, k * HG_WIDTH : (k + 1) * HG_WIDTH] = part.astype(BF16)
        dlb_scr[...] += jnp.sum(dfk * (1.0 - sig), axis=0, keepdims=True)
        dhgn_ref[...] += d_hgn
        for j in hs:
            dst_scr[j] = dst_new[j]

        @pl.when(c == nc - 1)
        def _():
            dl0 = dlb_scr[...] * lb * (1.0 - lb)
            dlbl_ref[0:1, :] = dl0
            dlbl_ref[1:2, :] = -dl0

    wide = HG_BLOCK * HEAD_DIM
    groups = HG_HEADS // HG_BLOCK

    def col(k):
        return pl.BlockSpec((CHUNK, wide), lambda g, c: (nc - 1 - c, k * groups + g))

    blk = pl.BlockSpec((CHUNK, wide), lambda g, c: (nc - 1 - c, g))
    assert groups == 1, "d(q, f, i, g) are written as one contiguous column range of the in_proj gradient"
    return _pallas(
        body,
        9,
        dep,
        name="hgrn_bwd",
        grid=(groups, nc),
        in_specs=[
            col(0), col(1), col(2), col(3),
            pl.BlockSpec((2, wide), lambda g, c: (0, g)),
            pl.BlockSpec((1, HEAD_DIM), lambda g, c: (0, 0)),
            blk, blk,
            pl.BlockSpec((HG_BLOCK, None, HEAD_DIM, HEAD_DIM), lambda g, c: (g, nc - 1 - c, 0, 0)),
        ],
        out_specs=[
            pl.BlockSpec((CHUNK, 4 * HG_WIDTH), lambda g, c: (nc - 1 - c, 0)),
            pl.BlockSpec((2, wide), lambda g, c: (0, g)),
            pl.BlockSpec((1, HEAD_DIM), lambda g, c: (0, 0)),
        ],
        out_shape=[
            jax.ShapeDtypeStruct((t, 4 * HG_WIDTH + 3 * ATT_WIDTH), BF16),
            jax.ShapeDtypeStruct((2, HG_WIDTH), F32),
            jax.ShapeDtypeStruct((1, HEAD_DIM), F32),
        ],
        scratch_shapes=[pltpu.VMEM((HG_BLOCK, HEAD_DIM, HEAD_DIM), F32), pltpu.VMEM((1, wide), F32)],
        compiler_params=_params("arbitrary", "arbitrary"),
    )(proj, proj, proj, proj, lb_logits, hg_norm, o_hg, dycat, states)


def _diagonal_slots(shift):
    i = lax.broadcasted_iota(jnp.int32, (N_REL_PAD, DIAG), 0)
    u = lax.broadcasted_iota(jnp.int32, (N_REL_PAD, DIAG), 1)
    offset = u - shift if shift else jnp.where(u < K_BLOCK, u, u - DIAG)
    return jnp.where(jnp.clip(PAD - offset, -REL_CLIP, REL_CLIP) + REL_CLIP == i, 1.0, 0.0).astype(BF16)


def _split3(x):
    hi = x.astype(BF16)
    mid = (x - hi.astype(F32)).astype(BF16)
    return hi, mid, (x - hi.astype(F32) - mid.astype(F32)).astype(BF16)


def _bias_table(rel_bias, dep=None):
    def body(rb_ref, o_ref, diag):
        h = pl.program_id(0)

        @pl.when(h == 0)
        def _():
            hi, mid, lo = _split3(rb_ref[...])
            slots = _diagonal_slots(0)
            diag[...] = _dot(hi, slots) + (_dot(mid, slots) + _dot(lo, slots))

        rows = jnp.broadcast_to(diag[pl.ds(h, 1), :], (Q_BLOCK, DIAG))
        row = lax.broadcasted_iota(jnp.int32, (Q_BLOCK, K_BLOCK), 0)
        col = lax.broadcasted_iota(jnp.int32, (Q_BLOCK, K_BLOCK), 1)
        first = row - (row & (CHUNK - 1))
        seen = (col >= first) & (col < first + BAND)
        o_ref[...] = jnp.where(seen, pltpu.roll(rows, 0, 1, stride=1, stride_axis=0)[:, :K_BLOCK], MASKED)

    return _pallas(
        body,
        1,
        dep,
        name="bias_table",
        grid=(ATT_HEADS,),
        in_specs=[pl.BlockSpec((ATT_HEADS, N_REL_PAD), lambda h: (0, 0))],
        out_specs=pl.BlockSpec((None, Q_BLOCK, K_BLOCK), lambda h: (h, 0, 0)),
        out_shape=jax.ShapeDtypeStruct((ATT_HEADS, Q_BLOCK, K_BLOCK), F32),
        scratch_shapes=[pltpu.VMEM((ATT_HEADS, DIAG), F32)],
        compiler_params=_params("arbitrary"),
    )(rel_bias)


def _att_probs(q_ref, kpad, bias_ref, blk, cols=slice(None), head=None):
    qs = (q_ref[:, cols] * ATT_SCALE).astype(BF16)
    start = pl.multiple_of(blk * Q_BLOCK, Q_BLOCK)
    kb = kpad[pl.ds(start, K_BLOCK), cols]
    s = _dot(qs, kb, NT) + (bias_ref[...] if head is None else bias_ref[head])
    col = lax.broadcasted_iota(jnp.int32, (Q_BLOCK, K_BLOCK), 1)
    s = jnp.where(col >= PAD - blk * Q_BLOCK, s, MASKED)
    e = jnp.exp(s - jnp.max(s, axis=-1, keepdims=True))
    return qs, kb, start, e * (1.0 / jnp.sum(e, axis=-1, keepdims=True))


def _fill_padded(dst, src):
    dst[0:PAD, :] = jnp.zeros((PAD, dst.shape[1]), BF16)
    dst[PAD:, :] = src[...].astype(BF16)


def _att_fwd(proj, bias, y_mix, dep=None):
    t = proj.shape[0]
    nb = t // Q_BLOCK

    def body(q_ref, k_ref, v_ref, bias_ref, y_in_ref, y_ref, kpad, vpad):
        c = pl.program_id(1)

        @pl.when(c == 0)
        def _():
            _fill_padded(kpad, k_ref)
            _fill_padded(vpad, v_ref)

        for j in range(ATT_BLOCK):
            cols = slice(j * HEAD_DIM, (j + 1) * HEAD_DIM)
            _, _, start, p = _att_probs(q_ref, kpad, bias_ref, c, cols, j)
            y_ref[:, cols] = _dot(p.astype(BF16), vpad[pl.ds(start, K_BLOCK), cols]).astype(BF16)

    wide = ATT_BLOCK * HEAD_DIM
    base = 4 * HG_HEADS // ATT_BLOCK
    groups = ATT_HEADS // ATT_BLOCK
    return _pallas(
        body,
        5,
        dep,
        name="att_fwd",
        grid=(groups, nb),
        in_specs=[
            pl.BlockSpec((Q_BLOCK, wide), lambda g, c: (c, base + g)),
            pl.BlockSpec((t, wide), lambda g, c: (0, base + groups + g)),
            pl.BlockSpec((t, wide), lambda g, c: (0, base + 2 * groups + g)),
            pl.BlockSpec((ATT_BLOCK, Q_BLOCK, K_BLOCK), lambda g, c: (g, 0, 0)),
            ANY,
        ],
        out_specs=pl.BlockSpec((Q_BLOCK, wide), lambda g, c: (c, HG_HEADS // ATT_BLOCK + g)),
        out_shape=jax.ShapeDtypeStruct(y_mix.shape, BF16),
        input_output_aliases={4: 0},
        scratch_shapes=[pltpu.VMEM((t + PAD, wide), BF16), pltpu.VMEM((t + PAD, wide), BF16)],
        compiler_params=_params("arbitrary", "arbitrary"),
    )(proj, proj, proj, bias, y_mix)


def _att_bwd(proj, bias, dycat, dproj, dep=None):
    t = proj.shape[0]
    nb = t // Q_BLOCK

    def body(q_ref, k_ref, v_ref, bias_ref, dy_ref, dp_in_ref, dq_ref, dk_ref, dv_ref, g_ref, kpad, vpad, dkacc, dvacc):
        c = pl.program_id(1)

        @pl.when(c == 0)
        def _():
            _fill_padded(kpad, k_ref)
            _fill_padded(vpad, v_ref)
            dkacc[...] = jnp.zeros_like(dkacc)
            dvacc[...] = jnp.zeros_like(dvacc)
            g_ref[...] = jnp.zeros_like(g_ref)

        for j in range(ATT_BLOCK):
            cols = slice(j * HEAD_DIM, (j + 1) * HEAD_DIM)
            qs, kb, start, p = _att_probs(q_ref, kpad, bias_ref, c, cols, j)
            band = pl.ds(start, K_BLOCK)
            dyb = dy_ref[:, cols].astype(BF16)
            dvacc[band, cols] += _dot(p.astype(BF16), dyb, TN)
            dp = _dot(dyb, vpad[band, cols], NT)
            ds = p * (dp - jnp.sum(dp * p, axis=-1, keepdims=True))
            g_ref[j] += ds
            dsb = ds.astype(BF16)
            dq_ref[:, cols] = (_dot(dsb, kb) * ATT_SCALE).astype(BF16)
            dkacc[band, cols] += _dot(dsb, qs, TN)

        @pl.when(c == nb - 1)
        def _():
            dk_ref[...] = dkacc[PAD:, :].astype(BF16)
            dv_ref[...] = dvacc[PAD:, :].astype(BF16)

    wide = ATT_BLOCK * HEAD_DIM
    base = 4 * HG_HEADS // ATT_BLOCK
    groups = ATT_HEADS // ATT_BLOCK
    whole = pl.BlockSpec((t, wide), lambda g, c: (0, g))
    table = pl.BlockSpec((ATT_BLOCK, Q_BLOCK, K_BLOCK), lambda g, c: (g, 0, 0))
    return _pallas(
        body,
        6,
        dep,
        name="att_bwd",
        grid=(groups, nb),
        in_specs=[
            pl.BlockSpec((Q_BLOCK, wide), lambda g, c: (c, base + g)),
            pl.BlockSpec((t, wide), lambda g, c: (0, base + groups + g)),
            pl.BlockSpec((t, wide), lambda g, c: (0, base + 2 * groups + g)),
            table,
            pl.BlockSpec((Q_BLOCK, wide), lambda g, c: (c, HG_HEADS // ATT_BLOCK + g)),
            ANY,
        ],
        out_specs=[pl.BlockSpec((Q_BLOCK, wide), lambda g, c: (c, base + g)), whole, whole, table],
        input_output_aliases={5: 0},
        out_shape=[
            jax.ShapeDtypeStruct(dproj.shape, BF16),
            jax.ShapeDtypeStruct((t, ATT_WIDTH), BF16),
            jax.ShapeDtypeStruct((t, ATT_WIDTH), BF16),
            jax.ShapeDtypeStruct((ATT_HEADS, Q_BLOCK, K_BLOCK), F32),
        ],
        scratch_shapes=[
            pltpu.VMEM((t + PAD, wide), BF16),
            pltpu.VMEM((t + PAD, wide), BF16),
            pltpu.VMEM((t + PAD, wide), F32),
            pltpu.VMEM((t + PAD, wide), F32),
        ],
        compiler_params=pltpu.CompilerParams(dimension_semantics=("arbitrary", "arbitrary"), vmem_limit_bytes=ATT_BWD_VMEM),
    )(proj, proj, proj, bias, dycat, dproj)


def _rel_bias_grad(gsum):
    def body(g_ref, o_ref):
        r = lax.broadcasted_iota(jnp.int32, (Q_BLOCK, Q_BLOCK), 0)
        c = lax.broadcasted_iota(jnp.int32, (Q_BLOCK, Q_BLOCK), 1)
        flip = jnp.where(r + c == Q_BLOCK - 1, 1.0, 0.0).astype(BF16)
        sums = []
        for h in range(ATT_HEADS):
            hi, mid, lo = _split3(g_ref[h])
            rev = _dot(flip, hi) + (_dot(flip, mid) + _dot(flip, lo))
            wide = jnp.concatenate([rev, jnp.zeros((Q_BLOCK, DIAG - K_BLOCK), F32)], axis=1)
            sums.append(jnp.sum(pltpu.roll(wide, 0, 1, stride=1, stride_axis=0), axis=0, keepdims=True))
        hi, mid, lo = _split3(jnp.concatenate(sums, axis=0))
        slots = _diagonal_slots(Q_BLOCK - 1)
        o_ref[...] = _dot(hi, slots, NT) + (_dot(mid, slots, NT) + _dot(lo, slots, NT))

    return pl.pallas_call(
        body,
        name="rel_bias_grad",
        out_shape=jax.ShapeDtypeStruct((ATT_HEADS, N_REL_PAD), F32),
        compiler_params=_params(),
    )(gsum)


HALO = 16


FF_TILE = 1408
FF_TILES = D_FF // FF_TILE


def _interleave_cols(a):
    lead = a.shape[:-1]
    return jnp.swapaxes(a.reshape(*lead, 2, FF_TILES, FF_TILE), -3, -2).reshape(*lead, 2 * D_FF)


def _deinterleave_cols(a):
    lead = a.shape[:-1]
    return jnp.swapaxes(a.reshape(*lead, FF_TILES, 2, FF_TILE), -3, -2).reshape(*lead, 2 * D_FF)


def _ffn_specs(t, tm):
    wide = 2 * FF_TILE
    tile = pl.BlockSpec((tm, wide), lambda j, i: (i, j))
    before = pl.BlockSpec((HALO, wide), lambda j, i: (jnp.maximum(i * (tm // HALO) - 1, 0), j))
    after = pl.BlockSpec((HALO, wide), lambda j, i: (jnp.minimum((i + 1) * (tm // HALO), t // HALO - 1), j))
    vec = lambda rows: pl.BlockSpec((rows, wide), lambda j, i: (0, j))
    return tile, before, after, vec


def _shifted(x, rows, offsets):
    r = lax.broadcasted_iota(jnp.int32, (rows, x.shape[0]), 0)
    c = lax.broadcasted_iota(jnp.int32, (rows, x.shape[0]), 1)
    pick = jnp.concatenate([jnp.where(c == r + o, 1.0, 0.0).astype(BF16) for o in offsets], axis=0)
    out = _dot(pick, x)
    return [out[k * rows : (k + 1) * rows] for k in range(len(offsets))]


def _conv(x, w, b, rows):
    taps = _shifted(x, rows, [HALO - 2, HALO - 1]) + [x[HALO : HALO + rows].astype(F32)]
    return b + w[0:1] * taps[0] + w[1:2] * taps[1] + w[2:3] * taps[2], taps


def _ffn_act_fwd(u, conv_w, conv_b):
    t = u.shape[0]
    tm = _tile(t, (128,))
    tile, before, _, vec = _ffn_specs(t, tm)

    def body(u_ref, h_ref, w_ref, b_ref, z_ref):
        first = pl.program_id(1) == 0
        halo = h_ref[...]
        x = jnp.concatenate([jnp.where(first, jnp.zeros_like(halo), halo), u_ref[...]], axis=0)
        c, _ = _conv(x, w_ref[...], b_ref[...], tm)
        gate, val = c[:, :FF_TILE], c[:, FF_TILE:]
        z_ref[...] = (gate * _sigmoid(gate) * val).astype(BF16)

    return pl.pallas_call(
        body,
        name="ffn_act_fwd",
        grid=(FF_TILES, t // tm),
        in_specs=[tile, before, vec(3), vec(1)],
        out_specs=pl.BlockSpec((tm, FF_TILE), lambda j, i: (i, j)),
        out_shape=jax.ShapeDtypeStruct((t, D_FF), BF16),
        compiler_params=_params("parallel", "parallel"),
    )(u, u, conv_w, conv_b)


def _ffn_act_bwd(u, dz, conv_w, conv_b, dep=None):
    t = u.shape[0]
    tm = _tile(t, (128,))
    nt = t // tm
    ext = tm + HALO
    tile, before, after, vec = _ffn_specs(t, tm)

    def body(u_ref, ub_ref, ua_ref, w_ref, b_ref, dz_ref, dza_ref, du_ref, dw_ref, db_ref):
        i = pl.program_id(1)
        first, last = i == 0, i == nt - 1
        ub, ua = ub_ref[...], ua_ref[...]
        parts = [jnp.where(first, jnp.zeros_like(ub), ub), u_ref[...], jnp.where(last, jnp.zeros_like(ua), ua)]
        w = w_ref[...]
        c, taps = _conv(jnp.concatenate(parts, axis=0), w, b_ref[...], ext)
        gate, val = c[:, :FF_TILE], c[:, FF_TILE:]
        dz = jnp.concatenate([dz_ref[...].astype(F32), jnp.where(last, 0.0, dza_ref[...].astype(F32))], axis=0)
        sg = _sigmoid(gate)
        d = jnp.concatenate([dz * val * (sg * (1.0 + gate * (1.0 - sg))), dz * (gate * sg)], axis=1)
        d1, d2 = _shifted(d.astype(BF16), tm, [1, 2])
        du_ref[...] = (w[2:3] * d[:tm] + w[1:2] * d1 + w[0:1] * d2).astype(BF16)

        @pl.when(first)
        def _():
            dw_ref[...] = jnp.zeros_like(dw_ref)
            db_ref[...] = jnp.zeros_like(db_ref)

        for k, tap in enumerate(taps):
            dw_ref[k : k + 1, :] += jnp.sum(d[:tm] * tap[:tm], axis=0, keepdims=True)
        db_ref[...] += jnp.sum(d[:tm], axis=0, keepdims=True)

    narrow = lambda rows, index: pl.BlockSpec((rows, FF_TILE), index)
    return _pallas(
        body,
        7,
        dep,
        name="ffn_act_bwd",
        grid=(FF_TILES, nt),
        in_specs=[
            tile, before, after, vec(3), vec(1),
            narrow(tm, lambda j, i: (i, j)),
            narrow(HALO, lambda j, i: (jnp.minimum((i + 1) * (tm // HALO), t // HALO - 1), j)),
        ],
        out_specs=[tile, vec(3), vec(1)],
        out_shape=[
            jax.ShapeDtypeStruct((t, 2 * D_FF), BF16),
            jax.ShapeDtypeStruct((3, 2 * D_FF), F32),
            jax.ShapeDtypeStruct((1, 2 * D_FF), F32),
        ],
        compiler_params=_params("parallel", "arbitrary"),
    )(u, u, u, conv_w, conv_b, dz, dz)


def _ple_loss(gpre, pp, h2, final_norm, target):
    t, d = h2.shape
    tm = _tile(t, (256,))

    def body(gp_ref, pp_ref, h_ref, g_ref, tg_ref, dh_ref, dgp_ref, dpp_ref, dg_ref, loss_ref):
        i = pl.program_id(0)
        gate = _sigmoid(gp_ref[...])
        ppv = pp_ref[...]
        h3 = h_ref[...] + gate * ppv
        r = lax.rsqrt(jnp.mean(h3 * h3, axis=-1, keepdims=True) + EPS)
        n = h3 * r
        g = g_ref[...]
        err = n * g - tg_ref[...]
        loss = 0.5 * jnp.sum(jnp.mean(err * err, axis=-1, keepdims=True))
        dy = err * (1.0 / d)
        dn = dy * g
        dh = r * (dn - n * jnp.mean(dn * n, axis=-1, keepdims=True))
        dh_ref[...] = dh
        dgp_ref[...] = (dh * ppv * gate * (1.0 - gate)).astype(BF16)
        dpp_ref[...] = (dh * gate).astype(BF16)
        dg = jnp.sum(dy * n, axis=0, keepdims=True)

        @pl.when(i == 0)
        def _():
            dg_ref[...] = dg
            loss_ref[...] = jnp.full(loss_ref.shape, loss, F32)

        @pl.when(i > 0)
        def _():
            dg_ref[...] += dg
            loss_ref[...] += loss

    row = pl.BlockSpec((tm, d), lambda i: (i, 0))
    vec = pl.BlockSpec((1, d), lambda i: (0, 0))
    return pl.pallas_call(
        body,
        name="ple_loss",
        grid=(t // tm,),
        in_specs=[row, row, row, vec, row],
        out_specs=[row, row, row, vec, pl.BlockSpec((8, 128), lambda i: (0, 0))],
        out_shape=[
            jax.ShapeDtypeStruct((t, d), F32),
            jax.ShapeDtypeStruct((t, d), BF16),
            jax.ShapeDtypeStruct((t, d), BF16),
            jax.ShapeDtypeStruct((1, d), F32),
            jax.ShapeDtypeStruct((8, 128), F32),
        ],
        compiler_params=_params("arbitrary"),
    )(gpre, pp, h2, final_norm, target)


def _adamw(w, g, m, v):
    m = ADAM_B1 * m + (1.0 - ADAM_B1) * g
    v = ADAM_B2 * v + (1.0 - ADAM_B2) * (g * g)
    m_hat = m / (1.0 - ADAM_B1 ** ADAM_STEP)
    v_hat = v / (1.0 - ADAM_B2 ** ADAM_STEP)
    return -ADAM_LR * (m_hat / (jnp.sqrt(v_hat) + ADAM_EPS) + ADAM_WD * w), m, v


def _adam_big(w, m, v, own, recv, name, dep=None):
    r, c = w.shape
    tr = _tile(r, (256, 176))

    def body(w_ref, m_ref, v_ref, own_ref, recv_ref, g_ref, d_ref, nm_ref, nv_ref):
        g = own_ref[...]
        for k in range(3):
            g = g + recv_ref[k].astype(F32)
        g_ref[...] = g
        d_ref[...], nm_ref[...], nv_ref[...] = _adamw(w_ref[...], g, m_ref[...], v_ref[...])

    blk = pl.BlockSpec((tr, c), lambda i: (i, 0))
    return _pallas(
        body,
        5,
        dep,
        name=name,
        grid=(r // tr,),
        in_specs=[blk, blk, blk, blk, pl.BlockSpec((3, tr, c), lambda i: (0, i, 0))],
        out_specs=[blk] * 4,
        out_shape=[jax.ShapeDtypeStruct((r, c), F32)] * 4,
        compiler_params=_params("parallel"),
    )(w, m, v, own, recv)


def _adam_small(w, g, m, v):
    def body(w_ref, g_ref, m_ref, v_ref, d_ref, nm_ref, nv_ref):
        d_ref[...], nm_ref[...], nv_ref[...] = _adamw(w_ref[...], g_ref[...], m_ref[...], v_ref[...])

    return pl.pallas_call(body, name="adam_small", out_shape=[jax.ShapeDtypeStruct(w.shape, F32)] * 3, compiler_params=_params())(w, g, m, v)


def _position():
    return lax.axis_index("x"), lax.axis_index("y"), lax.axis_index("c")


def _other_chips(x, y):
    return [(1 - x, y), (x, 1 - y), (1 - x, 1 - y)]


def _block_index(dev, interleaved):
    x, y, c = dev
    return 4 * y + 2 * c + x if interleaved else 4 * x + 2 * y + c


def _shard_of(ref, axis, size, dev, interleaved=False):
    start = pl.multiple_of(_block_index(dev, interleaved) * size, 128 if axis == 1 else 16)
    return ref.at[:, pl.ds(start, size)] if axis == 1 else ref.at[pl.ds(start, size), :]


def _add_blocks(ids, grad, landed, axis, size, targets, out_dtype, name):
    rows = size if axis == 0 else grad.shape[0]
    cols = size if axis == 1 else grad.shape[1]
    tr = _tile(rows, (256, 176))
    nr = rows // tr
    nt = len(targets)

    def body(ids_ref, g_ref, l_ref, o_ref):
        o_ref[...] = (g_ref[...] + l_ref[...]).astype(out_dtype)

    if axis == 1:
        g_spec = pl.BlockSpec((tr, cols), lambda k, i, ids: (i, ids[targets[0] + k]))
    else:
        g_spec = pl.BlockSpec((tr, cols), lambda k, i, ids: (ids[targets[0] + k] * nr + i, 0))
    return pl.pallas_call(
        body,
        name=name,
        grid_spec=pltpu.PrefetchScalarGridSpec(
            num_scalar_prefetch=1,
            grid=(nt, nr),
            in_specs=[g_spec, pl.BlockSpec((None, tr, cols), lambda k, i, ids: (ids[4 + targets[0] + k], i, 0))],
            out_specs=pl.BlockSpec((None, tr, cols), lambda k, i, ids: (k, i, 0)),
        ),
        out_shape=jax.ShapeDtypeStruct((nt, rows, cols), out_dtype),
        compiler_params=_params("parallel", "parallel"),
    )(ids, grad, landed)


def _all_reduce_small(vec, name):
    rows = vec.shape[0]

    def body(v_ref, o_ref, land, send_sems, recv_sems):
        x, y, c = _position()
        mine = 4 * x + 2 * y + c
        copies = []
        for mask in range(1, N_DEV):
            peer = (1 - x if mask & 4 else x, 1 - y if mask & 2 else y, 1 - c if mask & 1 else c)
            copies.append(
                pltpu.make_async_remote_copy(
                    src_ref=v_ref, dst_ref=land.at[mine], send_sem=send_sems.at[mask - 1], recv_sem=recv_sems.at[mask - 1], device_id=peer, device_id_type=MESH
                )
            )
        for cp in copies:
            cp.start()
        land[mine] = v_ref[...]
        for cp in copies:
            cp.wait()
        acc = land[0]
        for k in range(1, N_DEV):
            acc = acc + land[k]
        o_ref[...] = acc

    return pl.pallas_call(
        body,
        name=name,
        out_shape=jax.ShapeDtypeStruct(vec.shape, F32),
        in_specs=[pl.BlockSpec(memory_space=pltpu.VMEM)],
        out_specs=pl.BlockSpec(memory_space=pltpu.VMEM),
        scratch_shapes=[pltpu.VMEM((N_DEV, rows, 128), F32), pltpu.SemaphoreType.DMA((N_DEV - 1,)), pltpu.SemaphoreType.DMA((N_DEV - 1,))],
    )(vec)


def _rows128(a, rows):
    flat = a.reshape(-1)
    return jnp.pad(flat, (0, rows * 128 - flat.shape[0])).reshape(rows, 128)


def _pad_rel(a):
    return jnp.pad(a.reshape(ATT_HEADS, -1)[:, :N_REL], ((0, 0), (0, N_REL_PAD - N_REL)))


SMALL = [("norm_mix", 16), ("lb_logits", 16), ("hg_norm", 8), ("rel_bias", 24), ("norm_ffn", 16), ("conv_b", 88), ("norm_ple", 16), ("final_norm", 16)]
CONV_W_FULL_ROWS = 3 * 2 * D_FF // 128
CONV_W_SHARD_ROWS = 40


def _pack_small(parts):
    return jnp.concatenate([_rows128(_pad_rel(parts[k]) if k == "rel_bias" else parts[k], rows) for k, rows in SMALL], axis=0)


def _unpack_small(packed, shapes):
    out, at = {}, 0
    for k, rows in SMALL:
        blk = packed[at : at + rows]
        at += rows
        if k == "rel_bias":
            out[k] = blk.reshape(ATT_HEADS, N_REL_PAD)[:, :N_REL].reshape(shapes[k])
        else:
            n = 1
            for s in shapes[k]:
                n *= s
            out[k] = blk.reshape(-1)[:n].reshape(shapes[k])
    return out, at


BIG = [("w_in", 1), ("w_out", 0), ("w_up", 1), ("w_down", 0), ("w_ple_gate", 0), ("w_ple_proj", 1)]


HBM = pl.BlockSpec(memory_space=pltpu.HBM)
SEM = pl.BlockSpec(memory_space=pltpu.SEMAPHORE)
EFFECT = pltpu.SideEffectType.DATAFLOW_SIDE_EFFECTING


def _copies(plan, refs, send_sems, recv_sems):
    return [
        pltpu.make_async_remote_copy(src_ref=src, dst_ref=dst, send_sem=send_sems.at[i], recv_sem=recv_sems.at[i], device_id=dev, device_id_type=MESH)
        for i, (src, dst, dev) in enumerate(plan(refs))
    ]


def _split_start(name, arrays, plan, n):
    k = len(arrays)

    def body(*refs):
        for cp in _copies(plan, refs[:k], refs[k], refs[k + 1]):
            cp.start()
        refs[-1][...] = jnp.zeros_like(refs[-1])

    out = pl.pallas_call(
        body,
        name=name,
        out_shape=(pltpu.SemaphoreType.DMA((n,)), pltpu.SemaphoreType.DMA((n,)), *[pltpu.HBM(a.shape, a.dtype) for a in arrays], jax.ShapeDtypeStruct((8, 128), F32)),
        in_specs=[HBM] * k,
        out_specs=(SEM, SEM, *[HBM] * k, pl.BlockSpec(memory_space=pltpu.VMEM)),
        input_output_aliases={i: 2 + i for i in range(k)},
        compiler_params=pltpu.CompilerParams(has_side_effects=EFFECT),
    )(*[pltpu.with_memory_space_constraint(a, pltpu.HBM) for a in arrays])
    return out[0], out[1], list(out[2 : 2 + k]), out[-1]


def _split_wait(name, send, recv, arrays, plan, after):
    k = len(arrays)

    def body(*refs):
        for cp in _copies(plan, refs[:k], refs[k], refs[k + 1]):
            cp.wait_send()
            cp.wait_recv()

    out = pl.pallas_call(
        body,
        name=name,
        out_shape=tuple(pltpu.HBM(a.shape, a.dtype) for a in arrays),
        in_specs=[HBM] * k + [SEM, SEM, ANY],
        out_specs=tuple([HBM] * k),
        input_output_aliases={i: i for i in range(k)},
        compiler_params=pltpu.CompilerParams(has_side_effects=EFFECT),
    )(*arrays, send, recv, after)
    return list(out)


def _cast_into(w, me, axis, name, dep, dtype):
    r, c = w.shape
    tr = _tile(r, (256, 176))
    nr = r // tr
    deps = [] if dep is None else [dep]

    def body(me_ref, w_ref, *rest):
        rest[-1][...] = w_ref[...].astype(dtype)

    if axis == 1:
        shape, o_spec = (r, N_DEV * c), pl.BlockSpec((tr, c), lambda i, me: (i, me[0]))
    else:
        shape, o_spec = (N_DEV * r, c), pl.BlockSpec((tr, c), lambda i, me: (me[0] * nr + i, 0))
    return pl.pallas_call(
        body,
        name=name,
        grid_spec=pltpu.PrefetchScalarGridSpec(
            num_scalar_prefetch=1, grid=(nr,), in_specs=[pl.BlockSpec((tr, c), lambda i, me: (i, 0))] + [ANY] * len(deps), out_specs=o_spec
        ),
        out_shape=jax.ShapeDtypeStruct(shape, dtype),
        compiler_params=_params("parallel"),
    )(me, w, *deps)


GATHER = [
    (["w_in"], None, "norm_mix_fwd", "bias_table"),
    (["w_out"], "norm_mix_fwd", "att_fwd", None),
    (["w_up", "conv_w"], "norm_mix_fwd", "att_fwd", "norm_ffn_fwd"),
    (["w_down", "w_ple_gate", "w_ple_proj"], "att_fwd", "up_proj", "ffn_act_fwd"),
]
GROUPS = [["w_ple_proj", "w_ple_gate", "w_down"], ["w_up"], ["w_out"], ["w_in"]]
STAGES = ["ffn_act_bwd", "d_mix_out", "hgrn_bwd", "d_norm_mix_out"]
INTERLEAVED = {"w_up", "conv_w"}


class _Exchange:
    def __init__(self, big, conv_w, position):
        self.big, self.axis = big, dict(BIG, conv_w=1)
        self.shards = dict(big, conv_w=conv_w)
        self.size = {k: w.shape[self.axis[k]] for k, w in self.shards.items()}
        self.x, self.y, self.c = position
        chips = [(self.x, self.y)] + _other_chips(self.x, self.y)
        landed = [2 * cx + cy for cx, cy in chips]
        self.ids = {
            flag: jnp.stack([_block_index((cx, cy, self.c), flag) for cx, cy in chips] + landed).astype(jnp.int32) for flag in (False, True)
        }
        self.tokens, self.grads, self.state, self.wfull = [], {}, {}, {}


    def _slot(self, ref, k, dev):
        return _shard_of(ref, self.axis[k], self.size[k], dev, interleaved=k in INTERLEAVED)

    def _plan_gather(self, names, direct, refs):
        x, y, c = _position()
        me, out = (x, y, c), []
        for k, ref in zip(names, refs):
            mine = self._slot(ref, k, me)
            out.append((mine, mine, (x, y, 1 - c)))
            out += [(mine, mine, (*chip, c)) for chip in _other_chips(x, y)]
            if direct:
                out += [(mine, mine, (*chip, 1 - c)) for chip in _other_chips(x, y)]
        return out

    def _plan_forward(self, names, refs):
        x, y, c = _position()
        out = []
        for k, ref in zip(names, refs):
            for chip in _other_chips(x, y):
                block = self._slot(ref, k, (*chip, c))
                out.append((block, block, (x, y, 1 - c)))
        return out

    def _plan_sibling(self, names, refs):
        x, y, c = _position()
        n = len(names)
        return [(self._slot(refs[i], k, (p // 2, p % 2, 1 - c)), refs[n + i].at[p], (x, y, 1 - c)) for i, k in enumerate(names) for p in range(4)]

    def _plan_chips(self, names, refs):
        x, y, c = _position()
        n = len(names)
        return [(refs[i].at[j], refs[n + i].at[j], (*chip, c)) for i in range(n) for j, chip in enumerate(_other_chips(x, y))]


    def gather(self):
        me = {flag: _block_index((self.x, self.y, self.c), flag).astype(jnp.int32).reshape(1) for flag in (False, True)}
        self.late, self.unsent = {}, {}
        after = None
        for gi, (names, issued, *_) in enumerate(GATHER):
            self.unsent[gi] = [
                _cast_into(self.shards[k], me[k in INTERLEAVED], self.axis[k], "cast_" + k, after, F32 if k == "conv_w" else BF16) for k in names
            ]
            if issued is None:
                self._issue(None)
                after = self.tokens[-1]
        self.tokens += [a for arrays in self.unsent.values() for a in arrays]

    def _issue(self, stage):
        for gi, (names, issued, _, forwarded) in enumerate(GATHER):
            if issued == stage and gi in self.unsent:
                plan = functools.partial(self._plan_gather, names, forwarded is None)
                copies = (7 if forwarded is None else 4) * len(names)
                send, recv, fulls, token = _split_start(f"gather_start_{gi}", self.unsent.pop(gi), plan, copies)
                self.tokens.append(token)
                self.late[gi] = (send, recv, fulls, plan)

    def weight(self, k):
        return self.wfull[k]

    def dep(self):
        tokens, self.tokens = self.tokens, []
        return tokens

    def reduce(self, vec, name):
        return _all_reduce_small(vec, name)

    def grad(self, k, g):
        self.grads[k] = g
        for gi, names in enumerate(GROUPS):
            if k == names[-1]:
                plan = functools.partial(self._plan_sibling, names)
                lands = [lax.empty((4, *self._shard_shape(n)), F32) for n in names]
                send, recv, arrays, token = _split_start(f"sibling_start_{gi}", [self.grads[n] for n in names] + lands, plan, 4 * len(names))
                self.tokens.append(token)
                self.state[gi] = (send, recv, arrays, plan)

    def done(self, stage, after):
        for gi, (names, _, _, forwarded) in enumerate(GATHER):
            if forwarded == stage:
                send, recv, fulls, plan = self.late[gi]
                self.wfull.update(zip(names, _split_wait(f"forward_wait_{gi}", send, recv, fulls, plan, after)))
        for gi, (names, _, arrived, forwarded) in enumerate(GATHER):
            if arrived == stage:
                send, recv, fulls, plan = self.late[gi]
                fulls = _split_wait(f"gather_wait_{gi}", send, recv, fulls, plan, after)
                if forwarded is None:
                    self.wfull.update(zip(names, fulls))
                else:
                    plan = functools.partial(self._plan_forward, names)
                    send, recv, fulls, token = _split_start(f"forward_start_{gi}", fulls, plan, 3 * len(names))
                    self.tokens.append(token)
                    self.late[gi] = (send, recv, fulls, plan)
        self._issue(stage)
        if stage in STAGES:
            self._to_chips(STAGES.index(stage), after)

    def _shard_shape(self, k):
        shape = list(self.grads[k].shape)
        shape[self.axis[k]] = self.size[k]
        return tuple(shape)

    def _to_chips(self, gi, after):
        names = GROUPS[gi]
        n = len(names)
        send, recv, arrays, plan = self.state[gi]
        arrays = _split_wait(f"sibling_wait_{gi}", send, recv, arrays, plan, after)
        own, parts = [], []
        for k, g, land in zip(names, arrays[:n], arrays[n:]):
            ids = self.ids[k in INTERLEAVED]
            own.append(_add_blocks(ids, g, land, self.axis[k], self.size[k], [0], F32, "add_own_" + k)[0])
            parts.append(_add_blocks(ids, g, land, self.axis[k], self.size[k], [1, 2, 3], BF16, "add_send_" + k))
        plan = functools.partial(self._plan_chips, names)
        lands = [lax.empty(part.shape, BF16) for part in parts]
        send, recv, arrays, token = _split_start(f"chips_start_{gi}", parts + lands, plan, 3 * n)
        self.tokens.append(token)
        self.state[gi] = (send, recv, arrays, plan, own)

    def finish(self, gi, after):
        names = GROUPS[gi]
        send, recv, arrays, plan, own = self.state[gi]
        arrays = _split_wait(f"chips_wait_{gi}", send, recv, arrays, plan, after)
        return {k: (o, r) for k, o, r in zip(names, own, arrays[len(names) :])}


class _Resident:
    def __init__(self, wfull):
        self.wfull, self.grads = wfull, {}

    def weight(self, k):
        return self.wfull[k]

    def grad(self, k, g):
        self.grads[k] = g

    def dep(self):
        return None

    def reduce(self, vec, name):
        return vec

    def done(self, stage, after):
        pass


def _local_step(x, p, target, small, ex):
    bias = _bias_table(jnp.pad(small["rel_bias"], ((0, 0), (0, N_REL_PAD - N_REL))), dep=ex.dep())
    a1, r1 = _rms_fwd(x, small["norm_mix"], "norm_mix_fwd", dep=[bias])
    ex.done("norm_mix_fwd", a1)
    ex.done("bias_table", a1)
    proj = _matmul(a1, ex.weight("w_in"), "nn", F32, "in_proj", dep=ex.dep())
    y_hg, o_hg, states = _hgrn_fwd(proj, small["lb_logits"], small["hg_norm"])
    ycat = _att_fwd(proj, bias, y_hg, dep=ex.dep())
    ex.done("att_fwd", ycat)
    h1 = _matmul(ycat, ex.weight("w_out"), "nn", F32, "out_proj", resid=x, dep=ex.dep())
    a2, r2 = _rms_fwd(h1, small["norm_ffn"], "norm_ffn_fwd")
    ex.done("norm_ffn_fwd", a2)
    conv_w = ex.weight("conv_w")
    u = _matmul(a2, ex.weight("w_up"), "nn", BF16, "up_proj")
    conv_b = _interleave_cols(small["conv_b"])
    ex.done("up_proj", u)
    z = _ffn_act_fwd(u, conv_w, conv_b)
    ex.done("ffn_act_fwd", z)
    h2 = _matmul(z, ex.weight("w_down"), "nn", F32, "down_proj", tk=2816, resid=h1)
    a3, r3 = _rms_fwd(h2, small["norm_ple"], "norm_ple_fwd")
    gpre = _matmul(a3, ex.weight("w_ple_gate"), "nn", F32, "ple_gate")
    pp = _matmul(p, ex.weight("w_ple_proj"), "nn", F32, "ple_proj")
    dh3, dgpre, dpp, d_final, loss = _ple_loss(gpre, pp, h2, small["final_norm"], target)

    ex.grad("w_ple_proj", _matmul(p, dpp, "tn", F32, "d_w_ple_proj", tm=512))
    ex.grad("w_ple_gate", _matmul(a3, dgpre, "tn", F32, "d_w_ple_gate", tm=512))
    da3 = _matmul(dgpre, ex.weight("w_ple_gate"), "nt", BF16, "d_norm_ple_out")
    dh2, dh2b, d_ple = _rms_bwd(da3, h2, r3, small["norm_ple"], dh3, "norm_ple_bwd")
    dz = _matmul(dh2b, ex.weight("w_down"), "nt", BF16, "d_ffn_act")
    ex.grad("w_down", _matmul(z, dh2b, "tn", F32, "d_w_down", tm=512))
    du, dcw, dcb = _ffn_act_bwd(u, dz, conv_w, conv_b, dep=ex.dep())
    ex.done("ffn_act_bwd", du)
    d_conv_w, d_conv_b = _deinterleave_cols(dcw), _deinterleave_cols(dcb)
    ex.grad("w_up", _matmul(a2, du, "tn", F32, "d_w_up", tm=512, dep=ex.dep()))
    da2 = _matmul(du, ex.weight("w_up"), "nt", BF16, "d_norm_ffn_out", tk=2816, dep=ex.dep())
    dh1, dh1b, d_ffn = _rms_bwd(da2, h1, r2, small["norm_ffn"], dh2, "norm_ffn_bwd")
    dycat = _matmul(dh1b, ex.weight("w_out"), "nt", F32, "d_mix_out")
    ex.done("d_mix_out", dycat)
    ex.grad("w_out", _matmul(ycat, dh1b, "tn", F32, "d_w_out", tm=512, dep=ex.dep()))
    dp_hg, d_lb, d_hgn = _hgrn_bwd(proj, small["lb_logits"], small["hg_norm"], o_hg, dycat, states, dep=ex.dep())
    ex.done("hgrn_bwd", d_lb)
    dproj, dk_att, dv_att, gsum = _att_bwd(proj, bias, dycat, dp_hg, dep=ex.dep())
    d_rel = _rel_bias_grad(gsum)
    d_small = {
        "norm_mix": jnp.zeros_like(small["norm_mix"]), "lb_logits": d_lb, "hg_norm": d_hgn, "rel_bias": d_rel, "norm_ffn": d_ffn,
        "conv_b": d_conv_b, "norm_ple": d_ple, "final_norm": d_final,
    }
    packed = jnp.concatenate([_pack_small(d_small), _rows128(d_conv_w, CONV_W_FULL_ROWS), _rows128(loss[0:1, 0:1], 8)], axis=0)
    early = ex.reduce(packed, "all_reduce_small")
    for k, part in enumerate((dk_att, dv_att)):
        dproj = lax.dynamic_update_slice(dproj, part, (0, 4 * HG_WIDTH + (k + 1) * ATT_WIDTH))
    ex.grad("w_in", _matmul(a1, dproj, "tn", F32, "d_w_in", tm=512, dep=[early]))
    da1 = _matmul(dproj, ex.weight("w_in"), "nt", BF16, "d_norm_mix_out", tk=1792, dep=ex.dep())
    dx, _, d_mix = _rms_bwd(da1, x, r1, small["norm_mix"], dh1, "norm_mix_bwd")
    rows = dict(SMALL)["norm_mix"]
    late = ex.reduce(_rows128(d_mix, rows), "all_reduce_norm_mix")
    ex.done("d_norm_mix_out", late)
    return dx, jnp.concatenate([late, early[rows:]], axis=0)


def kernel(x, p, norm_mix, w_in, lb_logits, hg_norm, rel_bias, w_out, norm_ffn, w_up, conv_w, conv_b, w_down, norm_ple, w_ple_gate, w_ple_proj, final_norm, loss_target, m_norm_mix, m_w_in, m_lb_logits, m_hg_norm, m_rel_bias, m_w_out, m_norm_ffn, m_w_up, m_conv_w, m_conv_b, m_w_down, m_norm_ple, m_w_ple_gate, m_w_ple_proj, m_final_norm, v_norm_mix, v_w_in, v_lb_logits, v_hg_norm, v_rel_bias, v_w_out, v_norm_ffn, v_w_up, v_conv_w, v_conv_b, v_w_down, v_norm_ple, v_w_ple_gate, v_w_ple_proj, v_final_norm):
    given = dict(locals())
    mx, my, mc = _position()
    me = 4 * mx + 2 * my + mc
    big = {k: given[k][0] for k, _ in BIG}
    ex = _Exchange(big, conv_w[0], (mx, my, mc))
    ex.gather()

    small = {
        "norm_mix": norm_mix, "lb_logits": lb_logits, "hg_norm": hg_norm, "rel_bias": rel_bias[0], "norm_ffn": norm_ffn,
        "conv_b": conv_b, "norm_ple": norm_ple, "final_norm": final_norm.reshape(1, -1),
    }
    dx, reduced = _local_step(x[0], p[0, 0], loss_target[0], small, ex)

    out = {}
    shapes = {k: given[k].shape for k, _ in SMALL}
    g_small, at = _unpack_small(reduced, shapes)
    g_conv_full = reduced[at : at + CONV_W_FULL_ROWS].reshape(3, 2 * D_FF)
    total_loss = reduced[at + CONV_W_FULL_ROWS, 0]
    cw = conv_w.shape[2]
    g_conv = lax.dynamic_slice_in_dim(g_conv_full, me * cw, cw, axis=1)

    def pack_with_conv(parts, conv_part):
        return jnp.concatenate([_pack_small(parts), _rows128(conv_part, CONV_W_SHARD_ROWS)], axis=0)

    d_pk, m_pk, v_pk = _adam_small(
        pack_with_conv({k: given[k] for k, _ in SMALL}, conv_w),
        pack_with_conv(g_small, g_conv),
        pack_with_conv({k: given["m_" + k] for k, _ in SMALL}, m_conv_w),
        pack_with_conv({k: given["v_" + k] for k, _ in SMALL}, v_conv_w),
    )
    for name, pk in (("d", d_pk), ("m", m_pk), ("v", v_pk)):
        parts, at = _unpack_small(pk, shapes)
        parts["conv_w"] = pk[at : at + CONV_W_SHARD_ROWS].reshape(-1)[: 3 * cw].reshape(conv_w.shape)
        for k, a in parts.items():
            out.setdefault(k, {})
            out[k][name] = a
    for k, _ in SMALL:
        out[k]["g"] = g_small[k]
    out["conv_w"]["g"] = g_conv.reshape(conv_w.shape)

    after, started = v_pk, ex.dep()
    for gi in range(len(GROUPS)):
        for k, (o, r) in ex.finish(gi, after).items():
            g, d, nm, nv = _adam_big(big[k], given["m_" + k][0], given["v_" + k][0], o, r, "adam_" + k, dep=started)
            out[k] = tuple(a[None] for a in (g, d, nm, nv))
            after = nv

    order = ["norm_mix", "w_in", "lb_logits", "hg_norm", "rel_bias", "w_out", "norm_ffn", "w_up", "conv_w", "conv_b", "w_down", "norm_ple", "w_ple_gate", "w_ple_proj", "final_norm"]

    def pick(k, what):
        return out[k][what] if isinstance(out[k], dict) else out[k][{"g": 0, "d": 1, "m": 2, "v": 3}[what]]

    return (total_loss, dx[None], *[pick(k, "g") for k in order], *[pick(k, "d") for k in order], *[pick(k, "m") for k in order], *[pick(k, "v") for k in order])
```
